```python
import math
import jax, jax.numpy as jnp
from jax import lax
import numpy as np

D_MODEL = 1024
BATCH = 8
SEQ = 2048
DEPTH = 4
DEC_BATCH = 32
DEC_SEQ = 32
PAST_LEN = 1024

CHUNK = 64
HEAD_DIM = 64
H_A = 8
N_IDX_HEADS = 8
D_IDX = 64
TOPK_MAX = 256
T5_BUCKETS = 32
T5_MAX_DIST = 128
H_B = 8
DK_B = 64
DV_B = 64
CONV_B = 4
H_C = 8
DK_C = 32
DV_C = 64
H_D = 8
BAND_CHUNKS = 8
REL_CLIP = 128
D_FF = 2816
CONV_FF = 3
N_EVEN = (DEPTH + 1) // 2
N_ODD = DEPTH // 2
ALPHA = (2 * DEPTH) ** 0.25
BETA_INIT = (8 * DEPTH) ** -0.25
W_A = H_A * HEAD_DIM
W_B = H_B * DV_B
W_C = H_C * DV_C
W_D = H_D * HEAD_DIM
EVEN_SIZES = (W_A, W_A, W_A, N_IDX_HEADS * D_IDX, D_IDX, N_IDX_HEADS, 2 * H_B * DK_B + W_B, H_B, H_B, W_B)
ODD_SIZES = (H_C * DK_C, H_C * DK_C, W_C, H_C, H_C, W_C, 3 * W_D)

kernel_name = 'hybrid_dsa_gdn_mlstm_band_streaming_step'


def split_cols(h, sizes):
    return jnp.split(h, [int(s) for s in np.cumsum(sizes)[:-1]], axis=-1)


def layer_norm(x, g, b, eps=1e-5):
    xf = x.astype(jnp.float32)
    mu = jnp.mean(xf, -1, keepdims=True)
    var = jnp.mean(jnp.square(xf - mu), -1, keepdims=True)
    return ((xf - mu) * lax.rsqrt(var + eps) * g.astype(jnp.float32) + b.astype(jnp.float32)).astype(x.dtype)


def head_rms_norm(x, g, eps=1e-6):
    xf = x.astype(jnp.float32)
    return xf * lax.rsqrt(jnp.mean(xf * xf, -1, keepdims=True) + eps) * g.astype(jnp.float32)


def l2_normalize(x, eps=1e-6):
    return x * lax.rsqrt(jnp.sum(x * x, -1, keepdims=True) + eps)


def causal_dwconv(x, hist, w):
    width = w.shape[0]
    t = x.shape[1]
    xp = jnp.concatenate([hist.astype(x.dtype), x], axis=1)
    y = sum(xp[:, j:j + t] * w[j] for j in range(width))
    return y, xp[:, xp.shape[1] - (width - 1):]


def chunk_scan(step, carry, xs, chunk):
    t = xs[0].shape[1]
    n = t // chunk
    def to_blocks(u):
        return jnp.swapaxes(u.reshape((u.shape[0], n, chunk) + u.shape[2:]), 0, 1)
    carry, ys = lax.scan(step, carry, tuple(to_blocks(u) for u in xs))
    ys = jnp.swapaxes(ys, 0, 1)
    return carry, ys.reshape((ys.shape[0], t) + ys.shape[3:])


def t5_bucket(rel):
    nb = T5_BUCKETS // 2
    max_exact = nb // 2
    n = jnp.abs(rel)
    n_f = jnp.maximum(n, 1).astype(jnp.float32)
    large = max_exact + (jnp.log(n_f / max_exact) / math.log(T5_MAX_DIST / max_exact) * (nb - max_exact)).astype(jnp.int32)
    large = jnp.minimum(large, nb - 1)
    return jnp.where(rel > 0, nb, 0) + jnp.where(n < max_exact, n, large)


def rel_bias(table, rel):
    return table[jnp.clip(rel, -REL_CLIP, REL_CLIP) + REL_CLIP]


def dsa_attend(qa, qi, wi, pos_q, ka, va, ki, t5_table, topk):
    bsz, length = ka.shape[0], ka.shape[1]
    key_pos = jnp.arange(length)
    admissible = (key_pos[None, :] // CHUNK) <= (pos_q[:, None] // CHUNK)
    idx_logits = jnp.einsum('bqnd,bld->bqnl', qi, ki).astype(jnp.float32) * D_IDX ** -0.5
    score = jnp.einsum('bqnl,bqn->bql', jax.nn.relu(idx_logits), wi.astype(jnp.float32) * N_IDX_HEADS ** -0.5)
    score = jnp.where(admissible[None], score, -jnp.inf)
    _, sel = lax.top_k(score, topk)
    bidx = jnp.arange(bsz)[:, None, None]
    k_sel = ka[bidx, sel]
    v_sel = va[bidx, sel]
    valid = (sel // CHUNK) <= (pos_q[None, :, None] // CHUNK)
    bias = t5_table[t5_bucket(sel - pos_q[None, :, None])].astype(jnp.float32)
    s = jnp.einsum('bqhd,bqkhd->bqhk', qa, k_sel).astype(jnp.float32) * HEAD_DIM ** -0.5 + jnp.swapaxes(bias, -1, -2)
    s = jnp.where(valid[:, :, None, :], s, -jnp.inf)
    p = jax.nn.softmax(s, axis=-1).astype(va.dtype)
    return jnp.einsum('bqhk,bqkhd->bqhd', p, v_sel)


def dsa_prompt(qa, qi, wi, ka, va, ki, t5_table):
    bsz, s = qa.shape[:2]
    nc = s // CHUNK
    topk = min(TOPK_MAX, s // 4)
    def to_blocks(u):
        return jnp.swapaxes(u.reshape((bsz, nc, CHUNK) + u.shape[2:]), 0, 1)
    def one_block(args):
        q_blk, qi_blk, wi_blk, c = args
        pos_q = c * CHUNK + jnp.arange(CHUNK)
        return dsa_attend(q_blk, qi_blk, wi_blk, pos_q, ka, va, ki, t5_table, topk)
    out = lax.map(one_block, (to_blocks(qa), to_blocks(qi), to_blocks(wi), jnp.arange(nc)))
    return jnp.swapaxes(out, 0, 1).reshape(bsz, s, H_A, HEAD_DIM)


def gdn_chunk(s_state, xs):
    q, k, v, beta, g = xs
    c = q.shape[1]
    lower = jnp.tril(jnp.ones((c, c), bool))
    strict = jnp.tril(jnp.ones((c, c), bool), -1)
    g_cum = jnp.cumsum(g, axis=1)
    g_h = jnp.moveaxis(g_cum, 1, -1)
    decay = jnp.where(lower, jnp.exp(jnp.where(lower, g_h[..., :, None] - g_h[..., None, :], 0.0)), 0.0)
    beta_h = jnp.moveaxis(beta, 1, -1)
    a_mat = jnp.where(strict, beta_h[..., :, None] * jnp.einsum('bihd,bjhd->bhij', k, k) * decay, 0.0)
    eye = jnp.eye(c, dtype=a_mat.dtype)
    t_mat = lax.linalg.triangular_solve(eye + a_mat, jnp.broadcast_to(eye, a_mat.shape), left_side=True, lower=True)
    value = jnp.einsum('bhij,bjhd->bihd', t_mat, v * beta[..., None])
    k_cum = jnp.einsum('bhij,bjhd->bihd', t_mat, k * (beta * jnp.exp(g_cum))[..., None])
    v_new = value - jnp.einsum('bihk,bhkv->bihv', k_cum, s_state)
    attn = jnp.einsum('bihd,bjhd->bhij', q, k) * decay
    o = (jnp.einsum('bihk,bhkv->bihv', q * jnp.exp(g_cum)[..., None], s_state)
         + jnp.einsum('bhij,bjhv->bihv', attn, v_new))
    g_last = g_cum[:, -1]
    s_new = (s_state * jnp.exp(g_last)[..., None, None]
             + jnp.einsum('bjhk,bjhv->bhkv', k * jnp.exp(g_last[:, None] - g_cum)[..., None], v_new))
    return s_new, o


def gated_deltanet(qkv_b, a_b, b_b, z_b, hist, s0, conv_w, a_log, dt_bias, norm_w, chunk):
    f32 = jnp.float32
    bsz, t = qkv_b.shape[:2]
    conv, new_hist = causal_dwconv(qkv_b, hist, conv_w)
    qkv = jax.nn.silu(conv.astype(f32))
    q, k, v = split_cols(qkv, (H_B * DK_B, H_B * DK_B, W_B))
    q = l2_normalize(q.reshape(bsz, t, H_B, DK_B)) * DK_B ** -0.5
    k = l2_normalize(k.reshape(bsz, t, H_B, DK_B))
    v = v.reshape(bsz, t, H_B, DV_B)
    beta = jax.nn.sigmoid(b_b.astype(f32))
    g = -jnp.exp(a_log.astype(f32)) * jax.nn.softplus(a_b.astype(f32) + dt_bias.astype(f32))
    s_new, o = chunk_scan(gdn_chunk, s0, (q, k, v, beta, g), chunk)
    y = head_rms_norm(o, norm_w) * jax.nn.silu(z_b.astype(f32)).reshape(bsz, t, H_B, DV_B)
    return y.reshape(bsz, t, W_B).astype(qkv_b.dtype), s_new, new_hist


def mlstm_chunk(carry, xs):
    c_st, n_st, m_st = carry
    q, k, v, ig, lf = xs
    length = q.shape[1]
    causal = jnp.tril(jnp.ones((length, length), bool))
    f_cum = jnp.moveaxis(jnp.cumsum(lf, axis=1), 1, -1)
    ig_h = jnp.moveaxis(ig, 1, -1)
    log_w = jnp.where(causal, f_cum[..., :, None] - f_cum[..., None, :] + ig_h[..., None, :], -jnp.inf)
    log_inter = f_cum + m_st[..., None]
    m_t = jnp.maximum(log_inter, jnp.max(log_w, axis=-1))
    w_intra = jnp.exp(log_w - m_t[..., None])
    w_inter = jnp.exp(log_inter - m_t)
    qk = jnp.einsum('bthd,bshd->bhts', q, k) * w_intra
    num = jnp.einsum('bhts,bshv->bhtv', qk, v) + w_inter[..., None] * jnp.einsum('bthk,bhkv->bhtv', q, c_st)
    den = jnp.sum(qk, -1) + w_inter * jnp.einsum('bthk,bhk->bht', q, n_st)
    h = num / jnp.maximum(jnp.abs(den), jnp.exp(-m_t))[..., None]
    m_new = m_t[..., -1]
    w_last = w_intra[..., -1, :]
    dec = w_inter[..., -1]
    c_new = dec[..., None, None] * c_st + jnp.einsum('bhs,bshk,bshv->bhkv', w_last, k, v)
    n_new = dec[..., None] * n_st + jnp.einsum('bhs,bshk->bhk', w_last, k)
    return (c_new, n_new, m_new), jnp.moveaxis(h, 1, 2)


def mlstm(qc, kc, vc, ic, fc, oc, carry, i_bias, f_bias, norm_w, chunk):
    f32 = jnp.float32
    bsz, t = qc.shape[:2]
    q = qc.astype(f32).reshape(bsz, t, H_C, DK_C)
    k = kc.astype(f32).reshape(bsz, t, H_C, DK_C) * DK_C ** -0.5
    v = vc.astype(f32).reshape(bsz, t, H_C, DV_C)
    ig = ic.astype(f32) + i_bias.astype(f32)
    lf = jax.nn.log_sigmoid(fc.astype(f32) + f_bias.astype(f32))
    carry, h = chunk_scan(mlstm_chunk, carry, (q, k, v, ig, lf), chunk)
    y = head_rms_norm(h, norm_w) * jax.nn.sigmoid(oc.astype(f32)).reshape(bsz, t, H_C, DV_C)
    return y.reshape(bsz, t, W_C).astype(qc.dtype), carry


def band_prompt(qd, kd, vd, rel_table):
    bsz, s = qd.shape[:2]
    nc = s // CHUNK
    nb = BAND_CHUNKS + 1
    qc = qd.reshape(bsz, nc, CHUNK, H_D, HEAD_DIM)
    def band(u):
        uc = jnp.pad(u.reshape(bsz, nc, CHUNK, H_D, HEAD_DIM), ((0, 0), (BAND_CHUNKS, 0), (0, 0), (0, 0), (0, 0)))
        return jnp.stack([uc[:, j:j + nc] for j in range(nb)], axis=2).reshape(bsz, nc, nb * CHUNK, H_D, HEAD_DIM)
    kb, vb = band(kd), band(vd)
    r = jnp.arange(nb * CHUNK)
    i = jnp.arange(CHUNK)
    rel = (r[None, :] - BAND_CHUNKS * CHUNK) - i[:, None]
    bias = jnp.transpose(rel_bias(rel_table, rel), (2, 0, 1)).astype(jnp.float32)
    valid = (jnp.arange(nc)[:, None] - BAND_CHUNKS + r[None, :] // CHUNK) >= 0
    s_ = jnp.einsum('bcihd,bckhd->bchik', qc, kb).astype(jnp.float32) * HEAD_DIM ** -0.5 + bias[None, None]
    s_ = jnp.where(valid[None, :, None, None, :], s_, -jnp.inf)
    p = jax.nn.softmax(s_, axis=-1).astype(vd.dtype)
    return jnp.einsum('bchik,bckhd->bcihd', p, vb).reshape(bsz, s, H_D, HEAD_DIM)


def band_sample(qd, kd, vd, cache_k, cache_v, rel_table, past):
    t = qd.shape[1]
    w = cache_k.shape[1]
    k_all = jnp.concatenate([cache_k.astype(kd.dtype), kd], axis=1)
    v_all = jnp.concatenate([cache_v.astype(vd.dtype), vd], axis=1)
    pos_q = past + jnp.arange(t)
    pos_k = jnp.concatenate([past - w + jnp.arange(w), pos_q])
    qch = pos_q // CHUNK
    kch = pos_k // CHUNK
    valid = (pos_k[None] >= 0) & (kch[None] >= qch[:, None] - BAND_CHUNKS) & (kch[None] <= qch[:, None])
    bias = jnp.transpose(rel_bias(rel_table, pos_k[None] - pos_q[:, None]), (2, 0, 1)).astype(jnp.float32)
    s_ = jnp.einsum('bihd,bkhd->bhik', qd, k_all).astype(jnp.float32) * HEAD_DIM ** -0.5 + bias[None]
    s_ = jnp.where(valid[None, None], s_, -jnp.inf)
    p = jax.nn.softmax(s_, axis=-1).astype(v_all.dtype)
    return jnp.einsum('bhik,bkhd->bihd', p, v_all)


def even_parts(x, w_in):
    bsz, t = x.shape[:2]
    qa, ka, va, qi, ki, wi, qkv_b, a_b, b_b, z_b = split_cols(x @ w_in, EVEN_SIZES)
    def heads(u):
        return u.reshape(bsz, t, H_A, HEAD_DIM)
    return heads(qa), heads(ka), heads(va), qi.reshape(bsz, t, N_IDX_HEADS, D_IDX), ki, wi, qkv_b, a_b, b_b, z_b


def odd_parts(x, w_in):
    bsz, t = x.shape[:2]
    qc, kc, vc, ic, fc, oc, qkv_d = split_cols(x @ w_in, ODD_SIZES)
    qd, kd, vd = [u.reshape(bsz, t, H_D, HEAD_DIM) for u in jnp.split(qkv_d, 3, axis=-1)]
    return qc, kc, vc, ic, fc, oc, qd, kd, vd


def conv_ffn(x, hist, w_up, conv_w, w_down):
    h, new_hist = causal_dwconv(x @ w_up, hist, conv_w)
    g, u = jnp.split(h, 2, axis=-1)
    return (jax.nn.silu(g) * u) @ w_down, new_hist


def setup_inputs(seed: int = 0) -> dict:
    key = jax.random.key(seed)
    ks = iter(jax.random.split(key, 40))
    def nrm(shape, scale=1.0):
        return jax.random.normal(next(ks), shape, jnp.float32) * scale
    d_win = min(BAND_CHUNKS * CHUNK, PAST_LEN)
    e_tot = sum(EVEN_SIZES)
    o_tot = sum(ODD_SIZES)
    conv_b_width = 2 * H_B * DK_B + W_B
    dt = jnp.exp(jax.random.uniform(next(ks), (N_EVEN, H_B), minval=math.log(1e-3), maxval=math.log(1e-1)))
    return {
        'x_prompt': nrm((BATCH, SEQ, D_MODEL)),
        'x_sample': nrm((DEC_BATCH, DEC_SEQ, D_MODEL)),
        'cache_a_k': nrm((N_EVEN, DEC_BATCH, PAST_LEN, H_A, HEAD_DIM)),
        'cache_a_v': nrm((N_EVEN, DEC_BATCH, PAST_LEN, H_A, HEAD_DIM)),
        'cache_a_kidx': nrm((N_EVEN, DEC_BATCH, PAST_LEN, D_IDX)),
        'state_b_s': nrm((N_EVEN, DEC_BATCH, H_B, DK_B, DV_B), 0.3),
        'state_b_conv': nrm((N_EVEN, DEC_BATCH, CONV_B - 1, conv_b_width)),
        'state_c_c': nrm((N_ODD, DEC_BATCH, H_C, DK_C, DV_C), 0.5),
        'state_c_n': nrm((N_ODD, DEC_BATCH, H_C, DK_C), 0.5),
        'state_c_m': nrm((N_ODD, DEC_BATCH, H_C)),
        'cache_d_k': nrm((N_ODD, DEC_BATCH, d_win, H_D, HEAD_DIM)),
        'cache_d_v': nrm((N_ODD, DEC_BATCH, d_win, H_D, HEAD_DIM)),
        'state_ffn_conv': nrm((DEPTH, DEC_BATCH, CONV_FF - 1, 2 * D_FF)),
        'w_in_even': nrm((N_EVEN, D_MODEL, e_tot), D_MODEL ** -0.5),
        'w_out_even': nrm((N_EVEN, W_A + W_B, D_MODEL), (W_A + W_B) ** -0.5 * BETA_INIT),
        't5_bias': nrm((T5_BUCKETS, H_A), 0.2),
        'b_conv_w': nrm((N_EVEN, CONV_B, conv_b_width), CONV_B ** -0.5),
        'b_a_log': jnp.log(jax.random.uniform(next(ks), (N_EVEN, H_B), minval=1.0, maxval=16.0)),
        'b_dt_bias': dt + jnp.log(-jnp.expm1(-dt)),
        'b_norm_w': 1.0 + nrm((N_EVEN, DV_B), 0.02),
        'w_in_odd': nrm((N_ODD, D_MODEL, o_tot), D_MODEL ** -0.5),
        'w_out_odd': nrm((N_ODD, W_C + W_D, D_MODEL), (W_C + W_D) ** -0.5 * BETA_INIT),
        'c_i_bias': nrm((N_ODD, H_C), 0.1),
        'c_f_bias': jax.random.uniform(next(ks), (N_ODD, H_C), minval=3.0, maxval=6.0),
        'c_norm_w': 1.0 + nrm((N_ODD, DV_C), 0.02),
        'd_rel_bias': nrm((N_ODD, 2 * REL_CLIP + 1, H_D), 0.2),
        'ln_mix_g': 1.0 + nrm((DEPTH, D_MODEL), 0.02),
        'ln_mix_b': nrm((DEPTH, D_MODEL), 0.02),
        'ln_ffn_g': 1.0 + nrm((DEPTH, D_MODEL), 0.02),
        'ln_ffn_b': nrm((DEPTH, D_MODEL), 0.02),
        'ffn_w_up': nrm((DEPTH, D_MODEL, 2 * D_FF), D_MODEL ** -0.5),
        'ffn_conv_w': nrm((DEPTH, CONV_FF, 2 * D_FF), CONV_FF ** -0.5),
        'ffn_w_down': nrm((DEPTH, D_FF, D_MODEL), D_FF ** -0.5 * BETA_INIT),
    }


def reference(x_prompt, x_sample, cache_a_k, cache_a_v, cache_a_kidx, state_b_s, state_b_conv,
              state_c_c, state_c_n, state_c_m, cache_d_k, cache_d_v, state_ffn_conv,
              w_in_even, w_out_even, t5_bias, b_conv_w, b_a_log, b_dt_bias, b_norm_w,
              w_in_odd, w_out_odd, c_i_bias, c_f_bias, c_norm_w, d_rel_bias,
              ln_mix_g, ln_mix_b, ln_ffn_g, ln_ffn_b, ffn_w_up, ffn_conv_w, ffn_w_down):
    f32 = jnp.float32
    bp, sp = x_prompt.shape[:2]
    bs, ts = x_sample.shape[:2]
    past = cache_a_k.shape[2]
    d_win_p = min(BAND_CHUNKS * CHUNK, sp)
    xp, xs = x_prompt, x_sample
    ak_p, ak_s, av_p, av_s, aki_p, aki_s = [], [], [], [], [], []
    bs_p, bs_s, bc_p, bc_s = [], [], [], []
    cc_p, cc_s, cn_p, cn_s, cm_p, cm_s = [], [], [], [], [], []
    dk_p, dk_s, dv_p, dv_s = [], [], [], []
    fc_p, fc_s = [], []
    for layer in range(DEPTH):
        if layer % 2 == 0:
            e = layer // 2
            qa, ka, va, qi, ki, wi, qkv_b, a_b, b_b, z_b = even_parts(xp, w_in_even[e])
            o_a = dsa_prompt(qa, qi, wi, ka, va, ki, t5_bias)
            y_b, s_b, h_b = gated_deltanet(
                qkv_b, a_b, b_b, z_b, jnp.zeros((bp, CONV_B - 1, qkv_b.shape[-1]), xp.dtype),
                jnp.zeros((bp, H_B, DK_B, DV_B), f32), b_conv_w[e], b_a_log[e], b_dt_bias[e], b_norm_w[e], CHUNK)
            mix_p = jnp.concatenate([o_a.reshape(bp, sp, W_A), y_b], axis=-1) @ w_out_even[e]
            ak_p.append(ka); av_p.append(va); aki_p.append(ki)
            bs_p.append(s_b.astype(state_b_s.dtype)); bc_p.append(h_b.astype(state_b_conv.dtype))
            qa, ka, va, qi, ki, wi, qkv_b, a_b, b_b, z_b = even_parts(xs, w_in_even[e])
            o_a = dsa_attend(qa, qi, wi, past + jnp.arange(ts),
                             jnp.concatenate([cache_a_k[e].astype(ka.dtype), ka], axis=1),
                             jnp.concatenate([cache_a_v[e].astype(va.dtype), va], axis=1),
                             jnp.concatenate([cache_a_kidx[e].astype(ki.dtype), ki], axis=1),
                             t5_bias, min(TOPK_MAX, (past + ts) // 4))
            y_b, s_b, h_b = gated_deltanet(
                qkv_b, a_b, b_b, z_b, state_b_conv[e], state_b_s[e].astype(f32),
                b_conv_w[e], b_a_log[e], b_dt_bias[e], b_norm_w[e], ts)
            mix_s = jnp.concatenate([o_a.reshape(bs, ts, W_A), y_b], axis=-1) @ w_out_even[e]
            ak_s.append(ka); av_s.append(va); aki_s.append(ki)
            bs_s.append(s_b.astype(state_b_s.dtype)); bc_s.append(h_b.astype(state_b_conv.dtype))
        else:
            o = layer // 2
            qc, kc, vc, ic, fc, oc, qd, kd, vd = odd_parts(xp, w_in_odd[o])
            carry0 = (jnp.zeros((bp, H_C, DK_C, DV_C), f32), jnp.zeros((bp, H_C, DK_C), f32), jnp.zeros((bp, H_C), f32))
            y_c, (c_c, c_n, c_m) = mlstm(qc, kc, vc, ic, fc, oc, carry0, c_i_bias[o], c_f_bias[o], c_norm_w[o], CHUNK)
            o_d = band_prompt(qd, kd, vd, d_rel_bias[o])
            mix_p = jnp.concatenate([y_c, o_d.reshape(bp, sp, W_D)], axis=-1) @ w_out_odd[o]
            cc_p.append(c_c.astype(state_c_c.dtype)); cn_p.append(c_n.astype(state_c_n.dtype)); cm_p.append(c_m.astype(state_c_m.dtype))
            dk_p.append(kd[:, sp - d_win_p:]); dv_p.append(vd[:, sp - d_win_p:])
            qc, kc, vc, ic, fc, oc, qd, kd, vd = odd_parts(xs, w_in_odd[o])
            carry0 = (state_c_c[o].astype(f32), state_c_n[o].astype(f32), state_c_m[o].astype(f32))
            y_c, (c_c, c_n, c_m) = mlstm(qc, kc, vc, ic, fc, oc, carry0, c_i_bias[o], c_f_bias[o], c_norm_w[o], ts)
            o_d = band_sample(qd, kd, vd, cache_d_k[o], cache_d_v[o], d_rel_bias[o], past)
            mix_s = jnp.concatenate([y_c, o_d.reshape(bs, ts, W_D)], axis=-1) @ w_out_odd[o]
            cc_s.append(c_c.astype(state_c_c.dtype)); cn_s.append(c_n.astype(state_c_n.dtype)); cm_s.append(c_m.astype(state_c_m.dtype))
            dk_s.append(kd); dv_s.append(vd)
        xp = layer_norm(ALPHA * xp + mix_p, ln_mix_g[layer], ln_mix_b[layer])
        xs = layer_norm(ALPHA * xs + mix_s, ln_mix_g[layer], ln_mix_b[layer])
        f_p, hist_p = conv_ffn(xp, jnp.zeros((bp, CONV_FF - 1, 2 * D_FF), xp.dtype), ffn_w_up[layer], ffn_conv_w[layer], ffn_w_down[layer])
        f_s, hist_s = conv_ffn(xs, state_ffn_conv[layer], ffn_w_up[layer], ffn_conv_w[layer], ffn_w_down[layer])
        fc_p.append(hist_p.astype(state_ffn_conv.dtype)); fc_s.append(hist_s.astype(state_ffn_conv.dtype))
        xp = layer_norm(ALPHA * xp + f_p, ln_ffn_g[layer], ln_ffn_b[layer])
        xs = layer_norm(ALPHA * xs + f_s, ln_ffn_g[layer], ln_ffn_b[layer])
    return (xp, xs,
            jnp.stack(ak_p), jnp.stack(ak_s), jnp.stack(av_p), jnp.stack(av_s), jnp.stack(aki_p), jnp.stack(aki_s),
            jnp.stack(bs_p), jnp.stack(bs_s), jnp.stack(bc_p), jnp.stack(bc_s),
            jnp.stack(cc_p), jnp.stack(cc_s), jnp.stack(cn_p), jnp.stack(cn_s), jnp.stack(cm_p), jnp.stack(cm_s),
            jnp.stack(dk_p), jnp.stack(dk_s), jnp.stack(dv_p), jnp.stack(dv_s),
            jnp.stack(fc_p), jnp.stack(fc_s))
```

```python
import functools
import math

import numpy as np
import jax
import jax.numpy as jnp
from jax import lax
from jax.experimental import pallas as pl
from jax.experimental.pallas import tpu as pltpu

F32 = jnp.float32
BF16 = jnp.bfloat16
HI = lax.Precision.HIGHEST

CHUNK = 64
HEAD_DIM = 64
N_HEADS = 8
N_IDX_HEADS = 8
D_IDX = 64
TOPK_MAX = 256
T5_BUCKETS = 32
T5_MAX_DIST = 128
DK_B = 64
DV_B = 64
CONV_B = 4
DK_C = 32
DV_C = 64
BAND_CHUNKS = 8
REL_CLIP = 128
CONV_FF = 3
NEAR_CHUNKS = 3
T5_FAR = 128

LANES = 128
SUBLANES = 8
VMEM_LIMIT = 56 * 1024 * 1024

NEG = -1e30
INT_MIN = -2 ** 31


def _params(*sem):
    return pltpu.CompilerParams(dimension_semantics=sem, vmem_limit_bytes=VMEM_LIMIT)


def _dot(a, b):
    return jnp.dot(a.astype(BF16), b.astype(BF16), preferred_element_type=F32)


def _dot_nt(a, b, precision=None):
    return lax.dot_general(a, b, (((1,), (1,)), ((), ())), precision=precision, preferred_element_type=F32)


def _dot_tn(a, b, precision=None):
    return lax.dot_general(a, b, (((0,), (0,)), ((), ())), precision=precision, preferred_element_type=F32)


def _dot_hi(a, b):
    return jnp.dot(a, b, precision=HI, preferred_element_type=F32)


def _sigmoid(x):
    return 1.0 / (1.0 + jnp.exp(-x))


def _softplus(x):
    return jnp.maximum(x, 0.0) + jnp.log(1.0 + jnp.exp(-jnp.abs(x)))


def _iota(shape, dim):
    return lax.broadcasted_iota(jnp.int32, shape, dim)


def _proj_kernel(x_ref, w_ref, *out_refs, sizes):
    xb = x_ref[...].astype(BF16)
    off = 0
    for o_ref, size in zip(out_refs, sizes):
        o_ref[...] = jnp.dot(xb, w_ref[:, off:off + size], preferred_element_type=F32)
        off += size


def _proj(x2d, w, sizes, tm):
    n, d = x2d.shape
    return pl.pallas_call(
        functools.partial(_proj_kernel, sizes=sizes),
        grid=(n // tm,),
        in_specs=[pl.BlockSpec((tm, d), lambda i: (i, 0)),
                  pl.BlockSpec((d, sum(sizes)), lambda i: (0, 0))],
        out_specs=[pl.BlockSpec((tm, s), lambda i: (i, 0)) for s in sizes],
        out_shape=[jax.ShapeDtypeStruct((n, s), F32) for s in sizes],
        compiler_params=_params("parallel"),
        name="in_proj",
    )(x2d, w)


def _mm_res_ln_kernel(*refs, nparts, alpha):
    part_refs = refs[:nparts]
    w_refs = refs[nparts:2 * nparts]
    x_ref, g_ref, b_ref, o_ref = refs[2 * nparts:]
    acc = alpha * x_ref[...]
    for p_ref, w_ref in zip(part_refs, w_refs):
        acc = acc + jnp.dot(p_ref[...].astype(BF16), w_ref[...], preferred_element_type=F32)
    mu = jnp.mean(acc, axis=-1, keepdims=True)
    cen = acc - mu
    var = jnp.mean(cen * cen, axis=-1, keepdims=True)
    o_ref[...] = cen * lax.rsqrt(var + 1e-5) * g_ref[...] + b_ref[...]


def _mm_res_ln(parts, ws, x2d, g, b, alpha, tm):
    n, d = x2d.shape
    nparts = len(parts)
    in_specs = ([pl.BlockSpec((tm, p.shape[1]), lambda i: (i, 0)) for p in parts]
                + [pl.BlockSpec(w.shape, lambda i: (0, 0)) for w in ws]
                + [pl.BlockSpec((tm, d), lambda i: (i, 0)),
                   pl.BlockSpec((1, d), lambda i: (0, 0)),
                   pl.BlockSpec((1, d), lambda i: (0, 0))])
    return pl.pallas_call(
        functools.partial(_mm_res_ln_kernel, nparts=nparts, alpha=alpha),
        grid=(n // tm,),
        in_specs=in_specs,
        out_specs=pl.BlockSpec((tm, d), lambda i: (i, 0)),
        out_shape=jax.ShapeDtypeStruct((n, d), F32),
        compiler_params=_params("parallel"),
        name="out_proj_ln",
    )(*parts, *ws, x2d, g.reshape(1, d), b.reshape(1, d))


def _ffn_up_kernel(x_ref, w_ref, cw_ref, hist_ref, act_ref, newhist_ref, ext_ref, *,
                   ns, tt, tiles_per_seq, dff, cc):
    i = pl.program_id(0)
    tm = ns * tt
    hw = CONV_FF - 1
    base = SUBLANES
    if tiles_per_seq == 1:
        ext_ref[:, base - hw:base, :] = hist_ref[...]
    else:
        @pl.when(i % tiles_per_seq == 0)
        def _():
            ext_ref[:, base - hw:base, :] = hist_ref[...]

        @pl.when(i % tiles_per_seq != 0)
        def _():
            ext_ref[:, base - hw:base, :] = ext_ref[:, base + tt - hw:base + tt, :]
    xb = x_ref[...].astype(BF16)
    for j in range(2 * dff // cc):
        cols = slice(j * cc, (j + 1) * cc)
        h = jnp.dot(xb, w_ref[:, cols], preferred_element_type=F32)
        ext_ref[:, base:base + tt, cols] = h.reshape(ns, tt, cc)
    newhist_ref[...] = ext_ref[:, base + tt - hw:base + tt, :]

    def conv(cols):
        acc = None
        for k in range(CONV_FF):
            term = ext_ref[:, base - hw + k:base - hw + k + tt, cols] * cw_ref[k:k + 1, cols]
            acc = term if acc is None else acc + term
        return acc

    for j in range(dff // cc):
        g = conv(slice(j * cc, (j + 1) * cc))
        u = conv(slice(dff + j * cc, dff + (j + 1) * cc))
        act = g * _sigmoid(g) * u
        act_ref[:, j * cc:(j + 1) * cc] = act.reshape(tm, cc).astype(BF16)


def _ffn_up(x2d, w_up, conv_w, hist, ns, tt, tiles_per_seq):
    n, d = x2d.shape
    c2 = w_up.shape[1]
    dff = c2 // 2
    tm = ns * tt
    cc = 256
    hw = CONV_FF - 1
    if tiles_per_seq == 1:
        hist_map = lambda i: (i, 0, 0)
    else:
        hist_map = lambda i: (i // tiles_per_seq, 0, 0)
    return pl.pallas_call(
        functools.partial(_ffn_up_kernel, ns=ns, tt=tt, tiles_per_seq=tiles_per_seq, dff=dff, cc=cc),
        grid=(n // tm,),
        in_specs=[pl.BlockSpec((tm, d), lambda i: (i, 0)),
                  pl.BlockSpec((d, c2), lambda i: (0, 0)),
                  pl.BlockSpec((CONV_FF, c2), lambda i: (0, 0)),
                  pl.BlockSpec((ns, hw, c2), hist_map)],
        out_specs=[pl.BlockSpec((tm, dff), lambda i: (i, 0)),
                   pl.BlockSpec((ns, hw, c2), hist_map)],
        out_shape=[jax.ShapeDtypeStruct((n, dff), BF16),
                   jax.ShapeDtypeStruct(hist.shape, F32)],
        scratch_shapes=[pltpu.VMEM((ns, SUBLANES + tt, c2), F32)],
        compiler_params=_params("arbitrary"),
        name="ffn_up_conv_gate",
    )(x2d, w_up, conv_w, hist)


def _sortable(x):
    b = lax.bitcast_convert_type(x, jnp.int32)
    return b ^ ((b >> 31) & jnp.int32(0x7FFFFFFF))


def _count(mask):
    return jnp.sum(jnp.where(mask, 1.0, 0.0), axis=-1, keepdims=True)


def _dsa_kernel(*refs, topk, prompt, tq, lf, ln, wi_col):
    if prompt:
        (qa_ref, qi_ref, qm_ref, kf_ref, vf_ref, kif_ref, nbias_ref, fbias_ref,
         o_ref, self_ref, seln_ref) = refs
        c = pl.program_id(1)
        v = jnp.minimum(c, NEAR_CHUNKS - 1)
        start = pl.multiple_of((c - v) * CHUNK, CHUNK)
        load_kn = lambda hs: kf_ref[0, pl.ds(start, ln), hs]
        load_vn = lambda hs: vf_ref[0, pl.ds(start, ln), hs]
        kin = kif_ref[0, pl.ds(start, ln), :][:, :D_IDX]
    else:
        (qa_ref, qi_ref, qm_ref, kf_ref, vf_ref, kif_ref, kn_ref, vn_ref, kin_ref, nbias_ref, fbias_ref,
         o_ref, self_ref, seln_ref) = refs
        v = 0
        start = lf - T5_FAR
        load_kn = lambda hs: kn_ref[0, :, hs]
        load_vn = lambda hs: vn_ref[0, :, hs]
        kin = kin_ref[0][:, :D_IDX]
    qa = qa_ref[...]
    qi = qi_ref[...]
    wi = qm_ref[:, wi_col:wi_col + N_IDX_HEADS] * (N_IDX_HEADS ** -0.5) * (D_IDX ** -0.5)
    kif = kif_ref[0][:, :D_IDX].astype(BF16)
    kinb = kin.astype(BF16)

    sc_f = jnp.zeros((tq, lf), F32)
    sc_n = jnp.zeros((tq, ln), F32)
    for n in range(N_IDX_HEADS):
        qn = qi[:, n * D_IDX:(n + 1) * D_IDX].astype(BF16)
        wn = wi[:, n:n + 1]
        sc_f = sc_f + jnp.maximum(_dot_nt(qn, kif), 0.0) * wn
        sc_n = sc_n + jnp.maximum(_dot_nt(qn, kinb), 0.0) * wn
    adm_f = _iota((tq, lf), 1) < start
    adm_n = nbias_ref[v, 0] > 0.5 * NEG
    key_f = jnp.where(adm_f, _sortable(sc_f), jnp.int32(INT_MIN))
    key_n = jnp.where(adm_n, _sortable(sc_n), jnp.int32(INT_MIN))

    kf32 = float(topk)

    def body(i, t_u):
        cand_u = t_u | lax.shift_left(jnp.int32(1), 31 - i)
        cand_s = cand_u ^ jnp.int32(INT_MIN)
        cnt = _count(key_f >= cand_s) + _count(key_n >= cand_s)
        return jnp.where(cnt >= kf32, cand_u, t_u)

    t_u = lax.fori_loop(0, 32, body, jnp.zeros((tq, 1), jnp.int32))
    thr = t_u ^ jnp.int32(INT_MIN)

    n_gt = _count(key_f > thr) + _count(key_n > thr)
    n_eq = _count(key_f == thr) + _count(key_n == thr)
    need = kf32 - n_gt
    open_row = thr == jnp.int32(INT_MIN)
    conflict = jnp.logical_and(n_eq != need, jnp.logical_not(open_row))
    self_ref[...] = jnp.where(jnp.logical_and(key_f >= thr, adm_f), 0.0, NEG)
    seln_ref[...] = jnp.where(jnp.logical_and(key_n >= thr, adm_n), 0.0, NEG)

    @pl.when(jnp.max(jnp.where(conflict, 1.0, 0.0)) > 0.0)
    def _():
        upper = jnp.where(_iota((LANES, LANES), 0) < _iota((LANES, LANES), 1), 1.0, 0.0).astype(BF16)
        offset = jnp.zeros((tq, 1), F32)
        for ref, key, width in ((self_ref, key_f, lf), (seln_ref, key_n, ln)):
            for j0 in range(0, width, LANES):
                w = min(LANES, width - j0)
                kb = key[:, j0:j0 + w]
                e = jnp.where(kb == thr, 1.0, 0.0)
                rank = offset + jnp.dot(e.astype(BF16), upper[:w, :w], preferred_element_type=F32)
                take = jnp.where(kb > thr, 1.0, jnp.where(rank < need, e, 0.0))
                take = jnp.where(open_row, jnp.where(kb > thr, 1.0, 0.0), take)
                ref[:, j0:j0 + w] = jnp.where(take > 0.5, 0.0, NEG)
                offset = offset + jnp.sum(e, axis=-1, keepdims=True)

    sel_f = self_ref[...]
    sel_n = seln_ref[...]
    for h in range(N_HEADS):
        hs = slice(h * HEAD_DIM, (h + 1) * HEAD_DIM)
        qh = qa[:, hs].astype(BF16)
        s_f = _dot_nt(qh, kf_ref[0, :, hs].astype(BF16)) * (HEAD_DIM ** -0.5) + fbias_ref[:, h:h + 1] + sel_f
        s_n = _dot_nt(qh, load_kn(hs).astype(BF16)) * (HEAD_DIM ** -0.5) + nbias_ref[v, h] + sel_n
        m = jnp.maximum(jnp.max(s_f, axis=-1, keepdims=True), jnp.max(s_n, axis=-1, keepdims=True))
        p_f = jnp.exp(s_f - m)
        p_n = jnp.exp(s_n - m)
        den = jnp.sum(p_f, axis=-1, keepdims=True) + jnp.sum(p_n, axis=-1, keepdims=True)
        o = _dot(p_f, vf_ref[0, :, hs]) + _dot(p_n, load_vn(hs))
        o_ref[:, hs] = o / den


def _dsa(qa, qi, qmisc, kf, vf, kif, near, nbias, fbias, *, nb, tq, topk, wi_col):
    n = qa.shape[0]
    nq = n // (nb * tq)
    lf = kf.shape[1]
    prompt = near is None
    ln = nbias.shape[-1]
    row = lambda b, c: (b * nq + c, 0)
    per_b = lambda b, c: (b, 0, 0)
    in_specs = [pl.BlockSpec((tq, qa.shape[1]), row),
                pl.BlockSpec((tq, qi.shape[1]), row),
                pl.BlockSpec((tq, qmisc.shape[1]), row),
                pl.BlockSpec((1,) + kf.shape[1:], per_b),
                pl.BlockSpec((1,) + vf.shape[1:], per_b),
                pl.BlockSpec((1,) + kif.shape[1:], per_b)]
    args = [qa, qi, qmisc, kf, vf, kif]
    if not prompt:
        for a in near:
            in_specs.append(pl.BlockSpec((1,) + a.shape[1:], per_b))
            args.append(a)
    in_specs += [pl.BlockSpec(nbias.shape, lambda b, c: (0, 0, 0, 0)),
                 pl.BlockSpec(fbias.shape, lambda b, c: (0, 0))]
    args += [nbias, fbias]
    return pl.pallas_call(
        functools.partial(_dsa_kernel, topk=topk, prompt=prompt, tq=tq, lf=lf, ln=ln, wi_col=wi_col),
        grid=(nb, nq),
        in_specs=in_specs,
        out_specs=pl.BlockSpec((tq, qa.shape[1]), row),
        out_shape=jax.ShapeDtypeStruct(qa.shape, F32),
        scratch_shapes=[pltpu.VMEM((tq, lf), F32), pltpu.VMEM((tq, ln), F32)],
        compiler_params=_params("parallel", "arbitrary"),
        name="dsa_attention",
    )(*args)


def _unit_lower_inverse(a, n):
    eye = jnp.where(_iota((n, n), 0) == _iota((n, n), 1), 1.0, 0.0)
    p = eye - a
    ak = a
    k = 1
    while 2 * k < n:
        ak = _dot_hi(ak, ak)
        p = p + _dot_hi(p, ak)
        k *= 2
    return p


def _head_rms(x, w):
    return x * lax.rsqrt(jnp.mean(x * x, axis=-1, keepdims=True) + 1e-6) * w


def _gdn_kernel(qkv_ref, misc_ref, z_ref, hist_ref, s0_ref, cw_ref, alog_ref, dtb_ref, nw_ref,
                y_ref, sfin_ref, newhist_ref, ext_ref, s_ref, *, c, a_col, b_col):
    j = pl.program_id(1)
    hw = CONV_B - 1
    base = SUBLANES
    wq = N_HEADS * DK_B

    @pl.when(j == 0)
    def _():
        ext_ref[base - hw:base, :] = hist_ref[0]
        s_ref[...] = s0_ref[0]

    @pl.when(j > 0)
    def _():
        ext_ref[base - hw:base, :] = ext_ref[base + c - hw:base + c, :]

    ext_ref[base:base + c, :] = qkv_ref[...]
    newhist_ref[0] = ext_ref[base + c - hw:base + c, :]
    conv = None
    for k in range(CONV_B):
        term = ext_ref[base - hw + k:base - hw + k + c, :] * cw_ref[k:k + 1, :]
        conv = term if conv is None else conv + term
    act = conv * _sigmoid(conv)

    beta = _sigmoid(misc_ref[:, b_col:b_col + N_HEADS])
    g = -jnp.exp(alog_ref[...]) * _softplus(misc_ref[:, a_col:a_col + N_HEADS] + dtb_ref[...])
    ri = _iota((c, c), 0)
    ci = _iota((c, c), 1)
    lower = ri >= ci
    strict = ri > ci
    gc = _dot_hi(jnp.where(lower, 1.0, 0.0), g)
    eye_h = jnp.where(_iota((N_HEADS, N_HEADS), 0) == _iota((N_HEADS, N_HEADS), 1), 1.0, 0.0)
    gc_t = _dot_nt(eye_h, gc, HI)
    eg = jnp.exp(gc)
    g_last = gc[c - 1:c, :]
    e_last = jnp.exp(g_last)
    e_rest = jnp.exp(g_last - gc)
    z = z_ref[...]
    for h in range(N_HEADS):
        qh = act[:, h * DK_B:(h + 1) * DK_B]
        kh = act[:, wq + h * DK_B:wq + (h + 1) * DK_B]
        vh = act[:, 2 * wq + h * DV_B:2 * wq + (h + 1) * DV_B]
        qh = qh * lax.rsqrt(jnp.sum(qh * qh, axis=-1, keepdims=True) + 1e-6) * (DK_B ** -0.5)
        kh = kh * lax.rsqrt(jnp.sum(kh * kh, axis=-1, keepdims=True) + 1e-6)
        col = gc[:, h:h + 1]
        decay = jnp.where(lower, jnp.exp(jnp.where(lower, col - gc_t[h:h + 1, :], 0.0)), 0.0)
        bh = beta[:, h:h + 1]
        a_mat = jnp.where(strict, bh * _dot_nt(kh, kh, HI) * decay, 0.0)
        t_mat = _unit_lower_inverse(a_mat, c)
        value = _dot_hi(t_mat, vh * bh)
        k_cum = _dot_hi(t_mat, kh * (bh * eg[:, h:h + 1]))
        s_h = s_ref[h]
        v_new = value - _dot_hi(k_cum, s_h)
        attn = _dot_nt(qh, kh, HI) * decay
        o = _dot_hi(qh * eg[:, h:h + 1], s_h) + _dot_hi(attn, v_new)
        s_ref[h] = s_h * e_last[:, h:h + 1] + _dot_tn(kh * e_rest[:, h:h + 1], v_new, HI)
        zh = z[:, h * DV_B:(h + 1) * DV_B]
        y_ref[:, h * DV_B:(h + 1) * DV_B] = _head_rms(o, nw_ref[...]) * (zh * _sigmoid(zh))
    sfin_ref[0] = s_ref[...]


def _gdn(qkv, misc, z, hist, s0, conv_w, a_log, dt_bias, norm_w, *, nb, c, a_col, b_col):
    n, wqkv = qkv.shape
    nch = n // (nb * c)
    row = lambda b, j: (b * nch + j, 0)
    per_b3 = lambda b, j: (b, 0, 0)
    per_b4 = lambda b, j: (b, 0, 0, 0)
    const2 = lambda b, j: (0, 0)
    hw = CONV_B - 1
    return pl.pallas_call(
        functools.partial(_gdn_kernel, c=c, a_col=a_col, b_col=b_col),
        grid=(nb, nch),
        in_specs=[pl.BlockSpec((c, wqkv), row),
                  pl.BlockSpec((c, misc.shape[1]), row),
                  pl.BlockSpec((c, z.shape[1]), row),
                  pl.BlockSpec((1, hw, wqkv), per_b3),
                  pl.BlockSpec((1,) + s0.shape[1:], per_b4),
                  pl.BlockSpec((CONV_B, wqkv), const2),
                  pl.BlockSpec((1, N_HEADS), const2),
                  pl.BlockSpec((1, N_HEADS), const2),
                  pl.BlockSpec((1, DV_B), const2)],
        out_specs=[pl.BlockSpec((c, z.shape[1]), row),
                   pl.BlockSpec((1,) + s0.shape[1:], per_b4),
                   pl.BlockSpec((1, hw, wqkv), per_b3)],
        out_shape=[jax.ShapeDtypeStruct(z.shape, F32),
                   jax.ShapeDtypeStruct(s0.shape, F32),
                   jax.ShapeDtypeStruct(hist.shape, F32)],
        scratch_shapes=[pltpu.VMEM((SUBLANES + c, wqkv), F32),
                        pltpu.VMEM(s0.shape[1:], F32)],
        compiler_params=_params("parallel", "arbitrary"),
        name="gated_deltanet",
    )(qkv, misc, z, hist, s0, conv_w, a_log.reshape(1, -1), dt_bias.reshape(1, -1), norm_w.reshape(1, -1))


def _mlstm_kernel(qk_ref, v_ref, og_ref, misc_ref, c0_ref, n0_ref, m0_ref, ib_ref, fb_ref, nw_ref,
                  y_ref, cfin_ref, nfin_ref, mfin_ref, c_ref, n_ref, m_ref, *, l, i_col, f_col):
    j = pl.program_id(1)
    wq = N_HEADS * DK_C

    @pl.when(j == 0)
    def _():
        c_ref[...] = c0_ref[0]
        n_ref[...] = n0_ref[0]
        m_ref[...] = m0_ref[0]

    ig = misc_ref[:, i_col:i_col + N_HEADS] + ib_ref[...]
    lf = -_softplus(-(misc_ref[:, f_col:f_col + N_HEADS] + fb_ref[...]))
    ri = _iota((l, l), 0)
    ci = _iota((l, l), 1)
    causal = ri >= ci
    fc = _dot_hi(jnp.where(causal, 1.0, 0.0), lf)
    eye_h = jnp.where(_iota((N_HEADS, N_HEADS), 0) == _iota((N_HEADS, N_HEADS), 1), 1.0, 0.0)
    row_terms = _dot_nt(eye_h, ig - fc, HI)
    m_prev = m_ref[...]
    log_inter = fc + m_prev
    f_last = fc[l - 1:l, :]
    qk = qk_ref[...]
    v = v_ref[...]
    og = og_ref[...]
    m_new_all = []
    for h in range(N_HEADS):
        qh = qk[:, h * DK_C:(h + 1) * DK_C]
        kh = qk[:, wq + h * DK_C:wq + (h + 1) * DK_C] * (DK_C ** -0.5)
        vh = v[:, h * DV_C:(h + 1) * DV_C]
        log_w = jnp.where(causal, fc[:, h:h + 1] + row_terms[h:h + 1, :], -jnp.inf)
        li = log_inter[:, h:h + 1]
        m_t = jnp.maximum(li, jnp.max(log_w, axis=-1, keepdims=True))
        w_intra = jnp.exp(log_w - m_t)
        w_inter = jnp.exp(li - m_t)
        qkw = _dot_nt(qh, kh, HI) * w_intra
        c_h = c_ref[h]
        n_h = n_ref[h:h + 1, :]
        num = _dot_hi(qkw, vh) + w_inter * _dot_hi(qh, c_h)
        den = jnp.sum(qkw, axis=-1, keepdims=True) + w_inter * jnp.sum(qh * n_h, axis=-1, keepdims=True)
        hid = num / jnp.maximum(jnp.abs(den), jnp.exp(-m_t))
        m_new = m_t[l - 1:l, :]
        dec = w_inter[l - 1:l, :]
        w_last = jnp.exp(f_last[:, h:h + 1] - fc[:, h:h + 1] + ig[:, h:h + 1] - m_new)
        kw = kh * w_last
        c_ref[h] = dec * c_h + _dot_tn(kw, vh, HI)
        n_ref[h:h + 1, :] = dec * n_h + jnp.sum(kw, axis=0, keepdims=True)
        m_new_all.append(m_new)
        oh = og[:, h * DV_C:(h + 1) * DV_C]
        y_ref[:, h * DV_C:(h + 1) * DV_C] = _head_rms(hid, nw_ref[...]) * _sigmoid(oh)
    lane = _iota((1, N_HEADS), 1)
    m_vec = jnp.zeros((1, N_HEADS), F32)
    for h in range(N_HEADS):
        m_vec = jnp.where(lane == h, m_new_all[h], m_vec)
    m_ref[...] = m_vec
    cfin_ref[0] = c_ref[...]
    nfin_ref[0] = n_ref[...]
    mfin_ref[0] = m_ref[...]


def _mlstm(qk, v, og, misc, c0, n0, m0, i_bias, f_bias, norm_w, *, nb, l, i_col, f_col):
    n = qk.shape[0]
    nch = n // (nb * l)
    row = lambda b, j: (b * nch + j, 0)
    per_b3 = lambda b, j: (b, 0, 0)
    per_b4 = lambda b, j: (b, 0, 0, 0)
    const2 = lambda b, j: (0, 0)
    m0 = m0.reshape(nb, 1, N_HEADS)
    outs = pl.pallas_call(
        functools.partial(_mlstm_kernel, l=l, i_col=i_col, f_col=f_col),
        grid=(nb, nch),
        in_specs=[pl.BlockSpec((l, qk.shape[1]), row),
                  pl.BlockSpec((l, v.shape[1]), row),
                  pl.BlockSpec((l, og.shape[1]), row),
                  pl.BlockSpec((l, misc.shape[1]), row),
                  pl.BlockSpec((1,) + c0.shape[1:], per_b4),
                  pl.BlockSpec((1,) + n0.shape[1:], per_b3),
                  pl.BlockSpec((1, 1, N_HEADS), per_b3),
                  pl.BlockSpec((1, N_HEADS), const2),
                  pl.BlockSpec((1, N_HEADS), const2),
                  pl.BlockSpec((1, DV_C), const2)],
        out_specs=[pl.BlockSpec((l, v.shape[1]), row),
                   pl.BlockSpec((1,) + c0.shape[1:], per_b4),
                   pl.BlockSpec((1,) + n0.shape[1:], per_b3),
                   pl.BlockSpec((1, 1, N_HEADS), per_b3)],
        out_shape=[jax.ShapeDtypeStruct(v.shape, F32),
                   jax.ShapeDtypeStruct(c0.shape, F32),
                   jax.ShapeDtypeStruct(n0.shape, F32),
                   jax.ShapeDtypeStruct(m0.shape, F32)],
        scratch_shapes=[pltpu.VMEM(c0.shape[1:], F32),
                        pltpu.VMEM(n0.shape[1:], F32),
                        pltpu.VMEM((1, N_HEADS), F32)],
        compiler_params=_params("parallel", "arbitrary"),
        name="mlstm",
    )(qk, v, og, misc, c0, n0, m0, i_bias.reshape(1, -1), f_bias.reshape(1, -1), norm_w.reshape(1, -1))
    y, c_fin, n_fin, m_fin = outs
    return y, c_fin, n_fin, m_fin.reshape(nb, N_HEADS)


def _band_kernel(*refs, prompt, tq, lw):
    if prompt:
        q_ref, k_ref, v_ref, bias_ref, o_ref = refs
        c = pl.program_id(1)
        start = pl.multiple_of(c * CHUNK, CHUNK)
        pieces = [(lambda hs: k_ref[0, pl.ds(start, lw), hs], lambda hs: v_ref[0, pl.ds(start, lw), hs], bias_ref)]
        masks = [jnp.where(_iota((tq, lw), 1) >= (BAND_CHUNKS - c) * CHUNK, 0.0, NEG)]
    else:
        q_ref, kc_ref, vc_ref, kn_ref, vn_ref, biasc_ref, biasn_ref, o_ref = refs
        pieces = [(lambda hs: kc_ref[0, :, hs], lambda hs: vc_ref[0, :, hs], biasc_ref),
                  (lambda hs: kn_ref[0, :, hs], lambda hs: vn_ref[0, :, hs], biasn_ref)]
        masks = [None, None]
    q = q_ref[...]
    for h in range(N_HEADS):
        hs = slice(h * HEAD_DIM, (h + 1) * HEAD_DIM)
        qh = q[:, hs].astype(BF16)
        scores = []
        for (load_k, _, b_ref), mask in zip(pieces, masks):
            s = _dot_nt(qh, load_k(hs).astype(BF16)) * (HEAD_DIM ** -0.5) + b_ref[h]
            scores.append(s if mask is None else s + mask)
        m = functools.reduce(jnp.maximum, [jnp.max(s, axis=-1, keepdims=True) for s in scores])
        ps = [jnp.exp(s - m) for s in scores]
        den = sum(jnp.sum(p, axis=-1, keepdims=True) for p in ps)
        o = sum(_dot(p, load_v(hs)) for p, (_, load_v, _) in zip(ps, pieces))
        o_ref[:, hs] = o / den


def _band_prompt(q, k_pad, v_pad, bias, *, nb, tq):
    n, w = q.shape
    nq = n // (nb * tq)
    lw = bias.shape[-1]
    row = lambda b, c: (b * nq + c, 0)
    per_b = lambda b, c: (b, 0, 0)
    return pl.pallas_call(
        functools.partial(_band_kernel, prompt=True, tq=tq, lw=lw),
        grid=(nb, nq),
        in_specs=[pl.BlockSpec((tq, w), row),
                  pl.BlockSpec((1,) + k_pad.shape[1:], per_b),
                  pl.BlockSpec((1,) + v_pad.shape[1:], per_b),
                  pl.BlockSpec(bias.shape, lambda b, c: (0, 0, 0))],
        out_specs=pl.BlockSpec((tq, w), row),
        out_shape=jax.ShapeDtypeStruct(q.shape, F32),
        compiler_params=_params("parallel", "arbitrary"),
        name="band_attention_prompt",
    )(q, k_pad, v_pad, bias)


def _band_sample(q, kc, vc, kn, vn, bias_c, bias_n, *, nb, tq):
    n, w = q.shape
    row = lambda b, c: (b, 0)
    per_b = lambda b, c: (b, 0, 0)
    const3 = lambda b, c: (0, 0, 0)
    return pl.pallas_call(
        functools.partial(_band_kernel, prompt=False, tq=tq, lw=None),
        grid=(nb, 1),
        in_specs=[pl.BlockSpec((tq, w), row),
                  pl.BlockSpec((1,) + kc.shape[1:], per_b),
                  pl.BlockSpec((1,) + vc.shape[1:], per_b),
                  pl.BlockSpec((1,) + kn.shape[1:], per_b),
                  pl.BlockSpec((1,) + vn.shape[1:], per_b),
                  pl.BlockSpec(bias_c.shape, const3),
                  pl.BlockSpec(bias_n.shape, const3)],
        out_specs=pl.BlockSpec((tq, w), row),
        out_shape=jax.ShapeDtypeStruct(q.shape, F32),
        compiler_params=_params("parallel", "arbitrary"),
        name="band_attention_sample",
    )(q, kc, vc, kn, vn, bias_c, bias_n)


def _t5_bucket(rel):
    nb = T5_BUCKETS // 2
    max_exact = nb // 2
    n = jnp.abs(rel)
    n_f = jnp.maximum(n, 1).astype(jnp.float32)
    large = max_exact + (jnp.log(n_f / max_exact) / math.log(T5_MAX_DIST / max_exact) * (nb - max_exact)).astype(jnp.int32)
    large = jnp.minimum(large, nb - 1)
    return jnp.where(rel > 0, nb, 0) + jnp.where(n < max_exact, n, large)


def _t5_near_bias(table, rel, valid):
    bias = table[_t5_bucket(jnp.asarray(rel, jnp.int32))].astype(F32)
    bias = jnp.moveaxis(bias, -1, -3)
    return jnp.where(jnp.asarray(valid)[..., None, :, :], bias, NEG)


def _rel_bias(table, rel, valid):
    bias = table[np.clip(rel, -REL_CLIP, REL_CLIP) + REL_CLIP].astype(F32)
    bias = jnp.transpose(bias, (2, 0, 1))
    return jnp.where(jnp.asarray(valid)[None], bias, NEG)


def _pack_cols(w, sizes, groups):
    offs = np.concatenate([[0], np.cumsum(sizes)])
    cols, widths = [], []
    for grp in groups:
        width = 0
        for idx in grp:
            cols.append(w[:, offs[idx]:offs[idx + 1]])
            width += sizes[idx]
        pad = (-width) % LANES
        if pad:
            cols.append(jnp.zeros((w.shape[0], pad), w.dtype))
        widths.append(width + pad)
    return jnp.concatenate(cols, axis=1).astype(BF16), tuple(widths)


def _row_tile(n, target):
    t = min(n, target)
    while n % t:
        t //= 2
    return t


def kernel(x_prompt, x_sample, cache_a_k, cache_a_v, cache_a_kidx, state_b_s, state_b_conv, state_c_c, state_c_n, state_c_m, cache_d_k, cache_d_v, state_ffn_conv, w_in_even, w_out_even, t5_bias, b_conv_w, b_a_log, b_dt_bias, b_norm_w, w_in_odd, w_out_odd, c_i_bias, c_f_bias, c_norm_w, d_rel_bias, ln_mix_g, ln_mix_b, ln_ffn_g, ln_ffn_b, ffn_w_up, ffn_conv_w, ffn_w_down):
    bp, sp, d = x_prompt.shape
    bs, ts, _ = x_sample.shape
    depth = ffn_w_up.shape[0]
    past = cache_a_k.shape[2]
    d_win = cache_d_k.shape[2]
    dff = ffn_w_down.shape[1]
    alpha = (2 * depth) ** 0.25
    w_a = N_HEADS * HEAD_DIM
    w_b = N_HEADS * DV_B
    w_c = N_HEADS * DV_C
    qkv_b_w = 2 * N_HEADS * DK_B + w_b
    even_sizes = (w_a, w_a, w_a, N_IDX_HEADS * D_IDX, D_IDX, N_IDX_HEADS, qkv_b_w, N_HEADS, N_HEADS, w_b)
    odd_sizes = (N_HEADS * DK_C, N_HEADS * DK_C, w_c, N_HEADS, N_HEADS, w_c, w_a, w_a, w_a)
    even_groups = ((0,), (1,), (2,), (3,), (6,), (9,), (4, 5, 7, 8))
    wi_col, a_col, b_col = D_IDX, D_IDX + N_IDX_HEADS, D_IDX + N_IDX_HEADS + N_HEADS
    odd_groups = ((0, 1), (2,), (5,), (6,), (7,), (8,), (3, 4))
    i_col, f_col = 0, N_HEADS

    assert sp % CHUNK == 0 and ts <= CHUNK and past % CHUNK == 0 and past >= T5_FAR
    assert (past + ts - 1) // CHUNK == past // CHUNK
    topk_p = min(TOPK_MAX, sp // 4)
    topk_s = min(TOPK_MAX, (past + ts) // 4)
    n_p, n_s = bp * sp, bs * ts
    tm_p = _row_tile(n_p, 512)
    tm_s = _row_tile(n_s, 512)
    tff_p = _row_tile(sp, 512)
    ns_s = _row_tile(bs, max(1, 256 // ts))

    ln_p = NEAR_CHUNKS * CHUNK
    iq = np.arange(CHUNK)[:, None]
    rn = np.arange(ln_p)[None, :]
    rel_p = np.stack([rn - CHUNK * v - iq for v in range(NEAR_CHUNKS)])
    val_p = np.stack([np.broadcast_to(rn // CHUNK <= v, (CHUNK, ln_p)) for v in range(NEAR_CHUNKS)])
    nbias_p = _t5_near_bias(t5_bias, rel_p, val_p)
    ln_s = T5_FAR + ts
    rel_s = np.arange(ln_s)[None, :] - T5_FAR - np.arange(ts)[:, None]
    nbias_s = _t5_near_bias(t5_bias, rel_s[None], np.ones((1, ts, ln_s), bool))
    fbias = t5_bias[_t5_bucket(jnp.asarray(-T5_FAR - 1, jnp.int32))].astype(F32).reshape(1, N_HEADS)

    lw = (BAND_CHUNKS + 1) * CHUNK
    rel_bp = (np.arange(lw)[None, :] - BAND_CHUNKS * CHUNK) - iq
    pos_q = past + np.arange(ts)
    pos_kc = past - d_win + np.arange(d_win)
    def band_valid(pos_k):
        kch, qch = pos_k // CHUNK, pos_q // CHUNK
        return (pos_k[None] >= 0) & (kch[None] >= qch[:, None] - BAND_CHUNKS) & (kch[None] <= qch[:, None])
    rel_bc = pos_kc[None] - pos_q[:, None]
    rel_bn = pos_q[None] - pos_q[:, None]

    xp = x_prompt.reshape(n_p, d)
    xs = x_sample.reshape(n_s, d)
    outs = {k: [] for k in ("ak_p", "ak_s", "av_p", "av_s", "aki_p", "aki_s", "bs_p", "bs_s", "bc_p", "bc_s",
                            "cc_p", "cc_s", "cn_p", "cn_s", "cm_p", "cm_s", "dk_p", "dk_s", "dv_p", "dv_s",
                            "fc_p", "fc_s")}
    for layer in range(depth):
        if layer % 2 == 0:
            e = layer // 2
            w_in, widths = _pack_cols(w_in_even[e], even_sizes, even_groups)
            w_out = w_out_even[e].astype(BF16)
            qa, ka, va, qi, qkv_b, z_b, misc = _proj(xp, w_in, widths, tm_p)
            o_a = _dsa(qa, qi, misc, ka.reshape(bp, sp, w_a), va.reshape(bp, sp, w_a), misc.reshape(bp, sp, LANES),
                       None, nbias_p, fbias, nb=bp, tq=CHUNK, topk=topk_p, wi_col=wi_col)
            y_b, s_b, h_b = _gdn(qkv_b, misc, z_b, jnp.zeros((bp, CONV_B - 1, qkv_b_w), F32),
                                 jnp.zeros((bp, N_HEADS, DK_B, DV_B), F32), b_conv_w[e], b_a_log[e], b_dt_bias[e],
                                 b_norm_w[e], nb=bp, c=CHUNK, a_col=a_col, b_col=b_col)
            xp = _mm_res_ln([o_a, y_b], [w_out[:w_a], w_out[w_a:]], xp, ln_mix_g[layer], ln_mix_b[layer], alpha, tm_p)
            outs["ak_p"].append(ka.reshape(bp, sp, N_HEADS, HEAD_DIM))
            outs["av_p"].append(va.reshape(bp, sp, N_HEADS, HEAD_DIM))
            outs["aki_p"].append(misc[:, :D_IDX].reshape(bp, sp, D_IDX))
            outs["bs_p"].append(s_b)
            outs["bc_p"].append(h_b)
            qa, ka, va, qi, qkv_b, z_b, misc = _proj(xs, w_in, widths, tm_s)
            ki = misc[:, :D_IDX]
            ck = cache_a_k[e].reshape(bs, past, w_a)
            cv = cache_a_v[e].reshape(bs, past, w_a)
            cki = cache_a_kidx[e]
            near = (jnp.concatenate([ck[:, past - T5_FAR:], ka.reshape(bs, ts, w_a)], axis=1),
                    jnp.concatenate([cv[:, past - T5_FAR:], va.reshape(bs, ts, w_a)], axis=1),
                    jnp.concatenate([cki[:, past - T5_FAR:], ki.reshape(bs, ts, D_IDX)], axis=1))
            o_a = _dsa(qa, qi, misc, ck, cv, cki, near, nbias_s, fbias, nb=bs, tq=ts, topk=topk_s, wi_col=wi_col)
            y_b, s_b, h_b = _gdn(qkv_b, misc, z_b, state_b_conv[e], state_b_s[e], b_conv_w[e], b_a_log[e],
                                 b_dt_bias[e], b_norm_w[e], nb=bs, c=ts, a_col=a_col, b_col=b_col)
            xs = _mm_res_ln([o_a, y_b], [w_out[:w_a], w_out[w_a:]], xs, ln_mix_g[layer], ln_mix_b[layer], alpha, tm_s)
            outs["ak_s"].append(ka.reshape(bs, ts, N_HEADS, HEAD_DIM))
            outs["av_s"].append(va.reshape(bs, ts, N_HEADS, HEAD_DIM))
            outs["aki_s"].append(ki.reshape(bs, ts, D_IDX))
            outs["bs_s"].append(s_b)
            outs["bc_s"].append(h_b)
        else:
            o = layer // 2
            w_in, widths = _pack_cols(w_in_odd[o], odd_sizes, odd_groups)
            w_out = w_out_odd[o].astype(BF16)
            qk_c, v_c, o_c, q_d, k_d, v_d, misc = _proj(xp, w_in, widths, tm_p)
            y_c, c_c, c_n, c_m = _mlstm(qk_c, v_c, o_c, misc, jnp.zeros((bp, N_HEADS, DK_C, DV_C), F32),
                                        jnp.zeros((bp, N_HEADS, DK_C), F32), jnp.zeros((bp, N_HEADS), F32),
                                        c_i_bias[o], c_f_bias[o], c_norm_w[o], nb=bp, l=CHUNK, i_col=i_col, f_col=f_col)
            k3 = k_d.reshape(bp, sp, w_a)
            v3 = v_d.reshape(bp, sp, w_a)
            pad = ((0, 0), (BAND_CHUNKS * CHUNK, 0), (0, 0))
            bias_bp = _rel_bias(d_rel_bias[o], rel_bp, np.ones(rel_bp.shape, bool))
            o_d = _band_prompt(q_d, jnp.pad(k3, pad), jnp.pad(v3, pad), bias_bp, nb=bp, tq=CHUNK)
            xp = _mm_res_ln([y_c, o_d], [w_out[:w_c], w_out[w_c:]], xp, ln_mix_g[layer], ln_mix_b[layer], alpha, tm_p)
            d_win_p = min(BAND_CHUNKS * CHUNK, sp)
            outs["cc_p"].append(c_c)
            outs["cn_p"].append(c_n)
            outs["cm_p"].append(c_m)
            outs["dk_p"].append(k3[:, sp - d_win_p:].reshape(bp, d_win_p, N_HEADS, HEAD_DIM))
            outs["dv_p"].append(v3[:, sp - d_win_p:].reshape(bp, d_win_p, N_HEADS, HEAD_DIM))
            qk_c, v_c, o_c, q_d, k_d, v_d, misc = _proj(xs, w_in, widths, tm_s)
            y_c, c_c, c_n, c_m = _mlstm(qk_c, v_c, o_c, misc, state_c_c[o], state_c_n[o], state_c_m[o],
                                        c_i_bias[o], c_f_bias[o], c_norm_w[o], nb=bs, l=ts, i_col=i_col, f_col=f_col)
            o_d = _band_sample(q_d, cache_d_k[o].reshape(bs, d_win, w_a), cache_d_v[o].reshape(bs, d_win, w_a),
                               k_d.reshape(bs, ts, w_a), v_d.reshape(bs, ts, w_a),
                               _rel_bias(d_rel_bias[o], rel_bc, band_valid(pos_kc)),
                               _rel_bias(d_rel_bias[o], rel_bn, band_valid(pos_q)), nb=bs, tq=ts)
            xs = _mm_res_ln([y_c, o_d], [w_out[:w_c], w_out[w_c:]], xs, ln_mix_g[layer], ln_mix_b[layer], alpha, tm_s)
            outs["cc_s"].append(c_c)
            outs["cn_s"].append(c_n)
            outs["cm_s"].append(c_m)
            outs["dk_s"].append(k_d.reshape(bs, ts, N_HEADS, HEAD_DIM))
            outs["dv_s"].append(v_d.reshape(bs, ts, N_HEADS, HEAD_DIM))
        w_up = ffn_w_up[layer].astype(BF16)
        w_down = ffn_w_down[layer].astype(BF16)
        act, hist_p = _ffn_up(xp, w_up, ffn_conv_w[layer], jnp.zeros((bp, CONV_FF - 1, 2 * dff), F32),
                              1, tff_p, sp // tff_p)
        xp = _mm_res_ln([act], [w_down], xp, ln_ffn_g[layer], ln_ffn_b[layer], alpha, tm_p)
        act, hist_s = _ffn_up(xs, w_up, ffn_conv_w[layer], state_ffn_conv[layer], ns_s, ts, 1)
        xs = _mm_res_ln([act], [w_down], xs, ln_ffn_g[layer], ln_ffn_b[layer], alpha, tm_s)
        outs["fc_p"].append(hist_p)
        outs["fc_s"].append(hist_s)

    st = lambda k: jnp.stack(outs[k])
    return (xp.reshape(bp, sp, d), xs.reshape(bs, ts, d),
            st("ak_p"), st("ak_s"), st("av_p"), st("av_s"), st("aki_p"), st("aki_s"),
            st("bs_p"), st("bs_s"), st("bc_p"), st("bc_s"),
            st("cc_p"), st("cc_s"), st("cn_p"), st("cn_s"), st("cm_p"), st("cm_s"),
            st("dk_p"), st("dk_s"), st("dv_p"), st("dv_s"),
            st("fc_p"), st("fc_s"))
```

```python
import functools
import math

import numpy as np
import jax
import jax.numpy as jnp
from jax import lax
from jax.experimental import pallas as pl
from jax.experimental.pallas import tpu as pltpu

F32 = jnp.float32
BF16 = jnp.bfloat16
HI = lax.Precision.HIGHEST

CHUNK = 64
HEAD_DIM = 64
N_HEADS = 8
N_IDX_HEADS = 8
D_IDX = 64
TOPK_MAX = 256
T5_BUCKETS = 32
T5_MAX_DIST = 128
DK_B = 64
DV_B = 64
CONV_B = 4
DK_C = 32
DV_C = 64
BAND_CHUNKS = 8
REL_CLIP = 128
CONV_FF = 3
NEAR_CHUNKS = 3
T5_FAR = 128

LANES = 128
SUBLANES = 8
VMEM_LIMIT = 56 * 1024 * 1024

NEG = -1e30
INT_MIN = -2 ** 31


def _params(*sem):
    return pltpu.CompilerParams(dimension_semantics=sem, vmem_limit_bytes=VMEM_LIMIT)


def _dot(a, b):
    return jnp.dot(a.astype(BF16), b.astype(BF16), preferred_element_type=F32)


def _dot_nt(a, b, precision=None):
    return lax.dot_general(a, b, (((1,), (1,)), ((), ())), precision=precision, preferred_element_type=F32)


def _dot_tn(a, b, precision=None):
    return lax.dot_general(a, b, (((0,), (0,)), ((), ())), precision=precision, preferred_element_type=F32)


def _dot_hi(a, b):
    return jnp.dot(a, b, precision=HI, preferred_element_type=F32)


def _split(a):
    hi = a.astype(BF16)
    return hi, (a - hi.astype(F32)).astype(BF16)


def _dot3(a, b):
    a_hi, a_lo = a
    b_hi, b_lo = b
    d = functools.partial(jnp.dot, preferred_element_type=F32)
    return d(a_hi, b_hi) + (d(a_hi, b_lo) + d(a_lo, b_hi))


def _sigmoid(x):
    return 1.0 / (1.0 + jnp.exp(-x))


def _softplus(x):
    return jnp.maximum(x, 0.0) + jnp.log(1.0 + jnp.exp(-jnp.abs(x)))


def _iota(shape, dim):
    return lax.broadcasted_iota(jnp.int32, shape, dim)


def _proj_kernel(x_ref, w_ref, *out_refs, sizes):
    xb = x_ref[...].astype(BF16)
    off = 0
    for o_ref, size in zip(out_refs, sizes):
        o_ref[...] = jnp.dot(xb, w_ref[:, off:off + size], preferred_element_type=F32)
        off += size


def _proj(x2d, w, sizes, tm):
    n, d = x2d.shape
    return pl.pallas_call(
        functools.partial(_proj_kernel, sizes=sizes),
        grid=(n // tm,),
        in_specs=[pl.BlockSpec((tm, d), lambda i: (i, 0)),
                  pl.BlockSpec((d, sum(sizes)), lambda i: (0, 0))],
        out_specs=[pl.BlockSpec((tm, s), lambda i: (i, 0)) for s in sizes],
        out_shape=[jax.ShapeDtypeStruct((n, s), F32) for s in sizes],
        compiler_params=_params("parallel"),
        name="in_proj",
    )(x2d, w)


def _mm_res_ln_kernel(*refs, nparts, alpha):
    part_refs = refs[:nparts]
    w_refs = refs[nparts:2 * nparts]
    x_ref, g_ref, b_ref, o_ref = refs[2 * nparts:]
    acc = alpha * x_ref[...]
    for p_ref, w_ref in zip(part_refs, w_refs):
        acc = acc + jnp.dot(p_ref[...].astype(BF16), w_ref[...], preferred_element_type=F32)
    mu = jnp.mean(acc, axis=-1, keepdims=True)
    cen = acc - mu
    var = jnp.mean(cen * cen, axis=-1, keepdims=True)
    o_ref[...] = cen * lax.rsqrt(var + 1e-5) * g_ref[...] + b_ref[...]


def _mm_res_ln(parts, ws, x2d, g, b, alpha, tm):
    n, d = x2d.shape
    nparts = len(parts)
    in_specs = ([pl.BlockSpec((tm, p.shape[1]), lambda i: (i, 0)) for p in parts]
                + [pl.BlockSpec(w.shape, lambda i: (0, 0)) for w in ws]
                + [pl.BlockSpec((tm, d), lambda i: (i, 0)),
                   pl.BlockSpec((1, d), lambda i: (0, 0)),
                   pl.BlockSpec((1, d), lambda i: (0, 0))])
    return pl.pallas_call(
        functools.partial(_mm_res_ln_kernel, nparts=nparts, alpha=alpha),
        grid=(n // tm,),
        in_specs=in_specs,
        out_specs=pl.BlockSpec((tm, d), lambda i: (i, 0)),
        out_shape=jax.ShapeDtypeStruct((n, d), F32),
        compiler_params=_params("parallel"),
        name="out_proj_ln",
    )(*parts, *ws, x2d, g.reshape(1, d), b.reshape(1, d))


def _ffn_up_kernel(x_ref, w_ref, cw_ref, hist_ref, act_ref, newhist_ref, ext_ref, *,
                   ns, tt, tiles_per_seq, dff, cc):
    i = pl.program_id(0)
    tm = ns * tt
    hw = CONV_FF - 1
    base = SUBLANES
    if tiles_per_seq == 1:
        ext_ref[:, base - hw:base, :] = hist_ref[...]
    else:
        @pl.when(i % tiles_per_seq == 0)
        def _():
            ext_ref[:, base - hw:base, :] = hist_ref[...]

        @pl.when(i % tiles_per_seq != 0)
        def _():
            ext_ref[:, base - hw:base, :] = ext_ref[:, base + tt - hw:base + tt, :]
    xb = x_ref[...].astype(BF16)
    for j in range(2 * dff // cc):
        cols = slice(j * cc, (j + 1) * cc)
        h = jnp.dot(xb, w_ref[:, cols], preferred_element_type=F32)
        ext_ref[:, base:base + tt, cols] = h.reshape(ns, tt, cc)
    newhist_ref[...] = ext_ref[:, base + tt - hw:base + tt, :]

    def conv(cols):
        acc = None
        for k in range(CONV_FF):
            term = ext_ref[:, base - hw + k:base - hw + k + tt, cols] * cw_ref[k:k + 1, cols]
            acc = term if acc is None else acc + term
        return acc

    for j in range(dff // cc):
        g = conv(slice(j * cc, (j + 1) * cc))
        u = conv(slice(dff + j * cc, dff + (j + 1) * cc))
        act = g * _sigmoid(g) * u
        act_ref[:, j * cc:(j + 1) * cc] = act.reshape(tm, cc).astype(BF16)


def _ffn_up(x2d, w_up, conv_w, hist, ns, tt, tiles_per_seq):
    n, d = x2d.shape
    c2 = w_up.shape[1]
    dff = c2 // 2
    tm = ns * tt
    cc = 256
    hw = CONV_FF - 1
    if tiles_per_seq == 1:
        hist_map = lambda i: (i, 0, 0)
    else:
        hist_map = lambda i: (i // tiles_per_seq, 0, 0)
    return pl.pallas_call(
        functools.partial(_ffn_up_kernel, ns=ns, tt=tt, tiles_per_seq=tiles_per_seq, dff=dff, cc=cc),
        grid=(n // tm,),
        in_specs=[pl.BlockSpec((tm, d), lambda i: (i, 0)),
                  pl.BlockSpec((d, c2), lambda i: (0, 0)),
                  pl.BlockSpec((CONV_FF, c2), lambda i: (0, 0)),
                  pl.BlockSpec((ns, hw, c2), hist_map)],
        out_specs=[pl.BlockSpec((tm, dff), lambda i: (i, 0)),
                   pl.BlockSpec((ns, hw, c2), hist_map)],
        out_shape=[jax.ShapeDtypeStruct((n, dff), BF16),
                   jax.ShapeDtypeStruct(hist.shape, F32)],
        scratch_shapes=[pltpu.VMEM((ns, SUBLANES + tt, c2), F32)],
        compiler_params=_params("arbitrary"),
        name="ffn_up_conv_gate",
    )(x2d, w_up, conv_w, hist)


def _sortable(x):
    b = lax.bitcast_convert_type(x, jnp.int32)
    return b ^ ((b >> 31) & jnp.int32(0x7FFFFFFF))


def _count(mask):
    return jnp.sum(jnp.where(mask, 1.0, 0.0), axis=-1, keepdims=True)


def _dsa_kernel(*refs, topk, prompt, tq, lf, ln, wi_col):
    if prompt:
        (qa_ref, qi_ref, qm_ref, kf_ref, vf_ref, kif_ref, nbias_ref, fbias_ref,
         o_ref, self_ref, seln_ref) = refs
        c = pl.program_id(1)
        v = jnp.minimum(c, NEAR_CHUNKS - 1)
        start = pl.multiple_of((c - v) * CHUNK, CHUNK)
        load_kn = lambda hs: kf_ref[0, pl.ds(start, ln), hs]
        load_vn = lambda hs: vf_ref[0, pl.ds(start, ln), hs]
        kin = kif_ref[0, pl.ds(start, ln), :][:, :D_IDX]
    else:
        (qa_ref, qi_ref, qm_ref, kf_ref, vf_ref, kif_ref, kn_ref, vn_ref, kin_ref, nbias_ref, fbias_ref,
         o_ref, self_ref, seln_ref) = refs
        v = 0
        start = lf - T5_FAR
        load_kn = lambda hs: kn_ref[0, :, hs]
        load_vn = lambda hs: vn_ref[0, :, hs]
        kin = kin_ref[0][:, :D_IDX]
    qa = qa_ref[...]
    qi = qi_ref[...]
    wi = qm_ref[:, wi_col:wi_col + N_IDX_HEADS] * (N_IDX_HEADS ** -0.5) * (D_IDX ** -0.5)
    kif = kif_ref[0][:, :D_IDX].astype(BF16)
    kinb = kin.astype(BF16)

    sc_f = jnp.zeros((tq, lf), F32)
    sc_n = jnp.zeros((tq, ln), F32)
    for n in range(N_IDX_HEADS):
        qn = qi[:, n * D_IDX:(n + 1) * D_IDX].astype(BF16)
        wn = wi[:, n:n + 1]
        sc_f = sc_f + jnp.maximum(_dot_nt(qn, kif), 0.0) * wn
        sc_n = sc_n + jnp.maximum(_dot_nt(qn, kinb), 0.0) * wn
    adm_f = _iota((tq, lf), 1) < start
    adm_n = nbias_ref[v, 0] > 0.5 * NEG
    key_f = jnp.where(adm_f, _sortable(sc_f), jnp.int32(INT_MIN))
    key_n = jnp.where(adm_n, _sortable(sc_n), jnp.int32(INT_MIN))

    kf32 = float(topk)

    def body(i, t_u):
        cand_u = t_u | lax.shift_left(jnp.int32(1), 31 - i)
        cand_s = cand_u ^ jnp.int32(INT_MIN)
        cnt = _count(key_f >= cand_s) + _count(key_n >= cand_s)
        return jnp.where(cnt >= kf32, cand_u, t_u)

    t_u = lax.fori_loop(0, 32, body, jnp.zeros((tq, 1), jnp.int32))
    thr = t_u ^ jnp.int32(INT_MIN)

    n_gt = _count(key_f > thr) + _count(key_n > thr)
    n_eq = _count(key_f == thr) + _count(key_n == thr)
    need = kf32 - n_gt
    open_row = thr == jnp.int32(INT_MIN)
    conflict = jnp.logical_and(n_eq != need, jnp.logical_not(open_row))
    self_ref[...] = jnp.where(jnp.logical_and(key_f >= thr, adm_f), 0.0, NEG)
    seln_ref[...] = jnp.where(jnp.logical_and(key_n >= thr, adm_n), 0.0, NEG)

    @pl.when(jnp.max(jnp.where(conflict, 1.0, 0.0)) > 0.0)
    def _():
        upper = jnp.where(_iota((LANES, LANES), 0) < _iota((LANES, LANES), 1), 1.0, 0.0).astype(BF16)
        offset = jnp.zeros((tq, 1), F32)
        for ref, key, width in ((self_ref, key_f, lf), (seln_ref, key_n, ln)):
            for j0 in range(0, width, LANES):
                w = min(LANES, width - j0)
                kb = key[:, j0:j0 + w]
                e = jnp.where(kb == thr, 1.0, 0.0)
                rank = offset + jnp.dot(e.astype(BF16), upper[:w, :w], preferred_element_type=F32)
                take = jnp.where(kb > thr, 1.0, jnp.where(rank < need, e, 0.0))
                take = jnp.where(open_row, jnp.where(kb > thr, 1.0, 0.0), take)
                ref[:, j0:j0 + w] = jnp.where(take > 0.5, 0.0, NEG)
                offset = offset + jnp.sum(e, axis=-1, keepdims=True)

    sel_f = self_ref[...]
    sel_n = seln_ref[...]
    for h in range(N_HEADS):
        hs = slice(h * HEAD_DIM, (h + 1) * HEAD_DIM)
        qh = qa[:, hs].astype(BF16)
        s_f = _dot_nt(qh, kf_ref[0, :, hs].astype(BF16)) * (HEAD_DIM ** -0.5) + fbias_ref[:, h:h + 1] + sel_f
        s_n = _dot_nt(qh, load_kn(hs).astype(BF16)) * (HEAD_DIM ** -0.5) + nbias_ref[v, h] + sel_n
        m = jnp.maximum(jnp.max(s_f, axis=-1, keepdims=True), jnp.max(s_n, axis=-1, keepdims=True))
        p_f = jnp.exp(s_f - m)
        p_n = jnp.exp(s_n - m)
        den = jnp.sum(p_f, axis=-1, keepdims=True) + jnp.sum(p_n, axis=-1, keepdims=True)
        o = _dot(p_f, vf_ref[0, :, hs]) + _dot(p_n, load_vn(hs))
        o_ref[:, hs] = o / den


def _dsa(qa, qi, qmisc, kf, vf, kif, near, nbias, fbias, *, nb, tq, topk, wi_col):
    n = qa.shape[0]
    nq = n // (nb * tq)
    lf = kf.shape[1]
    prompt = near is None
    ln = nbias.shape[-1]
    row = lambda b, c: (b * nq + c, 0)
    per_b = lambda b, c: (b, 0, 0)
    in_specs = [pl.BlockSpec((tq, qa.shape[1]), row),
                pl.BlockSpec((tq, qi.shape[1]), row),
                pl.BlockSpec((tq, qmisc.shape[1]), row),
                pl.BlockSpec((1,) + kf.shape[1:], per_b),
                pl.BlockSpec((1,) + vf.shape[1:], per_b),
                pl.BlockSpec((1,) + kif.shape[1:], per_b)]
    args = [qa, qi, qmisc, kf, vf, kif]
    if not prompt:
        for a in near:
            in_specs.append(pl.BlockSpec((1,) + a.shape[1:], per_b))
            args.append(a)
    in_specs += [pl.BlockSpec(nbias.shape, lambda b, c: (0, 0, 0, 0)),
                 pl.BlockSpec(fbias.shape, lambda b, c: (0, 0))]
    args += [nbias, fbias]
    return pl.pallas_call(
        functools.partial(_dsa_kernel, topk=topk, prompt=prompt, tq=tq, lf=lf, ln=ln, wi_col=wi_col),
        grid=(nb, nq),
        in_specs=in_specs,
        out_specs=pl.BlockSpec((tq, qa.shape[1]), row),
        out_shape=jax.ShapeDtypeStruct(qa.shape, F32),
        scratch_shapes=[pltpu.VMEM((tq, lf), F32), pltpu.VMEM((tq, ln), F32)],
        compiler_params=_params("parallel", "arbitrary"),
        name="dsa_attention",
    )(*args)


def _unit_lower_inverse(mats, n):
    eye = jnp.where(_iota((n, n), 0) == _iota((n, n), 1), 1.0, 0.0)
    ps = [eye - a for a in mats]
    aks = [_split(a) for a in mats]
    k = 1
    while 2 * k < n:
        aks = [_split(_dot3(ak, ak)) for ak in aks]
        ps = [p + _dot3(_split(p), ak) for p, ak in zip(ps, aks)]
        k *= 2
    return ps


def _head_rms(x, w):
    return x * lax.rsqrt(jnp.mean(x * x, axis=-1, keepdims=True) + 1e-6) * w


def _gdn_kernel(qkv_ref, misc_ref, z_ref, hist_ref, s0_ref, cw_ref, alog_ref, dtb_ref, nw_ref,
                y_ref, sfin_ref, newhist_ref, ext_ref, s_ref, *, c, a_col, b_col):
    j = pl.program_id(1)
    hw = CONV_B - 1
    base = SUBLANES
    wq = N_HEADS * DK_B

    @pl.when(j == 0)
    def _():
        ext_ref[base - hw:base, :] = hist_ref[0]
        s_ref[...] = s0_ref[0]

    @pl.when(j > 0)
    def _():
        ext_ref[base - hw:base, :] = ext_ref[base + c - hw:base + c, :]

    ext_ref[base:base + c, :] = qkv_ref[...]
    newhist_ref[0] = ext_ref[base + c - hw:base + c, :]
    conv = None
    for k in range(CONV_B):
        term = ext_ref[base - hw + k:base - hw + k + c, :] * cw_ref[k:k + 1, :]
        conv = term if conv is None else conv + term
    act = conv * _sigmoid(conv)

    beta = _sigmoid(misc_ref[:, b_col:b_col + N_HEADS])
    g = -jnp.exp(alog_ref[...]) * _softplus(misc_ref[:, a_col:a_col + N_HEADS] + dtb_ref[...])
    ri = _iota((c, c), 0)
    ci = _iota((c, c), 1)
    lower = ri >= ci
    strict = ri > ci
    gc = _dot_hi(jnp.where(lower, 1.0, 0.0), g)
    eye_h = jnp.where(_iota((N_HEADS, N_HEADS), 0) == _iota((N_HEADS, N_HEADS), 1), 1.0, 0.0)
    gc_t = _dot_nt(eye_h, gc, HI)
    eg = jnp.exp(gc)
    g_last = gc[c - 1:c, :]
    e_last = jnp.exp(g_last)
    e_rest = jnp.exp(g_last - gc)
    z = z_ref[...]
    heads = range(N_HEADS)
    col = lambda x, h: x[:, h:h + 1]
    qs = [act[:, h * DK_B:(h + 1) * DK_B] for h in heads]
    ks = [act[:, wq + h * DK_B:wq + (h + 1) * DK_B] for h in heads]
    vs = [act[:, 2 * wq + h * DV_B:2 * wq + (h + 1) * DV_B] for h in heads]
    qs = [q * lax.rsqrt(jnp.sum(q * q, axis=-1, keepdims=True) + 1e-6) * (DK_B ** -0.5) for q in qs]
    ks = [k * lax.rsqrt(jnp.sum(k * k, axis=-1, keepdims=True) + 1e-6) for k in ks]
    kbs = [k.astype(BF16) for k in ks]
    decay = [jnp.where(lower, jnp.exp(jnp.where(lower, col(gc, h) - gc_t[h:h + 1, :], 0.0)), 0.0) for h in heads]
    kk = [_dot_nt(kbs[h], kbs[h]) for h in heads]
    attn = [_dot_nt(qs[h].astype(BF16), kbs[h]) * decay[h] for h in heads]
    a_mat = [jnp.where(strict, col(beta, h) * kk[h] * decay[h], 0.0) for h in heads]
    t_mat = [t.astype(BF16) for t in _unit_lower_inverse(a_mat, c)]
    value = [_dot(t_mat[h], vs[h] * col(beta, h)) for h in heads]
    k_cum = [_dot(t_mat[h], ks[h] * (col(beta, h) * col(eg, h))) for h in heads]
    s_old = [s_ref[h] for h in heads]
    sbs = [s.astype(BF16) for s in s_old]
    v_new = [(value[h] - _dot(k_cum[h], sbs[h])).astype(BF16) for h in heads]
    o_inter = [_dot(qs[h] * col(eg, h), sbs[h]) for h in heads]
    o = [o_inter[h] + _dot(attn[h], v_new[h]) for h in heads]
    for h in heads:
        s_ref[h] = s_old[h] * col(e_last, h) + _dot_tn((ks[h] * col(e_rest, h)).astype(BF16), v_new[h])
    for h in heads:
        zh = z[:, h * DV_B:(h + 1) * DV_B]
        y_ref[:, h * DV_B:(h + 1) * DV_B] = _head_rms(o[h], nw_ref[...]) * (zh * _sigmoid(zh))
    sfin_ref[0] = s_ref[...]


def _gdn(qkv, misc, z, hist, s0, conv_w, a_log, dt_bias, norm_w, *, nb, c, a_col, b_col):
    n, wqkv = qkv.shape
    nch = n // (nb * c)
    row = lambda b, j: (b * nch + j, 0)
    per_b3 = lambda b, j: (b, 0, 0)
    per_b4 = lambda b, j: (b, 0, 0, 0)
    const2 = lambda b, j: (0, 0)
    hw = CONV_B - 1
    return pl.pallas_call(
        functools.partial(_gdn_kernel, c=c, a_col=a_col, b_col=b_col),
        grid=(nb, nch),
        in_specs=[pl.BlockSpec((c, wqkv), row),
                  pl.BlockSpec((c, misc.shape[1]), row),
                  pl.BlockSpec((c, z.shape[1]), row),
                  pl.BlockSpec((1, hw, wqkv), per_b3),
                  pl.BlockSpec((1,) + s0.shape[1:], per_b4),
                  pl.BlockSpec((CONV_B, wqkv), const2),
                  pl.BlockSpec((1, N_HEADS), const2),
                  pl.BlockSpec((1, N_HEADS), const2),
                  pl.BlockSpec((1, DV_B), const2)],
        out_specs=[pl.BlockSpec((c, z.shape[1]), row),
                   pl.BlockSpec((1,) + s0.shape[1:], per_b4),
                   pl.BlockSpec((1, hw, wqkv), per_b3)],
        out_shape=[jax.ShapeDtypeStruct(z.shape, F32),
                   jax.ShapeDtypeStruct(s0.shape, F32),
                   jax.ShapeDtypeStruct(hist.shape, F32)],
        scratch_shapes=[pltpu.VMEM((SUBLANES + c, wqkv), F32),
                        pltpu.VMEM(s0.shape[1:], F32)],
        compiler_params=_params("parallel", "arbitrary"),
        name="gated_deltanet",
    )(qkv, misc, z, hist, s0, conv_w, a_log.reshape(1, -1), dt_bias.reshape(1, -1), norm_w.reshape(1, -1))


def _mlstm_kernel(qk_ref, v_ref, og_ref, misc_ref, c0_ref, n0_ref, m0_ref, ib_ref, fb_ref, nw_ref,
                  y_ref, cfin_ref, nfin_ref, mfin_ref, c_ref, n_ref, m_ref, *, l, i_col, f_col):
    j = pl.program_id(1)
    wq = N_HEADS * DK_C

    @pl.when(j == 0)
    def _():
        c_ref[...] = c0_ref[0]
        n_ref[...] = n0_ref[0]
        m_ref[...] = m0_ref[0]

    ig = misc_ref[:, i_col:i_col + N_HEADS] + ib_ref[...]
    lf = -_softplus(-(misc_ref[:, f_col:f_col + N_HEADS] + fb_ref[...]))
    ri = _iota((l, l), 0)
    ci = _iota((l, l), 1)
    causal = ri >= ci
    fc = _dot_hi(jnp.where(causal, 1.0, 0.0), lf)
    eye_h = jnp.where(_iota((N_HEADS, N_HEADS), 0) == _iota((N_HEADS, N_HEADS), 1), 1.0, 0.0)
    row_terms = _dot_nt(eye_h, ig - fc, HI)
    m_prev = m_ref[...]
    log_inter = fc + m_prev
    f_last = fc[l - 1:l, :]
    qk = qk_ref[...]
    v = v_ref[...]
    og = og_ref[...]
    heads = range(N_HEADS)
    col = lambda x, h: x[:, h:h + 1]
    qs = [qk[:, h * DK_C:(h + 1) * DK_C] for h in heads]
    ks = [qk[:, wq + h * DK_C:wq + (h + 1) * DK_C] * (DK_C ** -0.5) for h in heads]
    vbs = [v[:, h * DV_C:(h + 1) * DV_C].astype(BF16) for h in heads]
    c_old = [c_ref[h] for h in heads]
    n_old = [n_ref[h:h + 1, :] for h in heads]
    qk_raw = [_dot_nt(qs[h].astype(BF16), ks[h].astype(BF16)) for h in heads]
    q_c = [_dot(qs[h], c_old[h]) for h in heads]
    log_w = [jnp.where(causal, col(fc, h) + row_terms[h:h + 1, :], -jnp.inf) for h in heads]
    m_t = [jnp.maximum(col(log_inter, h), jnp.max(log_w[h], axis=-1, keepdims=True)) for h in heads]
    w_inter = [jnp.exp(col(log_inter, h) - m_t[h]) for h in heads]
    qkw = [qk_raw[h] * jnp.exp(log_w[h] - m_t[h]) for h in heads]
    num = [_dot(qkw[h], vbs[h]) + w_inter[h] * q_c[h] for h in heads]
    den = [jnp.sum(qkw[h], axis=-1, keepdims=True) + w_inter[h] * jnp.sum(qs[h] * n_old[h], axis=-1, keepdims=True)
           for h in heads]
    hid = [num[h] / jnp.maximum(jnp.abs(den[h]), jnp.exp(-m_t[h])) for h in heads]
    m_vec = f_last + jnp.maximum(m_prev, jnp.max(ig - fc, axis=0, keepdims=True))
    dec_vec = jnp.exp(f_last + m_prev - m_vec)
    w_last = jnp.exp(f_last - fc + ig - m_vec)
    kw = [ks[h] * col(w_last, h) for h in heads]
    for h in heads:
        c_ref[h] = col(dec_vec, h) * c_old[h] + _dot_tn(kw[h].astype(BF16), vbs[h])
        n_ref[h:h + 1, :] = col(dec_vec, h) * n_old[h] + jnp.sum(kw[h], axis=0, keepdims=True)
    for h in heads:
        oh = og[:, h * DV_C:(h + 1) * DV_C]
        y_ref[:, h * DV_C:(h + 1) * DV_C] = _head_rms(hid[h], nw_ref[...]) * _sigmoid(oh)
    m_ref[...] = m_vec
    cfin_ref[0] = c_ref[...]
    nfin_ref[0] = n_ref[...]
    mfin_ref[0] = m_ref[...]


def _mlstm(qk, v, og, misc, c0, n0, m0, i_bias, f_bias, norm_w, *, nb, l, i_col, f_col):
    n = qk.shape[0]
    nch = n // (nb * l)
    row = lambda b, j: (b * nch + j, 0)
    per_b3 = lambda b, j: (b, 0, 0)
    per_b4 = lambda b, j: (b, 0, 0, 0)
    const2 = lambda b, j: (0, 0)
    m0 = m0.reshape(nb, 1, N_HEADS)
    outs = pl.pallas_call(
        functools.partial(_mlstm_kernel, l=l, i_col=i_col, f_col=f_col),
        grid=(nb, nch),
        in_specs=[pl.BlockSpec((l, qk.shape[1]), row),
                  pl.BlockSpec((l, v.shape[1]), row),
                  pl.BlockSpec((l, og.shape[1]), row),
                  pl.BlockSpec((l, misc.shape[1]), row),
                  pl.BlockSpec((1,) + c0.shape[1:], per_b4),
                  pl.BlockSpec((1,) + n0.shape[1:], per_b3),
                  pl.BlockSpec((1, 1, N_HEADS), per_b3),
                  pl.BlockSpec((1, N_HEADS), const2),
                  pl.BlockSpec((1, N_HEADS), const2),
                  pl.BlockSpec((1, DV_C), const2)],
        out_specs=[pl.BlockSpec((l, v.shape[1]), row),
                   pl.BlockSpec((1,) + c0.shape[1:], per_b4),
                   pl.BlockSpec((1,) + n0.shape[1:], per_b3),
                   pl.BlockSpec((1, 1, N_HEADS), per_b3)],
        out_shape=[jax.ShapeDtypeStruct(v.shape, F32),
                   jax.ShapeDtypeStruct(c0.shape, F32),
                   jax.ShapeDtypeStruct(n0.shape, F32),
                   jax.ShapeDtypeStruct(m0.shape, F32)],
        scratch_shapes=[pltpu.VMEM(c0.shape[1:], F32),
                        pltpu.VMEM(n0.shape[1:], F32),
                        pltpu.VMEM((1, N_HEADS), F32)],
        compiler_params=_params("parallel", "arbitrary"),
        name="mlstm",
    )(qk, v, og, misc, c0, n0, m0, i_bias.reshape(1, -1), f_bias.reshape(1, -1), norm_w.reshape(1, -1))
    y, c_fin, n_fin, m_fin = outs
    return y, c_fin, n_fin, m_fin.reshape(nb, N_HEADS)


def _band_kernel(*refs, prompt, tq, lw):
    if prompt:
        q_ref, k_ref, v_ref, bias_ref, o_ref = refs
        c = pl.program_id(1)
        start = pl.multiple_of(c * CHUNK, CHUNK)
        pieces = [(lambda hs: k_ref[0, pl.ds(start, lw), hs], lambda hs: v_ref[0, pl.ds(start, lw), hs], bias_ref)]
        masks = [jnp.where(_iota((tq, lw), 1) >= (BAND_CHUNKS - c) * CHUNK, 0.0, NEG)]
    else:
        q_ref, kc_ref, vc_ref, kn_ref, vn_ref, biasc_ref, biasn_ref, o_ref = refs
        pieces = [(lambda hs: kc_ref[0, :, hs], lambda hs: vc_ref[0, :, hs], biasc_ref),
                  (lambda hs: kn_ref[0, :, hs], lambda hs: vn_ref[0, :, hs], biasn_ref)]
        masks = [None, None]
    q = q_ref[...]
    for h in range(N_HEADS):
        hs = slice(h * HEAD_DIM, (h + 1) * HEAD_DIM)
        qh = q[:, hs].astype(BF16)
        scores = []
        for (load_k, _, b_ref), mask in zip(pieces, masks):
            s = _dot_nt(qh, load_k(hs).astype(BF16)) * (HEAD_DIM ** -0.5) + b_ref[h]
            scores.append(s if mask is None else s + mask)
        m = functools.reduce(jnp.maximum, [jnp.max(s, axis=-1, keepdims=True) for s in scores])
        ps = [jnp.exp(s - m) for s in scores]
        den = sum(jnp.sum(p, axis=-1, keepdims=True) for p in ps)
        o = sum(_dot(p, load_v(hs)) for p, (_, load_v, _) in zip(ps, pieces))
        o_ref[:, hs] = o / den


def _band_prompt(q, k_pad, v_pad, bias, *, nb, tq):
    n, w = q.shape
    nq = n // (nb * tq)
    lw = bias.shape[-1]
    row = lambda b, c: (b * nq + c, 0)
    per_b = lambda b, c: (b, 0, 0)
    return pl.pallas_call(
        functools.partial(_band_kernel, prompt=True, tq=tq, lw=lw),
        grid=(nb, nq),
        in_specs=[pl.BlockSpec((tq, w), row),
                  pl.BlockSpec((1,) + k_pad.shape[1:], per_b),
                  pl.BlockSpec((1,) + v_pad.shape[1:], per_b),
                  pl.BlockSpec(bias.shape, lambda b, c: (0, 0, 0))],
        out_specs=pl.BlockSpec((tq, w), row),
        out_shape=jax.ShapeDtypeStruct(q.shape, F32),
        compiler_params=_params("parallel", "arbitrary"),
        name="band_attention_prompt",
    )(q, k_pad, v_pad, bias)


def _band_sample(q, kc, vc, kn, vn, bias_c, bias_n, *, nb, tq):
    n, w = q.shape
    row = lambda b, c: (b, 0)
    per_b = lambda b, c: (b, 0, 0)
    const3 = lambda b, c: (0, 0, 0)
    return pl.pallas_call(
        functools.partial(_band_kernel, prompt=False, tq=tq, lw=None),
        grid=(nb, 1),
        in_specs=[pl.BlockSpec((tq, w), row),
                  pl.BlockSpec((1,) + kc.shape[1:], per_b),
                  pl.BlockSpec((1,) + vc.shape[1:], per_b),
                  pl.BlockSpec((1,) + kn.shape[1:], per_b),
                  pl.BlockSpec((1,) + vn.shape[1:], per_b),
                  pl.BlockSpec(bias_c.shape, const3),
                  pl.BlockSpec(bias_n.shape, const3)],
        out_specs=pl.BlockSpec((tq, w), row),
        out_shape=jax.ShapeDtypeStruct(q.shape, F32),
        compiler_params=_params("parallel", "arbitrary"),
        name="band_attention_sample",
    )(q, kc, vc, kn, vn, bias_c, bias_n)


def _t5_bucket(rel):
    nb = T5_BUCKETS // 2
    max_exact = nb // 2
    n = jnp.abs(rel)
    n_f = jnp.maximum(n, 1).astype(jnp.float32)
    large = max_exact + (jnp.log(n_f / max_exact) / math.log(T5_MAX_DIST / max_exact) * (nb - max_exact)).astype(jnp.int32)
    large = jnp.minimum(large, nb - 1)
    return jnp.where(rel > 0, nb, 0) + jnp.where(n < max_exact, n, large)


def _t5_near_bias(table, rel, valid):
    bias = table[_t5_bucket(jnp.asarray(rel, jnp.int32))].astype(F32)
    bias = jnp.moveaxis(bias, -1, -3)
    return jnp.where(jnp.asarray(valid)[..., None, :, :], bias, NEG)


def _rel_bias(table, rel, valid):
    bias = table[np.clip(rel, -REL_CLIP, REL_CLIP) + REL_CLIP].astype(F32)
    bias = jnp.transpose(bias, (2, 0, 1))
    return jnp.where(jnp.asarray(valid)[None], bias, NEG)


def _pack_cols(w, sizes, groups):
    offs = np.concatenate([[0], np.cumsum(sizes)])
    cols, widths = [], []
    for grp in groups:
        width = 0
        for idx in grp:
            cols.append(w[:, offs[idx]:offs[idx + 1]])
            width += sizes[idx]
        pad = (-width) % LANES
        if pad:
            cols.append(jnp.zeros((w.shape[0], pad), w.dtype))
        widths.append(width + pad)
    return jnp.concatenate(cols, axis=1).astype(BF16), tuple(widths)


def _row_tile(n, target):
    t = min(n, target)
    while n % t:
        t //= 2
    return t


def kernel(x_prompt, x_sample, cache_a_k, cache_a_v, cache_a_kidx, state_b_s, state_b_conv, state_c_c, state_c_n, state_c_m, cache_d_k, cache_d_v, state_ffn_conv, w_in_even, w_out_even, t5_bias, b_conv_w, b_a_log, b_dt_bias, b_norm_w, w_in_odd, w_out_odd, c_i_bias, c_f_bias, c_norm_w, d_rel_bias, ln_mix_g, ln_mix_b, ln_ffn_g, ln_ffn_b, ffn_w_up, ffn_conv_w, ffn_w_down):
    bp, sp, d = x_prompt.shape
    bs, ts, _ = x_sample.shape
    depth = ffn_w_up.shape[0]
    past = cache_a_k.shape[2]
    d_win = cache_d_k.shape[2]
    dff = ffn_w_down.shape[1]
    alpha = (2 * depth) ** 0.25
    w_a = N_HEADS * HEAD_DIM
    w_b = N_HEADS * DV_B
    w_c = N_HEADS * DV_C
    qkv_b_w = 2 * N_HEADS * DK_B + w_b
    even_sizes = (w_a, w_a, w_a, N_IDX_HEADS * D_IDX, D_IDX, N_IDX_HEADS, qkv_b_w, N_HEADS, N_HEADS, w_b)
    odd_sizes = (N_HEADS * DK_C, N_HEADS * DK_C, w_c, N_HEADS, N_HEADS, w_c, w_a, w_a, w_a)
    even_groups = ((0,), (1,), (2,), (3,), (6,), (9,), (4, 5, 7, 8))
    wi_col, a_col, b_col = D_IDX, D_IDX + N_IDX_HEADS, D_IDX + N_IDX_HEADS + N_HEADS
    odd_groups = ((0, 1), (2,), (5,), (6,), (7,), (8,), (3, 4))
    i_col, f_col = 0, N_HEADS

    assert sp % CHUNK == 0 and ts <= CHUNK and past % CHUNK == 0 and past >= T5_FAR
    assert (past + ts - 1) // CHUNK == past // CHUNK
    topk_p = min(TOPK_MAX, sp // 4)
    topk_s = min(TOPK_MAX, (past + ts) // 4)
    n_p, n_s = bp * sp, bs * ts
    tm_p = _row_tile(n_p, 512)
    tm_s = _row_tile(n_s, 512)
    tff_p = _row_tile(sp, 512)
    ns_s = _row_tile(bs, max(1, 256 // ts))

    ln_p = NEAR_CHUNKS * CHUNK
    iq = np.arange(CHUNK)[:, None]
    rn = np.arange(ln_p)[None, :]
    rel_p = np.stack([rn - CHUNK * v - iq for v in range(NEAR_CHUNKS)])
    val_p = np.stack([np.broadcast_to(rn // CHUNK <= v, (CHUNK, ln_p)) for v in range(NEAR_CHUNKS)])
    nbias_p = _t5_near_bias(t5_bias, rel_p, val_p)
    ln_s = T5_FAR + ts
    rel_s = np.arange(ln_s)[None, :] - T5_FAR - np.arange(ts)[:, None]
    nbias_s = _t5_near_bias(t5_bias, rel_s[None], np.ones((1, ts, ln_s), bool))
    fbias = t5_bias[_t5_bucket(jnp.asarray(-T5_FAR - 1, jnp.int32))].astype(F32).reshape(1, N_HEADS)

    lw = (BAND_CHUNKS + 1) * CHUNK
    rel_bp = (np.arange(lw)[None, :] - BAND_CHUNKS * CHUNK) - iq
    pos_q = past + np.arange(ts)
    pos_kc = past - d_win + np.arange(d_win)
    def band_valid(pos_k):
        kch, qch = pos_k // CHUNK, pos_q // CHUNK
        return (pos_k[None] >= 0) & (kch[None] >= qch[:, None] - BAND_CHUNKS) & (kch[None] <= qch[:, None])
    rel_bc = pos_kc[None] - pos_q[:, None]
    rel_bn = pos_q[None] - pos_q[:, None]

    xp = x_prompt.reshape(n_p, d)
    xs = x_sample.reshape(n_s, d)
    outs = {k: [] for k in ("ak_p", "ak_s", "av_p", "av_s", "aki_p", "aki_s", "bs_p", "bs_s", "bc_p", "bc_s",
                            "cc_p", "cc_s", "cn_p", "cn_s", "cm_p", "cm_s", "dk_p", "dk_s", "dv_p", "dv_s",
                            "fc_p", "fc_s")}
    for layer in range(depth):
        if layer % 2 == 0:
            e = layer // 2
            w_in, widths = _pack_cols(w_in_even[e], even_sizes, even_groups)
            w_out = w_out_even[e].astype(BF16)
            qa, ka, va, qi, qkv_b, z_b, misc = _proj(xp, w_in, widths, tm_p)
            o_a = _dsa(qa, qi, misc, ka.reshape(bp, sp, w_a), va.reshape(bp, sp, w_a), misc.reshape(bp, sp, LANES),
                       None, nbias_p, fbias, nb=bp, tq=CHUNK, topk=topk_p, wi_col=wi_col)
            y_b, s_b, h_b = _gdn(qkv_b, misc, z_b, jnp.zeros((bp, CONV_B - 1, qkv_b_w), F32),
                                 jnp.zeros((bp, N_HEADS, DK_B, DV_B), F32), b_conv_w[e], b_a_log[e], b_dt_bias[e],
                                 b_norm_w[e], nb=bp, c=CHUNK, a_col=a_col, b_col=b_col)
            xp = _mm_res_ln([o_a, y_b], [w_out[:w_a], w_out[w_a:]], xp, ln_mix_g[layer], ln_mix_b[layer], alpha, tm_p)
            outs["ak_p"].append(ka.reshape(bp, sp, N_HEADS, HEAD_DIM))
            outs["av_p"].append(va.reshape(bp, sp, N_HEADS, HEAD_DIM))
            outs["aki_p"].append(misc[:, :D_IDX].reshape(bp, sp, D_IDX))
            outs["bs_p"].append(s_b)
            outs["bc_p"].append(h_b)
            qa, ka, va, qi, qkv_b, z_b, misc = _proj(xs, w_in, widths, tm_s)
            ki = misc[:, :D_IDX]
            ck = cache_a_k[e].reshape(bs, past, w_a)
            cv = cache_a_v[e].reshape(bs, past, w_a)
            cki = cache_a_kidx[e]
            near = (jnp.concatenate([ck[:, past - T5_FAR:], ka.reshape(bs, ts, w_a)], axis=1),
                    jnp.concatenate([cv[:, past - T5_FAR:], va.reshape(bs, ts, w_a)], axis=1),
                    jnp.concatenate([cki[:, past - T5_FAR:], ki.reshape(bs, ts, D_IDX)], axis=1))
            o_a = _dsa(qa, qi, misc, ck, cv, cki, near, nbias_s, fbias, nb=bs, tq=ts, topk=topk_s, wi_col=wi_col)
            y_b, s_b, h_b = _gdn(qkv_b, misc, z_b, state_b_conv[e], state_b_s[e], b_conv_w[e], b_a_log[e],
                                 b_dt_bias[e], b_norm_w[e], nb=bs, c=ts, a_col=a_col, b_col=b_col)
            xs = _mm_res_ln([o_a, y_b], [w_out[:w_a], w_out[w_a:]], xs, ln_mix_g[layer], ln_mix_b[layer], alpha, tm_s)
            outs["ak_s"].append(ka.reshape(bs, ts, N_HEADS, HEAD_DIM))
            outs["av_s"].append(va.reshape(bs, ts, N_HEADS, HEAD_DIM))
            outs["aki_s"].append(ki.reshape(bs, ts, D_IDX))
            outs["bs_s"].append(s_b)
            outs["bc_s"].append(h_b)
        else:
            o = layer // 2
            w_in, widths = _pack_cols(w_in_odd[o], odd_sizes, odd_groups)
            w_out = w_out_odd[o].astype(BF16)
            qk_c, v_c, o_c, q_d, k_d, v_d, misc = _proj(xp, w_in, widths, tm_p)
            y_c, c_c, c_n, c_m = _mlstm(qk_c, v_c, o_c, misc, jnp.zeros((bp, N_HEADS, DK_C, DV_C), F32),
                                        jnp.zeros((bp, N_HEADS, DK_C), F32), jnp.zeros((bp, N_HEADS), F32),
                                        c_i_bias[o], c_f_bias[o], c_norm_w[o], nb=bp, l=CHUNK, i_col=i_col, f_col=f_col)
            k3 = k_d.reshape(bp, sp, w_a)
            v3 = v_d.reshape(bp, sp, w_a)
            pad = ((0, 0), (BAND_CHUNKS * CHUNK, 0), (0, 0))
            bias_bp = _rel_bias(d_rel_bias[o], rel_bp, np.ones(rel_bp.shape, bool))
            o_d = _band_prompt(q_d, jnp.pad(k3, pad), jnp.pad(v3, pad), bias_bp, nb=bp, tq=CHUNK)
            xp = _mm_res_ln([y_c, o_d], [w_out[:w_c], w_out[w_c:]], xp, ln_mix_g[layer], ln_mix_b[layer], alpha, tm_p)
            d_win_p = min(BAND_CHUNKS * CHUNK, sp)
            outs["cc_p"].append(c_c)
            outs["cn_p"].append(c_n)
            outs["cm_p"].append(c_m)
            outs["dk_p"].append(k3[:, sp - d_win_p:].reshape(bp, d_win_p, N_HEADS, HEAD_DIM))
            outs["dv_p"].append(v3[:, sp - d_win_p:].reshape(bp, d_win_p, N_HEADS, HEAD_DIM))
            qk_c, v_c, o_c, q_d, k_d, v_d, misc = _proj(xs, w_in, widths, tm_s)
            y_c, c_c, c_n, c_m = _mlstm(qk_c, v_c, o_c, misc, state_c_c[o], state_c_n[o], state_c_m[o],
                                        c_i_bias[o], c_f_bias[o], c_norm_w[o], nb=bs, l=ts, i_col=i_col, f_col=f_col)
            o_d = _band_sample(q_d, cache_d_k[o].reshape(bs, d_win, w_a), cache_d_v[o].reshape(bs, d_win, w_a),
                               k_d.reshape(bs, ts, w_a), v_d.reshape(bs, ts, w_a),
                               _rel_bias(d_rel_bias[o], rel_bc, band_valid(pos_kc)),
                               _rel_bias(d_rel_bias[o], rel_bn, band_valid(pos_q)), nb=bs, tq=ts)
            xs = _mm_res_ln([y_c, o_d], [w_out[:w_c], w_out[w_c:]], xs, ln_mix_g[layer], ln_mix_b[layer], alpha, tm_s)
            outs["cc_s"].append(c_c)
            outs["cn_s"].append(c_n)
            outs["cm_s"].append(c_m)
            outs["dk_s"].append(k_d.reshape(bs, ts, N_HEADS, HEAD_DIM))
            outs["dv_s"].append(v_d.reshape(bs, ts, N_HEADS, HEAD_DIM))
        w_up = ffn_w_up[layer].astype(BF16)
        w_down = ffn_w_down[layer].astype(BF16)
        act, hist_p = _ffn_up(xp, w_up, ffn_conv_w[layer], jnp.zeros((bp, CONV_FF - 1, 2 * dff), F32),
                              1, tff_p, sp // tff_p)
        xp = _mm_res_ln([act], [w_down], xp, ln_ffn_g[layer], ln_ffn_b[layer], alpha, tm_p)
        act, hist_s = _ffn_up(xs, w_up, ffn_conv_w[layer], state_ffn_conv[layer], ns_s, ts, 1)
        xs = _mm_res_ln([act], [w_down], xs, ln_ffn_g[layer], ln_ffn_b[layer], alpha, tm_s)
        outs["fc_p"].append(hist_p)
        outs["fc_s"].append(hist_s)

    st = lambda k: jnp.stack(outs[k])
    return (xp.reshape(bp, sp, d), xs.reshape(bs, ts, d),
            st("ak_p"), st("ak_s"), st("av_p"), st("av_s"), st("aki_p"), st("aki_s"),
            st("bs_p"), st("bs_s"), st("bc_p"), st("bc_s"),
            st("cc_p"), st("cc_s"), st("cn_p"), st("cn_s"), st("cm_p"), st("cm_s"),
            st("dk_p"), st("dk_s"), st("dv_p"), st("dv_s"),
            st("fc_p"), st("fc_s"))
```

```python
import functools
import math

import numpy as np
import jax
import jax.numpy as jnp
from jax import lax
from jax.experimental import pallas as pl
from jax.experimental.pallas import tpu as pltpu

F32 = jnp.float32
BF16 = jnp.bfloat16
HI = lax.Precision.HIGHEST

CHUNK = 64
HEAD_DIM = 64
N_HEADS = 8
N_IDX_HEADS = 8
D_IDX = 64
TOPK_MAX = 256
T5_BUCKETS = 32
T5_MAX_DIST = 128
DK_B = 64
DV_B = 64
CONV_B = 4
DK_C = 32
DV_C = 64
BAND_CHUNKS = 8
REL_CLIP = 128
CONV_FF = 3
NEAR_CHUNKS = 3
T5_FAR = 128

LANES = 128
SUBLANES = 8
VMEM_LIMIT = 56 * 1024 * 1024

NEG = -1e30
INT_MIN = -2 ** 31


def _params(*sem):
    return pltpu.CompilerParams(dimension_semantics=sem, vmem_limit_bytes=VMEM_LIMIT)


def _dot(a, b):
    return jnp.dot(a.astype(BF16), b.astype(BF16), preferred_element_type=F32)


def _dot_nt(a, b, precision=None):
    return lax.dot_general(a, b, (((1,), (1,)), ((), ())), precision=precision, preferred_element_type=F32)


def _dot_tn(a, b, precision=None):
    return lax.dot_general(a, b, (((0,), (0,)), ((), ())), precision=precision, preferred_element_type=F32)


def _dot_hi(a, b):
    return jnp.dot(a, b, precision=HI, preferred_element_type=F32)


def _split(a):
    hi = a.astype(BF16)
    return hi, (a - hi.astype(F32)).astype(BF16)


def _dot3(a, b):
    a_hi, a_lo = a
    b_hi, b_lo = b
    d = functools.partial(jnp.dot, preferred_element_type=F32)
    return d(a_hi, b_hi) + (d(a_hi, b_lo) + d(a_lo, b_hi))


def _sigmoid(x):
    return 1.0 / (1.0 + jnp.exp(-x))


def _softplus(x):
    return jnp.maximum(x, 0.0) + jnp.log(1.0 + jnp.exp(-jnp.abs(x)))


def _iota(shape, dim):
    return lax.broadcasted_iota(jnp.int32, shape, dim)


def _proj_kernel(x_ref, w_ref, *out_refs, sizes):
    xb = x_ref[...].astype(BF16)
    off = 0
    for o_ref, size in zip(out_refs, sizes):
        o_ref[...] = jnp.dot(xb, w_ref[:, off:off + size], preferred_element_type=F32)
        off += size


def _proj(x2d, w, sizes, tm):
    n, d = x2d.shape
    return pl.pallas_call(
        functools.partial(_proj_kernel, sizes=sizes),
        grid=(n // tm,),
        in_specs=[pl.BlockSpec((tm, d), lambda i: (i, 0)),
                  pl.BlockSpec((d, sum(sizes)), lambda i: (0, 0))],
        out_specs=[pl.BlockSpec((tm, s), lambda i: (i, 0)) for s in sizes],
        out_shape=[jax.ShapeDtypeStruct((n, s), F32) for s in sizes],
        compiler_params=_params("parallel"),
        name="in_proj",
    )(x2d, w)


def _mm_res_ln_kernel(*refs, nparts, alpha):
    part_refs = refs[:nparts]
    w_refs = refs[nparts:2 * nparts]
    x_ref, g_ref, b_ref, o_ref = refs[2 * nparts:]
    acc = alpha * x_ref[...]
    for p_ref, w_ref in zip(part_refs, w_refs):
        acc = acc + jnp.dot(p_ref[...].astype(BF16), w_ref[...], preferred_element_type=F32)
    mu = jnp.mean(acc, axis=-1, keepdims=True)
    cen = acc - mu
    var = jnp.mean(cen * cen, axis=-1, keepdims=True)
    o_ref[...] = cen * lax.rsqrt(var + 1e-5) * g_ref[...] + b_ref[...]


def _mm_res_ln(parts, ws, x2d, g, b, alpha, tm):
    n, d = x2d.shape
    nparts = len(parts)
    in_specs = ([pl.BlockSpec((tm, p.shape[1]), lambda i: (i, 0)) for p in parts]
                + [pl.BlockSpec(w.shape, lambda i: (0, 0)) for w in ws]
                + [pl.BlockSpec((tm, d), lambda i: (i, 0)),
                   pl.BlockSpec((1, d), lambda i: (0, 0)),
                   pl.BlockSpec((1, d), lambda i: (0, 0))])
    return pl.pallas_call(
        functools.partial(_mm_res_ln_kernel, nparts=nparts, alpha=alpha),
        grid=(n // tm,),
        in_specs=in_specs,
        out_specs=pl.BlockSpec((tm, d), lambda i: (i, 0)),
        out_shape=jax.ShapeDtypeStruct((n, d), F32),
        compiler_params=_params("parallel"),
        name="out_proj_ln",
    )(*parts, *ws, x2d, g.reshape(1, d), b.reshape(1, d))


def _ffn_up_kernel(x_ref, w_ref, cw_ref, hist_ref, act_ref, newhist_ref, ext_ref, *,
                   ns, tt, tiles_per_seq, dff, cc):
    i = pl.program_id(0)
    tm = ns * tt
    hw = CONV_FF - 1
    base = SUBLANES
    if tiles_per_seq == 1:
        ext_ref[:, base - hw:base, :] = hist_ref[...]
    else:
        @pl.when(i % tiles_per_seq == 0)
        def _():
            ext_ref[:, base - hw:base, :] = hist_ref[...]

        @pl.when(i % tiles_per_seq != 0)
        def _():
            ext_ref[:, base - hw:base, :] = ext_ref[:, base + tt - hw:base + tt, :]
    xb = x_ref[...].astype(BF16)
    for j in range(2 * dff // cc):
        cols = slice(j * cc, (j + 1) * cc)
        h = jnp.dot(xb, w_ref[:, cols], preferred_element_type=F32)
        ext_ref[:, base:base + tt, cols] = h.reshape(ns, tt, cc)
    newhist_ref[...] = ext_ref[:, base + tt - hw:base + tt, :]

    def conv(cols):
        acc = None
        for k in range(CONV_FF):
            term = ext_ref[:, base - hw + k:base - hw + k + tt, cols] * cw_ref[k:k + 1, cols]
            acc = term if acc is None else acc + term
        return acc

    for j in range(dff // cc):
        g = conv(slice(j * cc, (j + 1) * cc))
        u = conv(slice(dff + j * cc, dff + (j + 1) * cc))
        act = g * _sigmoid(g) * u
        act_ref[:, j * cc:(j + 1) * cc] = act.reshape(tm, cc).astype(BF16)


def _ffn_up(x2d, w_up, conv_w, hist, ns, tt, tiles_per_seq):
    n, d = x2d.shape
    c2 = w_up.shape[1]
    dff = c2 // 2
    tm = ns * tt
    cc = 256
    hw = CONV_FF - 1
    if tiles_per_seq == 1:
        hist_map = lambda i: (i, 0, 0)
    else:
        hist_map = lambda i: (i // tiles_per_seq, 0, 0)
    return pl.pallas_call(
        functools.partial(_ffn_up_kernel, ns=ns, tt=tt, tiles_per_seq=tiles_per_seq, dff=dff, cc=cc),
        grid=(n // tm,),
        in_specs=[pl.BlockSpec((tm, d), lambda i: (i, 0)),
                  pl.BlockSpec((d, c2), lambda i: (0, 0)),
                  pl.BlockSpec((CONV_FF, c2), lambda i: (0, 0)),
                  pl.BlockSpec((ns, hw, c2), hist_map)],
        out_specs=[pl.BlockSpec((tm, dff), lambda i: (i, 0)),
                   pl.BlockSpec((ns, hw, c2), hist_map)],
        out_shape=[jax.ShapeDtypeStruct((n, dff), BF16),
                   jax.ShapeDtypeStruct(hist.shape, F32)],
        scratch_shapes=[pltpu.VMEM((ns, SUBLANES + tt, c2), F32)],
        compiler_params=_params("arbitrary"),
        name="ffn_up_conv_gate",
    )(x2d, w_up, conv_w, hist)


def _sortable(x):
    b = lax.bitcast_convert_type(x, jnp.int32)
    return b ^ ((b >> 31) & jnp.int32(0x7FFFFFFF))


def _count(mask):
    return jnp.sum(jnp.where(mask, 1.0, 0.0), axis=-1, keepdims=True)


def _dsa_sample_kernel(qa_ref, qi_ref, qm_ref, kf_ref, vf_ref, kif_ref, kn_ref, vn_ref, kin_ref, nbias_ref,
                       fbias_ref, o_ref, self_ref, seln_ref, *, topk, tq, lf, ln, wi_col):
    v = 0
    start = lf - T5_FAR
    load_kn = lambda hs: kn_ref[0, :, hs]
    load_vn = lambda hs: vn_ref[0, :, hs]
    kin = kin_ref[0][:, :D_IDX]
    qa = qa_ref[...]
    qi = qi_ref[...]
    wi = qm_ref[:, wi_col:wi_col + N_IDX_HEADS] * (N_IDX_HEADS ** -0.5) * (D_IDX ** -0.5)
    kif = kif_ref[0][:, :D_IDX].astype(BF16)
    kinb = kin.astype(BF16)

    sc_f = jnp.zeros((tq, lf), F32)
    sc_n = jnp.zeros((tq, ln), F32)
    for n in range(N_IDX_HEADS):
        qn = qi[:, n * D_IDX:(n + 1) * D_IDX].astype(BF16)
        wn = wi[:, n:n + 1]
        sc_f = sc_f + jnp.maximum(_dot_nt(qn, kif), 0.0) * wn
        sc_n = sc_n + jnp.maximum(_dot_nt(qn, kinb), 0.0) * wn
    adm_f = _iota((tq, lf), 1) < start
    adm_n = nbias_ref[v, 0] > 0.5 * NEG
    key_f = jnp.where(adm_f, _sortable(sc_f), jnp.int32(INT_MIN))
    key_n = jnp.where(adm_n, _sortable(sc_n), jnp.int32(INT_MIN))

    kf32 = float(topk)

    def body(i, t_u):
        cand_u = t_u | lax.shift_left(jnp.int32(1), 31 - i)
        cand_s = cand_u ^ jnp.int32(INT_MIN)
        cnt = _count(key_f >= cand_s) + _count(key_n >= cand_s)
        return jnp.where(cnt >= kf32, cand_u, t_u)

    t_u = lax.fori_loop(0, 32, body, jnp.zeros((tq, 1), jnp.int32))
    thr = t_u ^ jnp.int32(INT_MIN)

    n_gt = _count(key_f > thr) + _count(key_n > thr)
    n_eq = _count(key_f == thr) + _count(key_n == thr)
    need = kf32 - n_gt
    open_row = thr == jnp.int32(INT_MIN)
    conflict = jnp.logical_and(n_eq != need, jnp.logical_not(open_row))
    self_ref[...] = jnp.where(jnp.logical_and(key_f >= thr, adm_f), 0.0, NEG)
    seln_ref[...] = jnp.where(jnp.logical_and(key_n >= thr, adm_n), 0.0, NEG)

    @pl.when(jnp.max(jnp.where(conflict, 1.0, 0.0)) > 0.0)
    def _():
        upper = jnp.where(_iota((LANES, LANES), 0) < _iota((LANES, LANES), 1), 1.0, 0.0).astype(BF16)
        offset = jnp.zeros((tq, 1), F32)
        for ref, key, width in ((self_ref, key_f, lf), (seln_ref, key_n, ln)):
            for j0 in range(0, width, LANES):
                w = min(LANES, width - j0)
                kb = key[:, j0:j0 + w]
                e = jnp.where(kb == thr, 1.0, 0.0)
                rank = offset + jnp.dot(e.astype(BF16), upper[:w, :w], preferred_element_type=F32)
                take = jnp.where(kb > thr, 1.0, jnp.where(rank < need, e, 0.0))
                take = jnp.where(open_row, jnp.where(kb > thr, 1.0, 0.0), take)
                ref[:, j0:j0 + w] = jnp.where(take > 0.5, 0.0, NEG)
                offset = offset + jnp.sum(e, axis=-1, keepdims=True)

    sel_f = self_ref[...]
    sel_n = seln_ref[...]
    for h in range(N_HEADS):
        hs = slice(h * HEAD_DIM, (h + 1) * HEAD_DIM)
        qh = qa[:, hs].astype(BF16)
        s_f = _dot_nt(qh, kf_ref[0, :, hs].astype(BF16)) * (HEAD_DIM ** -0.5) + fbias_ref[:, h:h + 1] + sel_f
        s_n = _dot_nt(qh, load_kn(hs).astype(BF16)) * (HEAD_DIM ** -0.5) + nbias_ref[v, h] + sel_n
        m = jnp.maximum(jnp.max(s_f, axis=-1, keepdims=True), jnp.max(s_n, axis=-1, keepdims=True))
        p_f = jnp.exp(s_f - m)
        p_n = jnp.exp(s_n - m)
        den = jnp.sum(p_f, axis=-1, keepdims=True) + jnp.sum(p_n, axis=-1, keepdims=True)
        o = _dot(p_f, vf_ref[0, :, hs]) + _dot(p_n, load_vn(hs))
        o_ref[:, hs] = o / den


def _dsa_sample(qa, qi, qmisc, kf, vf, kif, near, nbias, fbias, *, nb, tq, topk, wi_col):
    lf = kf.shape[1]
    ln = nbias.shape[-1]
    row = lambda b: (b, 0)
    per_b = lambda b: (b, 0, 0)
    args = [qa, qi, qmisc, kf, vf, kif, *near]
    in_specs = ([pl.BlockSpec((tq, a.shape[1]), row) for a in args[:3]]
                + [pl.BlockSpec((1,) + a.shape[1:], per_b) for a in args[3:]]
                + [pl.BlockSpec(nbias.shape, lambda b: (0, 0, 0, 0)),
                   pl.BlockSpec(fbias.shape, lambda b: (0, 0))])
    return pl.pallas_call(
        functools.partial(_dsa_sample_kernel, topk=topk, tq=tq, lf=lf, ln=ln, wi_col=wi_col),
        grid=(nb,),
        in_specs=in_specs,
        out_specs=pl.BlockSpec((tq, qa.shape[1]), row),
        out_shape=jax.ShapeDtypeStruct(qa.shape, F32),
        scratch_shapes=[pltpu.VMEM((tq, lf), F32), pltpu.VMEM((tq, ln), F32)],
        compiler_params=_params("parallel"),
        name="dsa_attention_sample",
    )(*args, nbias, fbias)


def _reduce_rows(x, op, final):
    blk = 8 * SUBLANES
    parts = [x[r0:r0 + blk] for r0 in range(0, x.shape[0], blk)]
    while len(parts) > 1:
        parts = [op(parts[i], parts[i + 1]) for i in range(0, len(parts) - 1, 2)] + parts[len(parts) & ~1:]
    return final(parts[0], axis=0, keepdims=True)


def _dsa_prompt_tile(g, lk, qa_ref, qi_ref, qm_ref, corr_ref, o_ref,
                     kb_ref, vt_ref, kib_ref, key_ref, sel_ref, s_ref, ot_ref, *, topk, tq, wi_col):
    rb = 2 * LANES if lk % (2 * LANES) == 0 else LANES
    nw = corr_ref.shape[2]
    qa = (qa_ref[...] * (HEAD_DIM ** -0.5)).astype(BF16)
    qi = qi_ref[...].astype(BF16)
    wi = qm_ref[:, wi_col:wi_col + N_IDX_HEADS] * (N_IDX_HEADS ** -0.5) * (D_IDX ** -0.5)
    eye_h = jnp.where(_iota((N_IDX_HEADS, N_IDX_HEADS), 0) == _iota((N_IDX_HEADS, N_IDX_HEADS), 1), 1.0, 0.0)
    wi_t = _dot_nt(eye_h, wi, HI)
    q_chunk = (g * tq + _iota((1, tq), 1)) // CHUNK

    for r0 in range(0, lk, rb):
        kib = kib_ref[r0:r0 + rb, :D_IDX]
        acc = jnp.zeros((rb, tq), F32)
        for n in range(N_IDX_HEADS):
            acc = acc + jnp.maximum(_dot_nt(kib, qi[:, n * D_IDX:(n + 1) * D_IDX]), 0.0) * wi_t[n:n + 1, :]
        adm = (r0 + _iota((rb, 1), 0)) // CHUNK <= q_chunk
        key_ref[r0:r0 + rb, :] = jnp.where(adm, _sortable(acc), jnp.int32(INT_MIN))

    kf32 = float(topk)
    count = lambda mask: _reduce_rows(jnp.where(mask, 1.0, 0.0), jnp.add, jnp.sum)

    def body(i, t_u):
        cand_u = t_u | lax.shift_left(jnp.int32(1), 31 - i)
        cand_s = cand_u ^ jnp.int32(INT_MIN)
        return jnp.where(count(key_ref[0:lk, :] >= cand_s) >= kf32, cand_u, t_u)

    t_u = lax.fori_loop(0, 32, body, jnp.zeros((1, tq), jnp.int32))
    thr = t_u ^ jnp.int32(INT_MIN)
    keys = key_ref[0:lk, :]
    need = kf32 - count(keys > thr)
    open_row = thr == jnp.int32(INT_MIN)
    conflict = jnp.logical_and(count(keys == thr) != need, jnp.logical_not(open_row))
    sel_ref[0:lk, :] = jnp.where(keys >= jnp.maximum(thr, jnp.int32(INT_MIN + 1)), 0.0, NEG)

    @pl.when(jnp.max(jnp.where(conflict, 1.0, 0.0)) > 0.0)
    def _():
        below = jnp.where(_iota((LANES, LANES), 1) < _iota((LANES, LANES), 0), 1.0, 0.0).astype(BF16)
        offset = jnp.zeros((1, tq), F32)
        for r0 in range(0, lk, LANES):
            kblk = key_ref[r0:r0 + LANES, :]
            e = jnp.where(kblk == thr, 1.0, 0.0)
            rank = offset + jnp.dot(below, e.astype(BF16), preferred_element_type=F32)
            take = jnp.where(kblk > thr, 1.0, jnp.where(jnp.logical_or(rank >= need, open_row), 0.0, e))
            sel_ref[r0:r0 + LANES, :] = jnp.where(take > 0.5, 0.0, NEG)
            offset = offset + jnp.sum(e, axis=0, keepdims=True)

    v = jnp.minimum(g, 1)
    start = pl.multiple_of(jnp.maximum(g * tq - T5_FAR, 0), LANES)
    for h in range(N_HEADS):
        hs = slice(h * HEAD_DIM, (h + 1) * HEAD_DIM)
        s_ref[0:lk, :] = _dot_nt(kb_ref[0:lk, hs], qa[:, hs]) + sel_ref[0:lk, :]
        s_ref[pl.ds(start, nw), :] += corr_ref[v, h]
        s = s_ref[0:lk, :]
        p = jnp.exp(s - _reduce_rows(s, jnp.maximum, jnp.max))
        pn = (p * (1.0 / _reduce_rows(p, jnp.add, jnp.sum))).astype(BF16)
        ot_ref[hs, :] = jnp.dot(vt_ref[hs, 0:lk], pn, preferred_element_type=F32)
    o_ref[...] = ot_ref[...].T


def _dsa_prompt_kernel(qa_ref, qi_ref, qm_ref, k_ref, v_ref, ki_ref, corr_ref, o_ref,
                       kb_ref, vt_ref, kib_ref, key_ref, sel_ref, s_ref, ot_ref, *, topk, tq, seq, n_groups, wi_col):
    g = pl.program_id(1)
    blk = _row_tile(seq, 4 * LANES)

    @pl.when(g == 0)
    def _():
        for r0 in range(0, seq, blk):
            kb_ref[r0:r0 + blk, :] = k_ref[0, r0:r0 + blk, :].astype(BF16)
            vt_ref[:, r0:r0 + blk] = v_ref[0, r0:r0 + blk, :].T.astype(BF16)
            kib_ref[r0:r0 + blk, :] = ki_ref[0, r0:r0 + blk, :].astype(BF16)

    per_group = (seq // tq) // n_groups
    for grp in range(n_groups):
        @pl.when(g // per_group == grp)
        def _():
            _dsa_prompt_tile(g, (grp + 1) * per_group * tq, qa_ref, qi_ref, qm_ref, corr_ref, o_ref,
                             kb_ref, vt_ref, kib_ref, key_ref, sel_ref, s_ref, ot_ref,
                             topk=topk, tq=tq, wi_col=wi_col)


def _dsa_prompt(qa, qi, qmisc, k, v, ki, corr, *, nb, tq, topk, wi_col, n_groups):
    n, w = qa.shape
    seq = k.shape[1]
    nq = seq // tq
    row = lambda b, g: (b * nq + g, 0)
    per_b = lambda b, g: (b, 0, 0)
    return pl.pallas_call(
        functools.partial(_dsa_prompt_kernel, topk=topk, tq=tq, seq=seq, n_groups=n_groups, wi_col=wi_col),
        grid=(nb, nq),
        in_specs=[pl.BlockSpec((tq, w), row),
                  pl.BlockSpec((tq, qi.shape[1]), row),
                  pl.BlockSpec((tq, qmisc.shape[1]), row),
                  pl.BlockSpec((1, seq, w), per_b),
                  pl.BlockSpec((1, seq, w), per_b),
                  pl.BlockSpec((1, seq, ki.shape[2]), per_b),
                  pl.BlockSpec(corr.shape, lambda b, g: (0, 0, 0, 0))],
        out_specs=pl.BlockSpec((tq, w), row),
        out_shape=jax.ShapeDtypeStruct(qa.shape, F32),
        scratch_shapes=[pltpu.VMEM((seq, w), BF16),
                        pltpu.VMEM((w, seq), BF16),
                        pltpu.VMEM((seq, ki.shape[2]), BF16),
                        pltpu.VMEM((seq, tq), jnp.int32),
                        pltpu.VMEM((seq, tq), F32),
                        pltpu.VMEM((seq, tq), F32),
                        pltpu.VMEM((w, tq), F32)],
        compiler_params=_params("parallel", "arbitrary"),
        name="dsa_attention_prompt",
    )(qa, qi, qmisc, k, v, ki, corr)


def _unit_lower_inverse(mats, n):
    eye = jnp.where(_iota((n, n), 0) == _iota((n, n), 1), 1.0, 0.0)
    ps = [eye - a for a in mats]
    aks = [_split(a) for a in mats]
    k = 1
    while 2 * k < n:
        aks = [_split(_dot3(ak, ak)) for ak in aks]
        ps = [p + _dot3(_split(p), ak) for p, ak in zip(ps, aks)]
        k *= 2
    return ps


def _head_rms(x, w):
    return x * lax.rsqrt(jnp.mean(x * x, axis=-1, keepdims=True) + 1e-6) * w


def _gdn_kernel(qkv_ref, misc_ref, z_ref, hist_ref, s0_ref, cw_ref, alog_ref, dtb_ref, nw_ref,
                y_ref, sfin_ref, newhist_ref, ext_ref, s_ref, *, c, a_col, b_col):
    j = pl.program_id(1)
    hw = CONV_B - 1
    base = SUBLANES
    wq = N_HEADS * DK_B

    @pl.when(j == 0)
    def _():
        ext_ref[base - hw:base, :] = hist_ref[0]
        s_ref[...] = s0_ref[0]

    @pl.when(j > 0)
    def _():
        ext_ref[base - hw:base, :] = ext_ref[base + c - hw:base + c, :]

    ext_ref[base:base + c, :] = qkv_ref[...]
    newhist_ref[0] = ext_ref[base + c - hw:base + c, :]
    conv = None
    for k in range(CONV_B):
        term = ext_ref[base - hw + k:base - hw + k + c, :] * cw_ref[k:k + 1, :]
        conv = term if conv is None else conv + term
    act = conv * _sigmoid(conv)

    beta = _sigmoid(misc_ref[:, b_col:b_col + N_HEADS])
    g = -jnp.exp(alog_ref[...]) * _softplus(misc_ref[:, a_col:a_col + N_HEADS] + dtb_ref[...])
    ri = _iota((c, c), 0)
    ci = _iota((c, c), 1)
    lower = ri >= ci
    strict = ri > ci
    gc = _dot_hi(jnp.where(lower, 1.0, 0.0), g)
    eye_h = jnp.where(_iota((N_HEADS, N_HEADS), 0) == _iota((N_HEADS, N_HEADS), 1), 1.0, 0.0)
    gc_t = _dot_nt(eye_h, gc, HI)
    eg = jnp.exp(gc)
    g_last = gc[c - 1:c, :]
    e_last = jnp.exp(g_last)
    e_rest = jnp.exp(g_last - gc)
    z = z_ref[...]
    heads = range(N_HEADS)
    col = lambda x, h: x[:, h:h + 1]
    qs = [act[:, h * DK_B:(h + 1) * DK_B] for h in heads]
    ks = [act[:, wq + h * DK_B:wq + (h + 1) * DK_B] for h in heads]
    vs = [act[:, 2 * wq + h * DV_B:2 * wq + (h + 1) * DV_B] for h in heads]
    qs = [q * lax.rsqrt(jnp.sum(q * q, axis=-1, keepdims=True) + 1e-6) * (DK_B ** -0.5) for q in qs]
    ks = [k * lax.rsqrt(jnp.sum(k * k, axis=-1, keepdims=True) + 1e-6) for k in ks]
    kbs = [k.astype(BF16) for k in ks]
    decay = [jnp.where(lower, jnp.exp(jnp.where(lower, col(gc, h) - gc_t[h:h + 1, :], 0.0)), 0.0) for h in heads]
    kk = [_dot_nt(kbs[h], kbs[h]) for h in heads]
    attn = [_dot_nt(qs[h].astype(BF16), kbs[h]) * decay[h] for h in heads]
    a_mat = [jnp.where(strict, col(beta, h) * kk[h] * decay[h], 0.0) for h in heads]
    t_mat = [t.astype(BF16) for t in _unit_lower_inverse(a_mat, c)]
    value = [_dot(t_mat[h], vs[h] * col(beta, h)) for h in heads]
    k_cum = [_dot(t_mat[h], ks[h] * (col(beta, h) * col(eg, h))) for h in heads]
    s_old = [s_ref[h] for h in heads]
    sbs = [s.astype(BF16) for s in s_old]
    v_new = [(value[h] - _dot(k_cum[h], sbs[h])).astype(BF16) for h in heads]
    o_inter = [_dot(qs[h] * col(eg, h), sbs[h]) for h in heads]
    o = [o_inter[h] + _dot(attn[h], v_new[h]) for h in heads]
    for h in heads:
        s_ref[h] = s_old[h] * col(e_last, h) + _dot_tn((ks[h] * col(e_rest, h)).astype(BF16), v_new[h])
    for h in heads:
        zh = z[:, h * DV_B:(h + 1) * DV_B]
        y_ref[:, h * DV_B:(h + 1) * DV_B] = _head_rms(o[h], nw_ref[...]) * (zh * _sigmoid(zh))
    sfin_ref[0] = s_ref[...]


def _gdn(qkv, misc, z, hist, s0, conv_w, a_log, dt_bias, norm_w, *, nb, c, a_col, b_col):
    n, wqkv = qkv.shape
    nch = n // (nb * c)
    row = lambda b, j: (b * nch + j, 0)
    per_b3 = lambda b, j: (b, 0, 0)
    per_b4 = lambda b, j: (b, 0, 0, 0)
    const2 = lambda b, j: (0, 0)
    hw = CONV_B - 1
    return pl.pallas_call(
        functools.partial(_gdn_kernel, c=c, a_col=a_col, b_col=b_col),
        grid=(nb, nch),
        in_specs=[pl.BlockSpec((c, wqkv), row),
                  pl.BlockSpec((c, misc.shape[1]), row),
                  pl.BlockSpec((c, z.shape[1]), row),
                  pl.BlockSpec((1, hw, wqkv), per_b3),
                  pl.BlockSpec((1,) + s0.shape[1:], per_b4),
                  pl.BlockSpec((CONV_B, wqkv), const2),
                  pl.BlockSpec((1, N_HEADS), const2),
                  pl.BlockSpec((1, N_HEADS), const2),
                  pl.BlockSpec((1, DV_B), const2)],
        out_specs=[pl.BlockSpec((c, z.shape[1]), row),
                   pl.BlockSpec((1,) + s0.shape[1:], per_b4),
                   pl.BlockSpec((1, hw, wqkv), per_b3)],
        out_shape=[jax.ShapeDtypeStruct(z.shape, F32),
                   jax.ShapeDtypeStruct(s0.shape, F32),
                   jax.ShapeDtypeStruct(hist.shape, F32)],
        scratch_shapes=[pltpu.VMEM((SUBLANES + c, wqkv), F32),
                        pltpu.VMEM(s0.shape[1:], F32)],
        compiler_params=_params("parallel", "arbitrary"),
        name="gated_deltanet",
    )(qkv, misc, z, hist, s0, conv_w, a_log.reshape(1, -1), dt_bias.reshape(1, -1), norm_w.reshape(1, -1))


def _mlstm_kernel(qk_ref, v_ref, og_ref, misc_ref, c0_ref, n0_ref, m0_ref, ib_ref, fb_ref, nw_ref,
                  y_ref, cfin_ref, nfin_ref, mfin_ref, c_ref, n_ref, m_ref, *, l, i_col, f_col):
    j = pl.program_id(1)
    wq = N_HEADS * DK_C

    @pl.when(j == 0)
    def _():
        c_ref[...] = c0_ref[0]
        n_ref[...] = n0_ref[0]
        m_ref[...] = m0_ref[0]

    ig = misc_ref[:, i_col:i_col + N_HEADS] + ib_ref[...]
    lf = -_softplus(-(misc_ref[:, f_col:f_col + N_HEADS] + fb_ref[...]))
    ri = _iota((l, l), 0)
    ci = _iota((l, l), 1)
    causal = ri >= ci
    fc = _dot_hi(jnp.where(causal, 1.0, 0.0), lf)
    eye_h = jnp.where(_iota((N_HEADS, N_HEADS), 0) == _iota((N_HEADS, N_HEADS), 1), 1.0, 0.0)
    row_terms = _dot_nt(eye_h, ig - fc, HI)
    m_prev = m_ref[...]
    log_inter = fc + m_prev
    f_last = fc[l - 1:l, :]
    qk = qk_ref[...]
    v = v_ref[...]
    og = og_ref[...]
    heads = range(N_HEADS)
    col = lambda x, h: x[:, h:h + 1]
    qs = [qk[:, h * DK_C:(h + 1) * DK_C] for h in heads]
    ks = [qk[:, wq + h * DK_C:wq + (h + 1) * DK_C] * (DK_C ** -0.5) for h in heads]
    vbs = [v[:, h * DV_C:(h + 1) * DV_C].astype(BF16) for h in heads]
    c_old = [c_ref[h] for h in heads]
    n_old = [n_ref[h:h + 1, :] for h in heads]
    qk_raw = [_dot_nt(qs[h].astype(BF16), ks[h].astype(BF16)) for h in heads]
    q_c = [_dot(qs[h], c_old[h]) for h in heads]
    log_w = [jnp.where(causal, col(fc, h) + row_terms[h:h + 1, :], -jnp.inf) for h in heads]
    m_t = [jnp.maximum(col(log_inter, h), jnp.max(log_w[h], axis=-1, keepdims=True)) for h in heads]
    w_inter = [jnp.exp(col(log_inter, h) - m_t[h]) for h in heads]
    qkw = [qk_raw[h] * jnp.exp(log_w[h] - m_t[h]) for h in heads]
    num = [_dot(qkw[h], vbs[h]) + w_inter[h] * q_c[h] for h in heads]
    den = [jnp.sum(qkw[h], axis=-1, keepdims=True) + w_inter[h] * jnp.sum(qs[h] * n_old[h], axis=-1, keepdims=True)
           for h in heads]
    hid = [num[h] / jnp.maximum(jnp.abs(den[h]), jnp.exp(-m_t[h])) for h in heads]
    m_vec = f_last + jnp.maximum(m_prev, jnp.max(ig - fc, axis=0, keepdims=True))
    dec_vec = jnp.exp(f_last + m_prev - m_vec)
    w_last = jnp.exp(f_last - fc + ig - m_vec)
    kw = [ks[h] * col(w_last, h) for h in heads]
    for h in heads:
        c_ref[h] = col(dec_vec, h) * c_old[h] + _dot_tn(kw[h].astype(BF16), vbs[h])
        n_ref[h:h + 1, :] = col(dec_vec, h) * n_old[h] + jnp.sum(kw[h], axis=0, keepdims=True)
    for h in heads:
        oh = og[:, h * DV_C:(h + 1) * DV_C]
        y_ref[:, h * DV_C:(h + 1) * DV_C] = _head_rms(hid[h], nw_ref[...]) * _sigmoid(oh)
    m_ref[...] = m_vec
    cfin_ref[0] = c_ref[...]
    nfin_ref[0] = n_ref[...]
    mfin_ref[0] = m_ref[...]


def _mlstm(qk, v, og, misc, c0, n0, m0, i_bias, f_bias, norm_w, *, nb, l, i_col, f_col):
    n = qk.shape[0]
    nch = n // (nb * l)
    row = lambda b, j: (b * nch + j, 0)
    per_b3 = lambda b, j: (b, 0, 0)
    per_b4 = lambda b, j: (b, 0, 0, 0)
    const2 = lambda b, j: (0, 0)
    m0 = m0.reshape(nb, 1, N_HEADS)
    outs = pl.pallas_call(
        functools.partial(_mlstm_kernel, l=l, i_col=i_col, f_col=f_col),
        grid=(nb, nch),
        in_specs=[pl.BlockSpec((l, qk.shape[1]), row),
                  pl.BlockSpec((l, v.shape[1]), row),
                  pl.BlockSpec((l, og.shape[1]), row),
                  pl.BlockSpec((l, misc.shape[1]), row),
                  pl.BlockSpec((1,) + c0.shape[1:], per_b4),
                  pl.BlockSpec((1,) + n0.shape[1:], per_b3),
                  pl.BlockSpec((1, 1, N_HEADS), per_b3),
                  pl.BlockSpec((1, N_HEADS), const2),
                  pl.BlockSpec((1, N_HEADS), const2),
                  pl.BlockSpec((1, DV_C), const2)],
        out_specs=[pl.BlockSpec((l, v.shape[1]), row),
                   pl.BlockSpec((1,) + c0.shape[1:], per_b4),
                   pl.BlockSpec((1,) + n0.shape[1:], per_b3),
                   pl.BlockSpec((1, 1, N_HEADS), per_b3)],
        out_shape=[jax.ShapeDtypeStruct(v.shape, F32),
                   jax.ShapeDtypeStruct(c0.shape, F32),
                   jax.ShapeDtypeStruct(n0.shape, F32),
                   jax.ShapeDtypeStruct(m0.shape, F32)],
        scratch_shapes=[pltpu.VMEM(c0.shape[1:], F32),
                        pltpu.VMEM(n0.shape[1:], F32),
                        pltpu.VMEM((1, N_HEADS), F32)],
        compiler_params=_params("parallel", "arbitrary"),
        name="mlstm",
    )(qk, v, og, misc, c0, n0, m0, i_bias.reshape(1, -1), f_bias.reshape(1, -1), norm_w.reshape(1, -1))
    y, c_fin, n_fin, m_fin = outs
    return y, c_fin, n_fin, m_fin.reshape(nb, N_HEADS)


def _band_kernel(*refs, prompt, tq, lw):
    if prompt:
        q_ref, k_ref, v_ref, bias_ref, o_ref = refs
        c = pl.program_id(1)
        start = pl.multiple_of(c * CHUNK, CHUNK)
        pieces = [(lambda hs: k_ref[0, pl.ds(start, lw), hs], lambda hs: v_ref[0, pl.ds(start, lw), hs], bias_ref)]
        masks = [jnp.where(_iota((tq, lw), 1) >= (BAND_CHUNKS - c) * CHUNK, 0.0, NEG)]
    else:
        q_ref, kc_ref, vc_ref, kn_ref, vn_ref, biasc_ref, biasn_ref, o_ref = refs
        pieces = [(lambda hs: kc_ref[0, :, hs], lambda hs: vc_ref[0, :, hs], biasc_ref),
                  (lambda hs: kn_ref[0, :, hs], lambda hs: vn_ref[0, :, hs], biasn_ref)]
        masks = [None, None]
    q = q_ref[...]
    for h in range(N_HEADS):
        hs = slice(h * HEAD_DIM, (h + 1) * HEAD_DIM)
        qh = q[:, hs].astype(BF16)
        scores = []
        for (load_k, _, b_ref), mask in zip(pieces, masks):
            s = _dot_nt(qh, load_k(hs).astype(BF16)) * (HEAD_DIM ** -0.5) + b_ref[h]
            scores.append(s if mask is None else s + mask)
        m = functools.reduce(jnp.maximum, [jnp.max(s, axis=-1, keepdims=True) for s in scores])
        ps = [jnp.exp(s - m) for s in scores]
        den = sum(jnp.sum(p, axis=-1, keepdims=True) for p in ps)
        o = sum(_dot(p, load_v(hs)) for p, (_, load_v, _) in zip(ps, pieces))
        o_ref[:, hs] = o / den


def _band_prompt(q, k_pad, v_pad, bias, *, nb, tq):
    n, w = q.shape
    nq = n // (nb * tq)
    lw = bias.shape[-1]
    row = lambda b, c: (b * nq + c, 0)
    per_b = lambda b, c: (b, 0, 0)
    return pl.pallas_call(
        functools.partial(_band_kernel, prompt=True, tq=tq, lw=lw),
        grid=(nb, nq),
        in_specs=[pl.BlockSpec((tq, w), row),
                  pl.BlockSpec((1,) + k_pad.shape[1:], per_b),
                  pl.BlockSpec((1,) + v_pad.shape[1:], per_b),
                  pl.BlockSpec(bias.shape, lambda b, c: (0, 0, 0))],
        out_specs=pl.BlockSpec((tq, w), row),
        out_shape=jax.ShapeDtypeStruct(q.shape, F32),
        compiler_params=_params("parallel", "arbitrary"),
        name="band_attention_prompt",
    )(q, k_pad, v_pad, bias)


def _band_sample(q, kc, vc, kn, vn, bias_c, bias_n, *, nb, tq):
    n, w = q.shape
    row = lambda b, c: (b, 0)
    per_b = lambda b, c: (b, 0, 0)
    const3 = lambda b, c: (0, 0, 0)
    return pl.pallas_call(
        functools.partial(_band_kernel, prompt=False, tq=tq, lw=None),
        grid=(nb, 1),
        in_specs=[pl.BlockSpec((tq, w), row),
                  pl.BlockSpec((1,) + kc.shape[1:], per_b),
                  pl.BlockSpec((1,) + vc.shape[1:], per_b),
                  pl.BlockSpec((1,) + kn.shape[1:], per_b),
                  pl.BlockSpec((1,) + vn.shape[1:], per_b),
                  pl.BlockSpec(bias_c.shape, const3),
                  pl.BlockSpec(bias_n.shape, const3)],
        out_specs=pl.BlockSpec((tq, w), row),
        out_shape=jax.ShapeDtypeStruct(q.shape, F32),
        compiler_params=_params("parallel", "arbitrary"),
        name="band_attention_sample",
    )(q, kc, vc, kn, vn, bias_c, bias_n)


def _t5_bucket(rel):
    nb = T5_BUCKETS // 2
    max_exact = nb // 2
    n = jnp.abs(rel)
    n_f = jnp.maximum(n, 1).astype(jnp.float32)
    large = max_exact + (jnp.log(n_f / max_exact) / math.log(T5_MAX_DIST / max_exact) * (nb - max_exact)).astype(jnp.int32)
    large = jnp.minimum(large, nb - 1)
    return jnp.where(rel > 0, nb, 0) + jnp.where(n < max_exact, n, large)


def _toeplitz_bias(fn, n_rows, n_cols):
    n = n_rows + n_cols
    m = np.arange(n)
    f = jnp.transpose(fn(np.where(m < n_cols, m, m - n))).astype(F32)
    flat = jnp.tile(f, (1, n_rows))[:, :n_rows * (n - 1)]
    return flat.reshape(f.shape[0], n_rows, n - 1)[:, :, :n_cols]


def _pack_cols(w, sizes, groups):
    offs = np.concatenate([[0], np.cumsum(sizes)])
    cols, widths = [], []
    for grp in groups:
        width = 0
        for idx in grp:
            cols.append(w[:, offs[idx]:offs[idx + 1]])
            width += sizes[idx]
        pad = (-width) % LANES
        if pad:
            cols.append(jnp.zeros((w.shape[0], pad), w.dtype))
        widths.append(width + pad)
    return jnp.concatenate(cols, axis=1).astype(BF16), tuple(widths)


def _row_tile(n, target):
    t = min(n, target)
    while n % t:
        t //= 2
    return t


def kernel(x_prompt, x_sample, cache_a_k, cache_a_v, cache_a_kidx, state_b_s, state_b_conv, state_c_c, state_c_n, state_c_m, cache_d_k, cache_d_v, state_ffn_conv, w_in_even, w_out_even, t5_bias, b_conv_w, b_a_log, b_dt_bias, b_norm_w, w_in_odd, w_out_odd, c_i_bias, c_f_bias, c_norm_w, d_rel_bias, ln_mix_g, ln_mix_b, ln_ffn_g, ln_ffn_b, ffn_w_up, ffn_conv_w, ffn_w_down):
    bp, sp, d = x_prompt.shape
    bs, ts, _ = x_sample.shape
    depth = ffn_w_up.shape[0]
    past = cache_a_k.shape[2]
    d_win = cache_d_k.shape[2]
    dff = ffn_w_down.shape[1]
    alpha = (2 * depth) ** 0.25
    w_a = N_HEADS * HEAD_DIM
    w_b = N_HEADS * DV_B
    w_c = N_HEADS * DV_C
    qkv_b_w = 2 * N_HEADS * DK_B + w_b
    even_sizes = (w_a, w_a, w_a, N_IDX_HEADS * D_IDX, D_IDX, N_IDX_HEADS, qkv_b_w, N_HEADS, N_HEADS, w_b)
    odd_sizes = (N_HEADS * DK_C, N_HEADS * DK_C, w_c, N_HEADS, N_HEADS, w_c, w_a, w_a, w_a)
    even_groups = ((0,), (1,), (2,), (3,), (6,), (9,), (4, 5, 7, 8))
    wi_col, a_col, b_col = D_IDX, D_IDX + N_IDX_HEADS, D_IDX + N_IDX_HEADS + N_HEADS
    odd_groups = ((0, 1), (2,), (5,), (6,), (7,), (8,), (3, 4))
    i_col, f_col = 0, N_HEADS

    assert sp % CHUNK == 0 and ts <= CHUNK and past % CHUNK == 0 and past >= T5_FAR
    assert (past + ts - 1) // CHUNK == past // CHUNK
    topk_p = min(TOPK_MAX, sp // 4)
    topk_s = min(TOPK_MAX, (past + ts) // 4)
    n_p, n_s = bp * sp, bs * ts
    tm_p = _row_tile(n_p, 512)
    tm_s = _row_tile(n_s, 512)
    tff_p = _row_tile(sp, 512)
    ns_s = _row_tile(bs, max(1, 256 // ts))

    t5 = lambda rel: t5_bias[_t5_bucket(jnp.asarray(rel, jnp.int32))]
    fbias = t5(np.array([-T5_FAR - 1]))
    tq_a = 2 * CHUNK
    nw_a = tq_a + T5_FAR
    assert sp % tq_a == 0 and sp >= nw_a
    nq_a = sp // tq_a
    dsa_groups = max(gr for gr in (4, 2, 1) if nq_a % gr == 0 and (nq_a // gr) * tq_a >= nw_a)
    corr_p = jnp.stack([_toeplitz_bias(lambda dd: t5(-dd - T5_FAR * v) - fbias, nw_a, tq_a) for v in range(2)])
    ln_s = T5_FAR + ts
    nbias_s = _toeplitz_bias(lambda dd: t5(dd - T5_FAR), ts, ln_s)[None]

    lw = (BAND_CHUNKS + 1) * CHUNK
    pos_q = past + np.arange(ts)
    pos_kc = past - d_win + np.arange(d_win)
    def band_valid(pos_k):
        kch, qch = pos_k // CHUNK, pos_q // CHUNK
        return (pos_k[None] >= 0) & (kch[None] >= qch[:, None] - BAND_CHUNKS) & (kch[None] <= qch[:, None])
    def band_bias(table, shift, n_rows, n_cols, valid=None):
        bias = _toeplitz_bias(lambda dd: table[np.clip(dd + shift, -REL_CLIP, REL_CLIP) + REL_CLIP], n_rows, n_cols)
        return bias if valid is None else jnp.where(jnp.asarray(valid)[None], bias, NEG)

    xp = x_prompt.reshape(n_p, d)
    xs = x_sample.reshape(n_s, d)
    outs = {k: [] for k in ("ak_p", "ak_s", "av_p", "av_s", "aki_p", "aki_s", "bs_p", "bs_s", "bc_p", "bc_s",
                            "cc_p", "cc_s", "cn_p", "cn_s", "cm_p", "cm_s", "dk_p", "dk_s", "dv_p", "dv_s",
                            "fc_p", "fc_s")}
    for layer in range(depth):
        if layer % 2 == 0:
            e = layer // 2
            w_in, widths = _pack_cols(w_in_even[e], even_sizes, even_groups)
            w_out = w_out_even[e].astype(BF16)
            qa, ka, va, qi, qkv_b, z_b, misc = _proj(xp, w_in, widths, tm_p)
            o_a = _dsa_prompt(qa, qi, misc, ka.reshape(bp, sp, w_a), va.reshape(bp, sp, w_a),
                              misc.reshape(bp, sp, LANES), corr_p, nb=bp, tq=tq_a, topk=topk_p, wi_col=wi_col,
                              n_groups=dsa_groups)
            y_b, s_b, h_b = _gdn(qkv_b, misc, z_b, jnp.zeros((bp, CONV_B - 1, qkv_b_w), F32),
                                 jnp.zeros((bp, N_HEADS, DK_B, DV_B), F32), b_conv_w[e], b_a_log[e], b_dt_bias[e],
                                 b_norm_w[e], nb=bp, c=CHUNK, a_col=a_col, b_col=b_col)
            xp = _mm_res_ln([o_a, y_b], [w_out[:w_a], w_out[w_a:]], xp, ln_mix_g[layer], ln_mix_b[layer], alpha, tm_p)
            outs["ak_p"].append(ka.reshape(bp, sp, N_HEADS, HEAD_DIM))
            outs["av_p"].append(va.reshape(bp, sp, N_HEADS, HEAD_DIM))
            outs["aki_p"].append(misc[:, :D_IDX].reshape(bp, sp, D_IDX))
            outs["bs_p"].append(s_b)
            outs["bc_p"].append(h_b)
            qa, ka, va, qi, qkv_b, z_b, misc = _proj(xs, w_in, widths, tm_s)
            ki = misc[:, :D_IDX]
            ck = cache_a_k[e].reshape(bs, past, w_a)
            cv = cache_a_v[e].reshape(bs, past, w_a)
            cki = cache_a_kidx[e]
            near = (jnp.concatenate([ck[:, past - T5_FAR:], ka.reshape(bs, ts, w_a)], axis=1),
                    jnp.concatenate([cv[:, past - T5_FAR:], va.reshape(bs, ts, w_a)], axis=1),
                    jnp.concatenate([cki[:, past - T5_FAR:], ki.reshape(bs, ts, D_IDX)], axis=1))
            o_a = _dsa_sample(qa, qi, misc, ck, cv, cki, near, nbias_s, fbias, nb=bs, tq=ts, topk=topk_s,
                              wi_col=wi_col)
            y_b, s_b, h_b = _gdn(qkv_b, misc, z_b, state_b_conv[e], state_b_s[e], b_conv_w[e], b_a_log[e],
                                 b_dt_bias[e], b_norm_w[e], nb=bs, c=ts, a_col=a_col, b_col=b_col)
            xs = _mm_res_ln([o_a, y_b], [w_out[:w_a], w_out[w_a:]], xs, ln_mix_g[layer], ln_mix_b[layer], alpha, tm_s)
            outs["ak_s"].append(ka.reshape(bs, ts, N_HEADS, HEAD_DIM))
            outs["av_s"].append(va.reshape(bs, ts, N_HEADS, HEAD_DIM))
            outs["aki_s"].append(ki.reshape(bs, ts, D_IDX))
            outs["bs_s"].append(s_b)
            outs["bc_s"].append(h_b)
        else:
            o = layer // 2
            w_in, widths = _pack_cols(w_in_odd[o], odd_sizes, odd_groups)
            w_out = w_out_odd[o].astype(BF16)
            qk_c, v_c, o_c, q_d, k_d, v_d, misc = _proj(xp, w_in, widths, tm_p)
            y_c, c_c, c_n, c_m = _mlstm(qk_c, v_c, o_c, misc, jnp.zeros((bp, N_HEADS, DK_C, DV_C), F32),
                                        jnp.zeros((bp, N_HEADS, DK_C), F32), jnp.zeros((bp, N_HEADS), F32),
                                        c_i_bias[o], c_f_bias[o], c_norm_w[o], nb=bp, l=CHUNK, i_col=i_col, f_col=f_col)
            k3 = k_d.reshape(bp, sp, w_a)
            v3 = v_d.reshape(bp, sp, w_a)
            pad = ((0, 0), (BAND_CHUNKS * CHUNK, 0), (0, 0))
            bias_bp = band_bias(d_rel_bias[o], -BAND_CHUNKS * CHUNK, CHUNK, lw)
            o_d = _band_prompt(q_d, jnp.pad(k3, pad), jnp.pad(v3, pad), bias_bp, nb=bp, tq=CHUNK)
            xp = _mm_res_ln([y_c, o_d], [w_out[:w_c], w_out[w_c:]], xp, ln_mix_g[layer], ln_mix_b[layer], alpha, tm_p)
            d_win_p = min(BAND_CHUNKS * CHUNK, sp)
            outs["cc_p"].append(c_c)
            outs["cn_p"].append(c_n)
            outs["cm_p"].append(c_m)
            outs["dk_p"].append(k3[:, sp - d_win_p:].reshape(bp, d_win_p, N_HEADS, HEAD_DIM))
            outs["dv_p"].append(v3[:, sp - d_win_p:].reshape(bp, d_win_p, N_HEADS, HEAD_DIM))
            qk_c, v_c, o_c, q_d, k_d, v_d, misc = _proj(xs, w_in, widths, tm_s)
            y_c, c_c, c_n, c_m = _mlstm(qk_c, v_c, o_c, misc, state_c_c[o], state_c_n[o], state_c_m[o],
                                        c_i_bias[o], c_f_bias[o], c_norm_w[o], nb=bs, l=ts, i_col=i_col, f_col=f_col)
            o_d = _band_sample(q_d, cache_d_k[o].reshape(bs, d_win, w_a), cache_d_v[o].reshape(bs, d_win, w_a),
                               k_d.reshape(bs, ts, w_a), v_d.reshape(bs, ts, w_a),
                               band_bias(d_rel_bias[o], -d_win, ts, d_win, band_valid(pos_kc)),
                               band_bias(d_rel_bias[o], 0, ts, ts, band_valid(pos_q)), nb=bs, tq=ts)
            xs = _mm_res_ln([y_c, o_d], [w_out[:w_c], w_out[w_c:]], xs, ln_mix_g[layer], ln_mix_b[layer], alpha, tm_s)
            outs["cc_s"].append(c_c)
            outs["cn_s"].append(c_n)
            outs["cm_s"].append(c_m)
            outs["dk_s"].append(k_d.reshape(bs, ts, N_HEADS, HEAD_DIM))
            outs["dv_s"].append(v_d.reshape(bs, ts, N_HEADS, HEAD_DIM))
        w_up = ffn_w_up[layer].astype(BF16)
        w_down = ffn_w_down[layer].astype(BF16)
        act, hist_p = _ffn_up(xp, w_up, ffn_conv_w[layer], jnp.zeros((bp, CONV_FF - 1, 2 * dff), F32),
                              1, tff_p, sp // tff_p)
        xp = _mm_res_ln([act], [w_down], xp, ln_ffn_g[layer], ln_ffn_b[layer], alpha, tm_p)
        act, hist_s = _ffn_up(xs, w_up, ffn_conv_w[layer], state_ffn_conv[layer], ns_s, ts, 1)
        xs = _mm_res_ln([act], [w_down], xs, ln_ffn_g[layer], ln_ffn_b[layer], alpha, tm_s)
        outs["fc_p"].append(hist_p)
        outs["fc_s"].append(hist_s)

    st = lambda k: jnp.stack(outs[k])
    return (xp.reshape(bp, sp, d), xs.reshape(bs, ts, d),
            st("ak_p"), st("ak_s"), st("av_p"), st("av_s"), st("aki_p"), st("aki_s"),
            st("bs_p"), st("bs_s"), st("bc_p"), st("bc_s"),
            st("cc_p"), st("cc_s"), st("cn_p"), st("cn_s"), st("cm_p"), st("cm_s"),
            st("dk_p"), st("dk_s"), st("dv_p"), st("dv_s"),
            st("fc_p"), st("fc_s"))
```

```python
import functools
import math

import numpy as np
import jax
import jax.numpy as jnp
from jax import lax
from jax.experimental import pallas as pl
from jax.experimental.pallas import tpu as pltpu

F32 = jnp.float32
BF16 = jnp.bfloat16
HI = lax.Precision.HIGHEST

CHUNK = 64
HEAD_DIM = 64
N_HEADS = 8
N_IDX_HEADS = 8
D_IDX = 64
TOPK_MAX = 256
T5_BUCKETS = 32
T5_MAX_DIST = 128
DK_B = 64
DV_B = 64
CONV_B = 4
DK_C = 32
DV_C = 64
BAND_CHUNKS = 8
REL_CLIP = 128
CONV_FF = 3
NEAR_CHUNKS = 3
T5_FAR = 128
SEQS_PER_STEP = 1

LANES = 128
SUBLANES = 8
VMEM_LIMIT = 56 * 1024 * 1024

NEG = -1e30
INT_MIN = -2 ** 31


def _params(*sem):
    return pltpu.CompilerParams(dimension_semantics=sem, vmem_limit_bytes=VMEM_LIMIT)


def _dot(a, b):
    return jnp.dot(a.astype(BF16), b.astype(BF16), preferred_element_type=F32)


def _dot_nt(a, b, precision=None):
    return lax.dot_general(a, b, (((1,), (1,)), ((), ())), precision=precision, preferred_element_type=F32)


def _dot_tn(a, b, precision=None):
    return lax.dot_general(a, b, (((0,), (0,)), ((), ())), precision=precision, preferred_element_type=F32)


def _dot_hi(a, b):
    return jnp.dot(a, b, precision=HI, preferred_element_type=F32)


def _split(a):
    hi = a.astype(BF16)
    return hi, (a - hi.astype(F32)).astype(BF16)


def _dot3(a, b):
    a_hi, a_lo = a
    b_hi, b_lo = b
    d = functools.partial(jnp.dot, preferred_element_type=F32)
    return d(a_hi, b_hi) + (d(a_hi, b_lo) + d(a_lo, b_hi))


def _sigmoid(x):
    return 1.0 / (1.0 + jnp.exp(-x))


def _softplus(x):
    return jnp.maximum(x, 0.0) + jnp.log(1.0 + jnp.exp(-jnp.abs(x)))


def _iota(shape, dim):
    return lax.broadcasted_iota(jnp.int32, shape, dim)


def _proj_kernel(x_ref, w_ref, *out_refs, sizes):
    xb = x_ref[...].astype(BF16)
    off = 0
    for o_ref, size in zip(out_refs, sizes):
        o_ref[...] = jnp.dot(xb, w_ref[:, off:off + size], preferred_element_type=F32)
        off += size


def _proj(x2d, w, sizes, tm):
    n, d = x2d.shape
    return pl.pallas_call(
        functools.partial(_proj_kernel, sizes=sizes),
        grid=(n // tm,),
        in_specs=[pl.BlockSpec((tm, d), lambda i: (i, 0)),
                  pl.BlockSpec((d, sum(sizes)), lambda i: (0, 0))],
        out_specs=[pl.BlockSpec((tm, s), lambda i: (i, 0)) for s in sizes],
        out_shape=[jax.ShapeDtypeStruct((n, s), F32) for s in sizes],
        compiler_params=_params("parallel"),
        name="in_proj",
    )(x2d, w)


def _mm_res_ln_kernel(*refs, nparts, alpha):
    part_refs = refs[:nparts]
    w_refs = refs[nparts:2 * nparts]
    x_ref, g_ref, b_ref, o_ref = refs[2 * nparts:]
    acc = alpha * x_ref[...]
    for p_ref, w_ref in zip(part_refs, w_refs):
        acc = acc + jnp.dot(p_ref[...].astype(BF16), w_ref[...], preferred_element_type=F32)
    mu = jnp.mean(acc, axis=-1, keepdims=True)
    cen = acc - mu
    var = jnp.mean(cen * cen, axis=-1, keepdims=True)
    o_ref[...] = cen * lax.rsqrt(var + 1e-5) * g_ref[...] + b_ref[...]


def _mm_res_ln(parts, ws, x2d, g, b, alpha, tm):
    n, d = x2d.shape
    nparts = len(parts)
    in_specs = ([pl.BlockSpec((tm, p.shape[1]), lambda i: (i, 0)) for p in parts]
                + [pl.BlockSpec(w.shape, lambda i: (0, 0)) for w in ws]
                + [pl.BlockSpec((tm, d), lambda i: (i, 0)),
                   pl.BlockSpec((1, d), lambda i: (0, 0)),
                   pl.BlockSpec((1, d), lambda i: (0, 0))])
    return pl.pallas_call(
        functools.partial(_mm_res_ln_kernel, nparts=nparts, alpha=alpha),
        grid=(n // tm,),
        in_specs=in_specs,
        out_specs=pl.BlockSpec((tm, d), lambda i: (i, 0)),
        out_shape=jax.ShapeDtypeStruct((n, d), F32),
        compiler_params=_params("parallel"),
        name="out_proj_ln",
    )(*parts, *ws, x2d, g.reshape(1, d), b.reshape(1, d))


def _ffn_up_kernel(x_ref, w_ref, cw_ref, hist_ref, act_ref, newhist_ref, ext_ref, *,
                   ns, tt, tiles_per_seq, dff, cc):
    i = pl.program_id(0)
    tm = ns * tt
    hw = CONV_FF - 1
    base = SUBLANES
    if tiles_per_seq == 1:
        ext_ref[:, base - hw:base, :] = hist_ref[...]
    else:
        @pl.when(i % tiles_per_seq == 0)
        def _():
            ext_ref[:, base - hw:base, :] = hist_ref[...]

        @pl.when(i % tiles_per_seq != 0)
        def _():
            ext_ref[:, base - hw:base, :] = ext_ref[:, base + tt - hw:base + tt, :]
    xb = x_ref[...].astype(BF16)
    for j in range(2 * dff // cc):
        cols = slice(j * cc, (j + 1) * cc)
        h = jnp.dot(xb, w_ref[:, cols], preferred_element_type=F32)
        ext_ref[:, base:base + tt, cols] = h.reshape(ns, tt, cc)
    newhist_ref[...] = ext_ref[:, base + tt - hw:base + tt, :]

    def conv(cols):
        acc = None
        for k in range(CONV_FF):
            term = ext_ref[:, base - hw + k:base - hw + k + tt, cols] * cw_ref[k:k + 1, cols]
            acc = term if acc is None else acc + term
        return acc

    for j in range(dff // cc):
        g = conv(slice(j * cc, (j + 1) * cc))
        u = conv(slice(dff + j * cc, dff + (j + 1) * cc))
        act = g * _sigmoid(g) * u
        act_ref[:, j * cc:(j + 1) * cc] = act.reshape(tm, cc).astype(BF16)


def _ffn_up(x2d, w_up, conv_w, hist, ns, tt, tiles_per_seq):
    n, d = x2d.shape
    c2 = w_up.shape[1]
    dff = c2 // 2
    tm = ns * tt
    cc = 256
    hw = CONV_FF - 1
    if tiles_per_seq == 1:
        hist_map = lambda i: (i, 0, 0)
    else:
        hist_map = lambda i: (i // tiles_per_seq, 0, 0)
    return pl.pallas_call(
        functools.partial(_ffn_up_kernel, ns=ns, tt=tt, tiles_per_seq=tiles_per_seq, dff=dff, cc=cc),
        grid=(n // tm,),
        in_specs=[pl.BlockSpec((tm, d), lambda i: (i, 0)),
                  pl.BlockSpec((d, c2), lambda i: (0, 0)),
                  pl.BlockSpec((CONV_FF, c2), lambda i: (0, 0)),
                  pl.BlockSpec((ns, hw, c2), hist_map)],
        out_specs=[pl.BlockSpec((tm, dff), lambda i: (i, 0)),
                   pl.BlockSpec((ns, hw, c2), hist_map)],
        out_shape=[jax.ShapeDtypeStruct((n, dff), BF16),
                   jax.ShapeDtypeStruct(hist.shape, F32)],
        scratch_shapes=[pltpu.VMEM((ns, SUBLANES + tt, c2), F32)],
        compiler_params=_params("arbitrary"),
        name="ffn_up_conv_gate",
    )(x2d, w_up, conv_w, hist)


def _sortable(x):
    b = lax.bitcast_convert_type(x, jnp.int32)
    return b ^ ((b >> 31) & jnp.int32(0x7FFFFFFF))


def _count(mask):
    return jnp.sum(jnp.where(mask, 1.0, 0.0), axis=-1, keepdims=True)


def _dsa_sample_kernel(qa_ref, qi_ref, qm_ref, kf_ref, vf_ref, kif_ref, kn_ref, vn_ref, kin_ref, nbias_ref,
                       fbias_ref, o_ref, self_ref, seln_ref, *, topk, tq, lf, ln, wi_col):
    v = 0
    start = lf - T5_FAR
    load_kn = lambda hs: kn_ref[0, :, hs]
    load_vn = lambda hs: vn_ref[0, :, hs]
    kin = kin_ref[0][:, :D_IDX]
    qa = qa_ref[...]
    qi = qi_ref[...]
    wi = qm_ref[:, wi_col:wi_col + N_IDX_HEADS] * (N_IDX_HEADS ** -0.5) * (D_IDX ** -0.5)
    kif = kif_ref[0][:, :D_IDX].astype(BF16)
    kinb = kin.astype(BF16)

    sc_f = jnp.zeros((tq, lf), F32)
    sc_n = jnp.zeros((tq, ln), F32)
    for n in range(N_IDX_HEADS):
        qn = qi[:, n * D_IDX:(n + 1) * D_IDX].astype(BF16)
        wn = wi[:, n:n + 1]
        sc_f = sc_f + jnp.maximum(_dot_nt(qn, kif), 0.0) * wn
        sc_n = sc_n + jnp.maximum(_dot_nt(qn, kinb), 0.0) * wn
    adm_f = _iota((tq, lf), 1) < start
    adm_n = nbias_ref[v, 0] > 0.5 * NEG
    key_f = jnp.where(adm_f, _sortable(sc_f), jnp.int32(INT_MIN))
    key_n = jnp.where(adm_n, _sortable(sc_n), jnp.int32(INT_MIN))

    kf32 = float(topk)

    def body(i, t_u):
        cand_u = t_u | lax.shift_left(jnp.int32(1), 31 - i)
        cand_s = cand_u ^ jnp.int32(INT_MIN)
        cnt = _count(key_f >= cand_s) + _count(key_n >= cand_s)
        return jnp.where(cnt >= kf32, cand_u, t_u)

    t_u = lax.fori_loop(0, 32, body, jnp.zeros((tq, 1), jnp.int32))
    thr = t_u ^ jnp.int32(INT_MIN)

    n_gt = _count(key_f > thr) + _count(key_n > thr)
    n_eq = _count(key_f == thr) + _count(key_n == thr)
    need = kf32 - n_gt
    open_row = thr == jnp.int32(INT_MIN)
    conflict = jnp.logical_and(n_eq != need, jnp.logical_not(open_row))
    self_ref[...] = jnp.where(jnp.logical_and(key_f >= thr, adm_f), 0.0, NEG)
    seln_ref[...] = jnp.where(jnp.logical_and(key_n >= thr, adm_n), 0.0, NEG)

    @pl.when(jnp.max(jnp.where(conflict, 1.0, 0.0)) > 0.0)
    def _():
        upper = jnp.where(_iota((LANES, LANES), 0) < _iota((LANES, LANES), 1), 1.0, 0.0).astype(BF16)
        offset = jnp.zeros((tq, 1), F32)
        for ref, key, width in ((self_ref, key_f, lf), (seln_ref, key_n, ln)):
            for j0 in range(0, width, LANES):
                w = min(LANES, width - j0)
                kb = key[:, j0:j0 + w]
                e = jnp.where(kb == thr, 1.0, 0.0)
                rank = offset + jnp.dot(e.astype(BF16), upper[:w, :w], preferred_element_type=F32)
                take = jnp.where(kb > thr, 1.0, jnp.where(rank < need, e, 0.0))
                take = jnp.where(open_row, jnp.where(kb > thr, 1.0, 0.0), take)
                ref[:, j0:j0 + w] = jnp.where(take > 0.5, 0.0, NEG)
                offset = offset + jnp.sum(e, axis=-1, keepdims=True)

    sel_f = self_ref[...]
    sel_n = seln_ref[...]
    for h in range(N_HEADS):
        hs = slice(h * HEAD_DIM, (h + 1) * HEAD_DIM)
        qh = qa[:, hs].astype(BF16)
        s_f = _dot_nt(qh, kf_ref[0, :, hs].astype(BF16)) * (HEAD_DIM ** -0.5) + fbias_ref[:, h:h + 1] + sel_f
        s_n = _dot_nt(qh, load_kn(hs).astype(BF16)) * (HEAD_DIM ** -0.5) + nbias_ref[v, h] + sel_n
        m = jnp.maximum(jnp.max(s_f, axis=-1, keepdims=True), jnp.max(s_n, axis=-1, keepdims=True))
        p_f = jnp.exp(s_f - m)
        p_n = jnp.exp(s_n - m)
        den = jnp.sum(p_f, axis=-1, keepdims=True) + jnp.sum(p_n, axis=-1, keepdims=True)
        o = _dot(p_f, vf_ref[0, :, hs]) + _dot(p_n, load_vn(hs))
        o_ref[:, hs] = o / den


def _dsa_sample(qa, qi, qmisc, kf, vf, kif, near, nbias, fbias, *, nb, tq, topk, wi_col):
    lf = kf.shape[1]
    ln = nbias.shape[-1]
    row = lambda b: (b, 0)
    per_b = lambda b: (b, 0, 0)
    args = [qa, qi, qmisc, kf, vf, kif, *near]
    in_specs = ([pl.BlockSpec((tq, a.shape[1]), row) for a in args[:3]]
                + [pl.BlockSpec((1,) + a.shape[1:], per_b) for a in args[3:]]
                + [pl.BlockSpec(nbias.shape, lambda b: (0, 0, 0, 0)),
                   pl.BlockSpec(fbias.shape, lambda b: (0, 0))])
    return pl.pallas_call(
        functools.partial(_dsa_sample_kernel, topk=topk, tq=tq, lf=lf, ln=ln, wi_col=wi_col),
        grid=(nb,),
        in_specs=in_specs,
        out_specs=pl.BlockSpec((tq, qa.shape[1]), row),
        out_shape=jax.ShapeDtypeStruct(qa.shape, F32),
        scratch_shapes=[pltpu.VMEM((tq, lf), F32), pltpu.VMEM((tq, ln), F32)],
        compiler_params=_params("parallel"),
        name="dsa_attention_sample",
    )(*args, nbias, fbias)


def _reduce_rows(x, op, final):
    blk = 8 * SUBLANES
    parts = [x[r0:r0 + blk] for r0 in range(0, x.shape[0], blk)]
    while len(parts) > 1:
        parts = [op(parts[i], parts[i + 1]) for i in range(0, len(parts) - 1, 2)] + parts[len(parts) & ~1:]
    return final(parts[0], axis=0, keepdims=True)


def _dsa_prompt_tile(g, lk, qa_ref, qi_ref, qm_ref, corr_ref, o_ref,
                     kb_ref, vt_ref, kib_ref, key_ref, sel_ref, s_ref, ot_ref, *, topk, tq, wi_col):
    rb = 2 * LANES if lk % (2 * LANES) == 0 else LANES
    nw = corr_ref.shape[2]
    qa = (qa_ref[...] * (HEAD_DIM ** -0.5)).astype(BF16)
    qi = qi_ref[...].astype(BF16)
    wi = qm_ref[:, wi_col:wi_col + N_IDX_HEADS] * (N_IDX_HEADS ** -0.5) * (D_IDX ** -0.5)
    eye_h = jnp.where(_iota((N_IDX_HEADS, N_IDX_HEADS), 0) == _iota((N_IDX_HEADS, N_IDX_HEADS), 1), 1.0, 0.0)
    wi_t = _dot_nt(eye_h, wi, HI)
    q_chunk = (g * tq + _iota((1, tq), 1)) // CHUNK

    for r0 in range(0, lk, rb):
        kib = kib_ref[r0:r0 + rb, :D_IDX]
        acc = jnp.zeros((rb, tq), F32)
        for n in range(N_IDX_HEADS):
            acc = acc + jnp.maximum(_dot_nt(kib, qi[:, n * D_IDX:(n + 1) * D_IDX]), 0.0) * wi_t[n:n + 1, :]
        adm = (r0 + _iota((rb, 1), 0)) // CHUNK <= q_chunk
        key_ref[r0:r0 + rb, :] = jnp.where(adm, _sortable(acc), jnp.int32(INT_MIN))

    kf32 = float(topk)
    count = lambda mask: _reduce_rows(jnp.where(mask, 1.0, 0.0), jnp.add, jnp.sum)

    def body(i, t_u):
        cand_u = t_u | lax.shift_left(jnp.int32(1), 31 - i)
        cand_s = cand_u ^ jnp.int32(INT_MIN)
        return jnp.where(count(key_ref[0:lk, :] >= cand_s) >= kf32, cand_u, t_u)

    t_u = lax.fori_loop(0, 32, body, jnp.zeros((1, tq), jnp.int32))
    thr = t_u ^ jnp.int32(INT_MIN)
    keys = key_ref[0:lk, :]
    need = kf32 - count(keys > thr)
    open_row = thr == jnp.int32(INT_MIN)
    conflict = jnp.logical_and(count(keys == thr) != need, jnp.logical_not(open_row))
    sel_ref[0:lk, :] = jnp.where(keys >= jnp.maximum(thr, jnp.int32(INT_MIN + 1)), 0.0, NEG)

    @pl.when(jnp.max(jnp.where(conflict, 1.0, 0.0)) > 0.0)
    def _():
        below = jnp.where(_iota((LANES, LANES), 1) < _iota((LANES, LANES), 0), 1.0, 0.0).astype(BF16)
        offset = jnp.zeros((1, tq), F32)
        for r0 in range(0, lk, LANES):
            kblk = key_ref[r0:r0 + LANES, :]
            e = jnp.where(kblk == thr, 1.0, 0.0)
            rank = offset + jnp.dot(below, e.astype(BF16), preferred_element_type=F32)
            take = jnp.where(kblk > thr, 1.0, jnp.where(jnp.logical_or(rank >= need, open_row), 0.0, e))
            sel_ref[r0:r0 + LANES, :] = jnp.where(take > 0.5, 0.0, NEG)
            offset = offset + jnp.sum(e, axis=0, keepdims=True)

    v = jnp.minimum(g, 1)
    start = pl.multiple_of(jnp.maximum(g * tq - T5_FAR, 0), LANES)
    for h in range(N_HEADS):
        hs = slice(h * HEAD_DIM, (h + 1) * HEAD_DIM)
        s_ref[0:lk, :] = _dot_nt(kb_ref[0:lk, hs], qa[:, hs]) + sel_ref[0:lk, :]
        s_ref[pl.ds(start, nw), :] += corr_ref[v, h]
        s = s_ref[0:lk, :]
        p = jnp.exp(s - _reduce_rows(s, jnp.maximum, jnp.max))
        pn = (p * (1.0 / _reduce_rows(p, jnp.add, jnp.sum))).astype(BF16)
        ot_ref[hs, :] = jnp.dot(vt_ref[hs, 0:lk], pn, preferred_element_type=F32)
    o_ref[...] = ot_ref[...].T


def _dsa_prompt_kernel(qa_ref, qi_ref, qm_ref, k_ref, v_ref, ki_ref, corr_ref, o_ref,
                       kb_ref, vt_ref, kib_ref, key_ref, sel_ref, s_ref, ot_ref, *, topk, tq, seq, n_groups, wi_col):
    g = pl.program_id(1)
    blk = _row_tile(seq, 4 * LANES)

    @pl.when(g == 0)
    def _():
        for r0 in range(0, seq, blk):
            kb_ref[r0:r0 + blk, :] = k_ref[0, r0:r0 + blk, :].astype(BF16)
            vt_ref[:, r0:r0 + blk] = v_ref[0, r0:r0 + blk, :].T.astype(BF16)
            kib_ref[r0:r0 + blk, :] = ki_ref[0, r0:r0 + blk, :].astype(BF16)

    per_group = (seq // tq) // n_groups
    for grp in range(n_groups):
        @pl.when(g // per_group == grp)
        def _():
            _dsa_prompt_tile(g, (grp + 1) * per_group * tq, qa_ref, qi_ref, qm_ref, corr_ref, o_ref,
                             kb_ref, vt_ref, kib_ref, key_ref, sel_ref, s_ref, ot_ref,
                             topk=topk, tq=tq, wi_col=wi_col)


def _dsa_prompt(qa, qi, qmisc, k, v, ki, corr, *, nb, tq, topk, wi_col, n_groups):
    n, w = qa.shape
    seq = k.shape[1]
    nq = seq // tq
    row = lambda b, g: (b * nq + g, 0)
    per_b = lambda b, g: (b, 0, 0)
    return pl.pallas_call(
        functools.partial(_dsa_prompt_kernel, topk=topk, tq=tq, seq=seq, n_groups=n_groups, wi_col=wi_col),
        grid=(nb, nq),
        in_specs=[pl.BlockSpec((tq, w), row),
                  pl.BlockSpec((tq, qi.shape[1]), row),
                  pl.BlockSpec((tq, qmisc.shape[1]), row),
                  pl.BlockSpec((1, seq, w), per_b),
                  pl.BlockSpec((1, seq, w), per_b),
                  pl.BlockSpec((1, seq, ki.shape[2]), per_b),
                  pl.BlockSpec(corr.shape, lambda b, g: (0, 0, 0, 0))],
        out_specs=pl.BlockSpec((tq, w), row),
        out_shape=jax.ShapeDtypeStruct(qa.shape, F32),
        scratch_shapes=[pltpu.VMEM((seq, w), BF16),
                        pltpu.VMEM((w, seq), BF16),
                        pltpu.VMEM((seq, ki.shape[2]), BF16),
                        pltpu.VMEM((seq, tq), jnp.int32),
                        pltpu.VMEM((seq, tq), F32),
                        pltpu.VMEM((seq, tq), F32),
                        pltpu.VMEM((w, tq), F32)],
        compiler_params=_params("parallel", "arbitrary"),
        name="dsa_attention_prompt",
    )(qa, qi, qmisc, k, v, ki, corr)


def _unit_lower_inverse(mats, n):
    eye = jnp.where(_iota((n, n), 0) == _iota((n, n), 1), 1.0, 0.0)
    ps = [eye - a for a in mats]
    aks = [_split(a) for a in mats]
    k = 1
    while 2 * k < n:
        aks = [_split(_dot3(ak, ak)) for ak in aks]
        ps = [p + _dot3(_split(p), ak) for p, ak in zip(ps, aks)]
        k *= 2
    return ps


def _head_rms(x, w):
    return x * lax.rsqrt(jnp.mean(x * x, axis=-1, keepdims=True) + 1e-6) * w


def _gdn_kernel(qkv_ref, misc_ref, z_ref, hist_ref, s0_ref, cw_ref, alog_ref, dtb_ref, nw_ref,
                y_ref, sfin_ref, newhist_ref, ext_ref, s_ref, *, c, a_col, b_col):
    j = pl.program_id(1)
    hw = CONV_B - 1
    base = SUBLANES
    wq = N_HEADS * DK_B

    @pl.when(j == 0)
    def _():
        ext_ref[base - hw:base, :] = hist_ref[0]
        s_ref[...] = s0_ref[0]

    @pl.when(j > 0)
    def _():
        ext_ref[base - hw:base, :] = ext_ref[base + c - hw:base + c, :]

    ext_ref[base:base + c, :] = qkv_ref[...]
    newhist_ref[0] = ext_ref[base + c - hw:base + c, :]
    conv = None
    for k in range(CONV_B):
        term = ext_ref[base - hw + k:base - hw + k + c, :] * cw_ref[k:k + 1, :]
        conv = term if conv is None else conv + term
    act = conv * _sigmoid(conv)

    beta = _sigmoid(misc_ref[:, b_col:b_col + N_HEADS])
    g = -jnp.exp(alog_ref[...]) * _softplus(misc_ref[:, a_col:a_col + N_HEADS] + dtb_ref[...])
    ri = _iota((c, c), 0)
    ci = _iota((c, c), 1)
    lower = ri >= ci
    strict = ri > ci
    gc = _dot_hi(jnp.where(lower, 1.0, 0.0), g)
    eye_h = jnp.where(_iota((N_HEADS, N_HEADS), 0) == _iota((N_HEADS, N_HEADS), 1), 1.0, 0.0)
    gc_t = _dot_nt(eye_h, gc, HI)
    eg = jnp.exp(gc)
    g_last = gc[c - 1:c, :]
    e_last = jnp.exp(g_last)
    e_rest = jnp.exp(g_last - gc)
    z = z_ref[...]
    heads = range(N_HEADS)
    col = lambda x, h: x[:, h:h + 1]
    qs = [act[:, h * DK_B:(h + 1) * DK_B] for h in heads]
    ks = [act[:, wq + h * DK_B:wq + (h + 1) * DK_B] for h in heads]
    vs = [act[:, 2 * wq + h * DV_B:2 * wq + (h + 1) * DV_B] for h in heads]
    qs = [q * lax.rsqrt(jnp.sum(q * q, axis=-1, keepdims=True) + 1e-6) * (DK_B ** -0.5) for q in qs]
    ks = [k * lax.rsqrt(jnp.sum(k * k, axis=-1, keepdims=True) + 1e-6) for k in ks]
    kbs = [k.astype(BF16) for k in ks]
    decay = [jnp.where(lower, jnp.exp(jnp.where(lower, col(gc, h) - gc_t[h:h + 1, :], 0.0)), 0.0) for h in heads]
    kk = [_dot_nt(kbs[h], kbs[h]) for h in heads]
    attn = [_dot_nt(qs[h].astype(BF16), kbs[h]) * decay[h] for h in heads]
    a_mat = [jnp.where(strict, col(beta, h) * kk[h] * decay[h], 0.0) for h in heads]
    t_mat = [t.astype(BF16) for t in _unit_lower_inverse(a_mat, c)]
    value = [_dot(t_mat[h], vs[h] * col(beta, h)) for h in heads]
    k_cum = [_dot(t_mat[h], ks[h] * (col(beta, h) * col(eg, h))) for h in heads]
    s_old = [s_ref[h] for h in heads]
    sbs = [s.astype(BF16) for s in s_old]
    v_new = [(value[h] - _dot(k_cum[h], sbs[h])).astype(BF16) for h in heads]
    o_inter = [_dot(qs[h] * col(eg, h), sbs[h]) for h in heads]
    o = [o_inter[h] + _dot(attn[h], v_new[h]) for h in heads]
    for h in heads:
        s_ref[h] = s_old[h] * col(e_last, h) + _dot_tn((ks[h] * col(e_rest, h)).astype(BF16), v_new[h])
    for h in heads:
        zh = z[:, h * DV_B:(h + 1) * DV_B]
        y_ref[:, h * DV_B:(h + 1) * DV_B] = _head_rms(o[h], nw_ref[...]) * (zh * _sigmoid(zh))
    sfin_ref[0] = s_ref[...]


def _gdn(qkv, misc, z, hist, s0, conv_w, a_log, dt_bias, norm_w, *, nb, c, a_col, b_col):
    n, wqkv = qkv.shape
    nch = n // (nb * c)
    row = lambda b, j: (b * nch + j, 0)
    per_b3 = lambda b, j: (b, 0, 0)
    per_b4 = lambda b, j: (b, 0, 0, 0)
    const2 = lambda b, j: (0, 0)
    hw = CONV_B - 1
    return pl.pallas_call(
        functools.partial(_gdn_kernel, c=c, a_col=a_col, b_col=b_col),
        grid=(nb, nch),
        in_specs=[pl.BlockSpec((c, wqkv), row),
                  pl.BlockSpec((c, misc.shape[1]), row),
                  pl.BlockSpec((c, z.shape[1]), row),
                  pl.BlockSpec((1, hw, wqkv), per_b3),
                  pl.BlockSpec((1,) + s0.shape[1:], per_b4),
                  pl.BlockSpec((CONV_B, wqkv), const2),
                  pl.BlockSpec((1, N_HEADS), const2),
                  pl.BlockSpec((1, N_HEADS), const2),
                  pl.BlockSpec((1, DV_B), const2)],
        out_specs=[pl.BlockSpec((c, z.shape[1]), row),
                   pl.BlockSpec((1,) + s0.shape[1:], per_b4),
                   pl.BlockSpec((1, hw, wqkv), per_b3)],
        out_shape=[jax.ShapeDtypeStruct(z.shape, F32),
                   jax.ShapeDtypeStruct(s0.shape, F32),
                   jax.ShapeDtypeStruct(hist.shape, F32)],
        scratch_shapes=[pltpu.VMEM((SUBLANES + c, wqkv), F32),
                        pltpu.VMEM(s0.shape[1:], F32)],
        compiler_params=_params("parallel", "arbitrary"),
        name="gated_deltanet",
    )(qkv, misc, z, hist, s0, conv_w, a_log.reshape(1, -1), dt_bias.reshape(1, -1), norm_w.reshape(1, -1))


def _mlstm_kernel(qk_ref, v_ref, og_ref, misc_ref, c0_ref, n0_ref, m0_ref, ib_ref, fb_ref, nw_ref,
                  y_ref, cfin_ref, nfin_ref, mfin_ref, c_ref, n_ref, m_ref, *, l, nbb, i_col, f_col):
    j = pl.program_id(1)
    wq = N_HEADS * DK_C

    @pl.when(j == 0)
    def _():
        c_ref[...] = c0_ref[...]
        n_ref[...] = n0_ref[...]
        m_ref[...] = m0_ref[...]

    causal = _iota((l, l), 0) >= _iota((l, l), 1)
    tri = jnp.where(causal, 1.0, 0.0)
    eye_h = jnp.where(_iota((N_HEADS, N_HEADS), 0) == _iota((N_HEADS, N_HEADS), 1), 1.0, 0.0)
    seqs = range(nbb)
    ig = [misc_ref[b, :, i_col:i_col + N_HEADS] + ib_ref[...] for b in seqs]
    lf = [-_softplus(-(misc_ref[b, :, f_col:f_col + N_HEADS] + fb_ref[...])) for b in seqs]
    fc = [_dot_hi(tri, lf[b]) for b in seqs]
    row_terms = [_dot_nt(eye_h, ig[b] - fc[b], HI) for b in seqs]
    m_prev = [m_ref[b] for b in seqs]
    log_inter = [fc[b] + m_prev[b] for b in seqs]
    f_last = [fc[b][l - 1:l, :] for b in seqs]
    m_vec = [f_last[b] + jnp.maximum(m_prev[b], jnp.max(ig[b] - fc[b], axis=0, keepdims=True)) for b in seqs]
    dec_vec = [jnp.exp(f_last[b] + m_prev[b] - m_vec[b]) for b in seqs]
    w_last = [jnp.exp(f_last[b] - fc[b] + ig[b] - m_vec[b]) for b in seqs]

    items = [(b, h) for b in seqs for h in range(N_HEADS)]
    col = lambda x, h: x[:, h:h + 1]
    qs = [qk_ref[b, :, h * DK_C:(h + 1) * DK_C] for b, h in items]
    ks = [qk_ref[b, :, wq + h * DK_C:wq + (h + 1) * DK_C] * (DK_C ** -0.5) for b, h in items]
    vbs = [v_ref[b, :, h * DV_C:(h + 1) * DV_C].astype(BF16) for b, h in items]
    c_old = [c_ref[b, h] for b, h in items]
    n_old = [n_ref[b, h:h + 1, :] for b, h in items]
    qk_raw = [_dot_nt(q.astype(BF16), k.astype(BF16)) for q, k in zip(qs, ks)]
    q_c = [_dot(q, c) for q, c in zip(qs, c_old)]
    log_w = [jnp.where(causal, col(fc[b], h) + row_terms[b][h:h + 1, :], -jnp.inf) for b, h in items]
    li = [col(log_inter[b], h) for b, h in items]
    m_t = [jnp.maximum(a, jnp.max(lw, axis=-1, keepdims=True)) for a, lw in zip(li, log_w)]
    w_inter = [jnp.exp(a - m) for a, m in zip(li, m_t)]
    qkw = [r * jnp.exp(lw - m) for r, lw, m in zip(qk_raw, log_w, m_t)]
    num = [_dot(a, vb) + wi * qc for a, vb, wi, qc in zip(qkw, vbs, w_inter, q_c)]
    den = [jnp.sum(a, axis=-1, keepdims=True) + wi * jnp.sum(q * n, axis=-1, keepdims=True)
           for a, wi, q, n in zip(qkw, w_inter, qs, n_old)]
    hid = [nu / jnp.maximum(jnp.abs(de), jnp.exp(-m)) for nu, de, m in zip(num, den, m_t)]
    kw = [k * col(w_last[b], h) for k, (b, h) in zip(ks, items)]
    for i, (b, h) in enumerate(items):
        c_ref[b, h] = col(dec_vec[b], h) * c_old[i] + _dot_tn(kw[i].astype(BF16), vbs[i])
        n_ref[b, h:h + 1, :] = col(dec_vec[b], h) * n_old[i] + jnp.sum(kw[i], axis=0, keepdims=True)
    for i, (b, h) in enumerate(items):
        oh = og_ref[b, :, h * DV_C:(h + 1) * DV_C]
        y_ref[b, :, h * DV_C:(h + 1) * DV_C] = _head_rms(hid[i], nw_ref[...]) * _sigmoid(oh)
    for b in seqs:
        m_ref[b] = m_vec[b]
    cfin_ref[...] = c_ref[...]
    nfin_ref[...] = n_ref[...]
    mfin_ref[...] = m_ref[...]


def _mlstm(qk, v, og, misc, c0, n0, m0, i_bias, f_bias, norm_w, *, nb, l, i_col, f_col):
    n = qk.shape[0]
    seq = n // nb
    nbb = SEQS_PER_STEP if nb % SEQS_PER_STEP == 0 else 1
    tok = lambda b, j: (b, j, 0)
    per_b3 = lambda b, j: (b, 0, 0)
    per_b4 = lambda b, j: (b, 0, 0, 0)
    const2 = lambda b, j: (0, 0)
    m0 = m0.reshape(nb, 1, N_HEADS)
    tokens = [a.reshape(nb, seq, a.shape[1]) for a in (qk, v, og, misc)]
    outs = pl.pallas_call(
        functools.partial(_mlstm_kernel, l=l, nbb=nbb, i_col=i_col, f_col=f_col),
        grid=(nb // nbb, seq // l),
        in_specs=[pl.BlockSpec((nbb, l, a.shape[2]), tok) for a in tokens]
                 + [pl.BlockSpec((nbb,) + c0.shape[1:], per_b4),
                    pl.BlockSpec((nbb,) + n0.shape[1:], per_b3),
                    pl.BlockSpec((nbb, 1, N_HEADS), per_b3),
                    pl.BlockSpec((1, N_HEADS), const2),
                    pl.BlockSpec((1, N_HEADS), const2),
                    pl.BlockSpec((1, DV_C), const2)],
        out_specs=[pl.BlockSpec((nbb, l, v.shape[1]), tok),
                   pl.BlockSpec((nbb,) + c0.shape[1:], per_b4),
                   pl.BlockSpec((nbb,) + n0.shape[1:], per_b3),
                   pl.BlockSpec((nbb, 1, N_HEADS), per_b3)],
        out_shape=[jax.ShapeDtypeStruct((nb, seq, v.shape[1]), F32),
                   jax.ShapeDtypeStruct(c0.shape, F32),
                   jax.ShapeDtypeStruct(n0.shape, F32),
                   jax.ShapeDtypeStruct(m0.shape, F32)],
        scratch_shapes=[pltpu.VMEM((nbb,) + c0.shape[1:], F32),
                        pltpu.VMEM((nbb,) + n0.shape[1:], F32),
                        pltpu.VMEM((nbb, 1, N_HEADS), F32)],
        compiler_params=_params("parallel", "arbitrary"),
        name="mlstm",
    )(*tokens, c0, n0, m0, i_bias.reshape(1, -1), f_bias.reshape(1, -1), norm_w.reshape(1, -1))
    y, c_fin, n_fin, m_fin = outs
    return y.reshape(n, v.shape[1]), c_fin, n_fin, m_fin.reshape(nb, N_HEADS)


def _band_prompt_kernel(q_ref, k_ref, v_ref, bias_ref, o_ref, kb_ref, vt_ref, ot_ref, *, tq, seq):
    g = pl.program_id(1)
    w = q_ref.shape[1]
    pad = BAND_CHUNKS * CHUNK
    lw = pad + tq
    npad = pad // LANES
    per_tile = tq // LANES
    blk = _row_tile(seq, 4 * LANES)

    @pl.when(g == 0)
    def _():
        kb_ref[0:pad, :] = jnp.zeros((pad, w), BF16)
        vt_ref[0:npad] = jnp.zeros((npad, w, LANES), BF16)
        for r0 in range(0, seq, blk):
            kb_ref[pad + r0:pad + r0 + blk, :] = k_ref[0, r0:r0 + blk, :].astype(BF16)
        for j in range(seq // LANES):
            vt_ref[npad + j] = v_ref[0, j * LANES:(j + 1) * LANES, :].T.astype(BF16)

    start = pl.multiple_of(g * tq, tq)
    q = (q_ref[...] * (HEAD_DIM ** -0.5)).astype(BF16)
    before_seq = jnp.where(_iota((lw, tq), 0) >= pad - g * tq, 0.0, NEG)
    for h in range(N_HEADS):
        hs = slice(h * HEAD_DIM, (h + 1) * HEAD_DIM)
        s = _dot_nt(kb_ref[pl.ds(start, lw), hs], q[:, hs]) + (bias_ref[h] + before_seq)
        p = jnp.exp(s - _reduce_rows(s, jnp.maximum, jnp.max))
        pn = (p * (1.0 / _reduce_rows(p, jnp.add, jnp.sum))).astype(BF16)
        acc = None
        for j in range(lw // LANES):
            part = jnp.dot(vt_ref[g * per_tile + j, hs, :], pn[j * LANES:(j + 1) * LANES, :],
                           preferred_element_type=F32)
            acc = part if acc is None else acc + part
        ot_ref[hs, :] = acc
    o_ref[...] = ot_ref[...].T


def _band_prompt(q, k, v, bias, *, nb, tq):
    n, w = q.shape
    seq = k.shape[1]
    nq = seq // tq
    pad = BAND_CHUNKS * CHUNK
    row = lambda b, g: (b * nq + g, 0)
    per_b = lambda b, g: (b, 0, 0)
    return pl.pallas_call(
        functools.partial(_band_prompt_kernel, tq=tq, seq=seq),
        grid=(nb, nq),
        in_specs=[pl.BlockSpec((tq, w), row),
                  pl.BlockSpec((1, seq, w), per_b),
                  pl.BlockSpec((1, seq, w), per_b),
                  pl.BlockSpec(bias.shape, lambda b, g: (0, 0, 0))],
        out_specs=pl.BlockSpec((tq, w), row),
        out_shape=jax.ShapeDtypeStruct(q.shape, F32),
        scratch_shapes=[pltpu.VMEM((pad + seq, w), BF16),
                        pltpu.VMEM(((pad + seq) // LANES, w, LANES), BF16),
                        pltpu.VMEM((w, tq), F32)],
        compiler_params=_params("parallel", "arbitrary"),
        name="band_attention_prompt",
    )(q, k, v, bias)


def _band_sample_kernel(q_ref, kc_ref, vc_ref, kn_ref, vn_ref, biasc_ref, biasn_ref, o_ref):
    pieces = [(kc_ref, vc_ref, biasc_ref), (kn_ref, vn_ref, biasn_ref)]
    q = q_ref[...]
    for h in range(N_HEADS):
        hs = slice(h * HEAD_DIM, (h + 1) * HEAD_DIM)
        qh = q[:, hs].astype(BF16)
        scores = [_dot_nt(qh, k_ref[0, :, hs].astype(BF16)) * (HEAD_DIM ** -0.5) + b_ref[h]
                  for k_ref, _, b_ref in pieces]
        m = functools.reduce(jnp.maximum, [jnp.max(s, axis=-1, keepdims=True) for s in scores])
        ps = [jnp.exp(s - m) for s in scores]
        den = sum(jnp.sum(p, axis=-1, keepdims=True) for p in ps)
        o = sum(_dot(p, v_ref[0, :, hs]) for p, (_, v_ref, _) in zip(ps, pieces))
        o_ref[:, hs] = o / den


def _band_sample(q, kc, vc, kn, vn, bias_c, bias_n, *, nb, tq):
    n, w = q.shape
    row = lambda b: (b, 0)
    per_b = lambda b: (b, 0, 0)
    const3 = lambda b: (0, 0, 0)
    return pl.pallas_call(
        _band_sample_kernel,
        grid=(nb,),
        in_specs=[pl.BlockSpec((tq, w), row)]
                 + [pl.BlockSpec((1,) + a.shape[1:], per_b) for a in (kc, vc, kn, vn)]
                 + [pl.BlockSpec(bias_c.shape, const3), pl.BlockSpec(bias_n.shape, const3)],
        out_specs=pl.BlockSpec((tq, w), row),
        out_shape=jax.ShapeDtypeStruct(q.shape, F32),
        compiler_params=_params("parallel"),
        name="band_attention_sample",
    )(q, kc, vc, kn, vn, bias_c, bias_n)


def _t5_bucket(rel):
    nb = T5_BUCKETS // 2
    max_exact = nb // 2
    n = jnp.abs(rel)
    n_f = jnp.maximum(n, 1).astype(jnp.float32)
    large = max_exact + (jnp.log(n_f / max_exact) / math.log(T5_MAX_DIST / max_exact) * (nb - max_exact)).astype(jnp.int32)
    large = jnp.minimum(large, nb - 1)
    return jnp.where(rel > 0, nb, 0) + jnp.where(n < max_exact, n, large)


def _toeplitz_bias(fn, n_rows, n_cols):
    n = n_rows + n_cols
    m = np.arange(n)
    f = jnp.transpose(fn(np.where(m < n_cols, m, m - n))).astype(F32)
    flat = jnp.tile(f, (1, n_rows))[:, :n_rows * (n - 1)]
    return flat.reshape(f.shape[0], n_rows, n - 1)[:, :, :n_cols]


def _pack_cols(w, sizes, groups):
    offs = np.concatenate([[0], np.cumsum(sizes)])
    cols, widths = [], []
    for grp in groups:
        width = 0
        for idx in grp:
            cols.append(w[:, offs[idx]:offs[idx + 1]])
            width += sizes[idx]
        pad = (-width) % LANES
        if pad:
            cols.append(jnp.zeros((w.shape[0], pad), w.dtype))
        widths.append(width + pad)
    return jnp.concatenate(cols, axis=1).astype(BF16), tuple(widths)


def _row_tile(n, target):
    t = min(n, target)
    while n % t:
        t //= 2
    return t


def kernel(x_prompt, x_sample, cache_a_k, cache_a_v, cache_a_kidx, state_b_s, state_b_conv, state_c_c, state_c_n, state_c_m, cache_d_k, cache_d_v, state_ffn_conv, w_in_even, w_out_even, t5_bias, b_conv_w, b_a_log, b_dt_bias, b_norm_w, w_in_odd, w_out_odd, c_i_bias, c_f_bias, c_norm_w, d_rel_bias, ln_mix_g, ln_mix_b, ln_ffn_g, ln_ffn_b, ffn_w_up, ffn_conv_w, ffn_w_down):
    bp, sp, d = x_prompt.shape
    bs, ts, _ = x_sample.shape
    depth = ffn_w_up.shape[0]
    past = cache_a_k.shape[2]
    d_win = cache_d_k.shape[2]
    dff = ffn_w_down.shape[1]
    alpha = (2 * depth) ** 0.25
    w_a = N_HEADS * HEAD_DIM
    w_b = N_HEADS * DV_B
    w_c = N_HEADS * DV_C
    qkv_b_w = 2 * N_HEADS * DK_B + w_b
    even_sizes = (w_a, w_a, w_a, N_IDX_HEADS * D_IDX, D_IDX, N_IDX_HEADS, qkv_b_w, N_HEADS, N_HEADS, w_b)
    odd_sizes = (N_HEADS * DK_C, N_HEADS * DK_C, w_c, N_HEADS, N_HEADS, w_c, w_a, w_a, w_a)
    even_groups = ((0,), (1,), (2,), (3,), (6,), (9,), (4, 5, 7, 8))
    wi_col, a_col, b_col = D_IDX, D_IDX + N_IDX_HEADS, D_IDX + N_IDX_HEADS + N_HEADS
    odd_groups = ((0, 1), (2,), (5,), (6,), (7,), (8,), (3, 4))
    i_col, f_col = 0, N_HEADS

    assert sp % CHUNK == 0 and ts <= CHUNK and past % CHUNK == 0 and past >= T5_FAR
    assert (past + ts - 1) // CHUNK == past // CHUNK
    topk_p = min(TOPK_MAX, sp // 4)
    topk_s = min(TOPK_MAX, (past + ts) // 4)
    n_p, n_s = bp * sp, bs * ts
    tm_p = _row_tile(n_p, 512)
    tm_s = _row_tile(n_s, 512)
    tff_p = _row_tile(sp, 512)
    ns_s = _row_tile(bs, max(1, 256 // ts))

    t5 = lambda rel: t5_bias[_t5_bucket(jnp.asarray(rel, jnp.int32))]
    fbias = t5(np.array([-T5_FAR - 1]))
    tq_a = 2 * CHUNK
    nw_a = tq_a + T5_FAR
    assert sp % tq_a == 0 and sp >= nw_a
    nq_a = sp // tq_a
    dsa_groups = max(gr for gr in (4, 2, 1) if nq_a % gr == 0 and (nq_a // gr) * tq_a >= nw_a)
    corr_p = jnp.stack([_toeplitz_bias(lambda dd: t5(-dd - T5_FAR * v) - fbias, nw_a, tq_a) for v in range(2)])
    ln_s = T5_FAR + ts
    nbias_s = _toeplitz_bias(lambda dd: t5(dd - T5_FAR), ts, ln_s)[None]

    lw = BAND_CHUNKS * CHUNK + tq_a
    r_chunk = np.arange(lw)[:, None] // CHUNK
    q_chunk = BAND_CHUNKS + np.arange(tq_a)[None, :] // CHUNK
    band_ok = (r_chunk >= q_chunk - BAND_CHUNKS) & (r_chunk <= q_chunk)
    pos_q = past + np.arange(ts)
    pos_kc = past - d_win + np.arange(d_win)
    def band_valid(pos_k):
        kch, qch = pos_k // CHUNK, pos_q // CHUNK
        return (pos_k[None] >= 0) & (kch[None] >= qch[:, None] - BAND_CHUNKS) & (kch[None] <= qch[:, None])
    def band_bias(table, shift, n_rows, n_cols, valid=None):
        bias = _toeplitz_bias(lambda dd: table[np.clip(dd + shift, -REL_CLIP, REL_CLIP) + REL_CLIP], n_rows, n_cols)
        return bias if valid is None else jnp.where(jnp.asarray(valid)[None], bias, NEG)

    xp = x_prompt.reshape(n_p, d)
    xs = x_sample.reshape(n_s, d)
    outs = {k: [] for k in ("ak_p", "ak_s", "av_p", "av_s", "aki_p", "aki_s", "bs_p", "bs_s", "bc_p", "bc_s",
                            "cc_p", "cc_s", "cn_p", "cn_s", "cm_p", "cm_s", "dk_p", "dk_s", "dv_p", "dv_s",
                            "fc_p", "fc_s")}
    for layer in range(depth):
        if layer % 2 == 0:
            e = layer // 2
            w_in, widths = _pack_cols(w_in_even[e], even_sizes, even_groups)
            w_out = w_out_even[e].astype(BF16)
            qa, ka, va, qi, qkv_b, z_b, misc = _proj(xp, w_in, widths, tm_p)
            o_a = _dsa_prompt(qa, qi, misc, ka.reshape(bp, sp, w_a), va.reshape(bp, sp, w_a),
                              misc.reshape(bp, sp, LANES), corr_p, nb=bp, tq=tq_a, topk=topk_p, wi_col=wi_col,
                              n_groups=dsa_groups)
            y_b, s_b, h_b = _gdn(qkv_b, misc, z_b, jnp.zeros((bp, CONV_B - 1, qkv_b_w), F32),
                                 jnp.zeros((bp, N_HEADS, DK_B, DV_B), F32), b_conv_w[e], b_a_log[e], b_dt_bias[e],
                                 b_norm_w[e], nb=bp, c=CHUNK, a_col=a_col, b_col=b_col)
            xp = _mm_res_ln([o_a, y_b], [w_out[:w_a], w_out[w_a:]], xp, ln_mix_g[layer], ln_mix_b[layer], alpha, tm_p)
            outs["ak_p"].append(ka.reshape(bp, sp, N_HEADS, HEAD_DIM))
            outs["av_p"].append(va.reshape(bp, sp, N_HEADS, HEAD_DIM))
            outs["aki_p"].append(misc[:, :D_IDX].reshape(bp, sp, D_IDX))
            outs["bs_p"].append(s_b)
            outs["bc_p"].append(h_b)
            qa, ka, va, qi, qkv_b, z_b, misc = _proj(xs, w_in, widths, tm_s)
            ki = misc[:, :D_IDX]
            ck = cache_a_k[e].reshape(bs, past, w_a)
            cv = cache_a_v[e].reshape(bs, past, w_a)
            cki = cache_a_kidx[e]
            near = (jnp.concatenate([ck[:, past - T5_FAR:], ka.reshape(bs, ts, w_a)], axis=1),
                    jnp.concatenate([cv[:, past - T5_FAR:], va.reshape(bs, ts, w_a)], axis=1),
                    jnp.concatenate([cki[:, past - T5_FAR:], ki.reshape(bs, ts, D_IDX)], axis=1))
            o_a = _dsa_sample(qa, qi, misc, ck, cv, cki, near, nbias_s, fbias, nb=bs, tq=ts, topk=topk_s,
                              wi_col=wi_col)
            y_b, s_b, h_b = _gdn(qkv_b, misc, z_b, state_b_conv[e], state_b_s[e], b_conv_w[e], b_a_log[e],
                                 b_dt_bias[e], b_norm_w[e], nb=bs, c=ts, a_col=a_col, b_col=b_col)
            xs = _mm_res_ln([o_a, y_b], [w_out[:w_a], w_out[w_a:]], xs, ln_mix_g[layer], ln_mix_b[layer], alpha, tm_s)
            outs["ak_s"].append(ka.reshape(bs, ts, N_HEADS, HEAD_DIM))
            outs["av_s"].append(va.reshape(bs, ts, N_HEADS, HEAD_DIM))
            outs["aki_s"].append(ki.reshape(bs, ts, D_IDX))
            outs["bs_s"].append(s_b)
            outs["bc_s"].append(h_b)
        else:
            o = layer // 2
            w_in, widths = _pack_cols(w_in_odd[o], odd_sizes, odd_groups)
            w_out = w_out_odd[o].astype(BF16)
            qk_c, v_c, o_c, q_d, k_d, v_d, misc = _proj(xp, w_in, widths, tm_p)
            y_c, c_c, c_n, c_m = _mlstm(qk_c, v_c, o_c, misc, jnp.zeros((bp, N_HEADS, DK_C, DV_C), F32),
                                        jnp.zeros((bp, N_HEADS, DK_C), F32), jnp.zeros((bp, N_HEADS), F32),
                                        c_i_bias[o], c_f_bias[o], c_norm_w[o], nb=bp, l=CHUNK, i_col=i_col, f_col=f_col)
            k3 = k_d.reshape(bp, sp, w_a)
            v3 = v_d.reshape(bp, sp, w_a)
            bias_bp = _toeplitz_bias(
                lambda dd: d_rel_bias[o][np.clip(-dd - BAND_CHUNKS * CHUNK, -REL_CLIP, REL_CLIP) + REL_CLIP], lw, tq_a)
            bias_bp = jnp.where(jnp.asarray(band_ok)[None], bias_bp, NEG)
            o_d = _band_prompt(q_d, k3, v3, bias_bp, nb=bp, tq=tq_a)
            xp = _mm_res_ln([y_c, o_d], [w_out[:w_c], w_out[w_c:]], xp, ln_mix_g[layer], ln_mix_b[layer], alpha, tm_p)
            d_win_p = min(BAND_CHUNKS * CHUNK, sp)
            outs["cc_p"].append(c_c)
            outs["cn_p"].append(c_n)
            outs["cm_p"].append(c_m)
            outs["dk_p"].append(k3[:, sp - d_win_p:].reshape(bp, d_win_p, N_HEADS, HEAD_DIM))
            outs["dv_p"].append(v3[:, sp - d_win_p:].reshape(bp, d_win_p, N_HEADS, HEAD_DIM))
            qk_c, v_c, o_c, q_d, k_d, v_d, misc = _proj(xs, w_in, widths, tm_s)
            y_c, c_c, c_n, c_m = _mlstm(qk_c, v_c, o_c, misc, state_c_c[o], state_c_n[o], state_c_m[o],
                                        c_i_bias[o], c_f_bias[o], c_norm_w[o], nb=bs, l=ts, i_col=i_col, f_col=f_col)
            o_d = _band_sample(q_d, cache_d_k[o].reshape(bs, d_win, w_a), cache_d_v[o].reshape(bs, d_win, w_a),
                               k_d.reshape(bs, ts, w_a), v_d.reshape(bs, ts, w_a),
                               band_bias(d_rel_bias[o], -d_win, ts, d_win, band_valid(pos_kc)),
                               band_bias(d_rel_bias[o], 0, ts, ts, band_valid(pos_q)), nb=bs, tq=ts)
            xs = _mm_res_ln([y_c, o_d], [w_out[:w_c], w_out[w_c:]], xs, ln_mix_g[layer], ln_mix_b[layer], alpha, tm_s)
            outs["cc_s"].append(c_c)
            outs["cn_s"].append(c_n)
            outs["cm_s"].append(c_m)
            outs["dk_s"].append(k_d.reshape(bs, ts, N_HEADS, HEAD_DIM))
            outs["dv_s"].append(v_d.reshape(bs, ts, N_HEADS, HEAD_DIM))
        w_up = ffn_w_up[layer].astype(BF16)
        w_down = ffn_w_down[layer].astype(BF16)
        act, hist_p = _ffn_up(xp, w_up, ffn_conv_w[layer], jnp.zeros((bp, CONV_FF - 1, 2 * dff), F32),
                              1, tff_p, sp // tff_p)
        xp = _mm_res_ln([act], [w_down], xp, ln_ffn_g[layer], ln_ffn_b[layer], alpha, tm_p)
        act, hist_s = _ffn_up(xs, w_up, ffn_conv_w[layer], state_ffn_conv[layer], ns_s, ts, 1)
        xs = _mm_res_ln([act], [w_down], xs, ln_ffn_g[layer], ln_ffn_b[layer], alpha, tm_s)
        outs["fc_p"].append(hist_p)
        outs["fc_s"].append(hist_s)

    st = lambda k: jnp.stack(outs[k])
    return (xp.reshape(bp, sp, d), xs.reshape(bs, ts, d),
            st("ak_p"), st("ak_s"), st("av_p"), st("av_s"), st("aki_p"), st("aki_s"),
            st("bs_p"), st("bs_s"), st("bc_p"), st("bc_s"),
            st("cc_p"), st("cc_s"), st("cn_p"), st("cn_s"), st("cm_p"), st("cm_s"),
            st("dk_p"), st("dk_s"), st("dv_p"), st("dv_s"),
            st("fc_p"), st("fc_s"))
```

```python
import functools
import math

import numpy as np
import jax
import jax.numpy as jnp
from jax import lax
from jax.experimental import pallas as pl
from jax.experimental.pallas import tpu as pltpu

F32 = jnp.float32
BF16 = jnp.bfloat16
HI = lax.Precision.HIGHEST

CHUNK = 64
HEAD_DIM = 64
N_HEADS = 8
N_IDX_HEADS = 8
D_IDX = 64
TOPK_MAX = 256
T5_BUCKETS = 32
T5_MAX_DIST = 128
DK_B = 64
DV_B = 64
CONV_B = 4
DK_C = 32
DV_C = 64
BAND_CHUNKS = 8
REL_CLIP = 128
CONV_FF = 3
NEAR_CHUNKS = 3
T5_FAR = 128
SEQS_PER_STEP = 4

LANES = 128
SUBLANES = 8
VMEM_LIMIT = 56 * 1024 * 1024

NEG = -1e30
INT_MIN = -2 ** 31


def _params(*sem):
    return pltpu.CompilerParams(dimension_semantics=sem, vmem_limit_bytes=VMEM_LIMIT)


def _dot(a, b):
    return jnp.dot(a.astype(BF16), b.astype(BF16), preferred_element_type=F32)


def _dot_nt(a, b, precision=None):
    return lax.dot_general(a, b, (((1,), (1,)), ((), ())), precision=precision, preferred_element_type=F32)


def _dot_tn(a, b, precision=None):
    return lax.dot_general(a, b, (((0,), (0,)), ((), ())), precision=precision, preferred_element_type=F32)


def _dot_hi(a, b):
    return jnp.dot(a, b, precision=HI, preferred_element_type=F32)


def _split(a):
    hi = a.astype(BF16)
    return hi, (a - hi.astype(F32)).astype(BF16)


def _split3(a):
    p1 = a.astype(BF16)
    r1 = a - p1.astype(F32)
    p2 = r1.astype(BF16)
    return p1, p2, (r1 - p2.astype(F32)).astype(BF16)


def _dot3(a, b):
    a_hi, a_lo = a
    b_hi, b_lo = b
    d = functools.partial(jnp.dot, preferred_element_type=F32)
    return d(a_hi, b_hi) + (d(a_hi, b_lo) + d(a_lo, b_hi))


def _sigmoid(x):
    return 1.0 / (1.0 + jnp.exp(-x))


def _softplus(x):
    return jnp.maximum(x, 0.0) + jnp.log(1.0 + jnp.exp(-jnp.abs(x)))


def _iota(shape, dim):
    return lax.broadcasted_iota(jnp.int32, shape, dim)


def _proj_kernel(x_ref, w_ref, *out_refs, sizes):
    xb = x_ref[...].astype(BF16)
    off = 0
    for o_ref, size in zip(out_refs, sizes):
        o_ref[...] = jnp.dot(xb, w_ref[:, off:off + size], preferred_element_type=F32)
        off += size


def _proj(x2d, w, sizes, tm):
    n, d = x2d.shape
    return pl.pallas_call(
        functools.partial(_proj_kernel, sizes=sizes),
        grid=(n // tm,),
        in_specs=[pl.BlockSpec((tm, d), lambda i: (i, 0)),
                  pl.BlockSpec((d, sum(sizes)), lambda i: (0, 0))],
        out_specs=[pl.BlockSpec((tm, s), lambda i: (i, 0)) for s in sizes],
        out_shape=[jax.ShapeDtypeStruct((n, s), F32) for s in sizes],
        compiler_params=_params("parallel"),
        name="in_proj",
    )(x2d, w)


def _mm_res_ln_kernel(*refs, nparts, alpha):
    part_refs = refs[:nparts]
    w_refs = refs[nparts:2 * nparts]
    x_ref, g_ref, b_ref, o_ref = refs[2 * nparts:]
    acc = alpha * x_ref[...]
    for p_ref, w_ref in zip(part_refs, w_refs):
        acc = acc + jnp.dot(p_ref[...].astype(BF16), w_ref[...], preferred_element_type=F32)
    mu = jnp.mean(acc, axis=-1, keepdims=True)
    cen = acc - mu
    var = jnp.mean(cen * cen, axis=-1, keepdims=True)
    o_ref[...] = cen * lax.rsqrt(var + 1e-5) * g_ref[...] + b_ref[...]


def _mm_res_ln(parts, ws, x2d, g, b, alpha, tm):
    n, d = x2d.shape
    nparts = len(parts)
    in_specs = ([pl.BlockSpec((tm, p.shape[1]), lambda i: (i, 0)) for p in parts]
                + [pl.BlockSpec(w.shape, lambda i: (0, 0)) for w in ws]
                + [pl.BlockSpec((tm, d), lambda i: (i, 0)),
                   pl.BlockSpec((1, d), lambda i: (0, 0)),
                   pl.BlockSpec((1, d), lambda i: (0, 0))])
    return pl.pallas_call(
        functools.partial(_mm_res_ln_kernel, nparts=nparts, alpha=alpha),
        grid=(n // tm,),
        in_specs=in_specs,
        out_specs=pl.BlockSpec((tm, d), lambda i: (i, 0)),
        out_shape=jax.ShapeDtypeStruct((n, d), F32),
        compiler_params=_params("parallel"),
        name="out_proj_ln",
    )(*parts, *ws, x2d, g.reshape(1, d), b.reshape(1, d))


def _ffn_up_kernel(x_ref, w_ref, cw_ref, hist_ref, act_ref, newhist_ref, ext_ref, *,
                   ns, tt, tiles_per_seq, dff, cc):
    i = pl.program_id(0)
    tm = ns * tt
    hw = CONV_FF - 1
    base = SUBLANES
    if tiles_per_seq == 1:
        ext_ref[:, base - hw:base, :] = hist_ref[...]
    else:
        @pl.when(i % tiles_per_seq == 0)
        def _():
            ext_ref[:, base - hw:base, :] = hist_ref[...]

        @pl.when(i % tiles_per_seq != 0)
        def _():
            ext_ref[:, base - hw:base, :] = ext_ref[:, base + tt - hw:base + tt, :]
    xb = x_ref[...].astype(BF16)
    for j in range(2 * dff // cc):
        cols = slice(j * cc, (j + 1) * cc)
        h = jnp.dot(xb, w_ref[:, cols], preferred_element_type=F32)
        ext_ref[:, base:base + tt, cols] = h.reshape(ns, tt, cc)
    newhist_ref[...] = ext_ref[:, base + tt - hw:base + tt, :]

    def conv(cols):
        acc = None
        for k in range(CONV_FF):
            term = ext_ref[:, base - hw + k:base - hw + k + tt, cols] * cw_ref[k:k + 1, cols]
            acc = term if acc is None else acc + term
        return acc

    for j in range(dff // cc):
        g = conv(slice(j * cc, (j + 1) * cc))
        u = conv(slice(dff + j * cc, dff + (j + 1) * cc))
        act = g * _sigmoid(g) * u
        act_ref[:, j * cc:(j + 1) * cc] = act.reshape(tm, cc).astype(BF16)


def _ffn_up(x2d, w_up, conv_w, hist, ns, tt, tiles_per_seq):
    n, d = x2d.shape
    c2 = w_up.shape[1]
    dff = c2 // 2
    tm = ns * tt
    cc = 256
    hw = CONV_FF - 1
    if tiles_per_seq == 1:
        hist_map = lambda i: (i, 0, 0)
    else:
        hist_map = lambda i: (i // tiles_per_seq, 0, 0)
    return pl.pallas_call(
        functools.partial(_ffn_up_kernel, ns=ns, tt=tt, tiles_per_seq=tiles_per_seq, dff=dff, cc=cc),
        grid=(n // tm,),
        in_specs=[pl.BlockSpec((tm, d), lambda i: (i, 0)),
                  pl.BlockSpec((d, c2), lambda i: (0, 0)),
                  pl.BlockSpec((CONV_FF, c2), lambda i: (0, 0)),
                  pl.BlockSpec((ns, hw, c2), hist_map)],
        out_specs=[pl.BlockSpec((tm, dff), lambda i: (i, 0)),
                   pl.BlockSpec((ns, hw, c2), hist_map)],
        out_shape=[jax.ShapeDtypeStruct((n, dff), BF16),
                   jax.ShapeDtypeStruct(hist.shape, F32)],
        scratch_shapes=[pltpu.VMEM((ns, SUBLANES + tt, c2), F32)],
        compiler_params=_params("arbitrary"),
        name="ffn_up_conv_gate",
    )(x2d, w_up, conv_w, hist)


def _sortable(x):
    b = lax.bitcast_convert_type(x, jnp.int32)
    return b ^ ((b >> 31) & jnp.int32(0x7FFFFFFF))


def _count(mask):
    return jnp.sum(jnp.where(mask, 1.0, 0.0), axis=-1, keepdims=True)


def _dsa_sample_kernel(qa_ref, qi_ref, qm_ref, kf_ref, vf_ref, kif_ref, kn_ref, vn_ref, kin_ref, nbias_ref,
                       fbias_ref, o_ref, self_ref, seln_ref, *, topk, tq, lf, ln, wi_col):
    v = 0
    start = lf - T5_FAR
    load_kn = lambda hs: kn_ref[0, :, hs]
    load_vn = lambda hs: vn_ref[0, :, hs]
    kin = kin_ref[0][:, :D_IDX]
    qa = qa_ref[...]
    qi = qi_ref[...]
    wi = qm_ref[:, wi_col:wi_col + N_IDX_HEADS] * (N_IDX_HEADS ** -0.5) * (D_IDX ** -0.5)
    kif = kif_ref[0][:, :D_IDX].astype(BF16)
    kinb = kin.astype(BF16)

    sc_f = jnp.zeros((tq, lf), F32)
    sc_n = jnp.zeros((tq, ln), F32)
    for n in range(N_IDX_HEADS):
        qn = qi[:, n * D_IDX:(n + 1) * D_IDX].astype(BF16)
        wn = wi[:, n:n + 1]
        sc_f = sc_f + jnp.maximum(_dot_nt(qn, kif), 0.0) * wn
        sc_n = sc_n + jnp.maximum(_dot_nt(qn, kinb), 0.0) * wn
    adm_f = _iota((tq, lf), 1) < start
    adm_n = nbias_ref[v, 0] > 0.5 * NEG
    key_f = jnp.where(adm_f, _sortable(sc_f), jnp.int32(INT_MIN))
    key_n = jnp.where(adm_n, _sortable(sc_n), jnp.int32(INT_MIN))

    kf32 = float(topk)

    def body(i, t_u):
        cand_u = t_u | lax.shift_left(jnp.int32(1), 31 - i)
        cand_s = cand_u ^ jnp.int32(INT_MIN)
        cnt = _count(key_f >= cand_s) + _count(key_n >= cand_s)
        return jnp.where(cnt >= kf32, cand_u, t_u)

    t_u = lax.fori_loop(0, 32, body, jnp.zeros((tq, 1), jnp.int32))
    thr = t_u ^ jnp.int32(INT_MIN)

    n_gt = _count(key_f > thr) + _count(key_n > thr)
    n_eq = _count(key_f == thr) + _count(key_n == thr)
    need = kf32 - n_gt
    open_row = thr == jnp.int32(INT_MIN)
    conflict = jnp.logical_and(n_eq != need, jnp.logical_not(open_row))
    self_ref[...] = jnp.where(jnp.logical_and(key_f >= thr, adm_f), 0.0, NEG)
    seln_ref[...] = jnp.where(jnp.logical_and(key_n >= thr, adm_n), 0.0, NEG)

    @pl.when(jnp.max(jnp.where(conflict, 1.0, 0.0)) > 0.0)
    def _():
        upper = jnp.where(_iota((LANES, LANES), 0) < _iota((LANES, LANES), 1), 1.0, 0.0).astype(BF16)
        offset = jnp.zeros((tq, 1), F32)
        for ref, key, width in ((self_ref, key_f, lf), (seln_ref, key_n, ln)):
            for j0 in range(0, width, LANES):
                w = min(LANES, width - j0)
                kb = key[:, j0:j0 + w]
                e = jnp.where(kb == thr, 1.0, 0.0)
                rank = offset + jnp.dot(e.astype(BF16), upper[:w, :w], preferred_element_type=F32)
                take = jnp.where(kb > thr, 1.0, jnp.where(rank < need, e, 0.0))
                take = jnp.where(open_row, jnp.where(kb > thr, 1.0, 0.0), take)
                ref[:, j0:j0 + w] = jnp.where(take > 0.5, 0.0, NEG)
                offset = offset + jnp.sum(e, axis=-1, keepdims=True)

    sel_f = self_ref[...]
    sel_n = seln_ref[...]
    for h in range(N_HEADS):
        hs = slice(h * HEAD_DIM, (h + 1) * HEAD_DIM)
        qh = qa[:, hs].astype(BF16)
        s_f = _dot_nt(qh, kf_ref[0, :, hs].astype(BF16)) * (HEAD_DIM ** -0.5) + fbias_ref[:, h:h + 1] + sel_f
        s_n = _dot_nt(qh, load_kn(hs).astype(BF16)) * (HEAD_DIM ** -0.5) + nbias_ref[v, h] + sel_n
        m = jnp.maximum(jnp.max(s_f, axis=-1, keepdims=True), jnp.max(s_n, axis=-1, keepdims=True))
        p_f = jnp.exp(s_f - m)
        p_n = jnp.exp(s_n - m)
        den = jnp.sum(p_f, axis=-1, keepdims=True) + jnp.sum(p_n, axis=-1, keepdims=True)
        o = _dot(p_f, vf_ref[0, :, hs]) + _dot(p_n, load_vn(hs))
        o_ref[:, hs] = o / den


def _dsa_sample(qa, qi, qmisc, kf, vf, kif, near, nbias, fbias, *, nb, tq, topk, wi_col):
    lf = kf.shape[1]
    ln = nbias.shape[-1]
    row = lambda b: (b, 0)
    per_b = lambda b: (b, 0, 0)
    args = [qa, qi, qmisc, kf, vf, kif, *near]
    in_specs = ([pl.BlockSpec((tq, a.shape[1]), row) for a in args[:3]]
                + [pl.BlockSpec((1,) + a.shape[1:], per_b) for a in args[3:]]
                + [pl.BlockSpec(nbias.shape, lambda b: (0, 0, 0, 0)),
                   pl.BlockSpec(fbias.shape, lambda b: (0, 0))])
    return pl.pallas_call(
        functools.partial(_dsa_sample_kernel, topk=topk, tq=tq, lf=lf, ln=ln, wi_col=wi_col),
        grid=(nb,),
        in_specs=in_specs,
        out_specs=pl.BlockSpec((tq, qa.shape[1]), row),
        out_shape=jax.ShapeDtypeStruct(qa.shape, F32),
        scratch_shapes=[pltpu.VMEM((tq, lf), F32), pltpu.VMEM((tq, ln), F32)],
        compiler_params=_params("parallel"),
        name="dsa_attention_sample",
    )(*args, nbias, fbias)


def _reduce_rows(x, op, final):
    blk = 8 * SUBLANES
    parts = [x[r0:r0 + blk] for r0 in range(0, x.shape[0], blk)]
    while len(parts) > 1:
        parts = [op(parts[i], parts[i + 1]) for i in range(0, len(parts) - 1, 2)] + parts[len(parts) & ~1:]
    return final(parts[0], axis=0, keepdims=True)


def _dsa_prompt_tile(g, lk, qa_ref, qi_ref, qm_ref, corr_ref, o_ref,
                     kb_ref, vt_ref, kib_ref, key_ref, sel_ref, s_ref, ot_ref, *, topk, tq, wi_col):
    rb = 2 * LANES if lk % (2 * LANES) == 0 else LANES
    nw = corr_ref.shape[2]
    qa = (qa_ref[...] * (HEAD_DIM ** -0.5)).astype(BF16)
    qi = qi_ref[...].astype(BF16)
    wi = qm_ref[:, wi_col:wi_col + N_IDX_HEADS] * (N_IDX_HEADS ** -0.5) * (D_IDX ** -0.5)
    eye_h = jnp.where(_iota((N_IDX_HEADS, N_IDX_HEADS), 0) == _iota((N_IDX_HEADS, N_IDX_HEADS), 1), 1.0, 0.0)
    wi_t = _dot_nt(eye_h, wi, HI)
    q_chunk = (g * tq + _iota((1, tq), 1)) // CHUNK

    for r0 in range(0, lk, rb):
        kib = kib_ref[r0:r0 + rb, :D_IDX]
        acc = jnp.zeros((rb, tq), F32)
        for n in range(N_IDX_HEADS):
            acc = acc + jnp.maximum(_dot_nt(kib, qi[:, n * D_IDX:(n + 1) * D_IDX]), 0.0) * wi_t[n:n + 1, :]
        adm = (r0 + _iota((rb, 1), 0)) // CHUNK <= q_chunk
        key_ref[r0:r0 + rb, :] = jnp.where(adm, _sortable(acc), jnp.int32(INT_MIN))

    kf32 = float(topk)
    count = lambda mask: _reduce_rows(jnp.where(mask, 1.0, 0.0), jnp.add, jnp.sum)

    def body(i, t_u):
        cand_u = t_u | lax.shift_left(jnp.int32(1), 31 - i)
        cand_s = cand_u ^ jnp.int32(INT_MIN)
        return jnp.where(count(key_ref[0:lk, :] >= cand_s) >= kf32, cand_u, t_u)

    t_u = lax.fori_loop(0, 32, body, jnp.zeros((1, tq), jnp.int32))
    thr = t_u ^ jnp.int32(INT_MIN)
    keys = key_ref[0:lk, :]
    need = kf32 - count(keys > thr)
    open_row = thr == jnp.int32(INT_MIN)
    conflict = jnp.logical_and(count(keys == thr) != need, jnp.logical_not(open_row))
    sel_ref[0:lk, :] = jnp.where(keys >= jnp.maximum(thr, jnp.int32(INT_MIN + 1)), 0.0, NEG)

    @pl.when(jnp.max(jnp.where(conflict, 1.0, 0.0)) > 0.0)
    def _():
        below = jnp.where(_iota((LANES, LANES), 1) < _iota((LANES, LANES), 0), 1.0, 0.0).astype(BF16)
        offset = jnp.zeros((1, tq), F32)
        for r0 in range(0, lk, LANES):
            kblk = key_ref[r0:r0 + LANES, :]
            e = jnp.where(kblk == thr, 1.0, 0.0)
            rank = offset + jnp.dot(below, e.astype(BF16), preferred_element_type=F32)
            take = jnp.where(kblk > thr, 1.0, jnp.where(jnp.logical_or(rank >= need, open_row), 0.0, e))
            sel_ref[r0:r0 + LANES, :] = jnp.where(take > 0.5, 0.0, NEG)
            offset = offset + jnp.sum(e, axis=0, keepdims=True)

    v = jnp.minimum(g, 1)
    start = pl.multiple_of(jnp.maximum(g * tq - T5_FAR, 0), LANES)
    for h in range(N_HEADS):
        hs = slice(h * HEAD_DIM, (h + 1) * HEAD_DIM)
        s_ref[0:lk, :] = _dot_nt(kb_ref[0:lk, hs], qa[:, hs]) + sel_ref[0:lk, :]
        s_ref[pl.ds(start, nw), :] += corr_ref[v, h]
        s = s_ref[0:lk, :]
        p = jnp.exp(s - _reduce_rows(s, jnp.maximum, jnp.max))
        pn = (p * (1.0 / _reduce_rows(p, jnp.add, jnp.sum))).astype(BF16)
        ot_ref[hs, :] = jnp.dot(vt_ref[hs, 0:lk], pn, preferred_element_type=F32)
    o_ref[...] = ot_ref[...].T


def _dsa_prompt_kernel(qa_ref, qi_ref, qm_ref, k_ref, v_ref, ki_ref, corr_ref, o_ref,
                       kb_ref, vt_ref, kib_ref, key_ref, sel_ref, s_ref, ot_ref, *, topk, tq, seq, n_groups, wi_col):
    g = pl.program_id(1)
    blk = _row_tile(seq, 4 * LANES)

    @pl.when(g == 0)
    def _():
        for r0 in range(0, seq, blk):
            kb_ref[r0:r0 + blk, :] = k_ref[0, r0:r0 + blk, :].astype(BF16)
            vt_ref[:, r0:r0 + blk] = v_ref[0, r0:r0 + blk, :].T.astype(BF16)
            kib_ref[r0:r0 + blk, :] = ki_ref[0, r0:r0 + blk, :].astype(BF16)

    per_group = (seq // tq) // n_groups
    for grp in range(n_groups):
        @pl.when(g // per_group == grp)
        def _():
            _dsa_prompt_tile(g, (grp + 1) * per_group * tq, qa_ref, qi_ref, qm_ref, corr_ref, o_ref,
                             kb_ref, vt_ref, kib_ref, key_ref, sel_ref, s_ref, ot_ref,
                             topk=topk, tq=tq, wi_col=wi_col)


def _dsa_prompt(qa, qi, qmisc, k, v, ki, corr, *, nb, tq, topk, wi_col, n_groups):
    n, w = qa.shape
    seq = k.shape[1]
    nq = seq // tq
    row = lambda b, g: (b * nq + g, 0)
    per_b = lambda b, g: (b, 0, 0)
    return pl.pallas_call(
        functools.partial(_dsa_prompt_kernel, topk=topk, tq=tq, seq=seq, n_groups=n_groups, wi_col=wi_col),
        grid=(nb, nq),
        in_specs=[pl.BlockSpec((tq, w), row),
                  pl.BlockSpec((tq, qi.shape[1]), row),
                  pl.BlockSpec((tq, qmisc.shape[1]), row),
                  pl.BlockSpec((1, seq, w), per_b),
                  pl.BlockSpec((1, seq, w), per_b),
                  pl.BlockSpec((1, seq, ki.shape[2]), per_b),
                  pl.BlockSpec(corr.shape, lambda b, g: (0, 0, 0, 0))],
        out_specs=pl.BlockSpec((tq, w), row),
        out_shape=jax.ShapeDtypeStruct(qa.shape, F32),
        scratch_shapes=[pltpu.VMEM((seq, w), BF16),
                        pltpu.VMEM((w, seq), BF16),
                        pltpu.VMEM((seq, ki.shape[2]), BF16),
                        pltpu.VMEM((seq, tq), jnp.int32),
                        pltpu.VMEM((seq, tq), F32),
                        pltpu.VMEM((seq, tq), F32),
                        pltpu.VMEM((w, tq), F32)],
        compiler_params=_params("parallel", "arbitrary"),
        name="dsa_attention_prompt",
    )(qa, qi, qmisc, k, v, ki, corr)


def _unit_lower_inverse(mats, n):
    eye = jnp.where(_iota((n, n), 0) == _iota((n, n), 1), 1.0, 0.0)
    ps = [eye - a for a in mats]
    aks = [_split(a) for a in mats]
    k = 1
    while 2 * k < n:
        aks = [_split(_dot3(ak, ak)) for ak in aks]
        ps = [p + _dot3(_split(p), ak) for p, ak in zip(ps, aks)]
        k *= 2
    return ps


def _head_rms(x, w):
    return x * lax.rsqrt(jnp.mean(x * x, axis=-1, keepdims=True) + 1e-6) * w


def _gdn_kernel(qkv_ref, misc_ref, z_ref, hist_ref, s0_ref, cw_ref, alog_ref, dtb_ref, nw_ref,
                y_ref, sfin_ref, newhist_ref, ext_ref, s_ref, *, c, a_col, b_col):
    j = pl.program_id(1)
    hw = CONV_B - 1
    base = SUBLANES
    wq = N_HEADS * DK_B

    @pl.when(j == 0)
    def _():
        ext_ref[base - hw:base, :] = hist_ref[0]
        s_ref[...] = s0_ref[0]

    @pl.when(j > 0)
    def _():
        ext_ref[base - hw:base, :] = ext_ref[base + c - hw:base + c, :]

    ext_ref[base:base + c, :] = qkv_ref[...]
    newhist_ref[0] = ext_ref[base + c - hw:base + c, :]
    conv = None
    for k in range(CONV_B):
        term = ext_ref[base - hw + k:base - hw + k + c, :] * cw_ref[k:k + 1, :]
        conv = term if conv is None else conv + term
    act = conv * _sigmoid(conv)

    beta = _sigmoid(misc_ref[:, b_col:b_col + N_HEADS])
    g = -jnp.exp(alog_ref[...]) * _softplus(misc_ref[:, a_col:a_col + N_HEADS] + dtb_ref[...])
    ri = _iota((c, c), 0)
    ci = _iota((c, c), 1)
    lower = ri >= ci
    strict = ri > ci
    gc = _dot_hi(jnp.where(lower, 1.0, 0.0), g)
    eye_h = jnp.where(_iota((N_HEADS, N_HEADS), 0) == _iota((N_HEADS, N_HEADS), 1), 1.0, 0.0)
    gc_t = _dot_nt(eye_h, gc, HI)
    eg = jnp.exp(gc)
    g_last = gc[c - 1:c, :]
    e_last = jnp.exp(g_last)
    e_rest = jnp.exp(g_last - gc)
    z = z_ref[...]
    heads = range(N_HEADS)
    col = lambda x, h: x[:, h:h + 1]
    qs = [act[:, h * DK_B:(h + 1) * DK_B] for h in heads]
    ks = [act[:, wq + h * DK_B:wq + (h + 1) * DK_B] for h in heads]
    vs = [act[:, 2 * wq + h * DV_B:2 * wq + (h + 1) * DV_B] for h in heads]
    qs = [q * lax.rsqrt(jnp.sum(q * q, axis=-1, keepdims=True) + 1e-6) * (DK_B ** -0.5) for q in qs]
    ks = [k * lax.rsqrt(jnp.sum(k * k, axis=-1, keepdims=True) + 1e-6) for k in ks]
    kbs = [k.astype(BF16) for k in ks]
    decay = [jnp.where(lower, jnp.exp(jnp.where(lower, col(gc, h) - gc_t[h:h + 1, :], 0.0)), 0.0) for h in heads]
    kk = [_dot_nt(kbs[h], kbs[h]) for h in heads]
    attn = [_dot_nt(qs[h].astype(BF16), kbs[h]) * decay[h] for h in heads]
    a_mat = [jnp.where(strict, col(beta, h) * kk[h] * decay[h], 0.0) for h in heads]
    t_mat = [t.astype(BF16) for t in _unit_lower_inverse(a_mat, c)]
    value = [_dot(t_mat[h], vs[h] * col(beta, h)) for h in heads]
    k_cum = [_dot(t_mat[h], ks[h] * (col(beta, h) * col(eg, h))) for h in heads]
    s_old = [s_ref[h] for h in heads]
    sbs = [s.astype(BF16) for s in s_old]
    v_new = [(value[h] - _dot(k_cum[h], sbs[h])).astype(BF16) for h in heads]
    o_inter = [_dot(qs[h] * col(eg, h), sbs[h]) for h in heads]
    o = [o_inter[h] + _dot(attn[h], v_new[h]) for h in heads]
    for h in heads:
        s_ref[h] = s_old[h] * col(e_last, h) + _dot_tn((ks[h] * col(e_rest, h)).astype(BF16), v_new[h])
    for h in heads:
        zh = z[:, h * DV_B:(h + 1) * DV_B]
        y_ref[:, h * DV_B:(h + 1) * DV_B] = _head_rms(o[h], nw_ref[...]) * (zh * _sigmoid(zh))
    sfin_ref[0] = s_ref[...]


def _gdn(qkv, misc, z, hist, s0, conv_w, a_log, dt_bias, norm_w, *, nb, c, a_col, b_col):
    n, wqkv = qkv.shape
    nch = n // (nb * c)
    row = lambda b, j: (b * nch + j, 0)
    per_b3 = lambda b, j: (b, 0, 0)
    per_b4 = lambda b, j: (b, 0, 0, 0)
    const2 = lambda b, j: (0, 0)
    hw = CONV_B - 1
    return pl.pallas_call(
        functools.partial(_gdn_kernel, c=c, a_col=a_col, b_col=b_col),
        grid=(nb, nch),
        in_specs=[pl.BlockSpec((c, wqkv), row),
                  pl.BlockSpec((c, misc.shape[1]), row),
                  pl.BlockSpec((c, z.shape[1]), row),
                  pl.BlockSpec((1, hw, wqkv), per_b3),
                  pl.BlockSpec((1,) + s0.shape[1:], per_b4),
                  pl.BlockSpec((CONV_B, wqkv), const2),
                  pl.BlockSpec((1, N_HEADS), const2),
                  pl.BlockSpec((1, N_HEADS), const2),
                  pl.BlockSpec((1, DV_B), const2)],
        out_specs=[pl.BlockSpec((c, z.shape[1]), row),
                   pl.BlockSpec((1,) + s0.shape[1:], per_b4),
                   pl.BlockSpec((1, hw, wqkv), per_b3)],
        out_shape=[jax.ShapeDtypeStruct(z.shape, F32),
                   jax.ShapeDtypeStruct(s0.shape, F32),
                   jax.ShapeDtypeStruct(hist.shape, F32)],
        scratch_shapes=[pltpu.VMEM((SUBLANES + c, wqkv), F32),
                        pltpu.VMEM(s0.shape[1:], F32)],
        compiler_params=_params("parallel", "arbitrary"),
        name="gated_deltanet",
    )(qkv, misc, z, hist, s0, conv_w, a_log.reshape(1, -1), dt_bias.reshape(1, -1), norm_w.reshape(1, -1))


def _mlstm_kernel(qk_ref, v_ref, og_ref, misc_ref, c0_ref, n0_ref, m0_ref, ib_ref, fb_ref, nw_ref,
                  y_ref, cfin_ref, nfin_ref, mfin_ref, c_ref, n_ref, m_ref, *, l, nbb, i_col, f_col):
    j = pl.program_id(1)
    wq = N_HEADS * DK_C

    @pl.when(j == 0)
    def _():
        c_ref[...] = c0_ref[...]
        n_ref[...] = n0_ref[...]
        m_ref[...] = m0_ref[...]

    causal = _iota((l, l), 0) >= _iota((l, l), 1)
    tri = jnp.where(causal, 1.0, 0.0)
    eye_h = jnp.where(_iota((N_HEADS, N_HEADS), 0) == _iota((N_HEADS, N_HEADS), 1), 1.0, 0.0)
    seqs = range(nbb)
    ig = [misc_ref[b, :, i_col:i_col + N_HEADS] + ib_ref[...] for b in seqs]
    lf = [-_softplus(-(misc_ref[b, :, f_col:f_col + N_HEADS] + fb_ref[...])) for b in seqs]
    fc = [_dot_hi(tri, lf[b]) for b in seqs]
    row_terms = [_dot_nt(eye_h, ig[b] - fc[b], HI) for b in seqs]
    m_prev = [m_ref[b] for b in seqs]
    log_inter = [fc[b] + m_prev[b] for b in seqs]
    f_last = [fc[b][l - 1:l, :] for b in seqs]
    m_vec = [f_last[b] + jnp.maximum(m_prev[b], jnp.max(ig[b] - fc[b], axis=0, keepdims=True)) for b in seqs]
    dec_vec = [jnp.exp(f_last[b] + m_prev[b] - m_vec[b]) for b in seqs]
    w_last = [jnp.exp(f_last[b] - fc[b] + ig[b] - m_vec[b]) for b in seqs]

    items = [(b, h) for b in seqs for h in range(N_HEADS)]
    col = lambda x, h: x[:, h:h + 1]
    qs = [qk_ref[b, :, h * DK_C:(h + 1) * DK_C] for b, h in items]
    ks = [qk_ref[b, :, wq + h * DK_C:wq + (h + 1) * DK_C] * (DK_C ** -0.5) for b, h in items]
    vbs = [v_ref[b, :, h * DV_C:(h + 1) * DV_C].astype(BF16) for b, h in items]
    c_old = [c_ref[b, h] for b, h in items]
    n_old = [n_ref[b, h:h + 1, :] for b, h in items]
    qk_raw = [_dot_nt(q.astype(BF16), k.astype(BF16)) for q, k in zip(qs, ks)]
    q_c = [_dot(q, c) for q, c in zip(qs, c_old)]
    log_w = [jnp.where(causal, col(fc[b], h) + row_terms[b][h:h + 1, :], -jnp.inf) for b, h in items]
    li = [col(log_inter[b], h) for b, h in items]
    m_t = [jnp.maximum(a, jnp.max(lw, axis=-1, keepdims=True)) for a, lw in zip(li, log_w)]
    w_inter = [jnp.exp(a - m) for a, m in zip(li, m_t)]
    qkw = [r * jnp.exp(lw - m) for r, lw, m in zip(qk_raw, log_w, m_t)]
    num = [_dot(a, vb) + wi * qc for a, vb, wi, qc in zip(qkw, vbs, w_inter, q_c)]
    den = [jnp.sum(a, axis=-1, keepdims=True) + wi * jnp.sum(q * n, axis=-1, keepdims=True)
           for a, wi, q, n in zip(qkw, w_inter, qs, n_old)]
    hid = [nu / jnp.maximum(jnp.abs(de), jnp.exp(-m)) for nu, de, m in zip(num, den, m_t)]
    kw = [k * col(w_last[b], h) for k, (b, h) in zip(ks, items)]
    for i, (b, h) in enumerate(items):
        c_ref[b, h] = col(dec_vec[b], h) * c_old[i] + _dot_tn(kw[i].astype(BF16), vbs[i])
        n_ref[b, h:h + 1, :] = col(dec_vec[b], h) * n_old[i] + jnp.sum(kw[i], axis=0, keepdims=True)
    for i, (b, h) in enumerate(items):
        oh = og_ref[b, :, h * DV_C:(h + 1) * DV_C]
        y_ref[b, :, h * DV_C:(h + 1) * DV_C] = _head_rms(hid[i], nw_ref[...]) * _sigmoid(oh)
    for b in seqs:
        m_ref[b] = m_vec[b]
    cfin_ref[...] = c_ref[...]
    nfin_ref[...] = n_ref[...]
    mfin_ref[...] = m_ref[...]


def _mlstm(qk, v, og, misc, c0, n0, m0, i_bias, f_bias, norm_w, *, nb, l, i_col, f_col):
    n = qk.shape[0]
    seq = n // nb
    nbb = SEQS_PER_STEP if nb % SEQS_PER_STEP == 0 else 1
    tok = lambda b, j: (b, j, 0)
    per_b3 = lambda b, j: (b, 0, 0)
    per_b4 = lambda b, j: (b, 0, 0, 0)
    const2 = lambda b, j: (0, 0)
    m0 = m0.reshape(nb, 1, N_HEADS)
    tokens = [a.reshape(nb, seq, a.shape[1]) for a in (qk, v, og, misc)]
    outs = pl.pallas_call(
        functools.partial(_mlstm_kernel, l=l, nbb=nbb, i_col=i_col, f_col=f_col),
        grid=(nb // nbb, seq // l),
        in_specs=[pl.BlockSpec((nbb, l, a.shape[2]), tok) for a in tokens]
                 + [pl.BlockSpec((nbb,) + c0.shape[1:], per_b4),
                    pl.BlockSpec((nbb,) + n0.shape[1:], per_b3),
                    pl.BlockSpec((nbb, 1, N_HEADS), per_b3),
                    pl.BlockSpec((1, N_HEADS), const2),
                    pl.BlockSpec((1, N_HEADS), const2),
                    pl.BlockSpec((1, DV_C), const2)],
        out_specs=[pl.BlockSpec((nbb, l, v.shape[1]), tok),
                   pl.BlockSpec((nbb,) + c0.shape[1:], per_b4),
                   pl.BlockSpec((nbb,) + n0.shape[1:], per_b3),
                   pl.BlockSpec((nbb, 1, N_HEADS), per_b3)],
        out_shape=[jax.ShapeDtypeStruct((nb, seq, v.shape[1]), F32),
                   jax.ShapeDtypeStruct(c0.shape, F32),
                   jax.ShapeDtypeStruct(n0.shape, F32),
                   jax.ShapeDtypeStruct(m0.shape, F32)],
        scratch_shapes=[pltpu.VMEM((nbb,) + c0.shape[1:], F32),
                        pltpu.VMEM((nbb,) + n0.shape[1:], F32),
                        pltpu.VMEM((nbb, 1, N_HEADS), F32)],
        compiler_params=_params("parallel", "arbitrary"),
        name="mlstm",
    )(*tokens, c0, n0, m0, i_bias.reshape(1, -1), f_bias.reshape(1, -1), norm_w.reshape(1, -1))
    y, c_fin, n_fin, m_fin = outs
    return y.reshape(n, v.shape[1]), c_fin, n_fin, m_fin.reshape(nb, N_HEADS)


def _block_mask(rows, row_group, cols, col_group):
    return (np.arange(rows)[:, None] // row_group == np.arange(cols)[None, :] // col_group).astype(np.float32)


def _cummax_rows(x):
    rows = x.shape[0]
    row = _iota(x.shape, 0)
    sh = 1
    while sh < rows:
        x = jnp.maximum(x, jnp.where(row >= sh, pltpu.roll(x, sh, axis=0), -jnp.inf))
        sh *= 2
    return x


def _dot2(a, b):
    hi, lo = _split(a)
    return jnp.dot(hi, b, preferred_element_type=F32) + jnp.dot(lo, b, preferred_element_type=F32)


def _mlstm_dense_kernel(qk_ref, v_ref, og_ref, misc_ref, c0_ref, n0_ref, m0_ref, ib_ref, fb_ref, nw_ref,
                        el_ref, ev_ref, ek_ref, kmask_ref, vmask_ref, cmask_ref, cmaskb_ref, rms_ref, causal_ref,
                        dsel_ref,
                        y_ref, cfin_ref, nfin_ref, mfin_ref, c_ref, n_ref, m_ref, *, l, i_col, f_col):
    j = pl.program_id(1)
    wq = N_HEADS * DK_C
    nbb = qk_ref.shape[0]
    seqs = range(nbb)

    @pl.when(j == 0)
    def _():
        c_ref[...] = jnp.zeros(c_ref.shape, F32)
        for b in seqs:
            for h in range(N_HEADS):
                c_ref[b, h * DK_C:(h + 1) * DK_C, h * DV_C:(h + 1) * DV_C] = c0_ref[b, h]
        n_ref[...] = n0_ref[...]
        m_ref[...] = m0_ref[...]

    each = lambda f, *xs: [f(*args) for args in zip(*xs)]
    dot = functools.partial(jnp.dot, preferred_element_type=F32)
    spread = lambda x, e_ref: sum(dot(p, e_ref[...]) for p in _split3(x))
    tri = jnp.where(_iota((l, l), 0) >= _iota((l, l), 1), 1.0, 0.0).astype(BF16)
    ig = [misc_ref[b, :, i_col:i_col + N_HEADS] + ib_ref[...] for b in seqs]
    lf = [-_softplus(-(misc_ref[b, :, f_col:f_col + N_HEADS] + fb_ref[...])) for b in seqs]
    fc = each(lambda x: sum(dot(tri, p) for p in _split3(x)), lf)
    m_prev = [m_ref[b] for b in seqs]
    log_inter = each(jnp.add, fc, m_prev)
    a = each(jnp.subtract, ig, fc)
    m_t = each(lambda li, f, x: jnp.maximum(li, f + _cummax_rows(x)), log_inter, fc, a)
    w_inter = each(lambda li, m: jnp.exp(li - m), log_inter, m_t)
    w_last = each(lambda f, i, m: jnp.exp(f[l - 1:l, :] - f + i - m[l - 1:l, :]), fc, ig, m_t)

    key_terms = each(lambda x: jnp.sum(spread(x, el_ref) * dsel_ref[...], axis=0, keepdims=True), a)
    log_w = each(lambda f, kt: jnp.where(causal_ref[...] > 0.5, spread(f, el_ref) + kt, -jnp.inf), fc, key_terms)
    q = [qk_ref[b, :, :wq] for b in seqs]
    q_b = [x.astype(BF16) for x in q]
    k_s = [qk_ref[b, :, wq:] * (DK_C ** -0.5) for b in seqs]
    k_b = [x.astype(BF16) for x in k_s]
    v_b = [v_ref[b].astype(BF16) for b in seqs]
    kt_bd = [jnp.concatenate([x] * N_HEADS, axis=0) * kmask_ref[...] for x in k_b]
    v_bd = [jnp.concatenate([x] * N_HEADS, axis=0) * vmask_ref[...] for x in v_b]
    m_l = each(lambda m: spread(m, el_ref), m_t)
    qkw = each(lambda x, kt, lw, m: (_dot_nt(x, kt) * jnp.exp(lw - m)).astype(BF16), q_b, kt_bd, log_w, m_l)
    c_old = [c_ref[b] for b in seqs]
    n_old = [n_ref[b] for b in seqs]
    wi_s = each(lambda w: spread(w, ev_ref), w_inter)
    q_c = each(lambda x, c: dot(x, c.astype(BF16)), q_b, c_old)
    qn = each(lambda x, n: _dot2(x * n, cmaskb_ref[...]), q, n_old)
    num = each(lambda w, vb, wi, qc: dot(w, vb) + wi * qc, qkw, v_bd, wi_s, q_c)
    den = each(lambda w, wi, x: dot(w, vmask_ref[...]) + wi * x, qkw, wi_s, qn)
    m_v = m_l if l == DV_C else each(lambda m: spread(m, ev_ref), m_t)
    hid = each(lambda nu, de, m: nu / jnp.maximum(jnp.abs(de), jnp.exp(-m)), num, den, m_v)
    ms = each(lambda x: _dot2(x * x, rms_ref[...]), hid)
    for b in seqs:
        y_ref[b] = hid[b] * lax.rsqrt(ms[b] + 1e-6) * nw_ref[...] * _sigmoid(og_ref[b])

    kw = each(lambda k, w: k * spread(w, ek_ref), k_s, w_last)
    dec_rows = [jnp.broadcast_to(w[l - 1:l, :], (SUBLANES, N_HEADS)) for w in w_inter]
    upd = each(lambda x, vb: _dot_tn(x.astype(BF16), vb), kw, v_b)
    for b in seqs:
        c_ref[b] = c_old[b] * spread(dec_rows[b], ev_ref)[0:1, :] + upd[b] * cmask_ref[...]
        n_ref[b] = n_old[b] * spread(dec_rows[b], ek_ref)[0:1, :] + jnp.sum(kw[b], axis=0, keepdims=True)
        m_ref[b] = m_t[b][l - 1:l, :]

    @pl.when(j == pl.num_programs(1) - 1)
    def _():
        for b in seqs:
            for h in range(N_HEADS):
                cfin_ref[b, h] = c_ref[b, h * DK_C:(h + 1) * DK_C, h * DV_C:(h + 1) * DV_C]
        nfin_ref[...] = n_ref[...]
        mfin_ref[...] = m_ref[...]


def _mlstm_dense(qk, v, og, misc, c0, n0, m0, i_bias, f_bias, norm_w, *, nb, l, i_col, f_col):
    n = qk.shape[0]
    seq = n // nb
    wq, wv = N_HEADS * DK_C, N_HEADS * DV_C
    hl = N_HEADS * l
    nbb = SEQS_PER_STEP if nb % SEQS_PER_STEP == 0 else 1
    tok = lambda b, j: (b, j, 0)
    per_b3 = lambda b, j: (b, 0, 0)
    per_b4 = lambda b, j: (b, 0, 0, 0)
    const2 = lambda b, j: (0, 0)
    consts = [jnp.asarray(_block_mask(N_HEADS, 1, hl, l), BF16),
              jnp.asarray(_block_mask(N_HEADS, 1, wv, DV_C), BF16),
              jnp.asarray(_block_mask(N_HEADS, 1, wq, DK_C), BF16),
              jnp.asarray(_block_mask(hl, l, wq, DK_C), BF16),
              jnp.asarray(_block_mask(hl, l, wv, DV_C), BF16),
              jnp.asarray(_block_mask(wq, DK_C, wv, DV_C)),
              jnp.asarray(_block_mask(wq, DK_C, wv, DV_C), BF16),
              jnp.asarray(_block_mask(wv, DV_C, wv, DV_C) / DV_C, BF16),
              jnp.asarray(np.tile(np.tril(np.ones((l, l), np.float32)), (1, N_HEADS))),
              jnp.asarray(np.tile(np.eye(l, dtype=np.float32), (1, N_HEADS)))]
    tokens = [a.reshape(nb, seq, a.shape[1]) for a in (qk, v, og, misc)]
    outs = pl.pallas_call(
        functools.partial(_mlstm_dense_kernel, l=l, i_col=i_col, f_col=f_col),
        grid=(nb // nbb, seq // l),
        in_specs=[pl.BlockSpec((nbb, l, a.shape[2]), tok) for a in tokens]
                 + [pl.BlockSpec((nbb,) + c0.shape[1:], per_b4),
                    pl.BlockSpec((nbb, 1, wq), per_b3),
                    pl.BlockSpec((nbb, 1, N_HEADS), per_b3),
                    pl.BlockSpec((1, N_HEADS), const2),
                    pl.BlockSpec((1, N_HEADS), const2),
                    pl.BlockSpec((1, wv), const2)]
                 + [pl.BlockSpec(c.shape, const2) for c in consts],
        out_specs=[pl.BlockSpec((nbb, l, wv), tok),
                   pl.BlockSpec((nbb,) + c0.shape[1:], per_b4),
                   pl.BlockSpec((nbb, 1, wq), per_b3),
                   pl.BlockSpec((nbb, 1, N_HEADS), per_b3)],
        out_shape=[jax.ShapeDtypeStruct((nb, seq, wv), F32),
                   jax.ShapeDtypeStruct(c0.shape, F32),
                   jax.ShapeDtypeStruct((nb, 1, wq), F32),
                   jax.ShapeDtypeStruct((nb, 1, N_HEADS), F32)],
        scratch_shapes=[pltpu.VMEM((nbb, wq, wv), F32),
                        pltpu.VMEM((nbb, 1, wq), F32),
                        pltpu.VMEM((nbb, 1, N_HEADS), F32)],
        compiler_params=_params("parallel", "arbitrary"),
        name="mlstm",
    )(*tokens, c0, n0.reshape(nb, 1, wq), m0.reshape(nb, 1, N_HEADS), i_bias.reshape(1, -1), f_bias.reshape(1, -1),
      jnp.tile(norm_w, N_HEADS).reshape(1, wv), *consts)
    y, c_fin, n_fin, m_fin = outs
    return y.reshape(n, wv), c_fin, n_fin.reshape(nb, N_HEADS, DK_C), m_fin.reshape(nb, N_HEADS)


def _band_prompt_kernel(q_ref, k_ref, v_ref, bias_ref, o_ref, kb_ref, vt_ref, ot_ref, *, tq, seq):
    g = pl.program_id(1)
    w = q_ref.shape[1]
    pad = BAND_CHUNKS * CHUNK
    lw = pad + tq
    npad = pad // LANES
    per_tile = tq // LANES
    blk = _row_tile(seq, 4 * LANES)

    @pl.when(g == 0)
    def _():
        kb_ref[0:pad, :] = jnp.zeros((pad, w), BF16)
        vt_ref[0:npad] = jnp.zeros((npad, w, LANES), BF16)
        for r0 in range(0, seq, blk):
            kb_ref[pad + r0:pad + r0 + blk, :] = k_ref[0, r0:r0 + blk, :].astype(BF16)
        for j in range(seq // LANES):
            vt_ref[npad + j] = v_ref[0, j * LANES:(j + 1) * LANES, :].T.astype(BF16)

    start = pl.multiple_of(g * tq, tq)
    q = (q_ref[...] * (HEAD_DIM ** -0.5)).astype(BF16)
    before_seq = jnp.where(_iota((lw, tq), 0) >= pad - g * tq, 0.0, NEG)
    for h in range(N_HEADS):
        hs = slice(h * HEAD_DIM, (h + 1) * HEAD_DIM)
        s = _dot_nt(kb_ref[pl.ds(start, lw), hs], q[:, hs]) + (bias_ref[h] + before_seq)
        p = jnp.exp(s - _reduce_rows(s, jnp.maximum, jnp.max))
        pn = (p * (1.0 / _reduce_rows(p, jnp.add, jnp.sum))).astype(BF16)
        acc = None
        for j in range(lw // LANES):
            part = jnp.dot(vt_ref[g * per_tile + j, hs, :], pn[j * LANES:(j + 1) * LANES, :],
                           preferred_element_type=F32)
            acc = part if acc is None else acc + part
        ot_ref[hs, :] = acc
    o_ref[...] = ot_ref[...].T


def _band_prompt(q, k, v, bias, *, nb, tq):
    n, w = q.shape
    seq = k.shape[1]
    nq = seq // tq
    pad = BAND_CHUNKS * CHUNK
    row = lambda b, g: (b * nq + g, 0)
    per_b = lambda b, g: (b, 0, 0)
    return pl.pallas_call(
        functools.partial(_band_prompt_kernel, tq=tq, seq=seq),
        grid=(nb, nq),
        in_specs=[pl.BlockSpec((tq, w), row),
                  pl.BlockSpec((1, seq, w), per_b),
                  pl.BlockSpec((1, seq, w), per_b),
                  pl.BlockSpec(bias.shape, lambda b, g: (0, 0, 0))],
        out_specs=pl.BlockSpec((tq, w), row),
        out_shape=jax.ShapeDtypeStruct(q.shape, F32),
        scratch_shapes=[pltpu.VMEM((pad + seq, w), BF16),
                        pltpu.VMEM(((pad + seq) // LANES, w, LANES), BF16),
                        pltpu.VMEM((w, tq), F32)],
        compiler_params=_params("parallel", "arbitrary"),
        name="band_attention_prompt",
    )(q, k, v, bias)


def _band_sample_kernel(q_ref, kc_ref, vc_ref, kn_ref, vn_ref, biasc_ref, biasn_ref, o_ref):
    pieces = [(kc_ref, vc_ref, biasc_ref), (kn_ref, vn_ref, biasn_ref)]
    q = q_ref[...]
    for h in range(N_HEADS):
        hs = slice(h * HEAD_DIM, (h + 1) * HEAD_DIM)
        qh = q[:, hs].astype(BF16)
        scores = [_dot_nt(qh, k_ref[0, :, hs].astype(BF16)) * (HEAD_DIM ** -0.5) + b_ref[h]
                  for k_ref, _, b_ref in pieces]
        m = functools.reduce(jnp.maximum, [jnp.max(s, axis=-1, keepdims=True) for s in scores])
        ps = [jnp.exp(s - m) for s in scores]
        den = sum(jnp.sum(p, axis=-1, keepdims=True) for p in ps)
        o = sum(_dot(p, v_ref[0, :, hs]) for p, (_, v_ref, _) in zip(ps, pieces))
        o_ref[:, hs] = o / den


def _band_sample(q, kc, vc, kn, vn, bias_c, bias_n, *, nb, tq):
    n, w = q.shape
    row = lambda b: (b, 0)
    per_b = lambda b: (b, 0, 0)
    const3 = lambda b: (0, 0, 0)
    return pl.pallas_call(
        _band_sample_kernel,
        grid=(nb,),
        in_specs=[pl.BlockSpec((tq, w), row)]
                 + [pl.BlockSpec((1,) + a.shape[1:], per_b) for a in (kc, vc, kn, vn)]
                 + [pl.BlockSpec(bias_c.shape, const3), pl.BlockSpec(bias_n.shape, const3)],
        out_specs=pl.BlockSpec((tq, w), row),
        out_shape=jax.ShapeDtypeStruct(q.shape, F32),
        compiler_params=_params("parallel"),
        name="band_attention_sample",
    )(q, kc, vc, kn, vn, bias_c, bias_n)


def _t5_bucket(rel):
    nb = T5_BUCKETS // 2
    max_exact = nb // 2
    n = jnp.abs(rel)
    n_f = jnp.maximum(n, 1).astype(jnp.float32)
    large = max_exact + (jnp.log(n_f / max_exact) / math.log(T5_MAX_DIST / max_exact) * (nb - max_exact)).astype(jnp.int32)
    large = jnp.minimum(large, nb - 1)
    return jnp.where(rel > 0, nb, 0) + jnp.where(n < max_exact, n, large)


def _toeplitz_bias(fn, n_rows, n_cols):
    n = n_rows + n_cols
    m = np.arange(n)
    f = jnp.transpose(fn(np.where(m < n_cols, m, m - n))).astype(F32)
    flat = jnp.tile(f, (1, n_rows))[:, :n_rows * (n - 1)]
    return flat.reshape(f.shape[0], n_rows, n - 1)[:, :, :n_cols]


def _pack_cols(w, sizes, groups):
    offs = np.concatenate([[0], np.cumsum(sizes)])
    cols, widths = [], []
    for grp in groups:
        width = 0
        for idx in grp:
            cols.append(w[:, offs[idx]:offs[idx + 1]])
            width += sizes[idx]
        pad = (-width) % LANES
        if pad:
            cols.append(jnp.zeros((w.shape[0], pad), w.dtype))
        widths.append(width + pad)
    return jnp.concatenate(cols, axis=1).astype(BF16), tuple(widths)


def _row_tile(n, target):
    t = min(n, target)
    while n % t:
        t //= 2
    return t


def kernel(x_prompt, x_sample, cache_a_k, cache_a_v, cache_a_kidx, state_b_s, state_b_conv, state_c_c, state_c_n, state_c_m, cache_d_k, cache_d_v, state_ffn_conv, w_in_even, w_out_even, t5_bias, b_conv_w, b_a_log, b_dt_bias, b_norm_w, w_in_odd, w_out_odd, c_i_bias, c_f_bias, c_norm_w, d_rel_bias, ln_mix_g, ln_mix_b, ln_ffn_g, ln_ffn_b, ffn_w_up, ffn_conv_w, ffn_w_down):
    bp, sp, d = x_prompt.shape
    bs, ts, _ = x_sample.shape
    depth = ffn_w_up.shape[0]
    past = cache_a_k.shape[2]
    d_win = cache_d_k.shape[2]
    dff = ffn_w_down.shape[1]
    alpha = (2 * depth) ** 0.25
    w_a = N_HEADS * HEAD_DIM
    w_b = N_HEADS * DV_B
    w_c = N_HEADS * DV_C
    qkv_b_w = 2 * N_HEADS * DK_B + w_b
    even_sizes = (w_a, w_a, w_a, N_IDX_HEADS * D_IDX, D_IDX, N_IDX_HEADS, qkv_b_w, N_HEADS, N_HEADS, w_b)
    odd_sizes = (N_HEADS * DK_C, N_HEADS * DK_C, w_c, N_HEADS, N_HEADS, w_c, w_a, w_a, w_a)
    even_groups = ((0,), (1,), (2,), (3,), (6,), (9,), (4, 5, 7, 8))
    wi_col, a_col, b_col = D_IDX, D_IDX + N_IDX_HEADS, D_IDX + N_IDX_HEADS + N_HEADS
    odd_groups = ((0, 1), (2,), (5,), (6,), (7,), (8,), (3, 4))
    i_col, f_col = 0, N_HEADS

    assert sp % CHUNK == 0 and ts <= CHUNK and past % CHUNK == 0 and past >= T5_FAR
    assert (past + ts - 1) // CHUNK == past // CHUNK
    topk_p = min(TOPK_MAX, sp // 4)
    topk_s = min(TOPK_MAX, (past + ts) // 4)
    n_p, n_s = bp * sp, bs * ts
    tm_p = _row_tile(n_p, 512)
    tm_s = _row_tile(n_s, 512)
    tff_p = _row_tile(sp, 512)
    ns_s = _row_tile(bs, max(1, 256 // ts))

    t5 = lambda rel: t5_bias[_t5_bucket(jnp.asarray(rel, jnp.int32))]
    fbias = t5(np.array([-T5_FAR - 1]))
    tq_a = 2 * CHUNK
    nw_a = tq_a + T5_FAR
    assert sp % tq_a == 0 and sp >= nw_a
    nq_a = sp // tq_a
    dsa_groups = max(gr for gr in (4, 2, 1) if nq_a % gr == 0 and (nq_a // gr) * tq_a >= nw_a)
    corr_p = jnp.stack([_toeplitz_bias(lambda dd: t5(-dd - T5_FAR * v) - fbias, nw_a, tq_a) for v in range(2)])
    ln_s = T5_FAR + ts
    nbias_s = _toeplitz_bias(lambda dd: t5(dd - T5_FAR), ts, ln_s)[None]

    lw = BAND_CHUNKS * CHUNK + tq_a
    r_chunk = np.arange(lw)[:, None] // CHUNK
    q_chunk = BAND_CHUNKS + np.arange(tq_a)[None, :] // CHUNK
    band_ok = (r_chunk >= q_chunk - BAND_CHUNKS) & (r_chunk <= q_chunk)
    pos_q = past + np.arange(ts)
    pos_kc = past - d_win + np.arange(d_win)
    def band_valid(pos_k):
        kch, qch = pos_k // CHUNK, pos_q // CHUNK
        return (pos_k[None] >= 0) & (kch[None] >= qch[:, None] - BAND_CHUNKS) & (kch[None] <= qch[:, None])
    def band_bias(table, shift, n_rows, n_cols, valid=None):
        bias = _toeplitz_bias(lambda dd: table[np.clip(dd + shift, -REL_CLIP, REL_CLIP) + REL_CLIP], n_rows, n_cols)
        return bias if valid is None else jnp.where(jnp.asarray(valid)[None], bias, NEG)

    xp = x_prompt.reshape(n_p, d)
    xs = x_sample.reshape(n_s, d)
    outs = {k: [] for k in ("ak_p", "ak_s", "av_p", "av_s", "aki_p", "aki_s", "bs_p", "bs_s", "bc_p", "bc_s",
                            "cc_p", "cc_s", "cn_p", "cn_s", "cm_p", "cm_s", "dk_p", "dk_s", "dv_p", "dv_s",
                            "fc_p", "fc_s")}
    for layer in range(depth):
        if layer % 2 == 0:
            e = layer // 2
            w_in, widths = _pack_cols(w_in_even[e], even_sizes, even_groups)
            w_out = w_out_even[e].astype(BF16)
            qa, ka, va, qi, qkv_b, z_b, misc = _proj(xp, w_in, widths, tm_p)
            o_a = _dsa_prompt(qa, qi, misc, ka.reshape(bp, sp, w_a), va.reshape(bp, sp, w_a),
                              misc.reshape(bp, sp, LANES), corr_p, nb=bp, tq=tq_a, topk=topk_p, wi_col=wi_col,
                              n_groups=dsa_groups)
            y_b, s_b, h_b = _gdn(qkv_b, misc, z_b, jnp.zeros((bp, CONV_B - 1, qkv_b_w), F32),
                                 jnp.zeros((bp, N_HEADS, DK_B, DV_B), F32), b_conv_w[e], b_a_log[e], b_dt_bias[e],
                                 b_norm_w[e], nb=bp, c=CHUNK, a_col=a_col, b_col=b_col)
            xp = _mm_res_ln([o_a, y_b], [w_out[:w_a], w_out[w_a:]], xp, ln_mix_g[layer], ln_mix_b[layer], alpha, tm_p)
            outs["ak_p"].append(ka.reshape(bp, sp, N_HEADS, HEAD_DIM))
            outs["av_p"].append(va.reshape(bp, sp, N_HEADS, HEAD_DIM))
            outs["aki_p"].append(misc[:, :D_IDX].reshape(bp, sp, D_IDX))
            outs["bs_p"].append(s_b)
            outs["bc_p"].append(h_b)
            qa, ka, va, qi, qkv_b, z_b, misc = _proj(xs, w_in, widths, tm_s)
            ki = misc[:, :D_IDX]
            ck = cache_a_k[e].reshape(bs, past, w_a)
            cv = cache_a_v[e].reshape(bs, past, w_a)
            cki = cache_a_kidx[e]
            near = (jnp.concatenate([ck[:, past - T5_FAR:], ka.reshape(bs, ts, w_a)], axis=1),
                    jnp.concatenate([cv[:, past - T5_FAR:], va.reshape(bs, ts, w_a)], axis=1),
                    jnp.concatenate([cki[:, past - T5_FAR:], ki.reshape(bs, ts, D_IDX)], axis=1))
            o_a = _dsa_sample(qa, qi, misc, ck, cv, cki, near, nbias_s, fbias, nb=bs, tq=ts, topk=topk_s,
                              wi_col=wi_col)
            y_b, s_b, h_b = _gdn(qkv_b, misc, z_b, state_b_conv[e], state_b_s[e], b_conv_w[e], b_a_log[e],
                                 b_dt_bias[e], b_norm_w[e], nb=bs, c=ts, a_col=a_col, b_col=b_col)
            xs = _mm_res_ln([o_a, y_b], [w_out[:w_a], w_out[w_a:]], xs, ln_mix_g[layer], ln_mix_b[layer], alpha, tm_s)
            outs["ak_s"].append(ka.reshape(bs, ts, N_HEADS, HEAD_DIM))
            outs["av_s"].append(va.reshape(bs, ts, N_HEADS, HEAD_DIM))
            outs["aki_s"].append(ki.reshape(bs, ts, D_IDX))
            outs["bs_s"].append(s_b)
            outs["bc_s"].append(h_b)
        else:
            o = layer // 2
            w_in, widths = _pack_cols(w_in_odd[o], odd_sizes, odd_groups)
            w_out = w_out_odd[o].astype(BF16)
            qk_c, v_c, o_c, q_d, k_d, v_d, misc = _proj(xp, w_in, widths, tm_p)
            y_c, c_c, c_n, c_m = _mlstm_dense(qk_c, v_c, o_c, misc,jnp.zeros((bp, N_HEADS, DK_C, DV_C), F32),
                                        jnp.zeros((bp, N_HEADS, DK_C), F32), jnp.zeros((bp, N_HEADS), F32),
                                        c_i_bias[o], c_f_bias[o], c_norm_w[o], nb=bp, l=CHUNK, i_col=i_col, f_col=f_col)
            k3 = k_d.reshape(bp, sp, w_a)
            v3 = v_d.reshape(bp, sp, w_a)
            bias_bp = _toeplitz_bias(
                lambda dd: d_rel_bias[o][np.clip(-dd - BAND_CHUNKS * CHUNK, -REL_CLIP, REL_CLIP) + REL_CLIP], lw, tq_a)
            bias_bp = jnp.where(jnp.asarray(band_ok)[None], bias_bp, NEG)
            o_d = _band_prompt(q_d, k3, v3, bias_bp, nb=bp, tq=tq_a)
            xp = _mm_res_ln([y_c, o_d], [w_out[:w_c], w_out[w_c:]], xp, ln_mix_g[layer], ln_mix_b[layer], alpha, tm_p)
            d_win_p = min(BAND_CHUNKS * CHUNK, sp)
            outs["cc_p"].append(c_c)
            outs["cn_p"].append(c_n)
            outs["cm_p"].append(c_m)
            outs["dk_p"].append(k3[:, sp - d_win_p:].reshape(bp, d_win_p, N_HEADS, HEAD_DIM))
            outs["dv_p"].append(v3[:, sp - d_win_p:].reshape(bp, d_win_p, N_HEADS, HEAD_DIM))
            qk_c, v_c, o_c, q_d, k_d, v_d, misc = _proj(xs, w_in, widths, tm_s)
            y_c, c_c, c_n, c_m = _mlstm_dense(qk_c, v_c, o_c, misc,state_c_c[o], state_c_n[o], state_c_m[o],
                                        c_i_bias[o], c_f_bias[o], c_norm_w[o], nb=bs, l=ts, i_col=i_col, f_col=f_col)
            o_d = _band_sample(q_d, cache_d_k[o].reshape(bs, d_win, w_a), cache_d_v[o].reshape(bs, d_win, w_a),
                               k_d.reshape(bs, ts, w_a), v_d.reshape(bs, ts, w_a),
                               band_bias(d_rel_bias[o], -d_win, ts, d_win, band_valid(pos_kc)),
                               band_bias(d_rel_bias[o], 0, ts, ts, band_valid(pos_q)), nb=bs, tq=ts)
            xs = _mm_res_ln([y_c, o_d], [w_out[:w_c], w_out[w_c:]], xs, ln_mix_g[layer], ln_mix_b[layer], alpha, tm_s)
            outs["cc_s"].append(c_c)
            outs["cn_s"].append(c_n)
            outs["cm_s"].append(c_m)
            outs["dk_s"].append(k_d.reshape(bs, ts, N_HEADS, HEAD_DIM))
            outs["dv_s"].append(v_d.reshape(bs, ts, N_HEADS, HEAD_DIM))
        w_up = ffn_w_up[layer].astype(BF16)
        w_down = ffn_w_down[layer].astype(BF16)
        act, hist_p = _ffn_up(xp, w_up, ffn_conv_w[layer], jnp.zeros((bp, CONV_FF - 1, 2 * dff), F32),
                              1, tff_p, sp // tff_p)
        xp = _mm_res_ln([act], [w_down], xp, ln_ffn_g[layer], ln_ffn_b[layer], alpha, tm_p)
        act, hist_s = _ffn_up(xs, w_up, ffn_conv_w[layer], state_ffn_conv[layer], ns_s, ts, 1)
        xs = _mm_res_ln([act], [w_down], xs, ln_ffn_g[layer], ln_ffn_b[layer], alpha, tm_s)
        outs["fc_p"].append(hist_p)
        outs["fc_s"].append(hist_s)

    st = lambda k: jnp.stack(outs[k])
    return (xp.reshape(bp, sp, d), xs.reshape(bs, ts, d),
            st("ak_p"), st("ak_s"), st("av_p"), st("av_s"), st("aki_p"), st("aki_s"),
            st("bs_p"), st("bs_s"), st("bc_p"), st("bc_s"),
            st("cc_p"), st("cc_s"), st("cn_p"), st("cn_s"), st("cm_p"), st("cm_s"),
            st("dk_p"), st("dk_s"), st("dv_p"), st("dv_s"),
            st("fc_p"), st("fc_s"))
```

```python
import functools
import math

import numpy as np
import jax
import jax.numpy as jnp
from jax import lax
from jax.experimental import pallas as pl
from jax.experimental.pallas import tpu as pltpu

F32 = jnp.float32
BF16 = jnp.bfloat16
HI = lax.Precision.HIGHEST

CHUNK = 64
HEAD_DIM = 64
N_HEADS = 8
N_IDX_HEADS = 8
D_IDX = 64
TOPK_MAX = 256
T5_BUCKETS = 32
T5_MAX_DIST = 128
DK_B = 64
DV_B = 64
CONV_B = 4
DK_C = 32
DV_C = 64
BAND_CHUNKS = 8
REL_CLIP = 128
CONV_FF = 3
NEAR_CHUNKS = 3
T5_FAR = 128
SEQS_PER_STEP = 4

LANES = 128
SUBLANES = 8
VMEM_LIMIT = 56 * 1024 * 1024

NEG = -1e30
INT_MIN = -2 ** 31


def _params(*sem):
    return pltpu.CompilerParams(dimension_semantics=sem, vmem_limit_bytes=VMEM_LIMIT)


def _dot(a, b):
    return jnp.dot(a.astype(BF16), b.astype(BF16), preferred_element_type=F32)


def _dot_nt(a, b, precision=None):
    return lax.dot_general(a, b, (((1,), (1,)), ((), ())), precision=precision, preferred_element_type=F32)


def _dot_tn(a, b, precision=None):
    return lax.dot_general(a, b, (((0,), (0,)), ((), ())), precision=precision, preferred_element_type=F32)


def _dot_hi(a, b):
    return jnp.dot(a, b, precision=HI, preferred_element_type=F32)


def _split(a):
    hi = a.astype(BF16)
    return hi, (a - hi.astype(F32)).astype(BF16)


def _split3(a):
    p1 = a.astype(BF16)
    r1 = a - p1.astype(F32)
    p2 = r1.astype(BF16)
    return p1, p2, (r1 - p2.astype(F32)).astype(BF16)


def _dot3(a, b):
    a_hi, a_lo = a
    b_hi, b_lo = b
    d = functools.partial(jnp.dot, preferred_element_type=F32)
    return d(a_hi, b_hi) + (d(a_hi, b_lo) + d(a_lo, b_hi))


def _sigmoid(x):
    return 1.0 / (1.0 + jnp.exp(-x))


def _softplus(x):
    return jnp.maximum(x, 0.0) + jnp.log(1.0 + jnp.exp(-jnp.abs(x)))


def _iota(shape, dim):
    return lax.broadcasted_iota(jnp.int32, shape, dim)


def _proj_kernel(x_ref, w_ref, *out_refs, sizes):
    xb = x_ref[...].astype(BF16)
    off = 0
    for o_ref, size in zip(out_refs, sizes):
        o_ref[...] = jnp.dot(xb, w_ref[:, off:off + size], preferred_element_type=F32)
        off += size


def _proj(x2d, w, sizes, tm):
    n, d = x2d.shape
    return pl.pallas_call(
        functools.partial(_proj_kernel, sizes=sizes),
        grid=(n // tm,),
        in_specs=[pl.BlockSpec((tm, d), lambda i: (i, 0)),
                  pl.BlockSpec((d, sum(sizes)), lambda i: (0, 0))],
        out_specs=[pl.BlockSpec((tm, s), lambda i: (i, 0)) for s in sizes],
        out_shape=[jax.ShapeDtypeStruct((n, s), F32) for s in sizes],
        compiler_params=_params("parallel"),
        name="in_proj",
    )(x2d, w)


def _mm_res_ln_kernel(*refs, nparts, alpha):
    part_refs = refs[:nparts]
    w_refs = refs[nparts:2 * nparts]
    x_ref, g_ref, b_ref, o_ref = refs[2 * nparts:]
    acc = alpha * x_ref[...]
    for p_ref, w_ref in zip(part_refs, w_refs):
        acc = acc + jnp.dot(p_ref[...].astype(BF16), w_ref[...], preferred_element_type=F32)
    mu = jnp.mean(acc, axis=-1, keepdims=True)
    cen = acc - mu
    var = jnp.mean(cen * cen, axis=-1, keepdims=True)
    o_ref[...] = cen * lax.rsqrt(var + 1e-5) * g_ref[...] + b_ref[...]


def _mm_res_ln(parts, ws, x2d, g, b, alpha, tm):
    n, d = x2d.shape
    nparts = len(parts)
    in_specs = ([pl.BlockSpec((tm, p.shape[1]), lambda i: (i, 0)) for p in parts]
                + [pl.BlockSpec(w.shape, lambda i: (0, 0)) for w in ws]
                + [pl.BlockSpec((tm, d), lambda i: (i, 0)),
                   pl.BlockSpec((1, d), lambda i: (0, 0)),
                   pl.BlockSpec((1, d), lambda i: (0, 0))])
    return pl.pallas_call(
        functools.partial(_mm_res_ln_kernel, nparts=nparts, alpha=alpha),
        grid=(n // tm,),
        in_specs=in_specs,
        out_specs=pl.BlockSpec((tm, d), lambda i: (i, 0)),
        out_shape=jax.ShapeDtypeStruct((n, d), F32),
        compiler_params=_params("parallel"),
        name="out_proj_ln",
    )(*parts, *ws, x2d, g.reshape(1, d), b.reshape(1, d))


def _ffn_up_kernel(x_ref, w_ref, cw_ref, hist_ref, act_ref, newhist_ref, ext_ref, *,
                   ns, tt, tiles_per_seq, dff, cc):
    i = pl.program_id(0)
    tm = ns * tt
    hw = CONV_FF - 1
    base = SUBLANES
    if tiles_per_seq == 1:
        ext_ref[:, base - hw:base, :] = hist_ref[...]
    else:
        @pl.when(i % tiles_per_seq == 0)
        def _():
            ext_ref[:, base - hw:base, :] = hist_ref[...]

        @pl.when(i % tiles_per_seq != 0)
        def _():
            ext_ref[:, base - hw:base, :] = ext_ref[:, base + tt - hw:base + tt, :]
    xb = x_ref[...].astype(BF16)
    for j in range(2 * dff // cc):
        cols = slice(j * cc, (j + 1) * cc)
        h = jnp.dot(xb, w_ref[:, cols], preferred_element_type=F32)
        ext_ref[:, base:base + tt, cols] = h.reshape(ns, tt, cc)
    newhist_ref[...] = ext_ref[:, base + tt - hw:base + tt, :]

    def conv(cols):
        acc = None
        for k in range(CONV_FF):
            term = ext_ref[:, base - hw + k:base - hw + k + tt, cols] * cw_ref[k:k + 1, cols]
            acc = term if acc is None else acc + term
        return acc

    for j in range(dff // cc):
        g = conv(slice(j * cc, (j + 1) * cc))
        u = conv(slice(dff + j * cc, dff + (j + 1) * cc))
        act = g * _sigmoid(g) * u
        act_ref[:, j * cc:(j + 1) * cc] = act.reshape(tm, cc).astype(BF16)


def _ffn_up(x2d, w_up, conv_w, hist, ns, tt, tiles_per_seq):
    n, d = x2d.shape
    c2 = w_up.shape[1]
    dff = c2 // 2
    tm = ns * tt
    cc = 256
    hw = CONV_FF - 1
    if tiles_per_seq == 1:
        hist_map = lambda i: (i, 0, 0)
    else:
        hist_map = lambda i: (i // tiles_per_seq, 0, 0)
    return pl.pallas_call(
        functools.partial(_ffn_up_kernel, ns=ns, tt=tt, tiles_per_seq=tiles_per_seq, dff=dff, cc=cc),
        grid=(n // tm,),
        in_specs=[pl.BlockSpec((tm, d), lambda i: (i, 0)),
                  pl.BlockSpec((d, c2), lambda i: (0, 0)),
                  pl.BlockSpec((CONV_FF, c2), lambda i: (0, 0)),
                  pl.BlockSpec((ns, hw, c2), hist_map)],
        out_specs=[pl.BlockSpec((tm, dff), lambda i: (i, 0)),
                   pl.BlockSpec((ns, hw, c2), hist_map)],
        out_shape=[jax.ShapeDtypeStruct((n, dff), BF16),
                   jax.ShapeDtypeStruct(hist.shape, F32)],
        scratch_shapes=[pltpu.VMEM((ns, SUBLANES + tt, c2), F32)],
        compiler_params=_params("arbitrary"),
        name="ffn_up_conv_gate",
    )(x2d, w_up, conv_w, hist)


def _sortable(x):
    b = lax.bitcast_convert_type(x, jnp.int32)
    return b ^ ((b >> 31) & jnp.int32(0x7FFFFFFF))


def _count(mask):
    return jnp.sum(jnp.where(mask, 1.0, 0.0), axis=-1, keepdims=True)


def _dsa_sample_kernel(qa_ref, qi_ref, qm_ref, kf_ref, vf_ref, kif_ref, kn_ref, vn_ref, kin_ref, nbias_ref,
                       fbias_ref, o_ref, self_ref, seln_ref, *, topk, tq, lf, ln, wi_col):
    v = 0
    start = lf - T5_FAR
    qa = qa_ref[...]
    qi = qi_ref[...]
    wi = qm_ref[:, wi_col:wi_col + N_IDX_HEADS] * (N_IDX_HEADS ** -0.5) * (D_IDX ** -0.5)
    kif = kif_ref[0, 0].astype(BF16)
    kinb = kin_ref[0].astype(BF16)

    sc_f = jnp.zeros((tq, lf), F32)
    sc_n = jnp.zeros((tq, ln), F32)
    for n in range(N_IDX_HEADS):
        qn = qi[:, n * D_IDX:(n + 1) * D_IDX].astype(BF16)
        wn = wi[:, n:n + 1]
        sc_f = sc_f + jnp.maximum(jnp.dot(qn, kif, preferred_element_type=F32), 0.0) * wn
        sc_n = sc_n + jnp.maximum(jnp.dot(qn, kinb, preferred_element_type=F32), 0.0) * wn
    adm_f = _iota((tq, lf), 1) < start
    adm_n = nbias_ref[v, 0] > 0.5 * NEG
    key_f = jnp.where(adm_f, _sortable(sc_f), jnp.int32(INT_MIN))
    key_n = jnp.where(adm_n, _sortable(sc_n), jnp.int32(INT_MIN))

    kf32 = float(topk)

    def body(i, t_u):
        cand_u = t_u | lax.shift_left(jnp.int32(1), 31 - i)
        cand_s = cand_u ^ jnp.int32(INT_MIN)
        cnt = _count(key_f >= cand_s) + _count(key_n >= cand_s)
        return jnp.where(cnt >= kf32, cand_u, t_u)

    t_u = lax.fori_loop(0, 32, body, jnp.zeros((tq, 1), jnp.int32))
    thr = t_u ^ jnp.int32(INT_MIN)

    n_gt = _count(key_f > thr) + _count(key_n > thr)
    n_eq = _count(key_f == thr) + _count(key_n == thr)
    need = kf32 - n_gt
    open_row = thr == jnp.int32(INT_MIN)
    conflict = jnp.logical_and(n_eq != need, jnp.logical_not(open_row))
    self_ref[...] = jnp.where(jnp.logical_and(key_f >= thr, adm_f), 0.0, NEG)
    seln_ref[...] = jnp.where(jnp.logical_and(key_n >= thr, adm_n), 0.0, NEG)

    @pl.when(jnp.max(jnp.where(conflict, 1.0, 0.0)) > 0.0)
    def _():
        upper = jnp.where(_iota((LANES, LANES), 0) < _iota((LANES, LANES), 1), 1.0, 0.0).astype(BF16)
        offset = jnp.zeros((tq, 1), F32)
        for ref, key, width in ((self_ref, key_f, lf), (seln_ref, key_n, ln)):
            for j0 in range(0, width, LANES):
                w = min(LANES, width - j0)
                kb = key[:, j0:j0 + w]
                e = jnp.where(kb == thr, 1.0, 0.0)
                rank = offset + jnp.dot(e.astype(BF16), upper[:w, :w], preferred_element_type=F32)
                take = jnp.where(kb > thr, 1.0, jnp.where(rank < need, e, 0.0))
                take = jnp.where(open_row, jnp.where(kb > thr, 1.0, 0.0), take)
                ref[:, j0:j0 + w] = jnp.where(take > 0.5, 0.0, NEG)
                offset = offset + jnp.sum(e, axis=-1, keepdims=True)

    sel_f = self_ref[...]
    sel_n = seln_ref[...]
    for h in range(N_HEADS):
        hs = slice(h * HEAD_DIM, (h + 1) * HEAD_DIM)
        qh = qa[:, hs].astype(BF16)
        s_f = (jnp.dot(qh, kf_ref[0, 0, h].astype(BF16), preferred_element_type=F32) * (HEAD_DIM ** -0.5)
               + fbias_ref[:, h:h + 1] + sel_f)
        s_n = (jnp.dot(qh, kn_ref[0, h].astype(BF16), preferred_element_type=F32) * (HEAD_DIM ** -0.5)
               + nbias_ref[v, h] + sel_n)
        m = jnp.maximum(jnp.max(s_f, axis=-1, keepdims=True), jnp.max(s_n, axis=-1, keepdims=True))
        p_f = jnp.exp(s_f - m)
        p_n = jnp.exp(s_n - m)
        den = jnp.sum(p_f, axis=-1, keepdims=True) + jnp.sum(p_n, axis=-1, keepdims=True)
        o = (_dot_nt(p_f.astype(BF16), vf_ref[0, 0, h].astype(BF16))
             + _dot_nt(p_n.astype(BF16), vn_ref[0, h].astype(BF16)))
        o_ref[:, hs] = o / den


def _dsa_sample(qa, qi, qmisc, kf, vf, kif, layer, near, nbias, fbias, *, nb, tq, topk, wi_col):
    lf = kf.shape[-1]
    ln = nbias.shape[-1]
    row = lambda b: (b, 0)
    args = [qa, qi, qmisc, kf, vf, kif, *near]
    in_specs = ([pl.BlockSpec((tq, a.shape[1]), row) for a in args[:3]]
                + [pl.BlockSpec((1, 1) + a.shape[2:], lambda b, nd=a.ndim: (layer, b) + (0,) * (nd - 2))
                   for a in args[3:6]]
                + [pl.BlockSpec((1,) + a.shape[1:], lambda b, nd=a.ndim: (b,) + (0,) * (nd - 1)) for a in args[6:]]
                + [pl.BlockSpec(nbias.shape, lambda b: (0, 0, 0, 0)),
                   pl.BlockSpec(fbias.shape, lambda b: (0, 0))])
    return pl.pallas_call(
        functools.partial(_dsa_sample_kernel, topk=topk, tq=tq, lf=lf, ln=ln, wi_col=wi_col),
        grid=(nb,),
        in_specs=in_specs,
        out_specs=pl.BlockSpec((tq, qa.shape[1]), row),
        out_shape=jax.ShapeDtypeStruct(qa.shape, F32),
        scratch_shapes=[pltpu.VMEM((tq, lf), F32), pltpu.VMEM((tq, ln), F32)],
        compiler_params=_params("parallel"),
        name="dsa_attention_sample",
    )(*args, nbias, fbias)


def _reduce_rows(x, op, final):
    blk = 8 * SUBLANES
    parts = [x[r0:r0 + blk] for r0 in range(0, x.shape[0], blk)]
    while len(parts) > 1:
        parts = [op(parts[i], parts[i + 1]) for i in range(0, len(parts) - 1, 2)] + parts[len(parts) & ~1:]
    return final(parts[0], axis=0, keepdims=True)


def _dsa_prompt_tile(g, lk, qa_ref, qi_ref, qm_ref, corr_ref, o_ref,
                     kb_ref, vt_ref, kib_ref, key_ref, sel_ref, s_ref, ot_ref, *, topk, tq, wi_col):
    rb = 2 * LANES if lk % (2 * LANES) == 0 else LANES
    nw = corr_ref.shape[2]
    qa = (qa_ref[...] * (HEAD_DIM ** -0.5)).astype(BF16)
    qi = qi_ref[...].astype(BF16)
    wi = qm_ref[:, wi_col:wi_col + N_IDX_HEADS] * (N_IDX_HEADS ** -0.5) * (D_IDX ** -0.5)
    eye_h = jnp.where(_iota((N_IDX_HEADS, N_IDX_HEADS), 0) == _iota((N_IDX_HEADS, N_IDX_HEADS), 1), 1.0, 0.0)
    wi_t = _dot_nt(eye_h, wi, HI)
    q_chunk = (g * tq + _iota((1, tq), 1)) // CHUNK

    for r0 in range(0, lk, rb):
        kib = kib_ref[r0:r0 + rb, :D_IDX]
        acc = jnp.zeros((rb, tq), F32)
        for n in range(N_IDX_HEADS):
            acc = acc + jnp.maximum(_dot_nt(kib, qi[:, n * D_IDX:(n + 1) * D_IDX]), 0.0) * wi_t[n:n + 1, :]
        adm = (r0 + _iota((rb, 1), 0)) // CHUNK <= q_chunk
        key_ref[r0:r0 + rb, :] = jnp.where(adm, _sortable(acc), jnp.int32(INT_MIN))

    kf32 = float(topk)
    count = lambda mask: _reduce_rows(jnp.where(mask, 1.0, 0.0), jnp.add, jnp.sum)

    def body(i, t_u):
        cand_u = t_u | lax.shift_left(jnp.int32(1), 31 - i)
        cand_s = cand_u ^ jnp.int32(INT_MIN)
        return jnp.where(count(key_ref[0:lk, :] >= cand_s) >= kf32, cand_u, t_u)

    t_u = lax.fori_loop(0, 32, body, jnp.zeros((1, tq), jnp.int32))
    thr = t_u ^ jnp.int32(INT_MIN)
    keys = key_ref[0:lk, :]
    need = kf32 - count(keys > thr)
    open_row = thr == jnp.int32(INT_MIN)
    conflict = jnp.logical_and(count(keys == thr) != need, jnp.logical_not(open_row))
    sel_ref[0:lk, :] = jnp.where(keys >= jnp.maximum(thr, jnp.int32(INT_MIN + 1)), 0.0, NEG)

    @pl.when(jnp.max(jnp.where(conflict, 1.0, 0.0)) > 0.0)
    def _():
        below = jnp.where(_iota((LANES, LANES), 1) < _iota((LANES, LANES), 0), 1.0, 0.0).astype(BF16)
        offset = jnp.zeros((1, tq), F32)
        for r0 in range(0, lk, LANES):
            kblk = key_ref[r0:r0 + LANES, :]
            e = jnp.where(kblk == thr, 1.0, 0.0)
            rank = offset + jnp.dot(below, e.astype(BF16), preferred_element_type=F32)
            take = jnp.where(kblk > thr, 1.0, jnp.where(jnp.logical_or(rank >= need, open_row), 0.0, e))
            sel_ref[r0:r0 + LANES, :] = jnp.where(take > 0.5, 0.0, NEG)
            offset = offset + jnp.sum(e, axis=0, keepdims=True)

    v = jnp.minimum(g, 1)
    start = pl.multiple_of(jnp.maximum(g * tq - T5_FAR, 0), LANES)
    for h in range(N_HEADS):
        hs = slice(h * HEAD_DIM, (h + 1) * HEAD_DIM)
        s_ref[0:lk, :] = _dot_nt(kb_ref[0:lk, hs], qa[:, hs]) + sel_ref[0:lk, :]
        s_ref[pl.ds(start, nw), :] += corr_ref[v, h]
        s = s_ref[0:lk, :]
        p = jnp.exp(s - _reduce_rows(s, jnp.maximum, jnp.max))
        pn = (p * (1.0 / _reduce_rows(p, jnp.add, jnp.sum))).astype(BF16)
        ot_ref[hs, :] = jnp.dot(vt_ref[hs, 0:lk], pn, preferred_element_type=F32)
    o_ref[...] = ot_ref[...].T


def _dsa_prompt_kernel(qa_ref, qi_ref, qm_ref, k_ref, v_ref, ki_ref, corr_ref, o_ref,
                       kb_ref, vt_ref, kib_ref, key_ref, sel_ref, s_ref, ot_ref, *, topk, tq, seq, n_groups, wi_col):
    g = pl.program_id(1)
    blk = _row_tile(seq, 4 * LANES)

    @pl.when(g == 0)
    def _():
        for r0 in range(0, seq, blk):
            kb_ref[r0:r0 + blk, :] = k_ref[0, r0:r0 + blk, :].astype(BF16)
            vt_ref[:, r0:r0 + blk] = v_ref[0, r0:r0 + blk, :].T.astype(BF16)
            kib_ref[r0:r0 + blk, :] = ki_ref[0, r0:r0 + blk, :].astype(BF16)

    per_group = (seq // tq) // n_groups
    for grp in range(n_groups):
        @pl.when(g // per_group == grp)
        def _():
            _dsa_prompt_tile(g, (grp + 1) * per_group * tq, qa_ref, qi_ref, qm_ref, corr_ref, o_ref,
                             kb_ref, vt_ref, kib_ref, key_ref, sel_ref, s_ref, ot_ref,
                             topk=topk, tq=tq, wi_col=wi_col)


def _dsa_prompt(qa, qi, qmisc, k, v, ki, corr, *, nb, tq, topk, wi_col, n_groups):
    n, w = qa.shape
    seq = k.shape[1]
    nq = seq // tq
    row = lambda b, g: (b * nq + g, 0)
    per_b = lambda b, g: (b, 0, 0)
    return pl.pallas_call(
        functools.partial(_dsa_prompt_kernel, topk=topk, tq=tq, seq=seq, n_groups=n_groups, wi_col=wi_col),
        grid=(nb, nq),
        in_specs=[pl.BlockSpec((tq, w), row),
                  pl.BlockSpec((tq, qi.shape[1]), row),
                  pl.BlockSpec((tq, qmisc.shape[1]), row),
                  pl.BlockSpec((1, seq, w), per_b),
                  pl.BlockSpec((1, seq, w), per_b),
                  pl.BlockSpec((1, seq, ki.shape[2]), per_b),
                  pl.BlockSpec(corr.shape, lambda b, g: (0, 0, 0, 0))],
        out_specs=pl.BlockSpec((tq, w), row),
        out_shape=jax.ShapeDtypeStruct(qa.shape, F32),
        scratch_shapes=[pltpu.VMEM((seq, w), BF16),
                        pltpu.VMEM((w, seq), BF16),
                        pltpu.VMEM((seq, ki.shape[2]), BF16),
                        pltpu.VMEM((seq, tq), jnp.int32),
                        pltpu.VMEM((seq, tq), F32),
                        pltpu.VMEM((seq, tq), F32),
                        pltpu.VMEM((w, tq), F32)],
        compiler_params=_params("parallel", "arbitrary"),
        name="dsa_attention_prompt",
    )(qa, qi, qmisc, k, v, ki, corr)


def _unit_lower_inverse(mats, n):
    eye = jnp.where(_iota((n, n), 0) == _iota((n, n), 1), 1.0, 0.0)
    ps = [eye - a for a in mats]
    aks = [_split(a) for a in mats]
    k = 1
    while 2 * k < n:
        aks = [_split(_dot3(ak, ak)) for ak in aks]
        ps = [p + _dot3(_split(p), ak) for p, ak in zip(ps, aks)]
        k *= 2
    return ps


def _head_rms(x, w):
    return x * lax.rsqrt(jnp.mean(x * x, axis=-1, keepdims=True) + 1e-6) * w


def _gdn_kernel(qkv_ref, misc_ref, z_ref, hist_ref, s0_ref, cw_ref, alog_ref, dtb_ref, nw_ref,
                y_ref, sfin_ref, newhist_ref, ext_ref, s_ref, *, c, a_col, b_col):
    j = pl.program_id(1)
    hw = CONV_B - 1
    base = SUBLANES
    wq = N_HEADS * DK_B

    @pl.when(j == 0)
    def _():
        ext_ref[base - hw:base, :] = hist_ref[0]
        s_ref[...] = s0_ref[0]

    @pl.when(j > 0)
    def _():
        ext_ref[base - hw:base, :] = ext_ref[base + c - hw:base + c, :]

    ext_ref[base:base + c, :] = qkv_ref[...]
    newhist_ref[0] = ext_ref[base + c - hw:base + c, :]
    conv = None
    for k in range(CONV_B):
        term = ext_ref[base - hw + k:base - hw + k + c, :] * cw_ref[k:k + 1, :]
        conv = term if conv is None else conv + term
    act = conv * _sigmoid(conv)

    beta = _sigmoid(misc_ref[:, b_col:b_col + N_HEADS])
    g = -jnp.exp(alog_ref[...]) * _softplus(misc_ref[:, a_col:a_col + N_HEADS] + dtb_ref[...])
    ri = _iota((c, c), 0)
    ci = _iota((c, c), 1)
    lower = ri >= ci
    strict = ri > ci
    gc = _dot_hi(jnp.where(lower, 1.0, 0.0), g)
    eye_h = jnp.where(_iota((N_HEADS, N_HEADS), 0) == _iota((N_HEADS, N_HEADS), 1), 1.0, 0.0)
    gc_t = _dot_nt(eye_h, gc, HI)
    eg = jnp.exp(gc)
    g_last = gc[c - 1:c, :]
    e_last = jnp.exp(g_last)
    e_rest = jnp.exp(g_last - gc)
    z = z_ref[...]
    heads = range(N_HEADS)
    col = lambda x, h: x[:, h:h + 1]
    qs = [act[:, h * DK_B:(h + 1) * DK_B] for h in heads]
    ks = [act[:, wq + h * DK_B:wq + (h + 1) * DK_B] for h in heads]
    vs = [act[:, 2 * wq + h * DV_B:2 * wq + (h + 1) * DV_B] for h in heads]
    qs = [q * lax.rsqrt(jnp.sum(q * q, axis=-1, keepdims=True) + 1e-6) * (DK_B ** -0.5) for q in qs]
    ks = [k * lax.rsqrt(jnp.sum(k * k, axis=-1, keepdims=True) + 1e-6) for k in ks]
    kbs = [k.astype(BF16) for k in ks]
    decay = [jnp.where(lower, jnp.exp(jnp.where(lower, col(gc, h) - gc_t[h:h + 1, :], 0.0)), 0.0) for h in heads]
    kk = [_dot_nt(kbs[h], kbs[h]) for h in heads]
    attn = [_dot_nt(qs[h].astype(BF16), kbs[h]) * decay[h] for h in heads]
    a_mat = [jnp.where(strict, col(beta, h) * kk[h] * decay[h], 0.0) for h in heads]
    t_mat = [t.astype(BF16) for t in _unit_lower_inverse(a_mat, c)]
    value = [_dot(t_mat[h], vs[h] * col(beta, h)) for h in heads]
    k_cum = [_dot(t_mat[h], ks[h] * (col(beta, h) * col(eg, h))) for h in heads]
    s_old = [s_ref[h] for h in heads]
    sbs = [s.astype(BF16) for s in s_old]
    v_new = [(value[h] - _dot(k_cum[h], sbs[h])).astype(BF16) for h in heads]
    o_inter = [_dot(qs[h] * col(eg, h), sbs[h]) for h in heads]
    o = [o_inter[h] + _dot(attn[h], v_new[h]) for h in heads]
    for h in heads:
        s_ref[h] = s_old[h] * col(e_last, h) + _dot_tn((ks[h] * col(e_rest, h)).astype(BF16), v_new[h])
    for h in heads:
        zh = z[:, h * DV_B:(h + 1) * DV_B]
        y_ref[:, h * DV_B:(h + 1) * DV_B] = _head_rms(o[h], nw_ref[...]) * (zh * _sigmoid(zh))
    sfin_ref[0] = s_ref[...]


def _gdn(qkv, misc, z, hist, s0, conv_w, a_log, dt_bias, norm_w, *, nb, c, a_col, b_col):
    n, wqkv = qkv.shape
    nch = n // (nb * c)
    row = lambda b, j: (b * nch + j, 0)
    per_b3 = lambda b, j: (b, 0, 0)
    per_b4 = lambda b, j: (b, 0, 0, 0)
    const2 = lambda b, j: (0, 0)
    hw = CONV_B - 1
    return pl.pallas_call(
        functools.partial(_gdn_kernel, c=c, a_col=a_col, b_col=b_col),
        grid=(nb, nch),
        in_specs=[pl.BlockSpec((c, wqkv), row),
                  pl.BlockSpec((c, misc.shape[1]), row),
                  pl.BlockSpec((c, z.shape[1]), row),
                  pl.BlockSpec((1, hw, wqkv), per_b3),
                  pl.BlockSpec((1,) + s0.shape[1:], per_b4),
                  pl.BlockSpec((CONV_B, wqkv), const2),
                  pl.BlockSpec((1, N_HEADS), const2),
                  pl.BlockSpec((1, N_HEADS), const2),
                  pl.BlockSpec((1, DV_B), const2)],
        out_specs=[pl.BlockSpec((c, z.shape[1]), row),
                   pl.BlockSpec((1,) + s0.shape[1:], per_b4),
                   pl.BlockSpec((1, hw, wqkv), per_b3)],
        out_shape=[jax.ShapeDtypeStruct(z.shape, F32),
                   jax.ShapeDtypeStruct(s0.shape, F32),
                   jax.ShapeDtypeStruct(hist.shape, F32)],
        scratch_shapes=[pltpu.VMEM((SUBLANES + c, wqkv), F32),
                        pltpu.VMEM(s0.shape[1:], F32)],
        compiler_params=_params("parallel", "arbitrary"),
        name="gated_deltanet",
    )(qkv, misc, z, hist, s0, conv_w, a_log.reshape(1, -1), dt_bias.reshape(1, -1), norm_w.reshape(1, -1))


def _mlstm_kernel(qk_ref, v_ref, og_ref, misc_ref, c0_ref, n0_ref, m0_ref, ib_ref, fb_ref, nw_ref,
                  y_ref, cfin_ref, nfin_ref, mfin_ref, c_ref, n_ref, m_ref, *, l, nbb, i_col, f_col):
    j = pl.program_id(1)
    wq = N_HEADS * DK_C

    @pl.when(j == 0)
    def _():
        c_ref[...] = c0_ref[...]
        n_ref[...] = n0_ref[...]
        m_ref[...] = m0_ref[...]

    causal = _iota((l, l), 0) >= _iota((l, l), 1)
    tri = jnp.where(causal, 1.0, 0.0)
    eye_h = jnp.where(_iota((N_HEADS, N_HEADS), 0) == _iota((N_HEADS, N_HEADS), 1), 1.0, 0.0)
    seqs = range(nbb)
    ig = [misc_ref[b, :, i_col:i_col + N_HEADS] + ib_ref[...] for b in seqs]
    lf = [-_softplus(-(misc_ref[b, :, f_col:f_col + N_HEADS] + fb_ref[...])) for b in seqs]
    fc = [_dot_hi(tri, lf[b]) for b in seqs]
    row_terms = [_dot_nt(eye_h, ig[b] - fc[b], HI) for b in seqs]
    m_prev = [m_ref[b] for b in seqs]
    log_inter = [fc[b] + m_prev[b] for b in seqs]
    f_last = [fc[b][l - 1:l, :] for b in seqs]
    m_vec = [f_last[b] + jnp.maximum(m_prev[b], jnp.max(ig[b] - fc[b], axis=0, keepdims=True)) for b in seqs]
    dec_vec = [jnp.exp(f_last[b] + m_prev[b] - m_vec[b]) for b in seqs]
    w_last = [jnp.exp(f_last[b] - fc[b] + ig[b] - m_vec[b]) for b in seqs]

    items = [(b, h) for b in seqs for h in range(N_HEADS)]
    col = lambda x, h: x[:, h:h + 1]
    qs = [qk_ref[b, :, h * DK_C:(h + 1) * DK_C] for b, h in items]
    ks = [qk_ref[b, :, wq + h * DK_C:wq + (h + 1) * DK_C] * (DK_C ** -0.5) for b, h in items]
    vbs = [v_ref[b, :, h * DV_C:(h + 1) * DV_C].astype(BF16) for b, h in items]
    c_old = [c_ref[b, h] for b, h in items]
    n_old = [n_ref[b, h:h + 1, :] for b, h in items]
    qk_raw = [_dot_nt(q.astype(BF16), k.astype(BF16)) for q, k in zip(qs, ks)]
    q_c = [_dot(q, c) for q, c in zip(qs, c_old)]
    log_w = [jnp.where(causal, col(fc[b], h) + row_terms[b][h:h + 1, :], -jnp.inf) for b, h in items]
    li = [col(log_inter[b], h) for b, h in items]
    m_t = [jnp.maximum(a, jnp.max(lw, axis=-1, keepdims=True)) for a, lw in zip(li, log_w)]
    w_inter = [jnp.exp(a - m) for a, m in zip(li, m_t)]
    qkw = [r * jnp.exp(lw - m) for r, lw, m in zip(qk_raw, log_w, m_t)]
    num = [_dot(a, vb) + wi * qc for a, vb, wi, qc in zip(qkw, vbs, w_inter, q_c)]
    den = [jnp.sum(a, axis=-1, keepdims=True) + wi * jnp.sum(q * n, axis=-1, keepdims=True)
           for a, wi, q, n in zip(qkw, w_inter, qs, n_old)]
    hid = [nu / jnp.maximum(jnp.abs(de), jnp.exp(-m)) for nu, de, m in zip(num, den, m_t)]
    kw = [k * col(w_last[b], h) for k, (b, h) in zip(ks, items)]
    for i, (b, h) in enumerate(items):
        c_ref[b, h] = col(dec_vec[b], h) * c_old[i] + _dot_tn(kw[i].astype(BF16), vbs[i])
        n_ref[b, h:h + 1, :] = col(dec_vec[b], h) * n_old[i] + jnp.sum(kw[i], axis=0, keepdims=True)
    for i, (b, h) in enumerate(items):
        oh = og_ref[b, :, h * DV_C:(h + 1) * DV_C]
        y_ref[b, :, h * DV_C:(h + 1) * DV_C] = _head_rms(hid[i], nw_ref[...]) * _sigmoid(oh)
    for b in seqs:
        m_ref[b] = m_vec[b]
    cfin_ref[...] = c_ref[...]
    nfin_ref[...] = n_ref[...]
    mfin_ref[...] = m_ref[...]


def _mlstm(qk, v, og, misc, c0, n0, m0, i_bias, f_bias, norm_w, *, nb, l, i_col, f_col):
    n = qk.shape[0]
    seq = n // nb
    nbb = SEQS_PER_STEP if nb % SEQS_PER_STEP == 0 else 1
    tok = lambda b, j: (b, j, 0)
    per_b3 = lambda b, j: (b, 0, 0)
    per_b4 = lambda b, j: (b, 0, 0, 0)
    const2 = lambda b, j: (0, 0)
    m0 = m0.reshape(nb, 1, N_HEADS)
    tokens = [a.reshape(nb, seq, a.shape[1]) for a in (qk, v, og, misc)]
    outs = pl.pallas_call(
        functools.partial(_mlstm_kernel, l=l, nbb=nbb, i_col=i_col, f_col=f_col),
        grid=(nb // nbb, seq // l),
        in_specs=[pl.BlockSpec((nbb, l, a.shape[2]), tok) for a in tokens]
                 + [pl.BlockSpec((nbb,) + c0.shape[1:], per_b4),
                    pl.BlockSpec((nbb,) + n0.shape[1:], per_b3),
                    pl.BlockSpec((nbb, 1, N_HEADS), per_b3),
                    pl.BlockSpec((1, N_HEADS), const2),
                    pl.BlockSpec((1, N_HEADS), const2),
                    pl.BlockSpec((1, DV_C), const2)],
        out_specs=[pl.BlockSpec((nbb, l, v.shape[1]), tok),
                   pl.BlockSpec((nbb,) + c0.shape[1:], per_b4),
                   pl.BlockSpec((nbb,) + n0.shape[1:], per_b3),
                   pl.BlockSpec((nbb, 1, N_HEADS), per_b3)],
        out_shape=[jax.ShapeDtypeStruct((nb, seq, v.shape[1]), F32),
                   jax.ShapeDtypeStruct(c0.shape, F32),
                   jax.ShapeDtypeStruct(n0.shape, F32),
                   jax.ShapeDtypeStruct(m0.shape, F32)],
        scratch_shapes=[pltpu.VMEM((nbb,) + c0.shape[1:], F32),
                        pltpu.VMEM((nbb,) + n0.shape[1:], F32),
                        pltpu.VMEM((nbb, 1, N_HEADS), F32)],
        compiler_params=_params("parallel", "arbitrary"),
        name="mlstm",
    )(*tokens, c0, n0, m0, i_bias.reshape(1, -1), f_bias.reshape(1, -1), norm_w.reshape(1, -1))
    y, c_fin, n_fin, m_fin = outs
    return y.reshape(n, v.shape[1]), c_fin, n_fin, m_fin.reshape(nb, N_HEADS)


def _block_mask(rows, row_group, cols, col_group):
    return (np.arange(rows)[:, None] // row_group == np.arange(cols)[None, :] // col_group).astype(np.float32)


def _cummax_rows(x):
    rows = x.shape[0]
    row = _iota(x.shape, 0)
    sh = 1
    while sh < rows:
        x = jnp.maximum(x, jnp.where(row >= sh, pltpu.roll(x, sh, axis=0), -jnp.inf))
        sh *= 2
    return x


def _dot2(a, b):
    hi, lo = _split(a)
    return jnp.dot(hi, b, preferred_element_type=F32) + jnp.dot(lo, b, preferred_element_type=F32)


def _mlstm_dense_kernel(qk_ref, v_ref, og_ref, misc_ref, c0_ref, n0_ref, m0_ref, ib_ref, fb_ref, nw_ref,
                        el_ref, ev_ref, ek_ref, kmask_ref, vmask_ref, cmask_ref, cmaskb_ref, rms_ref, causal_ref,
                        dsel_ref,
                        y_ref, cfin_ref, nfin_ref, mfin_ref, c_ref, n_ref, m_ref, *, l, i_col, f_col):
    j = pl.program_id(1)
    wq = N_HEADS * DK_C
    nbb = qk_ref.shape[0]
    seqs = range(nbb)

    @pl.when(j == 0)
    def _():
        c_ref[...] = jnp.zeros(c_ref.shape, F32)
        for b in seqs:
            for h in range(N_HEADS):
                c_ref[b, h * DK_C:(h + 1) * DK_C, h * DV_C:(h + 1) * DV_C] = c0_ref[b, h]
        n_ref[...] = n0_ref[...]
        m_ref[...] = m0_ref[...]

    each = lambda f, *xs: [f(*args) for args in zip(*xs)]
    dot = functools.partial(jnp.dot, preferred_element_type=F32)
    spread = lambda x, e_ref: sum(dot(p, e_ref[...]) for p in _split3(x))
    tri = jnp.where(_iota((l, l), 0) >= _iota((l, l), 1), 1.0, 0.0).astype(BF16)
    ig = [misc_ref[b, :, i_col:i_col + N_HEADS] + ib_ref[...] for b in seqs]
    lf = [-_softplus(-(misc_ref[b, :, f_col:f_col + N_HEADS] + fb_ref[...])) for b in seqs]
    fc = each(lambda x: sum(dot(tri, p) for p in _split3(x)), lf)
    m_prev = [m_ref[b] for b in seqs]
    log_inter = each(jnp.add, fc, m_prev)
    a = each(jnp.subtract, ig, fc)
    m_t = each(lambda li, f, x: jnp.maximum(li, f + _cummax_rows(x)), log_inter, fc, a)
    w_inter = each(lambda li, m: jnp.exp(li - m), log_inter, m_t)
    w_last = each(lambda f, i, m: jnp.exp(f[l - 1:l, :] - f + i - m[l - 1:l, :]), fc, ig, m_t)

    key_terms = each(lambda x: jnp.sum(spread(x, el_ref) * dsel_ref[...], axis=0, keepdims=True), a)
    log_w = each(lambda f, kt: jnp.where(causal_ref[...] > 0.5, spread(f, el_ref) + kt, -jnp.inf), fc, key_terms)
    q = [qk_ref[b, :, :wq] for b in seqs]
    q_b = [x.astype(BF16) for x in q]
    k_s = [qk_ref[b, :, wq:] * (DK_C ** -0.5) for b in seqs]
    k_b = [x.astype(BF16) for x in k_s]
    v_b = [v_ref[b].astype(BF16) for b in seqs]
    kt_bd = [jnp.concatenate([x] * N_HEADS, axis=0) * kmask_ref[...] for x in k_b]
    v_bd = [jnp.concatenate([x] * N_HEADS, axis=0) * vmask_ref[...] for x in v_b]
    m_l = each(lambda m: spread(m, el_ref), m_t)
    qkw = each(lambda x, kt, lw, m: (_dot_nt(x, kt) * jnp.exp(lw - m)).astype(BF16), q_b, kt_bd, log_w, m_l)
    c_old = [c_ref[b] for b in seqs]
    n_old = [n_ref[b] for b in seqs]
    wi_s = each(lambda w: spread(w, ev_ref), w_inter)
    q_c = each(lambda x, c: dot(x, c.astype(BF16)), q_b, c_old)
    qn = each(lambda x, n: _dot2(x * n, cmaskb_ref[...]), q, n_old)
    num = each(lambda w, vb, wi, qc: dot(w, vb) + wi * qc, qkw, v_bd, wi_s, q_c)
    den = each(lambda w, wi, x: dot(w, vmask_ref[...]) + wi * x, qkw, wi_s, qn)
    m_v = m_l if l == DV_C else each(lambda m: spread(m, ev_ref), m_t)
    hid = each(lambda nu, de, m: nu / jnp.maximum(jnp.abs(de), jnp.exp(-m)), num, den, m_v)
    ms = each(lambda x: _dot2(x * x, rms_ref[...]), hid)
    for b in seqs:
        y_ref[b] = hid[b] * lax.rsqrt(ms[b] + 1e-6) * nw_ref[...] * _sigmoid(og_ref[b])

    kw = each(lambda k, w: k * spread(w, ek_ref), k_s, w_last)
    dec_rows = [jnp.broadcast_to(w[l - 1:l, :], (SUBLANES, N_HEADS)) for w in w_inter]
    upd = each(lambda x, vb: _dot_tn(x.astype(BF16), vb), kw, v_b)
    for b in seqs:
        c_ref[b] = c_old[b] * spread(dec_rows[b], ev_ref)[0:1, :] + upd[b] * cmask_ref[...]
        n_ref[b] = n_old[b] * spread(dec_rows[b], ek_ref)[0:1, :] + jnp.sum(kw[b], axis=0, keepdims=True)
        m_ref[b] = m_t[b][l - 1:l, :]

    @pl.when(j == pl.num_programs(1) - 1)
    def _():
        for b in seqs:
            for h in range(N_HEADS):
                cfin_ref[b, h] = c_ref[b, h * DK_C:(h + 1) * DK_C, h * DV_C:(h + 1) * DV_C]
        nfin_ref[...] = n_ref[...]
        mfin_ref[...] = m_ref[...]


def _mlstm_dense(qk, v, og, misc, c0, n0, m0, i_bias, f_bias, norm_w, *, nb, l, i_col, f_col):
    n = qk.shape[0]
    seq = n // nb
    wq, wv = N_HEADS * DK_C, N_HEADS * DV_C
    hl = N_HEADS * l
    nbb = SEQS_PER_STEP if nb % SEQS_PER_STEP == 0 else 1
    tok = lambda b, j: (b, j, 0)
    per_b3 = lambda b, j: (b, 0, 0)
    per_b4 = lambda b, j: (b, 0, 0, 0)
    const2 = lambda b, j: (0, 0)
    consts = [jnp.asarray(_block_mask(N_HEADS, 1, hl, l), BF16),
              jnp.asarray(_block_mask(N_HEADS, 1, wv, DV_C), BF16),
              jnp.asarray(_block_mask(N_HEADS, 1, wq, DK_C), BF16),
              jnp.asarray(_block_mask(hl, l, wq, DK_C), BF16),
              jnp.asarray(_block_mask(hl, l, wv, DV_C), BF16),
              jnp.asarray(_block_mask(wq, DK_C, wv, DV_C)),
              jnp.asarray(_block_mask(wq, DK_C, wv, DV_C), BF16),
              jnp.asarray(_block_mask(wv, DV_C, wv, DV_C) / DV_C, BF16),
              jnp.asarray(np.tile(np.tril(np.ones((l, l), np.float32)), (1, N_HEADS))),
              jnp.asarray(np.tile(np.eye(l, dtype=np.float32), (1, N_HEADS)))]
    tokens = [a.reshape(nb, seq, a.shape[1]) for a in (qk, v, og, misc)]
    outs = pl.pallas_call(
        functools.partial(_mlstm_dense_kernel, l=l, i_col=i_col, f_col=f_col),
        grid=(nb // nbb, seq // l),
        in_specs=[pl.BlockSpec((nbb, l, a.shape[2]), tok) for a in tokens]
                 + [pl.BlockSpec((nbb,) + c0.shape[1:], per_b4),
                    pl.BlockSpec((nbb, 1, wq), per_b3),
                    pl.BlockSpec((nbb, 1, N_HEADS), per_b3),
                    pl.BlockSpec((1, N_HEADS), const2),
                    pl.BlockSpec((1, N_HEADS), const2),
                    pl.BlockSpec((1, wv), const2)]
                 + [pl.BlockSpec(c.shape, const2) for c in consts],
        out_specs=[pl.BlockSpec((nbb, l, wv), tok),
                   pl.BlockSpec((nbb,) + c0.shape[1:], per_b4),
                   pl.BlockSpec((nbb, 1, wq), per_b3),
                   pl.BlockSpec((nbb, 1, N_HEADS), per_b3)],
        out_shape=[jax.ShapeDtypeStruct((nb, seq, wv), F32),
                   jax.ShapeDtypeStruct(c0.shape, F32),
                   jax.ShapeDtypeStruct((nb, 1, wq), F32),
                   jax.ShapeDtypeStruct((nb, 1, N_HEADS), F32)],
        scratch_shapes=[pltpu.VMEM((nbb, wq, wv), F32),
                        pltpu.VMEM((nbb, 1, wq), F32),
                        pltpu.VMEM((nbb, 1, N_HEADS), F32)],
        compiler_params=_params("parallel", "arbitrary"),
        name="mlstm",
    )(*tokens, c0, n0.reshape(nb, 1, wq), m0.reshape(nb, 1, N_HEADS), i_bias.reshape(1, -1), f_bias.reshape(1, -1),
      jnp.tile(norm_w, N_HEADS).reshape(1, wv), *consts)
    y, c_fin, n_fin, m_fin = outs
    return y.reshape(n, wv), c_fin, n_fin.reshape(nb, N_HEADS, DK_C), m_fin.reshape(nb, N_HEADS)


def _band_prompt_kernel(q_ref, k_ref, v_ref, bias_ref, o_ref, kb_ref, vt_ref, ot_ref, *, tq, seq):
    g = pl.program_id(1)
    w = q_ref.shape[1]
    pad = BAND_CHUNKS * CHUNK
    lw = pad + tq
    npad = pad // LANES
    per_tile = tq // LANES
    blk = _row_tile(seq, 4 * LANES)

    @pl.when(g == 0)
    def _():
        kb_ref[0:pad, :] = jnp.zeros((pad, w), BF16)
        vt_ref[0:npad] = jnp.zeros((npad, w, LANES), BF16)
        for r0 in range(0, seq, blk):
            kb_ref[pad + r0:pad + r0 + blk, :] = k_ref[0, r0:r0 + blk, :].astype(BF16)
        for j in range(seq // LANES):
            vt_ref[npad + j] = v_ref[0, j * LANES:(j + 1) * LANES, :].T.astype(BF16)

    start = pl.multiple_of(g * tq, tq)
    q = (q_ref[...] * (HEAD_DIM ** -0.5)).astype(BF16)
    before_seq = jnp.where(_iota((lw, tq), 0) >= pad - g * tq, 0.0, NEG)
    for h in range(N_HEADS):
        hs = slice(h * HEAD_DIM, (h + 1) * HEAD_DIM)
        s = _dot_nt(kb_ref[pl.ds(start, lw), hs], q[:, hs]) + (bias_ref[h] + before_seq)
        p = jnp.exp(s - _reduce_rows(s, jnp.maximum, jnp.max))
        pn = (p * (1.0 / _reduce_rows(p, jnp.add, jnp.sum))).astype(BF16)
        acc = None
        for j in range(lw // LANES):
            part = jnp.dot(vt_ref[g * per_tile + j, hs, :], pn[j * LANES:(j + 1) * LANES, :],
                           preferred_element_type=F32)
            acc = part if acc is None else acc + part
        ot_ref[hs, :] = acc
    o_ref[...] = ot_ref[...].T


def _band_prompt(q, k, v, bias, *, nb, tq):
    n, w = q.shape
    seq = k.shape[1]
    nq = seq // tq
    pad = BAND_CHUNKS * CHUNK
    row = lambda b, g: (b * nq + g, 0)
    per_b = lambda b, g: (b, 0, 0)
    return pl.pallas_call(
        functools.partial(_band_prompt_kernel, tq=tq, seq=seq),
        grid=(nb, nq),
        in_specs=[pl.BlockSpec((tq, w), row),
                  pl.BlockSpec((1, seq, w), per_b),
                  pl.BlockSpec((1, seq, w), per_b),
                  pl.BlockSpec(bias.shape, lambda b, g: (0, 0, 0))],
        out_specs=pl.BlockSpec((tq, w), row),
        out_shape=jax.ShapeDtypeStruct(q.shape, F32),
        scratch_shapes=[pltpu.VMEM((pad + seq, w), BF16),
                        pltpu.VMEM(((pad + seq) // LANES, w, LANES), BF16),
                        pltpu.VMEM((w, tq), F32)],
        compiler_params=_params("parallel", "arbitrary"),
        name="band_attention_prompt",
    )(q, k, v, bias)


def _band_sample_kernel(q_ref, kc_ref, vc_ref, kn_ref, vn_ref, biasc_ref, biasn_ref, o_ref):
    pieces = [(lambda h: kc_ref[0, 0, h], lambda h: vc_ref[0, 0, h], biasc_ref),
              (lambda h: kn_ref[0, h], lambda h: vn_ref[0, h], biasn_ref)]
    q = q_ref[...]
    for h in range(N_HEADS):
        hs = slice(h * HEAD_DIM, (h + 1) * HEAD_DIM)
        qh = q[:, hs].astype(BF16)
        scores = [jnp.dot(qh, load_k(h).astype(BF16), preferred_element_type=F32) * (HEAD_DIM ** -0.5) + b_ref[h]
                  for load_k, _, b_ref in pieces]
        m = functools.reduce(jnp.maximum, [jnp.max(s, axis=-1, keepdims=True) for s in scores])
        ps = [jnp.exp(s - m) for s in scores]
        den = sum(jnp.sum(p, axis=-1, keepdims=True) for p in ps)
        o = sum(_dot_nt(p.astype(BF16), load_v(h).astype(BF16)) for p, (_, load_v, _) in zip(ps, pieces))
        o_ref[:, hs] = o / den


def _band_sample(q, kc, vc, layer, kn, vn, bias_c, bias_n, *, nb, tq):
    n, w = q.shape
    row = lambda b: (b, 0)
    const3 = lambda b: (0, 0, 0)
    return pl.pallas_call(
        _band_sample_kernel,
        grid=(nb,),
        in_specs=[pl.BlockSpec((tq, w), row)]
                 + [pl.BlockSpec((1, 1) + a.shape[2:], lambda b: (layer, b, 0, 0, 0)) for a in (kc, vc)]
                 + [pl.BlockSpec((1,) + a.shape[1:], lambda b: (b, 0, 0, 0)) for a in (kn, vn)]
                 + [pl.BlockSpec(bias_c.shape, const3), pl.BlockSpec(bias_n.shape, const3)],
        out_specs=pl.BlockSpec((tq, w), row),
        out_shape=jax.ShapeDtypeStruct(q.shape, F32),
        compiler_params=_params("parallel"),
        name="band_attention_sample",
    )(q, kc, vc, kn, vn, bias_c, bias_n)


def _t5_bucket(rel):
    nb = T5_BUCKETS // 2
    max_exact = nb // 2
    n = jnp.abs(rel)
    n_f = jnp.maximum(n, 1).astype(jnp.float32)
    large = max_exact + (jnp.log(n_f / max_exact) / math.log(T5_MAX_DIST / max_exact) * (nb - max_exact)).astype(jnp.int32)
    large = jnp.minimum(large, nb - 1)
    return jnp.where(rel > 0, nb, 0) + jnp.where(n < max_exact, n, large)


def _toeplitz_bias(fn, n_rows, n_cols):
    n = n_rows + n_cols
    m = np.arange(n)
    f = jnp.transpose(fn(np.where(m < n_cols, m, m - n))).astype(F32)
    flat = jnp.tile(f, (1, n_rows))[:, :n_rows * (n - 1)]
    return flat.reshape(f.shape[0], n_rows, n - 1)[:, :, :n_cols]


def _pack_cols(w, sizes, groups):
    offs = np.concatenate([[0], np.cumsum(sizes)])
    cols, widths = [], []
    for grp in groups:
        width = 0
        for idx in grp:
            cols.append(w[:, offs[idx]:offs[idx + 1]])
            width += sizes[idx]
        pad = (-width) % LANES
        if pad:
            cols.append(jnp.zeros((w.shape[0], pad), w.dtype))
        widths.append(width + pad)
    return jnp.concatenate(cols, axis=1).astype(BF16), tuple(widths)


def _row_tile(n, target):
    t = min(n, target)
    while n % t:
        t //= 2
    return t


def kernel(x_prompt, x_sample, cache_a_k, cache_a_v, cache_a_kidx, state_b_s, state_b_conv, state_c_c, state_c_n, state_c_m, cache_d_k, cache_d_v, state_ffn_conv, w_in_even, w_out_even, t5_bias, b_conv_w, b_a_log, b_dt_bias, b_norm_w, w_in_odd, w_out_odd, c_i_bias, c_f_bias, c_norm_w, d_rel_bias, ln_mix_g, ln_mix_b, ln_ffn_g, ln_ffn_b, ffn_w_up, ffn_conv_w, ffn_w_down):
    bp, sp, d = x_prompt.shape
    bs, ts, _ = x_sample.shape
    depth = ffn_w_up.shape[0]
    past = cache_a_k.shape[2]
    d_win = cache_d_k.shape[2]
    dff = ffn_w_down.shape[1]
    alpha = (2 * depth) ** 0.25
    w_a = N_HEADS * HEAD_DIM
    w_b = N_HEADS * DV_B
    w_c = N_HEADS * DV_C
    qkv_b_w = 2 * N_HEADS * DK_B + w_b
    even_sizes = (w_a, w_a, w_a, N_IDX_HEADS * D_IDX, D_IDX, N_IDX_HEADS, qkv_b_w, N_HEADS, N_HEADS, w_b)
    odd_sizes = (N_HEADS * DK_C, N_HEADS * DK_C, w_c, N_HEADS, N_HEADS, w_c, w_a, w_a, w_a)
    even_groups = ((0,), (1,), (2,), (3,), (6,), (9,), (4, 5, 7, 8))
    wi_col, a_col, b_col = D_IDX, D_IDX + N_IDX_HEADS, D_IDX + N_IDX_HEADS + N_HEADS
    odd_groups = ((0, 1), (2,), (5,), (6,), (7,), (8,), (3, 4))
    i_col, f_col = 0, N_HEADS

    assert sp % CHUNK == 0 and ts <= CHUNK and past % CHUNK == 0 and past >= T5_FAR
    assert (past + ts - 1) // CHUNK == past // CHUNK
    topk_p = min(TOPK_MAX, sp // 4)
    topk_s = min(TOPK_MAX, (past + ts) // 4)
    n_p, n_s = bp * sp, bs * ts
    tm_p = _row_tile(n_p, 512)
    tm_s = _row_tile(n_s, 512)
    tff_p = _row_tile(sp, 512)
    ns_s = _row_tile(bs, max(1, 256 // ts))

    t5 = lambda rel: t5_bias[_t5_bucket(jnp.asarray(rel, jnp.int32))]
    fbias = t5(np.array([-T5_FAR - 1]))
    tq_a = 2 * CHUNK
    nw_a = tq_a + T5_FAR
    assert sp % tq_a == 0 and sp >= nw_a
    nq_a = sp // tq_a
    dsa_groups = max(gr for gr in (4, 2, 1) if nq_a % gr == 0 and (nq_a // gr) * tq_a >= nw_a)
    corr_p = jnp.stack([_toeplitz_bias(lambda dd: t5(-dd - T5_FAR * v) - fbias, nw_a, tq_a) for v in range(2)])
    ln_s = T5_FAR + ts
    nbias_s = _toeplitz_bias(lambda dd: t5(dd - T5_FAR), ts, ln_s)[None]

    lw = BAND_CHUNKS * CHUNK + tq_a
    r_chunk = np.arange(lw)[:, None] // CHUNK
    q_chunk = BAND_CHUNKS + np.arange(tq_a)[None, :] // CHUNK
    band_ok = (r_chunk >= q_chunk - BAND_CHUNKS) & (r_chunk <= q_chunk)
    pos_q = past + np.arange(ts)
    pos_kc = past - d_win + np.arange(d_win)
    def band_valid(pos_k):
        kch, qch = pos_k // CHUNK, pos_q // CHUNK
        return (pos_k[None] >= 0) & (kch[None] >= qch[:, None] - BAND_CHUNKS) & (kch[None] <= qch[:, None])
    def band_bias(table, shift, n_rows, n_cols, valid=None):
        bias = _toeplitz_bias(lambda dd: table[np.clip(dd + shift, -REL_CLIP, REL_CLIP) + REL_CLIP], n_rows, n_cols)
        return bias if valid is None else jnp.where(jnp.asarray(valid)[None], bias, NEG)

    cak_t = jnp.transpose(cache_a_k, (0, 1, 3, 4, 2))
    cav_t = jnp.transpose(cache_a_v, (0, 1, 3, 4, 2))
    caki_t = jnp.transpose(cache_a_kidx, (0, 1, 3, 2))
    cdk_t = jnp.transpose(cache_d_k, (0, 1, 3, 4, 2))
    cdv_t = jnp.transpose(cache_d_v, (0, 1, 3, 4, 2))

    xp = x_prompt.reshape(n_p, d)
    xs = x_sample.reshape(n_s, d)
    outs = {k: [] for k in ("ak_p", "ak_s", "av_p", "av_s", "aki_p", "aki_s", "bs_p", "bs_s", "bc_p", "bc_s",
                            "cc_p", "cc_s", "cn_p", "cn_s", "cm_p", "cm_s", "dk_p", "dk_s", "dv_p", "dv_s",
                            "fc_p", "fc_s")}
    for layer in range(depth):
        if layer % 2 == 0:
            e = layer // 2
            w_in, widths = _pack_cols(w_in_even[e], even_sizes, even_groups)
            w_out = w_out_even[e].astype(BF16)
            qa, ka, va, qi, qkv_b, z_b, misc = _proj(xp, w_in, widths, tm_p)
            o_a = _dsa_prompt(qa, qi, misc, ka.reshape(bp, sp, w_a), va.reshape(bp, sp, w_a),
                              misc.reshape(bp, sp, LANES), corr_p, nb=bp, tq=tq_a, topk=topk_p, wi_col=wi_col,
                              n_groups=dsa_groups)
            y_b, s_b, h_b = _gdn(qkv_b, misc, z_b, jnp.zeros((bp, CONV_B - 1, qkv_b_w), F32),
                                 jnp.zeros((bp, N_HEADS, DK_B, DV_B), F32), b_conv_w[e], b_a_log[e], b_dt_bias[e],
                                 b_norm_w[e], nb=bp, c=CHUNK, a_col=a_col, b_col=b_col)
            xp = _mm_res_ln([o_a, y_b], [w_out[:w_a], w_out[w_a:]], xp, ln_mix_g[layer], ln_mix_b[layer], alpha, tm_p)
            outs["ak_p"].append(ka.reshape(bp, sp, N_HEADS, HEAD_DIM))
            outs["av_p"].append(va.reshape(bp, sp, N_HEADS, HEAD_DIM))
            outs["aki_p"].append(misc[:, :D_IDX].reshape(bp, sp, D_IDX))
            outs["bs_p"].append(s_b)
            outs["bc_p"].append(h_b)
            qa, ka, va, qi, qkv_b, z_b, misc = _proj(xs, w_in, widths, tm_s)
            ki = misc[:, :D_IDX]
            heads_t = lambda u: jnp.transpose(u.reshape(bs, ts, N_HEADS, HEAD_DIM), (0, 2, 3, 1))
            near = (jnp.concatenate([cak_t[e, ..., past - T5_FAR:], heads_t(ka)], axis=-1),
                    jnp.concatenate([cav_t[e, ..., past - T5_FAR:], heads_t(va)], axis=-1),
                    jnp.concatenate([caki_t[e, ..., past - T5_FAR:],
                                     jnp.transpose(ki.reshape(bs, ts, D_IDX), (0, 2, 1))], axis=-1))
            o_a = _dsa_sample(qa, qi, misc, cak_t, cav_t, caki_t, e, near, nbias_s, fbias, nb=bs, tq=ts,
                              topk=topk_s, wi_col=wi_col)
            y_b, s_b, h_b = _gdn(qkv_b, misc, z_b, state_b_conv[e], state_b_s[e], b_conv_w[e], b_a_log[e],
                                 b_dt_bias[e], b_norm_w[e], nb=bs, c=ts, a_col=a_col, b_col=b_col)
            xs = _mm_res_ln([o_a, y_b], [w_out[:w_a], w_out[w_a:]], xs, ln_mix_g[layer], ln_mix_b[layer], alpha, tm_s)
            outs["ak_s"].append(ka.reshape(bs, ts, N_HEADS, HEAD_DIM))
            outs["av_s"].append(va.reshape(bs, ts, N_HEADS, HEAD_DIM))
            outs["aki_s"].append(ki.reshape(bs, ts, D_IDX))
            outs["bs_s"].append(s_b)
            outs["bc_s"].append(h_b)
        else:
            o = layer // 2
            w_in, widths = _pack_cols(w_in_odd[o], odd_sizes, odd_groups)
            w_out = w_out_odd[o].astype(BF16)
            qk_c, v_c, o_c, q_d, k_d, v_d, misc = _proj(xp, w_in, widths, tm_p)
            y_c, c_c, c_n, c_m = _mlstm_dense(qk_c, v_c, o_c, misc,jnp.zeros((bp, N_HEADS, DK_C, DV_C), F32),
                                        jnp.zeros((bp, N_HEADS, DK_C), F32), jnp.zeros((bp, N_HEADS), F32),
                                        c_i_bias[o], c_f_bias[o], c_norm_w[o], nb=bp, l=CHUNK, i_col=i_col, f_col=f_col)
            k3 = k_d.reshape(bp, sp, w_a)
            v3 = v_d.reshape(bp, sp, w_a)
            bias_bp = _toeplitz_bias(
                lambda dd: d_rel_bias[o][np.clip(-dd - BAND_CHUNKS * CHUNK, -REL_CLIP, REL_CLIP) + REL_CLIP], lw, tq_a)
            bias_bp = jnp.where(jnp.asarray(band_ok)[None], bias_bp, NEG)
            o_d = _band_prompt(q_d, k3, v3, bias_bp, nb=bp, tq=tq_a)
            xp = _mm_res_ln([y_c, o_d], [w_out[:w_c], w_out[w_c:]], xp, ln_mix_g[layer], ln_mix_b[layer], alpha, tm_p)
            d_win_p = min(BAND_CHUNKS * CHUNK, sp)
            outs["cc_p"].append(c_c)
            outs["cn_p"].append(c_n)
            outs["cm_p"].append(c_m)
            outs["dk_p"].append(k3[:, sp - d_win_p:].reshape(bp, d_win_p, N_HEADS, HEAD_DIM))
            outs["dv_p"].append(v3[:, sp - d_win_p:].reshape(bp, d_win_p, N_HEADS, HEAD_DIM))
            qk_c, v_c, o_c, q_d, k_d, v_d, misc = _proj(xs, w_in, widths, tm_s)
            y_c, c_c, c_n, c_m = _mlstm_dense(qk_c, v_c, o_c, misc,state_c_c[o], state_c_n[o], state_c_m[o],
                                        c_i_bias[o], c_f_bias[o], c_norm_w[o], nb=bs, l=ts, i_col=i_col, f_col=f_col)
            heads_t = lambda u: jnp.transpose(u.reshape(bs, ts, N_HEADS, HEAD_DIM), (0, 2, 3, 1))
            o_d = _band_sample(q_d, cdk_t, cdv_t, o, heads_t(k_d), heads_t(v_d),
                               band_bias(d_rel_bias[o], -d_win, ts, d_win, band_valid(pos_kc)),
                               band_bias(d_rel_bias[o], 0, ts, ts, band_valid(pos_q)), nb=bs, tq=ts)
            xs = _mm_res_ln([y_c, o_d], [w_out[:w_c], w_out[w_c:]], xs, ln_mix_g[layer], ln_mix_b[layer], alpha, tm_s)
            outs["cc_s"].append(c_c)
            outs["cn_s"].append(c_n)
            outs["cm_s"].append(c_m)
            outs["dk_s"].append(k_d.reshape(bs, ts, N_HEADS, HEAD_DIM))
            outs["dv_s"].append(v_d.reshape(bs, ts, N_HEADS, HEAD_DIM))
        w_up = ffn_w_up[layer].astype(BF16)
        w_down = ffn_w_down[layer].astype(BF16)
        act, hist_p = _ffn_up(xp, w_up, ffn_conv_w[layer], jnp.zeros((bp, CONV_FF - 1, 2 * dff), F32),
                              1, tff_p, sp // tff_p)
        xp = _mm_res_ln([act], [w_down], xp, ln_ffn_g[layer], ln_ffn_b[layer], alpha, tm_p)
        act, hist_s = _ffn_up(xs, w_up, ffn_conv_w[layer], state_ffn_conv[layer], ns_s, ts, 1)
        xs = _mm_res_ln([act], [w_down], xs, ln_ffn_g[layer], ln_ffn_b[layer], alpha, tm_s)
        outs["fc_p"].append(hist_p)
        outs["fc_s"].append(hist_s)

    st = lambda k: jnp.stack(outs[k])
    return (xp.reshape(bp, sp, d), xs.reshape(bs, ts, d),
            st("ak_p"), st("ak_s"), st("av_p"), st("av_s"), st("aki_p"), st("aki_s"),
            st("bs_p"), st("bs_s"), st("bc_p"), st("bc_s"),
            st("cc_p"), st("cc_s"), st("cn_p"), st("cn_s"), st("cm_p"), st("cm_s"),
            st("dk_p"), st("dk_s"), st("dv_p"), st("dv_s"),
            st("fc_p"), st("fc_s"))
```

```python
import functools
import math

import numpy as np
import jax
import jax.numpy as jnp
from jax import lax
from jax.experimental import pallas as pl
from jax.experimental.pallas import tpu as pltpu

F32 = jnp.float32
BF16 = jnp.bfloat16
HI = lax.Precision.HIGHEST

CHUNK = 64
HEAD_DIM = 64
N_HEADS = 8
N_IDX_HEADS = 8
D_IDX = 64
TOPK_MAX = 256
T5_BUCKETS = 32
T5_MAX_DIST = 128
DK_B = 64
DV_B = 64
CONV_B = 4
DK_C = 32
DV_C = 64
BAND_CHUNKS = 8
REL_CLIP = 128
CONV_FF = 3
NEAR_CHUNKS = 3
T5_FAR = 128
SEQS_PER_STEP = 4

LANES = 128
SUBLANES = 8
VMEM_LIMIT = 56 * 1024 * 1024

NEG = -1e30
INT_MIN = -2 ** 31


def _params(*sem):
    return pltpu.CompilerParams(dimension_semantics=sem, vmem_limit_bytes=VMEM_LIMIT)


def _dot(a, b):
    return jnp.dot(a.astype(BF16), b.astype(BF16), preferred_element_type=F32)


def _dot_nt(a, b, precision=None):
    return lax.dot_general(a, b, (((1,), (1,)), ((), ())), precision=precision, preferred_element_type=F32)


def _dot_tn(a, b, precision=None):
    return lax.dot_general(a, b, (((0,), (0,)), ((), ())), precision=precision, preferred_element_type=F32)


def _dot_hi(a, b):
    return jnp.dot(a, b, precision=HI, preferred_element_type=F32)


def _split(a):
    hi = a.astype(BF16)
    return hi, (a - hi.astype(F32)).astype(BF16)


def _split3(a):
    p1 = a.astype(BF16)
    r1 = a - p1.astype(F32)
    p2 = r1.astype(BF16)
    return p1, p2, (r1 - p2.astype(F32)).astype(BF16)


def _dot3(a, b):
    a_hi, a_lo = a
    b_hi, b_lo = b
    d = functools.partial(jnp.dot, preferred_element_type=F32)
    return d(a_hi, b_hi) + (d(a_hi, b_lo) + d(a_lo, b_hi))


def _sigmoid(x):
    return 1.0 / (1.0 + jnp.exp(-x))


def _softplus(x):
    return jnp.maximum(x, 0.0) + jnp.log(1.0 + jnp.exp(-jnp.abs(x)))


def _iota(shape, dim):
    return lax.broadcasted_iota(jnp.int32, shape, dim)


def _proj_kernel(x_ref, w_ref, *out_refs, sizes):
    xb = x_ref[...].astype(BF16)
    off = 0
    for o_ref, size in zip(out_refs, sizes):
        o_ref[...] = jnp.dot(xb, w_ref[:, off:off + size], preferred_element_type=F32)
        off += size


def _proj(x2d, w, sizes, tm):
    n, d = x2d.shape
    return pl.pallas_call(
        functools.partial(_proj_kernel, sizes=sizes),
        grid=(n // tm,),
        in_specs=[pl.BlockSpec((tm, d), lambda i: (i, 0)),
                  pl.BlockSpec((d, sum(sizes)), lambda i: (0, 0))],
        out_specs=[pl.BlockSpec((tm, s), lambda i: (i, 0)) for s in sizes],
        out_shape=[jax.ShapeDtypeStruct((n, s), F32) for s in sizes],
        compiler_params=_params("parallel"),
        name="in_proj",
    )(x2d, w)


def _mm_res_ln_kernel(*refs, nparts, alpha):
    part_refs = refs[:nparts]
    w_refs = refs[nparts:2 * nparts]
    x_ref, g_ref, b_ref, o_ref = refs[2 * nparts:]
    acc = alpha * x_ref[...]
    for p_ref, w_ref in zip(part_refs, w_refs):
        acc = acc + jnp.dot(p_ref[...].astype(BF16), w_ref[...], preferred_element_type=F32)
    mu = jnp.mean(acc, axis=-1, keepdims=True)
    cen = acc - mu
    var = jnp.mean(cen * cen, axis=-1, keepdims=True)
    o_ref[...] = cen * lax.rsqrt(var + 1e-5) * g_ref[...] + b_ref[...]


def _mm_res_ln(parts, ws, x2d, g, b, alpha, tm):
    n, d = x2d.shape
    nparts = len(parts)
    in_specs = ([pl.BlockSpec((tm, p.shape[1]), lambda i: (i, 0)) for p in parts]
                + [pl.BlockSpec(w.shape, lambda i: (0, 0)) for w in ws]
                + [pl.BlockSpec((tm, d), lambda i: (i, 0)),
                   pl.BlockSpec((1, d), lambda i: (0, 0)),
                   pl.BlockSpec((1, d), lambda i: (0, 0))])
    return pl.pallas_call(
        functools.partial(_mm_res_ln_kernel, nparts=nparts, alpha=alpha),
        grid=(n // tm,),
        in_specs=in_specs,
        out_specs=pl.BlockSpec((tm, d), lambda i: (i, 0)),
        out_shape=jax.ShapeDtypeStruct((n, d), F32),
        compiler_params=_params("parallel"),
        name="out_proj_ln",
    )(*parts, *ws, x2d, g.reshape(1, d), b.reshape(1, d))


def _ffn_up_kernel(x_ref, w_ref, cw_ref, hist_ref, act_ref, newhist_ref, ext_ref, *,
                   ns, tt, tiles_per_seq, dff, cc):
    i = pl.program_id(0)
    tm = ns * tt
    hw = CONV_FF - 1
    base = SUBLANES
    if tiles_per_seq == 1:
        ext_ref[:, base - hw:base, :] = hist_ref[...]
    else:
        @pl.when(i % tiles_per_seq == 0)
        def _():
            ext_ref[:, base - hw:base, :] = hist_ref[...]

        @pl.when(i % tiles_per_seq != 0)
        def _():
            ext_ref[:, base - hw:base, :] = ext_ref[:, base + tt - hw:base + tt, :]
    xb = x_ref[...].astype(BF16)
    for j in range(2 * dff // cc):
        cols = slice(j * cc, (j + 1) * cc)
        h = jnp.dot(xb, w_ref[:, cols], preferred_element_type=F32)
        ext_ref[:, base:base + tt, cols] = h.reshape(ns, tt, cc)
    newhist_ref[...] = ext_ref[:, base + tt - hw:base + tt, :]

    def conv(cols):
        acc = None
        for k in range(CONV_FF):
            term = ext_ref[:, base - hw + k:base - hw + k + tt, cols] * cw_ref[k:k + 1, cols]
            acc = term if acc is None else acc + term
        return acc

    for j in range(dff // cc):
        g = conv(slice(j * cc, (j + 1) * cc))
        u = conv(slice(dff + j * cc, dff + (j + 1) * cc))
        act = g * _sigmoid(g) * u
        act_ref[:, j * cc:(j + 1) * cc] = act.reshape(tm, cc).astype(BF16)


def _ffn_up(x2d, w_up, conv_w, hist, ns, tt, tiles_per_seq):
    n, d = x2d.shape
    c2 = w_up.shape[1]
    dff = c2 // 2
    tm = ns * tt
    cc = 256
    hw = CONV_FF - 1
    if tiles_per_seq == 1:
        hist_map = lambda i: (i, 0, 0)
    else:
        hist_map = lambda i: (i // tiles_per_seq, 0, 0)
    return pl.pallas_call(
        functools.partial(_ffn_up_kernel, ns=ns, tt=tt, tiles_per_seq=tiles_per_seq, dff=dff, cc=cc),
        grid=(n // tm,),
        in_specs=[pl.BlockSpec((tm, d), lambda i: (i, 0)),
                  pl.BlockSpec((d, c2), lambda i: (0, 0)),
                  pl.BlockSpec((CONV_FF, c2), lambda i: (0, 0)),
                  pl.BlockSpec((ns, hw, c2), hist_map)],
        out_specs=[pl.BlockSpec((tm, dff), lambda i: (i, 0)),
                   pl.BlockSpec((ns, hw, c2), hist_map)],
        out_shape=[jax.ShapeDtypeStruct((n, dff), BF16),
                   jax.ShapeDtypeStruct(hist.shape, F32)],
        scratch_shapes=[pltpu.VMEM((ns, SUBLANES + tt, c2), F32)],
        compiler_params=_params("arbitrary"),
        name="ffn_up_conv_gate",
    )(x2d, w_up, conv_w, hist)


def _sortable(x):
    b = lax.bitcast_convert_type(x, jnp.int32)
    return b ^ ((b >> 31) & jnp.int32(0x7FFFFFFF))


def _count(mask):
    return jnp.sum(jnp.where(mask, 1.0, 0.0), axis=-1, keepdims=True)


def _dsa_sample_kernel(qa_ref, qi_ref, qm_ref, kf_ref, vf_ref, kif_ref, kn_ref, vn_ref, kin_ref, nbias_ref,
                       fbias_ref, o_ref, self_ref, seln_ref, *, topk, tq, lf, ln, wi_col):
    v = 0
    start = lf - T5_FAR
    qa = qa_ref[...]
    qi = qi_ref[...]
    wi = qm_ref[:, wi_col:wi_col + N_IDX_HEADS] * (N_IDX_HEADS ** -0.5) * (D_IDX ** -0.5)
    kif = kif_ref[0, 0].astype(BF16)
    kinb = kin_ref[0].astype(BF16)

    sc_f = jnp.zeros((tq, lf), F32)
    sc_n = jnp.zeros((tq, ln), F32)
    for n in range(N_IDX_HEADS):
        qn = qi[:, n * D_IDX:(n + 1) * D_IDX].astype(BF16)
        wn = wi[:, n:n + 1]
        sc_f = sc_f + jnp.maximum(jnp.dot(qn, kif, preferred_element_type=F32), 0.0) * wn
        sc_n = sc_n + jnp.maximum(jnp.dot(qn, kinb, preferred_element_type=F32), 0.0) * wn
    adm_f = _iota((tq, lf), 1) < start
    adm_n = nbias_ref[v, 0] > 0.5 * NEG
    key_f = jnp.where(adm_f, _sortable(sc_f), jnp.int32(INT_MIN))
    key_n = jnp.where(adm_n, _sortable(sc_n), jnp.int32(INT_MIN))

    kf32 = float(topk)

    def body(i, t_u):
        cand_u = t_u | lax.shift_left(jnp.int32(1), 31 - i)
        cand_s = cand_u ^ jnp.int32(INT_MIN)
        cnt = _count(key_f >= cand_s) + _count(key_n >= cand_s)
        return jnp.where(cnt >= kf32, cand_u, t_u)

    t_u = lax.fori_loop(0, 32, body, jnp.zeros((tq, 1), jnp.int32))
    thr = t_u ^ jnp.int32(INT_MIN)

    n_gt = _count(key_f > thr) + _count(key_n > thr)
    n_eq = _count(key_f == thr) + _count(key_n == thr)
    need = kf32 - n_gt
    open_row = thr == jnp.int32(INT_MIN)
    conflict = jnp.logical_and(n_eq != need, jnp.logical_not(open_row))
    self_ref[...] = jnp.where(jnp.logical_and(key_f >= thr, adm_f), 0.0, NEG)
    seln_ref[...] = jnp.where(jnp.logical_and(key_n >= thr, adm_n), 0.0, NEG)

    @pl.when(jnp.max(jnp.where(conflict, 1.0, 0.0)) > 0.0)
    def _():
        upper = jnp.where(_iota((LANES, LANES), 0) < _iota((LANES, LANES), 1), 1.0, 0.0).astype(BF16)
        offset = jnp.zeros((tq, 1), F32)
        for ref, key, width in ((self_ref, key_f, lf), (seln_ref, key_n, ln)):
            for j0 in range(0, width, LANES):
                w = min(LANES, width - j0)
                kb = key[:, j0:j0 + w]
                e = jnp.where(kb == thr, 1.0, 0.0)
                rank = offset + jnp.dot(e.astype(BF16), upper[:w, :w], preferred_element_type=F32)
                take = jnp.where(kb > thr, 1.0, jnp.where(rank < need, e, 0.0))
                take = jnp.where(open_row, jnp.where(kb > thr, 1.0, 0.0), take)
                ref[:, j0:j0 + w] = jnp.where(take > 0.5, 0.0, NEG)
                offset = offset + jnp.sum(e, axis=-1, keepdims=True)

    sel_f = self_ref[...]
    sel_n = seln_ref[...]
    for h in range(N_HEADS):
        hs = slice(h * HEAD_DIM, (h + 1) * HEAD_DIM)
        qh = qa[:, hs].astype(BF16)
        s_f = (jnp.dot(qh, kf_ref[0, 0, h].astype(BF16), preferred_element_type=F32) * (HEAD_DIM ** -0.5)
               + fbias_ref[:, h:h + 1] + sel_f)
        s_n = (jnp.dot(qh, kn_ref[0, h].astype(BF16), preferred_element_type=F32) * (HEAD_DIM ** -0.5)
               + nbias_ref[v, h] + sel_n)
        m = jnp.maximum(jnp.max(s_f, axis=-1, keepdims=True), jnp.max(s_n, axis=-1, keepdims=True))
        p_f = jnp.exp(s_f - m)
        p_n = jnp.exp(s_n - m)
        den = jnp.sum(p_f, axis=-1, keepdims=True) + jnp.sum(p_n, axis=-1, keepdims=True)
        o = (_dot_nt(p_f.astype(BF16), vf_ref[0, 0, h].astype(BF16))
             + _dot_nt(p_n.astype(BF16), vn_ref[0, h].astype(BF16)))
        o_ref[:, hs] = o / den


def _dsa_sample(qa, qi, qmisc, kf, vf, kif, layer, near, nbias, fbias, *, nb, tq, topk, wi_col):
    lf = kf.shape[-1]
    ln = nbias.shape[-1]
    row = lambda b: (b, 0)
    args = [qa, qi, qmisc, kf, vf, kif, *near]
    in_specs = ([pl.BlockSpec((tq, a.shape[1]), row) for a in args[:3]]
                + [pl.BlockSpec((1, 1) + a.shape[2:], lambda b, nd=a.ndim: (layer, b) + (0,) * (nd - 2))
                   for a in args[3:6]]
                + [pl.BlockSpec((1,) + a.shape[1:], lambda b, nd=a.ndim: (b,) + (0,) * (nd - 1)) for a in args[6:]]
                + [pl.BlockSpec(nbias.shape, lambda b: (0, 0, 0, 0)),
                   pl.BlockSpec(fbias.shape, lambda b: (0, 0))])
    return pl.pallas_call(
        functools.partial(_dsa_sample_kernel, topk=topk, tq=tq, lf=lf, ln=ln, wi_col=wi_col),
        grid=(nb,),
        in_specs=in_specs,
        out_specs=pl.BlockSpec((tq, qa.shape[1]), row),
        out_shape=jax.ShapeDtypeStruct(qa.shape, F32),
        scratch_shapes=[pltpu.VMEM((tq, lf), F32), pltpu.VMEM((tq, ln), F32)],
        compiler_params=_params("parallel"),
        name="dsa_attention_sample",
    )(*args, nbias, fbias)


def _reduce_rows(x, op, final):
    blk = 8 * SUBLANES
    parts = [x[r0:r0 + blk] for r0 in range(0, x.shape[0], blk)]
    while len(parts) > 1:
        parts = [op(parts[i], parts[i + 1]) for i in range(0, len(parts) - 1, 2)] + parts[len(parts) & ~1:]
    return final(parts[0], axis=0, keepdims=True)


def _dsa_prompt_tile(g, lk, qa_ref, qi_ref, qm_ref, corr_ref, o_ref,
                     kb_ref, vt_ref, kib_ref, key_ref, sel_ref, s_ref, ot_ref, *, topk, tq, wi_col):
    rb = 2 * LANES if lk % (2 * LANES) == 0 else LANES
    nw = corr_ref.shape[2]
    qa = (qa_ref[...] * (HEAD_DIM ** -0.5)).astype(BF16)
    qi = qi_ref[...].astype(BF16)
    wi = qm_ref[:, wi_col:wi_col + N_IDX_HEADS] * (N_IDX_HEADS ** -0.5) * (D_IDX ** -0.5)
    eye_h = jnp.where(_iota((N_IDX_HEADS, N_IDX_HEADS), 0) == _iota((N_IDX_HEADS, N_IDX_HEADS), 1), 1.0, 0.0)
    wi_t = _dot_nt(eye_h, wi, HI)
    q_chunk = (g * tq + _iota((1, tq), 1)) // CHUNK

    for r0 in range(0, lk, rb):
        kib = kib_ref[r0:r0 + rb, :D_IDX]
        acc = jnp.zeros((rb, tq), F32)
        for n in range(N_IDX_HEADS):
            acc = acc + jnp.maximum(_dot_nt(kib, qi[:, n * D_IDX:(n + 1) * D_IDX]), 0.0) * wi_t[n:n + 1, :]
        adm = (r0 + _iota((rb, 1), 0)) // CHUNK <= q_chunk
        key_ref[r0:r0 + rb, :] = jnp.where(adm, _sortable(acc), jnp.int32(INT_MIN))

    kf32 = float(topk)
    count = lambda mask: _reduce_rows(jnp.where(mask, 1.0, 0.0), jnp.add, jnp.sum)

    def body(i, t_u):
        cand_u = t_u | lax.shift_left(jnp.int32(1), 31 - i)
        cand_s = cand_u ^ jnp.int32(INT_MIN)
        return jnp.where(count(key_ref[0:lk, :] >= cand_s) >= kf32, cand_u, t_u)

    t_u = lax.fori_loop(0, 32, body, jnp.zeros((1, tq), jnp.int32))
    thr = t_u ^ jnp.int32(INT_MIN)
    keys = key_ref[0:lk, :]
    need = kf32 - count(keys > thr)
    open_row = thr == jnp.int32(INT_MIN)
    conflict = jnp.logical_and(count(keys == thr) != need, jnp.logical_not(open_row))
    sel_ref[0:lk, :] = jnp.where(keys >= jnp.maximum(thr, jnp.int32(INT_MIN + 1)), 0.0, NEG)

    @pl.when(jnp.max(jnp.where(conflict, 1.0, 0.0)) > 0.0)
    def _():
        below = jnp.where(_iota((LANES, LANES), 1) < _iota((LANES, LANES), 0), 1.0, 0.0).astype(BF16)
        offset = jnp.zeros((1, tq), F32)
        for r0 in range(0, lk, LANES):
            kblk = key_ref[r0:r0 + LANES, :]
            e = jnp.where(kblk == thr, 1.0, 0.0)
            rank = offset + jnp.dot(below, e.astype(BF16), preferred_element_type=F32)
            take = jnp.where(kblk > thr, 1.0, jnp.where(jnp.logical_or(rank >= need, open_row), 0.0, e))
            sel_ref[r0:r0 + LANES, :] = jnp.where(take > 0.5, 0.0, NEG)
            offset = offset + jnp.sum(e, axis=0, keepdims=True)

    v = jnp.minimum(g, 1)
    start = pl.multiple_of(jnp.maximum(g * tq - T5_FAR, 0), LANES)
    for h in range(N_HEADS):
        hs = slice(h * HEAD_DIM, (h + 1) * HEAD_DIM)
        s_ref[0:lk, :] = _dot_nt(kb_ref[0:lk, hs], qa[:, hs]) + sel_ref[0:lk, :]
        s_ref[pl.ds(start, nw), :] += corr_ref[v, h]
        s = s_ref[0:lk, :]
        p = jnp.exp(s - _reduce_rows(s, jnp.maximum, jnp.max))
        pn = (p * (1.0 / _reduce_rows(p, jnp.add, jnp.sum))).astype(BF16)
        ot_ref[hs, :] = jnp.dot(vt_ref[hs, 0:lk], pn, preferred_element_type=F32)
    o_ref[...] = ot_ref[...].T


def _dsa_prompt_kernel(qa_ref, qi_ref, qm_ref, k_ref, v_ref, ki_ref, corr_ref, o_ref,
                       kb_ref, vt_ref, kib_ref, key_ref, sel_ref, s_ref, ot_ref, *, topk, tq, seq, n_groups, wi_col):
    g = pl.program_id(1)
    blk = _row_tile(seq, 4 * LANES)

    @pl.when(g == 0)
    def _():
        for r0 in range(0, seq, blk):
            kb_ref[r0:r0 + blk, :] = k_ref[0, r0:r0 + blk, :].astype(BF16)
            vt_ref[:, r0:r0 + blk] = v_ref[0, r0:r0 + blk, :].T.astype(BF16)
            kib_ref[r0:r0 + blk, :] = ki_ref[0, r0:r0 + blk, :].astype(BF16)

    per_group = (seq // tq) // n_groups
    for grp in range(n_groups):
        @pl.when(g // per_group == grp)
        def _():
            _dsa_prompt_tile(g, (grp + 1) * per_group * tq, qa_ref, qi_ref, qm_ref, corr_ref, o_ref,
                             kb_ref, vt_ref, kib_ref, key_ref, sel_ref, s_ref, ot_ref,
                             topk=topk, tq=tq, wi_col=wi_col)


def _dsa_prompt(qa, qi, qmisc, k, v, ki, corr, *, nb, tq, topk, wi_col, n_groups):
    n, w = qa.shape
    seq = k.shape[1]
    nq = seq // tq
    row = lambda b, g: (b * nq + g, 0)
    per_b = lambda b, g: (b, 0, 0)
    return pl.pallas_call(
        functools.partial(_dsa_prompt_kernel, topk=topk, tq=tq, seq=seq, n_groups=n_groups, wi_col=wi_col),
        grid=(nb, nq),
        in_specs=[pl.BlockSpec((tq, w), row),
                  pl.BlockSpec((tq, qi.shape[1]), row),
                  pl.BlockSpec((tq, qmisc.shape[1]), row),
                  pl.BlockSpec((1, seq, w), per_b),
                  pl.BlockSpec((1, seq, w), per_b),
                  pl.BlockSpec((1, seq, ki.shape[2]), per_b),
                  pl.BlockSpec(corr.shape, lambda b, g: (0, 0, 0, 0))],
        out_specs=pl.BlockSpec((tq, w), row),
        out_shape=jax.ShapeDtypeStruct(qa.shape, F32),
        scratch_shapes=[pltpu.VMEM((seq, w), BF16),
                        pltpu.VMEM((w, seq), BF16),
                        pltpu.VMEM((seq, ki.shape[2]), BF16),
                        pltpu.VMEM((seq, tq), jnp.int32),
                        pltpu.VMEM((seq, tq), F32),
                        pltpu.VMEM((seq, tq), F32),
                        pltpu.VMEM((w, tq), F32)],
        compiler_params=_params("parallel", "arbitrary"),
        name="dsa_attention_prompt",
    )(qa, qi, qmisc, k, v, ki, corr)


def _block_rows(x, op):
    blk = 8 * SUBLANES
    parts = [x[r0:r0 + blk] for r0 in range(0, x.shape[0], blk)]
    while len(parts) > 1:
        parts = [op(parts[i], parts[i + 1]) for i in range(0, len(parts) - 1, 2)] + parts[len(parts) & ~1:]
    return parts[0]


def _dsa_blocks_kernel(qa_ref, qi_ref, qm_ref, k_ref, v_ref, ki_ref, corr_ref, o_ref,
                       kb_ref, vt_ref, kib_ref, key_ref, sel_ref, m_ref, l_ref, ot_ref, *, topk, tq, seq, wi_col):
    g = pl.program_id(1)
    kb = 2 * LANES
    w = qa_ref.shape[1]
    blk = _row_tile(seq, 4 * LANES)

    @pl.when(g == 0)
    def _():
        for r0 in range(0, seq, blk):
            kb_ref[r0:r0 + blk, :] = k_ref[0, r0:r0 + blk, :].astype(BF16)
            kib_ref[r0:r0 + blk, :] = ki_ref[0, r0:r0 + blk, :].astype(BF16)
        for j in range(seq // LANES):
            vt_ref[j] = v_ref[0, j * LANES:(j + 1) * LANES, :].T.astype(BF16)

    top = (g + 1) * tq
    nkb = (top + kb - 1) // kb
    start_of = lambda i: pl.multiple_of(jnp.maximum(top - kb * (i + 1), 0), LANES)

    qa = (qa_ref[...] * (HEAD_DIM ** -0.5)).astype(BF16)
    qi = qi_ref[...].astype(BF16)
    wi = qm_ref[:, wi_col:wi_col + N_IDX_HEADS] * (N_IDX_HEADS ** -0.5) * (D_IDX ** -0.5)
    eye_h = jnp.where(_iota((N_IDX_HEADS, N_IDX_HEADS), 0) == _iota((N_IDX_HEADS, N_IDX_HEADS), 1), 1.0, 0.0)
    wi_t = _dot_nt(eye_h, wi, HI)
    q_chunk = (g * tq + _iota((1, tq), 1)) // CHUNK

    def score_block(i, carry):
        st = start_of(i)
        kib = kib_ref[pl.ds(st, kb), :][:, :D_IDX]
        acc = jnp.zeros((kb, tq), F32)
        for n in range(N_IDX_HEADS):
            acc = acc + jnp.maximum(_dot_nt(kib, qi[:, n * D_IDX:(n + 1) * D_IDX]), 0.0) * wi_t[n:n + 1, :]
        row = st + _iota((kb, 1), 0)
        ok = jnp.logical_and(row // CHUNK <= q_chunk, row < top - kb * i)
        key_ref[i] = jnp.where(ok, _sortable(acc), jnp.int32(INT_MIN))
        return carry

    lax.fori_loop(0, nkb, score_block, 0)

    kf32 = float(topk)

    def count(pred):
        part = lax.fori_loop(0, nkb, lambda i, a: a + _block_rows(jnp.where(pred(key_ref[i]), 1.0, 0.0), jnp.add),
                             jnp.zeros((8 * SUBLANES, tq), F32))
        return jnp.sum(part, axis=0, keepdims=True)

    def search(b, t_u):
        cand_u = t_u | lax.shift_left(jnp.int32(1), 31 - b)
        cand_s = cand_u ^ jnp.int32(INT_MIN)
        return jnp.where(count(lambda x: x >= cand_s) >= kf32, cand_u, t_u)

    t_u = lax.fori_loop(0, 32, search, jnp.zeros((1, tq), jnp.int32))
    thr = t_u ^ jnp.int32(INT_MIN)
    need = kf32 - count(lambda x: x > thr)
    open_row = thr == jnp.int32(INT_MIN)
    conflict = jnp.logical_and(count(lambda x: x == thr) != need, jnp.logical_not(open_row))
    floor = jnp.maximum(thr, jnp.int32(INT_MIN + 1))

    def select_block(i, carry):
        sel_ref[i] = jnp.where(key_ref[i] >= floor, 0.0, NEG)
        return carry

    lax.fori_loop(0, nkb, select_block, 0)

    @pl.when(jnp.max(jnp.where(conflict, 1.0, 0.0)) > 0.0)
    def _():
        below = jnp.where(_iota((LANES, LANES), 1) < _iota((LANES, LANES), 0), 1.0, 0.0).astype(BF16)

        def tie_block(ii, offset):
            i = nkb - 1 - ii
            for r0 in range(0, kb, LANES):
                kblk = key_ref[i, r0:r0 + LANES, :]
                e = jnp.where(kblk == thr, 1.0, 0.0)
                rank = offset + jnp.dot(below, e.astype(BF16), preferred_element_type=F32)
                take = jnp.where(kblk > thr, 1.0, jnp.where(jnp.logical_or(rank >= need, open_row), 0.0, e))
                sel_ref[i, r0:r0 + LANES, :] = jnp.where(take > 0.5, 0.0, NEG)
                offset = offset + jnp.sum(e, axis=0, keepdims=True)
            return offset

        lax.fori_loop(0, nkb, tie_block, jnp.zeros((1, tq), F32))

    v = jnp.minimum(g, 1)

    heads = range(N_HEADS)
    hs = [slice(h * HEAD_DIM, (h + 1) * HEAD_DIM) for h in heads]

    def attend_block(i, first):
        st = start_of(i)
        vblk = st // LANES
        sel = sel_ref[i]
        s = [_dot_nt(kb_ref[pl.ds(st, kb), hs[h]], qa[:, hs[h]]) + sel for h in heads]
        if first:
            s = [s[h] + corr_ref[v, h] for h in heads]
        m_blk = [jnp.max(_block_rows(x, jnp.maximum), axis=0, keepdims=True) for x in s]
        m_old = [m_ref[h:h + 1, :] for h in heads]
        m_new = m_blk if first else [jnp.maximum(a, b) for a, b in zip(m_old, m_blk)]
        p = [jnp.exp(x - m) for x, m in zip(s, m_new)]
        l_blk = [jnp.sum(_block_rows(x, jnp.add), axis=0, keepdims=True) for x in p]
        pb = [x.astype(BF16) for x in p]
        pv = [sum(jnp.dot(vt_ref[vblk + j, hs[h], :], pb[h][j * LANES:(j + 1) * LANES, :],
                          preferred_element_type=F32) for j in range(kb // LANES)) for h in heads]
        for h in heads:
            if first:
                l_ref[h:h + 1, :] = l_blk[h]
                ot_ref[hs[h], :] = pv[h]
            else:
                alpha = jnp.exp(m_old[h] - m_new[h])
                l_ref[h:h + 1, :] = l_ref[h:h + 1, :] * alpha + l_blk[h]
                ot_ref[hs[h], :] = ot_ref[hs[h], :] * alpha + pv[h]
            m_ref[h:h + 1, :] = m_new[h]

    attend_block(0, True)

    def attend_rest(i, carry):
        attend_block(i, False)
        return carry

    lax.fori_loop(1, nkb, attend_rest, 0)
    for h in heads:
        ot_ref[hs[h], :] = ot_ref[hs[h], :] * (1.0 / l_ref[h:h + 1, :])
    o_ref[...] = ot_ref[...].T


def _dsa_blocks(qa, qi, qmisc, k, v, ki, corr, *, nb, tq, topk, wi_col):
    n, w = qa.shape
    seq = k.shape[1]
    nq = seq // tq
    kb = 2 * LANES
    row = lambda b, g: (b * nq + g, 0)
    per_b = lambda b, g: (b, 0, 0)
    return pl.pallas_call(
        functools.partial(_dsa_blocks_kernel, topk=topk, tq=tq, seq=seq, wi_col=wi_col),
        grid=(nb, nq),
        in_specs=[pl.BlockSpec((tq, w), row),
                  pl.BlockSpec((tq, qi.shape[1]), row),
                  pl.BlockSpec((tq, qmisc.shape[1]), row),
                  pl.BlockSpec((1, seq, w), per_b),
                  pl.BlockSpec((1, seq, w), per_b),
                  pl.BlockSpec((1, seq, ki.shape[2]), per_b),
                  pl.BlockSpec(corr.shape, lambda b, g: (0, 0, 0, 0))],
        out_specs=pl.BlockSpec((tq, w), row),
        out_shape=jax.ShapeDtypeStruct(qa.shape, F32),
        scratch_shapes=[pltpu.VMEM((seq, w), BF16),
                        pltpu.VMEM((seq // LANES, w, LANES), BF16),
                        pltpu.VMEM((seq, ki.shape[2]), BF16),
                        pltpu.VMEM((seq // kb, kb, tq), jnp.int32),
                        pltpu.VMEM((seq // kb, kb, tq), F32),
                        pltpu.VMEM((N_HEADS, tq), F32),
                        pltpu.VMEM((N_HEADS, tq), F32),
                        pltpu.VMEM((w, tq), F32)],
        compiler_params=_params("parallel", "arbitrary"),
        name="dsa_attention_prompt",
    )(qa, qi, qmisc, k, v, ki, corr)


def _unit_lower_inverse(mats, n):
    eye = jnp.where(_iota((n, n), 0) == _iota((n, n), 1), 1.0, 0.0)
    ps = [eye - a for a in mats]
    aks = [_split(a) for a in mats]
    k = 1
    while 2 * k < n:
        aks = [_split(_dot3(ak, ak)) for ak in aks]
        ps = [p + _dot3(_split(p), ak) for p, ak in zip(ps, aks)]
        k *= 2
    return ps


def _head_rms(x, w):
    return x * lax.rsqrt(jnp.mean(x * x, axis=-1, keepdims=True) + 1e-6) * w


def _gdn_kernel(qkv_ref, misc_ref, z_ref, hist_ref, s0_ref, cw_ref, alog_ref, dtb_ref, nw_ref,
                y_ref, sfin_ref, newhist_ref, ext_ref, s_ref, *, c, a_col, b_col):
    j = pl.program_id(1)
    hw = CONV_B - 1
    base = SUBLANES
    wq = N_HEADS * DK_B

    @pl.when(j == 0)
    def _():
        ext_ref[base - hw:base, :] = hist_ref[0]
        s_ref[...] = s0_ref[0]

    @pl.when(j > 0)
    def _():
        ext_ref[base - hw:base, :] = ext_ref[base + c - hw:base + c, :]

    ext_ref[base:base + c, :] = qkv_ref[...]
    newhist_ref[0] = ext_ref[base + c - hw:base + c, :]
    conv = None
    for k in range(CONV_B):
        term = ext_ref[base - hw + k:base - hw + k + c, :] * cw_ref[k:k + 1, :]
        conv = term if conv is None else conv + term
    act = conv * _sigmoid(conv)

    beta = _sigmoid(misc_ref[:, b_col:b_col + N_HEADS])
    g = -jnp.exp(alog_ref[...]) * _softplus(misc_ref[:, a_col:a_col + N_HEADS] + dtb_ref[...])
    ri = _iota((c, c), 0)
    ci = _iota((c, c), 1)
    lower = ri >= ci
    strict = ri > ci
    gc = _dot_hi(jnp.where(lower, 1.0, 0.0), g)
    eye_h = jnp.where(_iota((N_HEADS, N_HEADS), 0) == _iota((N_HEADS, N_HEADS), 1), 1.0, 0.0)
    gc_t = _dot_nt(eye_h, gc, HI)
    eg = jnp.exp(gc)
    g_last = gc[c - 1:c, :]
    e_last = jnp.exp(g_last)
    e_rest = jnp.exp(g_last - gc)
    z = z_ref[...]
    heads = range(N_HEADS)
    col = lambda x, h: x[:, h:h + 1]
    qs = [act[:, h * DK_B:(h + 1) * DK_B] for h in heads]
    ks = [act[:, wq + h * DK_B:wq + (h + 1) * DK_B] for h in heads]
    vs = [act[:, 2 * wq + h * DV_B:2 * wq + (h + 1) * DV_B] for h in heads]
    qs = [q * lax.rsqrt(jnp.sum(q * q, axis=-1, keepdims=True) + 1e-6) * (DK_B ** -0.5) for q in qs]
    ks = [k * lax.rsqrt(jnp.sum(k * k, axis=-1, keepdims=True) + 1e-6) for k in ks]
    kbs = [k.astype(BF16) for k in ks]
    decay = [jnp.where(lower, jnp.exp(jnp.where(lower, col(gc, h) - gc_t[h:h + 1, :], 0.0)), 0.0) for h in heads]
    kk = [_dot_nt(kbs[h], kbs[h]) for h in heads]
    attn = [_dot_nt(qs[h].astype(BF16), kbs[h]) * decay[h] for h in heads]
    a_mat = [jnp.where(strict, col(beta, h) * kk[h] * decay[h], 0.0) for h in heads]
    t_mat = [t.astype(BF16) for t in _unit_lower_inverse(a_mat, c)]
    value = [_dot(t_mat[h], vs[h] * col(beta, h)) for h in heads]
    k_cum = [_dot(t_mat[h], ks[h] * (col(beta, h) * col(eg, h))) for h in heads]
    s_old = [s_ref[h] for h in heads]
    sbs = [s.astype(BF16) for s in s_old]
    v_new = [(value[h] - _dot(k_cum[h], sbs[h])).astype(BF16) for h in heads]
    o_inter = [_dot(qs[h] * col(eg, h), sbs[h]) for h in heads]
    o = [o_inter[h] + _dot(attn[h], v_new[h]) for h in heads]
    for h in heads:
        s_ref[h] = s_old[h] * col(e_last, h) + _dot_tn((ks[h] * col(e_rest, h)).astype(BF16), v_new[h])
    for h in heads:
        zh = z[:, h * DV_B:(h + 1) * DV_B]
        y_ref[:, h * DV_B:(h + 1) * DV_B] = _head_rms(o[h], nw_ref[...]) * (zh * _sigmoid(zh))
    sfin_ref[0] = s_ref[...]


def _gdn(qkv, misc, z, hist, s0, conv_w, a_log, dt_bias, norm_w, *, nb, c, a_col, b_col):
    n, wqkv = qkv.shape
    nch = n // (nb * c)
    row = lambda b, j: (b * nch + j, 0)
    per_b3 = lambda b, j: (b, 0, 0)
    per_b4 = lambda b, j: (b, 0, 0, 0)
    const2 = lambda b, j: (0, 0)
    hw = CONV_B - 1
    return pl.pallas_call(
        functools.partial(_gdn_kernel, c=c, a_col=a_col, b_col=b_col),
        grid=(nb, nch),
        in_specs=[pl.BlockSpec((c, wqkv), row),
                  pl.BlockSpec((c, misc.shape[1]), row),
                  pl.BlockSpec((c, z.shape[1]), row),
                  pl.BlockSpec((1, hw, wqkv), per_b3),
                  pl.BlockSpec((1,) + s0.shape[1:], per_b4),
                  pl.BlockSpec((CONV_B, wqkv), const2),
                  pl.BlockSpec((1, N_HEADS), const2),
                  pl.BlockSpec((1, N_HEADS), const2),
                  pl.BlockSpec((1, DV_B), const2)],
        out_specs=[pl.BlockSpec((c, z.shape[1]), row),
                   pl.BlockSpec((1,) + s0.shape[1:], per_b4),
                   pl.BlockSpec((1, hw, wqkv), per_b3)],
        out_shape=[jax.ShapeDtypeStruct(z.shape, F32),
                   jax.ShapeDtypeStruct(s0.shape, F32),
                   jax.ShapeDtypeStruct(hist.shape, F32)],
        scratch_shapes=[pltpu.VMEM((SUBLANES + c, wqkv), F32),
                        pltpu.VMEM(s0.shape[1:], F32)],
        compiler_params=_params("parallel", "arbitrary"),
        name="gated_deltanet",
    )(qkv, misc, z, hist, s0, conv_w, a_log.reshape(1, -1), dt_bias.reshape(1, -1), norm_w.reshape(1, -1))


def _mlstm_kernel(qk_ref, v_ref, og_ref, misc_ref, c0_ref, n0_ref, m0_ref, ib_ref, fb_ref, nw_ref,
                  y_ref, cfin_ref, nfin_ref, mfin_ref, c_ref, n_ref, m_ref, *, l, nbb, i_col, f_col):
    j = pl.program_id(1)
    wq = N_HEADS * DK_C

    @pl.when(j == 0)
    def _():
        c_ref[...] = c0_ref[...]
        n_ref[...] = n0_ref[...]
        m_ref[...] = m0_ref[...]

    causal = _iota((l, l), 0) >= _iota((l, l), 1)
    tri = jnp.where(causal, 1.0, 0.0)
    eye_h = jnp.where(_iota((N_HEADS, N_HEADS), 0) == _iota((N_HEADS, N_HEADS), 1), 1.0, 0.0)
    seqs = range(nbb)
    ig = [misc_ref[b, :, i_col:i_col + N_HEADS] + ib_ref[...] for b in seqs]
    lf = [-_softplus(-(misc_ref[b, :, f_col:f_col + N_HEADS] + fb_ref[...])) for b in seqs]
    fc = [_dot_hi(tri, lf[b]) for b in seqs]
    row_terms = [_dot_nt(eye_h, ig[b] - fc[b], HI) for b in seqs]
    m_prev = [m_ref[b] for b in seqs]
    log_inter = [fc[b] + m_prev[b] for b in seqs]
    f_last = [fc[b][l - 1:l, :] for b in seqs]
    m_vec = [f_last[b] + jnp.maximum(m_prev[b], jnp.max(ig[b] - fc[b], axis=0, keepdims=True)) for b in seqs]
    dec_vec = [jnp.exp(f_last[b] + m_prev[b] - m_vec[b]) for b in seqs]
    w_last = [jnp.exp(f_last[b] - fc[b] + ig[b] - m_vec[b]) for b in seqs]

    items = [(b, h) for b in seqs for h in range(N_HEADS)]
    col = lambda x, h: x[:, h:h + 1]
    qs = [qk_ref[b, :, h * DK_C:(h + 1) * DK_C] for b, h in items]
    ks = [qk_ref[b, :, wq + h * DK_C:wq + (h + 1) * DK_C] * (DK_C ** -0.5) for b, h in items]
    vbs = [v_ref[b, :, h * DV_C:(h + 1) * DV_C].astype(BF16) for b, h in items]
    c_old = [c_ref[b, h] for b, h in items]
    n_old = [n_ref[b, h:h + 1, :] for b, h in items]
    qk_raw = [_dot_nt(q.astype(BF16), k.astype(BF16)) for q, k in zip(qs, ks)]
    q_c = [_dot(q, c) for q, c in zip(qs, c_old)]
    log_w = [jnp.where(causal, col(fc[b], h) + row_terms[b][h:h + 1, :], -jnp.inf) for b, h in items]
    li = [col(log_inter[b], h) for b, h in items]
    m_t = [jnp.maximum(a, jnp.max(lw, axis=-1, keepdims=True)) for a, lw in zip(li, log_w)]
    w_inter = [jnp.exp(a - m) for a, m in zip(li, m_t)]
    qkw = [r * jnp.exp(lw - m) for r, lw, m in zip(qk_raw, log_w, m_t)]
    num = [_dot(a, vb) + wi * qc for a, vb, wi, qc in zip(qkw, vbs, w_inter, q_c)]
    den = [jnp.sum(a, axis=-1, keepdims=True) + wi * jnp.sum(q * n, axis=-1, keepdims=True)
           for a, wi, q, n in zip(qkw, w_inter, qs, n_old)]
    hid = [nu / jnp.maximum(jnp.abs(de), jnp.exp(-m)) for nu, de, m in zip(num, den, m_t)]
    kw = [k * col(w_last[b], h) for k, (b, h) in zip(ks, items)]
    for i, (b, h) in enumerate(items):
        c_ref[b, h] = col(dec_vec[b], h) * c_old[i] + _dot_tn(kw[i].astype(BF16), vbs[i])
        n_ref[b, h:h + 1, :] = col(dec_vec[b], h) * n_old[i] + jnp.sum(kw[i], axis=0, keepdims=True)
    for i, (b, h) in enumerate(items):
        oh = og_ref[b, :, h * DV_C:(h + 1) * DV_C]
        y_ref[b, :, h * DV_C:(h + 1) * DV_C] = _head_rms(hid[i], nw_ref[...]) * _sigmoid(oh)
    for b in seqs:
        m_ref[b] = m_vec[b]
    cfin_ref[...] = c_ref[...]
    nfin_ref[...] = n_ref[...]
    mfin_ref[...] = m_ref[...]


def _mlstm(qk, v, og, misc, c0, n0, m0, i_bias, f_bias, norm_w, *, nb, l, i_col, f_col):
    n = qk.shape[0]
    seq = n // nb
    nbb = SEQS_PER_STEP if nb % SEQS_PER_STEP == 0 else 1
    tok = lambda b, j: (b, j, 0)
    per_b3 = lambda b, j: (b, 0, 0)
    per_b4 = lambda b, j: (b, 0, 0, 0)
    const2 = lambda b, j: (0, 0)
    m0 = m0.reshape(nb, 1, N_HEADS)
    tokens = [a.reshape(nb, seq, a.shape[1]) for a in (qk, v, og, misc)]
    outs = pl.pallas_call(
        functools.partial(_mlstm_kernel, l=l, nbb=nbb, i_col=i_col, f_col=f_col),
        grid=(nb // nbb, seq // l),
        in_specs=[pl.BlockSpec((nbb, l, a.shape[2]), tok) for a in tokens]
                 + [pl.BlockSpec((nbb,) + c0.shape[1:], per_b4),
                    pl.BlockSpec((nbb,) + n0.shape[1:], per_b3),
                    pl.BlockSpec((nbb, 1, N_HEADS), per_b3),
                    pl.BlockSpec((1, N_HEADS), const2),
                    pl.BlockSpec((1, N_HEADS), const2),
                    pl.BlockSpec((1, DV_C), const2)],
        out_specs=[pl.BlockSpec((nbb, l, v.shape[1]), tok),
                   pl.BlockSpec((nbb,) + c0.shape[1:], per_b4),
                   pl.BlockSpec((nbb,) + n0.shape[1:], per_b3),
                   pl.BlockSpec((nbb, 1, N_HEADS), per_b3)],
        out_shape=[jax.ShapeDtypeStruct((nb, seq, v.shape[1]), F32),
                   jax.ShapeDtypeStruct(c0.shape, F32),
                   jax.ShapeDtypeStruct(n0.shape, F32),
                   jax.ShapeDtypeStruct(m0.shape, F32)],
        scratch_shapes=[pltpu.VMEM((nbb,) + c0.shape[1:], F32),
                        pltpu.VMEM((nbb,) + n0.shape[1:], F32),
                        pltpu.VMEM((nbb, 1, N_HEADS), F32)],
        compiler_params=_params("parallel", "arbitrary"),
        name="mlstm",
    )(*tokens, c0, n0, m0, i_bias.reshape(1, -1), f_bias.reshape(1, -1), norm_w.reshape(1, -1))
    y, c_fin, n_fin, m_fin = outs
    return y.reshape(n, v.shape[1]), c_fin, n_fin, m_fin.reshape(nb, N_HEADS)


def _block_mask(rows, row_group, cols, col_group):
    return (np.arange(rows)[:, None] // row_group == np.arange(cols)[None, :] // col_group).astype(np.float32)


def _cummax_rows(x):
    rows = x.shape[0]
    row = _iota(x.shape, 0)
    sh = 1
    while sh < rows:
        x = jnp.maximum(x, jnp.where(row >= sh, pltpu.roll(x, sh, axis=0), -jnp.inf))
        sh *= 2
    return x


def _dot2(a, b):
    hi, lo = _split(a)
    return jnp.dot(hi, b, preferred_element_type=F32) + jnp.dot(lo, b, preferred_element_type=F32)


def _mlstm_dense_kernel(qk_ref, v_ref, og_ref, misc_ref, c0_ref, n0_ref, m0_ref, ib_ref, fb_ref, nw_ref,
                        el_ref, ev_ref, ek_ref, kmask_ref, vmask_ref, cmask_ref, cmaskb_ref, rms_ref, causal_ref,
                        dsel_ref,
                        y_ref, cfin_ref, nfin_ref, mfin_ref, c_ref, n_ref, m_ref, *, l, i_col, f_col):
    j = pl.program_id(1)
    wq = N_HEADS * DK_C
    nbb = qk_ref.shape[0]
    seqs = range(nbb)

    @pl.when(j == 0)
    def _():
        c_ref[...] = jnp.zeros(c_ref.shape, F32)
        for b in seqs:
            for h in range(N_HEADS):
                c_ref[b, h * DK_C:(h + 1) * DK_C, h * DV_C:(h + 1) * DV_C] = c0_ref[b, h]
        n_ref[...] = n0_ref[...]
        m_ref[...] = m0_ref[...]

    each = lambda f, *xs: [f(*args) for args in zip(*xs)]
    dot = functools.partial(jnp.dot, preferred_element_type=F32)
    spread = lambda x, e_ref: sum(dot(p, e_ref[...]) for p in _split3(x))
    tri = jnp.where(_iota((l, l), 0) >= _iota((l, l), 1), 1.0, 0.0).astype(BF16)
    ig = [misc_ref[b, :, i_col:i_col + N_HEADS] + ib_ref[...] for b in seqs]
    lf = [-_softplus(-(misc_ref[b, :, f_col:f_col + N_HEADS] + fb_ref[...])) for b in seqs]
    fc = each(lambda x: sum(dot(tri, p) for p in _split3(x)), lf)
    m_prev = [m_ref[b] for b in seqs]
    log_inter = each(jnp.add, fc, m_prev)
    a = each(jnp.subtract, ig, fc)
    m_t = each(lambda li, f, x: jnp.maximum(li, f + _cummax_rows(x)), log_inter, fc, a)
    w_inter = each(lambda li, m: jnp.exp(li - m), log_inter, m_t)
    w_last = each(lambda f, i, m: jnp.exp(f[l - 1:l, :] - f + i - m[l - 1:l, :]), fc, ig, m_t)

    key_terms = each(lambda x: jnp.sum(spread(x, el_ref) * dsel_ref[...], axis=0, keepdims=True), a)
    log_w = each(lambda f, kt: jnp.where(causal_ref[...] > 0.5, spread(f, el_ref) + kt, -jnp.inf), fc, key_terms)
    q = [qk_ref[b, :, :wq] for b in seqs]
    q_b = [x.astype(BF16) for x in q]
    k_s = [qk_ref[b, :, wq:] * (DK_C ** -0.5) for b in seqs]
    k_b = [x.astype(BF16) for x in k_s]
    v_b = [v_ref[b].astype(BF16) for b in seqs]
    kt_bd = [jnp.concatenate([x] * N_HEADS, axis=0) * kmask_ref[...] for x in k_b]
    v_bd = [jnp.concatenate([x] * N_HEADS, axis=0) * vmask_ref[...] for x in v_b]
    m_l = each(lambda m: spread(m, el_ref), m_t)
    qkw = each(lambda x, kt, lw, m: (_dot_nt(x, kt) * jnp.exp(lw - m)).astype(BF16), q_b, kt_bd, log_w, m_l)
    c_old = [c_ref[b] for b in seqs]
    n_old = [n_ref[b] for b in seqs]
    wi_s = each(lambda w: spread(w, ev_ref), w_inter)
    q_c = each(lambda x, c: dot(x, c.astype(BF16)), q_b, c_old)
    qn = each(lambda x, n: _dot2(x * n, cmaskb_ref[...]), q, n_old)
    num = each(lambda w, vb, wi, qc: dot(w, vb) + wi * qc, qkw, v_bd, wi_s, q_c)
    den = each(lambda w, wi, x: dot(w, vmask_ref[...]) + wi * x, qkw, wi_s, qn)
    m_v = m_l if l == DV_C else each(lambda m: spread(m, ev_ref), m_t)
    hid = each(lambda nu, de, m: nu / jnp.maximum(jnp.abs(de), jnp.exp(-m)), num, den, m_v)
    ms = each(lambda x: _dot2(x * x, rms_ref[...]), hid)
    for b in seqs:
        y_ref[b] = hid[b] * lax.rsqrt(ms[b] + 1e-6) * nw_ref[...] * _sigmoid(og_ref[b])

    kw = each(lambda k, w: k * spread(w, ek_ref), k_s, w_last)
    dec_rows = [jnp.broadcast_to(w[l - 1:l, :], (SUBLANES, N_HEADS)) for w in w_inter]
    upd = each(lambda x, vb: _dot_tn(x.astype(BF16), vb), kw, v_b)
    for b in seqs:
        c_ref[b] = c_old[b] * spread(dec_rows[b], ev_ref)[0:1, :] + upd[b] * cmask_ref[...]
        n_ref[b] = n_old[b] * spread(dec_rows[b], ek_ref)[0:1, :] + jnp.sum(kw[b], axis=0, keepdims=True)
        m_ref[b] = m_t[b][l - 1:l, :]

    @pl.when(j == pl.num_programs(1) - 1)
    def _():
        for b in seqs:
            for h in range(N_HEADS):
                cfin_ref[b, h] = c_ref[b, h * DK_C:(h + 1) * DK_C, h * DV_C:(h + 1) * DV_C]
        nfin_ref[...] = n_ref[...]
        mfin_ref[...] = m_ref[...]


def _mlstm_dense(qk, v, og, misc, c0, n0, m0, i_bias, f_bias, norm_w, *, nb, l, i_col, f_col):
    n = qk.shape[0]
    seq = n // nb
    wq, wv = N_HEADS * DK_C, N_HEADS * DV_C
    hl = N_HEADS * l
    nbb = SEQS_PER_STEP if nb % SEQS_PER_STEP == 0 else 1
    tok = lambda b, j: (b, j, 0)
    per_b3 = lambda b, j: (b, 0, 0)
    per_b4 = lambda b, j: (b, 0, 0, 0)
    const2 = lambda b, j: (0, 0)
    consts = [jnp.asarray(_block_mask(N_HEADS, 1, hl, l), BF16),
              jnp.asarray(_block_mask(N_HEADS, 1, wv, DV_C), BF16),
              jnp.asarray(_block_mask(N_HEADS, 1, wq, DK_C), BF16),
              jnp.asarray(_block_mask(hl, l, wq, DK_C), BF16),
              jnp.asarray(_block_mask(hl, l, wv, DV_C), BF16),
              jnp.asarray(_block_mask(wq, DK_C, wv, DV_C)),
              jnp.asarray(_block_mask(wq, DK_C, wv, DV_C), BF16),
              jnp.asarray(_block_mask(wv, DV_C, wv, DV_C) / DV_C, BF16),
              jnp.asarray(np.tile(np.tril(np.ones((l, l), np.float32)), (1, N_HEADS))),
              jnp.asarray(np.tile(np.eye(l, dtype=np.float32), (1, N_HEADS)))]
    tokens = [a.reshape(nb, seq, a.shape[1]) for a in (qk, v, og, misc)]
    outs = pl.pallas_call(
        functools.partial(_mlstm_dense_kernel, l=l, i_col=i_col, f_col=f_col),
        grid=(nb // nbb, seq // l),
        in_specs=[pl.BlockSpec((nbb, l, a.shape[2]), tok) for a in tokens]
                 + [pl.BlockSpec((nbb,) + c0.shape[1:], per_b4),
                    pl.BlockSpec((nbb, 1, wq), per_b3),
                    pl.BlockSpec((nbb, 1, N_HEADS), per_b3),
                    pl.BlockSpec((1, N_HEADS), const2),
                    pl.BlockSpec((1, N_HEADS), const2),
                    pl.BlockSpec((1, wv), const2)]
                 + [pl.BlockSpec(c.shape, const2) for c in consts],
        out_specs=[pl.BlockSpec((nbb, l, wv), tok),
                   pl.BlockSpec((nbb,) + c0.shape[1:], per_b4),
                   pl.BlockSpec((nbb, 1, wq), per_b3),
                   pl.BlockSpec((nbb, 1, N_HEADS), per_b3)],
        out_shape=[jax.ShapeDtypeStruct((nb, seq, wv), F32),
                   jax.ShapeDtypeStruct(c0.shape, F32),
                   jax.ShapeDtypeStruct((nb, 1, wq), F32),
                   jax.ShapeDtypeStruct((nb, 1, N_HEADS), F32)],
        scratch_shapes=[pltpu.VMEM((nbb, wq, wv), F32),
                        pltpu.VMEM((nbb, 1, wq), F32),
                        pltpu.VMEM((nbb, 1, N_HEADS), F32)],
        compiler_params=_params("parallel", "arbitrary"),
        name="mlstm",
    )(*tokens, c0, n0.reshape(nb, 1, wq), m0.reshape(nb, 1, N_HEADS), i_bias.reshape(1, -1), f_bias.reshape(1, -1),
      jnp.tile(norm_w, N_HEADS).reshape(1, wv), *consts)
    y, c_fin, n_fin, m_fin = outs
    return y.reshape(n, wv), c_fin, n_fin.reshape(nb, N_HEADS, DK_C), m_fin.reshape(nb, N_HEADS)


def _band_prompt_kernel(q_ref, k_ref, v_ref, bias_ref, o_ref, kb_ref, vt_ref, ot_ref, *, tq, seq):
    g = pl.program_id(1)
    w = q_ref.shape[1]
    pad = BAND_CHUNKS * CHUNK
    lw = pad + tq
    npad = pad // LANES
    per_tile = tq // LANES
    blk = _row_tile(seq, 4 * LANES)

    @pl.when(g == 0)
    def _():
        kb_ref[0:pad, :] = jnp.zeros((pad, w), BF16)
        vt_ref[0:npad] = jnp.zeros((npad, w, LANES), BF16)
        for r0 in range(0, seq, blk):
            kb_ref[pad + r0:pad + r0 + blk, :] = k_ref[0, r0:r0 + blk, :].astype(BF16)
        for j in range(seq // LANES):
            vt_ref[npad + j] = v_ref[0, j * LANES:(j + 1) * LANES, :].T.astype(BF16)

    start = pl.multiple_of(g * tq, tq)
    q = (q_ref[...] * (HEAD_DIM ** -0.5)).astype(BF16)
    before_seq = jnp.where(_iota((lw, tq), 0) >= pad - g * tq, 0.0, NEG)
    for h in range(N_HEADS):
        hs = slice(h * HEAD_DIM, (h + 1) * HEAD_DIM)
        s = _dot_nt(kb_ref[pl.ds(start, lw), hs], q[:, hs]) + (bias_ref[h] + before_seq)
        p = jnp.exp(s - _reduce_rows(s, jnp.maximum, jnp.max))
        pn = (p * (1.0 / _reduce_rows(p, jnp.add, jnp.sum))).astype(BF16)
        acc = None
        for j in range(lw // LANES):
            part = jnp.dot(vt_ref[g * per_tile + j, hs, :], pn[j * LANES:(j + 1) * LANES, :],
                           preferred_element_type=F32)
            acc = part if acc is None else acc + part
        ot_ref[hs, :] = acc
    o_ref[...] = ot_ref[...].T


def _band_prompt(q, k, v, bias, *, nb, tq):
    n, w = q.shape
    seq = k.shape[1]
    nq = seq // tq
    pad = BAND_CHUNKS * CHUNK
    row = lambda b, g: (b * nq + g, 0)
    per_b = lambda b, g: (b, 0, 0)
    return pl.pallas_call(
        functools.partial(_band_prompt_kernel, tq=tq, seq=seq),
        grid=(nb, nq),
        in_specs=[pl.BlockSpec((tq, w), row),
                  pl.BlockSpec((1, seq, w), per_b),
                  pl.BlockSpec((1, seq, w), per_b),
                  pl.BlockSpec(bias.shape, lambda b, g: (0, 0, 0))],
        out_specs=pl.BlockSpec((tq, w), row),
        out_shape=jax.ShapeDtypeStruct(q.shape, F32),
        scratch_shapes=[pltpu.VMEM((pad + seq, w), BF16),
                        pltpu.VMEM(((pad + seq) // LANES, w, LANES), BF16),
                        pltpu.VMEM((w, tq), F32)],
        compiler_params=_params("parallel", "arbitrary"),
        name="band_attention_prompt",
    )(q, k, v, bias)


def _band_sample_kernel(q_ref, kc_ref, vc_ref, kn_ref, vn_ref, biasc_ref, biasn_ref, o_ref):
    pieces = [(lambda h: kc_ref[0, 0, h], lambda h: vc_ref[0, 0, h], biasc_ref),
              (lambda h: kn_ref[0, h], lambda h: vn_ref[0, h], biasn_ref)]
    q = q_ref[...]
    for h in range(N_HEADS):
        hs = slice(h * HEAD_DIM, (h + 1) * HEAD_DIM)
        qh = q[:, hs].astype(BF16)
        scores = [jnp.dot(qh, load_k(h).astype(BF16), preferred_element_type=F32) * (HEAD_DIM ** -0.5) + b_ref[h]
                  for load_k, _, b_ref in pieces]
        m = functools.reduce(jnp.maximum, [jnp.max(s, axis=-1, keepdims=True) for s in scores])
        ps = [jnp.exp(s - m) for s in scores]
        den = sum(jnp.sum(p, axis=-1, keepdims=True) for p in ps)
        o = sum(_dot_nt(p.astype(BF16), load_v(h).astype(BF16)) for p, (_, load_v, _) in zip(ps, pieces))
        o_ref[:, hs] = o / den


def _band_sample(q, kc, vc, layer, kn, vn, bias_c, bias_n, *, nb, tq):
    n, w = q.shape
    row = lambda b: (b, 0)
    const3 = lambda b: (0, 0, 0)
    return pl.pallas_call(
        _band_sample_kernel,
        grid=(nb,),
        in_specs=[pl.BlockSpec((tq, w), row)]
                 + [pl.BlockSpec((1, 1) + a.shape[2:], lambda b: (layer, b, 0, 0, 0)) for a in (kc, vc)]
                 + [pl.BlockSpec((1,) + a.shape[1:], lambda b: (b, 0, 0, 0)) for a in (kn, vn)]
                 + [pl.BlockSpec(bias_c.shape, const3), pl.BlockSpec(bias_n.shape, const3)],
        out_specs=pl.BlockSpec((tq, w), row),
        out_shape=jax.ShapeDtypeStruct(q.shape, F32),
        compiler_params=_params("parallel"),
        name="band_attention_sample",
    )(q, kc, vc, kn, vn, bias_c, bias_n)


def _t5_bucket(rel):
    nb = T5_BUCKETS // 2
    max_exact = nb // 2
    n = jnp.abs(rel)
    n_f = jnp.maximum(n, 1).astype(jnp.float32)
    large = max_exact + (jnp.log(n_f / max_exact) / math.log(T5_MAX_DIST / max_exact) * (nb - max_exact)).astype(jnp.int32)
    large = jnp.minimum(large, nb - 1)
    return jnp.where(rel > 0, nb, 0) + jnp.where(n < max_exact, n, large)


def _toeplitz_bias(fn, n_rows, n_cols):
    n = n_rows + n_cols
    m = np.arange(n)
    f = jnp.transpose(fn(np.where(m < n_cols, m, m - n))).astype(F32)
    flat = jnp.tile(f, (1, n_rows))[:, :n_rows * (n - 1)]
    return flat.reshape(f.shape[0], n_rows, n - 1)[:, :, :n_cols]


def _pack_cols(w, sizes, groups):
    offs = np.concatenate([[0], np.cumsum(sizes)])
    cols, widths = [], []
    for grp in groups:
        width = 0
        for idx in grp:
            cols.append(w[:, offs[idx]:offs[idx + 1]])
            width += sizes[idx]
        pad = (-width) % LANES
        if pad:
            cols.append(jnp.zeros((w.shape[0], pad), w.dtype))
        widths.append(width + pad)
    return jnp.concatenate(cols, axis=1).astype(BF16), tuple(widths)


def _row_tile(n, target):
    t = min(n, target)
    while n % t:
        t //= 2
    return t


def kernel(x_prompt, x_sample, cache_a_k, cache_a_v, cache_a_kidx, state_b_s, state_b_conv, state_c_c, state_c_n, state_c_m, cache_d_k, cache_d_v, state_ffn_conv, w_in_even, w_out_even, t5_bias, b_conv_w, b_a_log, b_dt_bias, b_norm_w, w_in_odd, w_out_odd, c_i_bias, c_f_bias, c_norm_w, d_rel_bias, ln_mix_g, ln_mix_b, ln_ffn_g, ln_ffn_b, ffn_w_up, ffn_conv_w, ffn_w_down):
    bp, sp, d = x_prompt.shape
    bs, ts, _ = x_sample.shape
    depth = ffn_w_up.shape[0]
    past = cache_a_k.shape[2]
    d_win = cache_d_k.shape[2]
    dff = ffn_w_down.shape[1]
    alpha = (2 * depth) ** 0.25
    w_a = N_HEADS * HEAD_DIM
    w_b = N_HEADS * DV_B
    w_c = N_HEADS * DV_C
    qkv_b_w = 2 * N_HEADS * DK_B + w_b
    even_sizes = (w_a, w_a, w_a, N_IDX_HEADS * D_IDX, D_IDX, N_IDX_HEADS, qkv_b_w, N_HEADS, N_HEADS, w_b)
    odd_sizes = (N_HEADS * DK_C, N_HEADS * DK_C, w_c, N_HEADS, N_HEADS, w_c, w_a, w_a, w_a)
    even_groups = ((0,), (1,), (2,), (3,), (6,), (9,), (4, 5, 7, 8))
    wi_col, a_col, b_col = D_IDX, D_IDX + N_IDX_HEADS, D_IDX + N_IDX_HEADS + N_HEADS
    odd_groups = ((0, 1), (2,), (5,), (6,), (7,), (8,), (3, 4))
    i_col, f_col = 0, N_HEADS

    assert sp % CHUNK == 0 and ts <= CHUNK and past % CHUNK == 0 and past >= T5_FAR
    assert (past + ts - 1) // CHUNK == past // CHUNK
    topk_p = min(TOPK_MAX, sp // 4)
    topk_s = min(TOPK_MAX, (past + ts) // 4)
    n_p, n_s = bp * sp, bs * ts
    tm_p = _row_tile(n_p, 512)
    tm_s = _row_tile(n_s, 512)
    tff_p = _row_tile(sp, 512)
    ns_s = _row_tile(bs, max(1, 256 // ts))

    t5 = lambda rel: t5_bias[_t5_bucket(jnp.asarray(rel, jnp.int32))]
    fbias = t5(np.array([-T5_FAR - 1]))
    tq_a = 2 * CHUNK
    nw_a = tq_a + T5_FAR
    assert sp % nw_a == 0
    corr_p = jnp.stack([_toeplitz_bias(lambda dd: t5(-dd - T5_FAR * v) - fbias, nw_a, tq_a) for v in range(2)])
    ln_s = T5_FAR + ts
    nbias_s = _toeplitz_bias(lambda dd: t5(dd - T5_FAR), ts, ln_s)[None]

    lw = BAND_CHUNKS * CHUNK + tq_a
    r_chunk = np.arange(lw)[:, None] // CHUNK
    q_chunk = BAND_CHUNKS + np.arange(tq_a)[None, :] // CHUNK
    band_ok = (r_chunk >= q_chunk - BAND_CHUNKS) & (r_chunk <= q_chunk)
    pos_q = past + np.arange(ts)
    pos_kc = past - d_win + np.arange(d_win)
    def band_valid(pos_k):
        kch, qch = pos_k // CHUNK, pos_q // CHUNK
        return (pos_k[None] >= 0) & (kch[None] >= qch[:, None] - BAND_CHUNKS) & (kch[None] <= qch[:, None])
    def band_bias(table, shift, n_rows, n_cols, valid=None):
        bias = _toeplitz_bias(lambda dd: table[np.clip(dd + shift, -REL_CLIP, REL_CLIP) + REL_CLIP], n_rows, n_cols)
        return bias if valid is None else jnp.where(jnp.asarray(valid)[None], bias, NEG)

    cak_t = jnp.transpose(cache_a_k, (0, 1, 3, 4, 2))
    cav_t = jnp.transpose(cache_a_v, (0, 1, 3, 4, 2))
    caki_t = jnp.transpose(cache_a_kidx, (0, 1, 3, 2))
    cdk_t = jnp.transpose(cache_d_k, (0, 1, 3, 4, 2))
    cdv_t = jnp.transpose(cache_d_v, (0, 1, 3, 4, 2))

    xp = x_prompt.reshape(n_p, d)
    xs = x_sample.reshape(n_s, d)
    outs = {k: [] for k in ("ak_p", "ak_s", "av_p", "av_s", "aki_p", "aki_s", "bs_p", "bs_s", "bc_p", "bc_s",
                            "cc_p", "cc_s", "cn_p", "cn_s", "cm_p", "cm_s", "dk_p", "dk_s", "dv_p", "dv_s",
                            "fc_p", "fc_s")}
    for layer in range(depth):
        if layer % 2 == 0:
            e = layer // 2
            w_in, widths = _pack_cols(w_in_even[e], even_sizes, even_groups)
            w_out = w_out_even[e].astype(BF16)
            qa, ka, va, qi, qkv_b, z_b, misc = _proj(xp, w_in, widths, tm_p)
            o_a = _dsa_blocks(qa, qi, misc, ka.reshape(bp, sp, w_a), va.reshape(bp, sp, w_a),
                              misc.reshape(bp, sp, LANES), corr_p, nb=bp, tq=tq_a, topk=topk_p, wi_col=wi_col)
            y_b, s_b, h_b = _gdn(qkv_b, misc, z_b, jnp.zeros((bp, CONV_B - 1, qkv_b_w), F32),
                                 jnp.zeros((bp, N_HEADS, DK_B, DV_B), F32), b_conv_w[e], b_a_log[e], b_dt_bias[e],
                                 b_norm_w[e], nb=bp, c=CHUNK, a_col=a_col, b_col=b_col)
            xp = _mm_res_ln([o_a, y_b], [w_out[:w_a], w_out[w_a:]], xp, ln_mix_g[layer], ln_mix_b[layer], alpha, tm_p)
            outs["ak_p"].append(ka.reshape(bp, sp, N_HEADS, HEAD_DIM))
            outs["av_p"].append(va.reshape(bp, sp, N_HEADS, HEAD_DIM))
            outs["aki_p"].append(misc[:, :D_IDX].reshape(bp, sp, D_IDX))
            outs["bs_p"].append(s_b)
            outs["bc_p"].append(h_b)
            qa, ka, va, qi, qkv_b, z_b, misc = _proj(xs, w_in, widths, tm_s)
            ki = misc[:, :D_IDX]
            heads_t = lambda u: jnp.transpose(u.reshape(bs, ts, N_HEADS, HEAD_DIM), (0, 2, 3, 1))
            near = (jnp.concatenate([cak_t[e, ..., past - T5_FAR:], heads_t(ka)], axis=-1),
                    jnp.concatenate([cav_t[e, ..., past - T5_FAR:], heads_t(va)], axis=-1),
                    jnp.concatenate([caki_t[e, ..., past - T5_FAR:],
                                     jnp.transpose(ki.reshape(bs, ts, D_IDX), (0, 2, 1))], axis=-1))
            o_a = _dsa_sample(qa, qi, misc, cak_t, cav_t, caki_t, e, near, nbias_s, fbias, nb=bs, tq=ts,
                              topk=topk_s, wi_col=wi_col)
            y_b, s_b, h_b = _gdn(qkv_b, misc, z_b, state_b_conv[e], state_b_s[e], b_conv_w[e], b_a_log[e],
                                 b_dt_bias[e], b_norm_w[e], nb=bs, c=ts, a_col=a_col, b_col=b_col)
            xs = _mm_res_ln([o_a, y_b], [w_out[:w_a], w_out[w_a:]], xs, ln_mix_g[layer], ln_mix_b[layer], alpha, tm_s)
            outs["ak_s"].append(ka.reshape(bs, ts, N_HEADS, HEAD_DIM))
            outs["av_s"].append(va.reshape(bs, ts, N_HEADS, HEAD_DIM))
            outs["aki_s"].append(ki.reshape(bs, ts, D_IDX))
            outs["bs_s"].append(s_b)
            outs["bc_s"].append(h_b)
        else:
            o = layer // 2
            w_in, widths = _pack_cols(w_in_odd[o], odd_sizes, odd_groups)
            w_out = w_out_odd[o].astype(BF16)
            qk_c, v_c, o_c, q_d, k_d, v_d, misc = _proj(xp, w_in, widths, tm_p)
            y_c, c_c, c_n, c_m = _mlstm_dense(qk_c, v_c, o_c, misc,jnp.zeros((bp, N_HEADS, DK_C, DV_C), F32),
                                        jnp.zeros((bp, N_HEADS, DK_C), F32), jnp.zeros((bp, N_HEADS), F32),
                                        c_i_bias[o], c_f_bias[o], c_norm_w[o], nb=bp, l=CHUNK, i_col=i_col, f_col=f_col)
            k3 = k_d.reshape(bp, sp, w_a)
            v3 = v_d.reshape(bp, sp, w_a)
            bias_bp = _toeplitz_bias(
                lambda dd: d_rel_bias[o][np.clip(-dd - BAND_CHUNKS * CHUNK, -REL_CLIP, REL_CLIP) + REL_CLIP], lw, tq_a)
            bias_bp = jnp.where(jnp.asarray(band_ok)[None], bias_bp, NEG)
            o_d = _band_prompt(q_d, k3, v3, bias_bp, nb=bp, tq=tq_a)
            xp = _mm_res_ln([y_c, o_d], [w_out[:w_c], w_out[w_c:]], xp, ln_mix_g[layer], ln_mix_b[layer], alpha, tm_p)
            d_win_p = min(BAND_CHUNKS * CHUNK, sp)
            outs["cc_p"].append(c_c)
            outs["cn_p"].append(c_n)
            outs["cm_p"].append(c_m)
            outs["dk_p"].append(k3[:, sp - d_win_p:].reshape(bp, d_win_p, N_HEADS, HEAD_DIM))
            outs["dv_p"].append(v3[:, sp - d_win_p:].reshape(bp, d_win_p, N_HEADS, HEAD_DIM))
            qk_c, v_c, o_c, q_d, k_d, v_d, misc = _proj(xs, w_in, widths, tm_s)
            y_c, c_c, c_n, c_m = _mlstm_dense(qk_c, v_c, o_c, misc,state_c_c[o], state_c_n[o], state_c_m[o],
                                        c_i_bias[o], c_f_bias[o], c_norm_w[o], nb=bs, l=ts, i_col=i_col, f_col=f_col)
            heads_t = lambda u: jnp.transpose(u.reshape(bs, ts, N_HEADS, HEAD_DIM), (0, 2, 3, 1))
            o_d = _band_sample(q_d, cdk_t, cdv_t, o, heads_t(k_d), heads_t(v_d),
                               band_bias(d_rel_bias[o], -d_win, ts, d_win, band_valid(pos_kc)),
                               band_bias(d_rel_bias[o], 0, ts, ts, band_valid(pos_q)), nb=bs, tq=ts)
            xs = _mm_res_ln([y_c, o_d], [w_out[:w_c], w_out[w_c:]], xs, ln_mix_g[layer], ln_mix_b[layer], alpha, tm_s)
            outs["cc_s"].append(c_c)
            outs["cn_s"].append(c_n)
            outs["cm_s"].append(c_m)
            outs["dk_s"].append(k_d.reshape(bs, ts, N_HEADS, HEAD_DIM))
            outs["dv_s"].append(v_d.reshape(bs, ts, N_HEADS, HEAD_DIM))
        w_up = ffn_w_up[layer].astype(BF16)
        w_down = ffn_w_down[layer].astype(BF16)
        act, hist_p = _ffn_up(xp, w_up, ffn_conv_w[layer], jnp.zeros((bp, CONV_FF - 1, 2 * dff), F32),
                              1, tff_p, sp // tff_p)
        xp = _mm_res_ln([act], [w_down], xp, ln_ffn_g[layer], ln_ffn_b[layer], alpha, tm_p)
        act, hist_s = _ffn_up(xs, w_up, ffn_conv_w[layer], state_ffn_conv[layer], ns_s, ts, 1)
        xs = _mm_res_ln([act], [w_down], xs, ln_ffn_g[layer], ln_ffn_b[layer], alpha, tm_s)
        outs["fc_p"].append(hist_p)
        outs["fc_s"].append(hist_s)

    st = lambda k: jnp.stack(outs[k])
    return (xp.reshape(bp, sp, d), xs.reshape(bs, ts, d),
            st("ak_p"), st("ak_s"), st("av_p"), st("av_s"), st("aki_p"), st("aki_s"),
            st("bs_p"), st("bs_s"), st("bc_p"), st("bc_s"),
            st("cc_p"), st("cc_s"), st("cn_p"), st("cn_s"), st("cm_p"), st("cm_s"),
            st("dk_p"), st("dk_s"), st("dv_p"), st("dv_s"),
            st("fc_p"), st("fc_s"))
```

```python
import functools
import math

import numpy as np
import jax
import jax.numpy as jnp
from jax import lax
from jax.experimental import pallas as pl
from jax.experimental.pallas import tpu as pltpu

F32 = jnp.float32
BF16 = jnp.bfloat16
HI = lax.Precision.HIGHEST

CHUNK = 64
HEAD_DIM = 64
N_HEADS = 8
N_IDX_HEADS = 8
D_IDX = 64
TOPK_MAX = 256
T5_BUCKETS = 32
T5_MAX_DIST = 128
DK_B = 64
DV_B = 64
CONV_B = 4
DK_C = 32
DV_C = 64
BAND_CHUNKS = 8
REL_CLIP = 128
CONV_FF = 3
NEAR_CHUNKS = 3
T5_FAR = 128
SEQS_PER_STEP = 4

LANES = 128
SUBLANES = 8
VMEM_LIMIT = 56 * 1024 * 1024

NEG = -1e30
INT_MIN = -2 ** 31


def _params(*sem):
    return pltpu.CompilerParams(dimension_semantics=sem, vmem_limit_bytes=VMEM_LIMIT)


def _dot(a, b):
    return jnp.dot(a.astype(BF16), b.astype(BF16), preferred_element_type=F32)


def _dot_nt(a, b, precision=None):
    return lax.dot_general(a, b, (((1,), (1,)), ((), ())), precision=precision, preferred_element_type=F32)


def _dot_tn(a, b, precision=None):
    return lax.dot_general(a, b, (((0,), (0,)), ((), ())), precision=precision, preferred_element_type=F32)


def _dot_hi(a, b):
    return jnp.dot(a, b, precision=HI, preferred_element_type=F32)


def _split(a):
    hi = a.astype(BF16)
    return hi, (a - hi.astype(F32)).astype(BF16)


def _split3(a):
    p1 = a.astype(BF16)
    r1 = a - p1.astype(F32)
    p2 = r1.astype(BF16)
    return p1, p2, (r1 - p2.astype(F32)).astype(BF16)


def _dot3(a, b):
    a_hi, a_lo = a
    b_hi, b_lo = b
    d = functools.partial(jnp.dot, preferred_element_type=F32)
    return d(a_hi, b_hi) + (d(a_hi, b_lo) + d(a_lo, b_hi))


def _sigmoid(x):
    return 1.0 / (1.0 + jnp.exp(-x))


def _softplus(x):
    return jnp.maximum(x, 0.0) + jnp.log(1.0 + jnp.exp(-jnp.abs(x)))


def _iota(shape, dim):
    return lax.broadcasted_iota(jnp.int32, shape, dim)


def _proj_kernel(x_ref, w_ref, *out_refs, sizes):
    xb = x_ref[...].astype(BF16)
    off = 0
    for o_ref, size in zip(out_refs, sizes):
        o_ref[...] = jnp.dot(xb, w_ref[:, off:off + size], preferred_element_type=F32)
        off += size


def _proj(x2d, w, sizes, tm):
    n, d = x2d.shape
    return pl.pallas_call(
        functools.partial(_proj_kernel, sizes=sizes),
        grid=(n // tm,),
        in_specs=[pl.BlockSpec((tm, d), lambda i: (i, 0)),
                  pl.BlockSpec((d, sum(sizes)), lambda i: (0, 0))],
        out_specs=[pl.BlockSpec((tm, s), lambda i: (i, 0)) for s in sizes],
        out_shape=[jax.ShapeDtypeStruct((n, s), F32) for s in sizes],
        compiler_params=_params("parallel"),
        name="in_proj",
    )(x2d, w)


def _mm_res_ln_kernel(*refs, nparts, alpha):
    part_refs = refs[:nparts]
    w_refs = refs[nparts:2 * nparts]
    x_ref, g_ref, b_ref, o_ref = refs[2 * nparts:]
    acc = alpha * x_ref[...]
    for p_ref, w_ref in zip(part_refs, w_refs):
        acc = acc + jnp.dot(p_ref[...].astype(BF16), w_ref[...], preferred_element_type=F32)
    mu = jnp.mean(acc, axis=-1, keepdims=True)
    cen = acc - mu
    var = jnp.mean(cen * cen, axis=-1, keepdims=True)
    o_ref[...] = cen * lax.rsqrt(var + 1e-5) * g_ref[...] + b_ref[...]


def _mm_res_ln(parts, ws, x2d, g, b, alpha, tm):
    n, d = x2d.shape
    nparts = len(parts)
    in_specs = ([pl.BlockSpec((tm, p.shape[1]), lambda i: (i, 0)) for p in parts]
                + [pl.BlockSpec(w.shape, lambda i: (0, 0)) for w in ws]
                + [pl.BlockSpec((tm, d), lambda i: (i, 0)),
                   pl.BlockSpec((1, d), lambda i: (0, 0)),
                   pl.BlockSpec((1, d), lambda i: (0, 0))])
    return pl.pallas_call(
        functools.partial(_mm_res_ln_kernel, nparts=nparts, alpha=alpha),
        grid=(n // tm,),
        in_specs=in_specs,
        out_specs=pl.BlockSpec((tm, d), lambda i: (i, 0)),
        out_shape=jax.ShapeDtypeStruct((n, d), F32),
        compiler_params=_params("parallel"),
        name="out_proj_ln",
    )(*parts, *ws, x2d, g.reshape(1, d), b.reshape(1, d))


def _ffn_up_kernel(x_ref, w_ref, cw_ref, hist_ref, act_ref, newhist_ref, ext_ref, *,
                   ns, tt, tiles_per_seq, dff, cc):
    i = pl.program_id(0)
    tm = ns * tt
    hw = CONV_FF - 1
    base = SUBLANES
    if tiles_per_seq == 1:
        ext_ref[:, base - hw:base, :] = hist_ref[...]
    else:
        @pl.when(i % tiles_per_seq == 0)
        def _():
            ext_ref[:, base - hw:base, :] = hist_ref[...]

        @pl.when(i % tiles_per_seq != 0)
        def _():
            ext_ref[:, base - hw:base, :] = ext_ref[:, base + tt - hw:base + tt, :]
    xb = x_ref[...].astype(BF16)
    for j in range(2 * dff // cc):
        cols = slice(j * cc, (j + 1) * cc)
        h = jnp.dot(xb, w_ref[:, cols], preferred_element_type=F32)
        ext_ref[:, base:base + tt, cols] = h.reshape(ns, tt, cc)
    newhist_ref[...] = ext_ref[:, base + tt - hw:base + tt, :]

    def conv(cols):
        acc = None
        for k in range(CONV_FF):
            term = ext_ref[:, base - hw + k:base - hw + k + tt, cols] * cw_ref[k:k + 1, cols]
            acc = term if acc is None else acc + term
        return acc

    for j in range(dff // cc):
        g = conv(slice(j * cc, (j + 1) * cc))
        u = conv(slice(dff + j * cc, dff + (j + 1) * cc))
        act = g * _sigmoid(g) * u
        act_ref[:, j * cc:(j + 1) * cc] = act.reshape(tm, cc).astype(BF16)


def _ffn_up(x2d, w_up, conv_w, hist, ns, tt, tiles_per_seq):
    n, d = x2d.shape
    c2 = w_up.shape[1]
    dff = c2 // 2
    tm = ns * tt
    cc = 256
    hw = CONV_FF - 1
    if tiles_per_seq == 1:
        hist_map = lambda i: (i, 0, 0)
    else:
        hist_map = lambda i: (i // tiles_per_seq, 0, 0)
    return pl.pallas_call(
        functools.partial(_ffn_up_kernel, ns=ns, tt=tt, tiles_per_seq=tiles_per_seq, dff=dff, cc=cc),
        grid=(n // tm,),
        in_specs=[pl.BlockSpec((tm, d), lambda i: (i, 0)),
                  pl.BlockSpec((d, c2), lambda i: (0, 0)),
                  pl.BlockSpec((CONV_FF, c2), lambda i: (0, 0)),
                  pl.BlockSpec((ns, hw, c2), hist_map)],
        out_specs=[pl.BlockSpec((tm, dff), lambda i: (i, 0)),
                   pl.BlockSpec((ns, hw, c2), hist_map)],
        out_shape=[jax.ShapeDtypeStruct((n, dff), BF16),
                   jax.ShapeDtypeStruct(hist.shape, F32)],
        scratch_shapes=[pltpu.VMEM((ns, SUBLANES + tt, c2), F32)],
        compiler_params=_params("arbitrary"),
        name="ffn_up_conv_gate",
    )(x2d, w_up, conv_w, hist)


def _sortable(x):
    b = lax.bitcast_convert_type(x, jnp.int32)
    return b ^ ((b >> 31) & jnp.int32(0x7FFFFFFF))


def _count(mask):
    return jnp.sum(jnp.where(mask, 1.0, 0.0), axis=-1, keepdims=True)


def _dsa_sample_kernel(qa_ref, qi_ref, qm_ref, kf_ref, vf_ref, kif_ref, kn_ref, vn_ref, kin_ref, nbias_ref,
                       fbias_ref, o_ref, self_ref, seln_ref, *, topk, tq, lf, ln, wi_col):
    v = 0
    start = lf - T5_FAR
    qa = qa_ref[...]
    qi = qi_ref[...]
    wi = qm_ref[:, wi_col:wi_col + N_IDX_HEADS] * (N_IDX_HEADS ** -0.5) * (D_IDX ** -0.5)
    kif = kif_ref[0, 0].astype(BF16)
    kinb = kin_ref[0].astype(BF16)

    sc_f = jnp.zeros((tq, lf), F32)
    sc_n = jnp.zeros((tq, ln), F32)
    for n in range(N_IDX_HEADS):
        qn = qi[:, n * D_IDX:(n + 1) * D_IDX].astype(BF16)
        wn = wi[:, n:n + 1]
        sc_f = sc_f + jnp.maximum(jnp.dot(qn, kif, preferred_element_type=F32), 0.0) * wn
        sc_n = sc_n + jnp.maximum(jnp.dot(qn, kinb, preferred_element_type=F32), 0.0) * wn
    adm_f = _iota((tq, lf), 1) < start
    adm_n = nbias_ref[v, 0] > 0.5 * NEG
    key_f = jnp.where(adm_f, _sortable(sc_f), jnp.int32(INT_MIN))
    key_n = jnp.where(adm_n, _sortable(sc_n), jnp.int32(INT_MIN))

    kf32 = float(topk)

    def body(i, t_u):
        cand_u = t_u | lax.shift_left(jnp.int32(1), 31 - i)
        cand_s = cand_u ^ jnp.int32(INT_MIN)
        cnt = _count(key_f >= cand_s) + _count(key_n >= cand_s)
        return jnp.where(cnt >= kf32, cand_u, t_u)

    t_u = lax.fori_loop(0, 32, body, jnp.zeros((tq, 1), jnp.int32))
    thr = t_u ^ jnp.int32(INT_MIN)

    n_gt = _count(key_f > thr) + _count(key_n > thr)
    n_eq = _count(key_f == thr) + _count(key_n == thr)
    need = kf32 - n_gt
    open_row = thr == jnp.int32(INT_MIN)
    conflict = jnp.logical_and(n_eq != need, jnp.logical_not(open_row))
    self_ref[...] = jnp.where(jnp.logical_and(key_f >= thr, adm_f), 0.0, NEG)
    seln_ref[...] = jnp.where(jnp.logical_and(key_n >= thr, adm_n), 0.0, NEG)

    @pl.when(jnp.max(jnp.where(conflict, 1.0, 0.0)) > 0.0)
    def _():
        upper = jnp.where(_iota((LANES, LANES), 0) < _iota((LANES, LANES), 1), 1.0, 0.0).astype(BF16)
        offset = jnp.zeros((tq, 1), F32)
        for ref, key, width in ((self_ref, key_f, lf), (seln_ref, key_n, ln)):
            for j0 in range(0, width, LANES):
                w = min(LANES, width - j0)
                kb = key[:, j0:j0 + w]
                e = jnp.where(kb == thr, 1.0, 0.0)
                rank = offset + jnp.dot(e.astype(BF16), upper[:w, :w], preferred_element_type=F32)
                take = jnp.where(kb > thr, 1.0, jnp.where(rank < need, e, 0.0))
                take = jnp.where(open_row, jnp.where(kb > thr, 1.0, 0.0), take)
                ref[:, j0:j0 + w] = jnp.where(take > 0.5, 0.0, NEG)
                offset = offset + jnp.sum(e, axis=-1, keepdims=True)

    sel_f = self_ref[...]
    sel_n = seln_ref[...]
    for h in range(N_HEADS):
        hs = slice(h * HEAD_DIM, (h + 1) * HEAD_DIM)
        qh = qa[:, hs].astype(BF16)
        s_f = (jnp.dot(qh, kf_ref[0, 0, h].astype(BF16), preferred_element_type=F32) * (HEAD_DIM ** -0.5)
               + fbias_ref[:, h:h + 1] + sel_f)
        s_n = (jnp.dot(qh, kn_ref[0, h].astype(BF16), preferred_element_type=F32) * (HEAD_DIM ** -0.5)
               + nbias_ref[v, h] + sel_n)
        m = jnp.maximum(jnp.max(s_f, axis=-1, keepdims=True), jnp.max(s_n, axis=-1, keepdims=True))
        p_f = jnp.exp(s_f - m)
        p_n = jnp.exp(s_n - m)
        den = jnp.sum(p_f, axis=-1, keepdims=True) + jnp.sum(p_n, axis=-1, keepdims=True)
        o = (_dot_nt(p_f.astype(BF16), vf_ref[0, 0, h].astype(BF16))
             + _dot_nt(p_n.astype(BF16), vn_ref[0, h].astype(BF16)))
        o_ref[:, hs] = o / den


def _dsa_sample(qa, qi, qmisc, kf, vf, kif, layer, near, nbias, fbias, *, nb, tq, topk, wi_col):
    lf = kf.shape[-1]
    ln = nbias.shape[-1]
    row = lambda b: (b, 0)
    args = [qa, qi, qmisc, kf, vf, kif, *near]
    in_specs = ([pl.BlockSpec((tq, a.shape[1]), row) for a in args[:3]]
                + [pl.BlockSpec((1, 1) + a.shape[2:], lambda b, nd=a.ndim: (layer, b) + (0,) * (nd - 2))
                   for a in args[3:6]]
                + [pl.BlockSpec((1,) + a.shape[1:], lambda b, nd=a.ndim: (b,) + (0,) * (nd - 1)) for a in args[6:]]
                + [pl.BlockSpec(nbias.shape, lambda b: (0, 0, 0, 0)),
                   pl.BlockSpec(fbias.shape, lambda b: (0, 0))])
    return pl.pallas_call(
        functools.partial(_dsa_sample_kernel, topk=topk, tq=tq, lf=lf, ln=ln, wi_col=wi_col),
        grid=(nb,),
        in_specs=in_specs,
        out_specs=pl.BlockSpec((tq, qa.shape[1]), row),
        out_shape=jax.ShapeDtypeStruct(qa.shape, F32),
        scratch_shapes=[pltpu.VMEM((tq, lf), F32), pltpu.VMEM((tq, ln), F32)],
        compiler_params=_params("parallel"),
        name="dsa_attention_sample",
    )(*args, nbias, fbias)


def _reduce_rows(x, op, final):
    blk = 8 * SUBLANES
    parts = [x[r0:r0 + blk] for r0 in range(0, x.shape[0], blk)]
    while len(parts) > 1:
        parts = [op(parts[i], parts[i + 1]) for i in range(0, len(parts) - 1, 2)] + parts[len(parts) & ~1:]
    return final(parts[0], axis=0, keepdims=True)


def _dsa_prompt_tile(g, lk, qa_ref, qi_ref, qm_ref, corr_ref, o_ref,
                     kb_ref, vt_ref, kib_ref, key_ref, sel_ref, s_ref, ot_ref, *, topk, tq, wi_col):
    rb = 2 * LANES if lk % (2 * LANES) == 0 else LANES
    nw = corr_ref.shape[2]
    qa = (qa_ref[...] * (HEAD_DIM ** -0.5)).astype(BF16)
    qi = qi_ref[...].astype(BF16)
    wi = qm_ref[:, wi_col:wi_col + N_IDX_HEADS] * (N_IDX_HEADS ** -0.5) * (D_IDX ** -0.5)
    eye_h = jnp.where(_iota((N_IDX_HEADS, N_IDX_HEADS), 0) == _iota((N_IDX_HEADS, N_IDX_HEADS), 1), 1.0, 0.0)
    wi_t = _dot_nt(eye_h, wi, HI)
    q_chunk = (g * tq + _iota((1, tq), 1)) // CHUNK

    for r0 in range(0, lk, rb):
        kib = kib_ref[r0:r0 + rb, :D_IDX]
        acc = jnp.zeros((rb, tq), F32)
        for n in range(N_IDX_HEADS):
            acc = acc + jnp.maximum(_dot_nt(kib, qi[:, n * D_IDX:(n + 1) * D_IDX]), 0.0) * wi_t[n:n + 1, :]
        adm = (r0 + _iota((rb, 1), 0)) // CHUNK <= q_chunk
        key_ref[r0:r0 + rb, :] = jnp.where(adm, _sortable(acc), jnp.int32(INT_MIN))

    kf32 = float(topk)
    count = lambda mask: _reduce_rows(jnp.where(mask, 1.0, 0.0), jnp.add, jnp.sum)

    def body(i, t_u):
        cand_u = t_u | lax.shift_left(jnp.int32(1), 31 - i)
        cand_s = cand_u ^ jnp.int32(INT_MIN)
        return jnp.where(count(key_ref[0:lk, :] >= cand_s) >= kf32, cand_u, t_u)

    t_u = lax.fori_loop(0, 32, body, jnp.zeros((1, tq), jnp.int32))
    thr = t_u ^ jnp.int32(INT_MIN)
    keys = key_ref[0:lk, :]
    need = kf32 - count(keys > thr)
    open_row = thr == jnp.int32(INT_MIN)
    conflict = jnp.logical_and(count(keys == thr) != need, jnp.logical_not(open_row))
    sel_ref[0:lk, :] = jnp.where(keys >= jnp.maximum(thr, jnp.int32(INT_MIN + 1)), 0.0, NEG)

    @pl.when(jnp.max(jnp.where(conflict, 1.0, 0.0)) > 0.0)
    def _():
        below = jnp.where(_iota((LANES, LANES), 1) < _iota((LANES, LANES), 0), 1.0, 0.0).astype(BF16)
        offset = jnp.zeros((1, tq), F32)
        for r0 in range(0, lk, LANES):
            kblk = key_ref[r0:r0 + LANES, :]
            e = jnp.where(kblk == thr, 1.0, 0.0)
            rank = offset + jnp.dot(below, e.astype(BF16), preferred_element_type=F32)
            take = jnp.where(kblk > thr, 1.0, jnp.where(jnp.logical_or(rank >= need, open_row), 0.0, e))
            sel_ref[r0:r0 + LANES, :] = jnp.where(take > 0.5, 0.0, NEG)
            offset = offset + jnp.sum(e, axis=0, keepdims=True)

    v = jnp.minimum(g, 1)
    start = pl.multiple_of(jnp.maximum(g * tq - T5_FAR, 0), LANES)
    for h in range(N_HEADS):
        hs = slice(h * HEAD_DIM, (h + 1) * HEAD_DIM)
        s_ref[0:lk, :] = _dot_nt(kb_ref[0:lk, hs], qa[:, hs]) + sel_ref[0:lk, :]
        s_ref[pl.ds(start, nw), :] += corr_ref[v, h]
        s = s_ref[0:lk, :]
        p = jnp.exp(s - _reduce_rows(s, jnp.maximum, jnp.max))
        pn = (p * (1.0 / _reduce_rows(p, jnp.add, jnp.sum))).astype(BF16)
        ot_ref[hs, :] = jnp.dot(vt_ref[hs, 0:lk], pn, preferred_element_type=F32)
    o_ref[...] = ot_ref[...].T


def _dsa_prompt_kernel(qa_ref, qi_ref, qm_ref, k_ref, v_ref, ki_ref, corr_ref, o_ref,
                       kb_ref, vt_ref, kib_ref, key_ref, sel_ref, s_ref, ot_ref, *, topk, tq, seq, n_groups, wi_col):
    g = pl.program_id(1)
    blk = _row_tile(seq, 4 * LANES)

    @pl.when(g == 0)
    def _():
        for r0 in range(0, seq, blk):
            kb_ref[r0:r0 + blk, :] = k_ref[0, r0:r0 + blk, :].astype(BF16)
            vt_ref[:, r0:r0 + blk] = v_ref[0, r0:r0 + blk, :].T.astype(BF16)
            kib_ref[r0:r0 + blk, :] = ki_ref[0, r0:r0 + blk, :].astype(BF16)

    per_group = (seq // tq) // n_groups
    for grp in range(n_groups):
        @pl.when(g // per_group == grp)
        def _():
            _dsa_prompt_tile(g, (grp + 1) * per_group * tq, qa_ref, qi_ref, qm_ref, corr_ref, o_ref,
                             kb_ref, vt_ref, kib_ref, key_ref, sel_ref, s_ref, ot_ref,
                             topk=topk, tq=tq, wi_col=wi_col)


def _dsa_prompt(qa, qi, qmisc, k, v, ki, corr, *, nb, tq, topk, wi_col, n_groups):
    n, w = qa.shape
    seq = k.shape[1]
    nq = seq // tq
    row = lambda b, g: (b * nq + g, 0)
    per_b = lambda b, g: (b, 0, 0)
    return pl.pallas_call(
        functools.partial(_dsa_prompt_kernel, topk=topk, tq=tq, seq=seq, n_groups=n_groups, wi_col=wi_col),
        grid=(nb, nq),
        in_specs=[pl.BlockSpec((tq, w), row),
                  pl.BlockSpec((tq, qi.shape[1]), row),
                  pl.BlockSpec((tq, qmisc.shape[1]), row),
                  pl.BlockSpec((1, seq, w), per_b),
                  pl.BlockSpec((1, seq, w), per_b),
                  pl.BlockSpec((1, seq, ki.shape[2]), per_b),
                  pl.BlockSpec(corr.shape, lambda b, g: (0, 0, 0, 0))],
        out_specs=pl.BlockSpec((tq, w), row),
        out_shape=jax.ShapeDtypeStruct(qa.shape, F32),
        scratch_shapes=[pltpu.VMEM((seq, w), BF16),
                        pltpu.VMEM((w, seq), BF16),
                        pltpu.VMEM((seq, ki.shape[2]), BF16),
                        pltpu.VMEM((seq, tq), jnp.int32),
                        pltpu.VMEM((seq, tq), F32),
                        pltpu.VMEM((seq, tq), F32),
                        pltpu.VMEM((w, tq), F32)],
        compiler_params=_params("parallel", "arbitrary"),
        name="dsa_attention_prompt",
    )(qa, qi, qmisc, k, v, ki, corr)


def _block_rows(x, op):
    blk = 8 * SUBLANES
    parts = [x[r0:r0 + blk] for r0 in range(0, x.shape[0], blk)]
    while len(parts) > 1:
        parts = [op(parts[i], parts[i + 1]) for i in range(0, len(parts) - 1, 2)] + parts[len(parts) & ~1:]
    return parts[0]


def _dsa_blocks_kernel(qa_ref, qi_ref, qm_ref, k_ref, v_ref, ki_ref, corr_ref, o_ref,
                       kb_ref, vt_ref, kib_ref, key_ref, sel_ref, thr_ref, stat_ref, m_ref, l_ref, ot_ref, *,
                       topk, tq, seq, wi_col):
    g = pl.program_id(1)
    kb = 2 * LANES
    w = qa_ref.shape[1]
    blk = _row_tile(seq, 4 * LANES)

    @pl.when(g == 0)
    def _():
        for r0 in range(0, seq, blk):
            kb_ref[r0:r0 + blk, :] = k_ref[0, r0:r0 + blk, :].astype(BF16)
            kib_ref[r0:r0 + blk, :] = ki_ref[0, r0:r0 + blk, :].astype(BF16)
        for j in range(seq // LANES):
            vt_ref[j] = v_ref[0, j * LANES:(j + 1) * LANES, :].T.astype(BF16)

    top = (g + 1) * tq
    nkb = (top + kb - 1) // kb
    start_of = lambda i: pl.multiple_of(jnp.maximum(top - kb * (i + 1), 0), LANES)

    qa = (qa_ref[...] * (HEAD_DIM ** -0.5)).astype(BF16)
    qi = qi_ref[...].astype(BF16)
    wi = qm_ref[:, wi_col:wi_col + N_IDX_HEADS] * (N_IDX_HEADS ** -0.5) * (D_IDX ** -0.5)
    eye_h = jnp.where(_iota((N_IDX_HEADS, N_IDX_HEADS), 0) == _iota((N_IDX_HEADS, N_IDX_HEADS), 1), 1.0, 0.0)
    wi_t = _dot_nt(eye_h, wi, HI)
    q_chunk = (g * tq + _iota((1, tq), 1)) // CHUNK

    def score_block(i, carry):
        st = start_of(i)
        kib = kib_ref[pl.ds(st, kb), :][:, :D_IDX]
        acc = jnp.zeros((kb, tq), F32)
        for n in range(N_IDX_HEADS):
            acc = acc + jnp.maximum(_dot_nt(kib, qi[:, n * D_IDX:(n + 1) * D_IDX]), 0.0) * wi_t[n:n + 1, :]
        row = st + _iota((kb, 1), 0)
        ok = jnp.logical_and(row // CHUNK <= q_chunk, row < top - kb * i)
        key_ref[i] = jnp.where(ok, _sortable(acc), jnp.int32(INT_MIN))
        return carry

    lax.fori_loop(0, nkb, score_block, 0)

    kf32 = float(topk)

    def search_blocks(n_blocks):
        def count(pred):
            parts = [_block_rows(jnp.where(pred(key_ref[i]), 1.0, 0.0), jnp.add) for i in range(n_blocks)]
            while len(parts) > 1:
                parts = [a + b for a, b in zip(parts[0::2], parts[1::2])] + parts[len(parts) & ~1:]
            return jnp.sum(parts[0], axis=0, keepdims=True)

        def search(b, t_u):
            cand_u = t_u | lax.shift_left(jnp.int32(1), 31 - b)
            cand_s = cand_u ^ jnp.int32(INT_MIN)
            return jnp.where(count(lambda x: x >= cand_s) >= kf32, cand_u, t_u)

        t_u = lax.fori_loop(0, 32, search, jnp.zeros((1, tq), jnp.int32))
        thr = t_u ^ jnp.int32(INT_MIN)
        thr_ref[0:1, :] = thr
        stat_ref[0:1, :] = kf32 - count(lambda x: x > thr)
        stat_ref[1:2, :] = count(lambda x: x == thr)

    for n_blocks in range(1, seq // kb + 1):
        pl.when(nkb == n_blocks)(functools.partial(search_blocks, n_blocks))

    thr = thr_ref[0:1, :]
    need = stat_ref[0:1, :]
    open_row = thr == jnp.int32(INT_MIN)
    conflict = jnp.logical_and(stat_ref[1:2, :] != need, jnp.logical_not(open_row))
    floor = jnp.maximum(thr, jnp.int32(INT_MIN + 1))

    def select_block(i, carry):
        sel_ref[i] = jnp.where(key_ref[i] >= floor, 0.0, NEG)
        return carry

    lax.fori_loop(0, nkb, select_block, 0)

    @pl.when(jnp.max(jnp.where(conflict, 1.0, 0.0)) > 0.0)
    def _():
        below = jnp.where(_iota((LANES, LANES), 1) < _iota((LANES, LANES), 0), 1.0, 0.0).astype(BF16)

        def tie_block(ii, offset):
            i = nkb - 1 - ii
            for r0 in range(0, kb, LANES):
                kblk = key_ref[i, r0:r0 + LANES, :]
                e = jnp.where(kblk == thr, 1.0, 0.0)
                rank = offset + jnp.dot(below, e.astype(BF16), preferred_element_type=F32)
                take = jnp.where(kblk > thr, 1.0, jnp.where(jnp.logical_or(rank >= need, open_row), 0.0, e))
                sel_ref[i, r0:r0 + LANES, :] = jnp.where(take > 0.5, 0.0, NEG)
                offset = offset + jnp.sum(e, axis=0, keepdims=True)
            return offset

        lax.fori_loop(0, nkb, tie_block, jnp.zeros((1, tq), F32))

    v = jnp.minimum(g, 1)

    heads = range(N_HEADS)
    hs = [slice(h * HEAD_DIM, (h + 1) * HEAD_DIM) for h in heads]

    def attend_block(i, first):
        st = start_of(i)
        vblk = st // LANES
        sel = sel_ref[i]
        s = [_dot_nt(kb_ref[pl.ds(st, kb), hs[h]], qa[:, hs[h]]) + sel for h in heads]
        if first:
            s = [s[h] + corr_ref[v, h] for h in heads]
        m_blk = [jnp.max(_block_rows(x, jnp.maximum), axis=0, keepdims=True) for x in s]
        m_old = [m_ref[h:h + 1, :] for h in heads]
        m_new = m_blk if first else [jnp.maximum(a, b) for a, b in zip(m_old, m_blk)]
        p = [jnp.exp(x - m) for x, m in zip(s, m_new)]
        l_blk = [jnp.sum(_block_rows(x, jnp.add), axis=0, keepdims=True) for x in p]
        pb = [x.astype(BF16) for x in p]
        pv = [sum(jnp.dot(vt_ref[vblk + j, hs[h], :], pb[h][j * LANES:(j + 1) * LANES, :],
                          preferred_element_type=F32) for j in range(kb // LANES)) for h in heads]
        for h in heads:
            if first:
                l_ref[h:h + 1, :] = l_blk[h]
                ot_ref[hs[h], :] = pv[h]
            else:
                alpha = jnp.exp(m_old[h] - m_new[h])
                l_ref[h:h + 1, :] = l_ref[h:h + 1, :] * alpha + l_blk[h]
                ot_ref[hs[h], :] = ot_ref[hs[h], :] * alpha + pv[h]
            m_ref[h:h + 1, :] = m_new[h]

    attend_block(0, True)

    def attend_rest(i, carry):
        attend_block(i, False)
        return carry

    lax.fori_loop(1, nkb, attend_rest, 0)
    for h in heads:
        ot_ref[hs[h], :] = ot_ref[hs[h], :] * (1.0 / l_ref[h:h + 1, :])
    o_ref[...] = ot_ref[...].T


def _dsa_blocks(qa, qi, qmisc, k, v, ki, corr, *, nb, tq, topk, wi_col):
    n, w = qa.shape
    seq = k.shape[1]
    nq = seq // tq
    kb = 2 * LANES
    row = lambda b, g: (b * nq + g, 0)
    per_b = lambda b, g: (b, 0, 0)
    return pl.pallas_call(
        functools.partial(_dsa_blocks_kernel, topk=topk, tq=tq, seq=seq, wi_col=wi_col),
        grid=(nb, nq),
        in_specs=[pl.BlockSpec((tq, w), row),
                  pl.BlockSpec((tq, qi.shape[1]), row),
                  pl.BlockSpec((tq, qmisc.shape[1]), row),
                  pl.BlockSpec((1, seq, w), per_b),
                  pl.BlockSpec((1, seq, w), per_b),
                  pl.BlockSpec((1, seq, ki.shape[2]), per_b),
                  pl.BlockSpec(corr.shape, lambda b, g: (0, 0, 0, 0))],
        out_specs=pl.BlockSpec((tq, w), row),
        out_shape=jax.ShapeDtypeStruct(qa.shape, F32),
        scratch_shapes=[pltpu.VMEM((seq, w), BF16),
                        pltpu.VMEM((seq // LANES, w, LANES), BF16),
                        pltpu.VMEM((seq, ki.shape[2]), BF16),
                        pltpu.VMEM((seq // kb, kb, tq), jnp.int32),
                        pltpu.VMEM((seq // kb, kb, tq), F32),
                        pltpu.VMEM((SUBLANES, tq), jnp.int32),
                        pltpu.VMEM((SUBLANES, tq), F32),
                        pltpu.VMEM((N_HEADS, tq), F32),
                        pltpu.VMEM((N_HEADS, tq), F32),
                        pltpu.VMEM((w, tq), F32)],
        compiler_params=_params("parallel", "arbitrary"),
        name="dsa_attention_prompt",
    )(qa, qi, qmisc, k, v, ki, corr)


def _unit_lower_inverse(mats, n):
    eye = jnp.where(_iota((n, n), 0) == _iota((n, n), 1), 1.0, 0.0)
    ps = [eye - a for a in mats]
    aks = [_split(a) for a in mats]
    k = 1
    while 2 * k < n:
        aks = [_split(_dot3(ak, ak)) for ak in aks]
        ps = [p + _dot3(_split(p), ak) for p, ak in zip(ps, aks)]
        k *= 2
    return ps


def _head_rms(x, w):
    return x * lax.rsqrt(jnp.mean(x * x, axis=-1, keepdims=True) + 1e-6) * w


def _gdn_kernel(qkv_ref, misc_ref, z_ref, hist_ref, s0_ref, cw_ref, alog_ref, dtb_ref, nw_ref,
                y_ref, sfin_ref, newhist_ref, ext_ref, s_ref, *, c, a_col, b_col):
    j = pl.program_id(1)
    hw = CONV_B - 1
    base = SUBLANES
    wq = N_HEADS * DK_B

    @pl.when(j == 0)
    def _():
        ext_ref[base - hw:base, :] = hist_ref[0]
        s_ref[...] = s0_ref[0]

    @pl.when(j > 0)
    def _():
        ext_ref[base - hw:base, :] = ext_ref[base + c - hw:base + c, :]

    ext_ref[base:base + c, :] = qkv_ref[...]
    newhist_ref[0] = ext_ref[base + c - hw:base + c, :]
    conv = None
    for k in range(CONV_B):
        term = ext_ref[base - hw + k:base - hw + k + c, :] * cw_ref[k:k + 1, :]
        conv = term if conv is None else conv + term
    act = conv * _sigmoid(conv)

    beta = _sigmoid(misc_ref[:, b_col:b_col + N_HEADS])
    g = -jnp.exp(alog_ref[...]) * _softplus(misc_ref[:, a_col:a_col + N_HEADS] + dtb_ref[...])
    ri = _iota((c, c), 0)
    ci = _iota((c, c), 1)
    lower = ri >= ci
    strict = ri > ci
    gc = _dot_hi(jnp.where(lower, 1.0, 0.0), g)
    eye_h = jnp.where(_iota((N_HEADS, N_HEADS), 0) == _iota((N_HEADS, N_HEADS), 1), 1.0, 0.0)
    gc_t = _dot_nt(eye_h, gc, HI)
    eg = jnp.exp(gc)
    g_last = gc[c - 1:c, :]
    e_last = jnp.exp(g_last)
    e_rest = jnp.exp(g_last - gc)
    z = z_ref[...]
    heads = range(N_HEADS)
    col = lambda x, h: x[:, h:h + 1]
    qs = [act[:, h * DK_B:(h + 1) * DK_B] for h in heads]
    ks = [act[:, wq + h * DK_B:wq + (h + 1) * DK_B] for h in heads]
    vs = [act[:, 2 * wq + h * DV_B:2 * wq + (h + 1) * DV_B] for h in heads]
    qs = [q * lax.rsqrt(jnp.sum(q * q, axis=-1, keepdims=True) + 1e-6) * (DK_B ** -0.5) for q in qs]
    ks = [k * lax.rsqrt(jnp.sum(k * k, axis=-1, keepdims=True) + 1e-6) for k in ks]
    kbs = [k.astype(BF16) for k in ks]
    decay = [jnp.where(lower, jnp.exp(jnp.where(lower, col(gc, h) - gc_t[h:h + 1, :], 0.0)), 0.0) for h in heads]
    kk = [_dot_nt(kbs[h], kbs[h]) for h in heads]
    attn = [_dot_nt(qs[h].astype(BF16), kbs[h]) * decay[h] for h in heads]
    a_mat = [jnp.where(strict, col(beta, h) * kk[h] * decay[h], 0.0) for h in heads]
    t_mat = [t.astype(BF16) for t in _unit_lower_inverse(a_mat, c)]
    value = [_dot(t_mat[h], vs[h] * col(beta, h)) for h in heads]
    k_cum = [_dot(t_mat[h], ks[h] * (col(beta, h) * col(eg, h))) for h in heads]
    s_old = [s_ref[h] for h in heads]
    sbs = [s.astype(BF16) for s in s_old]
    v_new = [(value[h] - _dot(k_cum[h], sbs[h])).astype(BF16) for h in heads]
    o_inter = [_dot(qs[h] * col(eg, h), sbs[h]) for h in heads]
    o = [o_inter[h] + _dot(attn[h], v_new[h]) for h in heads]
    for h in heads:
        s_ref[h] = s_old[h] * col(e_last, h) + _dot_tn((ks[h] * col(e_rest, h)).astype(BF16), v_new[h])
    for h in heads:
        zh = z[:, h * DV_B:(h + 1) * DV_B]
        y_ref[:, h * DV_B:(h + 1) * DV_B] = _head_rms(o[h], nw_ref[...]) * (zh * _sigmoid(zh))
    sfin_ref[0] = s_ref[...]


def _gdn(qkv, misc, z, hist, s0, conv_w, a_log, dt_bias, norm_w, *, nb, c, a_col, b_col):
    n, wqkv = qkv.shape
    nch = n // (nb * c)
    row = lambda b, j: (b * nch + j, 0)
    per_b3 = lambda b, j: (b, 0, 0)
    per_b4 = lambda b, j: (b, 0, 0, 0)
    const2 = lambda b, j: (0, 0)
    hw = CONV_B - 1
    return pl.pallas_call(
        functools.partial(_gdn_kernel, c=c, a_col=a_col, b_col=b_col),
        grid=(nb, nch),
        in_specs=[pl.BlockSpec((c, wqkv), row),
                  pl.BlockSpec((c, misc.shape[1]), row),
                  pl.BlockSpec((c, z.shape[1]), row),
                  pl.BlockSpec((1, hw, wqkv), per_b3),
                  pl.BlockSpec((1,) + s0.shape[1:], per_b4),
                  pl.BlockSpec((CONV_B, wqkv), const2),
                  pl.BlockSpec((1, N_HEADS), const2),
                  pl.BlockSpec((1, N_HEADS), const2),
                  pl.BlockSpec((1, DV_B), const2)],
        out_specs=[pl.BlockSpec((c, z.shape[1]), row),
                   pl.BlockSpec((1,) + s0.shape[1:], per_b4),
                   pl.BlockSpec((1, hw, wqkv), per_b3)],
        out_shape=[jax.ShapeDtypeStruct(z.shape, F32),
                   jax.ShapeDtypeStruct(s0.shape, F32),
                   jax.ShapeDtypeStruct(hist.shape, F32)],
        scratch_shapes=[pltpu.VMEM((SUBLANES + c, wqkv), F32),
                        pltpu.VMEM(s0.shape[1:], F32)],
        compiler_params=_params("parallel", "arbitrary"),
        name="gated_deltanet",
    )(qkv, misc, z, hist, s0, conv_w, a_log.reshape(1, -1), dt_bias.reshape(1, -1), norm_w.reshape(1, -1))


def _mlstm_kernel(qk_ref, v_ref, og_ref, misc_ref, c0_ref, n0_ref, m0_ref, ib_ref, fb_ref, nw_ref,
                  y_ref, cfin_ref, nfin_ref, mfin_ref, c_ref, n_ref, m_ref, *, l, nbb, i_col, f_col):
    j = pl.program_id(1)
    wq = N_HEADS * DK_C

    @pl.when(j == 0)
    def _():
        c_ref[...] = c0_ref[...]
        n_ref[...] = n0_ref[...]
        m_ref[...] = m0_ref[...]

    causal = _iota((l, l), 0) >= _iota((l, l), 1)
    tri = jnp.where(causal, 1.0, 0.0)
    eye_h = jnp.where(_iota((N_HEADS, N_HEADS), 0) == _iota((N_HEADS, N_HEADS), 1), 1.0, 0.0)
    seqs = range(nbb)
    ig = [misc_ref[b, :, i_col:i_col + N_HEADS] + ib_ref[...] for b in seqs]
    lf = [-_softplus(-(misc_ref[b, :, f_col:f_col + N_HEADS] + fb_ref[...])) for b in seqs]
    fc = [_dot_hi(tri, lf[b]) for b in seqs]
    row_terms = [_dot_nt(eye_h, ig[b] - fc[b], HI) for b in seqs]
    m_prev = [m_ref[b] for b in seqs]
    log_inter = [fc[b] + m_prev[b] for b in seqs]
    f_last = [fc[b][l - 1:l, :] for b in seqs]
    m_vec = [f_last[b] + jnp.maximum(m_prev[b], jnp.max(ig[b] - fc[b], axis=0, keepdims=True)) for b in seqs]
    dec_vec = [jnp.exp(f_last[b] + m_prev[b] - m_vec[b]) for b in seqs]
    w_last = [jnp.exp(f_last[b] - fc[b] + ig[b] - m_vec[b]) for b in seqs]

    items = [(b, h) for b in seqs for h in range(N_HEADS)]
    col = lambda x, h: x[:, h:h + 1]
    qs = [qk_ref[b, :, h * DK_C:(h + 1) * DK_C] for b, h in items]
    ks = [qk_ref[b, :, wq + h * DK_C:wq + (h + 1) * DK_C] * (DK_C ** -0.5) for b, h in items]
    vbs = [v_ref[b, :, h * DV_C:(h + 1) * DV_C].astype(BF16) for b, h in items]
    c_old = [c_ref[b, h] for b, h in items]
    n_old = [n_ref[b, h:h + 1, :] for b, h in items]
    qk_raw = [_dot_nt(q.astype(BF16), k.astype(BF16)) for q, k in zip(qs, ks)]
    q_c = [_dot(q, c) for q, c in zip(qs, c_old)]
    log_w = [jnp.where(causal, col(fc[b], h) + row_terms[b][h:h + 1, :], -jnp.inf) for b, h in items]
    li = [col(log_inter[b], h) for b, h in items]
    m_t = [jnp.maximum(a, jnp.max(lw, axis=-1, keepdims=True)) for a, lw in zip(li, log_w)]
    w_inter = [jnp.exp(a - m) for a, m in zip(li, m_t)]
    qkw = [r * jnp.exp(lw - m) for r, lw, m in zip(qk_raw, log_w, m_t)]
    num = [_dot(a, vb) + wi * qc for a, vb, wi, qc in zip(qkw, vbs, w_inter, q_c)]
    den = [jnp.sum(a, axis=-1, keepdims=True) + wi * jnp.sum(q * n, axis=-1, keepdims=True)
           for a, wi, q, n in zip(qkw, w_inter, qs, n_old)]
    hid = [nu / jnp.maximum(jnp.abs(de), jnp.exp(-m)) for nu, de, m in zip(num, den, m_t)]
    kw = [k * col(w_last[b], h) for k, (b, h) in zip(ks, items)]
    for i, (b, h) in enumerate(items):
        c_ref[b, h] = col(dec_vec[b], h) * c_old[i] + _dot_tn(kw[i].astype(BF16), vbs[i])
        n_ref[b, h:h + 1, :] = col(dec_vec[b], h) * n_old[i] + jnp.sum(kw[i], axis=0, keepdims=True)
    for i, (b, h) in enumerate(items):
        oh = og_ref[b, :, h * DV_C:(h + 1) * DV_C]
        y_ref[b, :, h * DV_C:(h + 1) * DV_C] = _head_rms(hid[i], nw_ref[...]) * _sigmoid(oh)
    for b in seqs:
        m_ref[b] = m_vec[b]
    cfin_ref[...] = c_ref[...]
    nfin_ref[...] = n_ref[...]
    mfin_ref[...] = m_ref[...]


def _mlstm(qk, v, og, misc, c0, n0, m0, i_bias, f_bias, norm_w, *, nb, l, i_col, f_col):
    n = qk.shape[0]
    seq = n // nb
    nbb = SEQS_PER_STEP if nb % SEQS_PER_STEP == 0 else 1
    tok = lambda b, j: (b, j, 0)
    per_b3 = lambda b, j: (b, 0, 0)
    per_b4 = lambda b, j: (b, 0, 0, 0)
    const2 = lambda b, j: (0, 0)
    m0 = m0.reshape(nb, 1, N_HEADS)
    tokens = [a.reshape(nb, seq, a.shape[1]) for a in (qk, v, og, misc)]
    outs = pl.pallas_call(
        functools.partial(_mlstm_kernel, l=l, nbb=nbb, i_col=i_col, f_col=f_col),
        grid=(nb // nbb, seq // l),
        in_specs=[pl.BlockSpec((nbb, l, a.shape[2]), tok) for a in tokens]
                 + [pl.BlockSpec((nbb,) + c0.shape[1:], per_b4),
                    pl.BlockSpec((nbb,) + n0.shape[1:], per_b3),
                    pl.BlockSpec((nbb, 1, N_HEADS), per_b3),
                    pl.BlockSpec((1, N_HEADS), const2),
                    pl.BlockSpec((1, N_HEADS), const2),
                    pl.BlockSpec((1, DV_C), const2)],
        out_specs=[pl.BlockSpec((nbb, l, v.shape[1]), tok),
                   pl.BlockSpec((nbb,) + c0.shape[1:], per_b4),
                   pl.BlockSpec((nbb,) + n0.shape[1:], per_b3),
                   pl.BlockSpec((nbb, 1, N_HEADS), per_b3)],
        out_shape=[jax.ShapeDtypeStruct((nb, seq, v.shape[1]), F32),
                   jax.ShapeDtypeStruct(c0.shape, F32),
                   jax.ShapeDtypeStruct(n0.shape, F32),
                   jax.ShapeDtypeStruct(m0.shape, F32)],
        scratch_shapes=[pltpu.VMEM((nbb,) + c0.shape[1:], F32),
                        pltpu.VMEM((nbb,) + n0.shape[1:], F32),
                        pltpu.VMEM((nbb, 1, N_HEADS), F32)],
        compiler_params=_params("parallel", "arbitrary"),
        name="mlstm",
    )(*tokens, c0, n0, m0, i_bias.reshape(1, -1), f_bias.reshape(1, -1), norm_w.reshape(1, -1))
    y, c_fin, n_fin, m_fin = outs
    return y.reshape(n, v.shape[1]), c_fin, n_fin, m_fin.reshape(nb, N_HEADS)


def _block_mask(rows, row_group, cols, col_group):
    return (np.arange(rows)[:, None] // row_group == np.arange(cols)[None, :] // col_group).astype(np.float32)


def _cummax_rows(x):
    rows = x.shape[0]
    row = _iota(x.shape, 0)
    sh = 1
    while sh < rows:
        x = jnp.maximum(x, jnp.where(row >= sh, pltpu.roll(x, sh, axis=0), -jnp.inf))
        sh *= 2
    return x


def _dot2(a, b):
    hi, lo = _split(a)
    return jnp.dot(hi, b, preferred_element_type=F32) + jnp.dot(lo, b, preferred_element_type=F32)


def _mlstm_dense_kernel(qk_ref, v_ref, og_ref, misc_ref, c0_ref, n0_ref, m0_ref, ib_ref, fb_ref, nw_ref,
                        el_ref, ev_ref, ek_ref, kmask_ref, vmask_ref, cmask_ref, cmaskb_ref, rms_ref, causal_ref,
                        dsel_ref,
                        y_ref, cfin_ref, nfin_ref, mfin_ref, c_ref, n_ref, m_ref, *, l, i_col, f_col):
    j = pl.program_id(1)
    wq = N_HEADS * DK_C
    nbb = qk_ref.shape[0]
    seqs = range(nbb)

    @pl.when(j == 0)
    def _():
        c_ref[...] = jnp.zeros(c_ref.shape, F32)
        for b in seqs:
            for h in range(N_HEADS):
                c_ref[b, h * DK_C:(h + 1) * DK_C, h * DV_C:(h + 1) * DV_C] = c0_ref[b, h]
        n_ref[...] = n0_ref[...]
        m_ref[...] = m0_ref[...]

    each = lambda f, *xs: [f(*args) for args in zip(*xs)]
    dot = functools.partial(jnp.dot, preferred_element_type=F32)
    spread = lambda x, e_ref: sum(dot(p, e_ref[...]) for p in _split3(x))
    tri = jnp.where(_iota((l, l), 0) >= _iota((l, l), 1), 1.0, 0.0).astype(BF16)
    ig = [misc_ref[b, :, i_col:i_col + N_HEADS] + ib_ref[...] for b in seqs]
    lf = [-_softplus(-(misc_ref[b, :, f_col:f_col + N_HEADS] + fb_ref[...])) for b in seqs]
    fc = each(lambda x: sum(dot(tri, p) for p in _split3(x)), lf)
    m_prev = [m_ref[b] for b in seqs]
    log_inter = each(jnp.add, fc, m_prev)
    a = each(jnp.subtract, ig, fc)
    m_t = each(lambda li, f, x: jnp.maximum(li, f + _cummax_rows(x)), log_inter, fc, a)
    w_inter = each(lambda li, m: jnp.exp(li - m), log_inter, m_t)
    w_last = each(lambda f, i, m: jnp.exp(f[l - 1:l, :] - f + i - m[l - 1:l, :]), fc, ig, m_t)

    key_terms = each(lambda x: jnp.sum(spread(x, el_ref) * dsel_ref[...], axis=0, keepdims=True), a)
    log_w = each(lambda f, kt: jnp.where(causal_ref[...] > 0.5, spread(f, el_ref) + kt, -jnp.inf), fc, key_terms)
    q = [qk_ref[b, :, :wq] for b in seqs]
    q_b = [x.astype(BF16) for x in q]
    k_s = [qk_ref[b, :, wq:] * (DK_C ** -0.5) for b in seqs]
    k_b = [x.astype(BF16) for x in k_s]
    v_b = [v_ref[b].astype(BF16) for b in seqs]
    kt_bd = [jnp.concatenate([x] * N_HEADS, axis=0) * kmask_ref[...] for x in k_b]
    v_bd = [jnp.concatenate([x] * N_HEADS, axis=0) * vmask_ref[...] for x in v_b]
    m_l = each(lambda m: spread(m, el_ref), m_t)
    qkw = each(lambda x, kt, lw, m: (_dot_nt(x, kt) * jnp.exp(lw - m)).astype(BF16), q_b, kt_bd, log_w, m_l)
    c_old = [c_ref[b] for b in seqs]
    n_old = [n_ref[b] for b in seqs]
    wi_s = each(lambda w: spread(w, ev_ref), w_inter)
    q_c = each(lambda x, c: dot(x, c.astype(BF16)), q_b, c_old)
    qn = each(lambda x, n: _dot2(x * n, cmaskb_ref[...]), q, n_old)
    num = each(lambda w, vb, wi, qc: dot(w, vb) + wi * qc, qkw, v_bd, wi_s, q_c)
    den = each(lambda w, wi, x: dot(w, vmask_ref[...]) + wi * x, qkw, wi_s, qn)
    m_v = m_l if l == DV_C else each(lambda m: spread(m, ev_ref), m_t)
    hid = each(lambda nu, de, m: nu / jnp.maximum(jnp.abs(de), jnp.exp(-m)), num, den, m_v)
    ms = each(lambda x: _dot2(x * x, rms_ref[...]), hid)
    for b in seqs:
        y_ref[b] = hid[b] * lax.rsqrt(ms[b] + 1e-6) * nw_ref[...] * _sigmoid(og_ref[b])

    kw = each(lambda k, w: k * spread(w, ek_ref), k_s, w_last)
    dec_rows = [jnp.broadcast_to(w[l - 1:l, :], (SUBLANES, N_HEADS)) for w in w_inter]
    upd = each(lambda x, vb: _dot_tn(x.astype(BF16), vb), kw, v_b)
    for b in seqs:
        c_ref[b] = c_old[b] * spread(dec_rows[b], ev_ref)[0:1, :] + upd[b] * cmask_ref[...]
        n_ref[b] = n_old[b] * spread(dec_rows[b], ek_ref)[0:1, :] + jnp.sum(kw[b], axis=0, keepdims=True)
        m_ref[b] = m_t[b][l - 1:l, :]

    @pl.when(j == pl.num_programs(1) - 1)
    def _():
        for b in seqs:
            for h in range(N_HEADS):
                cfin_ref[b, h] = c_ref[b, h * DK_C:(h + 1) * DK_C, h * DV_C:(h + 1) * DV_C]
        nfin_ref[...] = n_ref[...]
        mfin_ref[...] = m_ref[...]


def _mlstm_dense(qk, v, og, misc, c0, n0, m0, i_bias, f_bias, norm_w, *, nb, l, i_col, f_col):
    n = qk.shape[0]
    seq = n // nb
    wq, wv = N_HEADS * DK_C, N_HEADS * DV_C
    hl = N_HEADS * l
    nbb = SEQS_PER_STEP if nb % SEQS_PER_STEP == 0 else 1
    tok = lambda b, j: (b, j, 0)
    per_b3 = lambda b, j: (b, 0, 0)
    per_b4 = lambda b, j: (b, 0, 0, 0)
    const2 = lambda b, j: (0, 0)
    consts = [jnp.asarray(_block_mask(N_HEADS, 1, hl, l), BF16),
              jnp.asarray(_block_mask(N_HEADS, 1, wv, DV_C), BF16),
              jnp.asarray(_block_mask(N_HEADS, 1, wq, DK_C), BF16),
              jnp.asarray(_block_mask(hl, l, wq, DK_C), BF16),
              jnp.asarray(_block_mask(hl, l, wv, DV_C), BF16),
              jnp.asarray(_block_mask(wq, DK_C, wv, DV_C)),
              jnp.asarray(_block_mask(wq, DK_C, wv, DV_C), BF16),
              jnp.asarray(_block_mask(wv, DV_C, wv, DV_C) / DV_C, BF16),
              jnp.asarray(np.tile(np.tril(np.ones((l, l), np.float32)), (1, N_HEADS))),
              jnp.asarray(np.tile(np.eye(l, dtype=np.float32), (1, N_HEADS)))]
    tokens = [a.reshape(nb, seq, a.shape[1]) for a in (qk, v, og, misc)]
    outs = pl.pallas_call(
        functools.partial(_mlstm_dense_kernel, l=l, i_col=i_col, f_col=f_col),
        grid=(nb // nbb, seq // l),
        in_specs=[pl.BlockSpec((nbb, l, a.shape[2]), tok) for a in tokens]
                 + [pl.BlockSpec((nbb,) + c0.shape[1:], per_b4),
                    pl.BlockSpec((nbb, 1, wq), per_b3),
                    pl.BlockSpec((nbb, 1, N_HEADS), per_b3),
                    pl.BlockSpec((1, N_HEADS), const2),
                    pl.BlockSpec((1, N_HEADS), const2),
                    pl.BlockSpec((1, wv), const2)]
                 + [pl.BlockSpec(c.shape, const2) for c in consts],
        out_specs=[pl.BlockSpec((nbb, l, wv), tok),
                   pl.BlockSpec((nbb,) + c0.shape[1:], per_b4),
                   pl.BlockSpec((nbb, 1, wq), per_b3),
                   pl.BlockSpec((nbb, 1, N_HEADS), per_b3)],
        out_shape=[jax.ShapeDtypeStruct((nb, seq, wv), F32),
                   jax.ShapeDtypeStruct(c0.shape, F32),
                   jax.ShapeDtypeStruct((nb, 1, wq), F32),
                   jax.ShapeDtypeStruct((nb, 1, N_HEADS), F32)],
        scratch_shapes=[pltpu.VMEM((nbb, wq, wv), F32),
                        pltpu.VMEM((nbb, 1, wq), F32),
                        pltpu.VMEM((nbb, 1, N_HEADS), F32)],
        compiler_params=_params("parallel", "arbitrary"),
        name="mlstm",
    )(*tokens, c0, n0.reshape(nb, 1, wq), m0.reshape(nb, 1, N_HEADS), i_bias.reshape(1, -1), f_bias.reshape(1, -1),
      jnp.tile(norm_w, N_HEADS).reshape(1, wv), *consts)
    y, c_fin, n_fin, m_fin = outs
    return y.reshape(n, wv), c_fin, n_fin.reshape(nb, N_HEADS, DK_C), m_fin.reshape(nb, N_HEADS)


def _band_prompt_kernel(q_ref, k_ref, v_ref, bias_ref, o_ref, kb_ref, vt_ref, ot_ref, *, tq, seq):
    g = pl.program_id(1)
    w = q_ref.shape[1]
    pad = BAND_CHUNKS * CHUNK
    lw = pad + tq
    npad = pad // LANES
    per_tile = tq // LANES
    blk = _row_tile(seq, 4 * LANES)

    @pl.when(g == 0)
    def _():
        kb_ref[0:pad, :] = jnp.zeros((pad, w), BF16)
        vt_ref[0:npad] = jnp.zeros((npad, w, LANES), BF16)
        for r0 in range(0, seq, blk):
            kb_ref[pad + r0:pad + r0 + blk, :] = k_ref[0, r0:r0 + blk, :].astype(BF16)
        for j in range(seq // LANES):
            vt_ref[npad + j] = v_ref[0, j * LANES:(j + 1) * LANES, :].T.astype(BF16)

    start = pl.multiple_of(g * tq, tq)
    q = (q_ref[...] * (HEAD_DIM ** -0.5)).astype(BF16)
    before_seq = jnp.where(_iota((lw, tq), 0) >= pad - g * tq, 0.0, NEG)
    for h in range(N_HEADS):
        hs = slice(h * HEAD_DIM, (h + 1) * HEAD_DIM)
        s = _dot_nt(kb_ref[pl.ds(start, lw), hs], q[:, hs]) + (bias_ref[h] + before_seq)
        p = jnp.exp(s - _reduce_rows(s, jnp.maximum, jnp.max))
        pn = (p * (1.0 / _reduce_rows(p, jnp.add, jnp.sum))).astype(BF16)
        acc = None
        for j in range(lw // LANES):
            part = jnp.dot(vt_ref[g * per_tile + j, hs, :], pn[j * LANES:(j + 1) * LANES, :],
                           preferred_element_type=F32)
            acc = part if acc is None else acc + part
        ot_ref[hs, :] = acc
    o_ref[...] = ot_ref[...].T


def _band_prompt(q, k, v, bias, *, nb, tq):
    n, w = q.shape
    seq = k.shape[1]
    nq = seq // tq
    pad = BAND_CHUNKS * CHUNK
    row = lambda b, g: (b * nq + g, 0)
    per_b = lambda b, g: (b, 0, 0)
    return pl.pallas_call(
        functools.partial(_band_prompt_kernel, tq=tq, seq=seq),
        grid=(nb, nq),
        in_specs=[pl.BlockSpec((tq, w), row),
                  pl.BlockSpec((1, seq, w), per_b),
                  pl.BlockSpec((1, seq, w), per_b),
                  pl.BlockSpec(bias.shape, lambda b, g: (0, 0, 0))],
        out_specs=pl.BlockSpec((tq, w), row),
        out_shape=jax.ShapeDtypeStruct(q.shape, F32),
        scratch_shapes=[pltpu.VMEM((pad + seq, w), BF16),
                        pltpu.VMEM(((pad + seq) // LANES, w, LANES), BF16),
                        pltpu.VMEM((w, tq), F32)],
        compiler_params=_params("parallel", "arbitrary"),
        name="band_attention_prompt",
    )(q, k, v, bias)


def _band_sample_kernel(q_ref, kc_ref, vc_ref, kn_ref, vn_ref, biasc_ref, biasn_ref, o_ref):
    pieces = [(lambda h: kc_ref[0, 0, h], lambda h: vc_ref[0, 0, h], biasc_ref),
              (lambda h: kn_ref[0, h], lambda h: vn_ref[0, h], biasn_ref)]
    q = q_ref[...]
    for h in range(N_HEADS):
        hs = slice(h * HEAD_DIM, (h + 1) * HEAD_DIM)
        qh = q[:, hs].astype(BF16)
        scores = [jnp.dot(qh, load_k(h).astype(BF16), preferred_element_type=F32) * (HEAD_DIM ** -0.5) + b_ref[h]
                  for load_k, _, b_ref in pieces]
        m = functools.reduce(jnp.maximum, [jnp.max(s, axis=-1, keepdims=True) for s in scores])
        ps = [jnp.exp(s - m) for s in scores]
        den = sum(jnp.sum(p, axis=-1, keepdims=True) for p in ps)
        o = sum(_dot_nt(p.astype(BF16), load_v(h).astype(BF16)) for p, (_, load_v, _) in zip(ps, pieces))
        o_ref[:, hs] = o / den


def _band_sample(q, kc, vc, layer, kn, vn, bias_c, bias_n, *, nb, tq):
    n, w = q.shape
    row = lambda b: (b, 0)
    const3 = lambda b: (0, 0, 0)
    return pl.pallas_call(
        _band_sample_kernel,
        grid=(nb,),
        in_specs=[pl.BlockSpec((tq, w), row)]
                 + [pl.BlockSpec((1, 1) + a.shape[2:], lambda b: (layer, b, 0, 0, 0)) for a in (kc, vc)]
                 + [pl.BlockSpec((1,) + a.shape[1:], lambda b: (b, 0, 0, 0)) for a in (kn, vn)]
                 + [pl.BlockSpec(bias_c.shape, const3), pl.BlockSpec(bias_n.shape, const3)],
        out_specs=pl.BlockSpec((tq, w), row),
        out_shape=jax.ShapeDtypeStruct(q.shape, F32),
        compiler_params=_params("parallel"),
        name="band_attention_sample",
    )(q, kc, vc, kn, vn, bias_c, bias_n)


def _t5_bucket(rel):
    nb = T5_BUCKETS // 2
    max_exact = nb // 2
    n = jnp.abs(rel)
    n_f = jnp.maximum(n, 1).astype(jnp.float32)
    large = max_exact + (jnp.log(n_f / max_exact) / math.log(T5_MAX_DIST / max_exact) * (nb - max_exact)).astype(jnp.int32)
    large = jnp.minimum(large, nb - 1)
    return jnp.where(rel > 0, nb, 0) + jnp.where(n < max_exact, n, large)


def _toeplitz_bias(fn, n_rows, n_cols):
    n = n_rows + n_cols
    m = np.arange(n)
    f = jnp.transpose(fn(np.where(m < n_cols, m, m - n))).astype(F32)
    flat = jnp.tile(f, (1, n_rows))[:, :n_rows * (n - 1)]
    return flat.reshape(f.shape[0], n_rows, n - 1)[:, :, :n_cols]


def _pack_cols(w, sizes, groups):
    offs = np.concatenate([[0], np.cumsum(sizes)])
    cols, widths = [], []
    for grp in groups:
        width = 0
        for idx in grp:
            cols.append(w[:, offs[idx]:offs[idx + 1]])
            width += sizes[idx]
        pad = (-width) % LANES
        if pad:
            cols.append(jnp.zeros((w.shape[0], pad), w.dtype))
        widths.append(width + pad)
    return jnp.concatenate(cols, axis=1).astype(BF16), tuple(widths)


def _row_tile(n, target):
    t = min(n, target)
    while n % t:
        t //= 2
    return t


def kernel(x_prompt, x_sample, cache_a_k, cache_a_v, cache_a_kidx, state_b_s, state_b_conv, state_c_c, state_c_n, state_c_m, cache_d_k, cache_d_v, state_ffn_conv, w_in_even, w_out_even, t5_bias, b_conv_w, b_a_log, b_dt_bias, b_norm_w, w_in_odd, w_out_odd, c_i_bias, c_f_bias, c_norm_w, d_rel_bias, ln_mix_g, ln_mix_b, ln_ffn_g, ln_ffn_b, ffn_w_up, ffn_conv_w, ffn_w_down):
    bp, sp, d = x_prompt.shape
    bs, ts, _ = x_sample.shape
    depth = ffn_w_up.shape[0]
    past = cache_a_k.shape[2]
    d_win = cache_d_k.shape[2]
    dff = ffn_w_down.shape[1]
    alpha = (2 * depth) ** 0.25
    w_a = N_HEADS * HEAD_DIM
    w_b = N_HEADS * DV_B
    w_c = N_HEADS * DV_C
    qkv_b_w = 2 * N_HEADS * DK_B + w_b
    even_sizes = (w_a, w_a, w_a, N_IDX_HEADS * D_IDX, D_IDX, N_IDX_HEADS, qkv_b_w, N_HEADS, N_HEADS, w_b)
    odd_sizes = (N_HEADS * DK_C, N_HEADS * DK_C, w_c, N_HEADS, N_HEADS, w_c, w_a, w_a, w_a)
    even_groups = ((0,), (1,), (2,), (3,), (6,), (9,), (4, 5, 7, 8))
    wi_col, a_col, b_col = D_IDX, D_IDX + N_IDX_HEADS, D_IDX + N_IDX_HEADS + N_HEADS
    odd_groups = ((0, 1), (2,), (5,), (6,), (7,), (8,), (3, 4))
    i_col, f_col = 0, N_HEADS

    assert sp % CHUNK == 0 and ts <= CHUNK and past % CHUNK == 0 and past >= T5_FAR
    assert (past + ts - 1) // CHUNK == past // CHUNK
    topk_p = min(TOPK_MAX, sp // 4)
    topk_s = min(TOPK_MAX, (past + ts) // 4)
    n_p, n_s = bp * sp, bs * ts
    tm_p = _row_tile(n_p, 512)
    tm_s = _row_tile(n_s, 512)
    tff_p = _row_tile(sp, 512)
    ns_s = _row_tile(bs, max(1, 256 // ts))

    t5 = lambda rel: t5_bias[_t5_bucket(jnp.asarray(rel, jnp.int32))]
    fbias = t5(np.array([-T5_FAR - 1]))
    tq_a = 2 * CHUNK
    nw_a = tq_a + T5_FAR
    assert sp % nw_a == 0
    corr_p = jnp.stack([_toeplitz_bias(lambda dd: t5(-dd - T5_FAR * v) - fbias, nw_a, tq_a) for v in range(2)])
    ln_s = T5_FAR + ts
    nbias_s = _toeplitz_bias(lambda dd: t5(dd - T5_FAR), ts, ln_s)[None]

    lw = BAND_CHUNKS * CHUNK + tq_a
    r_chunk = np.arange(lw)[:, None] // CHUNK
    q_chunk = BAND_CHUNKS + np.arange(tq_a)[None, :] // CHUNK
    band_ok = (r_chunk >= q_chunk - BAND_CHUNKS) & (r_chunk <= q_chunk)
    pos_q = past + np.arange(ts)
    pos_kc = past - d_win + np.arange(d_win)
    def band_valid(pos_k):
        kch, qch = pos_k // CHUNK, pos_q // CHUNK
        return (pos_k[None] >= 0) & (kch[None] >= qch[:, None] - BAND_CHUNKS) & (kch[None] <= qch[:, None])
    def band_bias(table, shift, n_rows, n_cols, valid=None):
        bias = _toeplitz_bias(lambda dd: table[np.clip(dd + shift, -REL_CLIP, REL_CLIP) + REL_CLIP], n_rows, n_cols)
        return bias if valid is None else jnp.where(jnp.asarray(valid)[None], bias, NEG)

    cak_t = jnp.transpose(cache_a_k, (0, 1, 3, 4, 2))
    cav_t = jnp.transpose(cache_a_v, (0, 1, 3, 4, 2))
    caki_t = jnp.transpose(cache_a_kidx, (0, 1, 3, 2))
    cdk_t = jnp.transpose(cache_d_k, (0, 1, 3, 4, 2))
    cdv_t = jnp.transpose(cache_d_v, (0, 1, 3, 4, 2))

    xp = x_prompt.reshape(n_p, d)
    xs = x_sample.reshape(n_s, d)
    outs = {k: [] for k in ("ak_p", "ak_s", "av_p", "av_s", "aki_p", "aki_s", "bs_p", "bs_s", "bc_p", "bc_s",
                            "cc_p", "cc_s", "cn_p", "cn_s", "cm_p", "cm_s", "dk_p", "dk_s", "dv_p", "dv_s",
                            "fc_p", "fc_s")}
    for layer in range(depth):
        if layer % 2 == 0:
            e = layer // 2
            w_in, widths = _pack_cols(w_in_even[e], even_sizes, even_groups)
            w_out = w_out_even[e].astype(BF16)
            qa, ka, va, qi, qkv_b, z_b, misc = _proj(xp, w_in, widths, tm_p)
            o_a = _dsa_blocks(qa, qi, misc, ka.reshape(bp, sp, w_a), va.reshape(bp, sp, w_a),
                              misc.reshape(bp, sp, LANES), corr_p, nb=bp, tq=tq_a, topk=topk_p, wi_col=wi_col)
            y_b, s_b, h_b = _gdn(qkv_b, misc, z_b, jnp.zeros((bp, CONV_B - 1, qkv_b_w), F32),
                                 jnp.zeros((bp, N_HEADS, DK_B, DV_B), F32), b_conv_w[e], b_a_log[e], b_dt_bias[e],
                                 b_norm_w[e], nb=bp, c=CHUNK, a_col=a_col, b_col=b_col)
            xp = _mm_res_ln([o_a, y_b], [w_out[:w_a], w_out[w_a:]], xp, ln_mix_g[layer], ln_mix_b[layer], alpha, tm_p)
            outs["ak_p"].append(ka.reshape(bp, sp, N_HEADS, HEAD_DIM))
            outs["av_p"].append(va.reshape(bp, sp, N_HEADS, HEAD_DIM))
            outs["aki_p"].append(misc[:, :D_IDX].reshape(bp, sp, D_IDX))
            outs["bs_p"].append(s_b)
            outs["bc_p"].append(h_b)
            qa, ka, va, qi, qkv_b, z_b, misc = _proj(xs, w_in, widths, tm_s)
            ki = misc[:, :D_IDX]
            heads_t = lambda u: jnp.transpose(u.reshape(bs, ts, N_HEADS, HEAD_DIM), (0, 2, 3, 1))
            near = (jnp.concatenate([cak_t[e, ..., past - T5_FAR:], heads_t(ka)], axis=-1),
                    jnp.concatenate([cav_t[e, ..., past - T5_FAR:], heads_t(va)], axis=-1),
                    jnp.concatenate([caki_t[e, ..., past - T5_FAR:],
                                     jnp.transpose(ki.reshape(bs, ts, D_IDX), (0, 2, 1))], axis=-1))
            o_a = _dsa_sample(qa, qi, misc, cak_t, cav_t, caki_t, e, near, nbias_s, fbias, nb=bs, tq=ts,
                              topk=topk_s, wi_col=wi_col)
            y_b, s_b, h_b = _gdn(qkv_b, misc, z_b, state_b_conv[e], state_b_s[e], b_conv_w[e], b_a_log[e],
                                 b_dt_bias[e], b_norm_w[e], nb=bs, c=ts, a_col=a_col, b_col=b_col)
            xs = _mm_res_ln([o_a, y_b], [w_out[:w_a], w_out[w_a:]], xs, ln_mix_g[layer], ln_mix_b[layer], alpha, tm_s)
            outs["ak_s"].append(ka.reshape(bs, ts, N_HEADS, HEAD_DIM))
            outs["av_s"].append(va.reshape(bs, ts, N_HEADS, HEAD_DIM))
            outs["aki_s"].append(ki.reshape(bs, ts, D_IDX))
            outs["bs_s"].append(s_b)
            outs["bc_s"].append(h_b)
        else:
            o = layer // 2
            w_in, widths = _pack_cols(w_in_odd[o], odd_sizes, odd_groups)
            w_out = w_out_odd[o].astype(BF16)
            qk_c, v_c, o_c, q_d, k_d, v_d, misc = _proj(xp, w_in, widths, tm_p)
            y_c, c_c, c_n, c_m = _mlstm_dense(qk_c, v_c, o_c, misc,jnp.zeros((bp, N_HEADS, DK_C, DV_C), F32),
                                        jnp.zeros((bp, N_HEADS, DK_C), F32), jnp.zeros((bp, N_HEADS), F32),
                                        c_i_bias[o], c_f_bias[o], c_norm_w[o], nb=bp, l=CHUNK, i_col=i_col, f_col=f_col)
            k3 = k_d.reshape(bp, sp, w_a)
            v3 = v_d.reshape(bp, sp, w_a)
            bias_bp = _toeplitz_bias(
                lambda dd: d_rel_bias[o][np.clip(-dd - BAND_CHUNKS * CHUNK, -REL_CLIP, REL_CLIP) + REL_CLIP], lw, tq_a)
            bias_bp = jnp.where(jnp.asarray(band_ok)[None], bias_bp, NEG)
            o_d = _band_prompt(q_d, k3, v3, bias_bp, nb=bp, tq=tq_a)
            xp = _mm_res_ln([y_c, o_d], [w_out[:w_c], w_out[w_c:]], xp, ln_mix_g[layer], ln_mix_b[layer], alpha, tm_p)
            d_win_p = min(BAND_CHUNKS * CHUNK, sp)
            outs["cc_p"].append(c_c)
            outs["cn_p"].append(c_n)
            outs["cm_p"].append(c_m)
            outs["dk_p"].append(k3[:, sp - d_win_p:].reshape(bp, d_win_p, N_HEADS, HEAD_DIM))
            outs["dv_p"].append(v3[:, sp - d_win_p:].reshape(bp, d_win_p, N_HEADS, HEAD_DIM))
            qk_c, v_c, o_c, q_d, k_d, v_d, misc = _proj(xs, w_in, widths, tm_s)
            y_c, c_c, c_n, c_m = _mlstm_dense(qk_c, v_c, o_c, misc,state_c_c[o], state_c_n[o], state_c_m[o],
                                        c_i_bias[o], c_f_bias[o], c_norm_w[o], nb=bs, l=ts, i_col=i_col, f_col=f_col)
            heads_t = lambda u: jnp.transpose(u.reshape(bs, ts, N_HEADS, HEAD_DIM), (0, 2, 3, 1))
            o_d = _band_sample(q_d, cdk_t, cdv_t, o, heads_t(k_d), heads_t(v_d),
                               band_bias(d_rel_bias[o], -d_win, ts, d_win, band_valid(pos_kc)),
                               band_bias(d_rel_bias[o], 0, ts, ts, band_valid(pos_q)), nb=bs, tq=ts)
            xs = _mm_res_ln([y_c, o_d], [w_out[:w_c], w_out[w_c:]], xs, ln_mix_g[layer], ln_mix_b[layer], alpha, tm_s)
            outs["cc_s"].append(c_c)
            outs["cn_s"].append(c_n)
            outs["cm_s"].append(c_m)
            outs["dk_s"].append(k_d.reshape(bs, ts, N_HEADS, HEAD_DIM))
            outs["dv_s"].append(v_d.reshape(bs, ts, N_HEADS, HEAD_DIM))
        w_up = ffn_w_up[layer].astype(BF16)
        w_down = ffn_w_down[layer].astype(BF16)
        act, hist_p = _ffn_up(xp, w_up, ffn_conv_w[layer], jnp.zeros((bp, CONV_FF - 1, 2 * dff), F32),
                              1, tff_p, sp // tff_p)
        xp = _mm_res_ln([act], [w_down], xp, ln_ffn_g[layer], ln_ffn_b[layer], alpha, tm_p)
        act, hist_s = _ffn_up(xs, w_up, ffn_conv_w[layer], state_ffn_conv[layer], ns_s, ts, 1)
        xs = _mm_res_ln([act], [w_down], xs, ln_ffn_g[layer], ln_ffn_b[layer], alpha, tm_s)
        outs["fc_p"].append(hist_p)
        outs["fc_s"].append(hist_s)

    st = lambda k: jnp.stack(outs[k])
    return (xp.reshape(bp, sp, d), xs.reshape(bs, ts, d),
            st("ak_p"), st("ak_s"), st("av_p"), st("av_s"), st("aki_p"), st("aki_s"),
            st("bs_p"), st("bs_s"), st("bc_p"), st("bc_s"),
            st("cc_p"), st("cc_s"), st("cn_p"), st("cn_s"), st("cm_p"), st("cm_s"),
            st("dk_p"), st("dk_s"), st("dv_p"), st("dv_s"),
            st("fc_p"), st("fc_s"))
```

```python
import functools
import math

import numpy as np
import jax
import jax.numpy as jnp
from jax import lax
from jax.experimental import pallas as pl
from jax.experimental.pallas import tpu as pltpu

F32 = jnp.float32
BF16 = jnp.bfloat16
HI = lax.Precision.HIGHEST

CHUNK = 64
HEAD_DIM = 64
N_HEADS = 8
N_IDX_HEADS = 8
D_IDX = 64
TOPK_MAX = 256
T5_BUCKETS = 32
T5_MAX_DIST = 128
DK_B = 64
DV_B = 64
CONV_B = 4
DK_C = 32
DV_C = 64
BAND_CHUNKS = 8
REL_CLIP = 128
CONV_FF = 3
NEAR_CHUNKS = 3
T5_FAR = 128
INV_BLOCK = 8
GDN_SEQS_PER_STEP = 2
SEQS_PER_STEP = 4

LANES = 128
SUBLANES = 8
VMEM_LIMIT = 56 * 1024 * 1024

NEG = -1e30
INT_MIN = -2 ** 31


def _params(*sem):
    return pltpu.CompilerParams(dimension_semantics=sem, vmem_limit_bytes=VMEM_LIMIT)


def _dot(a, b):
    return jnp.dot(a.astype(BF16), b.astype(BF16), preferred_element_type=F32)


def _dot_nt(a, b, precision=None):
    return lax.dot_general(a, b, (((1,), (1,)), ((), ())), precision=precision, preferred_element_type=F32)


def _dot_tn(a, b, precision=None):
    return lax.dot_general(a, b, (((0,), (0,)), ((), ())), precision=precision, preferred_element_type=F32)


def _dot_hi(a, b):
    return jnp.dot(a, b, precision=HI, preferred_element_type=F32)


def _split(a):
    hi = a.astype(BF16)
    return hi, (a - hi.astype(F32)).astype(BF16)


def _split3(a):
    p1 = a.astype(BF16)
    r1 = a - p1.astype(F32)
    p2 = r1.astype(BF16)
    return p1, p2, (r1 - p2.astype(F32)).astype(BF16)


def _dot3(a, b):
    a_hi, a_lo = a
    b_hi, b_lo = b
    d = functools.partial(jnp.dot, preferred_element_type=F32)
    return d(a_hi, b_hi) + (d(a_hi, b_lo) + d(a_lo, b_hi))


def _sigmoid(x):
    return 1.0 / (1.0 + jnp.exp(-x))


def _softplus(x):
    return jnp.maximum(x, 0.0) + jnp.log(1.0 + jnp.exp(-jnp.abs(x)))


def _iota(shape, dim):
    return lax.broadcasted_iota(jnp.int32, shape, dim)


def _proj_kernel(x_ref, w_ref, *out_refs, sizes):
    xb = x_ref[...].astype(BF16)
    off = 0
    for o_ref, size in zip(out_refs, sizes):
        o_ref[...] = jnp.dot(xb, w_ref[:, off:off + size], preferred_element_type=F32)
        off += size


def _proj(x2d, w, sizes, tm):
    n, d = x2d.shape
    return pl.pallas_call(
        functools.partial(_proj_kernel, sizes=sizes),
        grid=(n // tm,),
        in_specs=[pl.BlockSpec((tm, d), lambda i: (i, 0)),
                  pl.BlockSpec((d, sum(sizes)), lambda i: (0, 0))],
        out_specs=[pl.BlockSpec((tm, s), lambda i: (i, 0)) for s in sizes],
        out_shape=[jax.ShapeDtypeStruct((n, s), F32) for s in sizes],
        compiler_params=_params("parallel"),
        name="in_proj",
    )(x2d, w)


def _mm_res_ln_kernel(*refs, nparts, alpha):
    part_refs = refs[:nparts]
    w_refs = refs[nparts:2 * nparts]
    x_ref, g_ref, b_ref, o_ref = refs[2 * nparts:]
    acc = alpha * x_ref[...]
    for p_ref, w_ref in zip(part_refs, w_refs):
        acc = acc + jnp.dot(p_ref[...].astype(BF16), w_ref[...], preferred_element_type=F32)
    mu = jnp.mean(acc, axis=-1, keepdims=True)
    cen = acc - mu
    var = jnp.mean(cen * cen, axis=-1, keepdims=True)
    o_ref[...] = cen * lax.rsqrt(var + 1e-5) * g_ref[...] + b_ref[...]


def _mm_res_ln(parts, ws, x2d, g, b, alpha, tm):
    n, d = x2d.shape
    nparts = len(parts)
    in_specs = ([pl.BlockSpec((tm, p.shape[1]), lambda i: (i, 0)) for p in parts]
                + [pl.BlockSpec(w.shape, lambda i: (0, 0)) for w in ws]
                + [pl.BlockSpec((tm, d), lambda i: (i, 0)),
                   pl.BlockSpec((1, d), lambda i: (0, 0)),
                   pl.BlockSpec((1, d), lambda i: (0, 0))])
    return pl.pallas_call(
        functools.partial(_mm_res_ln_kernel, nparts=nparts, alpha=alpha),
        grid=(n // tm,),
        in_specs=in_specs,
        out_specs=pl.BlockSpec((tm, d), lambda i: (i, 0)),
        out_shape=jax.ShapeDtypeStruct((n, d), F32),
        compiler_params=_params("parallel"),
        name="out_proj_ln",
    )(*parts, *ws, x2d, g.reshape(1, d), b.reshape(1, d))


def _ffn_up_kernel(x_ref, w_ref, cw_ref, hist_ref, act_ref, newhist_ref, ext_ref, *,
                   ns, tt, tiles_per_seq, dff, cc):
    i = pl.program_id(0)
    tm = ns * tt
    hw = CONV_FF - 1
    base = SUBLANES
    if tiles_per_seq == 1:
        ext_ref[:, base - hw:base, :] = hist_ref[...]
    else:
        @pl.when(i % tiles_per_seq == 0)
        def _():
            ext_ref[:, base - hw:base, :] = hist_ref[...]

        @pl.when(i % tiles_per_seq != 0)
        def _():
            ext_ref[:, base - hw:base, :] = ext_ref[:, base + tt - hw:base + tt, :]
    xb = x_ref[...].astype(BF16)
    for j in range(2 * dff // cc):
        cols = slice(j * cc, (j + 1) * cc)
        h = jnp.dot(xb, w_ref[:, cols], preferred_element_type=F32)
        ext_ref[:, base:base + tt, cols] = h.reshape(ns, tt, cc)
    newhist_ref[...] = ext_ref[:, base + tt - hw:base + tt, :]

    def conv(cols):
        acc = None
        for k in range(CONV_FF):
            term = ext_ref[:, base - hw + k:base - hw + k + tt, cols] * cw_ref[k:k + 1, cols]
            acc = term if acc is None else acc + term
        return acc

    for j in range(dff // cc):
        g = conv(slice(j * cc, (j + 1) * cc))
        u = conv(slice(dff + j * cc, dff + (j + 1) * cc))
        act = g * _sigmoid(g) * u
        act_ref[:, j * cc:(j + 1) * cc] = act.reshape(tm, cc).astype(BF16)


def _ffn_up(x2d, w_up, conv_w, hist, ns, tt, tiles_per_seq):
    n, d = x2d.shape
    c2 = w_up.shape[1]
    dff = c2 // 2
    tm = ns * tt
    cc = 256
    hw = CONV_FF - 1
    if tiles_per_seq == 1:
        hist_map = lambda i: (i, 0, 0)
    else:
        hist_map = lambda i: (i // tiles_per_seq, 0, 0)
    return pl.pallas_call(
        functools.partial(_ffn_up_kernel, ns=ns, tt=tt, tiles_per_seq=tiles_per_seq, dff=dff, cc=cc),
        grid=(n // tm,),
        in_specs=[pl.BlockSpec((tm, d), lambda i: (i, 0)),
                  pl.BlockSpec((d, c2), lambda i: (0, 0)),
                  pl.BlockSpec((CONV_FF, c2), lambda i: (0, 0)),
                  pl.BlockSpec((ns, hw, c2), hist_map)],
        out_specs=[pl.BlockSpec((tm, dff), lambda i: (i, 0)),
                   pl.BlockSpec((ns, hw, c2), hist_map)],
        out_shape=[jax.ShapeDtypeStruct((n, dff), BF16),
                   jax.ShapeDtypeStruct(hist.shape, F32)],
        scratch_shapes=[pltpu.VMEM((ns, SUBLANES + tt, c2), F32)],
        compiler_params=_params("arbitrary"),
        name="ffn_up_conv_gate",
    )(x2d, w_up, conv_w, hist)


def _sortable(x):
    b = lax.bitcast_convert_type(x, jnp.int32)
    return b ^ ((b >> 31) & jnp.int32(0x7FFFFFFF))


def _count(mask):
    return jnp.sum(jnp.where(mask, 1.0, 0.0), axis=-1, keepdims=True)


def _dsa_sample_kernel(qa_ref, qi_ref, qm_ref, kf_ref, vf_ref, kif_ref, kn_ref, vn_ref, kin_ref, nbias_ref,
                       fbias_ref, o_ref, self_ref, seln_ref, *, topk, tq, lf, ln, wi_col):
    v = 0
    start = lf - T5_FAR
    qa = qa_ref[...]
    qi = qi_ref[...]
    wi = qm_ref[:, wi_col:wi_col + N_IDX_HEADS] * (N_IDX_HEADS ** -0.5) * (D_IDX ** -0.5)
    kif = kif_ref[0, 0].astype(BF16)
    kinb = kin_ref[0].astype(BF16)

    sc_f = jnp.zeros((tq, lf), F32)
    sc_n = jnp.zeros((tq, ln), F32)
    for n in range(N_IDX_HEADS):
        qn = qi[:, n * D_IDX:(n + 1) * D_IDX].astype(BF16)
        wn = wi[:, n:n + 1]
        sc_f = sc_f + jnp.maximum(jnp.dot(qn, kif, preferred_element_type=F32), 0.0) * wn
        sc_n = sc_n + jnp.maximum(jnp.dot(qn, kinb, preferred_element_type=F32), 0.0) * wn
    adm_f = _iota((tq, lf), 1) < start
    adm_n = nbias_ref[v, 0] > 0.5 * NEG
    key_f = jnp.where(adm_f, _sortable(sc_f), jnp.int32(INT_MIN))
    key_n = jnp.where(adm_n, _sortable(sc_n), jnp.int32(INT_MIN))

    kf32 = float(topk)

    def body(i, t_u):
        cand_u = t_u | lax.shift_left(jnp.int32(1), 31 - i)
        cand_s = cand_u ^ jnp.int32(INT_MIN)
        cnt = _count(key_f >= cand_s) + _count(key_n >= cand_s)
        return jnp.where(cnt >= kf32, cand_u, t_u)

    t_u = lax.fori_loop(0, 32, body, jnp.zeros((tq, 1), jnp.int32))
    thr = t_u ^ jnp.int32(INT_MIN)

    n_gt = _count(key_f > thr) + _count(key_n > thr)
    n_eq = _count(key_f == thr) + _count(key_n == thr)
    need = kf32 - n_gt
    open_row = thr == jnp.int32(INT_MIN)
    conflict = jnp.logical_and(n_eq != need, jnp.logical_not(open_row))
    self_ref[...] = jnp.where(jnp.logical_and(key_f >= thr, adm_f), 0.0, NEG)
    seln_ref[...] = jnp.where(jnp.logical_and(key_n >= thr, adm_n), 0.0, NEG)

    @pl.when(jnp.max(jnp.where(conflict, 1.0, 0.0)) > 0.0)
    def _():
        upper = jnp.where(_iota((LANES, LANES), 0) < _iota((LANES, LANES), 1), 1.0, 0.0).astype(BF16)
        offset = jnp.zeros((tq, 1), F32)
        for ref, key, width in ((self_ref, key_f, lf), (seln_ref, key_n, ln)):
            for j0 in range(0, width, LANES):
                w = min(LANES, width - j0)
                kb = key[:, j0:j0 + w]
                e = jnp.where(kb == thr, 1.0, 0.0)
                rank = offset + jnp.dot(e.astype(BF16), upper[:w, :w], preferred_element_type=F32)
                take = jnp.where(kb > thr, 1.0, jnp.where(rank < need, e, 0.0))
                take = jnp.where(open_row, jnp.where(kb > thr, 1.0, 0.0), take)
                ref[:, j0:j0 + w] = jnp.where(take > 0.5, 0.0, NEG)
                offset = offset + jnp.sum(e, axis=-1, keepdims=True)

    sel_f = self_ref[...]
    sel_n = seln_ref[...]
    for h in range(N_HEADS):
        hs = slice(h * HEAD_DIM, (h + 1) * HEAD_DIM)
        qh = qa[:, hs].astype(BF16)
        s_f = (jnp.dot(qh, kf_ref[0, 0, h].astype(BF16), preferred_element_type=F32) * (HEAD_DIM ** -0.5)
               + fbias_ref[:, h:h + 1] + sel_f)
        s_n = (jnp.dot(qh, kn_ref[0, h].astype(BF16), preferred_element_type=F32) * (HEAD_DIM ** -0.5)
               + nbias_ref[v, h] + sel_n)
        m = jnp.maximum(jnp.max(s_f, axis=-1, keepdims=True), jnp.max(s_n, axis=-1, keepdims=True))
        p_f = jnp.exp(s_f - m)
        p_n = jnp.exp(s_n - m)
        den = jnp.sum(p_f, axis=-1, keepdims=True) + jnp.sum(p_n, axis=-1, keepdims=True)
        o = (_dot_nt(p_f.astype(BF16), vf_ref[0, 0, h].astype(BF16))
             + _dot_nt(p_n.astype(BF16), vn_ref[0, h].astype(BF16)))
        o_ref[:, hs] = o / den


def _dsa_sample(qa, qi, qmisc, kf, vf, kif, layer, near, nbias, fbias, *, nb, tq, topk, wi_col):
    lf = kf.shape[-1]
    ln = nbias.shape[-1]
    row = lambda b: (b, 0)
    args = [qa, qi, qmisc, kf, vf, kif, *near]
    in_specs = ([pl.BlockSpec((tq, a.shape[1]), row) for a in args[:3]]
                + [pl.BlockSpec((1, 1) + a.shape[2:], lambda b, nd=a.ndim: (layer, b) + (0,) * (nd - 2))
                   for a in args[3:6]]
                + [pl.BlockSpec((1,) + a.shape[1:], lambda b, nd=a.ndim: (b,) + (0,) * (nd - 1)) for a in args[6:]]
                + [pl.BlockSpec(nbias.shape, lambda b: (0, 0, 0, 0)),
                   pl.BlockSpec(fbias.shape, lambda b: (0, 0))])
    return pl.pallas_call(
        functools.partial(_dsa_sample_kernel, topk=topk, tq=tq, lf=lf, ln=ln, wi_col=wi_col),
        grid=(nb,),
        in_specs=in_specs,
        out_specs=pl.BlockSpec((tq, qa.shape[1]), row),
        out_shape=jax.ShapeDtypeStruct(qa.shape, F32),
        scratch_shapes=[pltpu.VMEM((tq, lf), F32), pltpu.VMEM((tq, ln), F32)],
        compiler_params=_params("parallel"),
        name="dsa_attention_sample",
    )(*args, nbias, fbias)


def _reduce_rows(x, op, final):
    blk = 8 * SUBLANES
    parts = [x[r0:r0 + blk] for r0 in range(0, x.shape[0], blk)]
    while len(parts) > 1:
        parts = [op(parts[i], parts[i + 1]) for i in range(0, len(parts) - 1, 2)] + parts[len(parts) & ~1:]
    return final(parts[0], axis=0, keepdims=True)


def _dsa_prompt_tile(g, lk, qa_ref, qi_ref, qm_ref, corr_ref, o_ref,
                     kb_ref, vt_ref, kib_ref, key_ref, sel_ref, s_ref, ot_ref, *, topk, tq, wi_col):
    rb = 2 * LANES if lk % (2 * LANES) == 0 else LANES
    nw = corr_ref.shape[2]
    qa = (qa_ref[...] * (HEAD_DIM ** -0.5)).astype(BF16)
    qi = qi_ref[...].astype(BF16)
    wi = qm_ref[:, wi_col:wi_col + N_IDX_HEADS] * (N_IDX_HEADS ** -0.5) * (D_IDX ** -0.5)
    eye_h = jnp.where(_iota((N_IDX_HEADS, N_IDX_HEADS), 0) == _iota((N_IDX_HEADS, N_IDX_HEADS), 1), 1.0, 0.0)
    wi_t = _dot_nt(eye_h, wi, HI)
    q_chunk = (g * tq + _iota((1, tq), 1)) // CHUNK

    for r0 in range(0, lk, rb):
        kib = kib_ref[r0:r0 + rb, :D_IDX]
        acc = jnp.zeros((rb, tq), F32)
        for n in range(N_IDX_HEADS):
            acc = acc + jnp.maximum(_dot_nt(kib, qi[:, n * D_IDX:(n + 1) * D_IDX]), 0.0) * wi_t[n:n + 1, :]
        adm = (r0 + _iota((rb, 1), 0)) // CHUNK <= q_chunk
        key_ref[r0:r0 + rb, :] = jnp.where(adm, _sortable(acc), jnp.int32(INT_MIN))

    kf32 = float(topk)
    count = lambda mask: _reduce_rows(jnp.where(mask, 1.0, 0.0), jnp.add, jnp.sum)

    def body(i, t_u):
        cand_u = t_u | lax.shift_left(jnp.int32(1), 31 - i)
        cand_s = cand_u ^ jnp.int32(INT_MIN)
        return jnp.where(count(key_ref[0:lk, :] >= cand_s) >= kf32, cand_u, t_u)

    t_u = lax.fori_loop(0, 32, body, jnp.zeros((1, tq), jnp.int32))
    thr = t_u ^ jnp.int32(INT_MIN)
    keys = key_ref[0:lk, :]
    need = kf32 - count(keys > thr)
    open_row = thr == jnp.int32(INT_MIN)
    conflict = jnp.logical_and(count(keys == thr) != need, jnp.logical_not(open_row))
    sel_ref[0:lk, :] = jnp.where(keys >= jnp.maximum(thr, jnp.int32(INT_MIN + 1)), 0.0, NEG)

    @pl.when(jnp.max(jnp.where(conflict, 1.0, 0.0)) > 0.0)
    def _():
        below = jnp.where(_iota((LANES, LANES), 1) < _iota((LANES, LANES), 0), 1.0, 0.0).astype(BF16)
        offset = jnp.zeros((1, tq), F32)
        for r0 in range(0, lk, LANES):
            kblk = key_ref[r0:r0 + LANES, :]
            e = jnp.where(kblk == thr, 1.0, 0.0)
            rank = offset + jnp.dot(below, e.astype(BF16), preferred_element_type=F32)
            take = jnp.where(kblk > thr, 1.0, jnp.where(jnp.logical_or(rank >= need, open_row), 0.0, e))
            sel_ref[r0:r0 + LANES, :] = jnp.where(take > 0.5, 0.0, NEG)
            offset = offset + jnp.sum(e, axis=0, keepdims=True)

    v = jnp.minimum(g, 1)
    start = pl.multiple_of(jnp.maximum(g * tq - T5_FAR, 0), LANES)
    for h in range(N_HEADS):
        hs = slice(h * HEAD_DIM, (h + 1) * HEAD_DIM)
        s_ref[0:lk, :] = _dot_nt(kb_ref[0:lk, hs], qa[:, hs]) + sel_ref[0:lk, :]
        s_ref[pl.ds(start, nw), :] += corr_ref[v, h]
        s = s_ref[0:lk, :]
        p = jnp.exp(s - _reduce_rows(s, jnp.maximum, jnp.max))
        pn = (p * (1.0 / _reduce_rows(p, jnp.add, jnp.sum))).astype(BF16)
        ot_ref[hs, :] = jnp.dot(vt_ref[hs, 0:lk], pn, preferred_element_type=F32)
    o_ref[...] = ot_ref[...].T


def _dsa_prompt_kernel(qa_ref, qi_ref, qm_ref, k_ref, v_ref, ki_ref, corr_ref, o_ref,
                       kb_ref, vt_ref, kib_ref, key_ref, sel_ref, s_ref, ot_ref, *, topk, tq, seq, n_groups, wi_col):
    g = pl.program_id(1)
    blk = _row_tile(seq, 4 * LANES)

    @pl.when(g == 0)
    def _():
        for r0 in range(0, seq, blk):
            kb_ref[r0:r0 + blk, :] = k_ref[0, r0:r0 + blk, :].astype(BF16)
            vt_ref[:, r0:r0 + blk] = v_ref[0, r0:r0 + blk, :].T.astype(BF16)
            kib_ref[r0:r0 + blk, :] = ki_ref[0, r0:r0 + blk, :].astype(BF16)

    per_group = (seq // tq) // n_groups
    for grp in range(n_groups):
        @pl.when(g // per_group == grp)
        def _():
            _dsa_prompt_tile(g, (grp + 1) * per_group * tq, qa_ref, qi_ref, qm_ref, corr_ref, o_ref,
                             kb_ref, vt_ref, kib_ref, key_ref, sel_ref, s_ref, ot_ref,
                             topk=topk, tq=tq, wi_col=wi_col)


def _dsa_prompt(qa, qi, qmisc, k, v, ki, corr, *, nb, tq, topk, wi_col, n_groups):
    n, w = qa.shape
    seq = k.shape[1]
    nq = seq // tq
    row = lambda b, g: (b * nq + g, 0)
    per_b = lambda b, g: (b, 0, 0)
    return pl.pallas_call(
        functools.partial(_dsa_prompt_kernel, topk=topk, tq=tq, seq=seq, n_groups=n_groups, wi_col=wi_col),
        grid=(nb, nq),
        in_specs=[pl.BlockSpec((tq, w), row),
                  pl.BlockSpec((tq, qi.shape[1]), row),
                  pl.BlockSpec((tq, qmisc.shape[1]), row),
                  pl.BlockSpec((1, seq, w), per_b),
                  pl.BlockSpec((1, seq, w), per_b),
                  pl.BlockSpec((1, seq, ki.shape[2]), per_b),
                  pl.BlockSpec(corr.shape, lambda b, g: (0, 0, 0, 0))],
        out_specs=pl.BlockSpec((tq, w), row),
        out_shape=jax.ShapeDtypeStruct(qa.shape, F32),
        scratch_shapes=[pltpu.VMEM((seq, w), BF16),
                        pltpu.VMEM((w, seq), BF16),
                        pltpu.VMEM((seq, ki.shape[2]), BF16),
                        pltpu.VMEM((seq, tq), jnp.int32),
                        pltpu.VMEM((seq, tq), F32),
                        pltpu.VMEM((seq, tq), F32),
                        pltpu.VMEM((w, tq), F32)],
        compiler_params=_params("parallel", "arbitrary"),
        name="dsa_attention_prompt",
    )(qa, qi, qmisc, k, v, ki, corr)


def _block_rows(x, op):
    blk = 8 * SUBLANES
    parts = [x[r0:r0 + blk] for r0 in range(0, x.shape[0], blk)]
    while len(parts) > 1:
        parts = [op(parts[i], parts[i + 1]) for i in range(0, len(parts) - 1, 2)] + parts[len(parts) & ~1:]
    return parts[0]


def _dsa_blocks_kernel(qa_ref, qi_ref, qm_ref, k_ref, v_ref, ki_ref, corr_ref, o_ref,
                       kb_ref, vt_ref, kib_ref, key_ref, sel_ref, thr_ref, stat_ref, m_ref, l_ref, ot_ref, *,
                       topk, tq, seq, wi_col):
    g = pl.program_id(1)
    kb = 2 * LANES
    w = qa_ref.shape[1]
    blk = _row_tile(seq, 4 * LANES)

    @pl.when(g == 0)
    def _():
        for r0 in range(0, seq, blk):
            kb_ref[r0:r0 + blk, :] = k_ref[0, r0:r0 + blk, :].astype(BF16)
            kib_ref[r0:r0 + blk, :] = ki_ref[0, r0:r0 + blk, :].astype(BF16)
        for j in range(seq // LANES):
            vt_ref[j] = v_ref[0, j * LANES:(j + 1) * LANES, :].T.astype(BF16)

    top = (g + 1) * tq
    nkb = (top + kb - 1) // kb
    start_of = lambda i: pl.multiple_of(jnp.maximum(top - kb * (i + 1), 0), LANES)

    qa = (qa_ref[...] * (HEAD_DIM ** -0.5)).astype(BF16)
    qi = qi_ref[...].astype(BF16)
    wi = qm_ref[:, wi_col:wi_col + N_IDX_HEADS] * (N_IDX_HEADS ** -0.5) * (D_IDX ** -0.5)
    eye_h = jnp.where(_iota((N_IDX_HEADS, N_IDX_HEADS), 0) == _iota((N_IDX_HEADS, N_IDX_HEADS), 1), 1.0, 0.0)
    wi_t = _dot_nt(eye_h, wi, HI)
    q_chunk = (g * tq + _iota((1, tq), 1)) // CHUNK

    def score_block(i, carry):
        st = start_of(i)
        kib = kib_ref[pl.ds(st, kb), :][:, :D_IDX]
        acc = jnp.zeros((kb, tq), F32)
        for n in range(N_IDX_HEADS):
            acc = acc + jnp.maximum(_dot_nt(kib, qi[:, n * D_IDX:(n + 1) * D_IDX]), 0.0) * wi_t[n:n + 1, :]
        row = st + _iota((kb, 1), 0)
        ok = jnp.logical_and(row // CHUNK <= q_chunk, row < top - kb * i)
        key_ref[i] = jnp.where(ok, _sortable(acc), jnp.int32(INT_MIN))
        return carry

    lax.fori_loop(0, nkb, score_block, 0)

    kf32 = float(topk)

    def search_blocks(n_blocks):
        def count(pred):
            parts = [_block_rows(jnp.where(pred(key_ref[i]), 1.0, 0.0), jnp.add) for i in range(n_blocks)]
            while len(parts) > 1:
                parts = [a + b for a, b in zip(parts[0::2], parts[1::2])] + parts[len(parts) & ~1:]
            return jnp.sum(parts[0], axis=0, keepdims=True)

        def search(b, t_u):
            cand_u = t_u | lax.shift_left(jnp.int32(1), 31 - b)
            cand_s = cand_u ^ jnp.int32(INT_MIN)
            return jnp.where(count(lambda x: x >= cand_s) >= kf32, cand_u, t_u)

        t_u = lax.fori_loop(0, 32, search, jnp.zeros((1, tq), jnp.int32))
        thr = t_u ^ jnp.int32(INT_MIN)
        thr_ref[0:1, :] = thr
        stat_ref[0:1, :] = kf32 - count(lambda x: x > thr)
        stat_ref[1:2, :] = count(lambda x: x == thr)

    for n_blocks in range(1, seq // kb + 1):
        pl.when(nkb == n_blocks)(functools.partial(search_blocks, n_blocks))

    thr = thr_ref[0:1, :]
    need = stat_ref[0:1, :]
    open_row = thr == jnp.int32(INT_MIN)
    conflict = jnp.logical_and(stat_ref[1:2, :] != need, jnp.logical_not(open_row))
    floor = jnp.maximum(thr, jnp.int32(INT_MIN + 1))

    def select_block(i, carry):
        sel_ref[i] = jnp.where(key_ref[i] >= floor, 0.0, NEG)
        return carry

    lax.fori_loop(0, nkb, select_block, 0)

    @pl.when(jnp.max(jnp.where(conflict, 1.0, 0.0)) > 0.0)
    def _():
        below = jnp.where(_iota((LANES, LANES), 1) < _iota((LANES, LANES), 0), 1.0, 0.0).astype(BF16)

        def tie_block(ii, offset):
            i = nkb - 1 - ii
            for r0 in range(0, kb, LANES):
                kblk = key_ref[i, r0:r0 + LANES, :]
                e = jnp.where(kblk == thr, 1.0, 0.0)
                rank = offset + jnp.dot(below, e.astype(BF16), preferred_element_type=F32)
                take = jnp.where(kblk > thr, 1.0, jnp.where(jnp.logical_or(rank >= need, open_row), 0.0, e))
                sel_ref[i, r0:r0 + LANES, :] = jnp.where(take > 0.5, 0.0, NEG)
                offset = offset + jnp.sum(e, axis=0, keepdims=True)
            return offset

        lax.fori_loop(0, nkb, tie_block, jnp.zeros((1, tq), F32))

    v = jnp.minimum(g, 1)

    heads = range(N_HEADS)
    hs = [slice(h * HEAD_DIM, (h + 1) * HEAD_DIM) for h in heads]

    def attend_block(i, first):
        st = start_of(i)
        vblk = st // LANES
        sel = sel_ref[i]
        s = [_dot_nt(kb_ref[pl.ds(st, kb), hs[h]], qa[:, hs[h]]) + sel for h in heads]
        if first:
            s = [s[h] + corr_ref[v, h] for h in heads]
        m_blk = [jnp.max(_block_rows(x, jnp.maximum), axis=0, keepdims=True) for x in s]
        m_old = [m_ref[h:h + 1, :] for h in heads]
        m_new = m_blk if first else [jnp.maximum(a, b) for a, b in zip(m_old, m_blk)]
        p = [jnp.exp(x - m) for x, m in zip(s, m_new)]
        l_blk = [jnp.sum(_block_rows(x, jnp.add), axis=0, keepdims=True) for x in p]
        pb = [x.astype(BF16) for x in p]
        pv = [sum(jnp.dot(vt_ref[vblk + j, hs[h], :], pb[h][j * LANES:(j + 1) * LANES, :],
                          preferred_element_type=F32) for j in range(kb // LANES)) for h in heads]
        for h in heads:
            if first:
                l_ref[h:h + 1, :] = l_blk[h]
                ot_ref[hs[h], :] = pv[h]
            else:
                alpha = jnp.exp(m_old[h] - m_new[h])
                l_ref[h:h + 1, :] = l_ref[h:h + 1, :] * alpha + l_blk[h]
                ot_ref[hs[h], :] = ot_ref[hs[h], :] * alpha + pv[h]
            m_ref[h:h + 1, :] = m_new[h]

    attend_block(0, True)

    def attend_rest(i, carry):
        attend_block(i, False)
        return carry

    lax.fori_loop(1, nkb, attend_rest, 0)
    for h in heads:
        ot_ref[hs[h], :] = ot_ref[hs[h], :] * (1.0 / l_ref[h:h + 1, :])
    o_ref[...] = ot_ref[...].T


def _dsa_blocks(qa, qi, qmisc, k, v, ki, corr, *, nb, tq, topk, wi_col):
    n, w = qa.shape
    seq = k.shape[1]
    nq = seq // tq
    kb = 2 * LANES
    row = lambda b, g: (b * nq + g, 0)
    per_b = lambda b, g: (b, 0, 0)
    return pl.pallas_call(
        functools.partial(_dsa_blocks_kernel, topk=topk, tq=tq, seq=seq, wi_col=wi_col),
        grid=(nb, nq),
        in_specs=[pl.BlockSpec((tq, w), row),
                  pl.BlockSpec((tq, qi.shape[1]), row),
                  pl.BlockSpec((tq, qmisc.shape[1]), row),
                  pl.BlockSpec((1, seq, w), per_b),
                  pl.BlockSpec((1, seq, w), per_b),
                  pl.BlockSpec((1, seq, ki.shape[2]), per_b),
                  pl.BlockSpec(corr.shape, lambda b, g: (0, 0, 0, 0))],
        out_specs=pl.BlockSpec((tq, w), row),
        out_shape=jax.ShapeDtypeStruct(qa.shape, F32),
        scratch_shapes=[pltpu.VMEM((seq, w), BF16),
                        pltpu.VMEM((seq // LANES, w, LANES), BF16),
                        pltpu.VMEM((seq, ki.shape[2]), BF16),
                        pltpu.VMEM((seq // kb, kb, tq), jnp.int32),
                        pltpu.VMEM((seq // kb, kb, tq), F32),
                        pltpu.VMEM((SUBLANES, tq), jnp.int32),
                        pltpu.VMEM((SUBLANES, tq), F32),
                        pltpu.VMEM((N_HEADS, tq), F32),
                        pltpu.VMEM((N_HEADS, tq), F32),
                        pltpu.VMEM((w, tq), F32)],
        compiler_params=_params("parallel", "arbitrary"),
        name="dsa_attention_prompt",
    )(qa, qi, qmisc, k, v, ki, corr)


def _unit_lower_inverse(mats, n):
    row, col = _iota((n, n), 0), _iota((n, n), 1)
    same = lambda width: row // width == col // width
    dot = functools.partial(jnp.dot, preferred_element_type=F32)
    eye = jnp.where(row == col, 1.0, 0.0)
    diag = [jnp.where(same(INV_BLOCK), a, 0.0) for a in mats]
    ts = [eye - d for d in diag]
    dks = [_split(d) for d in diag]
    k = 1
    while 2 * k < INV_BLOCK:
        dks = [_split(_dot3(dk, dk)) for dk in dks]
        ts = [t + _dot3(_split(t), dk) for t, dk in zip(ts, dks)]
        k *= 2
    width = INV_BLOCK
    while width < n:
        off = jnp.logical_and(same(2 * width), jnp.logical_not(same(width)))
        ls = [jnp.where(off, a, 0.0).astype(BF16) for a in mats]
        tb = [t.astype(BF16) for t in ts]
        tl = [dot(t, l).astype(BF16) for t, l in zip(tb, ls)]
        ts = [t - dot(x, y) for t, x, y in zip(ts, tl, tb)]
        width *= 2
    return ts


def _head_rms(x, w):
    return x * lax.rsqrt(jnp.mean(x * x, axis=-1, keepdims=True) + 1e-6) * w


def _gdn_kernel(qkv_ref, misc_ref, z_ref, hist_ref, s0_ref, cw_ref, alog_ref, dtb_ref, nw_ref,
                y_ref, sfin_ref, newhist_ref, ext_ref, s_ref, *, c, a_col, b_col):
    j = pl.program_id(1)
    hw = CONV_B - 1
    base = SUBLANES
    wq = N_HEADS * DK_B

    @pl.when(j == 0)
    def _():
        ext_ref[base - hw:base, :] = hist_ref[0]
        s_ref[...] = s0_ref[0]

    @pl.when(j > 0)
    def _():
        ext_ref[base - hw:base, :] = ext_ref[base + c - hw:base + c, :]

    ext_ref[base:base + c, :] = qkv_ref[...]
    newhist_ref[0] = ext_ref[base + c - hw:base + c, :]
    conv = None
    for k in range(CONV_B):
        term = ext_ref[base - hw + k:base - hw + k + c, :] * cw_ref[k:k + 1, :]
        conv = term if conv is None else conv + term
    act = conv * _sigmoid(conv)

    beta = _sigmoid(misc_ref[:, b_col:b_col + N_HEADS])
    g = -jnp.exp(alog_ref[...]) * _softplus(misc_ref[:, a_col:a_col + N_HEADS] + dtb_ref[...])
    ri = _iota((c, c), 0)
    ci = _iota((c, c), 1)
    lower = ri >= ci
    strict = ri > ci
    gc = _dot_hi(jnp.where(lower, 1.0, 0.0), g)
    eye_h = jnp.where(_iota((N_HEADS, N_HEADS), 0) == _iota((N_HEADS, N_HEADS), 1), 1.0, 0.0)
    gc_t = _dot_nt(eye_h, gc, HI)
    eg = jnp.exp(gc)
    g_last = gc[c - 1:c, :]
    e_last = jnp.exp(g_last)
    e_rest = jnp.exp(g_last - gc)
    z = z_ref[...]
    heads = range(N_HEADS)
    col = lambda x, h: x[:, h:h + 1]
    qs = [act[:, h * DK_B:(h + 1) * DK_B] for h in heads]
    ks = [act[:, wq + h * DK_B:wq + (h + 1) * DK_B] for h in heads]
    vs = [act[:, 2 * wq + h * DV_B:2 * wq + (h + 1) * DV_B] for h in heads]
    qs = [q * lax.rsqrt(jnp.sum(q * q, axis=-1, keepdims=True) + 1e-6) * (DK_B ** -0.5) for q in qs]
    ks = [k * lax.rsqrt(jnp.sum(k * k, axis=-1, keepdims=True) + 1e-6) for k in ks]
    kbs = [k.astype(BF16) for k in ks]
    decay = [jnp.where(lower, jnp.exp(jnp.where(lower, col(gc, h) - gc_t[h:h + 1, :], 0.0)), 0.0) for h in heads]
    kk = [_dot_nt(kbs[h], kbs[h]) for h in heads]
    attn = [_dot_nt(qs[h].astype(BF16), kbs[h]) * decay[h] for h in heads]
    a_mat = [jnp.where(strict, col(beta, h) * kk[h] * decay[h], 0.0) for h in heads]
    t_mat = [t.astype(BF16) for t in _unit_lower_inverse(a_mat, c)]
    value = [_dot(t_mat[h], vs[h] * col(beta, h)) for h in heads]
    k_cum = [_dot(t_mat[h], ks[h] * (col(beta, h) * col(eg, h))) for h in heads]
    s_old = [s_ref[h] for h in heads]
    sbs = [s.astype(BF16) for s in s_old]
    v_new = [(value[h] - _dot(k_cum[h], sbs[h])).astype(BF16) for h in heads]
    o_inter = [_dot(qs[h] * col(eg, h), sbs[h]) for h in heads]
    o = [o_inter[h] + _dot(attn[h], v_new[h]) for h in heads]
    for h in heads:
        s_ref[h] = s_old[h] * col(e_last, h) + _dot_tn((ks[h] * col(e_rest, h)).astype(BF16), v_new[h])
    for h in heads:
        zh = z[:, h * DV_B:(h + 1) * DV_B]
        y_ref[:, h * DV_B:(h + 1) * DV_B] = _head_rms(o[h], nw_ref[...]) * (zh * _sigmoid(zh))
    sfin_ref[0] = s_ref[...]


def _gdn_multi_kernel(qkv_ref, misc_ref, z_ref, hist_ref, s0_ref, cw_ref, alog_ref, dtb_ref, nw_ref,
                      y_ref, sfin_ref, newhist_ref, ext_ref, s_ref, *, c, a_col, b_col):
    j = pl.program_id(1)
    hw = CONV_B - 1
    base = SUBLANES
    wq = N_HEADS * DK_B
    seqs = range(qkv_ref.shape[0])

    @pl.when(j == 0)
    def _():
        ext_ref[:, base - hw:base, :] = hist_ref[...]
        s_ref[...] = s0_ref[...]

    @pl.when(j > 0)
    def _():
        ext_ref[:, base - hw:base, :] = ext_ref[:, base + c - hw:base + c, :]

    ext_ref[:, base:base + c, :] = qkv_ref[...]
    newhist_ref[...] = ext_ref[:, base + c - hw:base + c, :]
    ri = _iota((c, c), 0)
    ci = _iota((c, c), 1)
    lower = ri >= ci
    strict = ri > ci
    tri = jnp.where(lower, 1.0, 0.0)
    eye_h = jnp.where(_iota((N_HEADS, N_HEADS), 0) == _iota((N_HEADS, N_HEADS), 1), 1.0, 0.0)

    def conv_act(b):
        conv = None
        for k in range(CONV_B):
            term = ext_ref[b, base - hw + k:base - hw + k + c, :] * cw_ref[k:k + 1, :]
            conv = term if conv is None else conv + term
        return conv * _sigmoid(conv)

    act = [conv_act(b) for b in seqs]
    beta = [_sigmoid(misc_ref[b, :, b_col:b_col + N_HEADS]) for b in seqs]
    g = [-jnp.exp(alog_ref[...]) * _softplus(misc_ref[b, :, a_col:a_col + N_HEADS] + dtb_ref[...]) for b in seqs]
    gc = [_dot_hi(tri, x) for x in g]
    gc_t = [_dot_nt(eye_h, x, HI) for x in gc]
    eg = [jnp.exp(x) for x in gc]
    e_last = [jnp.exp(x[c - 1:c, :]) for x in gc]
    e_rest = [jnp.exp(x[c - 1:c, :] - x) for x in gc]

    items = [(b, h) for b in seqs for h in range(N_HEADS)]
    col = lambda x, h: x[:, h:h + 1]
    qs = [act[b][:, h * DK_B:(h + 1) * DK_B] for b, h in items]
    ks = [act[b][:, wq + h * DK_B:wq + (h + 1) * DK_B] for b, h in items]
    vs = [act[b][:, 2 * wq + h * DV_B:2 * wq + (h + 1) * DV_B] for b, h in items]
    qs = [q * lax.rsqrt(jnp.sum(q * q, axis=-1, keepdims=True) + 1e-6) * (DK_B ** -0.5) for q in qs]
    ks = [k * lax.rsqrt(jnp.sum(k * k, axis=-1, keepdims=True) + 1e-6) for k in ks]
    kbs = [k.astype(BF16) for k in ks]
    decay = [jnp.where(lower, jnp.exp(jnp.where(lower, col(gc[b], h) - gc_t[b][h:h + 1, :], 0.0)), 0.0)
             for b, h in items]
    kk = [_dot_nt(k, k) for k in kbs]
    attn = [_dot_nt(q.astype(BF16), k) * d for q, k, d in zip(qs, kbs, decay)]
    a_mat = [jnp.where(strict, col(beta[b], h) * x * d, 0.0) for (b, h), x, d in zip(items, kk, decay)]
    t_mat = [t.astype(BF16) for t in _unit_lower_inverse(a_mat, c)]
    value = [_dot(t, v * col(beta[b], h)) for (b, h), t, v in zip(items, t_mat, vs)]
    k_cum = [_dot(t, k * (col(beta[b], h) * col(eg[b], h))) for (b, h), t, k in zip(items, t_mat, ks)]
    s_old = [s_ref[b, h] for b, h in items]
    sbs = [s.astype(BF16) for s in s_old]
    v_new = [(v - _dot(kc, s)).astype(BF16) for v, kc, s in zip(value, k_cum, sbs)]
    o_inter = [_dot(q * col(eg[b], h), s) for (b, h), q, s in zip(items, qs, sbs)]
    o = [oi + _dot(a, v) for oi, a, v in zip(o_inter, attn, v_new)]
    for i, (b, h) in enumerate(items):
        s_ref[b, h] = s_old[i] * col(e_last[b], h) + _dot_tn((ks[i] * col(e_rest[b], h)).astype(BF16), v_new[i])
    for i, (b, h) in enumerate(items):
        zh = z_ref[b, :, h * DV_B:(h + 1) * DV_B]
        y_ref[b, :, h * DV_B:(h + 1) * DV_B] = _head_rms(o[i], nw_ref[...]) * (zh * _sigmoid(zh))
    sfin_ref[...] = s_ref[...]


def _gdn(qkv, misc, z, hist, s0, conv_w, a_log, dt_bias, norm_w, *, nb, c, a_col, b_col):
    n, wqkv = qkv.shape
    seq = n // nb
    nbb = GDN_SEQS_PER_STEP if nb % GDN_SEQS_PER_STEP == 0 else 1
    tok = lambda b, j: (b, j, 0)
    per_b3 = lambda b, j: (b, 0, 0)
    per_b4 = lambda b, j: (b, 0, 0, 0)
    const2 = lambda b, j: (0, 0)
    hw = CONV_B - 1
    tokens = [a.reshape(nb, seq, a.shape[1]) for a in (qkv, misc, z)]
    y, s_fin, new_hist = pl.pallas_call(
        functools.partial(_gdn_multi_kernel, c=c, a_col=a_col, b_col=b_col),
        grid=(nb // nbb, seq // c),
        in_specs=[pl.BlockSpec((nbb, c, a.shape[2]), tok) for a in tokens]
                 + [pl.BlockSpec((nbb, hw, wqkv), per_b3),
                    pl.BlockSpec((nbb,) + s0.shape[1:], per_b4),
                    pl.BlockSpec((CONV_B, wqkv), const2),
                    pl.BlockSpec((1, N_HEADS), const2),
                    pl.BlockSpec((1, N_HEADS), const2),
                    pl.BlockSpec((1, DV_B), const2)],
        out_specs=[pl.BlockSpec((nbb, c, z.shape[1]), tok),
                   pl.BlockSpec((nbb,) + s0.shape[1:], per_b4),
                   pl.BlockSpec((nbb, hw, wqkv), per_b3)],
        out_shape=[jax.ShapeDtypeStruct((nb, seq, z.shape[1]), F32),
                   jax.ShapeDtypeStruct(s0.shape, F32),
                   jax.ShapeDtypeStruct(hist.shape, F32)],
        scratch_shapes=[pltpu.VMEM((nbb, SUBLANES + c, wqkv), F32),
                        pltpu.VMEM((nbb,) + s0.shape[1:], F32)],
        compiler_params=_params("parallel", "arbitrary"),
        name="gated_deltanet",
    )(*tokens, hist, s0, conv_w, a_log.reshape(1, -1), dt_bias.reshape(1, -1), norm_w.reshape(1, -1))
    return y.reshape(n, z.shape[1]), s_fin, new_hist


def _gdn_single(qkv, misc, z, hist, s0, conv_w, a_log, dt_bias, norm_w, *, nb, c, a_col, b_col):
    n, wqkv = qkv.shape
    nch = n // (nb * c)
    row = lambda b, j: (b * nch + j, 0)
    per_b3 = lambda b, j: (b, 0, 0)
    per_b4 = lambda b, j: (b, 0, 0, 0)
    const2 = lambda b, j: (0, 0)
    hw = CONV_B - 1
    return pl.pallas_call(
        functools.partial(_gdn_kernel, c=c, a_col=a_col, b_col=b_col),
        grid=(nb, nch),
        in_specs=[pl.BlockSpec((c, wqkv), row),
                  pl.BlockSpec((c, misc.shape[1]), row),
                  pl.BlockSpec((c, z.shape[1]), row),
                  pl.BlockSpec((1, hw, wqkv), per_b3),
                  pl.BlockSpec((1,) + s0.shape[1:], per_b4),
                  pl.BlockSpec((CONV_B, wqkv), const2),
                  pl.BlockSpec((1, N_HEADS), const2),
                  pl.BlockSpec((1, N_HEADS), const2),
                  pl.BlockSpec((1, DV_B), const2)],
        out_specs=[pl.BlockSpec((c, z.shape[1]), row),
                   pl.BlockSpec((1,) + s0.shape[1:], per_b4),
                   pl.BlockSpec((1, hw, wqkv), per_b3)],
        out_shape=[jax.ShapeDtypeStruct(z.shape, F32),
                   jax.ShapeDtypeStruct(s0.shape, F32),
                   jax.ShapeDtypeStruct(hist.shape, F32)],
        scratch_shapes=[pltpu.VMEM((SUBLANES + c, wqkv), F32),
                        pltpu.VMEM(s0.shape[1:], F32)],
        compiler_params=_params("parallel", "arbitrary"),
        name="gated_deltanet",
    )(qkv, misc, z, hist, s0, conv_w, a_log.reshape(1, -1), dt_bias.reshape(1, -1), norm_w.reshape(1, -1))


def _mlstm_kernel(qk_ref, v_ref, og_ref, misc_ref, c0_ref, n0_ref, m0_ref, ib_ref, fb_ref, nw_ref,
                  y_ref, cfin_ref, nfin_ref, mfin_ref, c_ref, n_ref, m_ref, *, l, nbb, i_col, f_col):
    j = pl.program_id(1)
    wq = N_HEADS * DK_C

    @pl.when(j == 0)
    def _():
        c_ref[...] = c0_ref[...]
        n_ref[...] = n0_ref[...]
        m_ref[...] = m0_ref[...]

    causal = _iota((l, l), 0) >= _iota((l, l), 1)
    tri = jnp.where(causal, 1.0, 0.0)
    eye_h = jnp.where(_iota((N_HEADS, N_HEADS), 0) == _iota((N_HEADS, N_HEADS), 1), 1.0, 0.0)
    seqs = range(nbb)
    ig = [misc_ref[b, :, i_col:i_col + N_HEADS] + ib_ref[...] for b in seqs]
    lf = [-_softplus(-(misc_ref[b, :, f_col:f_col + N_HEADS] + fb_ref[...])) for b in seqs]
    fc = [_dot_hi(tri, lf[b]) for b in seqs]
    row_terms = [_dot_nt(eye_h, ig[b] - fc[b], HI) for b in seqs]
    m_prev = [m_ref[b] for b in seqs]
    log_inter = [fc[b] + m_prev[b] for b in seqs]
    f_last = [fc[b][l - 1:l, :] for b in seqs]
    m_vec = [f_last[b] + jnp.maximum(m_prev[b], jnp.max(ig[b] - fc[b], axis=0, keepdims=True)) for b in seqs]
    dec_vec = [jnp.exp(f_last[b] + m_prev[b] - m_vec[b]) for b in seqs]
    w_last = [jnp.exp(f_last[b] - fc[b] + ig[b] - m_vec[b]) for b in seqs]

    items = [(b, h) for b in seqs for h in range(N_HEADS)]
    col = lambda x, h: x[:, h:h + 1]
    qs = [qk_ref[b, :, h * DK_C:(h + 1) * DK_C] for b, h in items]
    ks = [qk_ref[b, :, wq + h * DK_C:wq + (h + 1) * DK_C] * (DK_C ** -0.5) for b, h in items]
    vbs = [v_ref[b, :, h * DV_C:(h + 1) * DV_C].astype(BF16) for b, h in items]
    c_old = [c_ref[b, h] for b, h in items]
    n_old = [n_ref[b, h:h + 1, :] for b, h in items]
    qk_raw = [_dot_nt(q.astype(BF16), k.astype(BF16)) for q, k in zip(qs, ks)]
    q_c = [_dot(q, c) for q, c in zip(qs, c_old)]
    log_w = [jnp.where(causal, col(fc[b], h) + row_terms[b][h:h + 1, :], -jnp.inf) for b, h in items]
    li = [col(log_inter[b], h) for b, h in items]
    m_t = [jnp.maximum(a, jnp.max(lw, axis=-1, keepdims=True)) for a, lw in zip(li, log_w)]
    w_inter = [jnp.exp(a - m) for a, m in zip(li, m_t)]
    qkw = [r * jnp.exp(lw - m) for r, lw, m in zip(qk_raw, log_w, m_t)]
    num = [_dot(a, vb) + wi * qc for a, vb, wi, qc in zip(qkw, vbs, w_inter, q_c)]
    den = [jnp.sum(a, axis=-1, keepdims=True) + wi * jnp.sum(q * n, axis=-1, keepdims=True)
           for a, wi, q, n in zip(qkw, w_inter, qs, n_old)]
    hid = [nu / jnp.maximum(jnp.abs(de), jnp.exp(-m)) for nu, de, m in zip(num, den, m_t)]
    kw = [k * col(w_last[b], h) for k, (b, h) in zip(ks, items)]
    for i, (b, h) in enumerate(items):
        c_ref[b, h] = col(dec_vec[b], h) * c_old[i] + _dot_tn(kw[i].astype(BF16), vbs[i])
        n_ref[b, h:h + 1, :] = col(dec_vec[b], h) * n_old[i] + jnp.sum(kw[i], axis=0, keepdims=True)
    for i, (b, h) in enumerate(items):
        oh = og_ref[b, :, h * DV_C:(h + 1) * DV_C]
        y_ref[b, :, h * DV_C:(h + 1) * DV_C] = _head_rms(hid[i], nw_ref[...]) * _sigmoid(oh)
    for b in seqs:
        m_ref[b] = m_vec[b]
    cfin_ref[...] = c_ref[...]
    nfin_ref[...] = n_ref[...]
    mfin_ref[...] = m_ref[...]


def _mlstm(qk, v, og, misc, c0, n0, m0, i_bias, f_bias, norm_w, *, nb, l, i_col, f_col):
    n = qk.shape[0]
    seq = n // nb
    nbb = SEQS_PER_STEP if nb % SEQS_PER_STEP == 0 else 1
    tok = lambda b, j: (b, j, 0)
    per_b3 = lambda b, j: (b, 0, 0)
    per_b4 = lambda b, j: (b, 0, 0, 0)
    const2 = lambda b, j: (0, 0)
    m0 = m0.reshape(nb, 1, N_HEADS)
    tokens = [a.reshape(nb, seq, a.shape[1]) for a in (qk, v, og, misc)]
    outs = pl.pallas_call(
        functools.partial(_mlstm_kernel, l=l, nbb=nbb, i_col=i_col, f_col=f_col),
        grid=(nb // nbb, seq // l),
        in_specs=[pl.BlockSpec((nbb, l, a.shape[2]), tok) for a in tokens]
                 + [pl.BlockSpec((nbb,) + c0.shape[1:], per_b4),
                    pl.BlockSpec((nbb,) + n0.shape[1:], per_b3),
                    pl.BlockSpec((nbb, 1, N_HEADS), per_b3),
                    pl.BlockSpec((1, N_HEADS), const2),
                    pl.BlockSpec((1, N_HEADS), const2),
                    pl.BlockSpec((1, DV_C), const2)],
        out_specs=[pl.BlockSpec((nbb, l, v.shape[1]), tok),
                   pl.BlockSpec((nbb,) + c0.shape[1:], per_b4),
                   pl.BlockSpec((nbb,) + n0.shape[1:], per_b3),
                   pl.BlockSpec((nbb, 1, N_HEADS), per_b3)],
        out_shape=[jax.ShapeDtypeStruct((nb, seq, v.shape[1]), F32),
                   jax.ShapeDtypeStruct(c0.shape, F32),
                   jax.ShapeDtypeStruct(n0.shape, F32),
                   jax.ShapeDtypeStruct(m0.shape, F32)],
        scratch_shapes=[pltpu.VMEM((nbb,) + c0.shape[1:], F32),
                        pltpu.VMEM((nbb,) + n0.shape[1:], F32),
                        pltpu.VMEM((nbb, 1, N_HEADS), F32)],
        compiler_params=_params("parallel", "arbitrary"),
        name="mlstm",
    )(*tokens, c0, n0, m0, i_bias.reshape(1, -1), f_bias.reshape(1, -1), norm_w.reshape(1, -1))
    y, c_fin, n_fin, m_fin = outs
    return y.reshape(n, v.shape[1]), c_fin, n_fin, m_fin.reshape(nb, N_HEADS)


def _block_mask(rows, row_group, cols, col_group):
    return (np.arange(rows)[:, None] // row_group == np.arange(cols)[None, :] // col_group).astype(np.float32)


def _cummax_rows(x):
    rows = x.shape[0]
    row = _iota(x.shape, 0)
    sh = 1
    while sh < rows:
        x = jnp.maximum(x, jnp.where(row >= sh, pltpu.roll(x, sh, axis=0), -jnp.inf))
        sh *= 2
    return x


def _dot2(a, b):
    hi, lo = _split(a)
    return jnp.dot(hi, b, preferred_element_type=F32) + jnp.dot(lo, b, preferred_element_type=F32)


def _mlstm_dense_kernel(qk_ref, v_ref, og_ref, misc_ref, c0_ref, n0_ref, m0_ref, ib_ref, fb_ref, nw_ref,
                        el_ref, ev_ref, ek_ref, kmask_ref, vmask_ref, cmask_ref, cmaskb_ref, rms_ref, causal_ref,
                        dsel_ref,
                        y_ref, cfin_ref, nfin_ref, mfin_ref, c_ref, n_ref, m_ref, *, l, i_col, f_col):
    j = pl.program_id(1)
    wq = N_HEADS * DK_C
    nbb = qk_ref.shape[0]
    seqs = range(nbb)

    @pl.when(j == 0)
    def _():
        c_ref[...] = jnp.zeros(c_ref.shape, F32)
        for b in seqs:
            for h in range(N_HEADS):
                c_ref[b, h * DK_C:(h + 1) * DK_C, h * DV_C:(h + 1) * DV_C] = c0_ref[b, h]
        n_ref[...] = n0_ref[...]
        m_ref[...] = m0_ref[...]

    each = lambda f, *xs: [f(*args) for args in zip(*xs)]
    dot = functools.partial(jnp.dot, preferred_element_type=F32)
    spread = lambda x, e_ref: sum(dot(p, e_ref[...]) for p in _split3(x))
    tri = jnp.where(_iota((l, l), 0) >= _iota((l, l), 1), 1.0, 0.0).astype(BF16)
    ig = [misc_ref[b, :, i_col:i_col + N_HEADS] + ib_ref[...] for b in seqs]
    lf = [-_softplus(-(misc_ref[b, :, f_col:f_col + N_HEADS] + fb_ref[...])) for b in seqs]
    fc = each(lambda x: sum(dot(tri, p) for p in _split3(x)), lf)
    m_prev = [m_ref[b] for b in seqs]
    log_inter = each(jnp.add, fc, m_prev)
    a = each(jnp.subtract, ig, fc)
    m_t = each(lambda li, f, x: jnp.maximum(li, f + _cummax_rows(x)), log_inter, fc, a)
    w_inter = each(lambda li, m: jnp.exp(li - m), log_inter, m_t)
    w_last = each(lambda f, i, m: jnp.exp(f[l - 1:l, :] - f + i - m[l - 1:l, :]), fc, ig, m_t)

    key_terms = each(lambda x: jnp.sum(spread(x, el_ref) * dsel_ref[...], axis=0, keepdims=True), a)
    log_w = each(lambda f, kt: jnp.where(causal_ref[...] > 0.5, spread(f, el_ref) + kt, -jnp.inf), fc, key_terms)
    q = [qk_ref[b, :, :wq] for b in seqs]
    q_b = [x.astype(BF16) for x in q]
    k_s = [qk_ref[b, :, wq:] * (DK_C ** -0.5) for b in seqs]
    k_b = [x.astype(BF16) for x in k_s]
    v_b = [v_ref[b].astype(BF16) for b in seqs]
    kt_bd = [jnp.concatenate([x] * N_HEADS, axis=0) * kmask_ref[...] for x in k_b]
    v_bd = [jnp.concatenate([x] * N_HEADS, axis=0) * vmask_ref[...] for x in v_b]
    m_l = each(lambda m: spread(m, el_ref), m_t)
    qkw = each(lambda x, kt, lw, m: (_dot_nt(x, kt) * jnp.exp(lw - m)).astype(BF16), q_b, kt_bd, log_w, m_l)
    c_old = [c_ref[b] for b in seqs]
    n_old = [n_ref[b] for b in seqs]
    wi_s = each(lambda w: spread(w, ev_ref), w_inter)
    q_c = each(lambda x, c: dot(x, c.astype(BF16)), q_b, c_old)
    qn = each(lambda x, n: _dot2(x * n, cmaskb_ref[...]), q, n_old)
    num = each(lambda w, vb, wi, qc: dot(w, vb) + wi * qc, qkw, v_bd, wi_s, q_c)
    den = each(lambda w, wi, x: dot(w, vmask_ref[...]) + wi * x, qkw, wi_s, qn)
    m_v = m_l if l == DV_C else each(lambda m: spread(m, ev_ref), m_t)
    hid = each(lambda nu, de, m: nu / jnp.maximum(jnp.abs(de), jnp.exp(-m)), num, den, m_v)
    ms = each(lambda x: _dot2(x * x, rms_ref[...]), hid)
    for b in seqs:
        y_ref[b] = hid[b] * lax.rsqrt(ms[b] + 1e-6) * nw_ref[...] * _sigmoid(og_ref[b])

    kw = each(lambda k, w: k * spread(w, ek_ref), k_s, w_last)
    dec_rows = [jnp.broadcast_to(w[l - 1:l, :], (SUBLANES, N_HEADS)) for w in w_inter]
    upd = each(lambda x, vb: _dot_tn(x.astype(BF16), vb), kw, v_b)
    for b in seqs:
        c_ref[b] = c_old[b] * spread(dec_rows[b], ev_ref)[0:1, :] + upd[b] * cmask_ref[...]
        n_ref[b] = n_old[b] * spread(dec_rows[b], ek_ref)[0:1, :] + jnp.sum(kw[b], axis=0, keepdims=True)
        m_ref[b] = m_t[b][l - 1:l, :]

    @pl.when(j == pl.num_programs(1) - 1)
    def _():
        for b in seqs:
            for h in range(N_HEADS):
                cfin_ref[b, h] = c_ref[b, h * DK_C:(h + 1) * DK_C, h * DV_C:(h + 1) * DV_C]
        nfin_ref[...] = n_ref[...]
        mfin_ref[...] = m_ref[...]


def _mlstm_dense(qk, v, og, misc, c0, n0, m0, i_bias, f_bias, norm_w, *, nb, l, i_col, f_col):
    n = qk.shape[0]
    seq = n // nb
    wq, wv = N_HEADS * DK_C, N_HEADS * DV_C
    hl = N_HEADS * l
    nbb = SEQS_PER_STEP if nb % SEQS_PER_STEP == 0 else 1
    tok = lambda b, j: (b, j, 0)
    per_b3 = lambda b, j: (b, 0, 0)
    per_b4 = lambda b, j: (b, 0, 0, 0)
    const2 = lambda b, j: (0, 0)
    consts = [jnp.asarray(_block_mask(N_HEADS, 1, hl, l), BF16),
              jnp.asarray(_block_mask(N_HEADS, 1, wv, DV_C), BF16),
              jnp.asarray(_block_mask(N_HEADS, 1, wq, DK_C), BF16),
              jnp.asarray(_block_mask(hl, l, wq, DK_C), BF16),
              jnp.asarray(_block_mask(hl, l, wv, DV_C), BF16),
              jnp.asarray(_block_mask(wq, DK_C, wv, DV_C)),
              jnp.asarray(_block_mask(wq, DK_C, wv, DV_C), BF16),
              jnp.asarray(_block_mask(wv, DV_C, wv, DV_C) / DV_C, BF16),
              jnp.asarray(np.tile(np.tril(np.ones((l, l), np.float32)), (1, N_HEADS))),
              jnp.asarray(np.tile(np.eye(l, dtype=np.float32), (1, N_HEADS)))]
    tokens = [a.reshape(nb, seq, a.shape[1]) for a in (qk, v, og, misc)]
    outs = pl.pallas_call(
        functools.partial(_mlstm_dense_kernel, l=l, i_col=i_col, f_col=f_col),
        grid=(nb // nbb, seq // l),
        in_specs=[pl.BlockSpec((nbb, l, a.shape[2]), tok) for a in tokens]
                 + [pl.BlockSpec((nbb,) + c0.shape[1:], per_b4),
                    pl.BlockSpec((nbb, 1, wq), per_b3),
                    pl.BlockSpec((nbb, 1, N_HEADS), per_b3),
                    pl.BlockSpec((1, N_HEADS), const2),
                    pl.BlockSpec((1, N_HEADS), const2),
                    pl.BlockSpec((1, wv), const2)]
                 + [pl.BlockSpec(c.shape, const2) for c in consts],
        out_specs=[pl.BlockSpec((nbb, l, wv), tok),
                   pl.BlockSpec((nbb,) + c0.shape[1:], per_b4),
                   pl.BlockSpec((nbb, 1, wq), per_b3),
                   pl.BlockSpec((nbb, 1, N_HEADS), per_b3)],
        out_shape=[jax.ShapeDtypeStruct((nb, seq, wv), F32),
                   jax.ShapeDtypeStruct(c0.shape, F32),
                   jax.ShapeDtypeStruct((nb, 1, wq), F32),
                   jax.ShapeDtypeStruct((nb, 1, N_HEADS), F32)],
        scratch_shapes=[pltpu.VMEM((nbb, wq, wv), F32),
                        pltpu.VMEM((nbb, 1, wq), F32),
                        pltpu.VMEM((nbb, 1, N_HEADS), F32)],
        compiler_params=_params("parallel", "arbitrary"),
        name="mlstm",
    )(*tokens, c0, n0.reshape(nb, 1, wq), m0.reshape(nb, 1, N_HEADS), i_bias.reshape(1, -1), f_bias.reshape(1, -1),
      jnp.tile(norm_w, N_HEADS).reshape(1, wv), *consts)
    y, c_fin, n_fin, m_fin = outs
    return y.reshape(n, wv), c_fin, n_fin.reshape(nb, N_HEADS, DK_C), m_fin.reshape(nb, N_HEADS)


def _band_prompt_kernel(q_ref, k_ref, v_ref, bias_ref, o_ref, kb_ref, vt_ref, ot_ref, *, tq, seq):
    g = pl.program_id(1)
    w = q_ref.shape[1]
    pad = BAND_CHUNKS * CHUNK
    lw = pad + tq
    npad = pad // LANES
    per_tile = tq // LANES
    blk = _row_tile(seq, 4 * LANES)

    @pl.when(g == 0)
    def _():
        kb_ref[0:pad, :] = jnp.zeros((pad, w), BF16)
        vt_ref[0:npad] = jnp.zeros((npad, w, LANES), BF16)
        for r0 in range(0, seq, blk):
            kb_ref[pad + r0:pad + r0 + blk, :] = k_ref[0, r0:r0 + blk, :].astype(BF16)
        for j in range(seq // LANES):
            vt_ref[npad + j] = v_ref[0, j * LANES:(j + 1) * LANES, :].T.astype(BF16)

    start = pl.multiple_of(g * tq, tq)
    q = (q_ref[...] * (HEAD_DIM ** -0.5)).astype(BF16)
    before_seq = jnp.where(_iota((lw, tq), 0) >= pad - g * tq, 0.0, NEG)
    for h in range(N_HEADS):
        hs = slice(h * HEAD_DIM, (h + 1) * HEAD_DIM)
        s = _dot_nt(kb_ref[pl.ds(start, lw), hs], q[:, hs]) + (bias_ref[h] + before_seq)
        p = jnp.exp(s - _reduce_rows(s, jnp.maximum, jnp.max))
        pn = (p * (1.0 / _reduce_rows(p, jnp.add, jnp.sum))).astype(BF16)
        acc = None
        for j in range(lw // LANES):
            part = jnp.dot(vt_ref[g * per_tile + j, hs, :], pn[j * LANES:(j + 1) * LANES, :],
                           preferred_element_type=F32)
            acc = part if acc is None else acc + part
        ot_ref[hs, :] = acc
    o_ref[...] = ot_ref[...].T


def _band_prompt(q, k, v, bias, *, nb, tq):
    n, w = q.shape
    seq = k.shape[1]
    nq = seq // tq
    pad = BAND_CHUNKS * CHUNK
    row = lambda b, g: (b * nq + g, 0)
    per_b = lambda b, g: (b, 0, 0)
    return pl.pallas_call(
        functools.partial(_band_prompt_kernel, tq=tq, seq=seq),
        grid=(nb, nq),
        in_specs=[pl.BlockSpec((tq, w), row),
                  pl.BlockSpec((1, seq, w), per_b),
                  pl.BlockSpec((1, seq, w), per_b),
                  pl.BlockSpec(bias.shape, lambda b, g: (0, 0, 0))],
        out_specs=pl.BlockSpec((tq, w), row),
        out_shape=jax.ShapeDtypeStruct(q.shape, F32),
        scratch_shapes=[pltpu.VMEM((pad + seq, w), BF16),
                        pltpu.VMEM(((pad + seq) // LANES, w, LANES), BF16),
                        pltpu.VMEM((w, tq), F32)],
        compiler_params=_params("parallel", "arbitrary"),
        name="band_attention_prompt",
    )(q, k, v, bias)


def _band_sample_kernel(q_ref, kc_ref, vc_ref, kn_ref, vn_ref, biasc_ref, biasn_ref, o_ref):
    pieces = [(lambda h: kc_ref[0, 0, h], lambda h: vc_ref[0, 0, h], biasc_ref),
              (lambda h: kn_ref[0, h], lambda h: vn_ref[0, h], biasn_ref)]
    q = q_ref[...]
    for h in range(N_HEADS):
        hs = slice(h * HEAD_DIM, (h + 1) * HEAD_DIM)
        qh = q[:, hs].astype(BF16)
        scores = [jnp.dot(qh, load_k(h).astype(BF16), preferred_element_type=F32) * (HEAD_DIM ** -0.5) + b_ref[h]
                  for load_k, _, b_ref in pieces]
        m = functools.reduce(jnp.maximum, [jnp.max(s, axis=-1, keepdims=True) for s in scores])
        ps = [jnp.exp(s - m) for s in scores]
        den = sum(jnp.sum(p, axis=-1, keepdims=True) for p in ps)
        o = sum(_dot_nt(p.astype(BF16), load_v(h).astype(BF16)) for p, (_, load_v, _) in zip(ps, pieces))
        o_ref[:, hs] = o / den


def _band_sample(q, kc, vc, layer, kn, vn, bias_c, bias_n, *, nb, tq):
    n, w = q.shape
    row = lambda b: (b, 0)
    const3 = lambda b: (0, 0, 0)
    return pl.pallas_call(
        _band_sample_kernel,
        grid=(nb,),
        in_specs=[pl.BlockSpec((tq, w), row)]
                 + [pl.BlockSpec((1, 1) + a.shape[2:], lambda b: (layer, b, 0, 0, 0)) for a in (kc, vc)]
                 + [pl.BlockSpec((1,) + a.shape[1:], lambda b: (b, 0, 0, 0)) for a in (kn, vn)]
                 + [pl.BlockSpec(bias_c.shape, const3), pl.BlockSpec(bias_n.shape, const3)],
        out_specs=pl.BlockSpec((tq, w), row),
        out_shape=jax.ShapeDtypeStruct(q.shape, F32),
        compiler_params=_params("parallel"),
        name="band_attention_sample",
    )(q, kc, vc, kn, vn, bias_c, bias_n)


def _t5_bucket(rel):
    nb = T5_BUCKETS // 2
    max_exact = nb // 2
    n = jnp.abs(rel)
    n_f = jnp.maximum(n, 1).astype(jnp.float32)
    large = max_exact + (jnp.log(n_f / max_exact) / math.log(T5_MAX_DIST / max_exact) * (nb - max_exact)).astype(jnp.int32)
    large = jnp.minimum(large, nb - 1)
    return jnp.where(rel > 0, nb, 0) + jnp.where(n < max_exact, n, large)


def _toeplitz_bias(fn, n_rows, n_cols):
    n = n_rows + n_cols
    m = np.arange(n)
    f = jnp.transpose(fn(np.where(m < n_cols, m, m - n))).astype(F32)
    flat = jnp.tile(f, (1, n_rows))[:, :n_rows * (n - 1)]
    return flat.reshape(f.shape[0], n_rows, n - 1)[:, :, :n_cols]


def _pack_cols(w, sizes, groups):
    offs = np.concatenate([[0], np.cumsum(sizes)])
    cols, widths = [], []
    for grp in groups:
        width = 0
        for idx in grp:
            cols.append(w[:, offs[idx]:offs[idx + 1]])
            width += sizes[idx]
        pad = (-width) % LANES
        if pad:
            cols.append(jnp.zeros((w.shape[0], pad), w.dtype))
        widths.append(width + pad)
    return jnp.concatenate(cols, axis=1).astype(BF16), tuple(widths)


def _row_tile(n, target):
    t = min(n, target)
    while n % t:
        t //= 2
    return t


def kernel(x_prompt, x_sample, cache_a_k, cache_a_v, cache_a_kidx, state_b_s, state_b_conv, state_c_c, state_c_n, state_c_m, cache_d_k, cache_d_v, state_ffn_conv, w_in_even, w_out_even, t5_bias, b_conv_w, b_a_log, b_dt_bias, b_norm_w, w_in_odd, w_out_odd, c_i_bias, c_f_bias, c_norm_w, d_rel_bias, ln_mix_g, ln_mix_b, ln_ffn_g, ln_ffn_b, ffn_w_up, ffn_conv_w, ffn_w_down):
    bp, sp, d = x_prompt.shape
    bs, ts, _ = x_sample.shape
    depth = ffn_w_up.shape[0]
    past = cache_a_k.shape[2]
    d_win = cache_d_k.shape[2]
    dff = ffn_w_down.shape[1]
    alpha = (2 * depth) ** 0.25
    w_a = N_HEADS * HEAD_DIM
    w_b = N_HEADS * DV_B
    w_c = N_HEADS * DV_C
    qkv_b_w = 2 * N_HEADS * DK_B + w_b
    even_sizes = (w_a, w_a, w_a, N_IDX_HEADS * D_IDX, D_IDX, N_IDX_HEADS, qkv_b_w, N_HEADS, N_HEADS, w_b)
    odd_sizes = (N_HEADS * DK_C, N_HEADS * DK_C, w_c, N_HEADS, N_HEADS, w_c, w_a, w_a, w_a)
    even_groups = ((0,), (1,), (2,), (3,), (6,), (9,), (4, 5, 7, 8))
    wi_col, a_col, b_col = D_IDX, D_IDX + N_IDX_HEADS, D_IDX + N_IDX_HEADS + N_HEADS
    odd_groups = ((0, 1), (2,), (5,), (6,), (7,), (8,), (3, 4))
    i_col, f_col = 0, N_HEADS

    assert sp % CHUNK == 0 and ts <= CHUNK and past % CHUNK == 0 and past >= T5_FAR
    assert (past + ts - 1) // CHUNK == past // CHUNK
    topk_p = min(TOPK_MAX, sp // 4)
    topk_s = min(TOPK_MAX, (past + ts) // 4)
    n_p, n_s = bp * sp, bs * ts
    tm_p = _row_tile(n_p, 512)
    tm_s = _row_tile(n_s, 512)
    tff_p = _row_tile(sp, 512)
    ns_s = _row_tile(bs, max(1, 256 // ts))

    t5 = lambda rel: t5_bias[_t5_bucket(jnp.asarray(rel, jnp.int32))]
    fbias = t5(np.array([-T5_FAR - 1]))
    tq_a = 2 * CHUNK
    nw_a = tq_a + T5_FAR
    assert sp % nw_a == 0
    corr_p = jnp.stack([_toeplitz_bias(lambda dd: t5(-dd - T5_FAR * v) - fbias, nw_a, tq_a) for v in range(2)])
    ln_s = T5_FAR + ts
    nbias_s = _toeplitz_bias(lambda dd: t5(dd - T5_FAR), ts, ln_s)[None]

    lw = BAND_CHUNKS * CHUNK + tq_a
    r_chunk = np.arange(lw)[:, None] // CHUNK
    q_chunk = BAND_CHUNKS + np.arange(tq_a)[None, :] // CHUNK
    band_ok = (r_chunk >= q_chunk - BAND_CHUNKS) & (r_chunk <= q_chunk)
    pos_q = past + np.arange(ts)
    pos_kc = past - d_win + np.arange(d_win)
    def band_valid(pos_k):
        kch, qch = pos_k // CHUNK, pos_q // CHUNK
        return (pos_k[None] >= 0) & (kch[None] >= qch[:, None] - BAND_CHUNKS) & (kch[None] <= qch[:, None])
    def band_bias(table, shift, n_rows, n_cols, valid=None):
        bias = _toeplitz_bias(lambda dd: table[np.clip(dd + shift, -REL_CLIP, REL_CLIP) + REL_CLIP], n_rows, n_cols)
        return bias if valid is None else jnp.where(jnp.asarray(valid)[None], bias, NEG)

    cak_t = jnp.transpose(cache_a_k, (0, 1, 3, 4, 2))
    cav_t = jnp.transpose(cache_a_v, (0, 1, 3, 4, 2))
    caki_t = jnp.transpose(cache_a_kidx, (0, 1, 3, 2))
    cdk_t = jnp.transpose(cache_d_k, (0, 1, 3, 4, 2))
    cdv_t = jnp.transpose(cache_d_v, (0, 1, 3, 4, 2))

    xp = x_prompt.reshape(n_p, d)
    xs = x_sample.reshape(n_s, d)
    outs = {k: [] for k in ("ak_p", "ak_s", "av_p", "av_s", "aki_p", "aki_s", "bs_p", "bs_s", "bc_p", "bc_s",
                            "cc_p", "cc_s", "cn_p", "cn_s", "cm_p", "cm_s", "dk_p", "dk_s", "dv_p", "dv_s",
                            "fc_p", "fc_s")}
    for layer in range(depth):
        if layer % 2 == 0:
            e = layer // 2
            w_in, widths = _pack_cols(w_in_even[e], even_sizes, even_groups)
            w_out = w_out_even[e].astype(BF16)
            qa, ka, va, qi, qkv_b, z_b, misc = _proj(xp, w_in, widths, tm_p)
            o_a = _dsa_blocks(qa, qi, misc, ka.reshape(bp, sp, w_a), va.reshape(bp, sp, w_a),
                              misc.reshape(bp, sp, LANES), corr_p, nb=bp, tq=tq_a, topk=topk_p, wi_col=wi_col)
            y_b, s_b, h_b = _gdn(qkv_b, misc, z_b, jnp.zeros((bp, CONV_B - 1, qkv_b_w), F32),
                                 jnp.zeros((bp, N_HEADS, DK_B, DV_B), F32), b_conv_w[e], b_a_log[e], b_dt_bias[e],
                                 b_norm_w[e], nb=bp, c=CHUNK, a_col=a_col, b_col=b_col)
            xp = _mm_res_ln([o_a, y_b], [w_out[:w_a], w_out[w_a:]], xp, ln_mix_g[layer], ln_mix_b[layer], alpha, tm_p)
            outs["ak_p"].append(ka.reshape(bp, sp, N_HEADS, HEAD_DIM))
            outs["av_p"].append(va.reshape(bp, sp, N_HEADS, HEAD_DIM))
            outs["aki_p"].append(misc[:, :D_IDX].reshape(bp, sp, D_IDX))
            outs["bs_p"].append(s_b)
            outs["bc_p"].append(h_b)
            qa, ka, va, qi, qkv_b, z_b, misc = _proj(xs, w_in, widths, tm_s)
            ki = misc[:, :D_IDX]
            heads_t = lambda u: jnp.transpose(u.reshape(bs, ts, N_HEADS, HEAD_DIM), (0, 2, 3, 1))
            near = (jnp.concatenate([cak_t[e, ..., past - T5_FAR:], heads_t(ka)], axis=-1),
                    jnp.concatenate([cav_t[e, ..., past - T5_FAR:], heads_t(va)], axis=-1),
                    jnp.concatenate([caki_t[e, ..., past - T5_FAR:],
                                     jnp.transpose(ki.reshape(bs, ts, D_IDX), (0, 2, 1))], axis=-1))
            o_a = _dsa_sample(qa, qi, misc, cak_t, cav_t, caki_t, e, near, nbias_s, fbias, nb=bs, tq=ts,
                              topk=topk_s, wi_col=wi_col)
            y_b, s_b, h_b = _gdn(qkv_b, misc, z_b, state_b_conv[e], state_b_s[e], b_conv_w[e], b_a_log[e],
                                 b_dt_bias[e], b_norm_w[e], nb=bs, c=ts, a_col=a_col, b_col=b_col)
            xs = _mm_res_ln([o_a, y_b], [w_out[:w_a], w_out[w_a:]], xs, ln_mix_g[layer], ln_mix_b[layer], alpha, tm_s)
            outs["ak_s"].append(ka.reshape(bs, ts, N_HEADS, HEAD_DIM))
            outs["av_s"].append(va.reshape(bs, ts, N_HEADS, HEAD_DIM))
            outs["aki_s"].append(ki.reshape(bs, ts, D_IDX))
            outs["bs_s"].append(s_b)
            outs["bc_s"].append(h_b)
        else:
            o = layer // 2
            w_in, widths = _pack_cols(w_in_odd[o], odd_sizes, odd_groups)
            w_out = w_out_odd[o].astype(BF16)
            qk_c, v_c, o_c, q_d, k_d, v_d, misc = _proj(xp, w_in, widths, tm_p)
            y_c, c_c, c_n, c_m = _mlstm_dense(qk_c, v_c, o_c, misc,jnp.zeros((bp, N_HEADS, DK_C, DV_C), F32),
                                        jnp.zeros((bp, N_HEADS, DK_C), F32), jnp.zeros((bp, N_HEADS), F32),
                                        c_i_bias[o], c_f_bias[o], c_norm_w[o], nb=bp, l=CHUNK, i_col=i_col, f_col=f_col)
            k3 = k_d.reshape(bp, sp, w_a)
            v3 = v_d.reshape(bp, sp, w_a)
            bias_bp = _toeplitz_bias(
                lambda dd: d_rel_bias[o][np.clip(-dd - BAND_CHUNKS * CHUNK, -REL_CLIP, REL_CLIP) + REL_CLIP], lw, tq_a)
            bias_bp = jnp.where(jnp.asarray(band_ok)[None], bias_bp, NEG)
            o_d = _band_prompt(q_d, k3, v3, bias_bp, nb=bp, tq=tq_a)
            xp = _mm_res_ln([y_c, o_d], [w_out[:w_c], w_out[w_c:]], xp, ln_mix_g[layer], ln_mix_b[layer], alpha, tm_p)
            d_win_p = min(BAND_CHUNKS * CHUNK, sp)
            outs["cc_p"].append(c_c)
            outs["cn_p"].append(c_n)
            outs["cm_p"].append(c_m)
            outs["dk_p"].append(k3[:, sp - d_win_p:].reshape(bp, d_win_p, N_HEADS, HEAD_DIM))
            outs["dv_p"].append(v3[:, sp - d_win_p:].reshape(bp, d_win_p, N_HEADS, HEAD_DIM))
            qk_c, v_c, o_c, q_d, k_d, v_d, misc = _proj(xs, w_in, widths, tm_s)
            y_c, c_c, c_n, c_m = _mlstm_dense(qk_c, v_c, o_c, misc,state_c_c[o], state_c_n[o], state_c_m[o],
                                        c_i_bias[o], c_f_bias[o], c_norm_w[o], nb=bs, l=ts, i_col=i_col, f_col=f_col)
            heads_t = lambda u: jnp.transpose(u.reshape(bs, ts, N_HEADS, HEAD_DIM), (0, 2, 3, 1))
            o_d = _band_sample(q_d, cdk_t, cdv_t, o, heads_t(k_d), heads_t(v_d),
                               band_bias(d_rel_bias[o], -d_win, ts, d_win, band_valid(pos_kc)),
                               band_bias(d_rel_bias[o], 0, ts, ts, band_valid(pos_q)), nb=bs, tq=ts)
            xs = _mm_res_ln([y_c, o_d], [w_out[:w_c], w_out[w_c:]], xs, ln_mix_g[layer], ln_mix_b[layer], alpha, tm_s)
            outs["cc_s"].append(c_c)
            outs["cn_s"].append(c_n)
            outs["cm_s"].append(c_m)
            outs["dk_s"].append(k_d.reshape(bs, ts, N_HEADS, HEAD_DIM))
            outs["dv_s"].append(v_d.reshape(bs, ts, N_HEADS, HEAD_DIM))
        w_up = ffn_w_up[layer].astype(BF16)
        w_down = ffn_w_down[layer].astype(BF16)
        act, hist_p = _ffn_up(xp, w_up, ffn_conv_w[layer], jnp.zeros((bp, CONV_FF - 1, 2 * dff), F32),
                              1, tff_p, sp // tff_p)
        xp = _mm_res_ln([act], [w_down], xp, ln_ffn_g[layer], ln_ffn_b[layer], alpha, tm_p)
        act, hist_s = _ffn_up(xs, w_up, ffn_conv_w[layer], state_ffn_conv[layer], ns_s, ts, 1)
        xs = _mm_res_ln([act], [w_down], xs, ln_ffn_g[layer], ln_ffn_b[layer], alpha, tm_s)
        outs["fc_p"].append(hist_p)
        outs["fc_s"].append(hist_s)

    st = lambda k: jnp.stack(outs[k])
    return (xp.reshape(bp, sp, d), xs.reshape(bs, ts, d),
            st("ak_p"), st("ak_s"), st("av_p"), st("av_s"), st("aki_p"), st("aki_s"),
            st("bs_p"), st("bs_s"), st("bc_p"), st("bc_s"),
            st("cc_p"), st("cc_s"), st("cn_p"), st("cn_s"), st("cm_p"), st("cm_s"),
            st("dk_p"), st("dk_s"), st("dv_p"), st("dv_s"),
            st("fc_p"), st("fc_s"))
```

```python
import functools
import math

import numpy as np
import jax
import jax.numpy as jnp
from jax import lax
from jax.experimental import pallas as pl
from jax.experimental.pallas import tpu as pltpu

F32 = jnp.float32
BF16 = jnp.bfloat16
HI = lax.Precision.HIGHEST

CHUNK = 64
HEAD_DIM = 64
N_HEADS = 8
N_IDX_HEADS = 8
D_IDX = 64
TOPK_MAX = 256
T5_BUCKETS = 32
T5_MAX_DIST = 128
DK_B = 64
DV_B = 64
CONV_B = 4
DK_C = 32
DV_C = 64
BAND_CHUNKS = 8
REL_CLIP = 128
CONV_FF = 3
NEAR_CHUNKS = 3
T5_FAR = 128
INV_BLOCK = 8
GDN_SEQS_PER_STEP = 2
SEQS_PER_STEP = 4

LANES = 128
SUBLANES = 8
VMEM_LIMIT = 56 * 1024 * 1024

NEG = -1e30
INT_MIN = -2 ** 31


def _params(*sem):
    return pltpu.CompilerParams(dimension_semantics=sem, vmem_limit_bytes=VMEM_LIMIT)


def _dot(a, b):
    return jnp.dot(a.astype(BF16), b.astype(BF16), preferred_element_type=F32)


def _dot_nt(a, b, precision=None):
    return lax.dot_general(a, b, (((1,), (1,)), ((), ())), precision=precision, preferred_element_type=F32)


def _dot_tn(a, b, precision=None):
    return lax.dot_general(a, b, (((0,), (0,)), ((), ())), precision=precision, preferred_element_type=F32)


def _dot_hi(a, b):
    return jnp.dot(a, b, precision=HI, preferred_element_type=F32)


def _split(a):
    hi = a.astype(BF16)
    return hi, (a - hi.astype(F32)).astype(BF16)


def _split3(a):
    p1 = a.astype(BF16)
    r1 = a - p1.astype(F32)
    p2 = r1.astype(BF16)
    return p1, p2, (r1 - p2.astype(F32)).astype(BF16)


def _dot3(a, b):
    a_hi, a_lo = a
    b_hi, b_lo = b
    d = functools.partial(jnp.dot, preferred_element_type=F32)
    return d(a_hi, b_hi) + (d(a_hi, b_lo) + d(a_lo, b_hi))


def _sigmoid(x):
    return 1.0 / (1.0 + jnp.exp(-x))


def _softplus(x):
    return jnp.maximum(x, 0.0) + jnp.log(1.0 + jnp.exp(-jnp.abs(x)))


def _iota(shape, dim):
    return lax.broadcasted_iota(jnp.int32, shape, dim)


def _proj_kernel(x_ref, w_ref, *out_refs, sizes):
    xb = x_ref[...].astype(BF16)
    off = 0
    for o_ref, size in zip(out_refs, sizes):
        o_ref[...] = jnp.dot(xb, w_ref[:, off:off + size], preferred_element_type=F32)
        off += size


def _proj(x2d, w, sizes, tm):
    n, d = x2d.shape
    return pl.pallas_call(
        functools.partial(_proj_kernel, sizes=sizes),
        grid=(n // tm,),
        in_specs=[pl.BlockSpec((tm, d), lambda i: (i, 0)),
                  pl.BlockSpec((d, sum(sizes)), lambda i: (0, 0))],
        out_specs=[pl.BlockSpec((tm, s), lambda i: (i, 0)) for s in sizes],
        out_shape=[jax.ShapeDtypeStruct((n, s), F32) for s in sizes],
        compiler_params=_params("parallel"),
        name="in_proj",
    )(x2d, w)


def _mm_res_ln_kernel(*refs, nparts, alpha):
    part_refs = refs[:nparts]
    w_refs = refs[nparts:2 * nparts]
    x_ref, g_ref, b_ref, o_ref = refs[2 * nparts:]
    acc = alpha * x_ref[...]
    for p_ref, w_ref in zip(part_refs, w_refs):
        acc = acc + jnp.dot(p_ref[...].astype(BF16), w_ref[...], preferred_element_type=F32)
    mu = jnp.mean(acc, axis=-1, keepdims=True)
    cen = acc - mu
    var = jnp.mean(cen * cen, axis=-1, keepdims=True)
    o_ref[...] = cen * lax.rsqrt(var + 1e-5) * g_ref[...] + b_ref[...]


def _mm_res_ln(parts, ws, x2d, g, b, alpha, tm):
    n, d = x2d.shape
    nparts = len(parts)
    in_specs = ([pl.BlockSpec((tm, p.shape[1]), lambda i: (i, 0)) for p in parts]
                + [pl.BlockSpec(w.shape, lambda i: (0, 0)) for w in ws]
                + [pl.BlockSpec((tm, d), lambda i: (i, 0)),
                   pl.BlockSpec((1, d), lambda i: (0, 0)),
                   pl.BlockSpec((1, d), lambda i: (0, 0))])
    return pl.pallas_call(
        functools.partial(_mm_res_ln_kernel, nparts=nparts, alpha=alpha),
        grid=(n // tm,),
        in_specs=in_specs,
        out_specs=pl.BlockSpec((tm, d), lambda i: (i, 0)),
        out_shape=jax.ShapeDtypeStruct((n, d), F32),
        compiler_params=_params("parallel"),
        name="out_proj_ln",
    )(*parts, *ws, x2d, g.reshape(1, d), b.reshape(1, d))


def _ffn_up_kernel(x_ref, w_ref, cw_ref, hist_ref, act_ref, newhist_ref, ext_ref, *,
                   ns, tt, tiles_per_seq, dff, cc):
    i = pl.program_id(0)
    tm = ns * tt
    hw = CONV_FF - 1
    base = SUBLANES
    if tiles_per_seq == 1:
        ext_ref[:, base - hw:base, :] = hist_ref[...]
    else:
        @pl.when(i % tiles_per_seq == 0)
        def _():
            ext_ref[:, base - hw:base, :] = hist_ref[...]

        @pl.when(i % tiles_per_seq != 0)
        def _():
            ext_ref[:, base - hw:base, :] = ext_ref[:, base + tt - hw:base + tt, :]
    xb = x_ref[...].astype(BF16)
    for j in range(2 * dff // cc):
        cols = slice(j * cc, (j + 1) * cc)
        h = jnp.dot(xb, w_ref[:, cols], preferred_element_type=F32)
        ext_ref[:, base:base + tt, cols] = h.reshape(ns, tt, cc)
    newhist_ref[...] = ext_ref[:, base + tt - hw:base + tt, :]

    def conv(cols):
        acc = None
        for k in range(CONV_FF):
            term = ext_ref[:, base - hw + k:base - hw + k + tt, cols] * cw_ref[k:k + 1, cols]
            acc = term if acc is None else acc + term
        return acc

    for j in range(dff // cc):
        g = conv(slice(j * cc, (j + 1) * cc))
        u = conv(slice(dff + j * cc, dff + (j + 1) * cc))
        act = g * _sigmoid(g) * u
        act_ref[:, j * cc:(j + 1) * cc] = act.reshape(tm, cc).astype(BF16)


def _ffn_up(x2d, w_up, conv_w, hist, ns, tt, tiles_per_seq):
    n, d = x2d.shape
    c2 = w_up.shape[1]
    dff = c2 // 2
    tm = ns * tt
    cc = 256
    hw = CONV_FF - 1
    if tiles_per_seq == 1:
        hist_map = lambda i: (i, 0, 0)
    else:
        hist_map = lambda i: (i // tiles_per_seq, 0, 0)
    return pl.pallas_call(
        functools.partial(_ffn_up_kernel, ns=ns, tt=tt, tiles_per_seq=tiles_per_seq, dff=dff, cc=cc),
        grid=(n // tm,),
        in_specs=[pl.BlockSpec((tm, d), lambda i: (i, 0)),
                  pl.BlockSpec((d, c2), lambda i: (0, 0)),
                  pl.BlockSpec((CONV_FF, c2), lambda i: (0, 0)),
                  pl.BlockSpec((ns, hw, c2), hist_map)],
        out_specs=[pl.BlockSpec((tm, dff), lambda i: (i, 0)),
                   pl.BlockSpec((ns, hw, c2), hist_map)],
        out_shape=[jax.ShapeDtypeStruct((n, dff), BF16),
                   jax.ShapeDtypeStruct(hist.shape, F32)],
        scratch_shapes=[pltpu.VMEM((ns, SUBLANES + tt, c2), F32)],
        compiler_params=_params("arbitrary"),
        name="ffn_up_conv_gate",
    )(x2d, w_up, conv_w, hist)


def _sortable(x):
    b = lax.bitcast_convert_type(x, jnp.int32)
    return b ^ ((b >> 31) & jnp.int32(0x7FFFFFFF))


def _count(mask):
    return jnp.sum(jnp.where(mask, 1.0, 0.0), axis=-1, keepdims=True)


def _dsa_sample_kernel(qa_ref, qi_ref, qm_ref, kf_ref, vf_ref, kif_ref, kn_ref, vn_ref, kin_ref, nbias_ref,
                       fbias_ref, o_ref, self_ref, seln_ref, *, topk, tq, lf, ln, wi_col):
    v = 0
    start = lf - T5_FAR
    qa = qa_ref[...]
    qi = qi_ref[...]
    wi = qm_ref[:, wi_col:wi_col + N_IDX_HEADS] * (N_IDX_HEADS ** -0.5) * (D_IDX ** -0.5)
    kif = kif_ref[0, 0].astype(BF16)
    kinb = kin_ref[0].astype(BF16)

    sc_f = jnp.zeros((tq, lf), F32)
    sc_n = jnp.zeros((tq, ln), F32)
    for n in range(N_IDX_HEADS):
        qn = qi[:, n * D_IDX:(n + 1) * D_IDX].astype(BF16)
        wn = wi[:, n:n + 1]
        sc_f = sc_f + jnp.maximum(jnp.dot(qn, kif, preferred_element_type=F32), 0.0) * wn
        sc_n = sc_n + jnp.maximum(jnp.dot(qn, kinb, preferred_element_type=F32), 0.0) * wn
    adm_f = _iota((tq, lf), 1) < start
    adm_n = nbias_ref[v, 0] > 0.5 * NEG
    key_f = jnp.where(adm_f, _sortable(sc_f), jnp.int32(INT_MIN))
    key_n = jnp.where(adm_n, _sortable(sc_n), jnp.int32(INT_MIN))

    kf32 = float(topk)

    def body(i, t_u):
        cand_u = t_u | lax.shift_left(jnp.int32(1), 31 - i)
        cand_s = cand_u ^ jnp.int32(INT_MIN)
        cnt = _count(key_f >= cand_s) + _count(key_n >= cand_s)
        return jnp.where(cnt >= kf32, cand_u, t_u)

    t_u = lax.fori_loop(0, 32, body, jnp.zeros((tq, 1), jnp.int32))
    thr = t_u ^ jnp.int32(INT_MIN)

    n_gt = _count(key_f > thr) + _count(key_n > thr)
    n_eq = _count(key_f == thr) + _count(key_n == thr)
    need = kf32 - n_gt
    open_row = thr == jnp.int32(INT_MIN)
    conflict = jnp.logical_and(n_eq != need, jnp.logical_not(open_row))
    self_ref[...] = jnp.where(jnp.logical_and(key_f >= thr, adm_f), 0.0, NEG)
    seln_ref[...] = jnp.where(jnp.logical_and(key_n >= thr, adm_n), 0.0, NEG)

    @pl.when(jnp.max(jnp.where(conflict, 1.0, 0.0)) > 0.0)
    def _():
        upper = jnp.where(_iota((LANES, LANES), 0) < _iota((LANES, LANES), 1), 1.0, 0.0).astype(BF16)
        offset = jnp.zeros((tq, 1), F32)
        for ref, key, width in ((self_ref, key_f, lf), (seln_ref, key_n, ln)):
            for j0 in range(0, width, LANES):
                w = min(LANES, width - j0)
                kb = key[:, j0:j0 + w]
                e = jnp.where(kb == thr, 1.0, 0.0)
                rank = offset + jnp.dot(e.astype(BF16), upper[:w, :w], preferred_element_type=F32)
                take = jnp.where(kb > thr, 1.0, jnp.where(rank < need, e, 0.0))
                take = jnp.where(open_row, jnp.where(kb > thr, 1.0, 0.0), take)
                ref[:, j0:j0 + w] = jnp.where(take > 0.5, 0.0, NEG)
                offset = offset + jnp.sum(e, axis=-1, keepdims=True)

    sel_f = self_ref[...]
    sel_n = seln_ref[...]
    for h in range(N_HEADS):
        hs = slice(h * HEAD_DIM, (h + 1) * HEAD_DIM)
        qh = qa[:, hs].astype(BF16)
        s_f = (jnp.dot(qh, kf_ref[0, 0, h].astype(BF16), preferred_element_type=F32) * (HEAD_DIM ** -0.5)
               + fbias_ref[:, h:h + 1] + sel_f)
        s_n = (jnp.dot(qh, kn_ref[0, h].astype(BF16), preferred_element_type=F32) * (HEAD_DIM ** -0.5)
               + nbias_ref[v, h] + sel_n)
        m = jnp.maximum(jnp.max(s_f, axis=-1, keepdims=True), jnp.max(s_n, axis=-1, keepdims=True))
        p_f = jnp.exp(s_f - m)
        p_n = jnp.exp(s_n - m)
        den = jnp.sum(p_f, axis=-1, keepdims=True) + jnp.sum(p_n, axis=-1, keepdims=True)
        o = (_dot_nt(p_f.astype(BF16), vf_ref[0, 0, h].astype(BF16))
             + _dot_nt(p_n.astype(BF16), vn_ref[0, h].astype(BF16)))
        o_ref[:, hs] = o / den


def _dsa_sample(qa, qi, qmisc, kf, vf, kif, layer, near, nbias, fbias, *, nb, tq, topk, wi_col):
    lf = kf.shape[-1]
    ln = nbias.shape[-1]
    row = lambda b: (b, 0)
    args = [qa, qi, qmisc, kf, vf, kif, *near]
    in_specs = ([pl.BlockSpec((tq, a.shape[1]), row) for a in args[:3]]
                + [pl.BlockSpec((1, 1) + a.shape[2:], lambda b, nd=a.ndim: (layer, b) + (0,) * (nd - 2))
                   for a in args[3:6]]
                + [pl.BlockSpec((1,) + a.shape[1:], lambda b, nd=a.ndim: (b,) + (0,) * (nd - 1)) for a in args[6:]]
                + [pl.BlockSpec(nbias.shape, lambda b: (0, 0, 0, 0)),
                   pl.BlockSpec(fbias.shape, lambda b: (0, 0))])
    return pl.pallas_call(
        functools.partial(_dsa_sample_kernel, topk=topk, tq=tq, lf=lf, ln=ln, wi_col=wi_col),
        grid=(nb,),
        in_specs=in_specs,
        out_specs=pl.BlockSpec((tq, qa.shape[1]), row),
        out_shape=jax.ShapeDtypeStruct(qa.shape, F32),
        scratch_shapes=[pltpu.VMEM((tq, lf), F32), pltpu.VMEM((tq, ln), F32)],
        compiler_params=_params("parallel"),
        name="dsa_attention_sample",
    )(*args, nbias, fbias)


def _reduce_rows(x, op, final):
    blk = 8 * SUBLANES
    parts = [x[r0:r0 + blk] for r0 in range(0, x.shape[0], blk)]
    while len(parts) > 1:
        parts = [op(parts[i], parts[i + 1]) for i in range(0, len(parts) - 1, 2)] + parts[len(parts) & ~1:]
    return final(parts[0], axis=0, keepdims=True)


def _dsa_prompt_tile(g, lk, qa_ref, qi_ref, qm_ref, corr_ref, o_ref,
                     kb_ref, vt_ref, kib_ref, key_ref, sel_ref, s_ref, ot_ref, *, topk, tq, wi_col):
    rb = 2 * LANES if lk % (2 * LANES) == 0 else LANES
    nw = corr_ref.shape[2]
    qa = (qa_ref[...] * (HEAD_DIM ** -0.5)).astype(BF16)
    qi = qi_ref[...].astype(BF16)
    wi = qm_ref[:, wi_col:wi_col + N_IDX_HEADS] * (N_IDX_HEADS ** -0.5) * (D_IDX ** -0.5)
    eye_h = jnp.where(_iota((N_IDX_HEADS, N_IDX_HEADS), 0) == _iota((N_IDX_HEADS, N_IDX_HEADS), 1), 1.0, 0.0)
    wi_t = _dot_nt(eye_h, wi, HI)
    q_chunk = (g * tq + _iota((1, tq), 1)) // CHUNK

    for r0 in range(0, lk, rb):
        kib = kib_ref[r0:r0 + rb, :D_IDX]
        acc = jnp.zeros((rb, tq), F32)
        for n in range(N_IDX_HEADS):
            acc = acc + jnp.maximum(_dot_nt(kib, qi[:, n * D_IDX:(n + 1) * D_IDX]), 0.0) * wi_t[n:n + 1, :]
        adm = (r0 + _iota((rb, 1), 0)) // CHUNK <= q_chunk
        key_ref[r0:r0 + rb, :] = jnp.where(adm, _sortable(acc), jnp.int32(INT_MIN))

    kf32 = float(topk)
    count = lambda mask: _reduce_rows(jnp.where(mask, 1.0, 0.0), jnp.add, jnp.sum)

    def body(i, t_u):
        cand_u = t_u | lax.shift_left(jnp.int32(1), 31 - i)
        cand_s = cand_u ^ jnp.int32(INT_MIN)
        return jnp.where(count(key_ref[0:lk, :] >= cand_s) >= kf32, cand_u, t_u)

    t_u = lax.fori_loop(0, 32, body, jnp.zeros((1, tq), jnp.int32))
    thr = t_u ^ jnp.int32(INT_MIN)
    keys = key_ref[0:lk, :]
    need = kf32 - count(keys > thr)
    open_row = thr == jnp.int32(INT_MIN)
    conflict = jnp.logical_and(count(keys == thr) != need, jnp.logical_not(open_row))
    sel_ref[0:lk, :] = jnp.where(keys >= jnp.maximum(thr, jnp.int32(INT_MIN + 1)), 0.0, NEG)

    @pl.when(jnp.max(jnp.where(conflict, 1.0, 0.0)) > 0.0)
    def _():
        below = jnp.where(_iota((LANES, LANES), 1) < _iota((LANES, LANES), 0), 1.0, 0.0).astype(BF16)
        offset = jnp.zeros((1, tq), F32)
        for r0 in range(0, lk, LANES):
            kblk = key_ref[r0:r0 + LANES, :]
            e = jnp.where(kblk == thr, 1.0, 0.0)
            rank = offset + jnp.dot(below, e.astype(BF16), preferred_element_type=F32)
            take = jnp.where(kblk > thr, 1.0, jnp.where(jnp.logical_or(rank >= need, open_row), 0.0, e))
            sel_ref[r0:r0 + LANES, :] = jnp.where(take > 0.5, 0.0, NEG)
            offset = offset + jnp.sum(e, axis=0, keepdims=True)

    v = jnp.minimum(g, 1)
    start = pl.multiple_of(jnp.maximum(g * tq - T5_FAR, 0), LANES)
    for h in range(N_HEADS):
        hs = slice(h * HEAD_DIM, (h + 1) * HEAD_DIM)
        s_ref[0:lk, :] = _dot_nt(kb_ref[0:lk, hs], qa[:, hs]) + sel_ref[0:lk, :]
        s_ref[pl.ds(start, nw), :] += corr_ref[v, h]
        s = s_ref[0:lk, :]
        p = jnp.exp(s - _reduce_rows(s, jnp.maximum, jnp.max))
        pn = (p * (1.0 / _reduce_rows(p, jnp.add, jnp.sum))).astype(BF16)
        ot_ref[hs, :] = jnp.dot(vt_ref[hs, 0:lk], pn, preferred_element_type=F32)
    o_ref[...] = ot_ref[...].T


def _dsa_prompt_kernel(qa_ref, qi_ref, qm_ref, k_ref, v_ref, ki_ref, corr_ref, o_ref,
                       kb_ref, vt_ref, kib_ref, key_ref, sel_ref, s_ref, ot_ref, *, topk, tq, seq, n_groups, wi_col):
    g = pl.program_id(1)
    blk = _row_tile(seq, 4 * LANES)

    @pl.when(g == 0)
    def _():
        for r0 in range(0, seq, blk):
            kb_ref[r0:r0 + blk, :] = k_ref[0, r0:r0 + blk, :].astype(BF16)
            vt_ref[:, r0:r0 + blk] = v_ref[0, r0:r0 + blk, :].T.astype(BF16)
            kib_ref[r0:r0 + blk, :] = ki_ref[0, r0:r0 + blk, :].astype(BF16)

    per_group = (seq // tq) // n_groups
    for grp in range(n_groups):
        @pl.when(g // per_group == grp)
        def _():
            _dsa_prompt_tile(g, (grp + 1) * per_group * tq, qa_ref, qi_ref, qm_ref, corr_ref, o_ref,
                             kb_ref, vt_ref, kib_ref, key_ref, sel_ref, s_ref, ot_ref,
                             topk=topk, tq=tq, wi_col=wi_col)


def _dsa_prompt(qa, qi, qmisc, k, v, ki, corr, *, nb, tq, topk, wi_col, n_groups):
    n, w = qa.shape
    seq = k.shape[1]
    nq = seq // tq
    row = lambda b, g: (b * nq + g, 0)
    per_b = lambda b, g: (b, 0, 0)
    return pl.pallas_call(
        functools.partial(_dsa_prompt_kernel, topk=topk, tq=tq, seq=seq, n_groups=n_groups, wi_col=wi_col),
        grid=(nb, nq),
        in_specs=[pl.BlockSpec((tq, w), row),
                  pl.BlockSpec((tq, qi.shape[1]), row),
                  pl.BlockSpec((tq, qmisc.shape[1]), row),
                  pl.BlockSpec((1, seq, w), per_b),
                  pl.BlockSpec((1, seq, w), per_b),
                  pl.BlockSpec((1, seq, ki.shape[2]), per_b),
                  pl.BlockSpec(corr.shape, lambda b, g: (0, 0, 0, 0))],
        out_specs=pl.BlockSpec((tq, w), row),
        out_shape=jax.ShapeDtypeStruct(qa.shape, F32),
        scratch_shapes=[pltpu.VMEM((seq, w), BF16),
                        pltpu.VMEM((w, seq), BF16),
                        pltpu.VMEM((seq, ki.shape[2]), BF16),
                        pltpu.VMEM((seq, tq), jnp.int32),
                        pltpu.VMEM((seq, tq), F32),
                        pltpu.VMEM((seq, tq), F32),
                        pltpu.VMEM((w, tq), F32)],
        compiler_params=_params("parallel", "arbitrary"),
        name="dsa_attention_prompt",
    )(qa, qi, qmisc, k, v, ki, corr)


def _block_rows(x, op):
    blk = 8 * SUBLANES
    parts = [x[r0:r0 + blk] for r0 in range(0, x.shape[0], blk)]
    while len(parts) > 1:
        parts = [op(parts[i], parts[i + 1]) for i in range(0, len(parts) - 1, 2)] + parts[len(parts) & ~1:]
    return parts[0]


def _dsa_blocks_kernel(qa_ref, qi_ref, qm_ref, k_ref, v_ref, ki_ref, corr_ref, o_ref,
                       kb_ref, vt_ref, kib_ref, key_ref, sel_ref, thr_ref, stat_ref, m_ref, l_ref, ot_ref, *,
                       topk, tq, seq, wi_col):
    g = pl.program_id(1)
    kb = 2 * LANES
    w = qa_ref.shape[1]
    blk = _row_tile(seq, 4 * LANES)

    @pl.when(g == 0)
    def _():
        for r0 in range(0, seq, blk):
            kb_ref[r0:r0 + blk, :] = k_ref[0, r0:r0 + blk, :].astype(BF16)
            kib_ref[r0:r0 + blk, :] = ki_ref[0, r0:r0 + blk, :].astype(BF16)
        for j in range(seq // LANES):
            vt_ref[j] = v_ref[0, j * LANES:(j + 1) * LANES, :].T.astype(BF16)

    top = (g + 1) * tq
    nkb = (top + kb - 1) // kb
    start_of = lambda i: pl.multiple_of(jnp.maximum(top - kb * (i + 1), 0), LANES)

    qa = (qa_ref[...] * (HEAD_DIM ** -0.5)).astype(BF16)
    qi = qi_ref[...].astype(BF16)
    wi = qm_ref[:, wi_col:wi_col + N_IDX_HEADS] * (N_IDX_HEADS ** -0.5) * (D_IDX ** -0.5)
    eye_h = jnp.where(_iota((N_IDX_HEADS, N_IDX_HEADS), 0) == _iota((N_IDX_HEADS, N_IDX_HEADS), 1), 1.0, 0.0)
    wi_t = _dot_nt(eye_h, wi, HI)
    q_chunk = (g * tq + _iota((1, tq), 1)) // CHUNK

    def score_block(i, carry):
        st = start_of(i)
        kib = kib_ref[pl.ds(st, kb), :][:, :D_IDX]
        acc = jnp.zeros((kb, tq), F32)
        for n in range(N_IDX_HEADS):
            acc = acc + jnp.maximum(_dot_nt(kib, qi[:, n * D_IDX:(n + 1) * D_IDX]), 0.0) * wi_t[n:n + 1, :]
        row = st + _iota((kb, 1), 0)
        ok = jnp.logical_and(row // CHUNK <= q_chunk, row < top - kb * i)
        key_ref[i] = jnp.where(ok, _sortable(acc), jnp.int32(INT_MIN))
        return carry

    lax.fori_loop(0, nkb, score_block, 0)

    kf32 = float(topk)

    def search_blocks(n_blocks):
        def count(pred):
            parts = [_block_rows(jnp.where(pred(key_ref[i]), 1.0, 0.0), jnp.add) for i in range(n_blocks)]
            while len(parts) > 1:
                parts = [a + b for a, b in zip(parts[0::2], parts[1::2])] + parts[len(parts) & ~1:]
            return jnp.sum(parts[0], axis=0, keepdims=True)

        def search(b, t_u):
            cand_u = t_u | lax.shift_left(jnp.int32(1), 31 - b)
            cand_s = cand_u ^ jnp.int32(INT_MIN)
            return jnp.where(count(lambda x: x >= cand_s) >= kf32, cand_u, t_u)

        t_u = lax.fori_loop(0, 32, search, jnp.zeros((1, tq), jnp.int32))
        thr = t_u ^ jnp.int32(INT_MIN)
        thr_ref[0:1, :] = thr
        stat_ref[0:1, :] = kf32 - count(lambda x: x > thr)
        stat_ref[1:2, :] = count(lambda x: x == thr)

    for n_blocks in range(1, seq // kb + 1):
        pl.when(nkb == n_blocks)(functools.partial(search_blocks, n_blocks))

    thr = thr_ref[0:1, :]
    need = stat_ref[0:1, :]
    open_row = thr == jnp.int32(INT_MIN)
    conflict = jnp.logical_and(stat_ref[1:2, :] != need, jnp.logical_not(open_row))
    floor = jnp.maximum(thr, jnp.int32(INT_MIN + 1))

    def select_block(i, carry):
        sel_ref[i] = jnp.where(key_ref[i] >= floor, 0.0, NEG)
        return carry

    lax.fori_loop(0, nkb, select_block, 0)

    @pl.when(jnp.max(jnp.where(conflict, 1.0, 0.0)) > 0.0)
    def _():
        below = jnp.where(_iota((LANES, LANES), 1) < _iota((LANES, LANES), 0), 1.0, 0.0).astype(BF16)

        def tie_block(ii, offset):
            i = nkb - 1 - ii
            for r0 in range(0, kb, LANES):
                kblk = key_ref[i, r0:r0 + LANES, :]
                e = jnp.where(kblk == thr, 1.0, 0.0)
                rank = offset + jnp.dot(below, e.astype(BF16), preferred_element_type=F32)
                take = jnp.where(kblk > thr, 1.0, jnp.where(jnp.logical_or(rank >= need, open_row), 0.0, e))
                sel_ref[i, r0:r0 + LANES, :] = jnp.where(take > 0.5, 0.0, NEG)
                offset = offset + jnp.sum(e, axis=0, keepdims=True)
            return offset

        lax.fori_loop(0, nkb, tie_block, jnp.zeros((1, tq), F32))

    heads = range(N_HEADS)
    hs = [slice(h * HEAD_DIM, (h + 1) * HEAD_DIM) for h in heads]

    def attend_block(i, first, near=None):
        st = start_of(i)
        vblk = st // LANES
        sel = sel_ref[i]
        s = [_dot_nt(kb_ref[pl.ds(st, kb), hs[h]], qa[:, hs[h]]) + sel for h in heads]
        if near is not None:
            s = [s[h] + corr_ref[near, h] for h in heads]
        m_blk = [jnp.max(_block_rows(x, jnp.maximum), axis=0, keepdims=True) for x in s]
        m_old = [m_ref[h:h + 1, :] for h in heads]
        m_new = m_blk if first else [jnp.maximum(a, b) for a, b in zip(m_old, m_blk)]
        p = [jnp.exp(x - m) for x, m in zip(s, m_new)]
        l_blk = [jnp.sum(_block_rows(x, jnp.add), axis=0, keepdims=True) for x in p]
        pb = [x.astype(BF16) for x in p]
        pv = [sum(jnp.dot(vt_ref[vblk + j, hs[h], :], pb[h][j * LANES:(j + 1) * LANES, :],
                          preferred_element_type=F32) for j in range(kb // LANES)) for h in heads]
        for h in heads:
            if first:
                l_ref[h:h + 1, :] = l_blk[h]
                ot_ref[hs[h], :] = pv[h]
            else:
                alpha = jnp.exp(m_old[h] - m_new[h])
                l_ref[h:h + 1, :] = l_ref[h:h + 1, :] * alpha + l_blk[h]
                ot_ref[hs[h], :] = ot_ref[hs[h], :] * alpha + pv[h]
            m_ref[h:h + 1, :] = m_new[h]

    attend_block(0, True, 0)
    pl.when(nkb > 1)(lambda: attend_block(1, False, 1))

    def attend_rest(i, carry):
        attend_block(i, False)
        return carry

    lax.fori_loop(2, nkb, attend_rest, 0)
    for h in heads:
        ot_ref[hs[h], :] = ot_ref[hs[h], :] * (1.0 / l_ref[h:h + 1, :])
    o_ref[...] = ot_ref[...].T


def _dsa_blocks(qa, qi, qmisc, k, v, ki, corr, *, nb, tq, topk, wi_col):
    n, w = qa.shape
    seq = k.shape[1]
    nq = seq // tq
    kb = 2 * LANES
    row = lambda b, g: (b * nq + g, 0)
    per_b = lambda b, g: (b, 0, 0)
    return pl.pallas_call(
        functools.partial(_dsa_blocks_kernel, topk=topk, tq=tq, seq=seq, wi_col=wi_col),
        grid=(nb, nq),
        in_specs=[pl.BlockSpec((tq, w), row),
                  pl.BlockSpec((tq, qi.shape[1]), row),
                  pl.BlockSpec((tq, qmisc.shape[1]), row),
                  pl.BlockSpec((1, seq, w), per_b),
                  pl.BlockSpec((1, seq, w), per_b),
                  pl.BlockSpec((1, seq, ki.shape[2]), per_b),
                  pl.BlockSpec(corr.shape, lambda b, g: (0, 0, 0, 0))],
        out_specs=pl.BlockSpec((tq, w), row),
        out_shape=jax.ShapeDtypeStruct(qa.shape, F32),
        scratch_shapes=[pltpu.VMEM((seq, w), BF16),
                        pltpu.VMEM((seq // LANES, w, LANES), BF16),
                        pltpu.VMEM((seq, ki.shape[2]), BF16),
                        pltpu.VMEM((seq // kb, kb, tq), jnp.int32),
                        pltpu.VMEM((seq // kb, kb, tq), F32),
                        pltpu.VMEM((SUBLANES, tq), jnp.int32),
                        pltpu.VMEM((SUBLANES, tq), F32),
                        pltpu.VMEM((N_HEADS, tq), F32),
                        pltpu.VMEM((N_HEADS, tq), F32),
                        pltpu.VMEM((w, tq), F32)],
        compiler_params=_params("parallel", "arbitrary"),
        name="dsa_attention_prompt",
    )(qa, qi, qmisc, k, v, ki, corr)


def _unit_lower_inverse(mats, n):
    row, col = _iota((n, n), 0), _iota((n, n), 1)
    same = lambda width: row // width == col // width
    dot = functools.partial(jnp.dot, preferred_element_type=F32)
    eye = jnp.where(row == col, 1.0, 0.0)
    diag = [jnp.where(same(INV_BLOCK), a, 0.0) for a in mats]
    ts = [eye - d for d in diag]
    dks = [_split(d) for d in diag]
    k = 1
    while 2 * k < INV_BLOCK:
        dks = [_split(_dot3(dk, dk)) for dk in dks]
        ts = [t + _dot3(_split(t), dk) for t, dk in zip(ts, dks)]
        k *= 2
    width = INV_BLOCK
    while width < n:
        off = jnp.logical_and(same(2 * width), jnp.logical_not(same(width)))
        ls = [jnp.where(off, a, 0.0).astype(BF16) for a in mats]
        tb = [t.astype(BF16) for t in ts]
        tl = [dot(t, l).astype(BF16) for t, l in zip(tb, ls)]
        ts = [t - dot(x, y) for t, x, y in zip(ts, tl, tb)]
        width *= 2
    return ts


def _head_rms(x, w):
    return x * lax.rsqrt(jnp.mean(x * x, axis=-1, keepdims=True) + 1e-6) * w


def _gdn_kernel(qkv_ref, misc_ref, z_ref, hist_ref, s0_ref, cw_ref, alog_ref, dtb_ref, nw_ref,
                y_ref, sfin_ref, newhist_ref, ext_ref, s_ref, *, c, a_col, b_col):
    j = pl.program_id(1)
    hw = CONV_B - 1
    base = SUBLANES
    wq = N_HEADS * DK_B

    @pl.when(j == 0)
    def _():
        ext_ref[base - hw:base, :] = hist_ref[0]
        s_ref[...] = s0_ref[0]

    @pl.when(j > 0)
    def _():
        ext_ref[base - hw:base, :] = ext_ref[base + c - hw:base + c, :]

    ext_ref[base:base + c, :] = qkv_ref[...]
    newhist_ref[0] = ext_ref[base + c - hw:base + c, :]
    conv = None
    for k in range(CONV_B):
        term = ext_ref[base - hw + k:base - hw + k + c, :] * cw_ref[k:k + 1, :]
        conv = term if conv is None else conv + term
    act = conv * _sigmoid(conv)

    beta = _sigmoid(misc_ref[:, b_col:b_col + N_HEADS])
    g = -jnp.exp(alog_ref[...]) * _softplus(misc_ref[:, a_col:a_col + N_HEADS] + dtb_ref[...])
    ri = _iota((c, c), 0)
    ci = _iota((c, c), 1)
    lower = ri >= ci
    strict = ri > ci
    gc = _dot_hi(jnp.where(lower, 1.0, 0.0), g)
    eye_h = jnp.where(_iota((N_HEADS, N_HEADS), 0) == _iota((N_HEADS, N_HEADS), 1), 1.0, 0.0)
    gc_t = _dot_nt(eye_h, gc, HI)
    eg = jnp.exp(gc)
    g_last = gc[c - 1:c, :]
    e_last = jnp.exp(g_last)
    e_rest = jnp.exp(g_last - gc)
    z = z_ref[...]
    heads = range(N_HEADS)
    col = lambda x, h: x[:, h:h + 1]
    qs = [act[:, h * DK_B:(h + 1) * DK_B] for h in heads]
    ks = [act[:, wq + h * DK_B:wq + (h + 1) * DK_B] for h in heads]
    vs = [act[:, 2 * wq + h * DV_B:2 * wq + (h + 1) * DV_B] for h in heads]
    qs = [q * lax.rsqrt(jnp.sum(q * q, axis=-1, keepdims=True) + 1e-6) * (DK_B ** -0.5) for q in qs]
    ks = [k * lax.rsqrt(jnp.sum(k * k, axis=-1, keepdims=True) + 1e-6) for k in ks]
    kbs = [k.astype(BF16) for k in ks]
    decay = [jnp.where(lower, jnp.exp(jnp.where(lower, col(gc, h) - gc_t[h:h + 1, :], 0.0)), 0.0) for h in heads]
    kk = [_dot_nt(kbs[h], kbs[h]) for h in heads]
    attn = [_dot_nt(qs[h].astype(BF16), kbs[h]) * decay[h] for h in heads]
    a_mat = [jnp.where(strict, col(beta, h) * kk[h] * decay[h], 0.0) for h in heads]
    t_mat = [t.astype(BF16) for t in _unit_lower_inverse(a_mat, c)]
    value = [_dot(t_mat[h], vs[h] * col(beta, h)) for h in heads]
    k_cum = [_dot(t_mat[h], ks[h] * (col(beta, h) * col(eg, h))) for h in heads]
    s_old = [s_ref[h] for h in heads]
    sbs = [s.astype(BF16) for s in s_old]
    v_new = [(value[h] - _dot(k_cum[h], sbs[h])).astype(BF16) for h in heads]
    o_inter = [_dot(qs[h] * col(eg, h), sbs[h]) for h in heads]
    o = [o_inter[h] + _dot(attn[h], v_new[h]) for h in heads]
    for h in heads:
        s_ref[h] = s_old[h] * col(e_last, h) + _dot_tn((ks[h] * col(e_rest, h)).astype(BF16), v_new[h])
    for h in heads:
        zh = z[:, h * DV_B:(h + 1) * DV_B]
        y_ref[:, h * DV_B:(h + 1) * DV_B] = _head_rms(o[h], nw_ref[...]) * (zh * _sigmoid(zh))
    sfin_ref[0] = s_ref[...]


def _gdn_multi_kernel(qkv_ref, misc_ref, z_ref, hist_ref, s0_ref, cw_ref, alog_ref, dtb_ref, nw_ref,
                      y_ref, sfin_ref, newhist_ref, ext_ref, s_ref, *, c, a_col, b_col):
    j = pl.program_id(1)
    hw = CONV_B - 1
    base = SUBLANES
    wq = N_HEADS * DK_B
    seqs = range(qkv_ref.shape[0])

    @pl.when(j == 0)
    def _():
        ext_ref[:, base - hw:base, :] = hist_ref[...]
        s_ref[...] = s0_ref[...]

    @pl.when(j > 0)
    def _():
        ext_ref[:, base - hw:base, :] = ext_ref[:, base + c - hw:base + c, :]

    ext_ref[:, base:base + c, :] = qkv_ref[...]
    newhist_ref[...] = ext_ref[:, base + c - hw:base + c, :]
    ri = _iota((c, c), 0)
    ci = _iota((c, c), 1)
    lower = ri >= ci
    strict = ri > ci
    tri = jnp.where(lower, 1.0, 0.0)
    eye_h = jnp.where(_iota((N_HEADS, N_HEADS), 0) == _iota((N_HEADS, N_HEADS), 1), 1.0, 0.0)

    def conv_act(b):
        conv = None
        for k in range(CONV_B):
            term = ext_ref[b, base - hw + k:base - hw + k + c, :] * cw_ref[k:k + 1, :]
            conv = term if conv is None else conv + term
        return conv * _sigmoid(conv)

    act = [conv_act(b) for b in seqs]
    beta = [_sigmoid(misc_ref[b, :, b_col:b_col + N_HEADS]) for b in seqs]
    g = [-jnp.exp(alog_ref[...]) * _softplus(misc_ref[b, :, a_col:a_col + N_HEADS] + dtb_ref[...]) for b in seqs]
    gc = [_dot_hi(tri, x) for x in g]
    gc_t = [_dot_nt(eye_h, x, HI) for x in gc]
    eg = [jnp.exp(x) for x in gc]
    e_last = [jnp.exp(x[c - 1:c, :]) for x in gc]
    e_rest = [jnp.exp(x[c - 1:c, :] - x) for x in gc]

    items = [(b, h) for b in seqs for h in range(N_HEADS)]
    col = lambda x, h: x[:, h:h + 1]
    qs = [act[b][:, h * DK_B:(h + 1) * DK_B] for b, h in items]
    ks = [act[b][:, wq + h * DK_B:wq + (h + 1) * DK_B] for b, h in items]
    vs = [act[b][:, 2 * wq + h * DV_B:2 * wq + (h + 1) * DV_B] for b, h in items]
    qs = [q * lax.rsqrt(jnp.sum(q * q, axis=-1, keepdims=True) + 1e-6) * (DK_B ** -0.5) for q in qs]
    ks = [k * lax.rsqrt(jnp.sum(k * k, axis=-1, keepdims=True) + 1e-6) for k in ks]
    kbs = [k.astype(BF16) for k in ks]
    decay = [jnp.where(lower, jnp.exp(jnp.where(lower, col(gc[b], h) - gc_t[b][h:h + 1, :], 0.0)), 0.0)
             for b, h in items]
    kk = [_dot_nt(k, k) for k in kbs]
    attn = [_dot_nt(q.astype(BF16), k) * d for q, k, d in zip(qs, kbs, decay)]
    a_mat = [jnp.where(strict, col(beta[b], h) * x * d, 0.0) for (b, h), x, d in zip(items, kk, decay)]
    t_mat = [t.astype(BF16) for t in _unit_lower_inverse(a_mat, c)]
    value = [_dot(t, v * col(beta[b], h)) for (b, h), t, v in zip(items, t_mat, vs)]
    k_cum = [_dot(t, k * (col(beta[b], h) * col(eg[b], h))) for (b, h), t, k in zip(items, t_mat, ks)]
    s_old = [s_ref[b, h] for b, h in items]
    sbs = [s.astype(BF16) for s in s_old]
    v_new = [(v - _dot(kc, s)).astype(BF16) for v, kc, s in zip(value, k_cum, sbs)]
    o_inter = [_dot(q * col(eg[b], h), s) for (b, h), q, s in zip(items, qs, sbs)]
    o = [oi + _dot(a, v) for oi, a, v in zip(o_inter, attn, v_new)]
    for i, (b, h) in enumerate(items):
        s_ref[b, h] = s_old[i] * col(e_last[b], h) + _dot_tn((ks[i] * col(e_rest[b], h)).astype(BF16), v_new[i])
    for i, (b, h) in enumerate(items):
        zh = z_ref[b, :, h * DV_B:(h + 1) * DV_B]
        y_ref[b, :, h * DV_B:(h + 1) * DV_B] = _head_rms(o[i], nw_ref[...]) * (zh * _sigmoid(zh))
    sfin_ref[...] = s_ref[...]


def _gdn(qkv, misc, z, hist, s0, conv_w, a_log, dt_bias, norm_w, *, nb, c, a_col, b_col):
    n, wqkv = qkv.shape
    seq = n // nb
    nbb = GDN_SEQS_PER_STEP if nb % GDN_SEQS_PER_STEP == 0 else 1
    tok = lambda b, j: (b, j, 0)
    per_b3 = lambda b, j: (b, 0, 0)
    per_b4 = lambda b, j: (b, 0, 0, 0)
    const2 = lambda b, j: (0, 0)
    hw = CONV_B - 1
    tokens = [a.reshape(nb, seq, a.shape[1]) for a in (qkv, misc, z)]
    y, s_fin, new_hist = pl.pallas_call(
        functools.partial(_gdn_multi_kernel, c=c, a_col=a_col, b_col=b_col),
        grid=(nb // nbb, seq // c),
        in_specs=[pl.BlockSpec((nbb, c, a.shape[2]), tok) for a in tokens]
                 + [pl.BlockSpec((nbb, hw, wqkv), per_b3),
                    pl.BlockSpec((nbb,) + s0.shape[1:], per_b4),
                    pl.BlockSpec((CONV_B, wqkv), const2),
                    pl.BlockSpec((1, N_HEADS), const2),
                    pl.BlockSpec((1, N_HEADS), const2),
                    pl.BlockSpec((1, DV_B), const2)],
        out_specs=[pl.BlockSpec((nbb, c, z.shape[1]), tok),
                   pl.BlockSpec((nbb,) + s0.shape[1:], per_b4),
                   pl.BlockSpec((nbb, hw, wqkv), per_b3)],
        out_shape=[jax.ShapeDtypeStruct((nb, seq, z.shape[1]), F32),
                   jax.ShapeDtypeStruct(s0.shape, F32),
                   jax.ShapeDtypeStruct(hist.shape, F32)],
        scratch_shapes=[pltpu.VMEM((nbb, SUBLANES + c, wqkv), F32),
                        pltpu.VMEM((nbb,) + s0.shape[1:], F32)],
        compiler_params=_params("parallel", "arbitrary"),
        name="gated_deltanet",
    )(*tokens, hist, s0, conv_w, a_log.reshape(1, -1), dt_bias.reshape(1, -1), norm_w.reshape(1, -1))
    return y.reshape(n, z.shape[1]), s_fin, new_hist


def _gdn_single(qkv, misc, z, hist, s0, conv_w, a_log, dt_bias, norm_w, *, nb, c, a_col, b_col):
    n, wqkv = qkv.shape
    nch = n // (nb * c)
    row = lambda b, j: (b * nch + j, 0)
    per_b3 = lambda b, j: (b, 0, 0)
    per_b4 = lambda b, j: (b, 0, 0, 0)
    const2 = lambda b, j: (0, 0)
    hw = CONV_B - 1
    return pl.pallas_call(
        functools.partial(_gdn_kernel, c=c, a_col=a_col, b_col=b_col),
        grid=(nb, nch),
        in_specs=[pl.BlockSpec((c, wqkv), row),
                  pl.BlockSpec((c, misc.shape[1]), row),
                  pl.BlockSpec((c, z.shape[1]), row),
                  pl.BlockSpec((1, hw, wqkv), per_b3),
                  pl.BlockSpec((1,) + s0.shape[1:], per_b4),
                  pl.BlockSpec((CONV_B, wqkv), const2),
                  pl.BlockSpec((1, N_HEADS), const2),
                  pl.BlockSpec((1, N_HEADS), const2),
                  pl.BlockSpec((1, DV_B), const2)],
        out_specs=[pl.BlockSpec((c, z.shape[1]), row),
                   pl.BlockSpec((1,) + s0.shape[1:], per_b4),
                   pl.BlockSpec((1, hw, wqkv), per_b3)],
        out_shape=[jax.ShapeDtypeStruct(z.shape, F32),
                   jax.ShapeDtypeStruct(s0.shape, F32),
                   jax.ShapeDtypeStruct(hist.shape, F32)],
        scratch_shapes=[pltpu.VMEM((SUBLANES + c, wqkv), F32),
                        pltpu.VMEM(s0.shape[1:], F32)],
        compiler_params=_params("parallel", "arbitrary"),
        name="gated_deltanet",
    )(qkv, misc, z, hist, s0, conv_w, a_log.reshape(1, -1), dt_bias.reshape(1, -1), norm_w.reshape(1, -1))


def _mlstm_kernel(qk_ref, v_ref, og_ref, misc_ref, c0_ref, n0_ref, m0_ref, ib_ref, fb_ref, nw_ref,
                  y_ref, cfin_ref, nfin_ref, mfin_ref, c_ref, n_ref, m_ref, *, l, nbb, i_col, f_col):
    j = pl.program_id(1)
    wq = N_HEADS * DK_C

    @pl.when(j == 0)
    def _():
        c_ref[...] = c0_ref[...]
        n_ref[...] = n0_ref[...]
        m_ref[...] = m0_ref[...]

    causal = _iota((l, l), 0) >= _iota((l, l), 1)
    tri = jnp.where(causal, 1.0, 0.0)
    eye_h = jnp.where(_iota((N_HEADS, N_HEADS), 0) == _iota((N_HEADS, N_HEADS), 1), 1.0, 0.0)
    seqs = range(nbb)
    ig = [misc_ref[b, :, i_col:i_col + N_HEADS] + ib_ref[...] for b in seqs]
    lf = [-_softplus(-(misc_ref[b, :, f_col:f_col + N_HEADS] + fb_ref[...])) for b in seqs]
    fc = [_dot_hi(tri, lf[b]) for b in seqs]
    row_terms = [_dot_nt(eye_h, ig[b] - fc[b], HI) for b in seqs]
    m_prev = [m_ref[b] for b in seqs]
    log_inter = [fc[b] + m_prev[b] for b in seqs]
    f_last = [fc[b][l - 1:l, :] for b in seqs]
    m_vec = [f_last[b] + jnp.maximum(m_prev[b], jnp.max(ig[b] - fc[b], axis=0, keepdims=True)) for b in seqs]
    dec_vec = [jnp.exp(f_last[b] + m_prev[b] - m_vec[b]) for b in seqs]
    w_last = [jnp.exp(f_last[b] - fc[b] + ig[b] - m_vec[b]) for b in seqs]

    items = [(b, h) for b in seqs for h in range(N_HEADS)]
    col = lambda x, h: x[:, h:h + 1]
    qs = [qk_ref[b, :, h * DK_C:(h + 1) * DK_C] for b, h in items]
    ks = [qk_ref[b, :, wq + h * DK_C:wq + (h + 1) * DK_C] * (DK_C ** -0.5) for b, h in items]
    vbs = [v_ref[b, :, h * DV_C:(h + 1) * DV_C].astype(BF16) for b, h in items]
    c_old = [c_ref[b, h] for b, h in items]
    n_old = [n_ref[b, h:h + 1, :] for b, h in items]
    qk_raw = [_dot_nt(q.astype(BF16), k.astype(BF16)) for q, k in zip(qs, ks)]
    q_c = [_dot(q, c) for q, c in zip(qs, c_old)]
    log_w = [jnp.where(causal, col(fc[b], h) + row_terms[b][h:h + 1, :], -jnp.inf) for b, h in items]
    li = [col(log_inter[b], h) for b, h in items]
    m_t = [jnp.maximum(a, jnp.max(lw, axis=-1, keepdims=True)) for a, lw in zip(li, log_w)]
    w_inter = [jnp.exp(a - m) for a, m in zip(li, m_t)]
    qkw = [r * jnp.exp(lw - m) for r, lw, m in zip(qk_raw, log_w, m_t)]
    num = [_dot(a, vb) + wi * qc for a, vb, wi, qc in zip(qkw, vbs, w_inter, q_c)]
    den = [jnp.sum(a, axis=-1, keepdims=True) + wi * jnp.sum(q * n, axis=-1, keepdims=True)
           for a, wi, q, n in zip(qkw, w_inter, qs, n_old)]
    hid = [nu / jnp.maximum(jnp.abs(de), jnp.exp(-m)) for nu, de, m in zip(num, den, m_t)]
    kw = [k * col(w_last[b], h) for k, (b, h) in zip(ks, items)]
    for i, (b, h) in enumerate(items):
        c_ref[b, h] = col(dec_vec[b], h) * c_old[i] + _dot_tn(kw[i].astype(BF16), vbs[i])
        n_ref[b, h:h + 1, :] = col(dec_vec[b], h) * n_old[i] + jnp.sum(kw[i], axis=0, keepdims=True)
    for i, (b, h) in enumerate(items):
        oh = og_ref[b, :, h * DV_C:(h + 1) * DV_C]
        y_ref[b, :, h * DV_C:(h + 1) * DV_C] = _head_rms(hid[i], nw_ref[...]) * _sigmoid(oh)
    for b in seqs:
        m_ref[b] = m_vec[b]
    cfin_ref[...] = c_ref[...]
    nfin_ref[...] = n_ref[...]
    mfin_ref[...] = m_ref[...]


def _mlstm(qk, v, og, misc, c0, n0, m0, i_bias, f_bias, norm_w, *, nb, l, i_col, f_col):
    n = qk.shape[0]
    seq = n // nb
    nbb = SEQS_PER_STEP if nb % SEQS_PER_STEP == 0 else 1
    tok = lambda b, j: (b, j, 0)
    per_b3 = lambda b, j: (b, 0, 0)
    per_b4 = lambda b, j: (b, 0, 0, 0)
    const2 = lambda b, j: (0, 0)
    m0 = m0.reshape(nb, 1, N_HEADS)
    tokens = [a.reshape(nb, seq, a.shape[1]) for a in (qk, v, og, misc)]
    outs = pl.pallas_call(
        functools.partial(_mlstm_kernel, l=l, nbb=nbb, i_col=i_col, f_col=f_col),
        grid=(nb // nbb, seq // l),
        in_specs=[pl.BlockSpec((nbb, l, a.shape[2]), tok) for a in tokens]
                 + [pl.BlockSpec((nbb,) + c0.shape[1:], per_b4),
                    pl.BlockSpec((nbb,) + n0.shape[1:], per_b3),
                    pl.BlockSpec((nbb, 1, N_HEADS), per_b3),
                    pl.BlockSpec((1, N_HEADS), const2),
                    pl.BlockSpec((1, N_HEADS), const2),
                    pl.BlockSpec((1, DV_C), const2)],
        out_specs=[pl.BlockSpec((nbb, l, v.shape[1]), tok),
                   pl.BlockSpec((nbb,) + c0.shape[1:], per_b4),
                   pl.BlockSpec((nbb,) + n0.shape[1:], per_b3),
                   pl.BlockSpec((nbb, 1, N_HEADS), per_b3)],
        out_shape=[jax.ShapeDtypeStruct((nb, seq, v.shape[1]), F32),
                   jax.ShapeDtypeStruct(c0.shape, F32),
                   jax.ShapeDtypeStruct(n0.shape, F32),
                   jax.ShapeDtypeStruct(m0.shape, F32)],
        scratch_shapes=[pltpu.VMEM((nbb,) + c0.shape[1:], F32),
                        pltpu.VMEM((nbb,) + n0.shape[1:], F32),
                        pltpu.VMEM((nbb, 1, N_HEADS), F32)],
        compiler_params=_params("parallel", "arbitrary"),
        name="mlstm",
    )(*tokens, c0, n0, m0, i_bias.reshape(1, -1), f_bias.reshape(1, -1), norm_w.reshape(1, -1))
    y, c_fin, n_fin, m_fin = outs
    return y.reshape(n, v.shape[1]), c_fin, n_fin, m_fin.reshape(nb, N_HEADS)


def _block_mask(rows, row_group, cols, col_group):
    return (np.arange(rows)[:, None] // row_group == np.arange(cols)[None, :] // col_group).astype(np.float32)


def _cummax_rows(x):
    rows = x.shape[0]
    row = _iota(x.shape, 0)
    sh = 1
    while sh < rows:
        x = jnp.maximum(x, jnp.where(row >= sh, pltpu.roll(x, sh, axis=0), -jnp.inf))
        sh *= 2
    return x


def _dot2(a, b):
    hi, lo = _split(a)
    return jnp.dot(hi, b, preferred_element_type=F32) + jnp.dot(lo, b, preferred_element_type=F32)


def _mlstm_dense_kernel(qk_ref, v_ref, og_ref, misc_ref, c0_ref, n0_ref, m0_ref, ib_ref, fb_ref, nw_ref,
                        el_ref, ev_ref, ek_ref, kmask_ref, vmask_ref, cmask_ref, cmaskb_ref, rms_ref, causal_ref,
                        dsel_ref,
                        y_ref, cfin_ref, nfin_ref, mfin_ref, c_ref, n_ref, m_ref, *, l, i_col, f_col):
    j = pl.program_id(1)
    wq = N_HEADS * DK_C
    nbb = qk_ref.shape[0]
    seqs = range(nbb)

    @pl.when(j == 0)
    def _():
        c_ref[...] = jnp.zeros(c_ref.shape, F32)
        for b in seqs:
            for h in range(N_HEADS):
                c_ref[b, h * DK_C:(h + 1) * DK_C, h * DV_C:(h + 1) * DV_C] = c0_ref[b, h]
        n_ref[...] = n0_ref[...]
        m_ref[...] = m0_ref[...]

    each = lambda f, *xs: [f(*args) for args in zip(*xs)]
    dot = functools.partial(jnp.dot, preferred_element_type=F32)
    spread = lambda x, e_ref: sum(dot(p, e_ref[...]) for p in _split3(x))
    tri = jnp.where(_iota((l, l), 0) >= _iota((l, l), 1), 1.0, 0.0).astype(BF16)
    ig = [misc_ref[b, :, i_col:i_col + N_HEADS] + ib_ref[...] for b in seqs]
    lf = [-_softplus(-(misc_ref[b, :, f_col:f_col + N_HEADS] + fb_ref[...])) for b in seqs]
    fc = each(lambda x: sum(dot(tri, p) for p in _split3(x)), lf)
    m_prev = [m_ref[b] for b in seqs]
    log_inter = each(jnp.add, fc, m_prev)
    a = each(jnp.subtract, ig, fc)
    m_t = each(lambda li, f, x: jnp.maximum(li, f + _cummax_rows(x)), log_inter, fc, a)
    w_inter = each(lambda li, m: jnp.exp(li - m), log_inter, m_t)
    w_last = each(lambda f, i, m: jnp.exp(f[l - 1:l, :] - f + i - m[l - 1:l, :]), fc, ig, m_t)

    key_terms = each(lambda x: jnp.sum(spread(x, el_ref) * dsel_ref[...], axis=0, keepdims=True), a)
    log_w = each(lambda f, kt: jnp.where(causal_ref[...] > 0.5, spread(f, el_ref) + kt, -jnp.inf), fc, key_terms)
    q = [qk_ref[b, :, :wq] for b in seqs]
    q_b = [x.astype(BF16) for x in q]
    k_s = [qk_ref[b, :, wq:] * (DK_C ** -0.5) for b in seqs]
    k_b = [x.astype(BF16) for x in k_s]
    v_b = [v_ref[b].astype(BF16) for b in seqs]
    kt_bd = [jnp.concatenate([x] * N_HEADS, axis=0) * kmask_ref[...] for x in k_b]
    v_bd = [jnp.concatenate([x] * N_HEADS, axis=0) * vmask_ref[...] for x in v_b]
    m_l = each(lambda m: spread(m, el_ref), m_t)
    qkw = each(lambda x, kt, lw, m: (_dot_nt(x, kt) * jnp.exp(lw - m)).astype(BF16), q_b, kt_bd, log_w, m_l)
    c_old = [c_ref[b] for b in seqs]
    n_old = [n_ref[b] for b in seqs]
    wi_s = each(lambda w: spread(w, ev_ref), w_inter)
    q_c = each(lambda x, c: dot(x, c.astype(BF16)), q_b, c_old)
    qn = each(lambda x, n: _dot2(x * n, cmaskb_ref[...]), q, n_old)
    num = each(lambda w, vb, wi, qc: dot(w, vb) + wi * qc, qkw, v_bd, wi_s, q_c)
    den = each(lambda w, wi, x: dot(w, vmask_ref[...]) + wi * x, qkw, wi_s, qn)
    m_v = m_l if l == DV_C else each(lambda m: spread(m, ev_ref), m_t)
    hid = each(lambda nu, de, m: nu / jnp.maximum(jnp.abs(de), jnp.exp(-m)), num, den, m_v)
    ms = each(lambda x: _dot2(x * x, rms_ref[...]), hid)
    for b in seqs:
        y_ref[b] = hid[b] * lax.rsqrt(ms[b] + 1e-6) * nw_ref[...] * _sigmoid(og_ref[b])

    kw = each(lambda k, w: k * spread(w, ek_ref), k_s, w_last)
    dec_rows = [jnp.broadcast_to(w[l - 1:l, :], (SUBLANES, N_HEADS)) for w in w_inter]
    upd = each(lambda x, vb: _dot_tn(x.astype(BF16), vb), kw, v_b)
    for b in seqs:
        c_ref[b] = c_old[b] * spread(dec_rows[b], ev_ref)[0:1, :] + upd[b] * cmask_ref[...]
        n_ref[b] = n_old[b] * spread(dec_rows[b], ek_ref)[0:1, :] + jnp.sum(kw[b], axis=0, keepdims=True)
        m_ref[b] = m_t[b][l - 1:l, :]

    @pl.when(j == pl.num_programs(1) - 1)
    def _():
        for b in seqs:
            for h in range(N_HEADS):
                cfin_ref[b, h] = c_ref[b, h * DK_C:(h + 1) * DK_C, h * DV_C:(h + 1) * DV_C]
        nfin_ref[...] = n_ref[...]
        mfin_ref[...] = m_ref[...]


def _mlstm_dense(qk, v, og, misc, c0, n0, m0, i_bias, f_bias, norm_w, *, nb, l, i_col, f_col):
    n = qk.shape[0]
    seq = n // nb
    wq, wv = N_HEADS * DK_C, N_HEADS * DV_C
    hl = N_HEADS * l
    nbb = SEQS_PER_STEP if nb % SEQS_PER_STEP == 0 else 1
    tok = lambda b, j: (b, j, 0)
    per_b3 = lambda b, j: (b, 0, 0)
    per_b4 = lambda b, j: (b, 0, 0, 0)
    const2 = lambda b, j: (0, 0)
    consts = [jnp.asarray(_block_mask(N_HEADS, 1, hl, l), BF16),
              jnp.asarray(_block_mask(N_HEADS, 1, wv, DV_C), BF16),
              jnp.asarray(_block_mask(N_HEADS, 1, wq, DK_C), BF16),
              jnp.asarray(_block_mask(hl, l, wq, DK_C), BF16),
              jnp.asarray(_block_mask(hl, l, wv, DV_C), BF16),
              jnp.asarray(_block_mask(wq, DK_C, wv, DV_C)),
              jnp.asarray(_block_mask(wq, DK_C, wv, DV_C), BF16),
              jnp.asarray(_block_mask(wv, DV_C, wv, DV_C) / DV_C, BF16),
              jnp.asarray(np.tile(np.tril(np.ones((l, l), np.float32)), (1, N_HEADS))),
              jnp.asarray(np.tile(np.eye(l, dtype=np.float32), (1, N_HEADS)))]
    tokens = [a.reshape(nb, seq, a.shape[1]) for a in (qk, v, og, misc)]
    outs = pl.pallas_call(
        functools.partial(_mlstm_dense_kernel, l=l, i_col=i_col, f_col=f_col),
        grid=(nb // nbb, seq // l),
        in_specs=[pl.BlockSpec((nbb, l, a.shape[2]), tok) for a in tokens]
                 + [pl.BlockSpec((nbb,) + c0.shape[1:], per_b4),
                    pl.BlockSpec((nbb, 1, wq), per_b3),
                    pl.BlockSpec((nbb, 1, N_HEADS), per_b3),
                    pl.BlockSpec((1, N_HEADS), const2),
                    pl.BlockSpec((1, N_HEADS), const2),
                    pl.BlockSpec((1, wv), const2)]
                 + [pl.BlockSpec(c.shape, const2) for c in consts],
        out_specs=[pl.BlockSpec((nbb, l, wv), tok),
                   pl.BlockSpec((nbb,) + c0.shape[1:], per_b4),
                   pl.BlockSpec((nbb, 1, wq), per_b3),
                   pl.BlockSpec((nbb, 1, N_HEADS), per_b3)],
        out_shape=[jax.ShapeDtypeStruct((nb, seq, wv), F32),
                   jax.ShapeDtypeStruct(c0.shape, F32),
                   jax.ShapeDtypeStruct((nb, 1, wq), F32),
                   jax.ShapeDtypeStruct((nb, 1, N_HEADS), F32)],
        scratch_shapes=[pltpu.VMEM((nbb, wq, wv), F32),
                        pltpu.VMEM((nbb, 1, wq), F32),
                        pltpu.VMEM((nbb, 1, N_HEADS), F32)],
        compiler_params=_params("parallel", "arbitrary"),
        name="mlstm",
    )(*tokens, c0, n0.reshape(nb, 1, wq), m0.reshape(nb, 1, N_HEADS), i_bias.reshape(1, -1), f_bias.reshape(1, -1),
      jnp.tile(norm_w, N_HEADS).reshape(1, wv), *consts)
    y, c_fin, n_fin, m_fin = outs
    return y.reshape(n, wv), c_fin, n_fin.reshape(nb, N_HEADS, DK_C), m_fin.reshape(nb, N_HEADS)


def _band_prompt_kernel(q_ref, k_ref, v_ref, bias_ref, o_ref, kb_ref, vt_ref, ot_ref, *, tq, seq):
    g = pl.program_id(1)
    w = q_ref.shape[1]
    pad = BAND_CHUNKS * CHUNK
    lw = pad + tq
    npad = pad // LANES
    per_tile = tq // LANES
    blk = _row_tile(seq, 4 * LANES)

    @pl.when(g == 0)
    def _():
        kb_ref[0:pad, :] = jnp.zeros((pad, w), BF16)
        vt_ref[0:npad] = jnp.zeros((npad, w, LANES), BF16)
        for r0 in range(0, seq, blk):
            kb_ref[pad + r0:pad + r0 + blk, :] = k_ref[0, r0:r0 + blk, :].astype(BF16)
        for j in range(seq // LANES):
            vt_ref[npad + j] = v_ref[0, j * LANES:(j + 1) * LANES, :].T.astype(BF16)

    start = pl.multiple_of(g * tq, tq)
    q = (q_ref[...] * (HEAD_DIM ** -0.5)).astype(BF16)
    before_seq = jnp.where(_iota((lw, tq), 0) >= pad - g * tq, 0.0, NEG)
    for h in range(N_HEADS):
        hs = slice(h * HEAD_DIM, (h + 1) * HEAD_DIM)
        s = _dot_nt(kb_ref[pl.ds(start, lw), hs], q[:, hs]) + (bias_ref[h] + before_seq)
        p = jnp.exp(s - _reduce_rows(s, jnp.maximum, jnp.max))
        pn = (p * (1.0 / _reduce_rows(p, jnp.add, jnp.sum))).astype(BF16)
        acc = None
        for j in range(lw // LANES):
            part = jnp.dot(vt_ref[g * per_tile + j, hs, :], pn[j * LANES:(j + 1) * LANES, :],
                           preferred_element_type=F32)
            acc = part if acc is None else acc + part
        ot_ref[hs, :] = acc
    o_ref[...] = ot_ref[...].T


def _band_prompt(q, k, v, bias, *, nb, tq):
    n, w = q.shape
    seq = k.shape[1]
    nq = seq // tq
    pad = BAND_CHUNKS * CHUNK
    row = lambda b, g: (b * nq + g, 0)
    per_b = lambda b, g: (b, 0, 0)
    return pl.pallas_call(
        functools.partial(_band_prompt_kernel, tq=tq, seq=seq),
        grid=(nb, nq),
        in_specs=[pl.BlockSpec((tq, w), row),
                  pl.BlockSpec((1, seq, w), per_b),
                  pl.BlockSpec((1, seq, w), per_b),
                  pl.BlockSpec(bias.shape, lambda b, g: (0, 0, 0))],
        out_specs=pl.BlockSpec((tq, w), row),
        out_shape=jax.ShapeDtypeStruct(q.shape, F32),
        scratch_shapes=[pltpu.VMEM((pad + seq, w), BF16),
                        pltpu.VMEM(((pad + seq) // LANES, w, LANES), BF16),
                        pltpu.VMEM((w, tq), F32)],
        compiler_params=_params("parallel", "arbitrary"),
        name="band_attention_prompt",
    )(q, k, v, bias)


def _band_sample_kernel(q_ref, kc_ref, vc_ref, kn_ref, vn_ref, biasc_ref, biasn_ref, o_ref):
    pieces = [(lambda h: kc_ref[0, 0, h], lambda h: vc_ref[0, 0, h], biasc_ref),
              (lambda h: kn_ref[0, h], lambda h: vn_ref[0, h], biasn_ref)]
    q = q_ref[...]
    for h in range(N_HEADS):
        hs = slice(h * HEAD_DIM, (h + 1) * HEAD_DIM)
        qh = q[:, hs].astype(BF16)
        scores = [jnp.dot(qh, load_k(h).astype(BF16), preferred_element_type=F32) * (HEAD_DIM ** -0.5) + b_ref[h]
                  for load_k, _, b_ref in pieces]
        m = functools.reduce(jnp.maximum, [jnp.max(s, axis=-1, keepdims=True) for s in scores])
        ps = [jnp.exp(s - m) for s in scores]
        den = sum(jnp.sum(p, axis=-1, keepdims=True) for p in ps)
        o = sum(_dot_nt(p.astype(BF16), load_v(h).astype(BF16)) for p, (_, load_v, _) in zip(ps, pieces))
        o_ref[:, hs] = o / den


def _band_sample(q, kc, vc, layer, kn, vn, bias_c, bias_n, *, nb, tq):
    n, w = q.shape
    row = lambda b: (b, 0)
    const3 = lambda b: (0, 0, 0)
    return pl.pallas_call(
        _band_sample_kernel,
        grid=(nb,),
        in_specs=[pl.BlockSpec((tq, w), row)]
                 + [pl.BlockSpec((1, 1) + a.shape[2:], lambda b: (layer, b, 0, 0, 0)) for a in (kc, vc)]
                 + [pl.BlockSpec((1,) + a.shape[1:], lambda b: (b, 0, 0, 0)) for a in (kn, vn)]
                 + [pl.BlockSpec(bias_c.shape, const3), pl.BlockSpec(bias_n.shape, const3)],
        out_specs=pl.BlockSpec((tq, w), row),
        out_shape=jax.ShapeDtypeStruct(q.shape, F32),
        compiler_params=_params("parallel"),
        name="band_attention_sample",
    )(q, kc, vc, kn, vn, bias_c, bias_n)


def _t5_bucket(rel):
    nb = T5_BUCKETS // 2
    max_exact = nb // 2
    n = jnp.abs(rel)
    n_f = jnp.maximum(n, 1).astype(jnp.float32)
    large = max_exact + (jnp.log(n_f / max_exact) / math.log(T5_MAX_DIST / max_exact) * (nb - max_exact)).astype(jnp.int32)
    large = jnp.minimum(large, nb - 1)
    return jnp.where(rel > 0, nb, 0) + jnp.where(n < max_exact, n, large)


def _toeplitz_bias(fn, n_rows, n_cols):
    n = n_rows + n_cols
    m = np.arange(n)
    f = jnp.transpose(fn(np.where(m < n_cols, m, m - n))).astype(F32)
    flat = jnp.tile(f, (1, n_rows))[:, :n_rows * (n - 1)]
    return flat.reshape(f.shape[0], n_rows, n - 1)[:, :, :n_cols]


def _pack_cols(w, sizes, groups):
    offs = np.concatenate([[0], np.cumsum(sizes)])
    cols, widths = [], []
    for grp in groups:
        width = 0
        for idx in grp:
            cols.append(w[:, offs[idx]:offs[idx + 1]])
            width += sizes[idx]
        pad = (-width) % LANES
        if pad:
            cols.append(jnp.zeros((w.shape[0], pad), w.dtype))
        widths.append(width + pad)
    return jnp.concatenate(cols, axis=1).astype(BF16), tuple(widths)


def _row_tile(n, target):
    t = min(n, target)
    while n % t:
        t //= 2
    return t


def kernel(x_prompt, x_sample, cache_a_k, cache_a_v, cache_a_kidx, state_b_s, state_b_conv, state_c_c, state_c_n, state_c_m, cache_d_k, cache_d_v, state_ffn_conv, w_in_even, w_out_even, t5_bias, b_conv_w, b_a_log, b_dt_bias, b_norm_w, w_in_odd, w_out_odd, c_i_bias, c_f_bias, c_norm_w, d_rel_bias, ln_mix_g, ln_mix_b, ln_ffn_g, ln_ffn_b, ffn_w_up, ffn_conv_w, ffn_w_down):
    bp, sp, d = x_prompt.shape
    bs, ts, _ = x_sample.shape
    depth = ffn_w_up.shape[0]
    past = cache_a_k.shape[2]
    d_win = cache_d_k.shape[2]
    dff = ffn_w_down.shape[1]
    alpha = (2 * depth) ** 0.25
    w_a = N_HEADS * HEAD_DIM
    w_b = N_HEADS * DV_B
    w_c = N_HEADS * DV_C
    qkv_b_w = 2 * N_HEADS * DK_B + w_b
    even_sizes = (w_a, w_a, w_a, N_IDX_HEADS * D_IDX, D_IDX, N_IDX_HEADS, qkv_b_w, N_HEADS, N_HEADS, w_b)
    odd_sizes = (N_HEADS * DK_C, N_HEADS * DK_C, w_c, N_HEADS, N_HEADS, w_c, w_a, w_a, w_a)
    even_groups = ((0,), (1,), (2,), (3,), (6,), (9,), (4, 5, 7, 8))
    wi_col, a_col, b_col = D_IDX, D_IDX + N_IDX_HEADS, D_IDX + N_IDX_HEADS + N_HEADS
    odd_groups = ((0, 1), (2,), (5,), (6,), (7,), (8,), (3, 4))
    i_col, f_col = 0, N_HEADS

    assert sp % CHUNK == 0 and ts <= CHUNK and past % CHUNK == 0 and past >= T5_FAR
    assert (past + ts - 1) // CHUNK == past // CHUNK
    topk_p = min(TOPK_MAX, sp // 4)
    topk_s = min(TOPK_MAX, (past + ts) // 4)
    n_p, n_s = bp * sp, bs * ts
    tm_p = _row_tile(n_p, 512)
    tm_s = _row_tile(n_s, 512)
    tff_p = _row_tile(sp, 512)
    ns_s = _row_tile(bs, max(1, 256 // ts))

    t5 = lambda rel: t5_bias[_t5_bucket(jnp.asarray(rel, jnp.int32))]
    fbias = t5(np.array([-T5_FAR - 1]))
    tq_dsa = 2 * LANES
    assert sp % tq_dsa == 0 and T5_FAR <= tq_dsa
    corr_p = jnp.stack([_toeplitz_bias(lambda dd: t5(-dd - tq_dsa * i) - fbias, tq_dsa, tq_dsa) for i in range(2)])
    tq_a = 2 * CHUNK
    ln_s = T5_FAR + ts
    nbias_s = _toeplitz_bias(lambda dd: t5(dd - T5_FAR), ts, ln_s)[None]

    lw = BAND_CHUNKS * CHUNK + tq_a
    r_chunk = np.arange(lw)[:, None] // CHUNK
    q_chunk = BAND_CHUNKS + np.arange(tq_a)[None, :] // CHUNK
    band_ok = (r_chunk >= q_chunk - BAND_CHUNKS) & (r_chunk <= q_chunk)
    pos_q = past + np.arange(ts)
    pos_kc = past - d_win + np.arange(d_win)
    def band_valid(pos_k):
        kch, qch = pos_k // CHUNK, pos_q // CHUNK
        return (pos_k[None] >= 0) & (kch[None] >= qch[:, None] - BAND_CHUNKS) & (kch[None] <= qch[:, None])
    def band_bias(table, shift, n_rows, n_cols, valid=None):
        bias = _toeplitz_bias(lambda dd: table[np.clip(dd + shift, -REL_CLIP, REL_CLIP) + REL_CLIP], n_rows, n_cols)
        return bias if valid is None else jnp.where(jnp.asarray(valid)[None], bias, NEG)

    cak_t = jnp.transpose(cache_a_k, (0, 1, 3, 4, 2))
    cav_t = jnp.transpose(cache_a_v, (0, 1, 3, 4, 2))
    caki_t = jnp.transpose(cache_a_kidx, (0, 1, 3, 2))
    cdk_t = jnp.transpose(cache_d_k, (0, 1, 3, 4, 2))
    cdv_t = jnp.transpose(cache_d_v, (0, 1, 3, 4, 2))

    xp = x_prompt.reshape(n_p, d)
    xs = x_sample.reshape(n_s, d)
    outs = {k: [] for k in ("ak_p", "ak_s", "av_p", "av_s", "aki_p", "aki_s", "bs_p", "bs_s", "bc_p", "bc_s",
                            "cc_p", "cc_s", "cn_p", "cn_s", "cm_p", "cm_s", "dk_p", "dk_s", "dv_p", "dv_s",
                            "fc_p", "fc_s")}
    for layer in range(depth):
        if layer % 2 == 0:
            e = layer // 2
            w_in, widths = _pack_cols(w_in_even[e], even_sizes, even_groups)
            w_out = w_out_even[e].astype(BF16)
            qa, ka, va, qi, qkv_b, z_b, misc = _proj(xp, w_in, widths, tm_p)
            o_a = _dsa_blocks(qa, qi, misc, ka.reshape(bp, sp, w_a), va.reshape(bp, sp, w_a),
                              misc.reshape(bp, sp, LANES), corr_p, nb=bp, tq=tq_dsa, topk=topk_p, wi_col=wi_col)
            y_b, s_b, h_b = _gdn(qkv_b, misc, z_b, jnp.zeros((bp, CONV_B - 1, qkv_b_w), F32),
                                 jnp.zeros((bp, N_HEADS, DK_B, DV_B), F32), b_conv_w[e], b_a_log[e], b_dt_bias[e],
                                 b_norm_w[e], nb=bp, c=CHUNK, a_col=a_col, b_col=b_col)
            xp = _mm_res_ln([o_a, y_b], [w_out[:w_a], w_out[w_a:]], xp, ln_mix_g[layer], ln_mix_b[layer], alpha, tm_p)
            outs["ak_p"].append(ka.reshape(bp, sp, N_HEADS, HEAD_DIM))
            outs["av_p"].append(va.reshape(bp, sp, N_HEADS, HEAD_DIM))
            outs["aki_p"].append(misc[:, :D_IDX].reshape(bp, sp, D_IDX))
            outs["bs_p"].append(s_b)
            outs["bc_p"].append(h_b)
            qa, ka, va, qi, qkv_b, z_b, misc = _proj(xs, w_in, widths, tm_s)
            ki = misc[:, :D_IDX]
            heads_t = lambda u: jnp.transpose(u.reshape(bs, ts, N_HEADS, HEAD_DIM), (0, 2, 3, 1))
            near = (jnp.concatenate([cak_t[e, ..., past - T5_FAR:], heads_t(ka)], axis=-1),
                    jnp.concatenate([cav_t[e, ..., past - T5_FAR:], heads_t(va)], axis=-1),
                    jnp.concatenate([caki_t[e, ..., past - T5_FAR:],
                                     jnp.transpose(ki.reshape(bs, ts, D_IDX), (0, 2, 1))], axis=-1))
            o_a = _dsa_sample(qa, qi, misc, cak_t, cav_t, caki_t, e, near, nbias_s, fbias, nb=bs, tq=ts,
                              topk=topk_s, wi_col=wi_col)
            y_b, s_b, h_b = _gdn(qkv_b, misc, z_b, state_b_conv[e], state_b_s[e], b_conv_w[e], b_a_log[e],
                                 b_dt_bias[e], b_norm_w[e], nb=bs, c=ts, a_col=a_col, b_col=b_col)
            xs = _mm_res_ln([o_a, y_b], [w_out[:w_a], w_out[w_a:]], xs, ln_mix_g[layer], ln_mix_b[layer], alpha, tm_s)
            outs["ak_s"].append(ka.reshape(bs, ts, N_HEADS, HEAD_DIM))
            outs["av_s"].append(va.reshape(bs, ts, N_HEADS, HEAD_DIM))
            outs["aki_s"].append(ki.reshape(bs, ts, D_IDX))
            outs["bs_s"].append(s_b)
            outs["bc_s"].append(h_b)
        else:
            o = layer // 2
            w_in, widths = _pack_cols(w_in_odd[o], odd_sizes, odd_groups)
            w_out = w_out_odd[o].astype(BF16)
            qk_c, v_c, o_c, q_d, k_d, v_d, misc = _proj(xp, w_in, widths, tm_p)
            y_c, c_c, c_n, c_m = _mlstm_dense(qk_c, v_c, o_c, misc,jnp.zeros((bp, N_HEADS, DK_C, DV_C), F32),
                                        jnp.zeros((bp, N_HEADS, DK_C), F32), jnp.zeros((bp, N_HEADS), F32),
                                        c_i_bias[o], c_f_bias[o], c_norm_w[o], nb=bp, l=CHUNK, i_col=i_col, f_col=f_col)
            k3 = k_d.reshape(bp, sp, w_a)
            v3 = v_d.reshape(bp, sp, w_a)
            bias_bp = _toeplitz_bias(
                lambda dd: d_rel_bias[o][np.clip(-dd - BAND_CHUNKS * CHUNK, -REL_CLIP, REL_CLIP) + REL_CLIP], lw, tq_a)
            bias_bp = jnp.where(jnp.asarray(band_ok)[None], bias_bp, NEG)
            o_d = _band_prompt(q_d, k3, v3, bias_bp, nb=bp, tq=tq_a)
            xp = _mm_res_ln([y_c, o_d], [w_out[:w_c], w_out[w_c:]], xp, ln_mix_g[layer], ln_mix_b[layer], alpha, tm_p)
            d_win_p = min(BAND_CHUNKS * CHUNK, sp)
            outs["cc_p"].append(c_c)
            outs["cn_p"].append(c_n)
            outs["cm_p"].append(c_m)
            outs["dk_p"].append(k3[:, sp - d_win_p:].reshape(bp, d_win_p, N_HEADS, HEAD_DIM))
            outs["dv_p"].append(v3[:, sp - d_win_p:].reshape(bp, d_win_p, N_HEADS, HEAD_DIM))
            qk_c, v_c, o_c, q_d, k_d, v_d, misc = _proj(xs, w_in, widths, tm_s)
            y_c, c_c, c_n, c_m = _mlstm_dense(qk_c, v_c, o_c, misc,state_c_c[o], state_c_n[o], state_c_m[o],
                                        c_i_bias[o], c_f_bias[o], c_norm_w[o], nb=bs, l=ts, i_col=i_col, f_col=f_col)
            heads_t = lambda u: jnp.transpose(u.reshape(bs, ts, N_HEADS, HEAD_DIM), (0, 2, 3, 1))
            o_d = _band_sample(q_d, cdk_t, cdv_t, o, heads_t(k_d), heads_t(v_d),
                               band_bias(d_rel_bias[o], -d_win, ts, d_win, band_valid(pos_kc)),
                               band_bias(d_rel_bias[o], 0, ts, ts, band_valid(pos_q)), nb=bs, tq=ts)
            xs = _mm_res_ln([y_c, o_d], [w_out[:w_c], w_out[w_c:]], xs, ln_mix_g[layer], ln_mix_b[layer], alpha, tm_s)
            outs["cc_s"].append(c_c)
            outs["cn_s"].append(c_n)
            outs["cm_s"].append(c_m)
            outs["dk_s"].append(k_d.reshape(bs, ts, N_HEADS, HEAD_DIM))
            outs["dv_s"].append(v_d.reshape(bs, ts, N_HEADS, HEAD_DIM))
        w_up = ffn_w_up[layer].astype(BF16)
        w_down = ffn_w_down[layer].astype(BF16)
        act, hist_p = _ffn_up(xp, w_up, ffn_conv_w[layer], jnp.zeros((bp, CONV_FF - 1, 2 * dff), F32),
                              1, tff_p, sp // tff_p)
        xp = _mm_res_ln([act], [w_down], xp, ln_ffn_g[layer], ln_ffn_b[layer], alpha, tm_p)
        act, hist_s = _ffn_up(xs, w_up, ffn_conv_w[layer], state_ffn_conv[layer], ns_s, ts, 1)
        xs = _mm_res_ln([act], [w_down], xs, ln_ffn_g[layer], ln_ffn_b[layer], alpha, tm_s)
        outs["fc_p"].append(hist_p)
        outs["fc_s"].append(hist_s)

    st = lambda k: jnp.stack(outs[k])
    return (xp.reshape(bp, sp, d), xs.reshape(bs, ts, d),
            st("ak_p"), st("ak_s"), st("av_p"), st("av_s"), st("aki_p"), st("aki_s"),
            st("bs_p"), st("bs_s"), st("bc_p"), st("bc_s"),
            st("cc_p"), st("cc_s"), st("cn_p"), st("cn_s"), st("cm_p"), st("cm_s"),
            st("dk_p"), st("dk_s"), st("dv_p"), st("dv_s"),
            st("fc_p"), st("fc_s"))
```

```python
import functools
import math

import numpy as np
import jax
import jax.numpy as jnp
from jax import lax
from jax.experimental import pallas as pl
from jax.experimental.pallas import tpu as pltpu

F32 = jnp.float32
BF16 = jnp.bfloat16
HI = lax.Precision.HIGHEST

CHUNK = 64
HEAD_DIM = 64
N_HEADS = 8
N_IDX_HEADS = 8
D_IDX = 64
TOPK_MAX = 256
T5_BUCKETS = 32
T5_MAX_DIST = 128
DK_B = 64
DV_B = 64
CONV_B = 4
DK_C = 32
DV_C = 64
BAND_CHUNKS = 8
REL_CLIP = 128
CONV_FF = 3
NEAR_CHUNKS = 3
T5_FAR = 128
INV_BLOCK = 8
SAMPLE_STREAMS_PER_STEP = 4
GDN_SEQS_PER_STEP = 2
SEQS_PER_STEP = 4

LANES = 128
SUBLANES = 8
VMEM_LIMIT = 56 * 1024 * 1024

NEG = -1e30
INT_MIN = -2 ** 31


def _params(*sem):
    return pltpu.CompilerParams(dimension_semantics=sem, vmem_limit_bytes=VMEM_LIMIT)


def _dot(a, b):
    return jnp.dot(a.astype(BF16), b.astype(BF16), preferred_element_type=F32)


def _dot_nt(a, b, precision=None):
    return lax.dot_general(a, b, (((1,), (1,)), ((), ())), precision=precision, preferred_element_type=F32)


def _dot_tn(a, b, precision=None):
    return lax.dot_general(a, b, (((0,), (0,)), ((), ())), precision=precision, preferred_element_type=F32)


def _dot_hi(a, b):
    return jnp.dot(a, b, precision=HI, preferred_element_type=F32)


def _split(a):
    hi = a.astype(BF16)
    return hi, (a - hi.astype(F32)).astype(BF16)


def _split3(a):
    p1 = a.astype(BF16)
    r1 = a - p1.astype(F32)
    p2 = r1.astype(BF16)
    return p1, p2, (r1 - p2.astype(F32)).astype(BF16)


def _dot3(a, b):
    a_hi, a_lo = a
    b_hi, b_lo = b
    d = functools.partial(jnp.dot, preferred_element_type=F32)
    return d(a_hi, b_hi) + (d(a_hi, b_lo) + d(a_lo, b_hi))


def _sigmoid(x):
    return 1.0 / (1.0 + jnp.exp(-x))


def _softplus(x):
    return jnp.maximum(x, 0.0) + jnp.log(1.0 + jnp.exp(-jnp.abs(x)))


def _iota(shape, dim):
    return lax.broadcasted_iota(jnp.int32, shape, dim)


def _proj_kernel(x_ref, w_ref, *out_refs, sizes):
    xb = x_ref[...].astype(BF16)
    off = 0
    for o_ref, size in zip(out_refs, sizes):
        o_ref[...] = jnp.dot(xb, w_ref[:, off:off + size], preferred_element_type=F32)
        off += size


def _proj(x2d, w, sizes, tm):
    n, d = x2d.shape
    return pl.pallas_call(
        functools.partial(_proj_kernel, sizes=sizes),
        grid=(n // tm,),
        in_specs=[pl.BlockSpec((tm, d), lambda i: (i, 0)),
                  pl.BlockSpec((d, sum(sizes)), lambda i: (0, 0))],
        out_specs=[pl.BlockSpec((tm, s), lambda i: (i, 0)) for s in sizes],
        out_shape=[jax.ShapeDtypeStruct((n, s), F32) for s in sizes],
        compiler_params=_params("parallel"),
        name="in_proj",
    )(x2d, w)


def _mm_res_ln_kernel(*refs, nparts, alpha):
    part_refs = refs[:nparts]
    w_refs = refs[nparts:2 * nparts]
    x_ref, g_ref, b_ref, o_ref = refs[2 * nparts:]
    acc = alpha * x_ref[...]
    for p_ref, w_ref in zip(part_refs, w_refs):
        acc = acc + jnp.dot(p_ref[...].astype(BF16), w_ref[...], preferred_element_type=F32)
    mu = jnp.mean(acc, axis=-1, keepdims=True)
    cen = acc - mu
    var = jnp.mean(cen * cen, axis=-1, keepdims=True)
    o_ref[...] = cen * lax.rsqrt(var + 1e-5) * g_ref[...] + b_ref[...]


def _mm_res_ln(parts, ws, x2d, g, b, alpha, tm):
    n, d = x2d.shape
    nparts = len(parts)
    in_specs = ([pl.BlockSpec((tm, p.shape[1]), lambda i: (i, 0)) for p in parts]
                + [pl.BlockSpec(w.shape, lambda i: (0, 0)) for w in ws]
                + [pl.BlockSpec((tm, d), lambda i: (i, 0)),
                   pl.BlockSpec((1, d), lambda i: (0, 0)),
                   pl.BlockSpec((1, d), lambda i: (0, 0))])
    return pl.pallas_call(
        functools.partial(_mm_res_ln_kernel, nparts=nparts, alpha=alpha),
        grid=(n // tm,),
        in_specs=in_specs,
        out_specs=pl.BlockSpec((tm, d), lambda i: (i, 0)),
        out_shape=jax.ShapeDtypeStruct((n, d), F32),
        compiler_params=_params("parallel"),
        name="out_proj_ln",
    )(*parts, *ws, x2d, g.reshape(1, d), b.reshape(1, d))


def _ffn_up_kernel(x_ref, w_ref, cw_ref, hist_ref, act_ref, newhist_ref, ext_ref, *,
                   ns, tt, tiles_per_seq, dff, cc):
    i = pl.program_id(0)
    tm = ns * tt
    hw = CONV_FF - 1
    base = SUBLANES
    if tiles_per_seq == 1:
        ext_ref[:, base - hw:base, :] = hist_ref[...]
    else:
        @pl.when(i % tiles_per_seq == 0)
        def _():
            ext_ref[:, base - hw:base, :] = hist_ref[...]

        @pl.when(i % tiles_per_seq != 0)
        def _():
            ext_ref[:, base - hw:base, :] = ext_ref[:, base + tt - hw:base + tt, :]
    xb = x_ref[...].astype(BF16)
    for j in range(2 * dff // cc):
        cols = slice(j * cc, (j + 1) * cc)
        h = jnp.dot(xb, w_ref[:, cols], preferred_element_type=F32)
        ext_ref[:, base:base + tt, cols] = h.reshape(ns, tt, cc)
    newhist_ref[...] = ext_ref[:, base + tt - hw:base + tt, :]

    def conv(cols):
        acc = None
        for k in range(CONV_FF):
            term = ext_ref[:, base - hw + k:base - hw + k + tt, cols] * cw_ref[k:k + 1, cols]
            acc = term if acc is None else acc + term
        return acc

    for j in range(dff // cc):
        g = conv(slice(j * cc, (j + 1) * cc))
        u = conv(slice(dff + j * cc, dff + (j + 1) * cc))
        act = g * _sigmoid(g) * u
        act_ref[:, j * cc:(j + 1) * cc] = act.reshape(tm, cc).astype(BF16)


def _ffn_up(x2d, w_up, conv_w, hist, ns, tt, tiles_per_seq):
    n, d = x2d.shape
    c2 = w_up.shape[1]
    dff = c2 // 2
    tm = ns * tt
    cc = 256
    hw = CONV_FF - 1
    if tiles_per_seq == 1:
        hist_map = lambda i: (i, 0, 0)
    else:
        hist_map = lambda i: (i // tiles_per_seq, 0, 0)
    return pl.pallas_call(
        functools.partial(_ffn_up_kernel, ns=ns, tt=tt, tiles_per_seq=tiles_per_seq, dff=dff, cc=cc),
        grid=(n // tm,),
        in_specs=[pl.BlockSpec((tm, d), lambda i: (i, 0)),
                  pl.BlockSpec((d, c2), lambda i: (0, 0)),
                  pl.BlockSpec((CONV_FF, c2), lambda i: (0, 0)),
                  pl.BlockSpec((ns, hw, c2), hist_map)],
        out_specs=[pl.BlockSpec((tm, dff), lambda i: (i, 0)),
                   pl.BlockSpec((ns, hw, c2), hist_map)],
        out_shape=[jax.ShapeDtypeStruct((n, dff), BF16),
                   jax.ShapeDtypeStruct(hist.shape, F32)],
        scratch_shapes=[pltpu.VMEM((ns, SUBLANES + tt, c2), F32)],
        compiler_params=_params("arbitrary"),
        name="ffn_up_conv_gate",
    )(x2d, w_up, conv_w, hist)


def _sortable(x):
    b = lax.bitcast_convert_type(x, jnp.int32)
    return b ^ ((b >> 31) & jnp.int32(0x7FFFFFFF))


def _count(mask):
    return jnp.sum(jnp.where(mask, 1.0, 0.0), axis=-1, keepdims=True)


def _dsa_sample_kernel(qa_ref, qi_ref, qm_ref, kf_ref, vf_ref, kif_ref, kn_ref, vn_ref, kin_ref, nbias_ref,
                       fbias_ref, o_ref, self_ref, seln_ref, *, topk, tq, nbs, lf, ln, wi_col):
    start = lf - T5_FAR
    streams = range(nbs)
    rows = [slice(b * tq, (b + 1) * tq) for b in streams]
    dot = functools.partial(jnp.dot, preferred_element_type=F32)
    wi = [qm_ref[rows[b], wi_col:wi_col + N_IDX_HEADS] * (N_IDX_HEADS ** -0.5) * (D_IDX ** -0.5) for b in streams]
    kif = [kif_ref[0, b].astype(BF16) for b in streams]
    kinb = [kin_ref[b].astype(BF16) for b in streams]

    sc_f = [jnp.zeros((tq, lf), F32) for _ in streams]
    sc_n = [jnp.zeros((tq, ln), F32) for _ in streams]
    for n in range(N_IDX_HEADS):
        qn = [qi_ref[rows[b], n * D_IDX:(n + 1) * D_IDX].astype(BF16) for b in streams]
        sc_f = [sc_f[b] + jnp.maximum(dot(qn[b], kif[b]), 0.0) * wi[b][:, n:n + 1] for b in streams]
        sc_n = [sc_n[b] + jnp.maximum(dot(qn[b], kinb[b]), 0.0) * wi[b][:, n:n + 1] for b in streams]
    adm_f = _iota((tq, lf), 1) < start
    adm_n = nbias_ref[0, 0] > 0.5 * NEG
    key_f = [jnp.where(adm_f, _sortable(x), jnp.int32(INT_MIN)) for x in sc_f]
    key_n = [jnp.where(adm_n, _sortable(x), jnp.int32(INT_MIN)) for x in sc_n]

    kf32 = float(topk)

    def body(i, t_us):
        bit = lax.shift_left(jnp.int32(1), 31 - i)
        cand_u = [t | bit for t in t_us]
        cand_s = [c ^ jnp.int32(INT_MIN) for c in cand_u]
        cnt = [_count(key_f[b] >= cand_s[b]) + _count(key_n[b] >= cand_s[b]) for b in streams]
        return tuple(jnp.where(cnt[b] >= kf32, cand_u[b], t_us[b]) for b in streams)

    t_us = lax.fori_loop(0, 32, body, tuple(jnp.zeros((tq, 1), jnp.int32) for _ in streams))
    thr = [t ^ jnp.int32(INT_MIN) for t in t_us]
    need = [kf32 - (_count(key_f[b] > thr[b]) + _count(key_n[b] > thr[b])) for b in streams]
    n_eq = [_count(key_f[b] == thr[b]) + _count(key_n[b] == thr[b]) for b in streams]
    open_row = [t == jnp.int32(INT_MIN) for t in thr]
    for b in streams:
        self_ref[b] = jnp.where(jnp.logical_and(key_f[b] >= thr[b], adm_f), 0.0, NEG)
        seln_ref[b] = jnp.where(jnp.logical_and(key_n[b] >= thr[b], adm_n), 0.0, NEG)

    def resolve_ties(b):
        upper = jnp.where(_iota((LANES, LANES), 0) < _iota((LANES, LANES), 1), 1.0, 0.0).astype(BF16)
        offset = jnp.zeros((tq, 1), F32)
        for ref, key, width in ((self_ref, key_f[b], lf), (seln_ref, key_n[b], ln)):
            for j0 in range(0, width, LANES):
                w = min(LANES, width - j0)
                kb = key[:, j0:j0 + w]
                e = jnp.where(kb == thr[b], 1.0, 0.0)
                rank = offset + dot(e.astype(BF16), upper[:w, :w])
                take = jnp.where(kb > thr[b], 1.0, jnp.where(rank < need[b], e, 0.0))
                take = jnp.where(open_row[b], jnp.where(kb > thr[b], 1.0, 0.0), take)
                ref[b, :, j0:j0 + w] = jnp.where(take > 0.5, 0.0, NEG)
                offset = offset + jnp.sum(e, axis=-1, keepdims=True)

    for b in streams:
        conflict = jnp.logical_and(n_eq[b] != need[b], jnp.logical_not(open_row[b]))
        pl.when(jnp.max(jnp.where(conflict, 1.0, 0.0)) > 0.0)(functools.partial(resolve_ties, b))

    sel_f = [self_ref[b] for b in streams]
    sel_n = [seln_ref[b] for b in streams]
    for h in range(N_HEADS):
        hs = slice(h * HEAD_DIM, (h + 1) * HEAD_DIM)
        qh = [qa_ref[rows[b], hs].astype(BF16) for b in streams]
        s_f = [dot(qh[b], kf_ref[0, b, h].astype(BF16)) * (HEAD_DIM ** -0.5) + fbias_ref[:, h:h + 1] + sel_f[b]
               for b in streams]
        s_n = [dot(qh[b], kn_ref[b, h].astype(BF16)) * (HEAD_DIM ** -0.5) + nbias_ref[0, h] + sel_n[b]
               for b in streams]
        m = [jnp.maximum(jnp.max(x, axis=-1, keepdims=True), jnp.max(y, axis=-1, keepdims=True))
             for x, y in zip(s_f, s_n)]
        p_f = [jnp.exp(x - mm) for x, mm in zip(s_f, m)]
        p_n = [jnp.exp(x - mm) for x, mm in zip(s_n, m)]
        den = [jnp.sum(x, axis=-1, keepdims=True) + jnp.sum(y, axis=-1, keepdims=True) for x, y in zip(p_f, p_n)]
        o = [_dot_nt(p_f[b].astype(BF16), vf_ref[0, b, h].astype(BF16))
             + _dot_nt(p_n[b].astype(BF16), vn_ref[b, h].astype(BF16)) for b in streams]
        for b in streams:
            o_ref[rows[b], hs] = o[b] / den[b]


def _dsa_sample(qa, qi, qmisc, kf, vf, kif, layer, near, nbias, fbias, *, nb, tq, topk, wi_col):
    lf = kf.shape[-1]
    ln = nbias.shape[-1]
    nbs = SAMPLE_STREAMS_PER_STEP if nb % SAMPLE_STREAMS_PER_STEP == 0 else 1
    row = lambda b: (b, 0)
    args = [qa, qi, qmisc, kf, vf, kif, *near]
    in_specs = ([pl.BlockSpec((nbs * tq, a.shape[1]), row) for a in args[:3]]
                + [pl.BlockSpec((1, nbs) + a.shape[2:], lambda b, nd=a.ndim: (layer, b) + (0,) * (nd - 2))
                   for a in args[3:6]]
                + [pl.BlockSpec((nbs,) + a.shape[1:], lambda b, nd=a.ndim: (b,) + (0,) * (nd - 1)) for a in args[6:]]
                + [pl.BlockSpec(nbias.shape, lambda b: (0, 0, 0, 0)),
                   pl.BlockSpec(fbias.shape, lambda b: (0, 0))])
    return pl.pallas_call(
        functools.partial(_dsa_sample_kernel, topk=topk, tq=tq, nbs=nbs, lf=lf, ln=ln, wi_col=wi_col),
        grid=(nb // nbs,),
        in_specs=in_specs,
        out_specs=pl.BlockSpec((nbs * tq, qa.shape[1]), row),
        out_shape=jax.ShapeDtypeStruct(qa.shape, F32),
        scratch_shapes=[pltpu.VMEM((nbs, tq, lf), F32), pltpu.VMEM((nbs, tq, ln), F32)],
        compiler_params=_params("parallel"),
        name="dsa_attention_sample",
    )(*args, nbias, fbias)


def _reduce_rows(x, op, final):
    blk = 8 * SUBLANES
    parts = [x[r0:r0 + blk] for r0 in range(0, x.shape[0], blk)]
    while len(parts) > 1:
        parts = [op(parts[i], parts[i + 1]) for i in range(0, len(parts) - 1, 2)] + parts[len(parts) & ~1:]
    return final(parts[0], axis=0, keepdims=True)


def _dsa_prompt_tile(g, lk, qa_ref, qi_ref, qm_ref, corr_ref, o_ref,
                     kb_ref, vt_ref, kib_ref, key_ref, sel_ref, s_ref, ot_ref, *, topk, tq, wi_col):
    rb = 2 * LANES if lk % (2 * LANES) == 0 else LANES
    nw = corr_ref.shape[2]
    qa = (qa_ref[...] * (HEAD_DIM ** -0.5)).astype(BF16)
    qi = qi_ref[...].astype(BF16)
    wi = qm_ref[:, wi_col:wi_col + N_IDX_HEADS] * (N_IDX_HEADS ** -0.5) * (D_IDX ** -0.5)
    eye_h = jnp.where(_iota((N_IDX_HEADS, N_IDX_HEADS), 0) == _iota((N_IDX_HEADS, N_IDX_HEADS), 1), 1.0, 0.0)
    wi_t = _dot_nt(eye_h, wi, HI)
    q_chunk = (g * tq + _iota((1, tq), 1)) // CHUNK

    for r0 in range(0, lk, rb):
        kib = kib_ref[r0:r0 + rb, :D_IDX]
        acc = jnp.zeros((rb, tq), F32)
        for n in range(N_IDX_HEADS):
            acc = acc + jnp.maximum(_dot_nt(kib, qi[:, n * D_IDX:(n + 1) * D_IDX]), 0.0) * wi_t[n:n + 1, :]
        adm = (r0 + _iota((rb, 1), 0)) // CHUNK <= q_chunk
        key_ref[r0:r0 + rb, :] = jnp.where(adm, _sortable(acc), jnp.int32(INT_MIN))

    kf32 = float(topk)
    count = lambda mask: _reduce_rows(jnp.where(mask, 1.0, 0.0), jnp.add, jnp.sum)

    def body(i, t_u):
        cand_u = t_u | lax.shift_left(jnp.int32(1), 31 - i)
        cand_s = cand_u ^ jnp.int32(INT_MIN)
        return jnp.where(count(key_ref[0:lk, :] >= cand_s) >= kf32, cand_u, t_u)

    t_u = lax.fori_loop(0, 32, body, jnp.zeros((1, tq), jnp.int32))
    thr = t_u ^ jnp.int32(INT_MIN)
    keys = key_ref[0:lk, :]
    need = kf32 - count(keys > thr)
    open_row = thr == jnp.int32(INT_MIN)
    conflict = jnp.logical_and(count(keys == thr) != need, jnp.logical_not(open_row))
    sel_ref[0:lk, :] = jnp.where(keys >= jnp.maximum(thr, jnp.int32(INT_MIN + 1)), 0.0, NEG)

    @pl.when(jnp.max(jnp.where(conflict, 1.0, 0.0)) > 0.0)
    def _():
        below = jnp.where(_iota((LANES, LANES), 1) < _iota((LANES, LANES), 0), 1.0, 0.0).astype(BF16)
        offset = jnp.zeros((1, tq), F32)
        for r0 in range(0, lk, LANES):
            kblk = key_ref[r0:r0 + LANES, :]
            e = jnp.where(kblk == thr, 1.0, 0.0)
            rank = offset + jnp.dot(below, e.astype(BF16), preferred_element_type=F32)
            take = jnp.where(kblk > thr, 1.0, jnp.where(jnp.logical_or(rank >= need, open_row), 0.0, e))
            sel_ref[r0:r0 + LANES, :] = jnp.where(take > 0.5, 0.0, NEG)
            offset = offset + jnp.sum(e, axis=0, keepdims=True)

    v = jnp.minimum(g, 1)
    start = pl.multiple_of(jnp.maximum(g * tq - T5_FAR, 0), LANES)
    for h in range(N_HEADS):
        hs = slice(h * HEAD_DIM, (h + 1) * HEAD_DIM)
        s_ref[0:lk, :] = _dot_nt(kb_ref[0:lk, hs], qa[:, hs]) + sel_ref[0:lk, :]
        s_ref[pl.ds(start, nw), :] += corr_ref[v, h]
        s = s_ref[0:lk, :]
        p = jnp.exp(s - _reduce_rows(s, jnp.maximum, jnp.max))
        pn = (p * (1.0 / _reduce_rows(p, jnp.add, jnp.sum))).astype(BF16)
        ot_ref[hs, :] = jnp.dot(vt_ref[hs, 0:lk], pn, preferred_element_type=F32)
    o_ref[...] = ot_ref[...].T


def _dsa_prompt_kernel(qa_ref, qi_ref, qm_ref, k_ref, v_ref, ki_ref, corr_ref, o_ref,
                       kb_ref, vt_ref, kib_ref, key_ref, sel_ref, s_ref, ot_ref, *, topk, tq, seq, n_groups, wi_col):
    g = pl.program_id(1)
    blk = _row_tile(seq, 4 * LANES)

    @pl.when(g == 0)
    def _():
        for r0 in range(0, seq, blk):
            kb_ref[r0:r0 + blk, :] = k_ref[0, r0:r0 + blk, :].astype(BF16)
            vt_ref[:, r0:r0 + blk] = v_ref[0, r0:r0 + blk, :].T.astype(BF16)
            kib_ref[r0:r0 + blk, :] = ki_ref[0, r0:r0 + blk, :].astype(BF16)

    per_group = (seq // tq) // n_groups
    for grp in range(n_groups):
        @pl.when(g // per_group == grp)
        def _():
            _dsa_prompt_tile(g, (grp + 1) * per_group * tq, qa_ref, qi_ref, qm_ref, corr_ref, o_ref,
                             kb_ref, vt_ref, kib_ref, key_ref, sel_ref, s_ref, ot_ref,
                             topk=topk, tq=tq, wi_col=wi_col)


def _dsa_prompt(qa, qi, qmisc, k, v, ki, corr, *, nb, tq, topk, wi_col, n_groups):
    n, w = qa.shape
    seq = k.shape[1]
    nq = seq // tq
    row = lambda b, g: (b * nq + g, 0)
    per_b = lambda b, g: (b, 0, 0)
    return pl.pallas_call(
        functools.partial(_dsa_prompt_kernel, topk=topk, tq=tq, seq=seq, n_groups=n_groups, wi_col=wi_col),
        grid=(nb, nq),
        in_specs=[pl.BlockSpec((tq, w), row),
                  pl.BlockSpec((tq, qi.shape[1]), row),
                  pl.BlockSpec((tq, qmisc.shape[1]), row),
                  pl.BlockSpec((1, seq, w), per_b),
                  pl.BlockSpec((1, seq, w), per_b),
                  pl.BlockSpec((1, seq, ki.shape[2]), per_b),
                  pl.BlockSpec(corr.shape, lambda b, g: (0, 0, 0, 0))],
        out_specs=pl.BlockSpec((tq, w), row),
        out_shape=jax.ShapeDtypeStruct(qa.shape, F32),
        scratch_shapes=[pltpu.VMEM((seq, w), BF16),
                        pltpu.VMEM((w, seq), BF16),
                        pltpu.VMEM((seq, ki.shape[2]), BF16),
                        pltpu.VMEM((seq, tq), jnp.int32),
                        pltpu.VMEM((seq, tq), F32),
                        pltpu.VMEM((seq, tq), F32),
                        pltpu.VMEM((w, tq), F32)],
        compiler_params=_params("parallel", "arbitrary"),
        name="dsa_attention_prompt",
    )(qa, qi, qmisc, k, v, ki, corr)


def _block_rows(x, op):
    blk = 8 * SUBLANES
    parts = [x[r0:r0 + blk] for r0 in range(0, x.shape[0], blk)]
    while len(parts) > 1:
        parts = [op(parts[i], parts[i + 1]) for i in range(0, len(parts) - 1, 2)] + parts[len(parts) & ~1:]
    return parts[0]


def _dsa_blocks_kernel(qa_ref, qi_ref, qm_ref, k_ref, v_ref, ki_ref, corr_ref, o_ref,
                       kb_ref, vt_ref, kib_ref, key_ref, sel_ref, thr_ref, stat_ref, m_ref, l_ref, ot_ref, *,
                       topk, tq, seq, wi_col):
    g = pl.program_id(1)
    kb = 2 * LANES
    w = qa_ref.shape[1]
    blk = _row_tile(seq, 4 * LANES)

    @pl.when(g == 0)
    def _():
        for r0 in range(0, seq, blk):
            kb_ref[r0:r0 + blk, :] = k_ref[0, r0:r0 + blk, :].astype(BF16)
            kib_ref[r0:r0 + blk, :] = ki_ref[0, r0:r0 + blk, :].astype(BF16)
        for j in range(seq // LANES):
            vt_ref[j] = v_ref[0, j * LANES:(j + 1) * LANES, :].T.astype(BF16)

    top = (g + 1) * tq
    nkb = (top + kb - 1) // kb
    start_of = lambda i: pl.multiple_of(jnp.maximum(top - kb * (i + 1), 0), LANES)

    qa = (qa_ref[...] * (HEAD_DIM ** -0.5)).astype(BF16)
    qi = qi_ref[...].astype(BF16)
    wi = qm_ref[:, wi_col:wi_col + N_IDX_HEADS] * (N_IDX_HEADS ** -0.5) * (D_IDX ** -0.5)
    eye_h = jnp.where(_iota((N_IDX_HEADS, N_IDX_HEADS), 0) == _iota((N_IDX_HEADS, N_IDX_HEADS), 1), 1.0, 0.0)
    wi_t = _dot_nt(eye_h, wi, HI)
    q_chunk = (g * tq + _iota((1, tq), 1)) // CHUNK

    def score_block(i, carry):
        st = start_of(i)
        kib = kib_ref[pl.ds(st, kb), :][:, :D_IDX]
        acc = jnp.zeros((kb, tq), F32)
        for n in range(N_IDX_HEADS):
            acc = acc + jnp.maximum(_dot_nt(kib, qi[:, n * D_IDX:(n + 1) * D_IDX]), 0.0) * wi_t[n:n + 1, :]
        row = st + _iota((kb, 1), 0)
        ok = jnp.logical_and(row // CHUNK <= q_chunk, row < top - kb * i)
        key_ref[i] = jnp.where(ok, _sortable(acc), jnp.int32(INT_MIN))
        return carry

    lax.fori_loop(0, nkb, score_block, 0)

    kf32 = float(topk)

    def search_blocks(n_blocks):
        def count(pred):
            parts = [_block_rows(jnp.where(pred(key_ref[i]), 1.0, 0.0), jnp.add) for i in range(n_blocks)]
            while len(parts) > 1:
                parts = [a + b for a, b in zip(parts[0::2], parts[1::2])] + parts[len(parts) & ~1:]
            return jnp.sum(parts[0], axis=0, keepdims=True)

        def search(b, t_u):
            cand_u = t_u | lax.shift_left(jnp.int32(1), 31 - b)
            cand_s = cand_u ^ jnp.int32(INT_MIN)
            return jnp.where(count(lambda x: x >= cand_s) >= kf32, cand_u, t_u)

        t_u = lax.fori_loop(0, 32, search, jnp.zeros((1, tq), jnp.int32))
        thr = t_u ^ jnp.int32(INT_MIN)
        thr_ref[0:1, :] = thr
        stat_ref[0:1, :] = kf32 - count(lambda x: x > thr)
        stat_ref[1:2, :] = count(lambda x: x == thr)

    for n_blocks in range(1, seq // kb + 1):
        pl.when(nkb == n_blocks)(functools.partial(search_blocks, n_blocks))

    thr = thr_ref[0:1, :]
    need = stat_ref[0:1, :]
    open_row = thr == jnp.int32(INT_MIN)
    conflict = jnp.logical_and(stat_ref[1:2, :] != need, jnp.logical_not(open_row))
    floor = jnp.maximum(thr, jnp.int32(INT_MIN + 1))

    def select_block(i, carry):
        sel_ref[i] = jnp.where(key_ref[i] >= floor, 0.0, NEG)
        return carry

    lax.fori_loop(0, nkb, select_block, 0)

    @pl.when(jnp.max(jnp.where(conflict, 1.0, 0.0)) > 0.0)
    def _():
        below = jnp.where(_iota((LANES, LANES), 1) < _iota((LANES, LANES), 0), 1.0, 0.0).astype(BF16)

        def tie_block(ii, offset):
            i = nkb - 1 - ii
            for r0 in range(0, kb, LANES):
                kblk = key_ref[i, r0:r0 + LANES, :]
                e = jnp.where(kblk == thr, 1.0, 0.0)
                rank = offset + jnp.dot(below, e.astype(BF16), preferred_element_type=F32)
                take = jnp.where(kblk > thr, 1.0, jnp.where(jnp.logical_or(rank >= need, open_row), 0.0, e))
                sel_ref[i, r0:r0 + LANES, :] = jnp.where(take > 0.5, 0.0, NEG)
                offset = offset + jnp.sum(e, axis=0, keepdims=True)
            return offset

        lax.fori_loop(0, nkb, tie_block, jnp.zeros((1, tq), F32))

    heads = range(N_HEADS)
    hs = [slice(h * HEAD_DIM, (h + 1) * HEAD_DIM) for h in heads]

    def attend_block(i, first, near=None):
        st = start_of(i)
        vblk = st // LANES
        sel = sel_ref[i]
        s = [_dot_nt(kb_ref[pl.ds(st, kb), hs[h]], qa[:, hs[h]]) + sel for h in heads]
        if near is not None:
            s = [s[h] + corr_ref[near, h] for h in heads]
        m_blk = [jnp.max(_block_rows(x, jnp.maximum), axis=0, keepdims=True) for x in s]
        m_old = [m_ref[h:h + 1, :] for h in heads]
        m_new = m_blk if first else [jnp.maximum(a, b) for a, b in zip(m_old, m_blk)]
        p = [jnp.exp(x - m) for x, m in zip(s, m_new)]
        l_blk = [jnp.sum(_block_rows(x, jnp.add), axis=0, keepdims=True) for x in p]
        pb = [x.astype(BF16) for x in p]
        pv = [sum(jnp.dot(vt_ref[vblk + j, hs[h], :], pb[h][j * LANES:(j + 1) * LANES, :],
                          preferred_element_type=F32) for j in range(kb // LANES)) for h in heads]
        for h in heads:
            if first:
                l_ref[h:h + 1, :] = l_blk[h]
                ot_ref[hs[h], :] = pv[h]
            else:
                alpha = jnp.exp(m_old[h] - m_new[h])
                l_ref[h:h + 1, :] = l_ref[h:h + 1, :] * alpha + l_blk[h]
                ot_ref[hs[h], :] = ot_ref[hs[h], :] * alpha + pv[h]
            m_ref[h:h + 1, :] = m_new[h]

    attend_block(0, True, 0)
    pl.when(nkb > 1)(lambda: attend_block(1, False, 1))

    def attend_rest(i, carry):
        attend_block(i, False)
        return carry

    lax.fori_loop(2, nkb, attend_rest, 0)
    for h in heads:
        ot_ref[hs[h], :] = ot_ref[hs[h], :] * (1.0 / l_ref[h:h + 1, :])
    o_ref[...] = ot_ref[...].T


def _dsa_blocks(qa, qi, qmisc, k, v, ki, corr, *, nb, tq, topk, wi_col):
    n, w = qa.shape
    seq = k.shape[1]
    nq = seq // tq
    kb = 2 * LANES
    row = lambda b, g: (b * nq + g, 0)
    per_b = lambda b, g: (b, 0, 0)
    return pl.pallas_call(
        functools.partial(_dsa_blocks_kernel, topk=topk, tq=tq, seq=seq, wi_col=wi_col),
        grid=(nb, nq),
        in_specs=[pl.BlockSpec((tq, w), row),
                  pl.BlockSpec((tq, qi.shape[1]), row),
                  pl.BlockSpec((tq, qmisc.shape[1]), row),
                  pl.BlockSpec((1, seq, w), per_b),
                  pl.BlockSpec((1, seq, w), per_b),
                  pl.BlockSpec((1, seq, ki.shape[2]), per_b),
                  pl.BlockSpec(corr.shape, lambda b, g: (0, 0, 0, 0))],
        out_specs=pl.BlockSpec((tq, w), row),
        out_shape=jax.ShapeDtypeStruct(qa.shape, F32),
        scratch_shapes=[pltpu.VMEM((seq, w), BF16),
                        pltpu.VMEM((seq // LANES, w, LANES), BF16),
                        pltpu.VMEM((seq, ki.shape[2]), BF16),
                        pltpu.VMEM((seq // kb, kb, tq), jnp.int32),
                        pltpu.VMEM((seq // kb, kb, tq), F32),
                        pltpu.VMEM((SUBLANES, tq), jnp.int32),
                        pltpu.VMEM((SUBLANES, tq), F32),
                        pltpu.VMEM((N_HEADS, tq), F32),
                        pltpu.VMEM((N_HEADS, tq), F32),
                        pltpu.VMEM((w, tq), F32)],
        compiler_params=_params("parallel", "arbitrary"),
        name="dsa_attention_prompt",
    )(qa, qi, qmisc, k, v, ki, corr)


def _unit_lower_inverse(mats, n):
    row, col = _iota((n, n), 0), _iota((n, n), 1)
    same = lambda width: row // width == col // width
    dot = functools.partial(jnp.dot, preferred_element_type=F32)
    eye = jnp.where(row == col, 1.0, 0.0)
    diag = [jnp.where(same(INV_BLOCK), a, 0.0) for a in mats]
    ts = [eye - d for d in diag]
    dks = [_split(d) for d in diag]
    k = 1
    while 2 * k < INV_BLOCK:
        dks = [_split(_dot3(dk, dk)) for dk in dks]
        ts = [t + _dot3(_split(t), dk) for t, dk in zip(ts, dks)]
        k *= 2
    width = INV_BLOCK
    while width < n:
        off = jnp.logical_and(same(2 * width), jnp.logical_not(same(width)))
        ls = [jnp.where(off, a, 0.0).astype(BF16) for a in mats]
        tb = [t.astype(BF16) for t in ts]
        tl = [dot(t, l).astype(BF16) for t, l in zip(tb, ls)]
        ts = [t - dot(x, y) for t, x, y in zip(ts, tl, tb)]
        width *= 2
    return ts


def _head_rms(x, w):
    return x * lax.rsqrt(jnp.mean(x * x, axis=-1, keepdims=True) + 1e-6) * w


def _gdn_kernel(qkv_ref, misc_ref, z_ref, hist_ref, s0_ref, cw_ref, alog_ref, dtb_ref, nw_ref,
                y_ref, sfin_ref, newhist_ref, ext_ref, s_ref, *, c, a_col, b_col):
    j = pl.program_id(1)
    hw = CONV_B - 1
    base = SUBLANES
    wq = N_HEADS * DK_B

    @pl.when(j == 0)
    def _():
        ext_ref[base - hw:base, :] = hist_ref[0]
        s_ref[...] = s0_ref[0]

    @pl.when(j > 0)
    def _():
        ext_ref[base - hw:base, :] = ext_ref[base + c - hw:base + c, :]

    ext_ref[base:base + c, :] = qkv_ref[...]
    newhist_ref[0] = ext_ref[base + c - hw:base + c, :]
    conv = None
    for k in range(CONV_B):
        term = ext_ref[base - hw + k:base - hw + k + c, :] * cw_ref[k:k + 1, :]
        conv = term if conv is None else conv + term
    act = conv * _sigmoid(conv)

    beta = _sigmoid(misc_ref[:, b_col:b_col + N_HEADS])
    g = -jnp.exp(alog_ref[...]) * _softplus(misc_ref[:, a_col:a_col + N_HEADS] + dtb_ref[...])
    ri = _iota((c, c), 0)
    ci = _iota((c, c), 1)
    lower = ri >= ci
    strict = ri > ci
    gc = _dot_hi(jnp.where(lower, 1.0, 0.0), g)
    eye_h = jnp.where(_iota((N_HEADS, N_HEADS), 0) == _iota((N_HEADS, N_HEADS), 1), 1.0, 0.0)
    gc_t = _dot_nt(eye_h, gc, HI)
    eg = jnp.exp(gc)
    g_last = gc[c - 1:c, :]
    e_last = jnp.exp(g_last)
    e_rest = jnp.exp(g_last - gc)
    z = z_ref[...]
    heads = range(N_HEADS)
    col = lambda x, h: x[:, h:h + 1]
    qs = [act[:, h * DK_B:(h + 1) * DK_B] for h in heads]
    ks = [act[:, wq + h * DK_B:wq + (h + 1) * DK_B] for h in heads]
    vs = [act[:, 2 * wq + h * DV_B:2 * wq + (h + 1) * DV_B] for h in heads]
    qs = [q * lax.rsqrt(jnp.sum(q * q, axis=-1, keepdims=True) + 1e-6) * (DK_B ** -0.5) for q in qs]
    ks = [k * lax.rsqrt(jnp.sum(k * k, axis=-1, keepdims=True) + 1e-6) for k in ks]
    kbs = [k.astype(BF16) for k in ks]
    decay = [jnp.where(lower, jnp.exp(jnp.where(lower, col(gc, h) - gc_t[h:h + 1, :], 0.0)), 0.0) for h in heads]
    kk = [_dot_nt(kbs[h], kbs[h]) for h in heads]
    attn = [_dot_nt(qs[h].astype(BF16), kbs[h]) * decay[h] for h in heads]
    a_mat = [jnp.where(strict, col(beta, h) * kk[h] * decay[h], 0.0) for h in heads]
    t_mat = [t.astype(BF16) for t in _unit_lower_inverse(a_mat, c)]
    value = [_dot(t_mat[h], vs[h] * col(beta, h)) for h in heads]
    k_cum = [_dot(t_mat[h], ks[h] * (col(beta, h) * col(eg, h))) for h in heads]
    s_old = [s_ref[h] for h in heads]
    sbs = [s.astype(BF16) for s in s_old]
    v_new = [(value[h] - _dot(k_cum[h], sbs[h])).astype(BF16) for h in heads]
    o_inter = [_dot(qs[h] * col(eg, h), sbs[h]) for h in heads]
    o = [o_inter[h] + _dot(attn[h], v_new[h]) for h in heads]
    for h in heads:
        s_ref[h] = s_old[h] * col(e_last, h) + _dot_tn((ks[h] * col(e_rest, h)).astype(BF16), v_new[h])
    for h in heads:
        zh = z[:, h * DV_B:(h + 1) * DV_B]
        y_ref[:, h * DV_B:(h + 1) * DV_B] = _head_rms(o[h], nw_ref[...]) * (zh * _sigmoid(zh))
    sfin_ref[0] = s_ref[...]


def _gdn_multi_kernel(qkv_ref, misc_ref, z_ref, hist_ref, s0_ref, cw_ref, alog_ref, dtb_ref, nw_ref,
                      y_ref, sfin_ref, newhist_ref, ext_ref, s_ref, *, c, a_col, b_col):
    j = pl.program_id(1)
    hw = CONV_B - 1
    base = SUBLANES
    wq = N_HEADS * DK_B
    seqs = range(qkv_ref.shape[0])

    @pl.when(j == 0)
    def _():
        ext_ref[:, base - hw:base, :] = hist_ref[...]
        s_ref[...] = s0_ref[...]

    @pl.when(j > 0)
    def _():
        ext_ref[:, base - hw:base, :] = ext_ref[:, base + c - hw:base + c, :]

    ext_ref[:, base:base + c, :] = qkv_ref[...]
    newhist_ref[...] = ext_ref[:, base + c - hw:base + c, :]
    ri = _iota((c, c), 0)
    ci = _iota((c, c), 1)
    lower = ri >= ci
    strict = ri > ci
    tri = jnp.where(lower, 1.0, 0.0)
    eye_h = jnp.where(_iota((N_HEADS, N_HEADS), 0) == _iota((N_HEADS, N_HEADS), 1), 1.0, 0.0)

    def conv_act(b):
        conv = None
        for k in range(CONV_B):
            term = ext_ref[b, base - hw + k:base - hw + k + c, :] * cw_ref[k:k + 1, :]
            conv = term if conv is None else conv + term
        return conv * _sigmoid(conv)

    act = [conv_act(b) for b in seqs]
    beta = [_sigmoid(misc_ref[b, :, b_col:b_col + N_HEADS]) for b in seqs]
    g = [-jnp.exp(alog_ref[...]) * _softplus(misc_ref[b, :, a_col:a_col + N_HEADS] + dtb_ref[...]) for b in seqs]
    gc = [_dot_hi(tri, x) for x in g]
    gc_t = [_dot_nt(eye_h, x, HI) for x in gc]
    eg = [jnp.exp(x) for x in gc]
    e_last = [jnp.exp(x[c - 1:c, :]) for x in gc]
    e_rest = [jnp.exp(x[c - 1:c, :] - x) for x in gc]

    items = [(b, h) for b in seqs for h in range(N_HEADS)]
    col = lambda x, h: x[:, h:h + 1]
    qs = [act[b][:, h * DK_B:(h + 1) * DK_B] for b, h in items]
    ks = [act[b][:, wq + h * DK_B:wq + (h + 1) * DK_B] for b, h in items]
    vs = [act[b][:, 2 * wq + h * DV_B:2 * wq + (h + 1) * DV_B] for b, h in items]
    qs = [q * lax.rsqrt(jnp.sum(q * q, axis=-1, keepdims=True) + 1e-6) * (DK_B ** -0.5) for q in qs]
    ks = [k * lax.rsqrt(jnp.sum(k * k, axis=-1, keepdims=True) + 1e-6) for k in ks]
    kbs = [k.astype(BF16) for k in ks]
    decay = [jnp.where(lower, jnp.exp(jnp.where(lower, col(gc[b], h) - gc_t[b][h:h + 1, :], 0.0)), 0.0)
             for b, h in items]
    kk = [_dot_nt(k, k) for k in kbs]
    attn = [_dot_nt(q.astype(BF16), k) * d for q, k, d in zip(qs, kbs, decay)]
    a_mat = [jnp.where(strict, col(beta[b], h) * x * d, 0.0) for (b, h), x, d in zip(items, kk, decay)]
    t_mat = [t.astype(BF16) for t in _unit_lower_inverse(a_mat, c)]
    value = [_dot(t, v * col(beta[b], h)) for (b, h), t, v in zip(items, t_mat, vs)]
    k_cum = [_dot(t, k * (col(beta[b], h) * col(eg[b], h))) for (b, h), t, k in zip(items, t_mat, ks)]
    s_old = [s_ref[b, h] for b, h in items]
    sbs = [s.astype(BF16) for s in s_old]
    v_new = [(v - _dot(kc, s)).astype(BF16) for v, kc, s in zip(value, k_cum, sbs)]
    o_inter = [_dot(q * col(eg[b], h), s) for (b, h), q, s in zip(items, qs, sbs)]
    o = [oi + _dot(a, v) for oi, a, v in zip(o_inter, attn, v_new)]
    for i, (b, h) in enumerate(items):
        s_ref[b, h] = s_old[i] * col(e_last[b], h) + _dot_tn((ks[i] * col(e_rest[b], h)).astype(BF16), v_new[i])
    for i, (b, h) in enumerate(items):
        zh = z_ref[b, :, h * DV_B:(h + 1) * DV_B]
        y_ref[b, :, h * DV_B:(h + 1) * DV_B] = _head_rms(o[i], nw_ref[...]) * (zh * _sigmoid(zh))
    sfin_ref[...] = s_ref[...]


def _gdn(qkv, misc, z, hist, s0, conv_w, a_log, dt_bias, norm_w, *, nb, c, a_col, b_col):
    n, wqkv = qkv.shape
    seq = n // nb
    nbb = GDN_SEQS_PER_STEP if nb % GDN_SEQS_PER_STEP == 0 else 1
    tok = lambda b, j: (b, j, 0)
    per_b3 = lambda b, j: (b, 0, 0)
    per_b4 = lambda b, j: (b, 0, 0, 0)
    const2 = lambda b, j: (0, 0)
    hw = CONV_B - 1
    tokens = [a.reshape(nb, seq, a.shape[1]) for a in (qkv, misc, z)]
    y, s_fin, new_hist = pl.pallas_call(
        functools.partial(_gdn_multi_kernel, c=c, a_col=a_col, b_col=b_col),
        grid=(nb // nbb, seq // c),
        in_specs=[pl.BlockSpec((nbb, c, a.shape[2]), tok) for a in tokens]
                 + [pl.BlockSpec((nbb, hw, wqkv), per_b3),
                    pl.BlockSpec((nbb,) + s0.shape[1:], per_b4),
                    pl.BlockSpec((CONV_B, wqkv), const2),
                    pl.BlockSpec((1, N_HEADS), const2),
                    pl.BlockSpec((1, N_HEADS), const2),
                    pl.BlockSpec((1, DV_B), const2)],
        out_specs=[pl.BlockSpec((nbb, c, z.shape[1]), tok),
                   pl.BlockSpec((nbb,) + s0.shape[1:], per_b4),
                   pl.BlockSpec((nbb, hw, wqkv), per_b3)],
        out_shape=[jax.ShapeDtypeStruct((nb, seq, z.shape[1]), F32),
                   jax.ShapeDtypeStruct(s0.shape, F32),
                   jax.ShapeDtypeStruct(hist.shape, F32)],
        scratch_shapes=[pltpu.VMEM((nbb, SUBLANES + c, wqkv), F32),
                        pltpu.VMEM((nbb,) + s0.shape[1:], F32)],
        compiler_params=_params("parallel", "arbitrary"),
        name="gated_deltanet",
    )(*tokens, hist, s0, conv_w, a_log.reshape(1, -1), dt_bias.reshape(1, -1), norm_w.reshape(1, -1))
    return y.reshape(n, z.shape[1]), s_fin, new_hist


def _gdn_single(qkv, misc, z, hist, s0, conv_w, a_log, dt_bias, norm_w, *, nb, c, a_col, b_col):
    n, wqkv = qkv.shape
    nch = n // (nb * c)
    row = lambda b, j: (b * nch + j, 0)
    per_b3 = lambda b, j: (b, 0, 0)
    per_b4 = lambda b, j: (b, 0, 0, 0)
    const2 = lambda b, j: (0, 0)
    hw = CONV_B - 1
    return pl.pallas_call(
        functools.partial(_gdn_kernel, c=c, a_col=a_col, b_col=b_col),
        grid=(nb, nch),
        in_specs=[pl.BlockSpec((c, wqkv), row),
                  pl.BlockSpec((c, misc.shape[1]), row),
                  pl.BlockSpec((c, z.shape[1]), row),
                  pl.BlockSpec((1, hw, wqkv), per_b3),
                  pl.BlockSpec((1,) + s0.shape[1:], per_b4),
                  pl.BlockSpec((CONV_B, wqkv), const2),
                  pl.BlockSpec((1, N_HEADS), const2),
                  pl.BlockSpec((1, N_HEADS), const2),
                  pl.BlockSpec((1, DV_B), const2)],
        out_specs=[pl.BlockSpec((c, z.shape[1]), row),
                   pl.BlockSpec((1,) + s0.shape[1:], per_b4),
                   pl.BlockSpec((1, hw, wqkv), per_b3)],
        out_shape=[jax.ShapeDtypeStruct(z.shape, F32),
                   jax.ShapeDtypeStruct(s0.shape, F32),
                   jax.ShapeDtypeStruct(hist.shape, F32)],
        scratch_shapes=[pltpu.VMEM((SUBLANES + c, wqkv), F32),
                        pltpu.VMEM(s0.shape[1:], F32)],
        compiler_params=_params("parallel", "arbitrary"),
        name="gated_deltanet",
    )(qkv, misc, z, hist, s0, conv_w, a_log.reshape(1, -1), dt_bias.reshape(1, -1), norm_w.reshape(1, -1))


def _mlstm_kernel(qk_ref, v_ref, og_ref, misc_ref, c0_ref, n0_ref, m0_ref, ib_ref, fb_ref, nw_ref,
                  y_ref, cfin_ref, nfin_ref, mfin_ref, c_ref, n_ref, m_ref, *, l, nbb, i_col, f_col):
    j = pl.program_id(1)
    wq = N_HEADS * DK_C

    @pl.when(j == 0)
    def _():
        c_ref[...] = c0_ref[...]
        n_ref[...] = n0_ref[...]
        m_ref[...] = m0_ref[...]

    causal = _iota((l, l), 0) >= _iota((l, l), 1)
    tri = jnp.where(causal, 1.0, 0.0)
    eye_h = jnp.where(_iota((N_HEADS, N_HEADS), 0) == _iota((N_HEADS, N_HEADS), 1), 1.0, 0.0)
    seqs = range(nbb)
    ig = [misc_ref[b, :, i_col:i_col + N_HEADS] + ib_ref[...] for b in seqs]
    lf = [-_softplus(-(misc_ref[b, :, f_col:f_col + N_HEADS] + fb_ref[...])) for b in seqs]
    fc = [_dot_hi(tri, lf[b]) for b in seqs]
    row_terms = [_dot_nt(eye_h, ig[b] - fc[b], HI) for b in seqs]
    m_prev = [m_ref[b] for b in seqs]
    log_inter = [fc[b] + m_prev[b] for b in seqs]
    f_last = [fc[b][l - 1:l, :] for b in seqs]
    m_vec = [f_last[b] + jnp.maximum(m_prev[b], jnp.max(ig[b] - fc[b], axis=0, keepdims=True)) for b in seqs]
    dec_vec = [jnp.exp(f_last[b] + m_prev[b] - m_vec[b]) for b in seqs]
    w_last = [jnp.exp(f_last[b] - fc[b] + ig[b] - m_vec[b]) for b in seqs]

    items = [(b, h) for b in seqs for h in range(N_HEADS)]
    col = lambda x, h: x[:, h:h + 1]
    qs = [qk_ref[b, :, h * DK_C:(h + 1) * DK_C] for b, h in items]
    ks = [qk_ref[b, :, wq + h * DK_C:wq + (h + 1) * DK_C] * (DK_C ** -0.5) for b, h in items]
    vbs = [v_ref[b, :, h * DV_C:(h + 1) * DV_C].astype(BF16) for b, h in items]
    c_old = [c_ref[b, h] for b, h in items]
    n_old = [n_ref[b, h:h + 1, :] for b, h in items]
    qk_raw = [_dot_nt(q.astype(BF16), k.astype(BF16)) for q, k in zip(qs, ks)]
    q_c = [_dot(q, c) for q, c in zip(qs, c_old)]
    log_w = [jnp.where(causal, col(fc[b], h) + row_terms[b][h:h + 1, :], -jnp.inf) for b, h in items]
    li = [col(log_inter[b], h) for b, h in items]
    m_t = [jnp.maximum(a, jnp.max(lw, axis=-1, keepdims=True)) for a, lw in zip(li, log_w)]
    w_inter = [jnp.exp(a - m) for a, m in zip(li, m_t)]
    qkw = [r * jnp.exp(lw - m) for r, lw, m in zip(qk_raw, log_w, m_t)]
    num = [_dot(a, vb) + wi * qc for a, vb, wi, qc in zip(qkw, vbs, w_inter, q_c)]
    den = [jnp.sum(a, axis=-1, keepdims=True) + wi * jnp.sum(q * n, axis=-1, keepdims=True)
           for a, wi, q, n in zip(qkw, w_inter, qs, n_old)]
    hid = [nu / jnp.maximum(jnp.abs(de), jnp.exp(-m)) for nu, de, m in zip(num, den, m_t)]
    kw = [k * col(w_last[b], h) for k, (b, h) in zip(ks, items)]
    for i, (b, h) in enumerate(items):
        c_ref[b, h] = col(dec_vec[b], h) * c_old[i] + _dot_tn(kw[i].astype(BF16), vbs[i])
        n_ref[b, h:h + 1, :] = col(dec_vec[b], h) * n_old[i] + jnp.sum(kw[i], axis=0, keepdims=True)
    for i, (b, h) in enumerate(items):
        oh = og_ref[b, :, h * DV_C:(h + 1) * DV_C]
        y_ref[b, :, h * DV_C:(h + 1) * DV_C] = _head_rms(hid[i], nw_ref[...]) * _sigmoid(oh)
    for b in seqs:
        m_ref[b] = m_vec[b]
    cfin_ref[...] = c_ref[...]
    nfin_ref[...] = n_ref[...]
    mfin_ref[...] = m_ref[...]


def _mlstm(qk, v, og, misc, c0, n0, m0, i_bias, f_bias, norm_w, *, nb, l, i_col, f_col):
    n = qk.shape[0]
    seq = n // nb
    nbb = SEQS_PER_STEP if nb % SEQS_PER_STEP == 0 else 1
    tok = lambda b, j: (b, j, 0)
    per_b3 = lambda b, j: (b, 0, 0)
    per_b4 = lambda b, j: (b, 0, 0, 0)
    const2 = lambda b, j: (0, 0)
    m0 = m0.reshape(nb, 1, N_HEADS)
    tokens = [a.reshape(nb, seq, a.shape[1]) for a in (qk, v, og, misc)]
    outs = pl.pallas_call(
        functools.partial(_mlstm_kernel, l=l, nbb=nbb, i_col=i_col, f_col=f_col),
        grid=(nb // nbb, seq // l),
        in_specs=[pl.BlockSpec((nbb, l, a.shape[2]), tok) for a in tokens]
                 + [pl.BlockSpec((nbb,) + c0.shape[1:], per_b4),
                    pl.BlockSpec((nbb,) + n0.shape[1:], per_b3),
                    pl.BlockSpec((nbb, 1, N_HEADS), per_b3),
                    pl.BlockSpec((1, N_HEADS), const2),
                    pl.BlockSpec((1, N_HEADS), const2),
                    pl.BlockSpec((1, DV_C), const2)],
        out_specs=[pl.BlockSpec((nbb, l, v.shape[1]), tok),
                   pl.BlockSpec((nbb,) + c0.shape[1:], per_b4),
                   pl.BlockSpec((nbb,) + n0.shape[1:], per_b3),
                   pl.BlockSpec((nbb, 1, N_HEADS), per_b3)],
        out_shape=[jax.ShapeDtypeStruct((nb, seq, v.shape[1]), F32),
                   jax.ShapeDtypeStruct(c0.shape, F32),
                   jax.ShapeDtypeStruct(n0.shape, F32),
                   jax.ShapeDtypeStruct(m0.shape, F32)],
        scratch_shapes=[pltpu.VMEM((nbb,) + c0.shape[1:], F32),
                        pltpu.VMEM((nbb,) + n0.shape[1:], F32),
                        pltpu.VMEM((nbb, 1, N_HEADS), F32)],
        compiler_params=_params("parallel", "arbitrary"),
        name="mlstm",
    )(*tokens, c0, n0, m0, i_bias.reshape(1, -1), f_bias.reshape(1, -1), norm_w.reshape(1, -1))
    y, c_fin, n_fin, m_fin = outs
    return y.reshape(n, v.shape[1]), c_fin, n_fin, m_fin.reshape(nb, N_HEADS)


def _block_mask(rows, row_group, cols, col_group):
    return (np.arange(rows)[:, None] // row_group == np.arange(cols)[None, :] // col_group).astype(np.float32)


def _cummax_rows(x):
    rows = x.shape[0]
    row = _iota(x.shape, 0)
    sh = 1
    while sh < rows:
        x = jnp.maximum(x, jnp.where(row >= sh, pltpu.roll(x, sh, axis=0), -jnp.inf))
        sh *= 2
    return x


def _dot2(a, b):
    hi, lo = _split(a)
    return jnp.dot(hi, b, preferred_element_type=F32) + jnp.dot(lo, b, preferred_element_type=F32)


def _mlstm_dense_kernel(qk_ref, v_ref, og_ref, misc_ref, c0_ref, n0_ref, m0_ref, ib_ref, fb_ref, nw_ref,
                        el_ref, ev_ref, ek_ref, kmask_ref, vmask_ref, cmask_ref, cmaskb_ref, rms_ref, causal_ref,
                        dsel_ref,
                        y_ref, cfin_ref, nfin_ref, mfin_ref, c_ref, n_ref, m_ref, *, l, i_col, f_col):
    j = pl.program_id(1)
    wq = N_HEADS * DK_C
    nbb = qk_ref.shape[0]
    seqs = range(nbb)

    @pl.when(j == 0)
    def _():
        c_ref[...] = jnp.zeros(c_ref.shape, F32)
        for b in seqs:
            for h in range(N_HEADS):
                c_ref[b, h * DK_C:(h + 1) * DK_C, h * DV_C:(h + 1) * DV_C] = c0_ref[b, h]
        n_ref[...] = n0_ref[...]
        m_ref[...] = m0_ref[...]

    each = lambda f, *xs: [f(*args) for args in zip(*xs)]
    dot = functools.partial(jnp.dot, preferred_element_type=F32)
    spread = lambda x, e_ref: sum(dot(p, e_ref[...]) for p in _split3(x))
    tri = jnp.where(_iota((l, l), 0) >= _iota((l, l), 1), 1.0, 0.0).astype(BF16)
    ig = [misc_ref[b, :, i_col:i_col + N_HEADS] + ib_ref[...] for b in seqs]
    lf = [-_softplus(-(misc_ref[b, :, f_col:f_col + N_HEADS] + fb_ref[...])) for b in seqs]
    fc = each(lambda x: sum(dot(tri, p) for p in _split3(x)), lf)
    m_prev = [m_ref[b] for b in seqs]
    log_inter = each(jnp.add, fc, m_prev)
    a = each(jnp.subtract, ig, fc)
    m_t = each(lambda li, f, x: jnp.maximum(li, f + _cummax_rows(x)), log_inter, fc, a)
    w_inter = each(lambda li, m: jnp.exp(li - m), log_inter, m_t)
    w_last = each(lambda f, i, m: jnp.exp(f[l - 1:l, :] - f + i - m[l - 1:l, :]), fc, ig, m_t)

    key_terms = each(lambda x: jnp.sum(spread(x, el_ref) * dsel_ref[...], axis=0, keepdims=True), a)
    log_w = each(lambda f, kt: jnp.where(causal_ref[...] > 0.5, spread(f, el_ref) + kt, -jnp.inf), fc, key_terms)
    q = [qk_ref[b, :, :wq] for b in seqs]
    q_b = [x.astype(BF16) for x in q]
    k_s = [qk_ref[b, :, wq:] * (DK_C ** -0.5) for b in seqs]
    k_b = [x.astype(BF16) for x in k_s]
    v_b = [v_ref[b].astype(BF16) for b in seqs]
    kt_bd = [jnp.concatenate([x] * N_HEADS, axis=0) * kmask_ref[...] for x in k_b]
    v_bd = [jnp.concatenate([x] * N_HEADS, axis=0) * vmask_ref[...] for x in v_b]
    m_l = each(lambda m: spread(m, el_ref), m_t)
    qkw = each(lambda x, kt, lw, m: (_dot_nt(x, kt) * jnp.exp(lw - m)).astype(BF16), q_b, kt_bd, log_w, m_l)
    c_old = [c_ref[b] for b in seqs]
    n_old = [n_ref[b] for b in seqs]
    wi_s = each(lambda w: spread(w, ev_ref), w_inter)
    q_c = each(lambda x, c: dot(x, c.astype(BF16)), q_b, c_old)
    qn = each(lambda x, n: _dot2(x * n, cmaskb_ref[...]), q, n_old)
    num = each(lambda w, vb, wi, qc: dot(w, vb) + wi * qc, qkw, v_bd, wi_s, q_c)
    den = each(lambda w, wi, x: dot(w, vmask_ref[...]) + wi * x, qkw, wi_s, qn)
    m_v = m_l if l == DV_C else each(lambda m: spread(m, ev_ref), m_t)
    hid = each(lambda nu, de, m: nu / jnp.maximum(jnp.abs(de), jnp.exp(-m)), num, den, m_v)
    ms = each(lambda x: _dot2(x * x, rms_ref[...]), hid)
    for b in seqs:
        y_ref[b] = hid[b] * lax.rsqrt(ms[b] + 1e-6) * nw_ref[...] * _sigmoid(og_ref[b])

    kw = each(lambda k, w: k * spread(w, ek_ref), k_s, w_last)
    dec_rows = [jnp.broadcast_to(w[l - 1:l, :], (SUBLANES, N_HEADS)) for w in w_inter]
    upd = each(lambda x, vb: _dot_tn(x.astype(BF16), vb), kw, v_b)
    for b in seqs:
        c_ref[b] = c_old[b] * spread(dec_rows[b], ev_ref)[0:1, :] + upd[b] * cmask_ref[...]
        n_ref[b] = n_old[b] * spread(dec_rows[b], ek_ref)[0:1, :] + jnp.sum(kw[b], axis=0, keepdims=True)
        m_ref[b] = m_t[b][l - 1:l, :]

    @pl.when(j == pl.num_programs(1) - 1)
    def _():
        for b in seqs:
            for h in range(N_HEADS):
                cfin_ref[b, h] = c_ref[b, h * DK_C:(h + 1) * DK_C, h * DV_C:(h + 1) * DV_C]
        nfin_ref[...] = n_ref[...]
        mfin_ref[...] = m_ref[...]


def _mlstm_dense(qk, v, og, misc, c0, n0, m0, i_bias, f_bias, norm_w, *, nb, l, i_col, f_col):
    n = qk.shape[0]
    seq = n // nb
    wq, wv = N_HEADS * DK_C, N_HEADS * DV_C
    hl = N_HEADS * l
    nbb = SEQS_PER_STEP if nb % SEQS_PER_STEP == 0 else 1
    tok = lambda b, j: (b, j, 0)
    per_b3 = lambda b, j: (b, 0, 0)
    per_b4 = lambda b, j: (b, 0, 0, 0)
    const2 = lambda b, j: (0, 0)
    consts = [jnp.asarray(_block_mask(N_HEADS, 1, hl, l), BF16),
              jnp.asarray(_block_mask(N_HEADS, 1, wv, DV_C), BF16),
              jnp.asarray(_block_mask(N_HEADS, 1, wq, DK_C), BF16),
              jnp.asarray(_block_mask(hl, l, wq, DK_C), BF16),
              jnp.asarray(_block_mask(hl, l, wv, DV_C), BF16),
              jnp.asarray(_block_mask(wq, DK_C, wv, DV_C)),
              jnp.asarray(_block_mask(wq, DK_C, wv, DV_C), BF16),
              jnp.asarray(_block_mask(wv, DV_C, wv, DV_C) / DV_C, BF16),
              jnp.asarray(np.tile(np.tril(np.ones((l, l), np.float32)), (1, N_HEADS))),
              jnp.asarray(np.tile(np.eye(l, dtype=np.float32), (1, N_HEADS)))]
    tokens = [a.reshape(nb, seq, a.shape[1]) for a in (qk, v, og, misc)]
    outs = pl.pallas_call(
        functools.partial(_mlstm_dense_kernel, l=l, i_col=i_col, f_col=f_col),
        grid=(nb // nbb, seq // l),
        in_specs=[pl.BlockSpec((nbb, l, a.shape[2]), tok) for a in tokens]
                 + [pl.BlockSpec((nbb,) + c0.shape[1:], per_b4),
                    pl.BlockSpec((nbb, 1, wq), per_b3),
                    pl.BlockSpec((nbb, 1, N_HEADS), per_b3),
                    pl.BlockSpec((1, N_HEADS), const2),
                    pl.BlockSpec((1, N_HEADS), const2),
                    pl.BlockSpec((1, wv), const2)]
                 + [pl.BlockSpec(c.shape, const2) for c in consts],
        out_specs=[pl.BlockSpec((nbb, l, wv), tok),
                   pl.BlockSpec((nbb,) + c0.shape[1:], per_b4),
                   pl.BlockSpec((nbb, 1, wq), per_b3),
                   pl.BlockSpec((nbb, 1, N_HEADS), per_b3)],
        out_shape=[jax.ShapeDtypeStruct((nb, seq, wv), F32),
                   jax.ShapeDtypeStruct(c0.shape, F32),
                   jax.ShapeDtypeStruct((nb, 1, wq), F32),
                   jax.ShapeDtypeStruct((nb, 1, N_HEADS), F32)],
        scratch_shapes=[pltpu.VMEM((nbb, wq, wv), F32),
                        pltpu.VMEM((nbb, 1, wq), F32),
                        pltpu.VMEM((nbb, 1, N_HEADS), F32)],
        compiler_params=_params("parallel", "arbitrary"),
        name="mlstm",
    )(*tokens, c0, n0.reshape(nb, 1, wq), m0.reshape(nb, 1, N_HEADS), i_bias.reshape(1, -1), f_bias.reshape(1, -1),
      jnp.tile(norm_w, N_HEADS).reshape(1, wv), *consts)
    y, c_fin, n_fin, m_fin = outs
    return y.reshape(n, wv), c_fin, n_fin.reshape(nb, N_HEADS, DK_C), m_fin.reshape(nb, N_HEADS)


def _band_prompt_kernel(q_ref, k_ref, v_ref, bias_ref, o_ref, kb_ref, vt_ref, ot_ref, *, tq, seq):
    g = pl.program_id(1)
    w = q_ref.shape[1]
    pad = BAND_CHUNKS * CHUNK
    lw = pad + tq
    npad = pad // LANES
    per_tile = tq // LANES
    blk = _row_tile(seq, 4 * LANES)

    @pl.when(g == 0)
    def _():
        kb_ref[0:pad, :] = jnp.zeros((pad, w), BF16)
        vt_ref[0:npad] = jnp.zeros((npad, w, LANES), BF16)
        for r0 in range(0, seq, blk):
            kb_ref[pad + r0:pad + r0 + blk, :] = k_ref[0, r0:r0 + blk, :].astype(BF16)
        for j in range(seq // LANES):
            vt_ref[npad + j] = v_ref[0, j * LANES:(j + 1) * LANES, :].T.astype(BF16)

    start = pl.multiple_of(g * tq, tq)
    q = (q_ref[...] * (HEAD_DIM ** -0.5)).astype(BF16)
    before_seq = jnp.where(_iota((lw, tq), 0) >= pad - g * tq, 0.0, NEG)
    group = 4
    for h0 in range(0, N_HEADS, group):
        heads = range(h0, h0 + group)
        hs = {h: slice(h * HEAD_DIM, (h + 1) * HEAD_DIM) for h in heads}
        s = [_dot_nt(kb_ref[pl.ds(start, lw), hs[h]], q[:, hs[h]]) + (bias_ref[h] + before_seq) for h in heads]
        p = [jnp.exp(x - _reduce_rows(x, jnp.maximum, jnp.max)) for x in s]
        inv = [1.0 / _reduce_rows(x, jnp.add, jnp.sum) for x in p]
        pb = [x.astype(BF16) for x in p]
        acc = [sum(jnp.dot(vt_ref[g * per_tile + j, hs[h], :], x[j * LANES:(j + 1) * LANES, :],
                           preferred_element_type=F32) for j in range(lw // LANES)) for h, x in zip(heads, pb)]
        for h, a, r in zip(heads, acc, inv):
            ot_ref[hs[h], :] = a * r
    o_ref[...] = ot_ref[...].T


def _band_prompt(q, k, v, bias, *, nb, tq):
    n, w = q.shape
    seq = k.shape[1]
    nq = seq // tq
    pad = BAND_CHUNKS * CHUNK
    row = lambda b, g: (b * nq + g, 0)
    per_b = lambda b, g: (b, 0, 0)
    return pl.pallas_call(
        functools.partial(_band_prompt_kernel, tq=tq, seq=seq),
        grid=(nb, nq),
        in_specs=[pl.BlockSpec((tq, w), row),
                  pl.BlockSpec((1, seq, w), per_b),
                  pl.BlockSpec((1, seq, w), per_b),
                  pl.BlockSpec(bias.shape, lambda b, g: (0, 0, 0))],
        out_specs=pl.BlockSpec((tq, w), row),
        out_shape=jax.ShapeDtypeStruct(q.shape, F32),
        scratch_shapes=[pltpu.VMEM((pad + seq, w), BF16),
                        pltpu.VMEM(((pad + seq) // LANES, w, LANES), BF16),
                        pltpu.VMEM((w, tq), F32)],
        compiler_params=_params("parallel", "arbitrary"),
        name="band_attention_prompt",
    )(q, k, v, bias)


def _band_sample_kernel(q_ref, kc_ref, vc_ref, kn_ref, vn_ref, biasc_ref, biasn_ref, o_ref):
    pieces = [(lambda h: kc_ref[0, 0, h], lambda h: vc_ref[0, 0, h], biasc_ref),
              (lambda h: kn_ref[0, h], lambda h: vn_ref[0, h], biasn_ref)]
    q = q_ref[...]
    for h in range(N_HEADS):
        hs = slice(h * HEAD_DIM, (h + 1) * HEAD_DIM)
        qh = q[:, hs].astype(BF16)
        scores = [jnp.dot(qh, load_k(h).astype(BF16), preferred_element_type=F32) * (HEAD_DIM ** -0.5) + b_ref[h]
                  for load_k, _, b_ref in pieces]
        m = functools.reduce(jnp.maximum, [jnp.max(s, axis=-1, keepdims=True) for s in scores])
        ps = [jnp.exp(s - m) for s in scores]
        den = sum(jnp.sum(p, axis=-1, keepdims=True) for p in ps)
        o = sum(_dot_nt(p.astype(BF16), load_v(h).astype(BF16)) for p, (_, load_v, _) in zip(ps, pieces))
        o_ref[:, hs] = o / den


def _band_sample(q, kc, vc, layer, kn, vn, bias_c, bias_n, *, nb, tq):
    n, w = q.shape
    row = lambda b: (b, 0)
    const3 = lambda b: (0, 0, 0)
    return pl.pallas_call(
        _band_sample_kernel,
        grid=(nb,),
        in_specs=[pl.BlockSpec((tq, w), row)]
                 + [pl.BlockSpec((1, 1) + a.shape[2:], lambda b: (layer, b, 0, 0, 0)) for a in (kc, vc)]
                 + [pl.BlockSpec((1,) + a.shape[1:], lambda b: (b, 0, 0, 0)) for a in (kn, vn)]
                 + [pl.BlockSpec(bias_c.shape, const3), pl.BlockSpec(bias_n.shape, const3)],
        out_specs=pl.BlockSpec((tq, w), row),
        out_shape=jax.ShapeDtypeStruct(q.shape, F32),
        compiler_params=_params("parallel"),
        name="band_attention_sample",
    )(q, kc, vc, kn, vn, bias_c, bias_n)


def _t5_bucket(rel):
    nb = T5_BUCKETS // 2
    max_exact = nb // 2
    n = jnp.abs(rel)
    n_f = jnp.maximum(n, 1).astype(jnp.float32)
    large = max_exact + (jnp.log(n_f / max_exact) / math.log(T5_MAX_DIST / max_exact) * (nb - max_exact)).astype(jnp.int32)
    large = jnp.minimum(large, nb - 1)
    return jnp.where(rel > 0, nb, 0) + jnp.where(n < max_exact, n, large)


def _toeplitz_bias(fn, n_rows, n_cols):
    n = n_rows + n_cols
    m = np.arange(n)
    f = jnp.transpose(fn(np.where(m < n_cols, m, m - n))).astype(F32)
    flat = jnp.tile(f, (1, n_rows))[:, :n_rows * (n - 1)]
    return flat.reshape(f.shape[0], n_rows, n - 1)[:, :, :n_cols]


def _pack_cols(w, sizes, groups):
    offs = np.concatenate([[0], np.cumsum(sizes)])
    cols, widths = [], []
    for grp in groups:
        width = 0
        for idx in grp:
            cols.append(w[:, offs[idx]:offs[idx + 1]])
            width += sizes[idx]
        pad = (-width) % LANES
        if pad:
            cols.append(jnp.zeros((w.shape[0], pad), w.dtype))
        widths.append(width + pad)
    return jnp.concatenate(cols, axis=1).astype(BF16), tuple(widths)


def _row_tile(n, target):
    t = min(n, target)
    while n % t:
        t //= 2
    return t


def kernel(x_prompt, x_sample, cache_a_k, cache_a_v, cache_a_kidx, state_b_s, state_b_conv, state_c_c, state_c_n, state_c_m, cache_d_k, cache_d_v, state_ffn_conv, w_in_even, w_out_even, t5_bias, b_conv_w, b_a_log, b_dt_bias, b_norm_w, w_in_odd, w_out_odd, c_i_bias, c_f_bias, c_norm_w, d_rel_bias, ln_mix_g, ln_mix_b, ln_ffn_g, ln_ffn_b, ffn_w_up, ffn_conv_w, ffn_w_down):
    bp, sp, d = x_prompt.shape
    bs, ts, _ = x_sample.shape
    depth = ffn_w_up.shape[0]
    past = cache_a_k.shape[2]
    d_win = cache_d_k.shape[2]
    dff = ffn_w_down.shape[1]
    alpha = (2 * depth) ** 0.25
    w_a = N_HEADS * HEAD_DIM
    w_b = N_HEADS * DV_B
    w_c = N_HEADS * DV_C
    qkv_b_w = 2 * N_HEADS * DK_B + w_b
    even_sizes = (w_a, w_a, w_a, N_IDX_HEADS * D_IDX, D_IDX, N_IDX_HEADS, qkv_b_w, N_HEADS, N_HEADS, w_b)
    odd_sizes = (N_HEADS * DK_C, N_HEADS * DK_C, w_c, N_HEADS, N_HEADS, w_c, w_a, w_a, w_a)
    even_groups = ((0,), (1,), (2,), (3,), (6,), (9,), (4, 5, 7, 8))
    wi_col, a_col, b_col = D_IDX, D_IDX + N_IDX_HEADS, D_IDX + N_IDX_HEADS + N_HEADS
    odd_groups = ((0, 1), (2,), (5,), (6,), (7,), (8,), (3, 4))
    i_col, f_col = 0, N_HEADS

    assert sp % CHUNK == 0 and ts <= CHUNK and past % CHUNK == 0 and past >= T5_FAR
    assert (past + ts - 1) // CHUNK == past // CHUNK
    topk_p = min(TOPK_MAX, sp // 4)
    topk_s = min(TOPK_MAX, (past + ts) // 4)
    n_p, n_s = bp * sp, bs * ts
    tm_p = _row_tile(n_p, 512)
    tm_s = _row_tile(n_s, 512)
    tff_p = _row_tile(sp, 512)
    ns_s = _row_tile(bs, max(1, 256 // ts))

    t5 = lambda rel: t5_bias[_t5_bucket(jnp.asarray(rel, jnp.int32))]
    fbias = t5(np.array([-T5_FAR - 1]))
    tq_dsa = 2 * LANES
    assert sp % tq_dsa == 0 and T5_FAR <= tq_dsa
    corr_p = jnp.stack([_toeplitz_bias(lambda dd: t5(-dd - tq_dsa * i) - fbias, tq_dsa, tq_dsa) for i in range(2)])
    tq_a = 4 * CHUNK
    ln_s = T5_FAR + ts
    nbias_s = _toeplitz_bias(lambda dd: t5(dd - T5_FAR), ts, ln_s)[None]

    lw = BAND_CHUNKS * CHUNK + tq_a
    r_chunk = np.arange(lw)[:, None] // CHUNK
    q_chunk = BAND_CHUNKS + np.arange(tq_a)[None, :] // CHUNK
    band_ok = (r_chunk >= q_chunk - BAND_CHUNKS) & (r_chunk <= q_chunk)
    pos_q = past + np.arange(ts)
    pos_kc = past - d_win + np.arange(d_win)
    def band_valid(pos_k):
        kch, qch = pos_k // CHUNK, pos_q // CHUNK
        return (pos_k[None] >= 0) & (kch[None] >= qch[:, None] - BAND_CHUNKS) & (kch[None] <= qch[:, None])
    def band_bias(table, shift, n_rows, n_cols, valid=None):
        bias = _toeplitz_bias(lambda dd: table[np.clip(dd + shift, -REL_CLIP, REL_CLIP) + REL_CLIP], n_rows, n_cols)
        return bias if valid is None else jnp.where(jnp.asarray(valid)[None], bias, NEG)

    cak_t = jnp.transpose(cache_a_k, (0, 1, 3, 4, 2))
    cav_t = jnp.transpose(cache_a_v, (0, 1, 3, 4, 2))
    caki_t = jnp.transpose(cache_a_kidx, (0, 1, 3, 2))
    cdk_t = jnp.transpose(cache_d_k, (0, 1, 3, 4, 2))
    cdv_t = jnp.transpose(cache_d_v, (0, 1, 3, 4, 2))

    xp = x_prompt.reshape(n_p, d)
    xs = x_sample.reshape(n_s, d)
    outs = {k: [] for k in ("ak_p", "ak_s", "av_p", "av_s", "aki_p", "aki_s", "bs_p", "bs_s", "bc_p", "bc_s",
                            "cc_p", "cc_s", "cn_p", "cn_s", "cm_p", "cm_s", "dk_p", "dk_s", "dv_p", "dv_s",
                            "fc_p", "fc_s")}
    for layer in range(depth):
        if layer % 2 == 0:
            e = layer // 2
            w_in, widths = _pack_cols(w_in_even[e], even_sizes, even_groups)
            w_out = w_out_even[e].astype(BF16)
            qa, ka, va, qi, qkv_b, z_b, misc = _proj(xp, w_in, widths, tm_p)
            o_a = _dsa_blocks(qa, qi, misc, ka.reshape(bp, sp, w_a), va.reshape(bp, sp, w_a),
                              misc.reshape(bp, sp, LANES), corr_p, nb=bp, tq=tq_dsa, topk=topk_p, wi_col=wi_col)
            y_b, s_b, h_b = _gdn(qkv_b, misc, z_b, jnp.zeros((bp, CONV_B - 1, qkv_b_w), F32),
                                 jnp.zeros((bp, N_HEADS, DK_B, DV_B), F32), b_conv_w[e], b_a_log[e], b_dt_bias[e],
                                 b_norm_w[e], nb=bp, c=CHUNK, a_col=a_col, b_col=b_col)
            xp = _mm_res_ln([o_a, y_b], [w_out[:w_a], w_out[w_a:]], xp, ln_mix_g[layer], ln_mix_b[layer], alpha, tm_p)
            outs["ak_p"].append(ka.reshape(bp, sp, N_HEADS, HEAD_DIM))
            outs["av_p"].append(va.reshape(bp, sp, N_HEADS, HEAD_DIM))
            outs["aki_p"].append(misc[:, :D_IDX].reshape(bp, sp, D_IDX))
            outs["bs_p"].append(s_b)
            outs["bc_p"].append(h_b)
            qa, ka, va, qi, qkv_b, z_b, misc = _proj(xs, w_in, widths, tm_s)
            ki = misc[:, :D_IDX]
            heads_t = lambda u: jnp.transpose(u.reshape(bs, ts, N_HEADS, HEAD_DIM), (0, 2, 3, 1))
            near = (jnp.concatenate([cak_t[e, ..., past - T5_FAR:], heads_t(ka)], axis=-1),
                    jnp.concatenate([cav_t[e, ..., past - T5_FAR:], heads_t(va)], axis=-1),
                    jnp.concatenate([caki_t[e, ..., past - T5_FAR:],
                                     jnp.transpose(ki.reshape(bs, ts, D_IDX), (0, 2, 1))], axis=-1))
            o_a = _dsa_sample(qa, qi, misc, cak_t, cav_t, caki_t, e, near, nbias_s, fbias, nb=bs, tq=ts,
                              topk=topk_s, wi_col=wi_col)
            y_b, s_b, h_b = _gdn(qkv_b, misc, z_b, state_b_conv[e], state_b_s[e], b_conv_w[e], b_a_log[e],
                                 b_dt_bias[e], b_norm_w[e], nb=bs, c=ts, a_col=a_col, b_col=b_col)
            xs = _mm_res_ln([o_a, y_b], [w_out[:w_a], w_out[w_a:]], xs, ln_mix_g[layer], ln_mix_b[layer], alpha, tm_s)
            outs["ak_s"].append(ka.reshape(bs, ts, N_HEADS, HEAD_DIM))
            outs["av_s"].append(va.reshape(bs, ts, N_HEADS, HEAD_DIM))
            outs["aki_s"].append(ki.reshape(bs, ts, D_IDX))
            outs["bs_s"].append(s_b)
            outs["bc_s"].append(h_b)
        else:
            o = layer // 2
            w_in, widths = _pack_cols(w_in_odd[o], odd_sizes, odd_groups)
            w_out = w_out_odd[o].astype(BF16)
            qk_c, v_c, o_c, q_d, k_d, v_d, misc = _proj(xp, w_in, widths, tm_p)
            y_c, c_c, c_n, c_m = _mlstm_dense(qk_c, v_c, o_c, misc,jnp.zeros((bp, N_HEADS, DK_C, DV_C), F32),
                                        jnp.zeros((bp, N_HEADS, DK_C), F32), jnp.zeros((bp, N_HEADS), F32),
                                        c_i_bias[o], c_f_bias[o], c_norm_w[o], nb=bp, l=CHUNK, i_col=i_col, f_col=f_col)
            k3 = k_d.reshape(bp, sp, w_a)
            v3 = v_d.reshape(bp, sp, w_a)
            bias_bp = _toeplitz_bias(
                lambda dd: d_rel_bias[o][np.clip(-dd - BAND_CHUNKS * CHUNK, -REL_CLIP, REL_CLIP) + REL_CLIP], lw, tq_a)
            bias_bp = jnp.where(jnp.asarray(band_ok)[None], bias_bp, NEG)
            o_d = _band_prompt(q_d, k3, v3, bias_bp, nb=bp, tq=tq_a)
            xp = _mm_res_ln([y_c, o_d], [w_out[:w_c], w_out[w_c:]], xp, ln_mix_g[layer], ln_mix_b[layer], alpha, tm_p)
            d_win_p = min(BAND_CHUNKS * CHUNK, sp)
            outs["cc_p"].append(c_c)
            outs["cn_p"].append(c_n)
            outs["cm_p"].append(c_m)
            outs["dk_p"].append(k3[:, sp - d_win_p:].reshape(bp, d_win_p, N_HEADS, HEAD_DIM))
            outs["dv_p"].append(v3[:, sp - d_win_p:].reshape(bp, d_win_p, N_HEADS, HEAD_DIM))
            qk_c, v_c, o_c, q_d, k_d, v_d, misc = _proj(xs, w_in, widths, tm_s)
            y_c, c_c, c_n, c_m = _mlstm_dense(qk_c, v_c, o_c, misc,state_c_c[o], state_c_n[o], state_c_m[o],
                                        c_i_bias[o], c_f_bias[o], c_norm_w[o], nb=bs, l=ts, i_col=i_col, f_col=f_col)
            heads_t = lambda u: jnp.transpose(u.reshape(bs, ts, N_HEADS, HEAD_DIM), (0, 2, 3, 1))
            o_d = _band_sample(q_d, cdk_t, cdv_t, o, heads_t(k_d), heads_t(v_d),
                               band_bias(d_rel_bias[o], -d_win, ts, d_win, band_valid(pos_kc)),
                               band_bias(d_rel_bias[o], 0, ts, ts, band_valid(pos_q)), nb=bs, tq=ts)
            xs = _mm_res_ln([y_c, o_d], [w_out[:w_c], w_out[w_c:]], xs, ln_mix_g[layer], ln_mix_b[layer], alpha, tm_s)
            outs["cc_s"].append(c_c)
            outs["cn_s"].append(c_n)
            outs["cm_s"].append(c_m)
            outs["dk_s"].append(k_d.reshape(bs, ts, N_HEADS, HEAD_DIM))
            outs["dv_s"].append(v_d.reshape(bs, ts, N_HEADS, HEAD_DIM))
        w_up = ffn_w_up[layer].astype(BF16)
        w_down = ffn_w_down[layer].astype(BF16)
        act, hist_p = _ffn_up(xp, w_up, ffn_conv_w[layer], jnp.zeros((bp, CONV_FF - 1, 2 * dff), F32),
                              1, tff_p, sp // tff_p)
        xp = _mm_res_ln([act], [w_down], xp, ln_ffn_g[layer], ln_ffn_b[layer], alpha, tm_p)
        act, hist_s = _ffn_up(xs, w_up, ffn_conv_w[layer], state_ffn_conv[layer], ns_s, ts, 1)
        xs = _mm_res_ln([act], [w_down], xs, ln_ffn_g[layer], ln_ffn_b[layer], alpha, tm_s)
        outs["fc_p"].append(hist_p)
        outs["fc_s"].append(hist_s)

    st = lambda k: jnp.stack(outs[k])
    return (xp.reshape(bp, sp, d), xs.reshape(bs, ts, d),
            st("ak_p"), st("ak_s"), st("av_p"), st("av_s"), st("aki_p"), st("aki_s"),
            st("bs_p"), st("bs_s"), st("bc_p"), st("bc_s"),
            st("cc_p"), st("cc_s"), st("cn_p"), st("cn_s"), st("cm_p"), st("cm_s"),
            st("dk_p"), st("dk_s"), st("dv_p"), st("dv_s"),
            st("fc_p"), st("fc_s"))
```

```python
import functools
import math

import numpy as np
import jax
import jax.numpy as jnp
from jax import lax
from jax.experimental import pallas as pl
from jax.experimental.pallas import tpu as pltpu

F32 = jnp.float32
BF16 = jnp.bfloat16
HI = lax.Precision.HIGHEST

CHUNK = 64
HEAD_DIM = 64
N_HEADS = 8
N_IDX_HEADS = 8
D_IDX = 64
TOPK_MAX = 256
T5_BUCKETS = 32
T5_MAX_DIST = 128
DK_B = 64
DV_B = 64
CONV_B = 4
DK_C = 32
DV_C = 64
BAND_CHUNKS = 8
REL_CLIP = 128
CONV_FF = 3
T5_FAR = 128
INV_BLOCK = 8
SAMPLE_STREAMS_PER_STEP = 4
GDN_SEQS_PER_STEP = 2
SEQS_PER_STEP = 4

LANES = 128
SUBLANES = 8
VMEM_LIMIT = 56 * 1024 * 1024

NEG = -1e30
INT_MIN = -2 ** 31


def _params(*sem):
    return pltpu.CompilerParams(dimension_semantics=sem, vmem_limit_bytes=VMEM_LIMIT)


def _dot(a, b):
    return jnp.dot(a.astype(BF16), b.astype(BF16), preferred_element_type=F32)


def _dot_nt(a, b, precision=None):
    return lax.dot_general(a, b, (((1,), (1,)), ((), ())), precision=precision, preferred_element_type=F32)


def _dot_tn(a, b, precision=None):
    return lax.dot_general(a, b, (((0,), (0,)), ((), ())), precision=precision, preferred_element_type=F32)


def _dot_hi(a, b):
    return jnp.dot(a, b, precision=HI, preferred_element_type=F32)


def _split(a):
    hi = a.astype(BF16)
    return hi, (a - hi.astype(F32)).astype(BF16)


def _split3(a):
    p1 = a.astype(BF16)
    r1 = a - p1.astype(F32)
    p2 = r1.astype(BF16)
    return p1, p2, (r1 - p2.astype(F32)).astype(BF16)


def _dot3(a, b):
    a_hi, a_lo = a
    b_hi, b_lo = b
    d = functools.partial(jnp.dot, preferred_element_type=F32)
    return d(a_hi, b_hi) + (d(a_hi, b_lo) + d(a_lo, b_hi))


def _sigmoid(x):
    return 1.0 / (1.0 + jnp.exp(-x))


def _softplus(x):
    return jnp.maximum(x, 0.0) + jnp.log(1.0 + jnp.exp(-jnp.abs(x)))


def _iota(shape, dim):
    return lax.broadcasted_iota(jnp.int32, shape, dim)


def _proj_kernel(x_ref, w_ref, *out_refs, sizes):
    xb = x_ref[...].astype(BF16)
    off = 0
    for o_ref, size in zip(out_refs, sizes):
        o_ref[...] = jnp.dot(xb, w_ref[:, off:off + size], preferred_element_type=F32)
        off += size


def _proj(x2d, w, sizes, tm):
    n, d = x2d.shape
    return pl.pallas_call(
        functools.partial(_proj_kernel, sizes=sizes),
        grid=(n // tm,),
        in_specs=[pl.BlockSpec((tm, d), lambda i: (i, 0)),
                  pl.BlockSpec((d, sum(sizes)), lambda i: (0, 0))],
        out_specs=[pl.BlockSpec((tm, s), lambda i: (i, 0)) for s in sizes],
        out_shape=[jax.ShapeDtypeStruct((n, s), F32) for s in sizes],
        compiler_params=_params("parallel"),
        name="in_proj",
    )(x2d, w)


def _mm_res_ln_kernel(*refs, nparts, alpha):
    part_refs = refs[:nparts]
    w_refs = refs[nparts:2 * nparts]
    x_ref, g_ref, b_ref, o_ref = refs[2 * nparts:]
    acc = alpha * x_ref[...]
    for p_ref, w_ref in zip(part_refs, w_refs):
        acc = acc + jnp.dot(p_ref[...].astype(BF16), w_ref[...], preferred_element_type=F32)
    mu = jnp.mean(acc, axis=-1, keepdims=True)
    cen = acc - mu
    var = jnp.mean(cen * cen, axis=-1, keepdims=True)
    o_ref[...] = cen * lax.rsqrt(var + 1e-5) * g_ref[...] + b_ref[...]


def _mm_res_ln(parts, ws, x2d, g, b, alpha, tm):
    n, d = x2d.shape
    nparts = len(parts)
    in_specs = ([pl.BlockSpec((tm, p.shape[1]), lambda i: (i, 0)) for p in parts]
                + [pl.BlockSpec(w.shape, lambda i: (0, 0)) for w in ws]
                + [pl.BlockSpec((tm, d), lambda i: (i, 0)),
                   pl.BlockSpec((1, d), lambda i: (0, 0)),
                   pl.BlockSpec((1, d), lambda i: (0, 0))])
    return pl.pallas_call(
        functools.partial(_mm_res_ln_kernel, nparts=nparts, alpha=alpha),
        grid=(n // tm,),
        in_specs=in_specs,
        out_specs=pl.BlockSpec((tm, d), lambda i: (i, 0)),
        out_shape=jax.ShapeDtypeStruct((n, d), F32),
        compiler_params=_params("parallel"),
        name="out_proj_ln",
    )(*parts, *ws, x2d, g.reshape(1, d), b.reshape(1, d))


def _ffn_up_kernel(x_ref, w_ref, cw_ref, hist_ref, act_ref, newhist_ref, ext_ref, *,
                   ns, tt, tiles_per_seq, dff, cc):
    i = pl.program_id(0)
    tm = ns * tt
    hw = CONV_FF - 1
    base = SUBLANES
    if tiles_per_seq == 1:
        ext_ref[:, base - hw:base, :] = hist_ref[...]
    else:
        @pl.when(i % tiles_per_seq == 0)
        def _():
            ext_ref[:, base - hw:base, :] = hist_ref[...]

        @pl.when(i % tiles_per_seq != 0)
        def _():
            ext_ref[:, base - hw:base, :] = ext_ref[:, base + tt - hw:base + tt, :]
    xb = x_ref[...].astype(BF16)
    for j in range(2 * dff // cc):
        cols = slice(j * cc, (j + 1) * cc)
        h = jnp.dot(xb, w_ref[:, cols], preferred_element_type=F32)
        ext_ref[:, base:base + tt, cols] = h.reshape(ns, tt, cc)
    newhist_ref[...] = ext_ref[:, base + tt - hw:base + tt, :]

    def conv(cols):
        acc = None
        for k in range(CONV_FF):
            term = ext_ref[:, base - hw + k:base - hw + k + tt, cols] * cw_ref[k:k + 1, cols]
            acc = term if acc is None else acc + term
        return acc

    for j in range(dff // cc):
        g = conv(slice(j * cc, (j + 1) * cc))
        u = conv(slice(dff + j * cc, dff + (j + 1) * cc))
        act = g * _sigmoid(g) * u
        act_ref[:, j * cc:(j + 1) * cc] = act.reshape(tm, cc).astype(BF16)


def _ffn_up(x2d, w_up, conv_w, hist, ns, tt, tiles_per_seq):
    n, d = x2d.shape
    c2 = w_up.shape[1]
    dff = c2 // 2
    tm = ns * tt
    cc = 256
    hw = CONV_FF - 1
    if tiles_per_seq == 1:
        hist_map = lambda i: (i, 0, 0)
    else:
        hist_map = lambda i: (i // tiles_per_seq, 0, 0)
    return pl.pallas_call(
        functools.partial(_ffn_up_kernel, ns=ns, tt=tt, tiles_per_seq=tiles_per_seq, dff=dff, cc=cc),
        grid=(n // tm,),
        in_specs=[pl.BlockSpec((tm, d), lambda i: (i, 0)),
                  pl.BlockSpec((d, c2), lambda i: (0, 0)),
                  pl.BlockSpec((CONV_FF, c2), lambda i: (0, 0)),
                  pl.BlockSpec((ns, hw, c2), hist_map)],
        out_specs=[pl.BlockSpec((tm, dff), lambda i: (i, 0)),
                   pl.BlockSpec((ns, hw, c2), hist_map)],
        out_shape=[jax.ShapeDtypeStruct((n, dff), BF16),
                   jax.ShapeDtypeStruct(hist.shape, F32)],
        scratch_shapes=[pltpu.VMEM((ns, SUBLANES + tt, c2), F32)],
        compiler_params=_params("arbitrary"),
        name="ffn_up_conv_gate",
    )(x2d, w_up, conv_w, hist)


def _sortable(x):
    b = lax.bitcast_convert_type(x, jnp.int32)
    return b ^ ((b >> 31) & jnp.int32(0x7FFFFFFF))


def _count(mask):
    return jnp.sum(jnp.where(mask, 1.0, 0.0), axis=-1, keepdims=True)


def _dsa_sample_kernel(qa_ref, qi_ref, qm_ref, kf_ref, vf_ref, kif_ref, kn_ref, vn_ref, kin_ref, nbias_ref,
                       fbias_ref, o_ref, self_ref, seln_ref, *, topk, tq, nbs, lf, ln, wi_col):
    start = lf - T5_FAR
    streams = range(nbs)
    rows = [slice(b * tq, (b + 1) * tq) for b in streams]
    dot = functools.partial(jnp.dot, preferred_element_type=F32)
    wi = [qm_ref[rows[b], wi_col:wi_col + N_IDX_HEADS] * (N_IDX_HEADS ** -0.5) * (D_IDX ** -0.5) for b in streams]
    kif = [kif_ref[0, b].astype(BF16) for b in streams]
    kinb = [kin_ref[b].astype(BF16) for b in streams]

    sc_f = [jnp.zeros((tq, lf), F32) for _ in streams]
    sc_n = [jnp.zeros((tq, ln), F32) for _ in streams]
    for n in range(N_IDX_HEADS):
        qn = [qi_ref[rows[b], n * D_IDX:(n + 1) * D_IDX].astype(BF16) for b in streams]
        sc_f = [sc_f[b] + jnp.maximum(dot(qn[b], kif[b]), 0.0) * wi[b][:, n:n + 1] for b in streams]
        sc_n = [sc_n[b] + jnp.maximum(dot(qn[b], kinb[b]), 0.0) * wi[b][:, n:n + 1] for b in streams]
    adm_f = _iota((tq, lf), 1) < start
    adm_n = nbias_ref[0, 0] > 0.5 * NEG
    key_f = [jnp.where(adm_f, _sortable(x), jnp.int32(INT_MIN)) for x in sc_f]
    key_n = [jnp.where(adm_n, _sortable(x), jnp.int32(INT_MIN)) for x in sc_n]

    kf32 = float(topk)

    def body(i, t_us):
        bit = lax.shift_left(jnp.int32(1), 31 - i)
        cand_u = [t | bit for t in t_us]
        cand_s = [c ^ jnp.int32(INT_MIN) for c in cand_u]
        cnt = [_count(key_f[b] >= cand_s[b]) + _count(key_n[b] >= cand_s[b]) for b in streams]
        return tuple(jnp.where(cnt[b] >= kf32, cand_u[b], t_us[b]) for b in streams)

    t_us = lax.fori_loop(0, 32, body, tuple(jnp.zeros((tq, 1), jnp.int32) for _ in streams))
    thr = [t ^ jnp.int32(INT_MIN) for t in t_us]
    need = [kf32 - (_count(key_f[b] > thr[b]) + _count(key_n[b] > thr[b])) for b in streams]
    n_eq = [_count(key_f[b] == thr[b]) + _count(key_n[b] == thr[b]) for b in streams]
    open_row = [t == jnp.int32(INT_MIN) for t in thr]
    for b in streams:
        self_ref[b] = jnp.where(jnp.logical_and(key_f[b] >= thr[b], adm_f), 0.0, NEG)
        seln_ref[b] = jnp.where(jnp.logical_and(key_n[b] >= thr[b], adm_n), 0.0, NEG)

    def resolve_ties(b):
        upper = jnp.where(_iota((LANES, LANES), 0) < _iota((LANES, LANES), 1), 1.0, 0.0).astype(BF16)
        offset = jnp.zeros((tq, 1), F32)
        for ref, key, width in ((self_ref, key_f[b], lf), (seln_ref, key_n[b], ln)):
            for j0 in range(0, width, LANES):
                w = min(LANES, width - j0)
                kb = key[:, j0:j0 + w]
                e = jnp.where(kb == thr[b], 1.0, 0.0)
                rank = offset + dot(e.astype(BF16), upper[:w, :w])
                take = jnp.where(kb > thr[b], 1.0, jnp.where(rank < need[b], e, 0.0))
                take = jnp.where(open_row[b], jnp.where(kb > thr[b], 1.0, 0.0), take)
                ref[b, :, j0:j0 + w] = jnp.where(take > 0.5, 0.0, NEG)
                offset = offset + jnp.sum(e, axis=-1, keepdims=True)

    for b in streams:
        conflict = jnp.logical_and(n_eq[b] != need[b], jnp.logical_not(open_row[b]))
        pl.when(jnp.max(jnp.where(conflict, 1.0, 0.0)) > 0.0)(functools.partial(resolve_ties, b))

    sel_f = [self_ref[b] for b in streams]
    sel_n = [seln_ref[b] for b in streams]
    for h in range(N_HEADS):
        hs = slice(h * HEAD_DIM, (h + 1) * HEAD_DIM)
        qh = [qa_ref[rows[b], hs].astype(BF16) for b in streams]
        s_f = [dot(qh[b], kf_ref[0, b, h].astype(BF16)) * (HEAD_DIM ** -0.5) + fbias_ref[:, h:h + 1] + sel_f[b]
               for b in streams]
        s_n = [dot(qh[b], kn_ref[b, h].astype(BF16)) * (HEAD_DIM ** -0.5) + nbias_ref[0, h] + sel_n[b]
               for b in streams]
        m = [jnp.maximum(jnp.max(x, axis=-1, keepdims=True), jnp.max(y, axis=-1, keepdims=True))
             for x, y in zip(s_f, s_n)]
        p_f = [jnp.exp(x - mm) for x, mm in zip(s_f, m)]
        p_n = [jnp.exp(x - mm) for x, mm in zip(s_n, m)]
        den = [jnp.sum(x, axis=-1, keepdims=True) + jnp.sum(y, axis=-1, keepdims=True) for x, y in zip(p_f, p_n)]
        o = [_dot_nt(p_f[b].astype(BF16), vf_ref[0, b, h].astype(BF16))
             + _dot_nt(p_n[b].astype(BF16), vn_ref[b, h].astype(BF16)) for b in streams]
        for b in streams:
            o_ref[rows[b], hs] = o[b] / den[b]


def _dsa_sample(qa, qi, qmisc, kf, vf, kif, layer, near, nbias, fbias, *, nb, tq, topk, wi_col):
    lf = kf.shape[-1]
    ln = nbias.shape[-1]
    nbs = SAMPLE_STREAMS_PER_STEP if nb % SAMPLE_STREAMS_PER_STEP == 0 else 1
    row = lambda b: (b, 0)
    args = [qa, qi, qmisc, kf, vf, kif, *near]
    in_specs = ([pl.BlockSpec((nbs * tq, a.shape[1]), row) for a in args[:3]]
                + [pl.BlockSpec((1, nbs) + a.shape[2:], lambda b, nd=a.ndim: (layer, b) + (0,) * (nd - 2))
                   for a in args[3:6]]
                + [pl.BlockSpec((nbs,) + a.shape[1:], lambda b, nd=a.ndim: (b,) + (0,) * (nd - 1)) for a in args[6:]]
                + [pl.BlockSpec(nbias.shape, lambda b: (0, 0, 0, 0)),
                   pl.BlockSpec(fbias.shape, lambda b: (0, 0))])
    return pl.pallas_call(
        functools.partial(_dsa_sample_kernel, topk=topk, tq=tq, nbs=nbs, lf=lf, ln=ln, wi_col=wi_col),
        grid=(nb // nbs,),
        in_specs=in_specs,
        out_specs=pl.BlockSpec((nbs * tq, qa.shape[1]), row),
        out_shape=jax.ShapeDtypeStruct(qa.shape, F32),
        scratch_shapes=[pltpu.VMEM((nbs, tq, lf), F32), pltpu.VMEM((nbs, tq, ln), F32)],
        compiler_params=_params("parallel"),
        name="dsa_attention_sample",
    )(*args, nbias, fbias)


def _reduce_rows(x, op, final):
    blk = 8 * SUBLANES
    parts = [x[r0:r0 + blk] for r0 in range(0, x.shape[0], blk)]
    while len(parts) > 1:
        parts = [op(parts[i], parts[i + 1]) for i in range(0, len(parts) - 1, 2)] + parts[len(parts) & ~1:]
    return final(parts[0], axis=0, keepdims=True)


def _block_rows(x, op):
    blk = 8 * SUBLANES
    parts = [x[r0:r0 + blk] for r0 in range(0, x.shape[0], blk)]
    while len(parts) > 1:
        parts = [op(parts[i], parts[i + 1]) for i in range(0, len(parts) - 1, 2)] + parts[len(parts) & ~1:]
    return parts[0]


def _dsa_blocks_kernel(qa_ref, qi_ref, qm_ref, k_ref, v_ref, ki_ref, corr_ref, o_ref,
                       kb_ref, vt_ref, kib_ref, key_ref, sel_ref, thr_ref, stat_ref, m_ref, l_ref, ot_ref, *,
                       topk, tq, seq, wi_col):
    g = pl.program_id(1)
    kb = 2 * LANES
    w = qa_ref.shape[1]
    blk = _row_tile(seq, 4 * LANES)

    @pl.when(g == 0)
    def _():
        for r0 in range(0, seq, blk):
            kb_ref[r0:r0 + blk, :] = k_ref[0, r0:r0 + blk, :].astype(BF16)
            kib_ref[r0:r0 + blk, :] = ki_ref[0, r0:r0 + blk, :].astype(BF16)
        for j in range(seq // LANES):
            vt_ref[j] = v_ref[0, j * LANES:(j + 1) * LANES, :].T.astype(BF16)

    top = (g + 1) * tq
    nkb = (top + kb - 1) // kb
    start_of = lambda i: pl.multiple_of(jnp.maximum(top - kb * (i + 1), 0), LANES)

    qa = (qa_ref[...] * (HEAD_DIM ** -0.5)).astype(BF16)
    qi = qi_ref[...].astype(BF16)
    wi = qm_ref[:, wi_col:wi_col + N_IDX_HEADS] * (N_IDX_HEADS ** -0.5) * (D_IDX ** -0.5)
    eye_h = jnp.where(_iota((N_IDX_HEADS, N_IDX_HEADS), 0) == _iota((N_IDX_HEADS, N_IDX_HEADS), 1), 1.0, 0.0)
    wi_t = _dot_nt(eye_h, wi, HI)
    q_chunk = (g * tq + _iota((1, tq), 1)) // CHUNK

    def score_block(i, carry):
        st = start_of(i)
        kib = kib_ref[pl.ds(st, kb), :][:, :D_IDX]
        acc = jnp.zeros((kb, tq), F32)
        for n in range(N_IDX_HEADS):
            acc = acc + jnp.maximum(_dot_nt(kib, qi[:, n * D_IDX:(n + 1) * D_IDX]), 0.0) * wi_t[n:n + 1, :]
        row = st + _iota((kb, 1), 0)
        ok = jnp.logical_and(row // CHUNK <= q_chunk, row < top - kb * i)
        key_ref[i] = jnp.where(ok, _sortable(acc), jnp.int32(INT_MIN))
        return carry

    lax.fori_loop(0, nkb, score_block, 0)

    kf32 = float(topk)

    def search_blocks(n_blocks):
        def count(pred):
            parts = [_block_rows(jnp.where(pred(key_ref[i]), 1.0, 0.0), jnp.add) for i in range(n_blocks)]
            while len(parts) > 1:
                parts = [a + b for a, b in zip(parts[0::2], parts[1::2])] + parts[len(parts) & ~1:]
            return jnp.sum(parts[0], axis=0, keepdims=True)

        def search(b, t_u):
            cand_u = t_u | lax.shift_left(jnp.int32(1), 31 - b)
            cand_s = cand_u ^ jnp.int32(INT_MIN)
            return jnp.where(count(lambda x: x >= cand_s) >= kf32, cand_u, t_u)

        t_u = lax.fori_loop(0, 32, search, jnp.zeros((1, tq), jnp.int32))
        thr = t_u ^ jnp.int32(INT_MIN)
        thr_ref[0:1, :] = thr
        stat_ref[0:1, :] = kf32 - count(lambda x: x > thr)
        stat_ref[1:2, :] = count(lambda x: x == thr)

    for n_blocks in range(1, seq // kb + 1):
        pl.when(nkb == n_blocks)(functools.partial(search_blocks, n_blocks))

    thr = thr_ref[0:1, :]
    need = stat_ref[0:1, :]
    open_row = thr == jnp.int32(INT_MIN)
    conflict = jnp.logical_and(stat_ref[1:2, :] != need, jnp.logical_not(open_row))
    floor = jnp.maximum(thr, jnp.int32(INT_MIN + 1))

    def select_block(i, carry):
        sel_ref[i] = jnp.where(key_ref[i] >= floor, 0.0, NEG)
        return carry

    lax.fori_loop(0, nkb, select_block, 0)

    @pl.when(jnp.max(jnp.where(conflict, 1.0, 0.0)) > 0.0)
    def _():
        below = jnp.where(_iota((LANES, LANES), 1) < _iota((LANES, LANES), 0), 1.0, 0.0).astype(BF16)

        def tie_block(ii, offset):
            i = nkb - 1 - ii
            for r0 in range(0, kb, LANES):
                kblk = key_ref[i, r0:r0 + LANES, :]
                e = jnp.where(kblk == thr, 1.0, 0.0)
                rank = offset + jnp.dot(below, e.astype(BF16), preferred_element_type=F32)
                take = jnp.where(kblk > thr, 1.0, jnp.where(jnp.logical_or(rank >= need, open_row), 0.0, e))
                sel_ref[i, r0:r0 + LANES, :] = jnp.where(take > 0.5, 0.0, NEG)
                offset = offset + jnp.sum(e, axis=0, keepdims=True)
            return offset

        lax.fori_loop(0, nkb, tie_block, jnp.zeros((1, tq), F32))

    heads = range(N_HEADS)
    hs = [slice(h * HEAD_DIM, (h + 1) * HEAD_DIM) for h in heads]

    def attend_block(i, first, near=None):
        st = start_of(i)
        vblk = st // LANES
        sel = sel_ref[i]
        s = [_dot_nt(kb_ref[pl.ds(st, kb), hs[h]], qa[:, hs[h]]) + sel for h in heads]
        if near is not None:
            s = [s[h] + corr_ref[near, h] for h in heads]
        m_blk = [jnp.max(_block_rows(x, jnp.maximum), axis=0, keepdims=True) for x in s]
        m_old = [m_ref[h:h + 1, :] for h in heads]
        m_new = m_blk if first else [jnp.maximum(a, b) for a, b in zip(m_old, m_blk)]
        p = [jnp.exp(x - m) for x, m in zip(s, m_new)]
        l_blk = [jnp.sum(_block_rows(x, jnp.add), axis=0, keepdims=True) for x in p]
        pb = [x.astype(BF16) for x in p]
        pv = [sum(jnp.dot(vt_ref[vblk + j, hs[h], :], pb[h][j * LANES:(j + 1) * LANES, :],
                          preferred_element_type=F32) for j in range(kb // LANES)) for h in heads]
        for h in heads:
            if first:
                l_ref[h:h + 1, :] = l_blk[h]
                ot_ref[hs[h], :] = pv[h]
            else:
                alpha = jnp.exp(m_old[h] - m_new[h])
                l_ref[h:h + 1, :] = l_ref[h:h + 1, :] * alpha + l_blk[h]
                ot_ref[hs[h], :] = ot_ref[hs[h], :] * alpha + pv[h]
            m_ref[h:h + 1, :] = m_new[h]

    attend_block(0, True, 0)
    pl.when(nkb > 1)(lambda: attend_block(1, False, 1))

    def attend_rest(i, carry):
        attend_block(i, False)
        return carry

    lax.fori_loop(2, nkb, attend_rest, 0)
    for h in heads:
        ot_ref[hs[h], :] = ot_ref[hs[h], :] * (1.0 / l_ref[h:h + 1, :])
    o_ref[...] = ot_ref[...].T


def _dsa_blocks(qa, qi, qmisc, k, v, ki, corr, *, nb, tq, topk, wi_col):
    n, w = qa.shape
    seq = k.shape[1]
    nq = seq // tq
    kb = 2 * LANES
    row = lambda b, g: (b * nq + g, 0)
    per_b = lambda b, g: (b, 0, 0)
    return pl.pallas_call(
        functools.partial(_dsa_blocks_kernel, topk=topk, tq=tq, seq=seq, wi_col=wi_col),
        grid=(nb, nq),
        in_specs=[pl.BlockSpec((tq, w), row),
                  pl.BlockSpec((tq, qi.shape[1]), row),
                  pl.BlockSpec((tq, qmisc.shape[1]), row),
                  pl.BlockSpec((1, seq, w), per_b),
                  pl.BlockSpec((1, seq, w), per_b),
                  pl.BlockSpec((1, seq, ki.shape[2]), per_b),
                  pl.BlockSpec(corr.shape, lambda b, g: (0, 0, 0, 0))],
        out_specs=pl.BlockSpec((tq, w), row),
        out_shape=jax.ShapeDtypeStruct(qa.shape, F32),
        scratch_shapes=[pltpu.VMEM((seq, w), BF16),
                        pltpu.VMEM((seq // LANES, w, LANES), BF16),
                        pltpu.VMEM((seq, ki.shape[2]), BF16),
                        pltpu.VMEM((seq // kb, kb, tq), jnp.int32),
                        pltpu.VMEM((seq // kb, kb, tq), F32),
                        pltpu.VMEM((SUBLANES, tq), jnp.int32),
                        pltpu.VMEM((SUBLANES, tq), F32),
                        pltpu.VMEM((N_HEADS, tq), F32),
                        pltpu.VMEM((N_HEADS, tq), F32),
                        pltpu.VMEM((w, tq), F32)],
        compiler_params=_params("parallel", "arbitrary"),
        name="dsa_attention_prompt",
    )(qa, qi, qmisc, k, v, ki, corr)


def _unit_lower_inverse(mats, n):
    row, col = _iota((n, n), 0), _iota((n, n), 1)
    same = lambda width: row // width == col // width
    dot = functools.partial(jnp.dot, preferred_element_type=F32)
    eye = jnp.where(row == col, 1.0, 0.0)
    diag = [jnp.where(same(INV_BLOCK), a, 0.0) for a in mats]
    ts = [eye - d for d in diag]
    dks = [_split(d) for d in diag]
    k = 1
    while 2 * k < INV_BLOCK:
        dks = [_split(_dot3(dk, dk)) for dk in dks]
        ts = [t + _dot3(_split(t), dk) for t, dk in zip(ts, dks)]
        k *= 2
    width = INV_BLOCK
    while width < n:
        off = jnp.logical_and(same(2 * width), jnp.logical_not(same(width)))
        ls = [jnp.where(off, a, 0.0).astype(BF16) for a in mats]
        tb = [t.astype(BF16) for t in ts]
        tl = [dot(t, l).astype(BF16) for t, l in zip(tb, ls)]
        ts = [t - dot(x, y) for t, x, y in zip(ts, tl, tb)]
        width *= 2
    return ts


def _head_rms(x, w):
    return x * lax.rsqrt(jnp.mean(x * x, axis=-1, keepdims=True) + 1e-6) * w


def _gdn_multi_kernel(qkv_ref, misc_ref, z_ref, hist_ref, s0_ref, cw_ref, alog_ref, dtb_ref, nw_ref,
                      y_ref, sfin_ref, newhist_ref, ext_ref, s_ref, *, c, a_col, b_col):
    j = pl.program_id(1)
    hw = CONV_B - 1
    base = SUBLANES
    wq = N_HEADS * DK_B
    seqs = range(qkv_ref.shape[0])

    @pl.when(j == 0)
    def _():
        ext_ref[:, base - hw:base, :] = hist_ref[...]
        s_ref[...] = s0_ref[...]

    @pl.when(j > 0)
    def _():
        ext_ref[:, base - hw:base, :] = ext_ref[:, base + c - hw:base + c, :]

    ext_ref[:, base:base + c, :] = qkv_ref[...]
    newhist_ref[...] = ext_ref[:, base + c - hw:base + c, :]
    ri = _iota((c, c), 0)
    ci = _iota((c, c), 1)
    lower = ri >= ci
    strict = ri > ci
    tri = jnp.where(lower, 1.0, 0.0)
    eye_h = jnp.where(_iota((N_HEADS, N_HEADS), 0) == _iota((N_HEADS, N_HEADS), 1), 1.0, 0.0)

    def conv_act(b):
        conv = None
        for k in range(CONV_B):
            term = ext_ref[b, base - hw + k:base - hw + k + c, :] * cw_ref[k:k + 1, :]
            conv = term if conv is None else conv + term
        return conv * _sigmoid(conv)

    act = [conv_act(b) for b in seqs]
    beta = [_sigmoid(misc_ref[b, :, b_col:b_col + N_HEADS]) for b in seqs]
    g = [-jnp.exp(alog_ref[...]) * _softplus(misc_ref[b, :, a_col:a_col + N_HEADS] + dtb_ref[...]) for b in seqs]
    gc = [_dot_hi(tri, x) for x in g]
    gc_t = [_dot_nt(eye_h, x, HI) for x in gc]
    eg = [jnp.exp(x) for x in gc]
    e_last = [jnp.exp(x[c - 1:c, :]) for x in gc]
    e_rest = [jnp.exp(x[c - 1:c, :] - x) for x in gc]

    items = [(b, h) for b in seqs for h in range(N_HEADS)]
    col = lambda x, h: x[:, h:h + 1]
    qs = [act[b][:, h * DK_B:(h + 1) * DK_B] for b, h in items]
    ks = [act[b][:, wq + h * DK_B:wq + (h + 1) * DK_B] for b, h in items]
    vs = [act[b][:, 2 * wq + h * DV_B:2 * wq + (h + 1) * DV_B] for b, h in items]
    qs = [q * lax.rsqrt(jnp.sum(q * q, axis=-1, keepdims=True) + 1e-6) * (DK_B ** -0.5) for q in qs]
    ks = [k * lax.rsqrt(jnp.sum(k * k, axis=-1, keepdims=True) + 1e-6) for k in ks]
    kbs = [k.astype(BF16) for k in ks]
    decay = [jnp.where(lower, jnp.exp(jnp.where(lower, col(gc[b], h) - gc_t[b][h:h + 1, :], 0.0)), 0.0)
             for b, h in items]
    kk = [_dot_nt(k, k) for k in kbs]
    attn = [_dot_nt(q.astype(BF16), k) * d for q, k, d in zip(qs, kbs, decay)]
    a_mat = [jnp.where(strict, col(beta[b], h) * x * d, 0.0) for (b, h), x, d in zip(items, kk, decay)]
    t_mat = [t.astype(BF16) for t in _unit_lower_inverse(a_mat, c)]
    value = [_dot(t, v * col(beta[b], h)) for (b, h), t, v in zip(items, t_mat, vs)]
    k_cum = [_dot(t, k * (col(beta[b], h) * col(eg[b], h))) for (b, h), t, k in zip(items, t_mat, ks)]
    s_old = [s_ref[b, h] for b, h in items]
    sbs = [s.astype(BF16) for s in s_old]
    v_new = [(v - _dot(kc, s)).astype(BF16) for v, kc, s in zip(value, k_cum, sbs)]
    o_inter = [_dot(q * col(eg[b], h), s) for (b, h), q, s in zip(items, qs, sbs)]
    o = [oi + _dot(a, v) for oi, a, v in zip(o_inter, attn, v_new)]
    for i, (b, h) in enumerate(items):
        s_ref[b, h] = s_old[i] * col(e_last[b], h) + _dot_tn((ks[i] * col(e_rest[b], h)).astype(BF16), v_new[i])
    for i, (b, h) in enumerate(items):
        zh = z_ref[b, :, h * DV_B:(h + 1) * DV_B]
        y_ref[b, :, h * DV_B:(h + 1) * DV_B] = _head_rms(o[i], nw_ref[...]) * (zh * _sigmoid(zh))
    sfin_ref[...] = s_ref[...]


def _gdn(qkv, misc, z, hist, s0, conv_w, a_log, dt_bias, norm_w, *, nb, c, a_col, b_col):
    n, wqkv = qkv.shape
    seq = n // nb
    nbb = GDN_SEQS_PER_STEP if nb % GDN_SEQS_PER_STEP == 0 else 1
    tok = lambda b, j: (b, j, 0)
    per_b3 = lambda b, j: (b, 0, 0)
    per_b4 = lambda b, j: (b, 0, 0, 0)
    const2 = lambda b, j: (0, 0)
    hw = CONV_B - 1
    tokens = [a.reshape(nb, seq, a.shape[1]) for a in (qkv, misc, z)]
    y, s_fin, new_hist = pl.pallas_call(
        functools.partial(_gdn_multi_kernel, c=c, a_col=a_col, b_col=b_col),
        grid=(nb // nbb, seq // c),
        in_specs=[pl.BlockSpec((nbb, c, a.shape[2]), tok) for a in tokens]
                 + [pl.BlockSpec((nbb, hw, wqkv), per_b3),
                    pl.BlockSpec((nbb,) + s0.shape[1:], per_b4),
                    pl.BlockSpec((CONV_B, wqkv), const2),
                    pl.BlockSpec((1, N_HEADS), const2),
                    pl.BlockSpec((1, N_HEADS), const2),
                    pl.BlockSpec((1, DV_B), const2)],
        out_specs=[pl.BlockSpec((nbb, c, z.shape[1]), tok),
                   pl.BlockSpec((nbb,) + s0.shape[1:], per_b4),
                   pl.BlockSpec((nbb, hw, wqkv), per_b3)],
        out_shape=[jax.ShapeDtypeStruct((nb, seq, z.shape[1]), F32),
                   jax.ShapeDtypeStruct(s0.shape, F32),
                   jax.ShapeDtypeStruct(hist.shape, F32)],
        scratch_shapes=[pltpu.VMEM((nbb, SUBLANES + c, wqkv), F32),
                        pltpu.VMEM((nbb,) + s0.shape[1:], F32)],
        compiler_params=_params("parallel", "arbitrary"),
        name="gated_deltanet",
    )(*tokens, hist, s0, conv_w, a_log.reshape(1, -1), dt_bias.reshape(1, -1), norm_w.reshape(1, -1))
    return y.reshape(n, z.shape[1]), s_fin, new_hist


def _block_mask(rows, row_group, cols, col_group):
    return (np.arange(rows)[:, None] // row_group == np.arange(cols)[None, :] // col_group).astype(np.float32)


def _cummax_rows(x):
    rows = x.shape[0]
    row = _iota(x.shape, 0)
    sh = 1
    while sh < rows:
        x = jnp.maximum(x, jnp.where(row >= sh, pltpu.roll(x, sh, axis=0), -jnp.inf))
        sh *= 2
    return x


def _dot2(a, b):
    hi, lo = _split(a)
    return jnp.dot(hi, b, preferred_element_type=F32) + jnp.dot(lo, b, preferred_element_type=F32)


def _mlstm_dense_kernel(qk_ref, v_ref, og_ref, misc_ref, c0_ref, n0_ref, m0_ref, ib_ref, fb_ref, nw_ref,
                        el_ref, ev_ref, ek_ref, kmask_ref, vmask_ref, cmask_ref, cmaskb_ref, rms_ref, causal_ref,
                        dsel_ref,
                        y_ref, cfin_ref, nfin_ref, mfin_ref, c_ref, n_ref, m_ref, *, l, i_col, f_col):
    j = pl.program_id(1)
    wq = N_HEADS * DK_C
    nbb = qk_ref.shape[0]
    seqs = range(nbb)

    @pl.when(j == 0)
    def _():
        c_ref[...] = jnp.zeros(c_ref.shape, F32)
        for b in seqs:
            for h in range(N_HEADS):
                c_ref[b, h * DK_C:(h + 1) * DK_C, h * DV_C:(h + 1) * DV_C] = c0_ref[b, h]
        n_ref[...] = n0_ref[...]
        m_ref[...] = m0_ref[...]

    each = lambda f, *xs: [f(*args) for args in zip(*xs)]
    dot = functools.partial(jnp.dot, preferred_element_type=F32)
    spread = lambda x, e_ref: sum(dot(p, e_ref[...]) for p in _split3(x))
    tri = jnp.where(_iota((l, l), 0) >= _iota((l, l), 1), 1.0, 0.0).astype(BF16)
    ig = [misc_ref[b, :, i_col:i_col + N_HEADS] + ib_ref[...] for b in seqs]
    lf = [-_softplus(-(misc_ref[b, :, f_col:f_col + N_HEADS] + fb_ref[...])) for b in seqs]
    fc = each(lambda x: sum(dot(tri, p) for p in _split3(x)), lf)
    m_prev = [m_ref[b] for b in seqs]
    log_inter = each(jnp.add, fc, m_prev)
    a = each(jnp.subtract, ig, fc)
    m_t = each(lambda li, f, x: jnp.maximum(li, f + _cummax_rows(x)), log_inter, fc, a)
    w_inter = each(lambda li, m: jnp.exp(li - m), log_inter, m_t)
    w_last = each(lambda f, i, m: jnp.exp(f[l - 1:l, :] - f + i - m[l - 1:l, :]), fc, ig, m_t)

    key_terms = each(lambda x: jnp.sum(spread(x, el_ref) * dsel_ref[...], axis=0, keepdims=True), a)
    log_w = each(lambda f, kt: jnp.where(causal_ref[...] > 0.5, spread(f, el_ref) + kt, -jnp.inf), fc, key_terms)
    q = [qk_ref[b, :, :wq] for b in seqs]
    q_b = [x.astype(BF16) for x in q]
    k_s = [qk_ref[b, :, wq:] * (DK_C ** -0.5) for b in seqs]
    k_b = [x.astype(BF16) for x in k_s]
    v_b = [v_ref[b].astype(BF16) for b in seqs]
    kt_bd = [jnp.concatenate([x] * N_HEADS, axis=0) * kmask_ref[...] for x in k_b]
    v_bd = [jnp.concatenate([x] * N_HEADS, axis=0) * vmask_ref[...] for x in v_b]
    m_l = each(lambda m: spread(m, el_ref), m_t)
    qkw = each(lambda x, kt, lw, m: (_dot_nt(x, kt) * jnp.exp(lw - m)).astype(BF16), q_b, kt_bd, log_w, m_l)
    c_old = [c_ref[b] for b in seqs]
    n_old = [n_ref[b] for b in seqs]
    wi_s = each(lambda w: spread(w, ev_ref), w_inter)
    q_c = each(lambda x, c: dot(x, c.astype(BF16)), q_b, c_old)
    qn = each(lambda x, n: _dot2(x * n, cmaskb_ref[...]), q, n_old)
    num = each(lambda w, vb, wi, qc: dot(w, vb) + wi * qc, qkw, v_bd, wi_s, q_c)
    den = each(lambda w, wi, x: dot(w, vmask_ref[...]) + wi * x, qkw, wi_s, qn)
    m_v = m_l if l == DV_C else each(lambda m: spread(m, ev_ref), m_t)
    hid = each(lambda nu, de, m: nu / jnp.maximum(jnp.abs(de), jnp.exp(-m)), num, den, m_v)
    ms = each(lambda x: _dot2(x * x, rms_ref[...]), hid)
    for b in seqs:
        y_ref[b] = hid[b] * lax.rsqrt(ms[b] + 1e-6) * nw_ref[...] * _sigmoid(og_ref[b])

    kw = each(lambda k, w: k * spread(w, ek_ref), k_s, w_last)
    dec_rows = [jnp.broadcast_to(w[l - 1:l, :], (SUBLANES, N_HEADS)) for w in w_inter]
    upd = each(lambda x, vb: _dot_tn(x.astype(BF16), vb), kw, v_b)
    for b in seqs:
        c_ref[b] = c_old[b] * spread(dec_rows[b], ev_ref)[0:1, :] + upd[b] * cmask_ref[...]
        n_ref[b] = n_old[b] * spread(dec_rows[b], ek_ref)[0:1, :] + jnp.sum(kw[b], axis=0, keepdims=True)
        m_ref[b] = m_t[b][l - 1:l, :]

    @pl.when(j == pl.num_programs(1) - 1)
    def _():
        for b in seqs:
            for h in range(N_HEADS):
                cfin_ref[b, h] = c_ref[b, h * DK_C:(h + 1) * DK_C, h * DV_C:(h + 1) * DV_C]
        nfin_ref[...] = n_ref[...]
        mfin_ref[...] = m_ref[...]


def _mlstm_dense(qk, v, og, misc, c0, n0, m0, i_bias, f_bias, norm_w, *, nb, l, i_col, f_col):
    n = qk.shape[0]
    seq = n // nb
    wq, wv = N_HEADS * DK_C, N_HEADS * DV_C
    hl = N_HEADS * l
    nbb = SEQS_PER_STEP if nb % SEQS_PER_STEP == 0 else 1
    tok = lambda b, j: (b, j, 0)
    per_b3 = lambda b, j: (b, 0, 0)
    per_b4 = lambda b, j: (b, 0, 0, 0)
    const2 = lambda b, j: (0, 0)
    consts = [jnp.asarray(_block_mask(N_HEADS, 1, hl, l), BF16),
              jnp.asarray(_block_mask(N_HEADS, 1, wv, DV_C), BF16),
              jnp.asarray(_block_mask(N_HEADS, 1, wq, DK_C), BF16),
              jnp.asarray(_block_mask(hl, l, wq, DK_C), BF16),
              jnp.asarray(_block_mask(hl, l, wv, DV_C), BF16),
              jnp.asarray(_block_mask(wq, DK_C, wv, DV_C)),
              jnp.asarray(_block_mask(wq, DK_C, wv, DV_C), BF16),
              jnp.asarray(_block_mask(wv, DV_C, wv, DV_C) / DV_C, BF16),
              jnp.asarray(np.tile(np.tril(np.ones((l, l), np.float32)), (1, N_HEADS))),
              jnp.asarray(np.tile(np.eye(l, dtype=np.float32), (1, N_HEADS)))]
    tokens = [a.reshape(nb, seq, a.shape[1]) for a in (qk, v, og, misc)]
    outs = pl.pallas_call(
        functools.partial(_mlstm_dense_kernel, l=l, i_col=i_col, f_col=f_col),
        grid=(nb // nbb, seq // l),
        in_specs=[pl.BlockSpec((nbb, l, a.shape[2]), tok) for a in tokens]
                 + [pl.BlockSpec((nbb,) + c0.shape[1:], per_b4),
                    pl.BlockSpec((nbb, 1, wq), per_b3),
                    pl.BlockSpec((nbb, 1, N_HEADS), per_b3),
                    pl.BlockSpec((1, N_HEADS), const2),
                    pl.BlockSpec((1, N_HEADS), const2),
                    pl.BlockSpec((1, wv), const2)]
                 + [pl.BlockSpec(c.shape, const2) for c in consts],
        out_specs=[pl.BlockSpec((nbb, l, wv), tok),
                   pl.BlockSpec((nbb,) + c0.shape[1:], per_b4),
                   pl.BlockSpec((nbb, 1, wq), per_b3),
                   pl.BlockSpec((nbb, 1, N_HEADS), per_b3)],
        out_shape=[jax.ShapeDtypeStruct((nb, seq, wv), F32),
                   jax.ShapeDtypeStruct(c0.shape, F32),
                   jax.ShapeDtypeStruct((nb, 1, wq), F32),
                   jax.ShapeDtypeStruct((nb, 1, N_HEADS), F32)],
        scratch_shapes=[pltpu.VMEM((nbb, wq, wv), F32),
                        pltpu.VMEM((nbb, 1, wq), F32),
                        pltpu.VMEM((nbb, 1, N_HEADS), F32)],
        compiler_params=_params("parallel", "arbitrary"),
        name="mlstm",
    )(*tokens, c0, n0.reshape(nb, 1, wq), m0.reshape(nb, 1, N_HEADS), i_bias.reshape(1, -1), f_bias.reshape(1, -1),
      jnp.tile(norm_w, N_HEADS).reshape(1, wv), *consts)
    y, c_fin, n_fin, m_fin = outs
    return y.reshape(n, wv), c_fin, n_fin.reshape(nb, N_HEADS, DK_C), m_fin.reshape(nb, N_HEADS)


def _band_prompt_kernel(q_ref, k_ref, v_ref, bias_ref, o_ref, kb_ref, vt_ref, ot_ref, *, tq, seq):
    g = pl.program_id(1)
    w = q_ref.shape[1]
    pad = BAND_CHUNKS * CHUNK
    lw = pad + tq
    npad = pad // LANES
    per_tile = tq // LANES
    blk = _row_tile(seq, 4 * LANES)

    @pl.when(g == 0)
    def _():
        kb_ref[0:pad, :] = jnp.zeros((pad, w), BF16)
        vt_ref[0:npad] = jnp.zeros((npad, w, LANES), BF16)
        for r0 in range(0, seq, blk):
            kb_ref[pad + r0:pad + r0 + blk, :] = k_ref[0, r0:r0 + blk, :].astype(BF16)
        for j in range(seq // LANES):
            vt_ref[npad + j] = v_ref[0, j * LANES:(j + 1) * LANES, :].T.astype(BF16)

    start = pl.multiple_of(g * tq, tq)
    q = (q_ref[...] * (HEAD_DIM ** -0.5)).astype(BF16)
    before_seq = jnp.where(_iota((lw, tq), 0) >= pad - g * tq, 0.0, NEG)
    group = 4
    for h0 in range(0, N_HEADS, group):
        heads = range(h0, h0 + group)
        hs = {h: slice(h * HEAD_DIM, (h + 1) * HEAD_DIM) for h in heads}
        s = [_dot_nt(kb_ref[pl.ds(start, lw), hs[h]], q[:, hs[h]]) + (bias_ref[h] + before_seq) for h in heads]
        p = [jnp.exp(x - _reduce_rows(x, jnp.maximum, jnp.max)) for x in s]
        inv = [1.0 / _reduce_rows(x, jnp.add, jnp.sum) for x in p]
        pb = [x.astype(BF16) for x in p]
        acc = [sum(jnp.dot(vt_ref[g * per_tile + j, hs[h], :], x[j * LANES:(j + 1) * LANES, :],
                           preferred_element_type=F32) for j in range(lw // LANES)) for h, x in zip(heads, pb)]
        for h, a, r in zip(heads, acc, inv):
            ot_ref[hs[h], :] = a * r
    o_ref[...] = ot_ref[...].T


def _band_prompt(q, k, v, bias, *, nb, tq):
    n, w = q.shape
    seq = k.shape[1]
    nq = seq // tq
    pad = BAND_CHUNKS * CHUNK
    row = lambda b, g: (b * nq + g, 0)
    per_b = lambda b, g: (b, 0, 0)
    return pl.pallas_call(
        functools.partial(_band_prompt_kernel, tq=tq, seq=seq),
        grid=(nb, nq),
        in_specs=[pl.BlockSpec((tq, w), row),
                  pl.BlockSpec((1, seq, w), per_b),
                  pl.BlockSpec((1, seq, w), per_b),
                  pl.BlockSpec(bias.shape, lambda b, g: (0, 0, 0))],
        out_specs=pl.BlockSpec((tq, w), row),
        out_shape=jax.ShapeDtypeStruct(q.shape, F32),
        scratch_shapes=[pltpu.VMEM((pad + seq, w), BF16),
                        pltpu.VMEM(((pad + seq) // LANES, w, LANES), BF16),
                        pltpu.VMEM((w, tq), F32)],
        compiler_params=_params("parallel", "arbitrary"),
        name="band_attention_prompt",
    )(q, k, v, bias)


def _band_sample_kernel(q_ref, kc_ref, vc_ref, kn_ref, vn_ref, biasc_ref, biasn_ref, o_ref, *, tq, nbs):
    streams = range(nbs)
    rows = [slice(b * tq, (b + 1) * tq) for b in streams]
    dot = functools.partial(jnp.dot, preferred_element_type=F32)
    for h in range(N_HEADS):
        hs = slice(h * HEAD_DIM, (h + 1) * HEAD_DIM)
        qh = [q_ref[rows[b], hs].astype(BF16) for b in streams]
        s_c = [dot(qh[b], kc_ref[0, b, h].astype(BF16)) * (HEAD_DIM ** -0.5) + biasc_ref[h] for b in streams]
        s_n = [dot(qh[b], kn_ref[b, h].astype(BF16)) * (HEAD_DIM ** -0.5) + biasn_ref[h] for b in streams]
        m = [jnp.maximum(jnp.max(x, axis=-1, keepdims=True), jnp.max(y, axis=-1, keepdims=True))
             for x, y in zip(s_c, s_n)]
        p_c = [jnp.exp(x - mm) for x, mm in zip(s_c, m)]
        p_n = [jnp.exp(x - mm) for x, mm in zip(s_n, m)]
        den = [jnp.sum(x, axis=-1, keepdims=True) + jnp.sum(y, axis=-1, keepdims=True) for x, y in zip(p_c, p_n)]
        o = [_dot_nt(p_c[b].astype(BF16), vc_ref[0, b, h].astype(BF16))
             + _dot_nt(p_n[b].astype(BF16), vn_ref[b, h].astype(BF16)) for b in streams]
        for b in streams:
            o_ref[rows[b], hs] = o[b] / den[b]


def _band_sample(q, kc, vc, layer, kn, vn, bias_c, bias_n, *, nb, tq):
    n, w = q.shape
    nbs = SAMPLE_STREAMS_PER_STEP if nb % SAMPLE_STREAMS_PER_STEP == 0 else 1
    row = lambda b: (b, 0)
    const3 = lambda b: (0, 0, 0)
    return pl.pallas_call(
        functools.partial(_band_sample_kernel, tq=tq, nbs=nbs),
        grid=(nb // nbs,),
        in_specs=[pl.BlockSpec((nbs * tq, w), row)]
                 + [pl.BlockSpec((1, nbs) + a.shape[2:], lambda b: (layer, b, 0, 0, 0)) for a in (kc, vc)]
                 + [pl.BlockSpec((nbs,) + a.shape[1:], lambda b: (b, 0, 0, 0)) for a in (kn, vn)]
                 + [pl.BlockSpec(bias_c.shape, const3), pl.BlockSpec(bias_n.shape, const3)],
        out_specs=pl.BlockSpec((nbs * tq, w), row),
        out_shape=jax.ShapeDtypeStruct(q.shape, F32),
        compiler_params=_params("parallel"),
        name="band_attention_sample",
    )(q, kc, vc, kn, vn, bias_c, bias_n)


def _t5_bucket(rel):
    nb = T5_BUCKETS // 2
    max_exact = nb // 2
    n = jnp.abs(rel)
    n_f = jnp.maximum(n, 1).astype(jnp.float32)
    large = max_exact + (jnp.log(n_f / max_exact) / math.log(T5_MAX_DIST / max_exact) * (nb - max_exact)).astype(jnp.int32)
    large = jnp.minimum(large, nb - 1)
    return jnp.where(rel > 0, nb, 0) + jnp.where(n < max_exact, n, large)


def _toeplitz_bias(fn, n_rows, n_cols):
    n = -(-(n_rows + n_cols - 1) // LANES) * LANES + 1
    m = np.arange(n)
    f = jnp.transpose(fn(np.where(m < n_cols, m, m - n))).astype(F32)
    flat = jnp.tile(f, (1, n_rows))[:, :n_rows * (n - 1)]
    return flat.reshape(f.shape[0], n_rows, n - 1)[:, :, :n_cols]


def _pack_cols(w, sizes, groups):
    offs = np.concatenate([[0], np.cumsum(sizes)])
    cols, widths = [], []
    for grp in groups:
        width = 0
        for idx in grp:
            cols.append(w[:, offs[idx]:offs[idx + 1]])
            width += sizes[idx]
        pad = (-width) % LANES
        if pad:
            cols.append(jnp.zeros((w.shape[0], pad), w.dtype))
        widths.append(width + pad)
    return jnp.concatenate(cols, axis=1).astype(BF16), tuple(widths)


def _row_tile(n, target):
    t = min(n, target)
    while n % t:
        t //= 2
    return t


def kernel(x_prompt, x_sample, cache_a_k, cache_a_v, cache_a_kidx, state_b_s, state_b_conv, state_c_c, state_c_n, state_c_m, cache_d_k, cache_d_v, state_ffn_conv, w_in_even, w_out_even, t5_bias, b_conv_w, b_a_log, b_dt_bias, b_norm_w, w_in_odd, w_out_odd, c_i_bias, c_f_bias, c_norm_w, d_rel_bias, ln_mix_g, ln_mix_b, ln_ffn_g, ln_ffn_b, ffn_w_up, ffn_conv_w, ffn_w_down):
    bp, sp, d = x_prompt.shape
    bs, ts, _ = x_sample.shape
    depth = ffn_w_up.shape[0]
    past = cache_a_k.shape[2]
    d_win = cache_d_k.shape[2]
    dff = ffn_w_down.shape[1]
    alpha = (2 * depth) ** 0.25
    w_a = N_HEADS * HEAD_DIM
    w_b = N_HEADS * DV_B
    w_c = N_HEADS * DV_C
    qkv_b_w = 2 * N_HEADS * DK_B + w_b
    even_sizes = (w_a, w_a, w_a, N_IDX_HEADS * D_IDX, D_IDX, N_IDX_HEADS, qkv_b_w, N_HEADS, N_HEADS, w_b)
    odd_sizes = (N_HEADS * DK_C, N_HEADS * DK_C, w_c, N_HEADS, N_HEADS, w_c, w_a, w_a, w_a)
    even_groups = ((0,), (1,), (2,), (3,), (6,), (9,), (4, 5, 7, 8))
    wi_col, a_col, b_col = D_IDX, D_IDX + N_IDX_HEADS, D_IDX + N_IDX_HEADS + N_HEADS
    odd_groups = ((0, 1), (2,), (5,), (6,), (7,), (8,), (3, 4))
    i_col, f_col = 0, N_HEADS

    assert sp % CHUNK == 0 and ts <= CHUNK and past % CHUNK == 0 and past >= T5_FAR
    assert (past + ts - 1) // CHUNK == past // CHUNK
    topk_p = min(TOPK_MAX, sp // 4)
    topk_s = min(TOPK_MAX, (past + ts) // 4)
    n_p, n_s = bp * sp, bs * ts
    tm_p = _row_tile(n_p, 512)
    tm_s = _row_tile(n_s, 512)
    tff_p = _row_tile(sp, 512)
    ns_s = _row_tile(bs, max(1, 256 // ts))

    t5 = lambda rel: t5_bias[_t5_bucket(jnp.asarray(rel, jnp.int32))]
    fbias = t5(np.array([-T5_FAR - 1]))
    tq_dsa = 2 * LANES
    assert sp % tq_dsa == 0 and T5_FAR <= tq_dsa
    corr_p = jnp.stack([_toeplitz_bias(lambda dd: t5(-dd - tq_dsa * i) - fbias, tq_dsa, tq_dsa) for i in range(2)])
    tq_a = 2 * CHUNK
    ln_s = T5_FAR + ts
    nbias_s = _toeplitz_bias(lambda dd: t5(dd - T5_FAR), ts, ln_s)[None]

    lw = BAND_CHUNKS * CHUNK + tq_a
    r_chunk = np.arange(lw)[:, None] // CHUNK
    q_chunk = BAND_CHUNKS + np.arange(tq_a)[None, :] // CHUNK
    band_ok = (r_chunk >= q_chunk - BAND_CHUNKS) & (r_chunk <= q_chunk)
    pos_q = past + np.arange(ts)
    pos_kc = past - d_win + np.arange(d_win)
    def band_valid(pos_k):
        kch, qch = pos_k // CHUNK, pos_q // CHUNK
        return (pos_k[None] >= 0) & (kch[None] >= qch[:, None] - BAND_CHUNKS) & (kch[None] <= qch[:, None])
    def band_bias(table, shift, n_rows, n_cols, valid=None):
        bias = _toeplitz_bias(lambda dd: table[np.clip(dd + shift, -REL_CLIP, REL_CLIP) + REL_CLIP], n_rows, n_cols)
        return bias if valid is None else jnp.where(jnp.asarray(valid)[None], bias, NEG)

    cak_t = jnp.transpose(cache_a_k, (0, 1, 3, 4, 2))
    cav_t = jnp.transpose(cache_a_v, (0, 1, 3, 4, 2))
    caki_t = jnp.transpose(cache_a_kidx, (0, 1, 3, 2))
    cdk_t = jnp.transpose(cache_d_k, (0, 1, 3, 4, 2))
    cdv_t = jnp.transpose(cache_d_v, (0, 1, 3, 4, 2))

    xp = x_prompt.reshape(n_p, d)
    xs = x_sample.reshape(n_s, d)
    outs = {k: [] for k in ("ak_p", "ak_s", "av_p", "av_s", "aki_p", "aki_s", "bs_p", "bs_s", "bc_p", "bc_s",
                            "cc_p", "cc_s", "cn_p", "cn_s", "cm_p", "cm_s", "dk_p", "dk_s", "dv_p", "dv_s",
                            "fc_p", "fc_s")}
    for layer in range(depth):
        if layer % 2 == 0:
            e = layer // 2
            w_in, widths = _pack_cols(w_in_even[e], even_sizes, even_groups)
            w_out = w_out_even[e].astype(BF16)
            qa, ka, va, qi, qkv_b, z_b, misc = _proj(xp, w_in, widths, tm_p)
            o_a = _dsa_blocks(qa, qi, misc, ka.reshape(bp, sp, w_a), va.reshape(bp, sp, w_a),
                              misc.reshape(bp, sp, LANES), corr_p, nb=bp, tq=tq_dsa, topk=topk_p, wi_col=wi_col)
            y_b, s_b, h_b = _gdn(qkv_b, misc, z_b, jnp.zeros((bp, CONV_B - 1, qkv_b_w), F32),
                                 jnp.zeros((bp, N_HEADS, DK_B, DV_B), F32), b_conv_w[e], b_a_log[e], b_dt_bias[e],
                                 b_norm_w[e], nb=bp, c=CHUNK, a_col=a_col, b_col=b_col)
            xp = _mm_res_ln([o_a, y_b], [w_out[:w_a], w_out[w_a:]], xp, ln_mix_g[layer], ln_mix_b[layer], alpha, tm_p)
            outs["ak_p"].append(ka.reshape(bp, sp, N_HEADS, HEAD_DIM))
            outs["av_p"].append(va.reshape(bp, sp, N_HEADS, HEAD_DIM))
            outs["aki_p"].append(misc[:, :D_IDX].reshape(bp, sp, D_IDX))
            outs["bs_p"].append(s_b)
            outs["bc_p"].append(h_b)
            qa, ka, va, qi, qkv_b, z_b, misc = _proj(xs, w_in, widths, tm_s)
            ki = misc[:, :D_IDX]
            heads_t = lambda u: jnp.transpose(u.reshape(bs, ts, N_HEADS, HEAD_DIM), (0, 2, 3, 1))
            near = (jnp.concatenate([cak_t[e, ..., past - T5_FAR:], heads_t(ka)], axis=-1),
                    jnp.concatenate([cav_t[e, ..., past - T5_FAR:], heads_t(va)], axis=-1),
                    jnp.concatenate([caki_t[e, ..., past - T5_FAR:],
                                     jnp.transpose(ki.reshape(bs, ts, D_IDX), (0, 2, 1))], axis=-1))
            o_a = _dsa_sample(qa, qi, misc, cak_t, cav_t, caki_t, e, near, nbias_s, fbias, nb=bs, tq=ts,
                              topk=topk_s, wi_col=wi_col)
            y_b, s_b, h_b = _gdn(qkv_b, misc, z_b, state_b_conv[e], state_b_s[e], b_conv_w[e], b_a_log[e],
                                 b_dt_bias[e], b_norm_w[e], nb=bs, c=ts, a_col=a_col, b_col=b_col)
            xs = _mm_res_ln([o_a, y_b], [w_out[:w_a], w_out[w_a:]], xs, ln_mix_g[layer], ln_mix_b[layer], alpha, tm_s)
            outs["ak_s"].append(ka.reshape(bs, ts, N_HEADS, HEAD_DIM))
            outs["av_s"].append(va.reshape(bs, ts, N_HEADS, HEAD_DIM))
            outs["aki_s"].append(ki.reshape(bs, ts, D_IDX))
            outs["bs_s"].append(s_b)
            outs["bc_s"].append(h_b)
        else:
            o = layer // 2
            w_in, widths = _pack_cols(w_in_odd[o], odd_sizes, odd_groups)
            w_out = w_out_odd[o].astype(BF16)
            qk_c, v_c, o_c, q_d, k_d, v_d, misc = _proj(xp, w_in, widths, tm_p)
            y_c, c_c, c_n, c_m = _mlstm_dense(qk_c, v_c, o_c, misc,jnp.zeros((bp, N_HEADS, DK_C, DV_C), F32),
                                        jnp.zeros((bp, N_HEADS, DK_C), F32), jnp.zeros((bp, N_HEADS), F32),
                                        c_i_bias[o], c_f_bias[o], c_norm_w[o], nb=bp, l=CHUNK, i_col=i_col, f_col=f_col)
            k3 = k_d.reshape(bp, sp, w_a)
            v3 = v_d.reshape(bp, sp, w_a)
            bias_bp = _toeplitz_bias(
                lambda dd: d_rel_bias[o][np.clip(-dd - BAND_CHUNKS * CHUNK, -REL_CLIP, REL_CLIP) + REL_CLIP], lw, tq_a)
            bias_bp = jnp.where(jnp.asarray(band_ok)[None], bias_bp, NEG)
            o_d = _band_prompt(q_d, k3, v3, bias_bp, nb=bp, tq=tq_a)
            xp = _mm_res_ln([y_c, o_d], [w_out[:w_c], w_out[w_c:]], xp, ln_mix_g[layer], ln_mix_b[layer], alpha, tm_p)
            d_win_p = min(BAND_CHUNKS * CHUNK, sp)
            outs["cc_p"].append(c_c)
            outs["cn_p"].append(c_n)
            outs["cm_p"].append(c_m)
            outs["dk_p"].append(k3[:, sp - d_win_p:].reshape(bp, d_win_p, N_HEADS, HEAD_DIM))
            outs["dv_p"].append(v3[:, sp - d_win_p:].reshape(bp, d_win_p, N_HEADS, HEAD_DIM))
            qk_c, v_c, o_c, q_d, k_d, v_d, misc = _proj(xs, w_in, widths, tm_s)
            y_c, c_c, c_n, c_m = _mlstm_dense(qk_c, v_c, o_c, misc,state_c_c[o], state_c_n[o], state_c_m[o],
                                        c_i_bias[o], c_f_bias[o], c_norm_w[o], nb=bs, l=ts, i_col=i_col, f_col=f_col)
            heads_t = lambda u: jnp.transpose(u.reshape(bs, ts, N_HEADS, HEAD_DIM), (0, 2, 3, 1))
            o_d = _band_sample(q_d, cdk_t, cdv_t, o, heads_t(k_d), heads_t(v_d),
                               band_bias(d_rel_bias[o], -d_win, ts, d_win, band_valid(pos_kc)),
                               band_bias(d_rel_bias[o], 0, ts, ts, band_valid(pos_q)), nb=bs, tq=ts)
            xs = _mm_res_ln([y_c, o_d], [w_out[:w_c], w_out[w_c:]], xs, ln_mix_g[layer], ln_mix_b[layer], alpha, tm_s)
            outs["cc_s"].append(c_c)
            outs["cn_s"].append(c_n)
            outs["cm_s"].append(c_m)
            outs["dk_s"].append(k_d.reshape(bs, ts, N_HEADS, HEAD_DIM))
            outs["dv_s"].append(v_d.reshape(bs, ts, N_HEADS, HEAD_DIM))
        w_up = ffn_w_up[layer].astype(BF16)
        w_down = ffn_w_down[layer].astype(BF16)
        act, hist_p = _ffn_up(xp, w_up, ffn_conv_w[layer], jnp.zeros((bp, CONV_FF - 1, 2 * dff), F32),
                              1, tff_p, sp // tff_p)
        xp = _mm_res_ln([act], [w_down], xp, ln_ffn_g[layer], ln_ffn_b[layer], alpha, tm_p)
        act, hist_s = _ffn_up(xs, w_up, ffn_conv_w[layer], state_ffn_conv[layer], ns_s, ts, 1)
        xs = _mm_res_ln([act], [w_down], xs, ln_ffn_g[layer], ln_ffn_b[layer], alpha, tm_s)
        outs["fc_p"].append(hist_p)
        outs["fc_s"].append(hist_s)

    st = lambda k: jnp.stack(outs[k])
    return (xp.reshape(bp, sp, d), xs.reshape(bs, ts, d),
            st("ak_p"), st("ak_s"), st("av_p"), st("av_s"), st("aki_p"), st("aki_s"),
            st("bs_p"), st("bs_s"), st("bc_p"), st("bc_s"),
            st("cc_p"), st("cc_s"), st("cn_p"), st("cn_s"), st("cm_p"), st("cm_s"),
            st("dk_p"), st("dk_s"), st("dv_p"), st("dv_s"),
            st("fc_p"), st("fc_s"))
```

```python
import functools
import math

import numpy as np
import jax
import jax.numpy as jnp
from jax import lax
from jax.experimental import pallas as pl
from jax.experimental.pallas import tpu as pltpu

F32 = jnp.float32
BF16 = jnp.bfloat16
HI = lax.Precision.HIGHEST

CHUNK = 64
HEAD_DIM = 64
N_HEADS = 8
N_IDX_HEADS = 8
D_IDX = 64
TOPK_MAX = 256
T5_BUCKETS = 32
T5_MAX_DIST = 128
DK_B = 64
DV_B = 64
CONV_B = 4
DK_C = 32
DV_C = 64
BAND_CHUNKS = 8
REL_CLIP = 128
CONV_FF = 3
T5_FAR = 128
INV_BLOCK = 8
SAMPLE_STREAMS_PER_STEP = 4
GDN_SEQS_PER_STEP = 2
SEQS_PER_STEP = 4

LANES = 128
SUBLANES = 8
VMEM_LIMIT = 56 * 1024 * 1024

NEG = -1e30
INT_MIN = -2 ** 31


def _params(*sem):
    return pltpu.CompilerParams(dimension_semantics=sem, vmem_limit_bytes=VMEM_LIMIT)


def _dot(a, b):
    return jnp.dot(a.astype(BF16), b.astype(BF16), preferred_element_type=F32)


def _dot_nt(a, b, precision=None):
    return lax.dot_general(a, b, (((1,), (1,)), ((), ())), precision=precision, preferred_element_type=F32)


def _dot_tn(a, b, precision=None):
    return lax.dot_general(a, b, (((0,), (0,)), ((), ())), precision=precision, preferred_element_type=F32)


def _dot_hi(a, b):
    return jnp.dot(a, b, precision=HI, preferred_element_type=F32)


def _split(a):
    hi = a.astype(BF16)
    return hi, (a - hi.astype(F32)).astype(BF16)


def _split3(a):
    p1 = a.astype(BF16)
    r1 = a - p1.astype(F32)
    p2 = r1.astype(BF16)
    return p1, p2, (r1 - p2.astype(F32)).astype(BF16)


def _dot3(a, b):
    a_hi, a_lo = a
    b_hi, b_lo = b
    d = functools.partial(jnp.dot, preferred_element_type=F32)
    return d(a_hi, b_hi) + (d(a_hi, b_lo) + d(a_lo, b_hi))


def _sigmoid(x):
    return 1.0 / (1.0 + jnp.exp(-x))


def _softplus(x):
    return jnp.maximum(x, 0.0) + jnp.log(1.0 + jnp.exp(-jnp.abs(x)))


def _iota(shape, dim):
    return lax.broadcasted_iota(jnp.int32, shape, dim)


def _proj_kernel(x_ref, w_ref, *out_refs, sizes):
    xb = x_ref[...].astype(BF16)
    off = 0
    for o_ref, size in zip(out_refs, sizes):
        o_ref[...] = jnp.dot(xb, w_ref[:, off:off + size], preferred_element_type=F32)
        off += size


def _proj(x2d, w, sizes, tm):
    n, d = x2d.shape
    return pl.pallas_call(
        functools.partial(_proj_kernel, sizes=sizes),
        grid=(n // tm,),
        in_specs=[pl.BlockSpec((tm, d), lambda i: (i, 0)),
                  pl.BlockSpec((d, sum(sizes)), lambda i: (0, 0))],
        out_specs=[pl.BlockSpec((tm, s), lambda i: (i, 0)) for s in sizes],
        out_shape=[jax.ShapeDtypeStruct((n, s), F32) for s in sizes],
        compiler_params=_params("parallel"),
        name="in_proj",
    )(x2d, w)


def _mm_res_ln_kernel(*refs, nparts, alpha):
    part_refs = refs[:nparts]
    w_refs = refs[nparts:2 * nparts]
    x_ref, g_ref, b_ref, o_ref = refs[2 * nparts:]
    acc = alpha * x_ref[...]
    for p_ref, w_ref in zip(part_refs, w_refs):
        acc = acc + jnp.dot(p_ref[...].astype(BF16), w_ref[...], preferred_element_type=F32)
    mu = jnp.mean(acc, axis=-1, keepdims=True)
    cen = acc - mu
    var = jnp.mean(cen * cen, axis=-1, keepdims=True)
    o_ref[...] = cen * lax.rsqrt(var + 1e-5) * g_ref[...] + b_ref[...]


def _mm_res_ln(parts, ws, x2d, g, b, alpha, tm):
    n, d = x2d.shape
    nparts = len(parts)
    in_specs = ([pl.BlockSpec((tm, p.shape[1]), lambda i: (i, 0)) for p in parts]
                + [pl.BlockSpec(w.shape, lambda i: (0, 0)) for w in ws]
                + [pl.BlockSpec((tm, d), lambda i: (i, 0)),
                   pl.BlockSpec((1, d), lambda i: (0, 0)),
                   pl.BlockSpec((1, d), lambda i: (0, 0))])
    return pl.pallas_call(
        functools.partial(_mm_res_ln_kernel, nparts=nparts, alpha=alpha),
        grid=(n // tm,),
        in_specs=in_specs,
        out_specs=pl.BlockSpec((tm, d), lambda i: (i, 0)),
        out_shape=jax.ShapeDtypeStruct((n, d), F32),
        compiler_params=_params("parallel"),
        name="out_proj_ln",
    )(*parts, *ws, x2d, g.reshape(1, d), b.reshape(1, d))


def _ffn_up_kernel(x_ref, w_ref, cw_ref, hist_ref, act_ref, newhist_ref, ext_ref, *,
                   ns, tt, tiles_per_seq, dff, cc):
    i = pl.program_id(0)
    tm = ns * tt
    hw = CONV_FF - 1
    base = SUBLANES
    if tiles_per_seq == 1:
        ext_ref[:, base - hw:base, :] = hist_ref[...]
    else:
        @pl.when(i % tiles_per_seq == 0)
        def _():
            ext_ref[:, base - hw:base, :] = hist_ref[...]

        @pl.when(i % tiles_per_seq != 0)
        def _():
            ext_ref[:, base - hw:base, :] = ext_ref[:, base + tt - hw:base + tt, :]
    xb = x_ref[...].astype(BF16)
    for j in range(2 * dff // cc):
        cols = slice(j * cc, (j + 1) * cc)
        h = jnp.dot(xb, w_ref[:, cols], preferred_element_type=F32)
        ext_ref[:, base:base + tt, cols] = h.reshape(ns, tt, cc)
    newhist_ref[...] = ext_ref[:, base + tt - hw:base + tt, :]

    def conv(cols):
        acc = None
        for k in range(CONV_FF):
            term = ext_ref[:, base - hw + k:base - hw + k + tt, cols] * cw_ref[k:k + 1, cols]
            acc = term if acc is None else acc + term
        return acc

    for j in range(dff // cc):
        g = conv(slice(j * cc, (j + 1) * cc))
        u = conv(slice(dff + j * cc, dff + (j + 1) * cc))
        act = g * _sigmoid(g) * u
        act_ref[:, j * cc:(j + 1) * cc] = act.reshape(tm, cc).astype(BF16)


def _ffn_up(x2d, w_up, conv_w, hist, ns, tt, tiles_per_seq):
    n, d = x2d.shape
    c2 = w_up.shape[1]
    dff = c2 // 2
    tm = ns * tt
    cc = 256
    hw = CONV_FF - 1
    if tiles_per_seq == 1:
        hist_map = lambda i: (i, 0, 0)
    else:
        hist_map = lambda i: (i // tiles_per_seq, 0, 0)
    return pl.pallas_call(
        functools.partial(_ffn_up_kernel, ns=ns, tt=tt, tiles_per_seq=tiles_per_seq, dff=dff, cc=cc),
        grid=(n // tm,),
        in_specs=[pl.BlockSpec((tm, d), lambda i: (i, 0)),
                  pl.BlockSpec((d, c2), lambda i: (0, 0)),
                  pl.BlockSpec((CONV_FF, c2), lambda i: (0, 0)),
                  pl.BlockSpec((ns, hw, c2), hist_map)],
        out_specs=[pl.BlockSpec((tm, dff), lambda i: (i, 0)),
                   pl.BlockSpec((ns, hw, c2), hist_map)],
        out_shape=[jax.ShapeDtypeStruct((n, dff), BF16),
                   jax.ShapeDtypeStruct(hist.shape, F32)],
        scratch_shapes=[pltpu.VMEM((ns, SUBLANES + tt, c2), F32)],
        compiler_params=_params("arbitrary"),
        name="ffn_up_conv_gate",
    )(x2d, w_up, conv_w, hist)


def _sortable(x):
    b = lax.bitcast_convert_type(x, jnp.int32)
    return b ^ ((b >> 31) & jnp.int32(0x7FFFFFFF))


def _count(mask):
    return jnp.sum(jnp.where(mask, 1.0, 0.0), axis=-1, keepdims=True)


def _dsa_sample_kernel(qa_ref, qi_ref, qm_ref, kf_ref, vf_ref, kif_ref, kn_ref, vn_ref, kin_ref, nbias_ref,
                       fbias_ref, o_ref, self_ref, seln_ref, *, topk, tq, nbs, lf, ln, wi_col):
    start = lf - T5_FAR
    streams = range(nbs)
    rows = [slice(b * tq, (b + 1) * tq) for b in streams]
    dot = functools.partial(jnp.dot, preferred_element_type=F32)
    wi = [qm_ref[rows[b], wi_col:wi_col + N_IDX_HEADS] * (N_IDX_HEADS ** -0.5) * (D_IDX ** -0.5) for b in streams]
    kif = [kif_ref[0, b].astype(BF16) for b in streams]
    kinb = [kin_ref[b].astype(BF16) for b in streams]

    sc_f = [jnp.zeros((tq, lf), F32) for _ in streams]
    sc_n = [jnp.zeros((tq, ln), F32) for _ in streams]
    for n in range(N_IDX_HEADS):
        qn = [qi_ref[rows[b], n * D_IDX:(n + 1) * D_IDX].astype(BF16) for b in streams]
        sc_f = [sc_f[b] + jnp.maximum(dot(qn[b], kif[b]), 0.0) * wi[b][:, n:n + 1] for b in streams]
        sc_n = [sc_n[b] + jnp.maximum(dot(qn[b], kinb[b]), 0.0) * wi[b][:, n:n + 1] for b in streams]
    adm_f = _iota((tq, lf), 1) < start
    adm_n = nbias_ref[0, 0] > 0.5 * NEG
    key_f = [jnp.where(adm_f, _sortable(x), jnp.int32(INT_MIN)) for x in sc_f]
    key_n = [jnp.where(adm_n, _sortable(x), jnp.int32(INT_MIN)) for x in sc_n]

    kf32 = float(topk)

    def body(i, t_us):
        bit = lax.shift_left(jnp.int32(1), 31 - i)
        cand_u = [t | bit for t in t_us]
        cand_s = [c ^ jnp.int32(INT_MIN) for c in cand_u]
        cnt = [_count(key_f[b] >= cand_s[b]) + _count(key_n[b] >= cand_s[b]) for b in streams]
        return tuple(jnp.where(cnt[b] >= kf32, cand_u[b], t_us[b]) for b in streams)

    t_us = lax.fori_loop(0, 32, body, tuple(jnp.zeros((tq, 1), jnp.int32) for _ in streams))
    thr = [t ^ jnp.int32(INT_MIN) for t in t_us]
    need = [kf32 - (_count(key_f[b] > thr[b]) + _count(key_n[b] > thr[b])) for b in streams]
    n_eq = [_count(key_f[b] == thr[b]) + _count(key_n[b] == thr[b]) for b in streams]
    open_row = [t == jnp.int32(INT_MIN) for t in thr]
    for b in streams:
        self_ref[b] = jnp.where(jnp.logical_and(key_f[b] >= thr[b], adm_f), 0.0, NEG)
        seln_ref[b] = jnp.where(jnp.logical_and(key_n[b] >= thr[b], adm_n), 0.0, NEG)

    def resolve_ties(b):
        upper = jnp.where(_iota((LANES, LANES), 0) < _iota((LANES, LANES), 1), 1.0, 0.0).astype(BF16)
        offset = jnp.zeros((tq, 1), F32)
        for ref, key, width in ((self_ref, key_f[b], lf), (seln_ref, key_n[b], ln)):
            for j0 in range(0, width, LANES):
                w = min(LANES, width - j0)
                kb = key[:, j0:j0 + w]
                e = jnp.where(kb == thr[b], 1.0, 0.0)
                rank = offset + dot(e.astype(BF16), upper[:w, :w])
                take = jnp.where(kb > thr[b], 1.0, jnp.where(rank < need[b], e, 0.0))
                take = jnp.where(open_row[b], jnp.where(kb > thr[b], 1.0, 0.0), take)
                ref[b, :, j0:j0 + w] = jnp.where(take > 0.5, 0.0, NEG)
                offset = offset + jnp.sum(e, axis=-1, keepdims=True)

    for b in streams:
        conflict = jnp.logical_and(n_eq[b] != need[b], jnp.logical_not(open_row[b]))
        pl.when(jnp.max(jnp.where(conflict, 1.0, 0.0)) > 0.0)(functools.partial(resolve_ties, b))

    sel_f = [self_ref[b] for b in streams]
    sel_n = [seln_ref[b] for b in streams]
    for h in range(N_HEADS):
        hs = slice(h * HEAD_DIM, (h + 1) * HEAD_DIM)
        qh = [qa_ref[rows[b], hs].astype(BF16) for b in streams]
        s_f = [dot(qh[b], kf_ref[0, b, h].astype(BF16)) * (HEAD_DIM ** -0.5) + fbias_ref[:, h:h + 1] + sel_f[b]
               for b in streams]
        s_n = [dot(qh[b], kn_ref[b, h].astype(BF16)) * (HEAD_DIM ** -0.5) + nbias_ref[0, h] + sel_n[b]
               for b in streams]
        m = [jnp.maximum(jnp.max(x, axis=-1, keepdims=True), jnp.max(y, axis=-1, keepdims=True))
             for x, y in zip(s_f, s_n)]
        p_f = [jnp.exp(x - mm) for x, mm in zip(s_f, m)]
        p_n = [jnp.exp(x - mm) for x, mm in zip(s_n, m)]
        den = [jnp.sum(x, axis=-1, keepdims=True) + jnp.sum(y, axis=-1, keepdims=True) for x, y in zip(p_f, p_n)]
        o = [_dot_nt(p_f[b].astype(BF16), vf_ref[0, b, h].astype(BF16))
             + _dot_nt(p_n[b].astype(BF16), vn_ref[b, h].astype(BF16)) for b in streams]
        for b in streams:
            o_ref[rows[b], hs] = o[b] / den[b]


def _dsa_sample(qa, qi, qmisc, kf, vf, kif, layer, near, nbias, fbias, *, nb, tq, topk, wi_col):
    lf = kf.shape[-1]
    ln = nbias.shape[-1]
    nbs = SAMPLE_STREAMS_PER_STEP if nb % SAMPLE_STREAMS_PER_STEP == 0 else 1
    row = lambda b: (b, 0)
    args = [qa, qi, qmisc, kf, vf, kif, *near]
    in_specs = ([pl.BlockSpec((nbs * tq, a.shape[1]), row) for a in args[:3]]
                + [pl.BlockSpec((1, nbs) + a.shape[2:], lambda b, nd=a.ndim: (layer, b) + (0,) * (nd - 2))
                   for a in args[3:6]]
                + [pl.BlockSpec((nbs,) + a.shape[1:], lambda b, nd=a.ndim: (b,) + (0,) * (nd - 1)) for a in args[6:]]
                + [pl.BlockSpec(nbias.shape, lambda b: (0, 0, 0, 0)),
                   pl.BlockSpec(fbias.shape, lambda b: (0, 0))])
    return pl.pallas_call(
        functools.partial(_dsa_sample_kernel, topk=topk, tq=tq, nbs=nbs, lf=lf, ln=ln, wi_col=wi_col),
        grid=(nb // nbs,),
        in_specs=in_specs,
        out_specs=pl.BlockSpec((nbs * tq, qa.shape[1]), row),
        out_shape=jax.ShapeDtypeStruct(qa.shape, F32),
        scratch_shapes=[pltpu.VMEM((nbs, tq, lf), F32), pltpu.VMEM((nbs, tq, ln), F32)],
        compiler_params=_params("parallel"),
        name="dsa_attention_sample",
    )(*args, nbias, fbias)


def _reduce_rows(x, op, final):
    blk = 8 * SUBLANES
    parts = [x[r0:r0 + blk] for r0 in range(0, x.shape[0], blk)]
    while len(parts) > 1:
        parts = [op(parts[i], parts[i + 1]) for i in range(0, len(parts) - 1, 2)] + parts[len(parts) & ~1:]
    return final(parts[0], axis=0, keepdims=True)


def _block_rows(x, op):
    blk = 8 * SUBLANES
    parts = [x[r0:r0 + blk] for r0 in range(0, x.shape[0], blk)]
    while len(parts) > 1:
        parts = [op(parts[i], parts[i + 1]) for i in range(0, len(parts) - 1, 2)] + parts[len(parts) & ~1:]
    return parts[0]


def _dsa_blocks_kernel(qa_ref, qi_ref, qm_ref, k_ref, v_ref, ki_ref, corr_ref, o_ref,
                       kb_ref, vt_ref, kib_ref, key_ref, sel_ref, thr_ref, stat_ref, m_ref, l_ref, ot_ref, *,
                       topk, tq, seq, wi_col):
    g = pl.program_id(1)
    kb = 2 * LANES
    w = qa_ref.shape[1]
    blk = _row_tile(seq, 4 * LANES)

    @pl.when(g == 0)
    def _():
        for r0 in range(0, seq, blk):
            kb_ref[r0:r0 + blk, :] = k_ref[0, r0:r0 + blk, :].astype(BF16)
            kib_ref[r0:r0 + blk, :] = ki_ref[0, r0:r0 + blk, :].astype(BF16)
        for j in range(seq // LANES):
            vt_ref[j] = v_ref[0, j * LANES:(j + 1) * LANES, :].T.astype(BF16)

    top = (g + 1) * tq
    nkb = (top + kb - 1) // kb
    start_of = lambda i: pl.multiple_of(jnp.maximum(top - kb * (i + 1), 0), LANES)

    qa = (qa_ref[...] * (HEAD_DIM ** -0.5)).astype(BF16)
    qi = qi_ref[...].astype(BF16)
    wi = qm_ref[:, wi_col:wi_col + N_IDX_HEADS] * (N_IDX_HEADS ** -0.5) * (D_IDX ** -0.5)
    eye_h = jnp.where(_iota((N_IDX_HEADS, N_IDX_HEADS), 0) == _iota((N_IDX_HEADS, N_IDX_HEADS), 1), 1.0, 0.0)
    wi_t = _dot_nt(eye_h, wi, HI)
    q_chunk = (g * tq + _iota((1, tq), 1)) // CHUNK

    def score_block(i, carry):
        st = start_of(i)
        kib = kib_ref[pl.ds(st, kb), :][:, :D_IDX]
        acc = jnp.zeros((kb, tq), F32)
        for n in range(N_IDX_HEADS):
            acc = acc + jnp.maximum(_dot_nt(kib, qi[:, n * D_IDX:(n + 1) * D_IDX]), 0.0) * wi_t[n:n + 1, :]
        row = st + _iota((kb, 1), 0)
        ok = jnp.logical_and(row // CHUNK <= q_chunk, row < top - kb * i)
        key_ref[i] = jnp.where(ok, _sortable(acc), jnp.int32(INT_MIN))
        return carry

    lax.fori_loop(0, nkb, score_block, 0)

    kf32 = float(topk)

    def search_blocks(n_blocks):
        def count(pred):
            parts = [_block_rows(jnp.where(pred(key_ref[i]), 1.0, 0.0), jnp.add) for i in range(n_blocks)]
            while len(parts) > 1:
                parts = [a + b for a, b in zip(parts[0::2], parts[1::2])] + parts[len(parts) & ~1:]
            return jnp.sum(parts[0], axis=0, keepdims=True)

        def search(b, t_u):
            cand_u = t_u | lax.shift_left(jnp.int32(1), 31 - b)
            cand_s = cand_u ^ jnp.int32(INT_MIN)
            return jnp.where(count(lambda x: x >= cand_s) >= kf32, cand_u, t_u)

        t_u = lax.fori_loop(0, 32, search, jnp.zeros((1, tq), jnp.int32))
        thr = t_u ^ jnp.int32(INT_MIN)
        thr_ref[0:1, :] = thr
        stat_ref[0:1, :] = kf32 - count(lambda x: x > thr)
        stat_ref[1:2, :] = count(lambda x: x == thr)

    for n_blocks in range(1, seq // kb + 1):
        pl.when(nkb == n_blocks)(functools.partial(search_blocks, n_blocks))

    thr = thr_ref[0:1, :]
    need = stat_ref[0:1, :]
    open_row = thr == jnp.int32(INT_MIN)
    conflict = jnp.logical_and(stat_ref[1:2, :] != need, jnp.logical_not(open_row))
    floor = jnp.maximum(thr, jnp.int32(INT_MIN + 1))

    def select_block(i, carry):
        sel_ref[i] = jnp.where(key_ref[i] >= floor, 0.0, NEG)
        return carry

    lax.fori_loop(0, nkb, select_block, 0)

    @pl.when(jnp.max(jnp.where(conflict, 1.0, 0.0)) > 0.0)
    def _():
        below = jnp.where(_iota((LANES, LANES), 1) < _iota((LANES, LANES), 0), 1.0, 0.0).astype(BF16)

        def tie_block(ii, offset):
            i = nkb - 1 - ii
            for r0 in range(0, kb, LANES):
                kblk = key_ref[i, r0:r0 + LANES, :]
                e = jnp.where(kblk == thr, 1.0, 0.0)
                rank = offset + jnp.dot(below, e.astype(BF16), preferred_element_type=F32)
                take = jnp.where(kblk > thr, 1.0, jnp.where(jnp.logical_or(rank >= need, open_row), 0.0, e))
                sel_ref[i, r0:r0 + LANES, :] = jnp.where(take > 0.5, 0.0, NEG)
                offset = offset + jnp.sum(e, axis=0, keepdims=True)
            return offset

        lax.fori_loop(0, nkb, tie_block, jnp.zeros((1, tq), F32))

    heads = range(N_HEADS)
    hs = [slice(h * HEAD_DIM, (h + 1) * HEAD_DIM) for h in heads]

    def attend_block(i, first, near=None):
        st = start_of(i)
        vblk = st // LANES
        sel = sel_ref[i]
        s = [_dot_nt(kb_ref[pl.ds(st, kb), hs[h]], qa[:, hs[h]]) + sel for h in heads]
        if near is not None:
            s = [s[h] + corr_ref[near, h] for h in heads]
        m_blk = [jnp.max(_block_rows(x, jnp.maximum), axis=0, keepdims=True) for x in s]
        m_old = [m_ref[h:h + 1, :] for h in heads]
        m_new = m_blk if first else [jnp.maximum(a, b) for a, b in zip(m_old, m_blk)]
        p = [jnp.exp(x - m) for x, m in zip(s, m_new)]
        l_blk = [jnp.sum(_block_rows(x, jnp.add), axis=0, keepdims=True) for x in p]
        pb = [x.astype(BF16) for x in p]
        pv = [sum(jnp.dot(vt_ref[vblk + j, hs[h], :], pb[h][j * LANES:(j + 1) * LANES, :],
                          preferred_element_type=F32) for j in range(kb // LANES)) for h in heads]
        for h in heads:
            if first:
                l_ref[h:h + 1, :] = l_blk[h]
                ot_ref[hs[h], :] = pv[h]
            else:
                alpha = jnp.exp(m_old[h] - m_new[h])
                l_ref[h:h + 1, :] = l_ref[h:h + 1, :] * alpha + l_blk[h]
                ot_ref[hs[h], :] = ot_ref[hs[h], :] * alpha + pv[h]
            m_ref[h:h + 1, :] = m_new[h]

    attend_block(0, True, 0)
    pl.when(nkb > 1)(lambda: attend_block(1, False, 1))

    def attend_rest(i, carry):
        attend_block(i, False)
        return carry

    lax.fori_loop(2, nkb, attend_rest, 0)
    for h in heads:
        ot_ref[hs[h], :] = ot_ref[hs[h], :] * (1.0 / l_ref[h:h + 1, :])
    o_ref[...] = ot_ref[...].T


def _dsa_blocks(qa, qi, qmisc, k, v, ki, corr, *, nb, tq, topk, wi_col):
    n, w = qa.shape
    seq = k.shape[1]
    nq = seq // tq
    kb = 2 * LANES
    row = lambda b, g: (b * nq + g, 0)
    per_b = lambda b, g: (b, 0, 0)
    return pl.pallas_call(
        functools.partial(_dsa_blocks_kernel, topk=topk, tq=tq, seq=seq, wi_col=wi_col),
        grid=(nb, nq),
        in_specs=[pl.BlockSpec((tq, w), row),
                  pl.BlockSpec((tq, qi.shape[1]), row),
                  pl.BlockSpec((tq, qmisc.shape[1]), row),
                  pl.BlockSpec((1, seq, w), per_b),
                  pl.BlockSpec((1, seq, w), per_b),
                  pl.BlockSpec((1, seq, ki.shape[2]), per_b),
                  pl.BlockSpec(corr.shape, lambda b, g: (0, 0, 0, 0))],
        out_specs=pl.BlockSpec((tq, w), row),
        out_shape=jax.ShapeDtypeStruct(qa.shape, F32),
        scratch_shapes=[pltpu.VMEM((seq, w), BF16),
                        pltpu.VMEM((seq // LANES, w, LANES), BF16),
                        pltpu.VMEM((seq, ki.shape[2]), BF16),
                        pltpu.VMEM((seq // kb, kb, tq), jnp.int32),
                        pltpu.VMEM((seq // kb, kb, tq), F32),
                        pltpu.VMEM((SUBLANES, tq), jnp.int32),
                        pltpu.VMEM((SUBLANES, tq), F32),
                        pltpu.VMEM((N_HEADS, tq), F32),
                        pltpu.VMEM((N_HEADS, tq), F32),
                        pltpu.VMEM((w, tq), F32)],
        compiler_params=_params("parallel", "arbitrary"),
        name="dsa_attention_prompt",
    )(qa, qi, qmisc, k, v, ki, corr)


def _unit_lower_inverse(mats, n):
    row, col = _iota((n, n), 0), _iota((n, n), 1)
    same = lambda width: row // width == col // width
    dot = functools.partial(jnp.dot, preferred_element_type=F32)
    eye = jnp.where(row == col, 1.0, 0.0)
    diag = [jnp.where(same(INV_BLOCK), a, 0.0) for a in mats]
    ts = [eye - d for d in diag]
    dks = [_split(d) for d in diag]
    k = 1
    while 2 * k < INV_BLOCK:
        dks = [_split(_dot3(dk, dk)) for dk in dks]
        ts = [t + _dot3(_split(t), dk) for t, dk in zip(ts, dks)]
        k *= 2
    width = INV_BLOCK
    while width < n:
        off = jnp.logical_and(same(2 * width), jnp.logical_not(same(width)))
        ls = [jnp.where(off, a, 0.0).astype(BF16) for a in mats]
        tb = [t.astype(BF16) for t in ts]
        tl = [dot(t, l).astype(BF16) for t, l in zip(tb, ls)]
        ts = [t - dot(x, y) for t, x, y in zip(ts, tl, tb)]
        width *= 2
    return ts


def _head_rms(x, w):
    return x * lax.rsqrt(jnp.mean(x * x, axis=-1, keepdims=True) + 1e-6) * w


def _gdn_multi_kernel(qkv_ref, misc_ref, z_ref, hist_ref, s0_ref, cw_ref, alog_ref, dtb_ref, nw_ref,
                      y_ref, sfin_ref, newhist_ref, ext_ref, s_ref, *, c, a_col, b_col):
    j = pl.program_id(1)
    hw = CONV_B - 1
    base = SUBLANES
    wq = N_HEADS * DK_B
    seqs = range(qkv_ref.shape[0])

    @pl.when(j == 0)
    def _():
        ext_ref[:, base - hw:base, :] = hist_ref[...]
        s_ref[...] = s0_ref[...]

    @pl.when(j > 0)
    def _():
        ext_ref[:, base - hw:base, :] = ext_ref[:, base + c - hw:base + c, :]

    ext_ref[:, base:base + c, :] = qkv_ref[...]
    newhist_ref[...] = ext_ref[:, base + c - hw:base + c, :]
    ri = _iota((c, c), 0)
    ci = _iota((c, c), 1)
    lower = ri >= ci
    strict = ri > ci
    tri = jnp.where(lower, 1.0, 0.0)
    eye_h = jnp.where(_iota((N_HEADS, N_HEADS), 0) == _iota((N_HEADS, N_HEADS), 1), 1.0, 0.0)

    def conv_act(b):
        conv = None
        for k in range(CONV_B):
            term = ext_ref[b, base - hw + k:base - hw + k + c, :] * cw_ref[k:k + 1, :]
            conv = term if conv is None else conv + term
        return conv * _sigmoid(conv)

    act = [conv_act(b) for b in seqs]
    beta = [_sigmoid(misc_ref[b, :, b_col:b_col + N_HEADS]) for b in seqs]
    g = [-jnp.exp(alog_ref[...]) * _softplus(misc_ref[b, :, a_col:a_col + N_HEADS] + dtb_ref[...]) for b in seqs]
    gc = [_dot_hi(tri, x) for x in g]
    gc_t = [_dot_nt(eye_h, x, HI) for x in gc]
    eg = [jnp.exp(x) for x in gc]
    e_last = [jnp.exp(x[c - 1:c, :]) for x in gc]
    e_rest = [jnp.exp(x[c - 1:c, :] - x) for x in gc]

    items = [(b, h) for b in seqs for h in range(N_HEADS)]
    col = lambda x, h: x[:, h:h + 1]
    qs = [act[b][:, h * DK_B:(h + 1) * DK_B] for b, h in items]
    ks = [act[b][:, wq + h * DK_B:wq + (h + 1) * DK_B] for b, h in items]
    vs = [act[b][:, 2 * wq + h * DV_B:2 * wq + (h + 1) * DV_B] for b, h in items]
    qs = [q * lax.rsqrt(jnp.sum(q * q, axis=-1, keepdims=True) + 1e-6) * (DK_B ** -0.5) for q in qs]
    ks = [k * lax.rsqrt(jnp.sum(k * k, axis=-1, keepdims=True) + 1e-6) for k in ks]
    kbs = [k.astype(BF16) for k in ks]
    decay = [jnp.where(lower, jnp.exp(jnp.where(lower, col(gc[b], h) - gc_t[b][h:h + 1, :], 0.0)), 0.0)
             for b, h in items]
    kk = [_dot_nt(k, k) for k in kbs]
    attn = [_dot_nt(q.astype(BF16), k) * d for q, k, d in zip(qs, kbs, decay)]
    a_mat = [jnp.where(strict, col(beta[b], h) * x * d, 0.0) for (b, h), x, d in zip(items, kk, decay)]
    t_mat = [t.astype(BF16) for t in _unit_lower_inverse(a_mat, c)]
    value = [_dot(t, v * col(beta[b], h)) for (b, h), t, v in zip(items, t_mat, vs)]
    k_cum = [_dot(t, k * (col(beta[b], h) * col(eg[b], h))) for (b, h), t, k in zip(items, t_mat, ks)]
    s_old = [s_ref[b, h] for b, h in items]
    sbs = [s.astype(BF16) for s in s_old]
    v_new = [(v - _dot(kc, s)).astype(BF16) for v, kc, s in zip(value, k_cum, sbs)]
    o_inter = [_dot(q * col(eg[b], h), s) for (b, h), q, s in zip(items, qs, sbs)]
    o = [oi + _dot(a, v) for oi, a, v in zip(o_inter, attn, v_new)]
    for i, (b, h) in enumerate(items):
        s_ref[b, h] = s_old[i] * col(e_last[b], h) + _dot_tn((ks[i] * col(e_rest[b], h)).astype(BF16), v_new[i])
    for i, (b, h) in enumerate(items):
        zh = z_ref[b, :, h * DV_B:(h + 1) * DV_B]
        y_ref[b, :, h * DV_B:(h + 1) * DV_B] = _head_rms(o[i], nw_ref[...]) * (zh * _sigmoid(zh))
    sfin_ref[...] = s_ref[...]


def _gdn(qkv, misc, z, hist, s0, conv_w, a_log, dt_bias, norm_w, *, nb, c, a_col, b_col):
    n, wqkv = qkv.shape
    seq = n // nb
    nbb = GDN_SEQS_PER_STEP if nb % GDN_SEQS_PER_STEP == 0 else 1
    tok = lambda b, j: (b, j, 0)
    per_b3 = lambda b, j: (b, 0, 0)
    per_b4 = lambda b, j: (b, 0, 0, 0)
    const2 = lambda b, j: (0, 0)
    hw = CONV_B - 1
    tokens = [a.reshape(nb, seq, a.shape[1]) for a in (qkv, misc, z)]
    y, s_fin, new_hist = pl.pallas_call(
        functools.partial(_gdn_multi_kernel, c=c, a_col=a_col, b_col=b_col),
        grid=(nb // nbb, seq // c),
        in_specs=[pl.BlockSpec((nbb, c, a.shape[2]), tok) for a in tokens]
                 + [pl.BlockSpec((nbb, hw, wqkv), per_b3),
                    pl.BlockSpec((nbb,) + s0.shape[1:], per_b4),
                    pl.BlockSpec((CONV_B, wqkv), const2),
                    pl.BlockSpec((1, N_HEADS), const2),
                    pl.BlockSpec((1, N_HEADS), const2),
                    pl.BlockSpec((1, DV_B), const2)],
        out_specs=[pl.BlockSpec((nbb, c, z.shape[1]), tok),
                   pl.BlockSpec((nbb,) + s0.shape[1:], per_b4),
                   pl.BlockSpec((nbb, hw, wqkv), per_b3)],
        out_shape=[jax.ShapeDtypeStruct((nb, seq, z.shape[1]), F32),
                   jax.ShapeDtypeStruct(s0.shape, F32),
                   jax.ShapeDtypeStruct(hist.shape, F32)],
        scratch_shapes=[pltpu.VMEM((nbb, SUBLANES + c, wqkv), F32),
                        pltpu.VMEM((nbb,) + s0.shape[1:], F32)],
        compiler_params=_params("parallel", "arbitrary"),
        name="gated_deltanet",
    )(*tokens, hist, s0, conv_w, a_log.reshape(1, -1), dt_bias.reshape(1, -1), norm_w.reshape(1, -1))
    return y.reshape(n, z.shape[1]), s_fin, new_hist


def _block_mask(rows, row_group, cols, col_group):
    return (np.arange(rows)[:, None] // row_group == np.arange(cols)[None, :] // col_group).astype(np.float32)


def _cummax_rows(x):
    rows = x.shape[0]
    row = _iota(x.shape, 0)
    sh = 1
    while sh < rows:
        x = jnp.maximum(x, jnp.where(row >= sh, pltpu.roll(x, sh, axis=0), -jnp.inf))
        sh *= 2
    return x


def _dot2(a, b):
    hi, lo = _split(a)
    return jnp.dot(hi, b, preferred_element_type=F32) + jnp.dot(lo, b, preferred_element_type=F32)


def _mlstm_dense_kernel(qk_ref, v_ref, og_ref, misc_ref, c0_ref, n0_ref, m0_ref, ib_ref, fb_ref, nw_ref,
                        el_ref, ev_ref, ek_ref, kmask_ref, vmask_ref, cmask_ref, cmaskb_ref, rms_ref, causal_ref,
                        dsel_ref,
                        y_ref, cfin_ref, nfin_ref, mfin_ref, c_ref, n_ref, m_ref, *, l, i_col, f_col):
    j = pl.program_id(1)
    wq = N_HEADS * DK_C
    nbb = qk_ref.shape[0]
    seqs = range(nbb)

    @pl.when(j == 0)
    def _():
        c_ref[...] = jnp.zeros(c_ref.shape, F32)
        for b in seqs:
            for h in range(N_HEADS):
                c_ref[b, h * DK_C:(h + 1) * DK_C, h * DV_C:(h + 1) * DV_C] = c0_ref[b, h]
        n_ref[...] = n0_ref[...]
        m_ref[...] = m0_ref[...]

    each = lambda f, *xs: [f(*args) for args in zip(*xs)]
    dot = functools.partial(jnp.dot, preferred_element_type=F32)
    spread = lambda x, e_ref: sum(dot(p, e_ref[...]) for p in _split3(x))
    tri = jnp.where(_iota((l, l), 0) >= _iota((l, l), 1), 1.0, 0.0).astype(BF16)
    ig = [misc_ref[b, :, i_col:i_col + N_HEADS] + ib_ref[...] for b in seqs]
    lf = [-_softplus(-(misc_ref[b, :, f_col:f_col + N_HEADS] + fb_ref[...])) for b in seqs]
    fc = each(lambda x: sum(dot(tri, p) for p in _split3(x)), lf)
    m_prev = [m_ref[b] for b in seqs]
    log_inter = each(jnp.add, fc, m_prev)
    a = each(jnp.subtract, ig, fc)
    m_t = each(lambda li, f, x: jnp.maximum(li, f + _cummax_rows(x)), log_inter, fc, a)
    w_inter = each(lambda li, m: jnp.exp(li - m), log_inter, m_t)
    w_last = each(lambda f, i, m: jnp.exp(f[l - 1:l, :] - f + i - m[l - 1:l, :]), fc, ig, m_t)

    key_terms = each(lambda x: jnp.sum(spread(x, el_ref) * dsel_ref[...], axis=0, keepdims=True), a)
    log_w = each(lambda f, kt: jnp.where(causal_ref[...] > 0.5, spread(f, el_ref) + kt, -jnp.inf), fc, key_terms)
    q = [qk_ref[b, :, :wq] for b in seqs]
    q_b = [x.astype(BF16) for x in q]
    k_s = [qk_ref[b, :, wq:] * (DK_C ** -0.5) for b in seqs]
    k_b = [x.astype(BF16) for x in k_s]
    v_b = [v_ref[b].astype(BF16) for b in seqs]
    kt_bd = [jnp.concatenate([x] * N_HEADS, axis=0) * kmask_ref[...] for x in k_b]
    v_bd = [jnp.concatenate([x] * N_HEADS, axis=0) * vmask_ref[...] for x in v_b]
    m_l = each(lambda m: spread(m, el_ref), m_t)
    qkw = each(lambda x, kt, lw, m: (_dot_nt(x, kt) * jnp.exp(lw - m)).astype(BF16), q_b, kt_bd, log_w, m_l)
    c_old = [c_ref[b] for b in seqs]
    n_old = [n_ref[b] for b in seqs]
    wi_s = each(lambda w: spread(w, ev_ref), w_inter)
    q_c = each(lambda x, c: dot(x, c.astype(BF16)), q_b, c_old)
    qn = each(lambda x, n: _dot2(x * n, cmaskb_ref[...]), q, n_old)
    num = each(lambda w, vb, wi, qc: dot(w, vb) + wi * qc, qkw, v_bd, wi_s, q_c)
    den = each(lambda w, wi, x: dot(w, vmask_ref[...]) + wi * x, qkw, wi_s, qn)
    m_v = m_l if l == DV_C else each(lambda m: spread(m, ev_ref), m_t)
    hid = each(lambda nu, de, m: nu / jnp.maximum(jnp.abs(de), jnp.exp(-m)), num, den, m_v)
    ms = each(lambda x: _dot2(x * x, rms_ref[...]), hid)
    for b in seqs:
        y_ref[b] = hid[b] * lax.rsqrt(ms[b] + 1e-6) * nw_ref[...] * _sigmoid(og_ref[b])

    kw = each(lambda k, w: k * spread(w, ek_ref), k_s, w_last)
    dec_rows = [jnp.broadcast_to(w[l - 1:l, :], (SUBLANES, N_HEADS)) for w in w_inter]
    upd = each(lambda x, vb: _dot_tn(x.astype(BF16), vb), kw, v_b)
    for b in seqs:
        c_ref[b] = c_old[b] * spread(dec_rows[b], ev_ref)[0:1, :] + upd[b] * cmask_ref[...]
        n_ref[b] = n_old[b] * spread(dec_rows[b], ek_ref)[0:1, :] + jnp.sum(kw[b], axis=0, keepdims=True)
        m_ref[b] = m_t[b][l - 1:l, :]

    @pl.when(j == pl.num_programs(1) - 1)
    def _():
        for b in seqs:
            for h in range(N_HEADS):
                cfin_ref[b, h] = c_ref[b, h * DK_C:(h + 1) * DK_C, h * DV_C:(h + 1) * DV_C]
        nfin_ref[...] = n_ref[...]
        mfin_ref[...] = m_ref[...]


def _mlstm_dense(qk, v, og, misc, c0, n0, m0, i_bias, f_bias, norm_w, *, nb, l, i_col, f_col):
    n = qk.shape[0]
    seq = n // nb
    wq, wv = N_HEADS * DK_C, N_HEADS * DV_C
    hl = N_HEADS * l
    nbb = SEQS_PER_STEP if nb % SEQS_PER_STEP == 0 else 1
    tok = lambda b, j: (b, j, 0)
    per_b3 = lambda b, j: (b, 0, 0)
    per_b4 = lambda b, j: (b, 0, 0, 0)
    const2 = lambda b, j: (0, 0)
    consts = [jnp.asarray(_block_mask(N_HEADS, 1, hl, l), BF16),
              jnp.asarray(_block_mask(N_HEADS, 1, wv, DV_C), BF16),
              jnp.asarray(_block_mask(N_HEADS, 1, wq, DK_C), BF16),
              jnp.asarray(_block_mask(hl, l, wq, DK_C), BF16),
              jnp.asarray(_block_mask(hl, l, wv, DV_C), BF16),
              jnp.asarray(_block_mask(wq, DK_C, wv, DV_C)),
              jnp.asarray(_block_mask(wq, DK_C, wv, DV_C), BF16),
              jnp.asarray(_block_mask(wv, DV_C, wv, DV_C) / DV_C, BF16),
              jnp.asarray(np.tile(np.tril(np.ones((l, l), np.float32)), (1, N_HEADS))),
              jnp.asarray(np.tile(np.eye(l, dtype=np.float32), (1, N_HEADS)))]
    tokens = [a.reshape(nb, seq, a.shape[1]) for a in (qk, v, og, misc)]
    outs = pl.pallas_call(
        functools.partial(_mlstm_dense_kernel, l=l, i_col=i_col, f_col=f_col),
        grid=(nb // nbb, seq // l),
        in_specs=[pl.BlockSpec((nbb, l, a.shape[2]), tok) for a in tokens]
                 + [pl.BlockSpec((nbb,) + c0.shape[1:], per_b4),
                    pl.BlockSpec((nbb, 1, wq), per_b3),
                    pl.BlockSpec((nbb, 1, N_HEADS), per_b3),
                    pl.BlockSpec((1, N_HEADS), const2),
                    pl.BlockSpec((1, N_HEADS), const2),
                    pl.BlockSpec((1, wv), const2)]
                 + [pl.BlockSpec(c.shape, const2) for c in consts],
        out_specs=[pl.BlockSpec((nbb, l, wv), tok),
                   pl.BlockSpec((nbb,) + c0.shape[1:], per_b4),
                   pl.BlockSpec((nbb, 1, wq), per_b3),
                   pl.BlockSpec((nbb, 1, N_HEADS), per_b3)],
        out_shape=[jax.ShapeDtypeStruct((nb, seq, wv), F32),
                   jax.ShapeDtypeStruct(c0.shape, F32),
                   jax.ShapeDtypeStruct((nb, 1, wq), F32),
                   jax.ShapeDtypeStruct((nb, 1, N_HEADS), F32)],
        scratch_shapes=[pltpu.VMEM((nbb, wq, wv), F32),
                        pltpu.VMEM((nbb, 1, wq), F32),
                        pltpu.VMEM((nbb, 1, N_HEADS), F32)],
        compiler_params=_params("parallel", "arbitrary"),
        name="mlstm",
    )(*tokens, c0, n0.reshape(nb, 1, wq), m0.reshape(nb, 1, N_HEADS), i_bias.reshape(1, -1), f_bias.reshape(1, -1),
      jnp.tile(norm_w, N_HEADS).reshape(1, wv), *consts)
    y, c_fin, n_fin, m_fin = outs
    return y.reshape(n, wv), c_fin, n_fin.reshape(nb, N_HEADS, DK_C), m_fin.reshape(nb, N_HEADS)


def _band_prompt_kernel(q_ref, k_ref, v_ref, bias_ref, o_ref, kb_ref, vt_ref, ot_ref, *, tq, seq):
    g = pl.program_id(1)
    w = q_ref.shape[1]
    pad = BAND_CHUNKS * CHUNK
    lw = pad + tq
    npad = pad // LANES
    per_tile = tq // LANES
    blk = _row_tile(seq, 4 * LANES)

    @pl.when(g == 0)
    def _():
        kb_ref[0:pad, :] = jnp.zeros((pad, w), BF16)
        vt_ref[0:npad] = jnp.zeros((npad, w, LANES), BF16)
        for r0 in range(0, seq, blk):
            kb_ref[pad + r0:pad + r0 + blk, :] = k_ref[0, r0:r0 + blk, :].astype(BF16)
        for j in range(seq // LANES):
            vt_ref[npad + j] = v_ref[0, j * LANES:(j + 1) * LANES, :].T.astype(BF16)

    start = pl.multiple_of(g * tq, tq)
    q = (q_ref[...] * (HEAD_DIM ** -0.5)).astype(BF16)
    before_seq = jnp.where(_iota((lw, tq), 0) >= pad - g * tq, 0.0, NEG)
    group = 4
    for h0 in range(0, N_HEADS, group):
        heads = range(h0, h0 + group)
        hs = {h: slice(h * HEAD_DIM, (h + 1) * HEAD_DIM) for h in heads}
        s = [_dot_nt(kb_ref[pl.ds(start, lw), hs[h]], q[:, hs[h]]) + (bias_ref[h] + before_seq) for h in heads]
        p = [jnp.exp(x - _reduce_rows(x, jnp.maximum, jnp.max)) for x in s]
        inv = [1.0 / _reduce_rows(x, jnp.add, jnp.sum) for x in p]
        pb = [x.astype(BF16) for x in p]
        acc = [sum(jnp.dot(vt_ref[g * per_tile + j, hs[h], :], x[j * LANES:(j + 1) * LANES, :],
                           preferred_element_type=F32) for j in range(lw // LANES)) for h, x in zip(heads, pb)]
        for h, a, r in zip(heads, acc, inv):
            ot_ref[hs[h], :] = a * r
    o_ref[...] = ot_ref[...].T


def _band_prompt(q, k, v, bias, *, nb, tq):
    n, w = q.shape
    seq = k.shape[1]
    nq = seq // tq
    pad = BAND_CHUNKS * CHUNK
    row = lambda b, g: (b * nq + g, 0)
    per_b = lambda b, g: (b, 0, 0)
    return pl.pallas_call(
        functools.partial(_band_prompt_kernel, tq=tq, seq=seq),
        grid=(nb, nq),
        in_specs=[pl.BlockSpec((tq, w), row),
                  pl.BlockSpec((1, seq, w), per_b),
                  pl.BlockSpec((1, seq, w), per_b),
                  pl.BlockSpec(bias.shape, lambda b, g: (0, 0, 0))],
        out_specs=pl.BlockSpec((tq, w), row),
        out_shape=jax.ShapeDtypeStruct(q.shape, F32),
        scratch_shapes=[pltpu.VMEM((pad + seq, w), BF16),
                        pltpu.VMEM(((pad + seq) // LANES, w, LANES), BF16),
                        pltpu.VMEM((w, tq), F32)],
        compiler_params=_params("parallel", "arbitrary"),
        name="band_attention_prompt",
    )(q, k, v, bias)


def _band_sample_kernel(q_ref, kc_ref, vc_ref, kn_ref, vn_ref, biasc_ref, biasn_ref, o_ref, *, tq, nbs):
    streams = range(nbs)
    rows = [slice(b * tq, (b + 1) * tq) for b in streams]
    dot = functools.partial(jnp.dot, preferred_element_type=F32)
    for h in range(N_HEADS):
        hs = slice(h * HEAD_DIM, (h + 1) * HEAD_DIM)
        qh = [q_ref[rows[b], hs].astype(BF16) for b in streams]
        s_c = [dot(qh[b], kc_ref[0, b, h].astype(BF16)) * (HEAD_DIM ** -0.5) + biasc_ref[h] for b in streams]
        s_n = [dot(qh[b], kn_ref[b, h].astype(BF16)) * (HEAD_DIM ** -0.5) + biasn_ref[h] for b in streams]
        m = [jnp.maximum(jnp.max(x, axis=-1, keepdims=True), jnp.max(y, axis=-1, keepdims=True))
             for x, y in zip(s_c, s_n)]
        p_c = [jnp.exp(x - mm) for x, mm in zip(s_c, m)]
        p_n = [jnp.exp(x - mm) for x, mm in zip(s_n, m)]
        den = [jnp.sum(x, axis=-1, keepdims=True) + jnp.sum(y, axis=-1, keepdims=True) for x, y in zip(p_c, p_n)]
        o = [_dot_nt(p_c[b].astype(BF16), vc_ref[0, b, h].astype(BF16))
             + _dot_nt(p_n[b].astype(BF16), vn_ref[b, h].astype(BF16)) for b in streams]
        for b in streams:
            o_ref[rows[b], hs] = o[b] / den[b]


def _band_sample(q, kc, vc, layer, kn, vn, bias_c, bias_n, *, nb, tq):
    n, w = q.shape
    nbs = SAMPLE_STREAMS_PER_STEP if nb % SAMPLE_STREAMS_PER_STEP == 0 else 1
    row = lambda b: (b, 0)
    const3 = lambda b: (0, 0, 0)
    return pl.pallas_call(
        functools.partial(_band_sample_kernel, tq=tq, nbs=nbs),
        grid=(nb // nbs,),
        in_specs=[pl.BlockSpec((nbs * tq, w), row)]
                 + [pl.BlockSpec((1, nbs) + a.shape[2:], lambda b: (layer, b, 0, 0, 0)) for a in (kc, vc)]
                 + [pl.BlockSpec((nbs,) + a.shape[1:], lambda b: (b, 0, 0, 0)) for a in (kn, vn)]
                 + [pl.BlockSpec(bias_c.shape, const3), pl.BlockSpec(bias_n.shape, const3)],
        out_specs=pl.BlockSpec((nbs * tq, w), row),
        out_shape=jax.ShapeDtypeStruct(q.shape, F32),
        compiler_params=_params("parallel"),
        name="band_attention_sample",
    )(q, kc, vc, kn, vn, bias_c, bias_n)


def _t5_bucket(rel):
    nb = T5_BUCKETS // 2
    max_exact = nb // 2
    n = jnp.abs(rel)
    n_f = jnp.maximum(n, 1).astype(jnp.float32)
    large = max_exact + (jnp.log(n_f / max_exact) / math.log(T5_MAX_DIST / max_exact) * (nb - max_exact)).astype(jnp.int32)
    large = jnp.minimum(large, nb - 1)
    return jnp.where(rel > 0, nb, 0) + jnp.where(n < max_exact, n, large)


def _toeplitz_bias(fn, n_rows, n_cols):
    n = n_rows + n_cols
    m = np.arange(n)
    f = jnp.transpose(fn(np.where(m < n_cols, m, m - n))).astype(F32)
    flat = jnp.tile(f, (1, n_rows))[:, :n_rows * (n - 1)]
    return flat.reshape(f.shape[0], n_rows, n - 1)[:, :, :n_cols]


def _pack_cols(w, sizes, groups):
    offs = np.concatenate([[0], np.cumsum(sizes)])
    cols, widths = [], []
    for grp in groups:
        width = 0
        for idx in grp:
            cols.append(w[:, offs[idx]:offs[idx + 1]])
            width += sizes[idx]
        pad = (-width) % LANES
        if pad:
            cols.append(jnp.zeros((w.shape[0], pad), w.dtype))
        widths.append(width + pad)
    return jnp.concatenate(cols, axis=1).astype(BF16), tuple(widths)


def _row_tile(n, target):
    t = min(n, target)
    while n % t:
        t //= 2
    return t


def kernel(x_prompt, x_sample, cache_a_k, cache_a_v, cache_a_kidx, state_b_s, state_b_conv, state_c_c, state_c_n, state_c_m, cache_d_k, cache_d_v, state_ffn_conv, w_in_even, w_out_even, t5_bias, b_conv_w, b_a_log, b_dt_bias, b_norm_w, w_in_odd, w_out_odd, c_i_bias, c_f_bias, c_norm_w, d_rel_bias, ln_mix_g, ln_mix_b, ln_ffn_g, ln_ffn_b, ffn_w_up, ffn_conv_w, ffn_w_down):
    bp, sp, d = x_prompt.shape
    bs, ts, _ = x_sample.shape
    depth = ffn_w_up.shape[0]
    past = cache_a_k.shape[2]
    d_win = cache_d_k.shape[2]
    dff = ffn_w_down.shape[1]
    alpha = (2 * depth) ** 0.25
    w_a = N_HEADS * HEAD_DIM
    w_b = N_HEADS * DV_B
    w_c = N_HEADS * DV_C
    qkv_b_w = 2 * N_HEADS * DK_B + w_b
    even_sizes = (w_a, w_a, w_a, N_IDX_HEADS * D_IDX, D_IDX, N_IDX_HEADS, qkv_b_w, N_HEADS, N_HEADS, w_b)
    odd_sizes = (N_HEADS * DK_C, N_HEADS * DK_C, w_c, N_HEADS, N_HEADS, w_c, w_a, w_a, w_a)
    even_groups = ((0,), (1,), (2,), (3,), (6,), (9,), (4, 5, 7, 8))
    wi_col, a_col, b_col = D_IDX, D_IDX + N_IDX_HEADS, D_IDX + N_IDX_HEADS + N_HEADS
    odd_groups = ((0, 1), (2,), (5,), (6,), (7,), (8,), (3, 4))
    i_col, f_col = 0, N_HEADS

    assert sp % CHUNK == 0 and ts <= CHUNK and past % CHUNK == 0 and past >= T5_FAR
    assert (past + ts - 1) // CHUNK == past // CHUNK
    topk_p = min(TOPK_MAX, sp // 4)
    topk_s = min(TOPK_MAX, (past + ts) // 4)
    n_p, n_s = bp * sp, bs * ts
    tm_p = _row_tile(n_p, 512)
    tm_s = _row_tile(n_s, 512)
    tff_p = _row_tile(sp, 512)
    ns_s = _row_tile(bs, max(1, 256 // ts))

    t5 = lambda rel: t5_bias[_t5_bucket(jnp.asarray(rel, jnp.int32))]
    fbias = t5(np.array([-T5_FAR - 1]))
    tq_dsa = 2 * LANES
    assert sp % tq_dsa == 0 and T5_FAR <= tq_dsa
    corr_p = jnp.stack([_toeplitz_bias(lambda dd: t5(-dd - tq_dsa * i) - fbias, tq_dsa, tq_dsa) for i in range(2)])
    tq_a = 2 * CHUNK
    ln_s = T5_FAR + ts
    nbias_s = _toeplitz_bias(lambda dd: t5(dd - T5_FAR), ts, ln_s)[None]

    lw = BAND_CHUNKS * CHUNK + tq_a
    r_chunk = np.arange(lw)[:, None] // CHUNK
    q_chunk = BAND_CHUNKS + np.arange(tq_a)[None, :] // CHUNK
    band_ok = (r_chunk >= q_chunk - BAND_CHUNKS) & (r_chunk <= q_chunk)
    pos_q = past + np.arange(ts)
    pos_kc = past - d_win + np.arange(d_win)
    def band_valid(pos_k):
        kch, qch = pos_k // CHUNK, pos_q // CHUNK
        return (pos_k[None] >= 0) & (kch[None] >= qch[:, None] - BAND_CHUNKS) & (kch[None] <= qch[:, None])
    def band_bias(table, shift, n_rows, n_cols, valid=None):
        bias = _toeplitz_bias(lambda dd: table[np.clip(dd + shift, -REL_CLIP, REL_CLIP) + REL_CLIP], n_rows, n_cols)
        return bias if valid is None else jnp.where(jnp.asarray(valid)[None], bias, NEG)

    cak_t = jnp.transpose(cache_a_k, (0, 1, 3, 4, 2))
    cav_t = jnp.transpose(cache_a_v, (0, 1, 3, 4, 2))
    caki_t = jnp.transpose(cache_a_kidx, (0, 1, 3, 2))
    cdk_t = jnp.transpose(cache_d_k, (0, 1, 3, 4, 2))
    cdv_t = jnp.transpose(cache_d_v, (0, 1, 3, 4, 2))

    xp = x_prompt.reshape(n_p, d)
    xs = x_sample.reshape(n_s, d)
    outs = {k: [] for k in ("ak_p", "ak_s", "av_p", "av_s", "aki_p", "aki_s", "bs_p", "bs_s", "bc_p", "bc_s",
                            "cc_p", "cc_s", "cn_p", "cn_s", "cm_p", "cm_s", "dk_p", "dk_s", "dv_p", "dv_s",
                            "fc_p", "fc_s")}
    for layer in range(depth):
        if layer % 2 == 0:
            e = layer // 2
            w_in, widths = _pack_cols(w_in_even[e], even_sizes, even_groups)
            w_out = w_out_even[e].astype(BF16)
            qa, ka, va, qi, qkv_b, z_b, misc = _proj(xp, w_in, widths, tm_p)
            o_a = _dsa_blocks(qa, qi, misc, ka.reshape(bp, sp, w_a), va.reshape(bp, sp, w_a),
                              misc.reshape(bp, sp, LANES), corr_p, nb=bp, tq=tq_dsa, topk=topk_p, wi_col=wi_col)
            y_b, s_b, h_b = _gdn(qkv_b, misc, z_b, jnp.zeros((bp, CONV_B - 1, qkv_b_w), F32),
                                 jnp.zeros((bp, N_HEADS, DK_B, DV_B), F32), b_conv_w[e], b_a_log[e], b_dt_bias[e],
                                 b_norm_w[e], nb=bp, c=CHUNK, a_col=a_col, b_col=b_col)
            xp = _mm_res_ln([o_a, y_b], [w_out[:w_a], w_out[w_a:]], xp, ln_mix_g[layer], ln_mix_b[layer], alpha, tm_p)
            outs["ak_p"].append(ka.reshape(bp, sp, N_HEADS, HEAD_DIM))
            outs["av_p"].append(va.reshape(bp, sp, N_HEADS, HEAD_DIM))
            outs["aki_p"].append(misc[:, :D_IDX].reshape(bp, sp, D_IDX))
            outs["bs_p"].append(s_b)
            outs["bc_p"].append(h_b)
            qa, ka, va, qi, qkv_b, z_b, misc = _proj(xs, w_in, widths, tm_s)
            ki = misc[:, :D_IDX]
            heads_t = lambda u: jnp.transpose(u.reshape(bs, ts, N_HEADS, HEAD_DIM), (0, 2, 3, 1))
            near = (jnp.concatenate([cak_t[e, ..., past - T5_FAR:], heads_t(ka)], axis=-1),
                    jnp.concatenate([cav_t[e, ..., past - T5_FAR:], heads_t(va)], axis=-1),
                    jnp.concatenate([caki_t[e, ..., past - T5_FAR:],
                                     jnp.transpose(ki.reshape(bs, ts, D_IDX), (0, 2, 1))], axis=-1))
            o_a = _dsa_sample(qa, qi, misc, cak_t, cav_t, caki_t, e, near, nbias_s, fbias, nb=bs, tq=ts,
                              topk=topk_s, wi_col=wi_col)
            y_b, s_b, h_b = _gdn(qkv_b, misc, z_b, state_b_conv[e], state_b_s[e], b_conv_w[e], b_a_log[e],
                                 b_dt_bias[e], b_norm_w[e], nb=bs, c=ts, a_col=a_col, b_col=b_col)
            xs = _mm_res_ln([o_a, y_b], [w_out[:w_a], w_out[w_a:]], xs, ln_mix_g[layer], ln_mix_b[layer], alpha, tm_s)
            outs["ak_s"].append(ka.reshape(bs, ts, N_HEADS, HEAD_DIM))
            outs["av_s"].append(va.reshape(bs, ts, N_HEADS, HEAD_DIM))
            outs["aki_s"].append(ki.reshape(bs, ts, D_IDX))
            outs["bs_s"].append(s_b)
            outs["bc_s"].append(h_b)
        else:
            o = layer // 2
            w_in, widths = _pack_cols(w_in_odd[o], odd_sizes, odd_groups)
            w_out = w_out_odd[o].astype(BF16)
            qk_c, v_c, o_c, q_d, k_d, v_d, misc = _proj(xp, w_in, widths, tm_p)
            y_c, c_c, c_n, c_m = _mlstm_dense(qk_c, v_c, o_c, misc,jnp.zeros((bp, N_HEADS, DK_C, DV_C), F32),
                                        jnp.zeros((bp, N_HEADS, DK_C), F32), jnp.zeros((bp, N_HEADS), F32),
                                        c_i_bias[o], c_f_bias[o], c_norm_w[o], nb=bp, l=CHUNK, i_col=i_col, f_col=f_col)
            k3 = k_d.reshape(bp, sp, w_a)
            v3 = v_d.reshape(bp, sp, w_a)
            bias_bp = _toeplitz_bias(
                lambda dd: d_rel_bias[o][np.clip(-dd - BAND_CHUNKS * CHUNK, -REL_CLIP, REL_CLIP) + REL_CLIP], lw, tq_a)
            bias_bp = jnp.where(jnp.asarray(band_ok)[None], bias_bp, NEG)
            o_d = _band_prompt(q_d, k3, v3, bias_bp, nb=bp, tq=tq_a)
            xp = _mm_res_ln([y_c, o_d], [w_out[:w_c], w_out[w_c:]], xp, ln_mix_g[layer], ln_mix_b[layer], alpha, tm_p)
            d_win_p = min(BAND_CHUNKS * CHUNK, sp)
            outs["cc_p"].append(c_c)
            outs["cn_p"].append(c_n)
            outs["cm_p"].append(c_m)
            outs["dk_p"].append(k3[:, sp - d_win_p:].reshape(bp, d_win_p, N_HEADS, HEAD_DIM))
            outs["dv_p"].append(v3[:, sp - d_win_p:].reshape(bp, d_win_p, N_HEADS, HEAD_DIM))
            qk_c, v_c, o_c, q_d, k_d, v_d, misc = _proj(xs, w_in, widths, tm_s)
            y_c, c_c, c_n, c_m = _mlstm_dense(qk_c, v_c, o_c, misc,state_c_c[o], state_c_n[o], state_c_m[o],
                                        c_i_bias[o], c_f_bias[o], c_norm_w[o], nb=bs, l=ts, i_col=i_col, f_col=f_col)
            heads_t = lambda u: jnp.transpose(u.reshape(bs, ts, N_HEADS, HEAD_DIM), (0, 2, 3, 1))
            o_d = _band_sample(q_d, cdk_t, cdv_t, o, heads_t(k_d), heads_t(v_d),
                               band_bias(d_rel_bias[o], -d_win, ts, d_win, band_valid(pos_kc)),
                               band_bias(d_rel_bias[o], 0, ts, ts, band_valid(pos_q)), nb=bs, tq=ts)
            xs = _mm_res_ln([y_c, o_d], [w_out[:w_c], w_out[w_c:]], xs, ln_mix_g[layer], ln_mix_b[layer], alpha, tm_s)
            outs["cc_s"].append(c_c)
            outs["cn_s"].append(c_n)
            outs["cm_s"].append(c_m)
            outs["dk_s"].append(k_d.reshape(bs, ts, N_HEADS, HEAD_DIM))
            outs["dv_s"].append(v_d.reshape(bs, ts, N_HEADS, HEAD_DIM))
        w_up = ffn_w_up[layer].astype(BF16)
        w_down = ffn_w_down[layer].astype(BF16)
        act, hist_p = _ffn_up(xp, w_up, ffn_conv_w[layer], jnp.zeros((bp, CONV_FF - 1, 2 * dff), F32),
                              1, tff_p, sp // tff_p)
        xp = _mm_res_ln([act], [w_down], xp, ln_ffn_g[layer], ln_ffn_b[layer], alpha, tm_p)
        act, hist_s = _ffn_up(xs, w_up, ffn_conv_w[layer], state_ffn_conv[layer], ns_s, ts, 1)
        xs = _mm_res_ln([act], [w_down], xs, ln_ffn_g[layer], ln_ffn_b[layer], alpha, tm_s)
        outs["fc_p"].append(hist_p)
        outs["fc_s"].append(hist_s)

    st = lambda k: jnp.stack(outs[k])
    return (xp.reshape(bp, sp, d), xs.reshape(bs, ts, d),
            st("ak_p"), st("ak_s"), st("av_p"), st("av_s"), st("aki_p"), st("aki_s"),
            st("bs_p"), st("bs_s"), st("bc_p"), st("bc_s"),
            st("cc_p"), st("cc_s"), st("cn_p"), st("cn_s"), st("cm_p"), st("cm_s"),
            st("dk_p"), st("dk_s"), st("dv_p"), st("dv_s"),
            st("fc_p"), st("fc_s"))
```

```python
import functools
import math

import numpy as np
import jax
import jax.numpy as jnp
from jax import lax
from jax.experimental import pallas as pl
from jax.experimental.pallas import tpu as pltpu

F32 = jnp.float32
BF16 = jnp.bfloat16
HI = lax.Precision.HIGHEST

CHUNK = 64
HEAD_DIM = 64
N_HEADS = 8
N_IDX_HEADS = 8
D_IDX = 64
TOPK_MAX = 256
T5_BUCKETS = 32
T5_MAX_DIST = 128
DK_B = 64
DV_B = 64
CONV_B = 4
DK_C = 32
DV_C = 64
BAND_CHUNKS = 8
REL_CLIP = 128
CONV_FF = 3
T5_FAR = 128
INV_BLOCK = 8
SAMPLE_STREAMS_PER_STEP = 4
GDN_SEQS_PER_STEP = 2
SEQS_PER_STEP = 8

LANES = 128
SUBLANES = 8
VMEM_LIMIT = 56 * 1024 * 1024

NEG = -1e30
INT_MIN = -2 ** 31


def _params(*sem):
    return pltpu.CompilerParams(dimension_semantics=sem, vmem_limit_bytes=VMEM_LIMIT)


def _dot(a, b):
    return jnp.dot(a.astype(BF16), b.astype(BF16), preferred_element_type=F32)


def _dot_nt(a, b, precision=None):
    return lax.dot_general(a, b, (((1,), (1,)), ((), ())), precision=precision, preferred_element_type=F32)


def _dot_tn(a, b, precision=None):
    return lax.dot_general(a, b, (((0,), (0,)), ((), ())), precision=precision, preferred_element_type=F32)


def _dot_hi(a, b):
    return jnp.dot(a, b, precision=HI, preferred_element_type=F32)


def _split(a):
    hi = a.astype(BF16)
    return hi, (a - hi.astype(F32)).astype(BF16)


def _split3(a):
    p1 = a.astype(BF16)
    r1 = a - p1.astype(F32)
    p2 = r1.astype(BF16)
    return p1, p2, (r1 - p2.astype(F32)).astype(BF16)


def _dot3(a, b):
    a_hi, a_lo = a
    b_hi, b_lo = b
    d = functools.partial(jnp.dot, preferred_element_type=F32)
    return d(a_hi, b_hi) + (d(a_hi, b_lo) + d(a_lo, b_hi))


def _sigmoid(x):
    return 1.0 / (1.0 + jnp.exp(-x))


def _softplus(x):
    return jnp.maximum(x, 0.0) + jnp.log(1.0 + jnp.exp(-jnp.abs(x)))


def _iota(shape, dim):
    return lax.broadcasted_iota(jnp.int32, shape, dim)


def _proj_kernel(x_ref, w_ref, *out_refs, sizes):
    xb = x_ref[...].astype(BF16)
    off = 0
    for o_ref, size in zip(out_refs, sizes):
        o_ref[...] = jnp.dot(xb, w_ref[:, off:off + size], preferred_element_type=F32)
        off += size


def _proj(x2d, w, sizes, tm):
    n, d = x2d.shape
    return pl.pallas_call(
        functools.partial(_proj_kernel, sizes=sizes),
        grid=(n // tm,),
        in_specs=[pl.BlockSpec((tm, d), lambda i: (i, 0)),
                  pl.BlockSpec((d, sum(sizes)), lambda i: (0, 0))],
        out_specs=[pl.BlockSpec((tm, s), lambda i: (i, 0)) for s in sizes],
        out_shape=[jax.ShapeDtypeStruct((n, s), F32) for s in sizes],
        compiler_params=_params("parallel"),
        name="in_proj",
    )(x2d, w)


def _mm_res_ln_kernel(*refs, nparts, alpha):
    part_refs = refs[:nparts]
    w_refs = refs[nparts:2 * nparts]
    x_ref, g_ref, b_ref, o_ref = refs[2 * nparts:]
    acc = alpha * x_ref[...]
    for p_ref, w_ref in zip(part_refs, w_refs):
        acc = acc + jnp.dot(p_ref[...].astype(BF16), w_ref[...], preferred_element_type=F32)
    mu = jnp.mean(acc, axis=-1, keepdims=True)
    cen = acc - mu
    var = jnp.mean(cen * cen, axis=-1, keepdims=True)
    o_ref[...] = cen * lax.rsqrt(var + 1e-5) * g_ref[...] + b_ref[...]


def _mm_res_ln(parts, ws, x2d, g, b, alpha, tm):
    n, d = x2d.shape
    nparts = len(parts)
    in_specs = ([pl.BlockSpec((tm, p.shape[1]), lambda i: (i, 0)) for p in parts]
                + [pl.BlockSpec(w.shape, lambda i: (0, 0)) for w in ws]
                + [pl.BlockSpec((tm, d), lambda i: (i, 0)),
                   pl.BlockSpec((1, d), lambda i: (0, 0)),
                   pl.BlockSpec((1, d), lambda i: (0, 0))])
    return pl.pallas_call(
        functools.partial(_mm_res_ln_kernel, nparts=nparts, alpha=alpha),
        grid=(n // tm,),
        in_specs=in_specs,
        out_specs=pl.BlockSpec((tm, d), lambda i: (i, 0)),
        out_shape=jax.ShapeDtypeStruct((n, d), F32),
        compiler_params=_params("parallel"),
        name="out_proj_ln",
    )(*parts, *ws, x2d, g.reshape(1, d), b.reshape(1, d))


def _ffn_up_kernel(x_ref, w_ref, cw_ref, hist_ref, act_ref, newhist_ref, ext_ref, *,
                   ns, tt, tiles_per_seq, dff, cc):
    i = pl.program_id(0)
    tm = ns * tt
    hw = CONV_FF - 1
    base = SUBLANES
    if tiles_per_seq == 1:
        ext_ref[:, base - hw:base, :] = hist_ref[...]
    else:
        @pl.when(i % tiles_per_seq == 0)
        def _():
            ext_ref[:, base - hw:base, :] = hist_ref[...]

        @pl.when(i % tiles_per_seq != 0)
        def _():
            ext_ref[:, base - hw:base, :] = ext_ref[:, base + tt - hw:base + tt, :]
    xb = x_ref[...].astype(BF16)
    for j in range(2 * dff // cc):
        cols = slice(j * cc, (j + 1) * cc)
        h = jnp.dot(xb, w_ref[:, cols], preferred_element_type=F32)
        ext_ref[:, base:base + tt, cols] = h.reshape(ns, tt, cc)
    newhist_ref[...] = ext_ref[:, base + tt - hw:base + tt, :]

    def conv(cols):
        acc = None
        for k in range(CONV_FF):
            term = ext_ref[:, base - hw + k:base - hw + k + tt, cols] * cw_ref[k:k + 1, cols]
            acc = term if acc is None else acc + term
        return acc

    for j in range(dff // cc):
        g = conv(slice(j * cc, (j + 1) * cc))
        u = conv(slice(dff + j * cc, dff + (j + 1) * cc))
        act = g * _sigmoid(g) * u
        act_ref[:, j * cc:(j + 1) * cc] = act.reshape(tm, cc).astype(BF16)


def _ffn_up(x2d, w_up, conv_w, hist, ns, tt, tiles_per_seq):
    n, d = x2d.shape
    c2 = w_up.shape[1]
    dff = c2 // 2
    tm = ns * tt
    cc = 256
    hw = CONV_FF - 1
    if tiles_per_seq == 1:
        hist_map = lambda i: (i, 0, 0)
    else:
        hist_map = lambda i: (i // tiles_per_seq, 0, 0)
    return pl.pallas_call(
        functools.partial(_ffn_up_kernel, ns=ns, tt=tt, tiles_per_seq=tiles_per_seq, dff=dff, cc=cc),
        grid=(n // tm,),
        in_specs=[pl.BlockSpec((tm, d), lambda i: (i, 0)),
                  pl.BlockSpec((d, c2), lambda i: (0, 0)),
                  pl.BlockSpec((CONV_FF, c2), lambda i: (0, 0)),
                  pl.BlockSpec((ns, hw, c2), hist_map)],
        out_specs=[pl.BlockSpec((tm, dff), lambda i: (i, 0)),
                   pl.BlockSpec((ns, hw, c2), hist_map)],
        out_shape=[jax.ShapeDtypeStruct((n, dff), BF16),
                   jax.ShapeDtypeStruct(hist.shape, F32)],
        scratch_shapes=[pltpu.VMEM((ns, SUBLANES + tt, c2), F32)],
        compiler_params=_params("arbitrary"),
        name="ffn_up_conv_gate",
    )(x2d, w_up, conv_w, hist)


def _sortable(x):
    b = lax.bitcast_convert_type(x, jnp.int32)
    return b ^ ((b >> 31) & jnp.int32(0x7FFFFFFF))


def _count(mask):
    return jnp.sum(jnp.where(mask, 1.0, 0.0), axis=-1, keepdims=True)


def _dsa_sample_kernel(qa_ref, qi_ref, qm_ref, kf_ref, vf_ref, kif_ref, kn_ref, vn_ref, kin_ref, nbias_ref,
                       fbias_ref, o_ref, self_ref, seln_ref, *, topk, tq, nbs, lf, ln, wi_col):
    start = lf - T5_FAR
    streams = range(nbs)
    rows = [slice(b * tq, (b + 1) * tq) for b in streams]
    dot = functools.partial(jnp.dot, preferred_element_type=F32)
    wi = [qm_ref[rows[b], wi_col:wi_col + N_IDX_HEADS] * (N_IDX_HEADS ** -0.5) * (D_IDX ** -0.5) for b in streams]
    kif = [kif_ref[0, b].astype(BF16) for b in streams]
    kinb = [kin_ref[b].astype(BF16) for b in streams]

    sc_f = [jnp.zeros((tq, lf), F32) for _ in streams]
    sc_n = [jnp.zeros((tq, ln), F32) for _ in streams]
    for n in range(N_IDX_HEADS):
        qn = [qi_ref[rows[b], n * D_IDX:(n + 1) * D_IDX].astype(BF16) for b in streams]
        sc_f = [sc_f[b] + jnp.maximum(dot(qn[b], kif[b]), 0.0) * wi[b][:, n:n + 1] for b in streams]
        sc_n = [sc_n[b] + jnp.maximum(dot(qn[b], kinb[b]), 0.0) * wi[b][:, n:n + 1] for b in streams]
    adm_f = _iota((tq, lf), 1) < start
    adm_n = nbias_ref[0, 0] > 0.5 * NEG
    key_f = [jnp.where(adm_f, _sortable(x), jnp.int32(INT_MIN)) for x in sc_f]
    key_n = [jnp.where(adm_n, _sortable(x), jnp.int32(INT_MIN)) for x in sc_n]

    kf32 = float(topk)

    def body(i, t_us):
        bit = lax.shift_left(jnp.int32(1), 31 - i)
        cand_u = [t | bit for t in t_us]
        cand_s = [c ^ jnp.int32(INT_MIN) for c in cand_u]
        cnt = [_count(key_f[b] >= cand_s[b]) + _count(key_n[b] >= cand_s[b]) for b in streams]
        return tuple(jnp.where(cnt[b] >= kf32, cand_u[b], t_us[b]) for b in streams)

    t_us = lax.fori_loop(0, 32, body, tuple(jnp.zeros((tq, 1), jnp.int32) for _ in streams))
    thr = [t ^ jnp.int32(INT_MIN) for t in t_us]
    need = [kf32 - (_count(key_f[b] > thr[b]) + _count(key_n[b] > thr[b])) for b in streams]
    n_eq = [_count(key_f[b] == thr[b]) + _count(key_n[b] == thr[b]) for b in streams]
    open_row = [t == jnp.int32(INT_MIN) for t in thr]
    for b in streams:
        self_ref[b] = jnp.where(jnp.logical_and(key_f[b] >= thr[b], adm_f), 0.0, NEG)
        seln_ref[b] = jnp.where(jnp.logical_and(key_n[b] >= thr[b], adm_n), 0.0, NEG)

    def resolve_ties(b):
        upper = jnp.where(_iota((LANES, LANES), 0) < _iota((LANES, LANES), 1), 1.0, 0.0).astype(BF16)
        offset = jnp.zeros((tq, 1), F32)
        for ref, key, width in ((self_ref, key_f[b], lf), (seln_ref, key_n[b], ln)):
            for j0 in range(0, width, LANES):
                w = min(LANES, width - j0)
                kb = key[:, j0:j0 + w]
                e = jnp.where(kb == thr[b], 1.0, 0.0)
                rank = offset + dot(e.astype(BF16), upper[:w, :w])
                take = jnp.where(kb > thr[b], 1.0, jnp.where(rank < need[b], e, 0.0))
                take = jnp.where(open_row[b], jnp.where(kb > thr[b], 1.0, 0.0), take)
                ref[b, :, j0:j0 + w] = jnp.where(take > 0.5, 0.0, NEG)
                offset = offset + jnp.sum(e, axis=-1, keepdims=True)

    for b in streams:
        conflict = jnp.logical_and(n_eq[b] != need[b], jnp.logical_not(open_row[b]))
        pl.when(jnp.max(jnp.where(conflict, 1.0, 0.0)) > 0.0)(functools.partial(resolve_ties, b))

    sel_f = [self_ref[b] for b in streams]
    sel_n = [seln_ref[b] for b in streams]
    for h in range(N_HEADS):
        hs = slice(h * HEAD_DIM, (h + 1) * HEAD_DIM)
        qh = [qa_ref[rows[b], hs].astype(BF16) for b in streams]
        s_f = [dot(qh[b], kf_ref[0, b, h].astype(BF16)) * (HEAD_DIM ** -0.5) + fbias_ref[:, h:h + 1] + sel_f[b]
               for b in streams]
        s_n = [dot(qh[b], kn_ref[b, h].astype(BF16)) * (HEAD_DIM ** -0.5) + nbias_ref[0, h] + sel_n[b]
               for b in streams]
        m = [jnp.maximum(jnp.max(x, axis=-1, keepdims=True), jnp.max(y, axis=-1, keepdims=True))
             for x, y in zip(s_f, s_n)]
        p_f = [jnp.exp(x - mm) for x, mm in zip(s_f, m)]
        p_n = [jnp.exp(x - mm) for x, mm in zip(s_n, m)]
        den = [jnp.sum(x, axis=-1, keepdims=True) + jnp.sum(y, axis=-1, keepdims=True) for x, y in zip(p_f, p_n)]
        o = [_dot_nt(p_f[b].astype(BF16), vf_ref[0, b, h].astype(BF16))
             + _dot_nt(p_n[b].astype(BF16), vn_ref[b, h].astype(BF16)) for b in streams]
        for b in streams:
            o_ref[rows[b], hs] = o[b] / den[b]


def _dsa_sample(qa, qi, qmisc, kf, vf, kif, layer, near, nbias, fbias, *, nb, tq, topk, wi_col):
    lf = kf.shape[-1]
    ln = nbias.shape[-1]
    nbs = SAMPLE_STREAMS_PER_STEP if nb % SAMPLE_STREAMS_PER_STEP == 0 else 1
    row = lambda b: (b, 0)
    args = [qa, qi, qmisc, kf, vf, kif, *near]
    in_specs = ([pl.BlockSpec((nbs * tq, a.shape[1]), row) for a in args[:3]]
                + [pl.BlockSpec((1, nbs) + a.shape[2:], lambda b, nd=a.ndim: (layer, b) + (0,) * (nd - 2))
                   for a in args[3:6]]
                + [pl.BlockSpec((nbs,) + a.shape[1:], lambda b, nd=a.ndim: (b,) + (0,) * (nd - 1)) for a in args[6:]]
                + [pl.BlockSpec(nbias.shape, lambda b: (0, 0, 0, 0)),
                   pl.BlockSpec(fbias.shape, lambda b: (0, 0))])
    return pl.pallas_call(
        functools.partial(_dsa_sample_kernel, topk=topk, tq=tq, nbs=nbs, lf=lf, ln=ln, wi_col=wi_col),
        grid=(nb // nbs,),
        in_specs=in_specs,
        out_specs=pl.BlockSpec((nbs * tq, qa.shape[1]), row),
        out_shape=jax.ShapeDtypeStruct(qa.shape, F32),
        scratch_shapes=[pltpu.VMEM((nbs, tq, lf), F32), pltpu.VMEM((nbs, tq, ln), F32)],
        compiler_params=_params("parallel"),
        name="dsa_attention_sample",
    )(*args, nbias, fbias)


def _reduce_rows(x, op, final):
    blk = 8 * SUBLANES
    parts = [x[r0:r0 + blk] for r0 in range(0, x.shape[0], blk)]
    while len(parts) > 1:
        parts = [op(parts[i], parts[i + 1]) for i in range(0, len(parts) - 1, 2)] + parts[len(parts) & ~1:]
    return final(parts[0], axis=0, keepdims=True)


def _block_rows(x, op):
    blk = 8 * SUBLANES
    parts = [x[r0:r0 + blk] for r0 in range(0, x.shape[0], blk)]
    while len(parts) > 1:
        parts = [op(parts[i], parts[i + 1]) for i in range(0, len(parts) - 1, 2)] + parts[len(parts) & ~1:]
    return parts[0]


def _dsa_blocks_kernel(qa_ref, qi_ref, qm_ref, k_ref, v_ref, ki_ref, corr_ref, o_ref,
                       kb_ref, vt_ref, kib_ref, key_ref, sel_ref, thr_ref, stat_ref, m_ref, l_ref, ot_ref, *,
                       topk, tq, seq, wi_col):
    g = pl.program_id(1)
    kb = 2 * LANES
    w = qa_ref.shape[1]
    blk = _row_tile(seq, 4 * LANES)

    @pl.when(g == 0)
    def _():
        for r0 in range(0, seq, blk):
            kb_ref[r0:r0 + blk, :] = k_ref[0, r0:r0 + blk, :].astype(BF16)
            kib_ref[r0:r0 + blk, :] = ki_ref[0, r0:r0 + blk, :].astype(BF16)
        for j in range(seq // LANES):
            vt_ref[j] = v_ref[0, j * LANES:(j + 1) * LANES, :].T.astype(BF16)

    top = (g + 1) * tq
    nkb = (top + kb - 1) // kb
    start_of = lambda i: pl.multiple_of(jnp.maximum(top - kb * (i + 1), 0), LANES)

    qa = (qa_ref[...] * (HEAD_DIM ** -0.5)).astype(BF16)
    qi = qi_ref[...].astype(BF16)
    wi = qm_ref[:, wi_col:wi_col + N_IDX_HEADS] * (N_IDX_HEADS ** -0.5) * (D_IDX ** -0.5)
    eye_h = jnp.where(_iota((N_IDX_HEADS, N_IDX_HEADS), 0) == _iota((N_IDX_HEADS, N_IDX_HEADS), 1), 1.0, 0.0)
    wi_t = _dot_nt(eye_h, wi, HI)
    q_chunk = (g * tq + _iota((1, tq), 1)) // CHUNK

    def score_block(i, carry):
        st = start_of(i)
        kib = kib_ref[pl.ds(st, kb), :][:, :D_IDX]
        acc = jnp.zeros((kb, tq), F32)
        for n in range(N_IDX_HEADS):
            acc = acc + jnp.maximum(_dot_nt(kib, qi[:, n * D_IDX:(n + 1) * D_IDX]), 0.0) * wi_t[n:n + 1, :]
        row = st + _iota((kb, 1), 0)
        ok = jnp.logical_and(row // CHUNK <= q_chunk, row < top - kb * i)
        key_ref[i] = jnp.where(ok, _sortable(acc), jnp.int32(INT_MIN))
        return carry

    lax.fori_loop(0, nkb, score_block, 0)

    kf32 = float(topk)

    def search_blocks(n_blocks):
        def count(pred):
            parts = [_block_rows(jnp.where(pred(key_ref[i]), 1.0, 0.0), jnp.add) for i in range(n_blocks)]
            while len(parts) > 1:
                parts = [a + b for a, b in zip(parts[0::2], parts[1::2])] + parts[len(parts) & ~1:]
            return jnp.sum(parts[0], axis=0, keepdims=True)

        def search(b, t_u):
            cand_u = t_u | lax.shift_left(jnp.int32(1), 31 - b)
            cand_s = cand_u ^ jnp.int32(INT_MIN)
            return jnp.where(count(lambda x: x >= cand_s) >= kf32, cand_u, t_u)

        t_u = lax.fori_loop(0, 32, search, jnp.zeros((1, tq), jnp.int32))
        thr = t_u ^ jnp.int32(INT_MIN)
        thr_ref[0:1, :] = thr
        stat_ref[0:1, :] = kf32 - count(lambda x: x > thr)
        stat_ref[1:2, :] = count(lambda x: x == thr)

    for n_blocks in range(1, seq // kb + 1):
        pl.when(nkb == n_blocks)(functools.partial(search_blocks, n_blocks))

    thr = thr_ref[0:1, :]
    need = stat_ref[0:1, :]
    open_row = thr == jnp.int32(INT_MIN)
    conflict = jnp.logical_and(stat_ref[1:2, :] != need, jnp.logical_not(open_row))
    floor = jnp.maximum(thr, jnp.int32(INT_MIN + 1))

    def select_block(i, carry):
        sel_ref[i] = jnp.where(key_ref[i] >= floor, 0.0, NEG)
        return carry

    lax.fori_loop(0, nkb, select_block, 0)

    @pl.when(jnp.max(jnp.where(conflict, 1.0, 0.0)) > 0.0)
    def _():
        below = jnp.where(_iota((LANES, LANES), 1) < _iota((LANES, LANES), 0), 1.0, 0.0).astype(BF16)

        def tie_block(ii, offset):
            i = nkb - 1 - ii
            for r0 in range(0, kb, LANES):
                kblk = key_ref[i, r0:r0 + LANES, :]
                e = jnp.where(kblk == thr, 1.0, 0.0)
                rank = offset + jnp.dot(below, e.astype(BF16), preferred_element_type=F32)
                take = jnp.where(kblk > thr, 1.0, jnp.where(jnp.logical_or(rank >= need, open_row), 0.0, e))
                sel_ref[i, r0:r0 + LANES, :] = jnp.where(take > 0.5, 0.0, NEG)
                offset = offset + jnp.sum(e, axis=0, keepdims=True)
            return offset

        lax.fori_loop(0, nkb, tie_block, jnp.zeros((1, tq), F32))

    heads = range(N_HEADS)
    hs = [slice(h * HEAD_DIM, (h + 1) * HEAD_DIM) for h in heads]

    def attend_block(i, first, near=None):
        st = start_of(i)
        vblk = st // LANES
        sel = sel_ref[i]
        s = [_dot_nt(kb_ref[pl.ds(st, kb), hs[h]], qa[:, hs[h]]) + sel for h in heads]
        if near is not None:
            s = [s[h] + corr_ref[near, h] for h in heads]
        m_blk = [jnp.max(_block_rows(x, jnp.maximum), axis=0, keepdims=True) for x in s]
        m_old = [m_ref[h:h + 1, :] for h in heads]
        m_new = m_blk if first else [jnp.maximum(a, b) for a, b in zip(m_old, m_blk)]
        p = [jnp.exp(x - m) for x, m in zip(s, m_new)]
        l_blk = [jnp.sum(_block_rows(x, jnp.add), axis=0, keepdims=True) for x in p]
        pb = [x.astype(BF16) for x in p]
        pv = [sum(jnp.dot(vt_ref[vblk + j, hs[h], :], pb[h][j * LANES:(j + 1) * LANES, :],
                          preferred_element_type=F32) for j in range(kb // LANES)) for h in heads]
        for h in heads:
            if first:
                l_ref[h:h + 1, :] = l_blk[h]
                ot_ref[hs[h], :] = pv[h]
            else:
                alpha = jnp.exp(m_old[h] - m_new[h])
                l_ref[h:h + 1, :] = l_ref[h:h + 1, :] * alpha + l_blk[h]
                ot_ref[hs[h], :] = ot_ref[hs[h], :] * alpha + pv[h]
            m_ref[h:h + 1, :] = m_new[h]

    attend_block(0, True, 0)
    pl.when(nkb > 1)(lambda: attend_block(1, False, 1))

    def attend_rest(i, carry):
        attend_block(i, False)
        return carry

    lax.fori_loop(2, nkb, attend_rest, 0)
    for h in heads:
        ot_ref[hs[h], :] = ot_ref[hs[h], :] * (1.0 / l_ref[h:h + 1, :])
    o_ref[...] = ot_ref[...].T


def _dsa_blocks(qa, qi, qmisc, k, v, ki, corr, *, nb, tq, topk, wi_col):
    n, w = qa.shape
    seq = k.shape[1]
    nq = seq // tq
    kb = 2 * LANES
    row = lambda b, g: (b * nq + g, 0)
    per_b = lambda b, g: (b, 0, 0)
    return pl.pallas_call(
        functools.partial(_dsa_blocks_kernel, topk=topk, tq=tq, seq=seq, wi_col=wi_col),
        grid=(nb, nq),
        in_specs=[pl.BlockSpec((tq, w), row),
                  pl.BlockSpec((tq, qi.shape[1]), row),
                  pl.BlockSpec((tq, qmisc.shape[1]), row),
                  pl.BlockSpec((1, seq, w), per_b),
                  pl.BlockSpec((1, seq, w), per_b),
                  pl.BlockSpec((1, seq, ki.shape[2]), per_b),
                  pl.BlockSpec(corr.shape, lambda b, g: (0, 0, 0, 0))],
        out_specs=pl.BlockSpec((tq, w), row),
        out_shape=jax.ShapeDtypeStruct(qa.shape, F32),
        scratch_shapes=[pltpu.VMEM((seq, w), BF16),
                        pltpu.VMEM((seq // LANES, w, LANES), BF16),
                        pltpu.VMEM((seq, ki.shape[2]), BF16),
                        pltpu.VMEM((seq // kb, kb, tq), jnp.int32),
                        pltpu.VMEM((seq // kb, kb, tq), F32),
                        pltpu.VMEM((SUBLANES, tq), jnp.int32),
                        pltpu.VMEM((SUBLANES, tq), F32),
                        pltpu.VMEM((N_HEADS, tq), F32),
                        pltpu.VMEM((N_HEADS, tq), F32),
                        pltpu.VMEM((w, tq), F32)],
        compiler_params=_params("parallel", "arbitrary"),
        name="dsa_attention_prompt",
    )(qa, qi, qmisc, k, v, ki, corr)


def _unit_lower_inverse(mats, n):
    row, col = _iota((n, n), 0), _iota((n, n), 1)
    same = lambda width: row // width == col // width
    dot = functools.partial(jnp.dot, preferred_element_type=F32)
    eye = jnp.where(row == col, 1.0, 0.0)
    diag = [jnp.where(same(INV_BLOCK), a, 0.0) for a in mats]
    ts = [eye - d for d in diag]
    dks = [_split(d) for d in diag]
    k = 1
    while 2 * k < INV_BLOCK:
        dks = [_split(_dot3(dk, dk)) for dk in dks]
        ts = [t + _dot3(_split(t), dk) for t, dk in zip(ts, dks)]
        k *= 2
    width = INV_BLOCK
    while width < n:
        off = jnp.logical_and(same(2 * width), jnp.logical_not(same(width)))
        ls = [jnp.where(off, a, 0.0).astype(BF16) for a in mats]
        tb = [t.astype(BF16) for t in ts]
        tl = [dot(t, l).astype(BF16) for t, l in zip(tb, ls)]
        ts = [t - dot(x, y) for t, x, y in zip(ts, tl, tb)]
        width *= 2
    return ts


def _head_rms(x, w):
    return x * lax.rsqrt(jnp.mean(x * x, axis=-1, keepdims=True) + 1e-6) * w


def _gdn_multi_kernel(qkv_ref, misc_ref, z_ref, hist_ref, s0_ref, cw_ref, alog_ref, dtb_ref, nw_ref,
                      y_ref, sfin_ref, newhist_ref, ext_ref, s_ref, *, c, a_col, b_col):
    j = pl.program_id(1)
    hw = CONV_B - 1
    base = SUBLANES
    wq = N_HEADS * DK_B
    seqs = range(qkv_ref.shape[0])

    @pl.when(j == 0)
    def _():
        ext_ref[:, base - hw:base, :] = hist_ref[...]
        s_ref[...] = s0_ref[...]

    @pl.when(j > 0)
    def _():
        ext_ref[:, base - hw:base, :] = ext_ref[:, base + c - hw:base + c, :]

    ext_ref[:, base:base + c, :] = qkv_ref[...]
    newhist_ref[...] = ext_ref[:, base + c - hw:base + c, :]
    ri = _iota((c, c), 0)
    ci = _iota((c, c), 1)
    lower = ri >= ci
    strict = ri > ci
    tri = jnp.where(lower, 1.0, 0.0)
    eye_h = jnp.where(_iota((N_HEADS, N_HEADS), 0) == _iota((N_HEADS, N_HEADS), 1), 1.0, 0.0)

    def conv_act(b):
        conv = None
        for k in range(CONV_B):
            term = ext_ref[b, base - hw + k:base - hw + k + c, :] * cw_ref[k:k + 1, :]
            conv = term if conv is None else conv + term
        return conv * _sigmoid(conv)

    act = [conv_act(b) for b in seqs]
    beta = [_sigmoid(misc_ref[b, :, b_col:b_col + N_HEADS]) for b in seqs]
    g = [-jnp.exp(alog_ref[...]) * _softplus(misc_ref[b, :, a_col:a_col + N_HEADS] + dtb_ref[...]) for b in seqs]
    gc = [_dot_hi(tri, x) for x in g]
    gc_t = [_dot_nt(eye_h, x, HI) for x in gc]
    eg = [jnp.exp(x) for x in gc]
    e_last = [jnp.exp(x[c - 1:c, :]) for x in gc]
    e_rest = [jnp.exp(x[c - 1:c, :] - x) for x in gc]

    items = [(b, h) for b in seqs for h in range(N_HEADS)]
    col = lambda x, h: x[:, h:h + 1]
    qs = [act[b][:, h * DK_B:(h + 1) * DK_B] for b, h in items]
    ks = [act[b][:, wq + h * DK_B:wq + (h + 1) * DK_B] for b, h in items]
    vs = [act[b][:, 2 * wq + h * DV_B:2 * wq + (h + 1) * DV_B] for b, h in items]
    qs = [q * lax.rsqrt(jnp.sum(q * q, axis=-1, keepdims=True) + 1e-6) * (DK_B ** -0.5) for q in qs]
    ks = [k * lax.rsqrt(jnp.sum(k * k, axis=-1, keepdims=True) + 1e-6) for k in ks]
    kbs = [k.astype(BF16) for k in ks]
    decay = [jnp.where(lower, jnp.exp(jnp.where(lower, col(gc[b], h) - gc_t[b][h:h + 1, :], 0.0)), 0.0)
             for b, h in items]
    kk = [_dot_nt(k, k) for k in kbs]
    attn = [_dot_nt(q.astype(BF16), k) * d for q, k, d in zip(qs, kbs, decay)]
    a_mat = [jnp.where(strict, col(beta[b], h) * x * d, 0.0) for (b, h), x, d in zip(items, kk, decay)]
    t_mat = [t.astype(BF16) for t in _unit_lower_inverse(a_mat, c)]
    value = [_dot(t, v * col(beta[b], h)) for (b, h), t, v in zip(items, t_mat, vs)]
    k_cum = [_dot(t, k * (col(beta[b], h) * col(eg[b], h))) for (b, h), t, k in zip(items, t_mat, ks)]
    s_old = [s_ref[b, h] for b, h in items]
    sbs = [s.astype(BF16) for s in s_old]
    v_new = [(v - _dot(kc, s)).astype(BF16) for v, kc, s in zip(value, k_cum, sbs)]
    o_inter = [_dot(q * col(eg[b], h), s) for (b, h), q, s in zip(items, qs, sbs)]
    o = [oi + _dot(a, v) for oi, a, v in zip(o_inter, attn, v_new)]
    for i, (b, h) in enumerate(items):
        s_ref[b, h] = s_old[i] * col(e_last[b], h) + _dot_tn((ks[i] * col(e_rest[b], h)).astype(BF16), v_new[i])
    for i, (b, h) in enumerate(items):
        zh = z_ref[b, :, h * DV_B:(h + 1) * DV_B]
        y_ref[b, :, h * DV_B:(h + 1) * DV_B] = _head_rms(o[i], nw_ref[...]) * (zh * _sigmoid(zh))
    sfin_ref[...] = s_ref[...]


def _gdn(qkv, misc, z, hist, s0, conv_w, a_log, dt_bias, norm_w, *, nb, c, a_col, b_col):
    n, wqkv = qkv.shape
    seq = n // nb
    nbb = GDN_SEQS_PER_STEP if nb % GDN_SEQS_PER_STEP == 0 else 1
    tok = lambda b, j: (b, j, 0)
    per_b3 = lambda b, j: (b, 0, 0)
    per_b4 = lambda b, j: (b, 0, 0, 0)
    const2 = lambda b, j: (0, 0)
    hw = CONV_B - 1
    tokens = [a.reshape(nb, seq, a.shape[1]) for a in (qkv, misc, z)]
    y, s_fin, new_hist = pl.pallas_call(
        functools.partial(_gdn_multi_kernel, c=c, a_col=a_col, b_col=b_col),
        grid=(nb // nbb, seq // c),
        in_specs=[pl.BlockSpec((nbb, c, a.shape[2]), tok) for a in tokens]
                 + [pl.BlockSpec((nbb, hw, wqkv), per_b3),
                    pl.BlockSpec((nbb,) + s0.shape[1:], per_b4),
                    pl.BlockSpec((CONV_B, wqkv), const2),
                    pl.BlockSpec((1, N_HEADS), const2),
                    pl.BlockSpec((1, N_HEADS), const2),
                    pl.BlockSpec((1, DV_B), const2)],
        out_specs=[pl.BlockSpec((nbb, c, z.shape[1]), tok),
                   pl.BlockSpec((nbb,) + s0.shape[1:], per_b4),
                   pl.BlockSpec((nbb, hw, wqkv), per_b3)],
        out_shape=[jax.ShapeDtypeStruct((nb, seq, z.shape[1]), F32),
                   jax.ShapeDtypeStruct(s0.shape, F32),
                   jax.ShapeDtypeStruct(hist.shape, F32)],
        scratch_shapes=[pltpu.VMEM((nbb, SUBLANES + c, wqkv), F32),
                        pltpu.VMEM((nbb,) + s0.shape[1:], F32)],
        compiler_params=_params("parallel", "arbitrary"),
        name="gated_deltanet",
    )(*tokens, hist, s0, conv_w, a_log.reshape(1, -1), dt_bias.reshape(1, -1), norm_w.reshape(1, -1))
    return y.reshape(n, z.shape[1]), s_fin, new_hist


def _block_mask(rows, row_group, cols, col_group):
    return (np.arange(rows)[:, None] // row_group == np.arange(cols)[None, :] // col_group).astype(np.float32)


def _cummax_rows(x):
    rows = x.shape[0]
    row = _iota(x.shape, 0)
    sh = 1
    while sh < rows:
        x = jnp.maximum(x, jnp.where(row >= sh, pltpu.roll(x, sh, axis=0), -jnp.inf))
        sh *= 2
    return x


def _dot2(a, b):
    hi, lo = _split(a)
    return jnp.dot(hi, b, preferred_element_type=F32) + jnp.dot(lo, b, preferred_element_type=F32)


def _mlstm_dense_kernel(qk_ref, v_ref, og_ref, misc_ref, c0_ref, n0_ref, m0_ref, ib_ref, fb_ref, nw_ref,
                        el_ref, ev_ref, ek_ref, kmask_ref, vmask_ref, cmask_ref, cmaskb_ref, rms_ref, causal_ref,
                        dsel_ref,
                        y_ref, cfin_ref, nfin_ref, mfin_ref, c_ref, n_ref, m_ref, *, l, i_col, f_col):
    j = pl.program_id(1)
    wq = N_HEADS * DK_C
    nbb = qk_ref.shape[0]
    seqs = range(nbb)

    @pl.when(j == 0)
    def _():
        c_ref[...] = jnp.zeros(c_ref.shape, F32)
        for b in seqs:
            for h in range(N_HEADS):
                c_ref[b, h * DK_C:(h + 1) * DK_C, h * DV_C:(h + 1) * DV_C] = c0_ref[b, h]
        n_ref[...] = n0_ref[...]
        m_ref[...] = m0_ref[...]

    each = lambda f, *xs: [f(*args) for args in zip(*xs)]
    dot = functools.partial(jnp.dot, preferred_element_type=F32)
    spread = lambda x, e_ref: sum(dot(p, e_ref[...]) for p in _split3(x))
    tri = jnp.where(_iota((l, l), 0) >= _iota((l, l), 1), 1.0, 0.0).astype(BF16)
    ig = [misc_ref[b, :, i_col:i_col + N_HEADS] + ib_ref[...] for b in seqs]
    lf = [-_softplus(-(misc_ref[b, :, f_col:f_col + N_HEADS] + fb_ref[...])) for b in seqs]
    fc = each(lambda x: sum(dot(tri, p) for p in _split3(x)), lf)
    m_prev = [m_ref[b] for b in seqs]
    log_inter = each(jnp.add, fc, m_prev)
    a = each(jnp.subtract, ig, fc)
    m_t = each(lambda li, f, x: jnp.maximum(li, f + _cummax_rows(x)), log_inter, fc, a)
    w_inter = each(lambda li, m: jnp.exp(li - m), log_inter, m_t)
    w_last = each(lambda f, i, m: jnp.exp(f[l - 1:l, :] - f + i - m[l - 1:l, :]), fc, ig, m_t)

    key_terms = each(lambda x: jnp.sum(spread(x, el_ref) * dsel_ref[...], axis=0, keepdims=True), a)
    log_w = each(lambda f, kt: jnp.where(causal_ref[...] > 0.5, spread(f, el_ref) + kt, -jnp.inf), fc, key_terms)
    q = [qk_ref[b, :, :wq] for b in seqs]
    q_b = [x.astype(BF16) for x in q]
    k_s = [qk_ref[b, :, wq:] * (DK_C ** -0.5) for b in seqs]
    k_b = [x.astype(BF16) for x in k_s]
    v_b = [v_ref[b].astype(BF16) for b in seqs]
    kt_bd = [jnp.concatenate([x] * N_HEADS, axis=0) * kmask_ref[...] for x in k_b]
    v_bd = [jnp.concatenate([x] * N_HEADS, axis=0) * vmask_ref[...] for x in v_b]
    m_l = each(lambda m: spread(m, el_ref), m_t)
    qkw = each(lambda x, kt, lw, m: (_dot_nt(x, kt) * jnp.exp(lw - m)).astype(BF16), q_b, kt_bd, log_w, m_l)
    c_old = [c_ref[b] for b in seqs]
    n_old = [n_ref[b] for b in seqs]
    wi_s = each(lambda w: spread(w, ev_ref), w_inter)
    q_c = each(lambda x, c: dot(x, c.astype(BF16)), q_b, c_old)
    qn = each(lambda x, n: _dot2(x * n, cmaskb_ref[...]), q, n_old)
    num = each(lambda w, vb, wi, qc: dot(w, vb) + wi * qc, qkw, v_bd, wi_s, q_c)
    den = each(lambda w, wi, x: dot(w, vmask_ref[...]) + wi * x, qkw, wi_s, qn)
    m_v = m_l if l == DV_C else each(lambda m: spread(m, ev_ref), m_t)
    hid = each(lambda nu, de, m: nu / jnp.maximum(jnp.abs(de), jnp.exp(-m)), num, den, m_v)
    ms = each(lambda x: _dot2(x * x, rms_ref[...]), hid)
    for b in seqs:
        y_ref[b] = hid[b] * lax.rsqrt(ms[b] + 1e-6) * nw_ref[...] * _sigmoid(og_ref[b])

    kw = each(lambda k, w: k * spread(w, ek_ref), k_s, w_last)
    dec_rows = [jnp.broadcast_to(w[l - 1:l, :], (SUBLANES, N_HEADS)) for w in w_inter]
    upd = each(lambda x, vb: _dot_tn(x.astype(BF16), vb), kw, v_b)
    for b in seqs:
        c_ref[b] = c_old[b] * spread(dec_rows[b], ev_ref)[0:1, :] + upd[b] * cmask_ref[...]
        n_ref[b] = n_old[b] * spread(dec_rows[b], ek_ref)[0:1, :] + jnp.sum(kw[b], axis=0, keepdims=True)
        m_ref[b] = m_t[b][l - 1:l, :]

    @pl.when(j == pl.num_programs(1) - 1)
    def _():
        for b in seqs:
            for h in range(N_HEADS):
                cfin_ref[b, h] = c_ref[b, h * DK_C:(h + 1) * DK_C, h * DV_C:(h + 1) * DV_C]
        nfin_ref[...] = n_ref[...]
        mfin_ref[...] = m_ref[...]


def _mlstm_dense(qk, v, og, misc, c0, n0, m0, i_bias, f_bias, norm_w, *, nb, l, i_col, f_col):
    n = qk.shape[0]
    seq = n // nb
    wq, wv = N_HEADS * DK_C, N_HEADS * DV_C
    hl = N_HEADS * l
    nbb = SEQS_PER_STEP if nb % SEQS_PER_STEP == 0 else 1
    tok = lambda b, j: (b, j, 0)
    per_b3 = lambda b, j: (b, 0, 0)
    per_b4 = lambda b, j: (b, 0, 0, 0)
    const2 = lambda b, j: (0, 0)
    consts = [jnp.asarray(_block_mask(N_HEADS, 1, hl, l), BF16),
              jnp.asarray(_block_mask(N_HEADS, 1, wv, DV_C), BF16),
              jnp.asarray(_block_mask(N_HEADS, 1, wq, DK_C), BF16),
              jnp.asarray(_block_mask(hl, l, wq, DK_C), BF16),
              jnp.asarray(_block_mask(hl, l, wv, DV_C), BF16),
              jnp.asarray(_block_mask(wq, DK_C, wv, DV_C)),
              jnp.asarray(_block_mask(wq, DK_C, wv, DV_C), BF16),
              jnp.asarray(_block_mask(wv, DV_C, wv, DV_C) / DV_C, BF16),
              jnp.asarray(np.tile(np.tril(np.ones((l, l), np.float32)), (1, N_HEADS))),
              jnp.asarray(np.tile(np.eye(l, dtype=np.float32), (1, N_HEADS)))]
    tokens = [a.reshape(nb, seq, a.shape[1]) for a in (qk, v, og, misc)]
    outs = pl.pallas_call(
        functools.partial(_mlstm_dense_kernel, l=l, i_col=i_col, f_col=f_col),
        grid=(nb // nbb, seq // l),
        in_specs=[pl.BlockSpec((nbb, l, a.shape[2]), tok) for a in tokens]
                 + [pl.BlockSpec((nbb,) + c0.shape[1:], per_b4),
                    pl.BlockSpec((nbb, 1, wq), per_b3),
                    pl.BlockSpec((nbb, 1, N_HEADS), per_b3),
                    pl.BlockSpec((1, N_HEADS), const2),
                    pl.BlockSpec((1, N_HEADS), const2),
                    pl.BlockSpec((1, wv), const2)]
                 + [pl.BlockSpec(c.shape, const2) for c in consts],
        out_specs=[pl.BlockSpec((nbb, l, wv), tok),
                   pl.BlockSpec((nbb,) + c0.shape[1:], per_b4),
                   pl.BlockSpec((nbb, 1, wq), per_b3),
                   pl.BlockSpec((nbb, 1, N_HEADS), per_b3)],
        out_shape=[jax.ShapeDtypeStruct((nb, seq, wv), F32),
                   jax.ShapeDtypeStruct(c0.shape, F32),
                   jax.ShapeDtypeStruct((nb, 1, wq), F32),
                   jax.ShapeDtypeStruct((nb, 1, N_HEADS), F32)],
        scratch_shapes=[pltpu.VMEM((nbb, wq, wv), F32),
                        pltpu.VMEM((nbb, 1, wq), F32),
                        pltpu.VMEM((nbb, 1, N_HEADS), F32)],
        compiler_params=_params("parallel", "arbitrary"),
        name="mlstm",
    )(*tokens, c0, n0.reshape(nb, 1, wq), m0.reshape(nb, 1, N_HEADS), i_bias.reshape(1, -1), f_bias.reshape(1, -1),
      jnp.tile(norm_w, N_HEADS).reshape(1, wv), *consts)
    y, c_fin, n_fin, m_fin = outs
    return y.reshape(n, wv), c_fin, n_fin.reshape(nb, N_HEADS, DK_C), m_fin.reshape(nb, N_HEADS)


def _band_prompt_kernel(q_ref, k_ref, v_ref, bias_ref, o_ref, kb_ref, vt_ref, ot_ref, *, tq, seq):
    g = pl.program_id(1)
    w = q_ref.shape[1]
    pad = BAND_CHUNKS * CHUNK
    lw = pad + tq
    npad = pad // LANES
    per_tile = tq // LANES
    blk = _row_tile(seq, 4 * LANES)

    @pl.when(g == 0)
    def _():
        kb_ref[0:pad, :] = jnp.zeros((pad, w), BF16)
        vt_ref[0:npad] = jnp.zeros((npad, w, LANES), BF16)
        for r0 in range(0, seq, blk):
            kb_ref[pad + r0:pad + r0 + blk, :] = k_ref[0, r0:r0 + blk, :].astype(BF16)
        for j in range(seq // LANES):
            vt_ref[npad + j] = v_ref[0, j * LANES:(j + 1) * LANES, :].T.astype(BF16)

    start = pl.multiple_of(g * tq, tq)
    q = (q_ref[...] * (HEAD_DIM ** -0.5)).astype(BF16)
    before_seq = jnp.where(_iota((lw, tq), 0) >= pad - g * tq, 0.0, NEG)
    group = 4
    for h0 in range(0, N_HEADS, group):
        heads = range(h0, h0 + group)
        hs = {h: slice(h * HEAD_DIM, (h + 1) * HEAD_DIM) for h in heads}
        s = [_dot_nt(kb_ref[pl.ds(start, lw), hs[h]], q[:, hs[h]]) + (bias_ref[h] + before_seq) for h in heads]
        p = [jnp.exp(x - _reduce_rows(x, jnp.maximum, jnp.max)) for x in s]
        inv = [1.0 / _reduce_rows(x, jnp.add, jnp.sum) for x in p]
        pb = [x.astype(BF16) for x in p]
        acc = [sum(jnp.dot(vt_ref[g * per_tile + j, hs[h], :], x[j * LANES:(j + 1) * LANES, :],
                           preferred_element_type=F32) for j in range(lw // LANES)) for h, x in zip(heads, pb)]
        for h, a, r in zip(heads, acc, inv):
            ot_ref[hs[h], :] = a * r
    o_ref[...] = ot_ref[...].T


def _band_prompt(q, k, v, bias, *, nb, tq):
    n, w = q.shape
    seq = k.shape[1]
    nq = seq // tq
    pad = BAND_CHUNKS * CHUNK
    row = lambda b, g: (b * nq + g, 0)
    per_b = lambda b, g: (b, 0, 0)
    return pl.pallas_call(
        functools.partial(_band_prompt_kernel, tq=tq, seq=seq),
        grid=(nb, nq),
        in_specs=[pl.BlockSpec((tq, w), row),
                  pl.BlockSpec((1, seq, w), per_b),
                  pl.BlockSpec((1, seq, w), per_b),
                  pl.BlockSpec(bias.shape, lambda b, g: (0, 0, 0))],
        out_specs=pl.BlockSpec((tq, w), row),
        out_shape=jax.ShapeDtypeStruct(q.shape, F32),
        scratch_shapes=[pltpu.VMEM((pad + seq, w), BF16),
                        pltpu.VMEM(((pad + seq) // LANES, w, LANES), BF16),
                        pltpu.VMEM((w, tq), F32)],
        compiler_params=_params("parallel", "arbitrary"),
        name="band_attention_prompt",
    )(q, k, v, bias)


def _band_sample_kernel(q_ref, kc_ref, vc_ref, kn_ref, vn_ref, biasc_ref, biasn_ref, o_ref, *, tq, nbs):
    streams = range(nbs)
    rows = [slice(b * tq, (b + 1) * tq) for b in streams]
    dot = functools.partial(jnp.dot, preferred_element_type=F32)
    for h in range(N_HEADS):
        hs = slice(h * HEAD_DIM, (h + 1) * HEAD_DIM)
        qh = [q_ref[rows[b], hs].astype(BF16) for b in streams]
        s_c = [dot(qh[b], kc_ref[0, b, h].astype(BF16)) * (HEAD_DIM ** -0.5) + biasc_ref[h] for b in streams]
        s_n = [dot(qh[b], kn_ref[b, h].astype(BF16)) * (HEAD_DIM ** -0.5) + biasn_ref[h] for b in streams]
        m = [jnp.maximum(jnp.max(x, axis=-1, keepdims=True), jnp.max(y, axis=-1, keepdims=True))
             for x, y in zip(s_c, s_n)]
        p_c = [jnp.exp(x - mm) for x, mm in zip(s_c, m)]
        p_n = [jnp.exp(x - mm) for x, mm in zip(s_n, m)]
        den = [jnp.sum(x, axis=-1, keepdims=True) + jnp.sum(y, axis=-1, keepdims=True) for x, y in zip(p_c, p_n)]
        o = [_dot_nt(p_c[b].astype(BF16), vc_ref[0, b, h].astype(BF16))
             + _dot_nt(p_n[b].astype(BF16), vn_ref[b, h].astype(BF16)) for b in streams]
        for b in streams:
            o_ref[rows[b], hs] = o[b] / den[b]


def _band_sample(q, kc, vc, layer, kn, vn, bias_c, bias_n, *, nb, tq):
    n, w = q.shape
    nbs = SAMPLE_STREAMS_PER_STEP if nb % SAMPLE_STREAMS_PER_STEP == 0 else 1
    row = lambda b: (b, 0)
    const3 = lambda b: (0, 0, 0)
    return pl.pallas_call(
        functools.partial(_band_sample_kernel, tq=tq, nbs=nbs),
        grid=(nb // nbs,),
        in_specs=[pl.BlockSpec((nbs * tq, w), row)]
                 + [pl.BlockSpec((1, nbs) + a.shape[2:], lambda b: (layer, b, 0, 0, 0)) for a in (kc, vc)]
                 + [pl.BlockSpec((nbs,) + a.shape[1:], lambda b: (b, 0, 0, 0)) for a in (kn, vn)]
                 + [pl.BlockSpec(bias_c.shape, const3), pl.BlockSpec(bias_n.shape, const3)],
        out_specs=pl.BlockSpec((nbs * tq, w), row),
        out_shape=jax.ShapeDtypeStruct(q.shape, F32),
        compiler_params=_params("parallel"),
        name="band_attention_sample",
    )(q, kc, vc, kn, vn, bias_c, bias_n)


def _t5_bucket(rel):
    nb = T5_BUCKETS // 2
    max_exact = nb // 2
    n = jnp.abs(rel)
    n_f = jnp.maximum(n, 1).astype(jnp.float32)
    large = max_exact + (jnp.log(n_f / max_exact) / math.log(T5_MAX_DIST / max_exact) * (nb - max_exact)).astype(jnp.int32)
    large = jnp.minimum(large, nb - 1)
    return jnp.where(rel > 0, nb, 0) + jnp.where(n < max_exact, n, large)


def _toeplitz_bias(fn, n_rows, n_cols):
    n = n_rows + n_cols
    m = np.arange(n)
    f = jnp.transpose(fn(np.where(m < n_cols, m, m - n))).astype(F32)
    flat = jnp.tile(f, (1, n_rows))[:, :n_rows * (n - 1)]
    return flat.reshape(f.shape[0], n_rows, n - 1)[:, :, :n_cols]


def _pack_cols(w, sizes, groups):
    offs = np.concatenate([[0], np.cumsum(sizes)])
    cols, widths = [], []
    for grp in groups:
        width = 0
        for idx in grp:
            cols.append(w[:, offs[idx]:offs[idx + 1]])
            width += sizes[idx]
        pad = (-width) % LANES
        if pad:
            cols.append(jnp.zeros((w.shape[0], pad), w.dtype))
        widths.append(width + pad)
    return jnp.concatenate(cols, axis=1).astype(BF16), tuple(widths)


def _row_tile(n, target):
    t = min(n, target)
    while n % t:
        t //= 2
    return t


def kernel(x_prompt, x_sample, cache_a_k, cache_a_v, cache_a_kidx, state_b_s, state_b_conv, state_c_c, state_c_n, state_c_m, cache_d_k, cache_d_v, state_ffn_conv, w_in_even, w_out_even, t5_bias, b_conv_w, b_a_log, b_dt_bias, b_norm_w, w_in_odd, w_out_odd, c_i_bias, c_f_bias, c_norm_w, d_rel_bias, ln_mix_g, ln_mix_b, ln_ffn_g, ln_ffn_b, ffn_w_up, ffn_conv_w, ffn_w_down):
    bp, sp, d = x_prompt.shape
    bs, ts, _ = x_sample.shape
    depth = ffn_w_up.shape[0]
    past = cache_a_k.shape[2]
    d_win = cache_d_k.shape[2]
    dff = ffn_w_down.shape[1]
    alpha = (2 * depth) ** 0.25
    w_a = N_HEADS * HEAD_DIM
    w_b = N_HEADS * DV_B
    w_c = N_HEADS * DV_C
    qkv_b_w = 2 * N_HEADS * DK_B + w_b
    even_sizes = (w_a, w_a, w_a, N_IDX_HEADS * D_IDX, D_IDX, N_IDX_HEADS, qkv_b_w, N_HEADS, N_HEADS, w_b)
    odd_sizes = (N_HEADS * DK_C, N_HEADS * DK_C, w_c, N_HEADS, N_HEADS, w_c, w_a, w_a, w_a)
    even_groups = ((0,), (1,), (2,), (3,), (6,), (9,), (4, 5, 7, 8))
    wi_col, a_col, b_col = D_IDX, D_IDX + N_IDX_HEADS, D_IDX + N_IDX_HEADS + N_HEADS
    odd_groups = ((0, 1), (2,), (5,), (6,), (7,), (8,), (3, 4))
    i_col, f_col = 0, N_HEADS

    assert sp % CHUNK == 0 and ts <= CHUNK and past % CHUNK == 0 and past >= T5_FAR
    assert (past + ts - 1) // CHUNK == past // CHUNK
    topk_p = min(TOPK_MAX, sp // 4)
    topk_s = min(TOPK_MAX, (past + ts) // 4)
    n_p, n_s = bp * sp, bs * ts
    tm_p = _row_tile(n_p, 512)
    tm_s = _row_tile(n_s, 512)
    tff_p = _row_tile(sp, 512)
    ns_s = _row_tile(bs, max(1, 256 // ts))

    t5 = lambda rel: t5_bias[_t5_bucket(jnp.asarray(rel, jnp.int32))]
    fbias = t5(np.array([-T5_FAR - 1]))
    tq_dsa = 2 * LANES
    assert sp % tq_dsa == 0 and T5_FAR <= tq_dsa
    corr_p = jnp.stack([_toeplitz_bias(lambda dd: t5(-dd - tq_dsa * i) - fbias, tq_dsa, tq_dsa) for i in range(2)])
    tq_a = 2 * CHUNK
    ln_s = T5_FAR + ts
    nbias_s = _toeplitz_bias(lambda dd: t5(dd - T5_FAR), ts, ln_s)[None]

    lw = BAND_CHUNKS * CHUNK + tq_a
    r_chunk = np.arange(lw)[:, None] // CHUNK
    q_chunk = BAND_CHUNKS + np.arange(tq_a)[None, :] // CHUNK
    band_ok = (r_chunk >= q_chunk - BAND_CHUNKS) & (r_chunk <= q_chunk)
    pos_q = past + np.arange(ts)
    pos_kc = past - d_win + np.arange(d_win)
    def band_valid(pos_k):
        kch, qch = pos_k // CHUNK, pos_q // CHUNK
        return (pos_k[None] >= 0) & (kch[None] >= qch[:, None] - BAND_CHUNKS) & (kch[None] <= qch[:, None])
    def band_bias(table, shift, n_rows, n_cols, valid=None):
        bias = _toeplitz_bias(lambda dd: table[np.clip(dd + shift, -REL_CLIP, REL_CLIP) + REL_CLIP], n_rows, n_cols)
        return bias if valid is None else jnp.where(jnp.asarray(valid)[None], bias, NEG)

    cak_t = jnp.transpose(cache_a_k, (0, 1, 3, 4, 2))
    cav_t = jnp.transpose(cache_a_v, (0, 1, 3, 4, 2))
    caki_t = jnp.transpose(cache_a_kidx, (0, 1, 3, 2))
    cdk_t = jnp.transpose(cache_d_k, (0, 1, 3, 4, 2))
    cdv_t = jnp.transpose(cache_d_v, (0, 1, 3, 4, 2))

    xp = x_prompt.reshape(n_p, d)
    xs = x_sample.reshape(n_s, d)
    outs = {k: [] for k in ("ak_p", "ak_s", "av_p", "av_s", "aki_p", "aki_s", "bs_p", "bs_s", "bc_p", "bc_s",
                            "cc_p", "cc_s", "cn_p", "cn_s", "cm_p", "cm_s", "dk_p", "dk_s", "dv_p", "dv_s",
                            "fc_p", "fc_s")}
    for layer in range(depth):
        if layer % 2 == 0:
            e = layer // 2
            w_in, widths = _pack_cols(w_in_even[e], even_sizes, even_groups)
            w_out = w_out_even[e].astype(BF16)
            qa, ka, va, qi, qkv_b, z_b, misc = _proj(xp, w_in, widths, tm_p)
            o_a = _dsa_blocks(qa, qi, misc, ka.reshape(bp, sp, w_a), va.reshape(bp, sp, w_a),
                              misc.reshape(bp, sp, LANES), corr_p, nb=bp, tq=tq_dsa, topk=topk_p, wi_col=wi_col)
            y_b, s_b, h_b = _gdn(qkv_b, misc, z_b, jnp.zeros((bp, CONV_B - 1, qkv_b_w), F32),
                                 jnp.zeros((bp, N_HEADS, DK_B, DV_B), F32), b_conv_w[e], b_a_log[e], b_dt_bias[e],
                                 b_norm_w[e], nb=bp, c=CHUNK, a_col=a_col, b_col=b_col)
            xp = _mm_res_ln([o_a, y_b], [w_out[:w_a], w_out[w_a:]], xp, ln_mix_g[layer], ln_mix_b[layer], alpha, tm_p)
            outs["ak_p"].append(ka.reshape(bp, sp, N_HEADS, HEAD_DIM))
            outs["av_p"].append(va.reshape(bp, sp, N_HEADS, HEAD_DIM))
            outs["aki_p"].append(misc[:, :D_IDX].reshape(bp, sp, D_IDX))
            outs["bs_p"].append(s_b)
            outs["bc_p"].append(h_b)
            qa, ka, va, qi, qkv_b, z_b, misc = _proj(xs, w_in, widths, tm_s)
            ki = misc[:, :D_IDX]
            heads_t = lambda u: jnp.transpose(u.reshape(bs, ts, N_HEADS, HEAD_DIM), (0, 2, 3, 1))
            near = (jnp.concatenate([cak_t[e, ..., past - T5_FAR:], heads_t(ka)], axis=-1),
                    jnp.concatenate([cav_t[e, ..., past - T5_FAR:], heads_t(va)], axis=-1),
                    jnp.concatenate([caki_t[e, ..., past - T5_FAR:],
                                     jnp.transpose(ki.reshape(bs, ts, D_IDX), (0, 2, 1))], axis=-1))
            o_a = _dsa_sample(qa, qi, misc, cak_t, cav_t, caki_t, e, near, nbias_s, fbias, nb=bs, tq=ts,
                              topk=topk_s, wi_col=wi_col)
            y_b, s_b, h_b = _gdn(qkv_b, misc, z_b, state_b_conv[e], state_b_s[e], b_conv_w[e], b_a_log[e],
                                 b_dt_bias[e], b_norm_w[e], nb=bs, c=ts, a_col=a_col, b_col=b_col)
            xs = _mm_res_ln([o_a, y_b], [w_out[:w_a], w_out[w_a:]], xs, ln_mix_g[layer], ln_mix_b[layer], alpha, tm_s)
            outs["ak_s"].append(ka.reshape(bs, ts, N_HEADS, HEAD_DIM))
            outs["av_s"].append(va.reshape(bs, ts, N_HEADS, HEAD_DIM))
            outs["aki_s"].append(ki.reshape(bs, ts, D_IDX))
            outs["bs_s"].append(s_b)
            outs["bc_s"].append(h_b)
        else:
            o = layer // 2
            w_in, widths = _pack_cols(w_in_odd[o], odd_sizes, odd_groups)
            w_out = w_out_odd[o].astype(BF16)
            qk_c, v_c, o_c, q_d, k_d, v_d, misc = _proj(xp, w_in, widths, tm_p)
            y_c, c_c, c_n, c_m = _mlstm_dense(qk_c, v_c, o_c, misc,jnp.zeros((bp, N_HEADS, DK_C, DV_C), F32),
                                        jnp.zeros((bp, N_HEADS, DK_C), F32), jnp.zeros((bp, N_HEADS), F32),
                                        c_i_bias[o], c_f_bias[o], c_norm_w[o], nb=bp, l=CHUNK, i_col=i_col, f_col=f_col)
            k3 = k_d.reshape(bp, sp, w_a)
            v3 = v_d.reshape(bp, sp, w_a)
            bias_bp = _toeplitz_bias(
                lambda dd: d_rel_bias[o][np.clip(-dd - BAND_CHUNKS * CHUNK, -REL_CLIP, REL_CLIP) + REL_CLIP], lw, tq_a)
            bias_bp = jnp.where(jnp.asarray(band_ok)[None], bias_bp, NEG)
            o_d = _band_prompt(q_d, k3, v3, bias_bp, nb=bp, tq=tq_a)
            xp = _mm_res_ln([y_c, o_d], [w_out[:w_c], w_out[w_c:]], xp, ln_mix_g[layer], ln_mix_b[layer], alpha, tm_p)
            d_win_p = min(BAND_CHUNKS * CHUNK, sp)
            outs["cc_p"].append(c_c)
            outs["cn_p"].append(c_n)
            outs["cm_p"].append(c_m)
            outs["dk_p"].append(k3[:, sp - d_win_p:].reshape(bp, d_win_p, N_HEADS, HEAD_DIM))
            outs["dv_p"].append(v3[:, sp - d_win_p:].reshape(bp, d_win_p, N_HEADS, HEAD_DIM))
            qk_c, v_c, o_c, q_d, k_d, v_d, misc = _proj(xs, w_in, widths, tm_s)
            y_c, c_c, c_n, c_m = _mlstm_dense(qk_c, v_c, o_c, misc,state_c_c[o], state_c_n[o], state_c_m[o],
                                        c_i_bias[o], c_f_bias[o], c_norm_w[o], nb=bs, l=ts, i_col=i_col, f_col=f_col)
            heads_t = lambda u: jnp.transpose(u.reshape(bs, ts, N_HEADS, HEAD_DIM), (0, 2, 3, 1))
            o_d = _band_sample(q_d, cdk_t, cdv_t, o, heads_t(k_d), heads_t(v_d),
                               band_bias(d_rel_bias[o], -d_win, ts, d_win, band_valid(pos_kc)),
                               band_bias(d_rel_bias[o], 0, ts, ts, band_valid(pos_q)), nb=bs, tq=ts)
            xs = _mm_res_ln([y_c, o_d], [w_out[:w_c], w_out[w_c:]], xs, ln_mix_g[layer], ln_mix_b[layer], alpha, tm_s)
            outs["cc_s"].append(c_c)
            outs["cn_s"].append(c_n)
            outs["cm_s"].append(c_m)
            outs["dk_s"].append(k_d.reshape(bs, ts, N_HEADS, HEAD_DIM))
            outs["dv_s"].append(v_d.reshape(bs, ts, N_HEADS, HEAD_DIM))
        w_up = ffn_w_up[layer].astype(BF16)
        w_down = ffn_w_down[layer].astype(BF16)
        act, hist_p = _ffn_up(xp, w_up, ffn_conv_w[layer], jnp.zeros((bp, CONV_FF - 1, 2 * dff), F32),
                              1, tff_p, sp // tff_p)
        xp = _mm_res_ln([act], [w_down], xp, ln_ffn_g[layer], ln_ffn_b[layer], alpha, tm_p)
        act, hist_s = _ffn_up(xs, w_up, ffn_conv_w[layer], state_ffn_conv[layer], ns_s, ts, 1)
        xs = _mm_res_ln([act], [w_down], xs, ln_ffn_g[layer], ln_ffn_b[layer], alpha, tm_s)
        outs["fc_p"].append(hist_p)
        outs["fc_s"].append(hist_s)

    st = lambda k: jnp.stack(outs[k])
    return (xp.reshape(bp, sp, d), xs.reshape(bs, ts, d),
            st("ak_p"), st("ak_s"), st("av_p"), st("av_s"), st("aki_p"), st("aki_s"),
            st("bs_p"), st("bs_s"), st("bc_p"), st("bc_s"),
            st("cc_p"), st("cc_s"), st("cn_p"), st("cn_s"), st("cm_p"), st("cm_s"),
            st("dk_p"), st("dk_s"), st("dv_p"), st("dv_s"),
            st("fc_p"), st("fc_s"))
```

```python
import functools
import math

import numpy as np
import jax
import jax.numpy as jnp
from jax import lax
from jax.experimental import pallas as pl
from jax.experimental.pallas import tpu as pltpu

F32 = jnp.float32
BF16 = jnp.bfloat16
HI = lax.Precision.HIGHEST

CHUNK = 64
HEAD_DIM = 64
N_HEADS = 8
N_IDX_HEADS = 8
D_IDX = 64
TOPK_MAX = 256
T5_BUCKETS = 32
T5_MAX_DIST = 128
DK_B = 64
DV_B = 64
CONV_B = 4
DK_C = 32
DV_C = 64
BAND_CHUNKS = 8
REL_CLIP = 128
CONV_FF = 3
T5_FAR = 128
INV_BLOCK = 8
SAMPLE_STREAMS_PER_STEP = 4
GDN_SEQS_PER_STEP = 2
SEQS_PER_STEP = 8

LANES = 128
SUBLANES = 8
VMEM_LIMIT = 56 * 1024 * 1024

NEG = -1e30
INT_MIN = -2 ** 31


def _params(*sem):
    return pltpu.CompilerParams(dimension_semantics=sem, vmem_limit_bytes=VMEM_LIMIT)


def _dot(a, b):
    return jnp.dot(a.astype(BF16), b.astype(BF16), preferred_element_type=F32)


def _dot_nt(a, b, precision=None):
    return lax.dot_general(a, b, (((1,), (1,)), ((), ())), precision=precision, preferred_element_type=F32)


def _dot_tn(a, b, precision=None):
    return lax.dot_general(a, b, (((0,), (0,)), ((), ())), precision=precision, preferred_element_type=F32)


def _dot_hi(a, b):
    return jnp.dot(a, b, precision=HI, preferred_element_type=F32)


def _split(a):
    hi = a.astype(BF16)
    return hi, (a - hi.astype(F32)).astype(BF16)


def _split3(a):
    p1 = a.astype(BF16)
    r1 = a - p1.astype(F32)
    p2 = r1.astype(BF16)
    return p1, p2, (r1 - p2.astype(F32)).astype(BF16)


def _dot3(a, b):
    a_hi, a_lo = a
    b_hi, b_lo = b
    d = functools.partial(jnp.dot, preferred_element_type=F32)
    return d(a_hi, b_hi) + (d(a_hi, b_lo) + d(a_lo, b_hi))


def _sigmoid(x):
    return 1.0 / (1.0 + jnp.exp(-x))


def _softplus(x):
    return jnp.maximum(x, 0.0) + jnp.log(1.0 + jnp.exp(-jnp.abs(x)))


def _iota(shape, dim):
    return lax.broadcasted_iota(jnp.int32, shape, dim)


def _proj_kernel(x_ref, w_ref, *out_refs, sizes):
    xb = x_ref[...].astype(BF16)
    off = 0
    for o_ref, size in zip(out_refs, sizes):
        o_ref[...] = jnp.dot(xb, w_ref[:, off:off + size], preferred_element_type=F32)
        off += size


def _proj(x2d, w, sizes, tm):
    n, d = x2d.shape
    return pl.pallas_call(
        functools.partial(_proj_kernel, sizes=sizes),
        grid=(n // tm,),
        in_specs=[pl.BlockSpec((tm, d), lambda i: (i, 0)),
                  pl.BlockSpec((d, sum(sizes)), lambda i: (0, 0))],
        out_specs=[pl.BlockSpec((tm, s), lambda i: (i, 0)) for s in sizes],
        out_shape=[jax.ShapeDtypeStruct((n, s), F32) for s in sizes],
        compiler_params=_params("parallel"),
        name="in_proj",
    )(x2d, w)


def _mm_res_ln_kernel(*refs, nparts, alpha):
    part_refs = refs[:nparts]
    w_refs = refs[nparts:2 * nparts]
    x_ref, g_ref, b_ref, o_ref = refs[2 * nparts:]
    acc = alpha * x_ref[...]
    for p_ref, w_ref in zip(part_refs, w_refs):
        acc = acc + jnp.dot(p_ref[...].astype(BF16), w_ref[...], preferred_element_type=F32)
    mu = jnp.mean(acc, axis=-1, keepdims=True)
    cen = acc - mu
    var = jnp.mean(cen * cen, axis=-1, keepdims=True)
    o_ref[...] = cen * lax.rsqrt(var + 1e-5) * g_ref[...] + b_ref[...]


def _mm_res_ln(parts, ws, x2d, g, b, alpha, tm):
    n, d = x2d.shape
    nparts = len(parts)
    in_specs = ([pl.BlockSpec((tm, p.shape[1]), lambda i: (i, 0)) for p in parts]
                + [pl.BlockSpec(w.shape, lambda i: (0, 0)) for w in ws]
                + [pl.BlockSpec((tm, d), lambda i: (i, 0)),
                   pl.BlockSpec((1, d), lambda i: (0, 0)),
                   pl.BlockSpec((1, d), lambda i: (0, 0))])
    return pl.pallas_call(
        functools.partial(_mm_res_ln_kernel, nparts=nparts, alpha=alpha),
        grid=(n // tm,),
        in_specs=in_specs,
        out_specs=pl.BlockSpec((tm, d), lambda i: (i, 0)),
        out_shape=jax.ShapeDtypeStruct((n, d), F32),
        compiler_params=_params("parallel"),
        name="out_proj_ln",
    )(*parts, *ws, x2d, g.reshape(1, d), b.reshape(1, d))


def _ffn_up_kernel(x_ref, w_ref, cw_ref, hist_ref, act_ref, newhist_ref, ext_ref, *,
                   ns, tt, tiles_per_seq, dff, cc):
    i = pl.program_id(0)
    tm = ns * tt
    hw = CONV_FF - 1
    base = SUBLANES
    if tiles_per_seq == 1:
        ext_ref[:, base - hw:base, :] = hist_ref[...]
    else:
        @pl.when(i % tiles_per_seq == 0)
        def _():
            ext_ref[:, base - hw:base, :] = hist_ref[...]

        @pl.when(i % tiles_per_seq != 0)
        def _():
            ext_ref[:, base - hw:base, :] = ext_ref[:, base + tt - hw:base + tt, :]
    xb = x_ref[...].astype(BF16)
    for j in range(2 * dff // cc):
        cols = slice(j * cc, (j + 1) * cc)
        h = jnp.dot(xb, w_ref[:, cols], preferred_element_type=F32)
        ext_ref[:, base:base + tt, cols] = h.reshape(ns, tt, cc)
    newhist_ref[...] = ext_ref[:, base + tt - hw:base + tt, :]

    def conv(cols):
        acc = None
        for k in range(CONV_FF):
            term = ext_ref[:, base - hw + k:base - hw + k + tt, cols] * cw_ref[k:k + 1, cols]
            acc = term if acc is None else acc + term
        return acc

    for j in range(dff // cc):
        g = conv(slice(j * cc, (j + 1) * cc))
        u = conv(slice(dff + j * cc, dff + (j + 1) * cc))
        act = g * _sigmoid(g) * u
        act_ref[:, j * cc:(j + 1) * cc] = act.reshape(tm, cc).astype(BF16)


def _ffn_up(x2d, w_up, conv_w, hist, ns, tt, tiles_per_seq):
    n, d = x2d.shape
    c2 = w_up.shape[1]
    dff = c2 // 2
    tm = ns * tt
    cc = 256
    hw = CONV_FF - 1
    if tiles_per_seq == 1:
        hist_map = lambda i: (i, 0, 0)
    else:
        hist_map = lambda i: (i // tiles_per_seq, 0, 0)
    return pl.pallas_call(
        functools.partial(_ffn_up_kernel, ns=ns, tt=tt, tiles_per_seq=tiles_per_seq, dff=dff, cc=cc),
        grid=(n // tm,),
        in_specs=[pl.BlockSpec((tm, d), lambda i: (i, 0)),
                  pl.BlockSpec((d, c2), lambda i: (0, 0)),
                  pl.BlockSpec((CONV_FF, c2), lambda i: (0, 0)),
                  pl.BlockSpec((ns, hw, c2), hist_map)],
        out_specs=[pl.BlockSpec((tm, dff), lambda i: (i, 0)),
                   pl.BlockSpec((ns, hw, c2), hist_map)],
        out_shape=[jax.ShapeDtypeStruct((n, dff), BF16),
                   jax.ShapeDtypeStruct(hist.shape, F32)],
        scratch_shapes=[pltpu.VMEM((ns, SUBLANES + tt, c2), F32)],
        compiler_params=_params("arbitrary"),
        name="ffn_up_conv_gate",
    )(x2d, w_up, conv_w, hist)


def _sortable(x):
    b = lax.bitcast_convert_type(x, jnp.int32)
    return b ^ ((b >> 31) & jnp.int32(0x7FFFFFFF))


def _count(mask):
    return jnp.sum(jnp.where(mask, 1.0, 0.0), axis=-1, keepdims=True)


def _dsa_sample_kernel(qa_ref, qi_ref, qm_ref, kf_ref, vf_ref, kif_ref, kn_ref, vn_ref, kin_ref, nbias_ref,
                       fbias_ref, o_ref, self_ref, seln_ref, *, topk, tq, nbs, lf, ln, wi_col):
    start = lf - T5_FAR
    streams = range(nbs)
    rows = [slice(b * tq, (b + 1) * tq) for b in streams]
    dot = functools.partial(jnp.dot, preferred_element_type=F32)
    wi = [qm_ref[rows[b], wi_col:wi_col + N_IDX_HEADS] * (N_IDX_HEADS ** -0.5) * (D_IDX ** -0.5) for b in streams]
    kif = [kif_ref[0, b].astype(BF16) for b in streams]
    kinb = [kin_ref[b].astype(BF16) for b in streams]

    sc_f = [jnp.zeros((tq, lf), F32) for _ in streams]
    sc_n = [jnp.zeros((tq, ln), F32) for _ in streams]
    for n in range(N_IDX_HEADS):
        qn = [qi_ref[rows[b], n * D_IDX:(n + 1) * D_IDX].astype(BF16) for b in streams]
        sc_f = [sc_f[b] + jnp.maximum(dot(qn[b], kif[b]), 0.0) * wi[b][:, n:n + 1] for b in streams]
        sc_n = [sc_n[b] + jnp.maximum(dot(qn[b], kinb[b]), 0.0) * wi[b][:, n:n + 1] for b in streams]
    adm_f = _iota((tq, lf), 1) < start
    adm_n = nbias_ref[0, 0] > 0.5 * NEG
    key_f = [jnp.where(adm_f, _sortable(x), jnp.int32(INT_MIN)) for x in sc_f]
    key_n = [jnp.where(adm_n, _sortable(x), jnp.int32(INT_MIN)) for x in sc_n]

    kf32 = float(topk)

    def body(i, t_us):
        bit = lax.shift_left(jnp.int32(1), 31 - i)
        cand_u = [t | bit for t in t_us]
        cand_s = [c ^ jnp.int32(INT_MIN) for c in cand_u]
        cnt = [_count(key_f[b] >= cand_s[b]) + _count(key_n[b] >= cand_s[b]) for b in streams]
        return tuple(jnp.where(cnt[b] >= kf32, cand_u[b], t_us[b]) for b in streams)

    t_us = lax.fori_loop(0, 32, body, tuple(jnp.zeros((tq, 1), jnp.int32) for _ in streams))
    thr = [t ^ jnp.int32(INT_MIN) for t in t_us]
    need = [kf32 - (_count(key_f[b] > thr[b]) + _count(key_n[b] > thr[b])) for b in streams]
    n_eq = [_count(key_f[b] == thr[b]) + _count(key_n[b] == thr[b]) for b in streams]
    open_row = [t == jnp.int32(INT_MIN) for t in thr]
    for b in streams:
        self_ref[b] = jnp.where(jnp.logical_and(key_f[b] >= thr[b], adm_f), 0.0, NEG)
        seln_ref[b] = jnp.where(jnp.logical_and(key_n[b] >= thr[b], adm_n), 0.0, NEG)

    def resolve_ties(b):
        upper = jnp.where(_iota((LANES, LANES), 0) < _iota((LANES, LANES), 1), 1.0, 0.0).astype(BF16)
        offset = jnp.zeros((tq, 1), F32)
        for ref, key, width in ((self_ref, key_f[b], lf), (seln_ref, key_n[b], ln)):
            for j0 in range(0, width, LANES):
                w = min(LANES, width - j0)
                kb = key[:, j0:j0 + w]
                e = jnp.where(kb == thr[b], 1.0, 0.0)
                rank = offset + dot(e.astype(BF16), upper[:w, :w])
                take = jnp.where(kb > thr[b], 1.0, jnp.where(rank < need[b], e, 0.0))
                take = jnp.where(open_row[b], jnp.where(kb > thr[b], 1.0, 0.0), take)
                ref[b, :, j0:j0 + w] = jnp.where(take > 0.5, 0.0, NEG)
                offset = offset + jnp.sum(e, axis=-1, keepdims=True)

    for b in streams:
        conflict = jnp.logical_and(n_eq[b] != need[b], jnp.logical_not(open_row[b]))
        pl.when(jnp.max(jnp.where(conflict, 1.0, 0.0)) > 0.0)(functools.partial(resolve_ties, b))

    sel_f = [self_ref[b] for b in streams]
    sel_n = [seln_ref[b] for b in streams]
    for h in range(N_HEADS):
        hs = slice(h * HEAD_DIM, (h + 1) * HEAD_DIM)
        qh = [qa_ref[rows[b], hs].astype(BF16) for b in streams]
        s_f = [dot(qh[b], kf_ref[0, b, h].astype(BF16)) * (HEAD_DIM ** -0.5) + fbias_ref[:, h:h + 1] + sel_f[b]
               for b in streams]
        s_n = [dot(qh[b], kn_ref[b, h].astype(BF16)) * (HEAD_DIM ** -0.5) + nbias_ref[0, h] + sel_n[b]
               for b in streams]
        m = [jnp.maximum(jnp.max(x, axis=-1, keepdims=True), jnp.max(y, axis=-1, keepdims=True))
             for x, y in zip(s_f, s_n)]
        p_f = [jnp.exp(x - mm) for x, mm in zip(s_f, m)]
        p_n = [jnp.exp(x - mm) for x, mm in zip(s_n, m)]
        den = [jnp.sum(x, axis=-1, keepdims=True) + jnp.sum(y, axis=-1, keepdims=True) for x, y in zip(p_f, p_n)]
        o = [_dot_nt(p_f[b].astype(BF16), vf_ref[0, b, h].astype(BF16))
             + _dot_nt(p_n[b].astype(BF16), vn_ref[b, h].astype(BF16)) for b in streams]
        for b in streams:
            o_ref[rows[b], hs] = o[b] / den[b]


def _dsa_sample(qa, qi, qmisc, kf, vf, kif, layer, near, nbias, fbias, *, nb, tq, topk, wi_col):
    lf = kf.shape[-1]
    ln = nbias.shape[-1]
    nbs = SAMPLE_STREAMS_PER_STEP if nb % SAMPLE_STREAMS_PER_STEP == 0 else 1
    row = lambda b: (b, 0)
    args = [qa, qi, qmisc, kf, vf, kif, *near]
    in_specs = ([pl.BlockSpec((nbs * tq, a.shape[1]), row) for a in args[:3]]
                + [pl.BlockSpec((1, nbs) + a.shape[2:], lambda b, nd=a.ndim: (layer, b) + (0,) * (nd - 2))
                   for a in args[3:6]]
                + [pl.BlockSpec((nbs,) + a.shape[1:], lambda b, nd=a.ndim: (b,) + (0,) * (nd - 1)) for a in args[6:]]
                + [pl.BlockSpec(nbias.shape, lambda b: (0, 0, 0, 0)),
                   pl.BlockSpec(fbias.shape, lambda b: (0, 0))])
    return pl.pallas_call(
        functools.partial(_dsa_sample_kernel, topk=topk, tq=tq, nbs=nbs, lf=lf, ln=ln, wi_col=wi_col),
        grid=(nb // nbs,),
        in_specs=in_specs,
        out_specs=pl.BlockSpec((nbs * tq, qa.shape[1]), row),
        out_shape=jax.ShapeDtypeStruct(qa.shape, F32),
        scratch_shapes=[pltpu.VMEM((nbs, tq, lf), F32), pltpu.VMEM((nbs, tq, ln), F32)],
        compiler_params=_params("parallel"),
        name="dsa_attention_sample",
    )(*args, nbias, fbias)


def _reduce_rows(x, op, final):
    blk = 8 * SUBLANES
    parts = [x[r0:r0 + blk] for r0 in range(0, x.shape[0], blk)]
    while len(parts) > 1:
        parts = [op(parts[i], parts[i + 1]) for i in range(0, len(parts) - 1, 2)] + parts[len(parts) & ~1:]
    return final(parts[0], axis=0, keepdims=True)


def _block_rows(x, op):
    blk = 8 * SUBLANES
    parts = [x[r0:r0 + blk] for r0 in range(0, x.shape[0], blk)]
    while len(parts) > 1:
        parts = [op(parts[i], parts[i + 1]) for i in range(0, len(parts) - 1, 2)] + parts[len(parts) & ~1:]
    return parts[0]


def _dsa_blocks_kernel(qa_ref, qi_ref, qm_ref, k_ref, v_ref, ki_ref, corr_ref, o_ref,
                       kb_ref, vt_ref, kib_ref, key_ref, sel_ref, thr_ref, stat_ref, m_ref, l_ref, ot_ref, *,
                       topk, tq, seq, wi_col):
    g = pl.program_id(1)
    kb = 2 * LANES
    w = qa_ref.shape[1]
    blk = _row_tile(seq, 4 * LANES)

    @pl.when(g == 0)
    def _():
        for r0 in range(0, seq, blk):
            kb_ref[r0:r0 + blk, :] = k_ref[0, r0:r0 + blk, :].astype(BF16)
            kib_ref[r0:r0 + blk, :] = ki_ref[0, r0:r0 + blk, :].astype(BF16)
        for j in range(seq // LANES):
            vt_ref[j] = v_ref[0, j * LANES:(j + 1) * LANES, :].T.astype(BF16)

    top = (g + 1) * tq
    nkb = (top + kb - 1) // kb
    start_of = lambda i: pl.multiple_of(jnp.maximum(top - kb * (i + 1), 0), LANES)

    qa = (qa_ref[...] * (HEAD_DIM ** -0.5)).astype(BF16)
    qi = qi_ref[...].astype(BF16)
    wi = qm_ref[:, wi_col:wi_col + N_IDX_HEADS] * (N_IDX_HEADS ** -0.5) * (D_IDX ** -0.5)
    eye_h = jnp.where(_iota((N_IDX_HEADS, N_IDX_HEADS), 0) == _iota((N_IDX_HEADS, N_IDX_HEADS), 1), 1.0, 0.0)
    wi_t = _dot_nt(eye_h, wi, HI)
    q_chunk = (g * tq + _iota((1, tq), 1)) // CHUNK

    def score_block(i, carry):
        st = start_of(i)
        kib = kib_ref[pl.ds(st, kb), :][:, :D_IDX]
        acc = jnp.zeros((kb, tq), F32)
        for n in range(N_IDX_HEADS):
            acc = acc + jnp.maximum(_dot_nt(kib, qi[:, n * D_IDX:(n + 1) * D_IDX]), 0.0) * wi_t[n:n + 1, :]
        row = st + _iota((kb, 1), 0)
        ok = jnp.logical_and(row // CHUNK <= q_chunk, row < top - kb * i)
        key_ref[i] = jnp.where(ok, _sortable(acc), jnp.int32(INT_MIN))
        return carry

    lax.fori_loop(0, nkb, score_block, 0)

    kf32 = float(topk)

    def search_blocks(n_blocks):
        def count(pred):
            parts = [_block_rows(jnp.where(pred(key_ref[i]), 1.0, 0.0), jnp.add) for i in range(n_blocks)]
            while len(parts) > 1:
                parts = [a + b for a, b in zip(parts[0::2], parts[1::2])] + parts[len(parts) & ~1:]
            return jnp.sum(parts[0], axis=0, keepdims=True)

        def search(b, t_u):
            cand_u = t_u | lax.shift_left(jnp.int32(1), 31 - b)
            cand_s = cand_u ^ jnp.int32(INT_MIN)
            return jnp.where(count(lambda x: x >= cand_s) >= kf32, cand_u, t_u)

        t_u = lax.fori_loop(0, 32, search, jnp.zeros((1, tq), jnp.int32))
        thr = t_u ^ jnp.int32(INT_MIN)
        thr_ref[0:1, :] = thr
        stat_ref[0:1, :] = kf32 - count(lambda x: x > thr)
        stat_ref[1:2, :] = count(lambda x: x == thr)

    for n_blocks in range(1, seq // kb + 1):
        pl.when(nkb == n_blocks)(functools.partial(search_blocks, n_blocks))

    thr = thr_ref[0:1, :]
    need = stat_ref[0:1, :]
    open_row = thr == jnp.int32(INT_MIN)
    conflict = jnp.logical_and(stat_ref[1:2, :] != need, jnp.logical_not(open_row))
    floor = jnp.maximum(thr, jnp.int32(INT_MIN + 1))

    def select_block(i, carry):
        sel_ref[i] = jnp.where(key_ref[i] >= floor, 0.0, NEG)
        return carry

    lax.fori_loop(0, nkb, select_block, 0)

    @pl.when(jnp.max(jnp.where(conflict, 1.0, 0.0)) > 0.0)
    def _():
        below = jnp.where(_iota((LANES, LANES), 1) < _iota((LANES, LANES), 0), 1.0, 0.0).astype(BF16)

        def tie_block(ii, offset):
            i = nkb - 1 - ii
            for r0 in range(0, kb, LANES):
                kblk = key_ref[i, r0:r0 + LANES, :]
                e = jnp.where(kblk == thr, 1.0, 0.0)
                rank = offset + jnp.dot(below, e.astype(BF16), preferred_element_type=F32)
                take = jnp.where(kblk > thr, 1.0, jnp.where(jnp.logical_or(rank >= need, open_row), 0.0, e))
                sel_ref[i, r0:r0 + LANES, :] = jnp.where(take > 0.5, 0.0, NEG)
                offset = offset + jnp.sum(e, axis=0, keepdims=True)
            return offset

        lax.fori_loop(0, nkb, tie_block, jnp.zeros((1, tq), F32))

    heads = range(N_HEADS)
    hs = [slice(h * HEAD_DIM, (h + 1) * HEAD_DIM) for h in heads]

    def attend_block(i, first, near=None):
        st = start_of(i)
        vblk = st // LANES
        sel = sel_ref[i]
        s = [_dot_nt(kb_ref[pl.ds(st, kb), hs[h]], qa[:, hs[h]]) + sel for h in heads]
        if near is not None:
            s = [s[h] + corr_ref[near, h] for h in heads]
        m_blk = [jnp.max(_block_rows(x, jnp.maximum), axis=0, keepdims=True) for x in s]
        m_old = [m_ref[h:h + 1, :] for h in heads]
        m_new = m_blk if first else [jnp.maximum(a, b) for a, b in zip(m_old, m_blk)]
        p = [jnp.exp(x - m) for x, m in zip(s, m_new)]
        l_blk = [jnp.sum(_block_rows(x, jnp.add), axis=0, keepdims=True) for x in p]
        pb = [x.astype(BF16) for x in p]
        pv = [sum(jnp.dot(vt_ref[vblk + j, hs[h], :], pb[h][j * LANES:(j + 1) * LANES, :],
                          preferred_element_type=F32) for j in range(kb // LANES)) for h in heads]
        for h in heads:
            if first:
                l_ref[h:h + 1, :] = l_blk[h]
                ot_ref[hs[h], :] = pv[h]
            else:
                alpha = jnp.exp(m_old[h] - m_new[h])
                l_ref[h:h + 1, :] = l_ref[h:h + 1, :] * alpha + l_blk[h]
                ot_ref[hs[h], :] = ot_ref[hs[h], :] * alpha + pv[h]
            m_ref[h:h + 1, :] = m_new[h]

    attend_block(0, True, 0)
    pl.when(nkb > 1)(lambda: attend_block(1, False, 1))

    def attend_rest(i, carry):
        attend_block(i, False)
        return carry

    lax.fori_loop(2, nkb, attend_rest, 0)
    for h in heads:
        ot_ref[hs[h], :] = ot_ref[hs[h], :] * (1.0 / l_ref[h:h + 1, :])
    o_ref[...] = ot_ref[...].T


def _dsa_blocks(qa, qi, qmisc, k, v, ki, corr, *, nb, tq, topk, wi_col):
    n, w = qa.shape
    seq = k.shape[1]
    nq = seq // tq
    kb = 2 * LANES
    row = lambda b, g: (b * nq + g, 0)
    per_b = lambda b, g: (b, 0, 0)
    return pl.pallas_call(
        functools.partial(_dsa_blocks_kernel, topk=topk, tq=tq, seq=seq, wi_col=wi_col),
        grid=(nb, nq),
        in_specs=[pl.BlockSpec((tq, w), row),
                  pl.BlockSpec((tq, qi.shape[1]), row),
                  pl.BlockSpec((tq, qmisc.shape[1]), row),
                  pl.BlockSpec((1, seq, w), per_b),
                  pl.BlockSpec((1, seq, w), per_b),
                  pl.BlockSpec((1, seq, ki.shape[2]), per_b),
                  pl.BlockSpec(corr.shape, lambda b, g: (0, 0, 0, 0))],
        out_specs=pl.BlockSpec((tq, w), row),
        out_shape=jax.ShapeDtypeStruct(qa.shape, F32),
        scratch_shapes=[pltpu.VMEM((seq, w), BF16),
                        pltpu.VMEM((seq // LANES, w, LANES), BF16),
                        pltpu.VMEM((seq, ki.shape[2]), BF16),
                        pltpu.VMEM((seq // kb, kb, tq), jnp.int32),
                        pltpu.VMEM((seq // kb, kb, tq), F32),
                        pltpu.VMEM((SUBLANES, tq), jnp.int32),
                        pltpu.VMEM((SUBLANES, tq), F32),
                        pltpu.VMEM((N_HEADS, tq), F32),
                        pltpu.VMEM((N_HEADS, tq), F32),
                        pltpu.VMEM((w, tq), F32)],
        compiler_params=_params("parallel", "arbitrary"),
        name="dsa_attention_prompt",
    )(qa, qi, qmisc, k, v, ki, corr)


def _unit_lower_inverse(mats, n):
    row, col = _iota((n, n), 0), _iota((n, n), 1)
    same = lambda width: row // width == col // width
    dot = functools.partial(jnp.dot, preferred_element_type=F32)
    eye = jnp.where(row == col, 1.0, 0.0)
    diag = [jnp.where(same(INV_BLOCK), a, 0.0) for a in mats]
    ts = [eye - d for d in diag]
    dks = [_split(d) for d in diag]
    k = 1
    while 2 * k < INV_BLOCK:
        dks = [_split(_dot3(dk, dk)) for dk in dks]
        ts = [t + _dot3(_split(t), dk) for t, dk in zip(ts, dks)]
        k *= 2
    width = INV_BLOCK
    while width < n:
        off = jnp.logical_and(same(2 * width), jnp.logical_not(same(width)))
        ls = [jnp.where(off, a, 0.0).astype(BF16) for a in mats]
        tb = [t.astype(BF16) for t in ts]
        tl = [dot(t, l).astype(BF16) for t, l in zip(tb, ls)]
        ts = [t - dot(x, y) for t, x, y in zip(ts, tl, tb)]
        width *= 2
    return ts


def _head_rms(x, w):
    return x * lax.rsqrt(jnp.mean(x * x, axis=-1, keepdims=True) + 1e-6) * w


def _gdn_multi_kernel(qkv_ref, misc_ref, z_ref, hist_ref, s0_ref, cw_ref, alog_ref, dtb_ref, nw_ref,
                      y_ref, sfin_ref, newhist_ref, ext_ref, s_ref, *, c, a_col, b_col):
    j = pl.program_id(1)
    hw = CONV_B - 1
    base = SUBLANES
    wq = N_HEADS * DK_B
    seqs = range(qkv_ref.shape[0])

    @pl.when(j == 0)
    def _():
        ext_ref[:, base - hw:base, :] = hist_ref[...]
        s_ref[...] = s0_ref[...]

    @pl.when(j > 0)
    def _():
        ext_ref[:, base - hw:base, :] = ext_ref[:, base + c - hw:base + c, :]

    ext_ref[:, base:base + c, :] = qkv_ref[...]
    newhist_ref[...] = ext_ref[:, base + c - hw:base + c, :]
    ri = _iota((c, c), 0)
    ci = _iota((c, c), 1)
    lower = ri >= ci
    strict = ri > ci
    tri = jnp.where(lower, 1.0, 0.0)
    eye_h = jnp.where(_iota((N_HEADS, N_HEADS), 0) == _iota((N_HEADS, N_HEADS), 1), 1.0, 0.0)

    def conv_act(b):
        conv = None
        for k in range(CONV_B):
            term = ext_ref[b, base - hw + k:base - hw + k + c, :] * cw_ref[k:k + 1, :]
            conv = term if conv is None else conv + term
        return conv * _sigmoid(conv)

    act = [conv_act(b) for b in seqs]
    beta = [_sigmoid(misc_ref[b, :, b_col:b_col + N_HEADS]) for b in seqs]
    g = [-jnp.exp(alog_ref[...]) * _softplus(misc_ref[b, :, a_col:a_col + N_HEADS] + dtb_ref[...]) for b in seqs]
    gc = [_dot_hi(tri, x) for x in g]
    gc_t = [_dot_nt(eye_h, x, HI) for x in gc]
    eg = [jnp.exp(x) for x in gc]
    e_last = [jnp.exp(x[c - 1:c, :]) for x in gc]
    e_rest = [jnp.exp(x[c - 1:c, :] - x) for x in gc]

    items = [(b, h) for b in seqs for h in range(N_HEADS)]
    col = lambda x, h: x[:, h:h + 1]
    qs = [act[b][:, h * DK_B:(h + 1) * DK_B] for b, h in items]
    ks = [act[b][:, wq + h * DK_B:wq + (h + 1) * DK_B] for b, h in items]
    vs = [act[b][:, 2 * wq + h * DV_B:2 * wq + (h + 1) * DV_B] for b, h in items]
    qs = [q * lax.rsqrt(jnp.sum(q * q, axis=-1, keepdims=True) + 1e-6) * (DK_B ** -0.5) for q in qs]
    ks = [k * lax.rsqrt(jnp.sum(k * k, axis=-1, keepdims=True) + 1e-6) for k in ks]
    kbs = [k.astype(BF16) for k in ks]
    decay = [jnp.where(lower, jnp.exp(jnp.where(lower, col(gc[b], h) - gc_t[b][h:h + 1, :], 0.0)), 0.0)
             for b, h in items]
    kk = [_dot_nt(k, k) for k in kbs]
    attn = [_dot_nt(q.astype(BF16), k) * d for q, k, d in zip(qs, kbs, decay)]
    a_mat = [jnp.where(strict, col(beta[b], h) * x * d, 0.0) for (b, h), x, d in zip(items, kk, decay)]
    t_mat = [t.astype(BF16) for t in _unit_lower_inverse(a_mat, c)]
    value = [_dot(t, v * col(beta[b], h)) for (b, h), t, v in zip(items, t_mat, vs)]
    k_cum = [_dot(t, k * (col(beta[b], h) * col(eg[b], h))) for (b, h), t, k in zip(items, t_mat, ks)]
    s_old = [s_ref[b, h] for b, h in items]
    sbs = [s.astype(BF16) for s in s_old]
    v_new = [(v - _dot(kc, s)).astype(BF16) for v, kc, s in zip(value, k_cum, sbs)]
    o_inter = [_dot(q * col(eg[b], h), s) for (b, h), q, s in zip(items, qs, sbs)]
    o = [oi + _dot(a, v) for oi, a, v in zip(o_inter, attn, v_new)]
    for i, (b, h) in enumerate(items):
        s_ref[b, h] = s_old[i] * col(e_last[b], h) + _dot_tn((ks[i] * col(e_rest[b], h)).astype(BF16), v_new[i])
    for i, (b, h) in enumerate(items):
        zh = z_ref[b, :, h * DV_B:(h + 1) * DV_B]
        y_ref[b, :, h * DV_B:(h + 1) * DV_B] = _head_rms(o[i], nw_ref[...]) * (zh * _sigmoid(zh))
    sfin_ref[...] = s_ref[...]


def _gdn(qkv, misc, z, hist, s0, conv_w, a_log, dt_bias, norm_w, *, nb, c, a_col, b_col):
    n, wqkv = qkv.shape
    seq = n // nb
    nbb = GDN_SEQS_PER_STEP if nb % GDN_SEQS_PER_STEP == 0 else 1
    tok = lambda b, j: (b, j, 0)
    per_b3 = lambda b, j: (b, 0, 0)
    per_b4 = lambda b, j: (b, 0, 0, 0)
    const2 = lambda b, j: (0, 0)
    hw = CONV_B - 1
    tokens = [a.reshape(nb, seq, a.shape[1]) for a in (qkv, misc, z)]
    y, s_fin, new_hist = pl.pallas_call(
        functools.partial(_gdn_multi_kernel, c=c, a_col=a_col, b_col=b_col),
        grid=(nb // nbb, seq // c),
        in_specs=[pl.BlockSpec((nbb, c, a.shape[2]), tok) for a in tokens]
                 + [pl.BlockSpec((nbb, hw, wqkv), per_b3),
                    pl.BlockSpec((nbb,) + s0.shape[1:], per_b4),
                    pl.BlockSpec((CONV_B, wqkv), const2),
                    pl.BlockSpec((1, N_HEADS), const2),
                    pl.BlockSpec((1, N_HEADS), const2),
                    pl.BlockSpec((1, DV_B), const2)],
        out_specs=[pl.BlockSpec((nbb, c, z.shape[1]), tok),
                   pl.BlockSpec((nbb,) + s0.shape[1:], per_b4),
                   pl.BlockSpec((nbb, hw, wqkv), per_b3)],
        out_shape=[jax.ShapeDtypeStruct((nb, seq, z.shape[1]), F32),
                   jax.ShapeDtypeStruct(s0.shape, F32),
                   jax.ShapeDtypeStruct(hist.shape, F32)],
        scratch_shapes=[pltpu.VMEM((nbb, SUBLANES + c, wqkv), F32),
                        pltpu.VMEM((nbb,) + s0.shape[1:], F32)],
        compiler_params=_params("parallel", "arbitrary"),
        name="gated_deltanet",
    )(*tokens, hist, s0, conv_w, a_log.reshape(1, -1), dt_bias.reshape(1, -1), norm_w.reshape(1, -1))
    return y.reshape(n, z.shape[1]), s_fin, new_hist


def _block_mask(rows, row_group, cols, col_group):
    return (np.arange(rows)[:, None] // row_group == np.arange(cols)[None, :] // col_group).astype(np.float32)


def _cummax_rows(x):
    rows = x.shape[0]
    row = _iota(x.shape, 0)
    sh = 1
    while sh < rows:
        x = jnp.maximum(x, jnp.where(row >= sh, pltpu.roll(x, sh, axis=0), -jnp.inf))
        sh *= 2
    return x


def _dot2(a, b):
    hi, lo = _split(a)
    return jnp.dot(hi, b, preferred_element_type=F32) + jnp.dot(lo, b, preferred_element_type=F32)


def _mlstm_dense_kernel(qk_ref, v_ref, og_ref, misc_ref, c0_ref, n0_ref, m0_ref, ib_ref, fb_ref, nw_ref,
                        el_ref, ev_ref, ek_ref, kmask_ref, vmask_ref, cmask_ref, cmaskb_ref, rms_ref, causal_ref,
                        dsel_ref,
                        y_ref, cfin_ref, nfin_ref, mfin_ref, c_ref, n_ref, m_ref, *, l, i_col, f_col):
    j = pl.program_id(1)
    wq = N_HEADS * DK_C
    nbb = qk_ref.shape[0]
    seqs = range(nbb)

    @pl.when(j == 0)
    def _():
        c_ref[...] = jnp.zeros(c_ref.shape, F32)
        for b in seqs:
            for h in range(N_HEADS):
                c_ref[b, h * DK_C:(h + 1) * DK_C, h * DV_C:(h + 1) * DV_C] = c0_ref[b, h]
        n_ref[...] = n0_ref[...]
        m_ref[...] = m0_ref[...]

    each = lambda f, *xs: [f(*args) for args in zip(*xs)]
    dot = functools.partial(jnp.dot, preferred_element_type=F32)
    spread = lambda x, e_ref: sum(dot(p, e_ref[...]) for p in _split3(x))
    tri = jnp.where(_iota((l, l), 0) >= _iota((l, l), 1), 1.0, 0.0).astype(BF16)
    ig = [misc_ref[b, :, i_col:i_col + N_HEADS] + ib_ref[...] for b in seqs]
    lf = [-_softplus(-(misc_ref[b, :, f_col:f_col + N_HEADS] + fb_ref[...])) for b in seqs]
    fc = each(lambda x: sum(dot(tri, p) for p in _split3(x)), lf)
    m_prev = [m_ref[b] for b in seqs]
    log_inter = each(jnp.add, fc, m_prev)
    a = each(jnp.subtract, ig, fc)
    m_t = each(lambda li, f, x: jnp.maximum(li, f + _cummax_rows(x)), log_inter, fc, a)
    w_inter = each(lambda li, m: jnp.exp(li - m), log_inter, m_t)
    w_last = each(lambda f, i, m: jnp.exp(f[l - 1:l, :] - f + i - m[l - 1:l, :]), fc, ig, m_t)

    key_terms = each(lambda x: jnp.sum(spread(x, el_ref) * dsel_ref[...], axis=0, keepdims=True), a)
    log_w = each(lambda f, kt: jnp.where(causal_ref[...] > 0.5, spread(f, el_ref) + kt, -jnp.inf), fc, key_terms)
    q = [qk_ref[b, :, :wq] for b in seqs]
    q_b = [x.astype(BF16) for x in q]
    k_s = [qk_ref[b, :, wq:] * (DK_C ** -0.5) for b in seqs]
    k_b = [x.astype(BF16) for x in k_s]
    v_b = [v_ref[b].astype(BF16) for b in seqs]
    kt_bd = [jnp.concatenate([x] * N_HEADS, axis=0) * kmask_ref[...] for x in k_b]
    v_bd = [jnp.concatenate([x] * N_HEADS, axis=0) * vmask_ref[...] for x in v_b]
    m_l = each(lambda m: spread(m, el_ref), m_t)
    qkw = each(lambda x, kt, lw, m: (_dot_nt(x, kt) * jnp.exp(lw - m)).astype(BF16), q_b, kt_bd, log_w, m_l)
    c_old = [c_ref[b] for b in seqs]
    n_old = [n_ref[b] for b in seqs]
    wi_s = each(lambda w: spread(w, ev_ref), w_inter)
    q_c = each(lambda x, c: dot(x, c.astype(BF16)), q_b, c_old)
    qn = each(lambda x, n: _dot2(x * n, cmaskb_ref[...]), q, n_old)
    num = each(lambda w, vb, wi, qc: dot(w, vb) + wi * qc, qkw, v_bd, wi_s, q_c)
    den = each(lambda w, wi, x: dot(w, vmask_ref[...]) + wi * x, qkw, wi_s, qn)
    m_v = m_l if l == DV_C else each(lambda m: spread(m, ev_ref), m_t)
    hid = each(lambda nu, de, m: nu / jnp.maximum(jnp.abs(de), jnp.exp(-m)), num, den, m_v)
    ms = each(lambda x: _dot2(x * x, rms_ref[...]), hid)
    for b in seqs:
        y_ref[b] = hid[b] * lax.rsqrt(ms[b] + 1e-6) * nw_ref[...] * _sigmoid(og_ref[b])

    kw = each(lambda k, w: k * spread(w, ek_ref), k_s, w_last)
    dec_rows = [jnp.broadcast_to(w[l - 1:l, :], (SUBLANES, N_HEADS)) for w in w_inter]
    upd = each(lambda x, vb: _dot_tn(x.astype(BF16), vb), kw, v_b)
    for b in seqs:
        c_ref[b] = c_old[b] * spread(dec_rows[b], ev_ref)[0:1, :] + upd[b] * cmask_ref[...]
        n_ref[b] = n_old[b] * spread(dec_rows[b], ek_ref)[0:1, :] + jnp.sum(kw[b], axis=0, keepdims=True)
        m_ref[b] = m_t[b][l - 1:l, :]

    @pl.when(j == pl.num_programs(1) - 1)
    def _():
        for b in seqs:
            for h in range(N_HEADS):
                cfin_ref[b, h] = c_ref[b, h * DK_C:(h + 1) * DK_C, h * DV_C:(h + 1) * DV_C]
        nfin_ref[...] = n_ref[...]
        mfin_ref[...] = m_ref[...]


def _mlstm_dense(qk, v, og, misc, c0, n0, m0, i_bias, f_bias, norm_w, *, nb, l, i_col, f_col):
    n = qk.shape[0]
    seq = n // nb
    wq, wv = N_HEADS * DK_C, N_HEADS * DV_C
    hl = N_HEADS * l
    nbb = SEQS_PER_STEP if nb % SEQS_PER_STEP == 0 else 1
    tok = lambda b, j: (b, j, 0)
    per_b3 = lambda b, j: (b, 0, 0)
    per_b4 = lambda b, j: (b, 0, 0, 0)
    const2 = lambda b, j: (0, 0)
    consts = [jnp.asarray(_block_mask(N_HEADS, 1, hl, l), BF16),
              jnp.asarray(_block_mask(N_HEADS, 1, wv, DV_C), BF16),
              jnp.asarray(_block_mask(N_HEADS, 1, wq, DK_C), BF16),
              jnp.asarray(_block_mask(hl, l, wq, DK_C), BF16),
              jnp.asarray(_block_mask(hl, l, wv, DV_C), BF16),
              jnp.asarray(_block_mask(wq, DK_C, wv, DV_C)),
              jnp.asarray(_block_mask(wq, DK_C, wv, DV_C), BF16),
              jnp.asarray(_block_mask(wv, DV_C, wv, DV_C) / DV_C, BF16),
              jnp.asarray(np.tile(np.tril(np.ones((l, l), np.float32)), (1, N_HEADS))),
              jnp.asarray(np.tile(np.eye(l, dtype=np.float32), (1, N_HEADS)))]
    tokens = [a.reshape(nb, seq, a.shape[1]) for a in (qk, v, og, misc)]
    outs = pl.pallas_call(
        functools.partial(_mlstm_dense_kernel, l=l, i_col=i_col, f_col=f_col),
        grid=(nb // nbb, seq // l),
        in_specs=[pl.BlockSpec((nbb, l, a.shape[2]), tok) for a in tokens]
                 + [pl.BlockSpec((nbb,) + c0.shape[1:], per_b4),
                    pl.BlockSpec((nbb, 1, wq), per_b3),
                    pl.BlockSpec((nbb, 1, N_HEADS), per_b3),
                    pl.BlockSpec((1, N_HEADS), const2),
                    pl.BlockSpec((1, N_HEADS), const2),
                    pl.BlockSpec((1, wv), const2)]
                 + [pl.BlockSpec(c.shape, const2) for c in consts],
        out_specs=[pl.BlockSpec((nbb, l, wv), tok),
                   pl.BlockSpec((nbb,) + c0.shape[1:], per_b4),
                   pl.BlockSpec((nbb, 1, wq), per_b3),
                   pl.BlockSpec((nbb, 1, N_HEADS), per_b3)],
        out_shape=[jax.ShapeDtypeStruct((nb, seq, wv), F32),
                   jax.ShapeDtypeStruct(c0.shape, F32),
                   jax.ShapeDtypeStruct((nb, 1, wq), F32),
                   jax.ShapeDtypeStruct((nb, 1, N_HEADS), F32)],
        scratch_shapes=[pltpu.VMEM((nbb, wq, wv), F32),
                        pltpu.VMEM((nbb, 1, wq), F32),
                        pltpu.VMEM((nbb, 1, N_HEADS), F32)],
        compiler_params=_params("parallel", "arbitrary"),
        name="mlstm",
    )(*tokens, c0, n0.reshape(nb, 1, wq), m0.reshape(nb, 1, N_HEADS), i_bias.reshape(1, -1), f_bias.reshape(1, -1),
      jnp.tile(norm_w, N_HEADS).reshape(1, wv), *consts)
    y, c_fin, n_fin, m_fin = outs
    return y.reshape(n, wv), c_fin, n_fin.reshape(nb, N_HEADS, DK_C), m_fin.reshape(nb, N_HEADS)


def _band_prompt_kernel(q_ref, k_ref, v_ref, bias_ref, o_ref, kb_ref, vt_ref, ot_ref, *, tq, seq):
    g = pl.program_id(1)
    w = q_ref.shape[1]
    pad = BAND_CHUNKS * CHUNK
    lw = pad + tq
    npad = pad // LANES
    per_tile = tq // LANES
    blk = _row_tile(seq, 4 * LANES)

    @pl.when(g == 0)
    def _():
        kb_ref[0:pad, :] = jnp.zeros((pad, w), BF16)
        vt_ref[0:npad] = jnp.zeros((npad, w, LANES), BF16)
        for r0 in range(0, seq, blk):
            kb_ref[pad + r0:pad + r0 + blk, :] = k_ref[0, r0:r0 + blk, :].astype(BF16)
        for j in range(seq // LANES):
            vt_ref[npad + j] = v_ref[0, j * LANES:(j + 1) * LANES, :].T.astype(BF16)

    start = pl.multiple_of(g * tq, tq)
    q = (q_ref[...] * (HEAD_DIM ** -0.5)).astype(BF16)
    before_seq = jnp.where(_iota((lw, tq), 0) >= pad - g * tq, 0.0, NEG)
    group = 4
    for h0 in range(0, N_HEADS, group):
        heads = range(h0, h0 + group)
        hs = {h: slice(h * HEAD_DIM, (h + 1) * HEAD_DIM) for h in heads}
        s = [_dot_nt(kb_ref[pl.ds(start, lw), hs[h]], q[:, hs[h]]) + (bias_ref[h] + before_seq) for h in heads]
        p = [jnp.exp(x - _reduce_rows(x, jnp.maximum, jnp.max)) for x in s]
        inv = [1.0 / _reduce_rows(x, jnp.add, jnp.sum) for x in p]
        pb = [x.astype(BF16) for x in p]
        acc = [sum(jnp.dot(vt_ref[g * per_tile + j, hs[h], :], x[j * LANES:(j + 1) * LANES, :],
                           preferred_element_type=F32) for j in range(lw // LANES)) for h, x in zip(heads, pb)]
        for h, a, r in zip(heads, acc, inv):
            ot_ref[hs[h], :] = a * r
    o_ref[...] = ot_ref[...].T


def _band_prompt(q, k, v, bias, *, nb, tq):
    n, w = q.shape
    seq = k.shape[1]
    nq = seq // tq
    pad = BAND_CHUNKS * CHUNK
    row = lambda b, g: (b * nq + g, 0)
    per_b = lambda b, g: (b, 0, 0)
    return pl.pallas_call(
        functools.partial(_band_prompt_kernel, tq=tq, seq=seq),
        grid=(nb, nq),
        in_specs=[pl.BlockSpec((tq, w), row),
                  pl.BlockSpec((1, seq, w), per_b),
                  pl.BlockSpec((1, seq, w), per_b),
                  pl.BlockSpec(bias.shape, lambda b, g: (0, 0, 0))],
        out_specs=pl.BlockSpec((tq, w), row),
        out_shape=jax.ShapeDtypeStruct(q.shape, F32),
        scratch_shapes=[pltpu.VMEM((pad + seq, w), BF16),
                        pltpu.VMEM(((pad + seq) // LANES, w, LANES), BF16),
                        pltpu.VMEM((w, tq), F32)],
        compiler_params=_params("parallel", "arbitrary"),
        name="band_attention_prompt",
    )(q, k, v, bias)


def _band_sample_kernel(q_ref, kc_ref, vc_ref, kn_ref, vn_ref, biasc_ref, biasn_ref, o_ref, *, tq, nbs):
    streams = range(nbs)
    rows = [slice(b * tq, (b + 1) * tq) for b in streams]
    dot = functools.partial(jnp.dot, preferred_element_type=F32)
    for h in range(N_HEADS):
        hs = slice(h * HEAD_DIM, (h + 1) * HEAD_DIM)
        qh = [q_ref[rows[b], hs].astype(BF16) for b in streams]
        s_c = [dot(qh[b], kc_ref[0, b, h].astype(BF16)) * (HEAD_DIM ** -0.5) + biasc_ref[h] for b in streams]
        s_n = [dot(qh[b], kn_ref[b, h].astype(BF16)) * (HEAD_DIM ** -0.5) + biasn_ref[h] for b in streams]
        m = [jnp.maximum(jnp.max(x, axis=-1, keepdims=True), jnp.max(y, axis=-1, keepdims=True))
             for x, y in zip(s_c, s_n)]
        p_c = [jnp.exp(x - mm) for x, mm in zip(s_c, m)]
        p_n = [jnp.exp(x - mm) for x, mm in zip(s_n, m)]
        den = [jnp.sum(x, axis=-1, keepdims=True) + jnp.sum(y, axis=-1, keepdims=True) for x, y in zip(p_c, p_n)]
        o = [_dot_nt(p_c[b].astype(BF16), vc_ref[0, b, h].astype(BF16))
             + _dot_nt(p_n[b].astype(BF16), vn_ref[b, h].astype(BF16)) for b in streams]
        for b in streams:
            o_ref[rows[b], hs] = o[b] / den[b]


def _band_sample(q, kc, vc, layer, kn, vn, bias_c, bias_n, *, nb, tq):
    n, w = q.shape
    nbs = SAMPLE_STREAMS_PER_STEP if nb % SAMPLE_STREAMS_PER_STEP == 0 else 1
    row = lambda b: (b, 0)
    const3 = lambda b: (0, 0, 0)
    return pl.pallas_call(
        functools.partial(_band_sample_kernel, tq=tq, nbs=nbs),
        grid=(nb // nbs,),
        in_specs=[pl.BlockSpec((nbs * tq, w), row)]
                 + [pl.BlockSpec((1, nbs) + a.shape[2:], lambda b: (layer, b, 0, 0, 0)) for a in (kc, vc)]
                 + [pl.BlockSpec((nbs,) + a.shape[1:], lambda b: (b, 0, 0, 0)) for a in (kn, vn)]
                 + [pl.BlockSpec(bias_c.shape, const3), pl.BlockSpec(bias_n.shape, const3)],
        out_specs=pl.BlockSpec((nbs * tq, w), row),
        out_shape=jax.ShapeDtypeStruct(q.shape, F32),
        compiler_params=_params("parallel"),
        name="band_attention_sample",
    )(q, kc, vc, kn, vn, bias_c, bias_n)


def _t5_bucket(rel):
    nb = T5_BUCKETS // 2
    max_exact = nb // 2
    n = jnp.abs(rel)
    n_f = jnp.maximum(n, 1).astype(jnp.float32)
    large = max_exact + (jnp.log(n_f / max_exact) / math.log(T5_MAX_DIST / max_exact) * (nb - max_exact)).astype(jnp.int32)
    large = jnp.minimum(large, nb - 1)
    return jnp.where(rel > 0, nb, 0) + jnp.where(n < max_exact, n, large)


def _toeplitz_bias(fn, n_rows, n_cols):
    n = n_rows + n_cols
    m = np.arange(n)
    f = jnp.transpose(fn(np.where(m < n_cols, m, m - n))).astype(F32)
    flat = jnp.tile(f, (1, n_rows))[:, :n_rows * (n - 1)]
    return flat.reshape(f.shape[0], n_rows, n - 1)[:, :, :n_cols]


def _pack_cols(w, sizes, groups):
    offs = np.concatenate([[0], np.cumsum(sizes)])
    cols, widths = [], []
    for grp in groups:
        width = 0
        for idx in grp:
            cols.append(w[:, offs[idx]:offs[idx + 1]])
            width += sizes[idx]
        pad = (-width) % LANES
        if pad:
            cols.append(jnp.zeros((w.shape[0], pad), w.dtype))
        widths.append(width + pad)
    return jnp.concatenate(cols, axis=1).astype(BF16), tuple(widths)


def _row_tile(n, target):
    t = min(n, target)
    while n % t:
        t //= 2
    return t


def kernel(x_prompt, x_sample, cache_a_k, cache_a_v, cache_a_kidx, state_b_s, state_b_conv, state_c_c, state_c_n, state_c_m, cache_d_k, cache_d_v, state_ffn_conv, w_in_even, w_out_even, t5_bias, b_conv_w, b_a_log, b_dt_bias, b_norm_w, w_in_odd, w_out_odd, c_i_bias, c_f_bias, c_norm_w, d_rel_bias, ln_mix_g, ln_mix_b, ln_ffn_g, ln_ffn_b, ffn_w_up, ffn_conv_w, ffn_w_down):
    bp, sp, d = x_prompt.shape
    bs, ts, _ = x_sample.shape
    depth = ffn_w_up.shape[0]
    past = cache_a_k.shape[2]
    d_win = cache_d_k.shape[2]
    dff = ffn_w_down.shape[1]
    alpha = (2 * depth) ** 0.25
    w_a = N_HEADS * HEAD_DIM
    w_b = N_HEADS * DV_B
    w_c = N_HEADS * DV_C
    qkv_b_w = 2 * N_HEADS * DK_B + w_b
    even_sizes = (w_a, w_a, w_a, N_IDX_HEADS * D_IDX, D_IDX, N_IDX_HEADS, qkv_b_w, N_HEADS, N_HEADS, w_b)
    odd_sizes = (N_HEADS * DK_C, N_HEADS * DK_C, w_c, N_HEADS, N_HEADS, w_c, w_a, w_a, w_a)
    even_groups = ((0,), (1,), (2,), (3,), (6,), (9,), (4, 5, 7, 8))
    wi_col, a_col, b_col = D_IDX, D_IDX + N_IDX_HEADS, D_IDX + N_IDX_HEADS + N_HEADS
    odd_groups = ((0, 1), (2,), (5,), (6,), (7,), (8,), (3, 4))
    i_col, f_col = 0, N_HEADS

    assert sp % CHUNK == 0 and ts <= CHUNK and past % CHUNK == 0 and past >= T5_FAR
    assert (past + ts - 1) // CHUNK == past // CHUNK
    topk_p = min(TOPK_MAX, sp // 4)
    topk_s = min(TOPK_MAX, (past + ts) // 4)
    n_p, n_s = bp * sp, bs * ts
    tm_p = _row_tile(n_p, 512)
    tm_o = _row_tile(n_p, 1024)
    tm_s = _row_tile(n_s, 512)
    tff_p = _row_tile(sp, 512)
    ns_s = _row_tile(bs, max(1, 256 // ts))

    t5 = lambda rel: t5_bias[_t5_bucket(jnp.asarray(rel, jnp.int32))]
    fbias = t5(np.array([-T5_FAR - 1]))
    tq_dsa = 2 * LANES
    assert sp % tq_dsa == 0 and T5_FAR <= tq_dsa
    corr_p = jnp.stack([_toeplitz_bias(lambda dd: t5(-dd - tq_dsa * i) - fbias, tq_dsa, tq_dsa) for i in range(2)])
    tq_a = 2 * CHUNK
    ln_s = T5_FAR + ts
    nbias_s = _toeplitz_bias(lambda dd: t5(dd - T5_FAR), ts, ln_s)[None]

    lw = BAND_CHUNKS * CHUNK + tq_a
    r_chunk = np.arange(lw)[:, None] // CHUNK
    q_chunk = BAND_CHUNKS + np.arange(tq_a)[None, :] // CHUNK
    band_ok = (r_chunk >= q_chunk - BAND_CHUNKS) & (r_chunk <= q_chunk)
    pos_q = past + np.arange(ts)
    pos_kc = past - d_win + np.arange(d_win)
    def band_valid(pos_k):
        kch, qch = pos_k // CHUNK, pos_q // CHUNK
        return (pos_k[None] >= 0) & (kch[None] >= qch[:, None] - BAND_CHUNKS) & (kch[None] <= qch[:, None])
    def band_bias(table, shift, n_rows, n_cols, valid=None):
        bias = _toeplitz_bias(lambda dd: table[np.clip(dd + shift, -REL_CLIP, REL_CLIP) + REL_CLIP], n_rows, n_cols)
        return bias if valid is None else jnp.where(jnp.asarray(valid)[None], bias, NEG)

    cak_t = jnp.transpose(cache_a_k, (0, 1, 3, 4, 2))
    cav_t = jnp.transpose(cache_a_v, (0, 1, 3, 4, 2))
    caki_t = jnp.transpose(cache_a_kidx, (0, 1, 3, 2))
    cdk_t = jnp.transpose(cache_d_k, (0, 1, 3, 4, 2))
    cdv_t = jnp.transpose(cache_d_v, (0, 1, 3, 4, 2))

    xp = x_prompt.reshape(n_p, d)
    xs = x_sample.reshape(n_s, d)
    outs = {k: [] for k in ("ak_p", "ak_s", "av_p", "av_s", "aki_p", "aki_s", "bs_p", "bs_s", "bc_p", "bc_s",
                            "cc_p", "cc_s", "cn_p", "cn_s", "cm_p", "cm_s", "dk_p", "dk_s", "dv_p", "dv_s",
                            "fc_p", "fc_s")}
    for layer in range(depth):
        if layer % 2 == 0:
            e = layer // 2
            w_in, widths = _pack_cols(w_in_even[e], even_sizes, even_groups)
            w_out = w_out_even[e].astype(BF16)
            qa, ka, va, qi, qkv_b, z_b, misc = _proj(xp, w_in, widths, tm_p)
            o_a = _dsa_blocks(qa, qi, misc, ka.reshape(bp, sp, w_a), va.reshape(bp, sp, w_a),
                              misc.reshape(bp, sp, LANES), corr_p, nb=bp, tq=tq_dsa, topk=topk_p, wi_col=wi_col)
            y_b, s_b, h_b = _gdn(qkv_b, misc, z_b, jnp.zeros((bp, CONV_B - 1, qkv_b_w), F32),
                                 jnp.zeros((bp, N_HEADS, DK_B, DV_B), F32), b_conv_w[e], b_a_log[e], b_dt_bias[e],
                                 b_norm_w[e], nb=bp, c=CHUNK, a_col=a_col, b_col=b_col)
            xp = _mm_res_ln([o_a, y_b], [w_out[:w_a], w_out[w_a:]], xp, ln_mix_g[layer], ln_mix_b[layer], alpha, tm_o)
            outs["ak_p"].append(ka.reshape(bp, sp, N_HEADS, HEAD_DIM))
            outs["av_p"].append(va.reshape(bp, sp, N_HEADS, HEAD_DIM))
            outs["aki_p"].append(misc[:, :D_IDX].reshape(bp, sp, D_IDX))
            outs["bs_p"].append(s_b)
            outs["bc_p"].append(h_b)
            qa, ka, va, qi, qkv_b, z_b, misc = _proj(xs, w_in, widths, tm_s)
            ki = misc[:, :D_IDX]
            heads_t = lambda u: jnp.transpose(u.reshape(bs, ts, N_HEADS, HEAD_DIM), (0, 2, 3, 1))
            near = (jnp.concatenate([cak_t[e, ..., past - T5_FAR:], heads_t(ka)], axis=-1),
                    jnp.concatenate([cav_t[e, ..., past - T5_FAR:], heads_t(va)], axis=-1),
                    jnp.concatenate([caki_t[e, ..., past - T5_FAR:],
                                     jnp.transpose(ki.reshape(bs, ts, D_IDX), (0, 2, 1))], axis=-1))
            o_a = _dsa_sample(qa, qi, misc, cak_t, cav_t, caki_t, e, near, nbias_s, fbias, nb=bs, tq=ts,
                              topk=topk_s, wi_col=wi_col)
            y_b, s_b, h_b = _gdn(qkv_b, misc, z_b, state_b_conv[e], state_b_s[e], b_conv_w[e], b_a_log[e],
                                 b_dt_bias[e], b_norm_w[e], nb=bs, c=ts, a_col=a_col, b_col=b_col)
            xs = _mm_res_ln([o_a, y_b], [w_out[:w_a], w_out[w_a:]], xs, ln_mix_g[layer], ln_mix_b[layer], alpha, tm_s)
            outs["ak_s"].append(ka.reshape(bs, ts, N_HEADS, HEAD_DIM))
            outs["av_s"].append(va.reshape(bs, ts, N_HEADS, HEAD_DIM))
            outs["aki_s"].append(ki.reshape(bs, ts, D_IDX))
            outs["bs_s"].append(s_b)
            outs["bc_s"].append(h_b)
        else:
            o = layer // 2
            w_in, widths = _pack_cols(w_in_odd[o], odd_sizes, odd_groups)
            w_out = w_out_odd[o].astype(BF16)
            qk_c, v_c, o_c, q_d, k_d, v_d, misc = _proj(xp, w_in, widths, tm_p)
            y_c, c_c, c_n, c_m = _mlstm_dense(qk_c, v_c, o_c, misc,jnp.zeros((bp, N_HEADS, DK_C, DV_C), F32),
                                        jnp.zeros((bp, N_HEADS, DK_C), F32), jnp.zeros((bp, N_HEADS), F32),
                                        c_i_bias[o], c_f_bias[o], c_norm_w[o], nb=bp, l=CHUNK, i_col=i_col, f_col=f_col)
            k3 = k_d.reshape(bp, sp, w_a)
            v3 = v_d.reshape(bp, sp, w_a)
            bias_bp = _toeplitz_bias(
                lambda dd: d_rel_bias[o][np.clip(-dd - BAND_CHUNKS * CHUNK, -REL_CLIP, REL_CLIP) + REL_CLIP], lw, tq_a)
            bias_bp = jnp.where(jnp.asarray(band_ok)[None], bias_bp, NEG)
            o_d = _band_prompt(q_d, k3, v3, bias_bp, nb=bp, tq=tq_a)
            xp = _mm_res_ln([y_c, o_d], [w_out[:w_c], w_out[w_c:]], xp, ln_mix_g[layer], ln_mix_b[layer], alpha, tm_o)
            d_win_p = min(BAND_CHUNKS * CHUNK, sp)
            outs["cc_p"].append(c_c)
            outs["cn_p"].append(c_n)
            outs["cm_p"].append(c_m)
            outs["dk_p"].append(k3[:, sp - d_win_p:].reshape(bp, d_win_p, N_HEADS, HEAD_DIM))
            outs["dv_p"].append(v3[:, sp - d_win_p:].reshape(bp, d_win_p, N_HEADS, HEAD_DIM))
            qk_c, v_c, o_c, q_d, k_d, v_d, misc = _proj(xs, w_in, widths, tm_s)
            y_c, c_c, c_n, c_m = _mlstm_dense(qk_c, v_c, o_c, misc,state_c_c[o], state_c_n[o], state_c_m[o],
                                        c_i_bias[o], c_f_bias[o], c_norm_w[o], nb=bs, l=ts, i_col=i_col, f_col=f_col)
            heads_t = lambda u: jnp.transpose(u.reshape(bs, ts, N_HEADS, HEAD_DIM), (0, 2, 3, 1))
            o_d = _band_sample(q_d, cdk_t, cdv_t, o, heads_t(k_d), heads_t(v_d),
                               band_bias(d_rel_bias[o], -d_win, ts, d_win, band_valid(pos_kc)),
                               band_bias(d_rel_bias[o], 0, ts, ts, band_valid(pos_q)), nb=bs, tq=ts)
            xs = _mm_res_ln([y_c, o_d], [w_out[:w_c], w_out[w_c:]], xs, ln_mix_g[layer], ln_mix_b[layer], alpha, tm_s)
            outs["cc_s"].append(c_c)
            outs["cn_s"].append(c_n)
            outs["cm_s"].append(c_m)
            outs["dk_s"].append(k_d.reshape(bs, ts, N_HEADS, HEAD_DIM))
            outs["dv_s"].append(v_d.reshape(bs, ts, N_HEADS, HEAD_DIM))
        w_up = ffn_w_up[layer].astype(BF16)
        w_down = ffn_w_down[layer].astype(BF16)
        act, hist_p = _ffn_up(xp, w_up, ffn_conv_w[layer], jnp.zeros((bp, CONV_FF - 1, 2 * dff), F32),
                              1, tff_p, sp // tff_p)
        xp = _mm_res_ln([act], [w_down], xp, ln_ffn_g[layer], ln_ffn_b[layer], alpha, tm_o)
        act, hist_s = _ffn_up(xs, w_up, ffn_conv_w[layer], state_ffn_conv[layer], ns_s, ts, 1)
        xs = _mm_res_ln([act], [w_down], xs, ln_ffn_g[layer], ln_ffn_b[layer], alpha, tm_s)
        outs["fc_p"].append(hist_p)
        outs["fc_s"].append(hist_s)

    st = lambda k: jnp.stack(outs[k])
    return (xp.reshape(bp, sp, d), xs.reshape(bs, ts, d),
            st("ak_p"), st("ak_s"), st("av_p"), st("av_s"), st("aki_p"), st("aki_s"),
            st("bs_p"), st("bs_s"), st("bc_p"), st("bc_s"),
            st("cc_p"), st("cc_s"), st("cn_p"), st("cn_s"), st("cm_p"), st("cm_s"),
            st("dk_p"), st("dk_s"), st("dv_p"), st("dv_s"),
            st("fc_p"), st("fc_s"))
```

```python
import functools
import math

import numpy as np
import jax
import jax.numpy as jnp
from jax import lax
from jax.experimental import pallas as pl
from jax.experimental.pallas import tpu as pltpu

F32 = jnp.float32
BF16 = jnp.bfloat16
HI = lax.Precision.HIGHEST

CHUNK = 64
HEAD_DIM = 64
N_HEADS = 8
N_IDX_HEADS = 8
D_IDX = 64
TOPK_MAX = 256
T5_BUCKETS = 32
T5_MAX_DIST = 128
DK_B = 64
DV_B = 64
CONV_B = 4
DK_C = 32
DV_C = 64
BAND_CHUNKS = 8
REL_CLIP = 128
CONV_FF = 3
T5_FAR = 128
INV_BLOCK = 8
SAMPLE_STREAMS_PER_STEP = 4
GDN_SEQS_PER_STEP = 2
SEQS_PER_STEP = 8

LANES = 128
SUBLANES = 8
VMEM_LIMIT = 56 * 1024 * 1024

NEG = -1e30
INT_MIN = -2 ** 31


def _params(*sem):
    return pltpu.CompilerParams(dimension_semantics=sem, vmem_limit_bytes=VMEM_LIMIT)


def _dot(a, b):
    return jnp.dot(a.astype(BF16), b.astype(BF16), preferred_element_type=F32)


def _dot_nt(a, b, precision=None):
    return lax.dot_general(a, b, (((1,), (1,)), ((), ())), precision=precision, preferred_element_type=F32)


def _dot_tn(a, b, precision=None):
    return lax.dot_general(a, b, (((0,), (0,)), ((), ())), precision=precision, preferred_element_type=F32)


def _dot_hi(a, b):
    return jnp.dot(a, b, precision=HI, preferred_element_type=F32)


def _split(a):
    hi = a.astype(BF16)
    return hi, (a - hi.astype(F32)).astype(BF16)


def _split3(a):
    p1 = a.astype(BF16)
    r1 = a - p1.astype(F32)
    p2 = r1.astype(BF16)
    return p1, p2, (r1 - p2.astype(F32)).astype(BF16)


def _dot3(a, b):
    a_hi, a_lo = a
    b_hi, b_lo = b
    d = functools.partial(jnp.dot, preferred_element_type=F32)
    return d(a_hi, b_hi) + (d(a_hi, b_lo) + d(a_lo, b_hi))


def _sigmoid(x):
    return 1.0 / (1.0 + jnp.exp(-x))


def _softplus(x):
    return jnp.maximum(x, 0.0) + jnp.log(1.0 + jnp.exp(-jnp.abs(x)))


def _iota(shape, dim):
    return lax.broadcasted_iota(jnp.int32, shape, dim)


def _proj_kernel(x_ref, w_ref, *out_refs, sizes):
    xb = x_ref[...].astype(BF16)
    off = 0
    for o_ref, size in zip(out_refs, sizes):
        o_ref[...] = jnp.dot(xb, w_ref[:, off:off + size], preferred_element_type=F32)
        off += size


def _proj(x2d, w, sizes, tm):
    n, d = x2d.shape
    return pl.pallas_call(
        functools.partial(_proj_kernel, sizes=sizes),
        grid=(n // tm,),
        in_specs=[pl.BlockSpec((tm, d), lambda i: (i, 0)),
                  pl.BlockSpec((d, sum(sizes)), lambda i: (0, 0))],
        out_specs=[pl.BlockSpec((tm, s), lambda i: (i, 0)) for s in sizes],
        out_shape=[jax.ShapeDtypeStruct((n, s), F32) for s in sizes],
        compiler_params=_params("parallel"),
        name="in_proj",
    )(x2d, w)


def _mm_res_ln_kernel(*refs, nparts, alpha):
    part_refs = refs[:nparts]
    w_refs = refs[nparts:2 * nparts]
    x_ref, g_ref, b_ref, o_ref = refs[2 * nparts:]
    acc = alpha * x_ref[...]
    for p_ref, w_ref in zip(part_refs, w_refs):
        acc = acc + jnp.dot(p_ref[...].astype(BF16), w_ref[...], preferred_element_type=F32)
    mu = jnp.mean(acc, axis=-1, keepdims=True)
    cen = acc - mu
    var = jnp.mean(cen * cen, axis=-1, keepdims=True)
    o_ref[...] = cen * lax.rsqrt(var + 1e-5) * g_ref[...] + b_ref[...]


def _mm_res_ln(parts, ws, x2d, g, b, alpha, tm):
    n, d = x2d.shape
    nparts = len(parts)
    in_specs = ([pl.BlockSpec((tm, p.shape[1]), lambda i: (i, 0)) for p in parts]
                + [pl.BlockSpec(w.shape, lambda i: (0, 0)) for w in ws]
                + [pl.BlockSpec((tm, d), lambda i: (i, 0)),
                   pl.BlockSpec((1, d), lambda i: (0, 0)),
                   pl.BlockSpec((1, d), lambda i: (0, 0))])
    return pl.pallas_call(
        functools.partial(_mm_res_ln_kernel, nparts=nparts, alpha=alpha),
        grid=(n // tm,),
        in_specs=in_specs,
        out_specs=pl.BlockSpec((tm, d), lambda i: (i, 0)),
        out_shape=jax.ShapeDtypeStruct((n, d), F32),
        compiler_params=_params("parallel"),
        name="out_proj_ln",
    )(*parts, *ws, x2d, g.reshape(1, d), b.reshape(1, d))


def _ffn_up_kernel(x_ref, w_ref, cw_ref, hist_ref, act_ref, newhist_ref, ext_ref, *,
                   ns, tt, tiles_per_seq, dff, cc):
    i = pl.program_id(0)
    tm = ns * tt
    hw = CONV_FF - 1
    base = SUBLANES
    if tiles_per_seq == 1:
        ext_ref[:, base - hw:base, :] = hist_ref[...]
    else:
        @pl.when(i % tiles_per_seq == 0)
        def _():
            ext_ref[:, base - hw:base, :] = hist_ref[...]

        @pl.when(i % tiles_per_seq != 0)
        def _():
            ext_ref[:, base - hw:base, :] = ext_ref[:, base + tt - hw:base + tt, :]
    xb = x_ref[...].astype(BF16)
    for j in range(2 * dff // cc):
        cols = slice(j * cc, (j + 1) * cc)
        h = jnp.dot(xb, w_ref[:, cols], preferred_element_type=F32)
        ext_ref[:, base:base + tt, cols] = h.reshape(ns, tt, cc)
    newhist_ref[...] = ext_ref[:, base + tt - hw:base + tt, :]

    def conv(cols):
        acc = None
        for k in range(CONV_FF):
            term = ext_ref[:, base - hw + k:base - hw + k + tt, cols] * cw_ref[k:k + 1, cols]
            acc = term if acc is None else acc + term
        return acc

    for j in range(dff // cc):
        g = conv(slice(j * cc, (j + 1) * cc))
        u = conv(slice(dff + j * cc, dff + (j + 1) * cc))
        act = g * _sigmoid(g) * u
        act_ref[:, j * cc:(j + 1) * cc] = act.reshape(tm, cc).astype(BF16)


def _ffn_up(x2d, w_up, conv_w, hist, ns, tt, tiles_per_seq):
    n, d = x2d.shape
    c2 = w_up.shape[1]
    dff = c2 // 2
    tm = ns * tt
    cc = 256
    hw = CONV_FF - 1
    if tiles_per_seq == 1:
        hist_map = lambda i: (i, 0, 0)
    else:
        hist_map = lambda i: (i // tiles_per_seq, 0, 0)
    return pl.pallas_call(
        functools.partial(_ffn_up_kernel, ns=ns, tt=tt, tiles_per_seq=tiles_per_seq, dff=dff, cc=cc),
        grid=(n // tm,),
        in_specs=[pl.BlockSpec((tm, d), lambda i: (i, 0)),
                  pl.BlockSpec((d, c2), lambda i: (0, 0)),
                  pl.BlockSpec((CONV_FF, c2), lambda i: (0, 0)),
                  pl.BlockSpec((ns, hw, c2), hist_map)],
        out_specs=[pl.BlockSpec((tm, dff), lambda i: (i, 0)),
                   pl.BlockSpec((ns, hw, c2), hist_map)],
        out_shape=[jax.ShapeDtypeStruct((n, dff), BF16),
                   jax.ShapeDtypeStruct(hist.shape, F32)],
        scratch_shapes=[pltpu.VMEM((ns, SUBLANES + tt, c2), F32)],
        compiler_params=_params("arbitrary"),
        name="ffn_up_conv_gate",
    )(x2d, w_up, conv_w, hist)


def _sortable(x):
    b = lax.bitcast_convert_type(x, jnp.int32)
    return b ^ ((b >> 31) & jnp.int32(0x7FFFFFFF))


def _count(mask):
    return jnp.sum(jnp.where(mask, 1.0, 0.0), axis=-1, keepdims=True)


def _dsa_sample_kernel(qa_ref, qi_ref, qm_ref, kf_ref, vf_ref, kif_ref, kn_ref, vn_ref, kin_ref, nbias_ref,
                       fbias_ref, o_ref, self_ref, seln_ref, *, topk, tq, nbs, lf, ln, wi_col):
    start = lf - T5_FAR
    streams = range(nbs)
    rows = [slice(b * tq, (b + 1) * tq) for b in streams]
    dot = functools.partial(jnp.dot, preferred_element_type=F32)
    wi = [qm_ref[rows[b], wi_col:wi_col + N_IDX_HEADS] * (N_IDX_HEADS ** -0.5) * (D_IDX ** -0.5) for b in streams]
    kif = [kif_ref[0, b].astype(BF16) for b in streams]
    kinb = [kin_ref[b].astype(BF16) for b in streams]

    sc_f = [jnp.zeros((tq, lf), F32) for _ in streams]
    sc_n = [jnp.zeros((tq, ln), F32) for _ in streams]
    for n in range(N_IDX_HEADS):
        qn = [qi_ref[rows[b], n * D_IDX:(n + 1) * D_IDX].astype(BF16) for b in streams]
        sc_f = [sc_f[b] + jnp.maximum(dot(qn[b], kif[b]), 0.0) * wi[b][:, n:n + 1] for b in streams]
        sc_n = [sc_n[b] + jnp.maximum(dot(qn[b], kinb[b]), 0.0) * wi[b][:, n:n + 1] for b in streams]
    adm_f = _iota((tq, lf), 1) < start
    adm_n = nbias_ref[0, 0] > 0.5 * NEG
    key_f = [jnp.where(adm_f, _sortable(x), jnp.int32(INT_MIN)) for x in sc_f]
    key_n = [jnp.where(adm_n, _sortable(x), jnp.int32(INT_MIN)) for x in sc_n]

    kf32 = float(topk)

    def body(i, t_us):
        bit = lax.shift_left(jnp.int32(1), 31 - i)
        cand_u = [t | bit for t in t_us]
        cand_s = [c ^ jnp.int32(INT_MIN) for c in cand_u]
        cnt = [_count(key_f[b] >= cand_s[b]) + _count(key_n[b] >= cand_s[b]) for b in streams]
        return tuple(jnp.where(cnt[b] >= kf32, cand_u[b], t_us[b]) for b in streams)

    t_us = lax.fori_loop(0, 32, body, tuple(jnp.zeros((tq, 1), jnp.int32) for _ in streams))
    thr = [t ^ jnp.int32(INT_MIN) for t in t_us]
    need = [kf32 - (_count(key_f[b] > thr[b]) + _count(key_n[b] > thr[b])) for b in streams]
    n_eq = [_count(key_f[b] == thr[b]) + _count(key_n[b] == thr[b]) for b in streams]
    open_row = [t == jnp.int32(INT_MIN) for t in thr]
    for b in streams:
        self_ref[b] = jnp.where(jnp.logical_and(key_f[b] >= thr[b], adm_f), 0.0, NEG)
        seln_ref[b] = jnp.where(jnp.logical_and(key_n[b] >= thr[b], adm_n), 0.0, NEG)

    def resolve_ties(b):
        upper = jnp.where(_iota((LANES, LANES), 0) < _iota((LANES, LANES), 1), 1.0, 0.0).astype(BF16)
        offset = jnp.zeros((tq, 1), F32)
        for ref, key, width in ((self_ref, key_f[b], lf), (seln_ref, key_n[b], ln)):
            for j0 in range(0, width, LANES):
                w = min(LANES, width - j0)
                kb = key[:, j0:j0 + w]
                e = jnp.where(kb == thr[b], 1.0, 0.0)
                rank = offset + dot(e.astype(BF16), upper[:w, :w])
                take = jnp.where(kb > thr[b], 1.0, jnp.where(rank < need[b], e, 0.0))
                take = jnp.where(open_row[b], jnp.where(kb > thr[b], 1.0, 0.0), take)
                ref[b, :, j0:j0 + w] = jnp.where(take > 0.5, 0.0, NEG)
                offset = offset + jnp.sum(e, axis=-1, keepdims=True)

    for b in streams:
        conflict = jnp.logical_and(n_eq[b] != need[b], jnp.logical_not(open_row[b]))
        pl.when(jnp.max(jnp.where(conflict, 1.0, 0.0)) > 0.0)(functools.partial(resolve_ties, b))

    sel_f = [self_ref[b] for b in streams]
    sel_n = [seln_ref[b] for b in streams]
    for h in range(N_HEADS):
        hs = slice(h * HEAD_DIM, (h + 1) * HEAD_DIM)
        qh = [qa_ref[rows[b], hs].astype(BF16) for b in streams]
        s_f = [dot(qh[b], kf_ref[0, b, h].astype(BF16)) * (HEAD_DIM ** -0.5) + fbias_ref[:, h:h + 1] + sel_f[b]
               for b in streams]
        s_n = [dot(qh[b], kn_ref[b, h].astype(BF16)) * (HEAD_DIM ** -0.5) + nbias_ref[0, h] + sel_n[b]
               for b in streams]
        m = [jnp.maximum(jnp.max(x, axis=-1, keepdims=True), jnp.max(y, axis=-1, keepdims=True))
             for x, y in zip(s_f, s_n)]
        p_f = [jnp.exp(x - mm) for x, mm in zip(s_f, m)]
        p_n = [jnp.exp(x - mm) for x, mm in zip(s_n, m)]
        den = [jnp.sum(x, axis=-1, keepdims=True) + jnp.sum(y, axis=-1, keepdims=True) for x, y in zip(p_f, p_n)]
        o = [_dot_nt(p_f[b].astype(BF16), vf_ref[0, b, h].astype(BF16))
             + _dot_nt(p_n[b].astype(BF16), vn_ref[b, h].astype(BF16)) for b in streams]
        for b in streams:
            o_ref[rows[b], hs] = o[b] / den[b]


def _dsa_sample(qa, qi, qmisc, kf, vf, kif, layer, near, nbias, fbias, *, nb, tq, topk, wi_col):
    lf = kf.shape[-1]
    ln = nbias.shape[-1]
    nbs = SAMPLE_STREAMS_PER_STEP if nb % SAMPLE_STREAMS_PER_STEP == 0 else 1
    row = lambda b: (b, 0)
    args = [qa, qi, qmisc, kf, vf, kif, *near]
    in_specs = ([pl.BlockSpec((nbs * tq, a.shape[1]), row) for a in args[:3]]
                + [pl.BlockSpec((1, nbs) + a.shape[2:], lambda b, nd=a.ndim: (layer, b) + (0,) * (nd - 2))
                   for a in args[3:6]]
                + [pl.BlockSpec((nbs,) + a.shape[1:], lambda b, nd=a.ndim: (b,) + (0,) * (nd - 1)) for a in args[6:]]
                + [pl.BlockSpec(nbias.shape, lambda b: (0, 0, 0, 0)),
                   pl.BlockSpec(fbias.shape, lambda b: (0, 0))])
    return pl.pallas_call(
        functools.partial(_dsa_sample_kernel, topk=topk, tq=tq, nbs=nbs, lf=lf, ln=ln, wi_col=wi_col),
        grid=(nb // nbs,),
        in_specs=in_specs,
        out_specs=pl.BlockSpec((nbs * tq, qa.shape[1]), row),
        out_shape=jax.ShapeDtypeStruct(qa.shape, F32),
        scratch_shapes=[pltpu.VMEM((nbs, tq, lf), F32), pltpu.VMEM((nbs, tq, ln), F32)],
        compiler_params=_params("parallel"),
        name="dsa_attention_sample",
    )(*args, nbias, fbias)


def _reduce_rows(x, op, final):
    blk = 8 * SUBLANES
    parts = [x[r0:r0 + blk] for r0 in range(0, x.shape[0], blk)]
    while len(parts) > 1:
        parts = [op(parts[i], parts[i + 1]) for i in range(0, len(parts) - 1, 2)] + parts[len(parts) & ~1:]
    return final(parts[0], axis=0, keepdims=True)


def _block_rows(x, op):
    blk = 8 * SUBLANES
    parts = [x[r0:r0 + blk] for r0 in range(0, x.shape[0], blk)]
    while len(parts) > 1:
        parts = [op(parts[i], parts[i + 1]) for i in range(0, len(parts) - 1, 2)] + parts[len(parts) & ~1:]
    return parts[0]


def _dsa_blocks_kernel(qa_ref, qi_ref, qm_ref, k_ref, v_ref, ki_ref, corr_ref, o_ref,
                       kb_ref, vt_ref, kib_ref, key_ref, sel_ref, thr_ref, stat_ref, m_ref, l_ref, ot_ref, *,
                       topk, tq, seq, wi_col):
    g = pl.program_id(1)
    kb = 2 * LANES
    w = qa_ref.shape[1]
    blk = _row_tile(seq, 4 * LANES)

    @pl.when(g == 0)
    def _():
        for r0 in range(0, seq, blk):
            kb_ref[r0:r0 + blk, :] = k_ref[0, r0:r0 + blk, :].astype(BF16)
            kib_ref[r0:r0 + blk, :] = ki_ref[0, r0:r0 + blk, :].astype(BF16)
        for j in range(seq // LANES):
            vt_ref[j] = v_ref[0, j * LANES:(j + 1) * LANES, :].T.astype(BF16)

    top = (g + 1) * tq
    nkb = (top + kb - 1) // kb
    start_of = lambda i: pl.multiple_of(jnp.maximum(top - kb * (i + 1), 0), LANES)

    qa = (qa_ref[...] * (HEAD_DIM ** -0.5)).astype(BF16)
    qi = qi_ref[...].astype(BF16)
    wi = qm_ref[:, wi_col:wi_col + N_IDX_HEADS] * (N_IDX_HEADS ** -0.5) * (D_IDX ** -0.5)
    eye_h = jnp.where(_iota((N_IDX_HEADS, N_IDX_HEADS), 0) == _iota((N_IDX_HEADS, N_IDX_HEADS), 1), 1.0, 0.0)
    wi_t = _dot_nt(eye_h, wi, HI)
    q_chunk = (g * tq + _iota((1, tq), 1)) // CHUNK

    def score_block(i, carry):
        st = start_of(i)
        kib = kib_ref[pl.ds(st, kb), :][:, :D_IDX]
        acc = jnp.zeros((kb, tq), F32)
        for n in range(N_IDX_HEADS):
            acc = acc + jnp.maximum(_dot_nt(kib, qi[:, n * D_IDX:(n + 1) * D_IDX]), 0.0) * wi_t[n:n + 1, :]
        row = st + _iota((kb, 1), 0)
        ok = jnp.logical_and(row // CHUNK <= q_chunk, row < top - kb * i)
        key_ref[i] = jnp.where(ok, _sortable(acc), jnp.int32(INT_MIN))
        return carry

    lax.fori_loop(0, nkb, score_block, 0)

    kf32 = float(topk)

    def search_blocks(n_blocks):
        def count(pred):
            parts = [_block_rows(jnp.where(pred(key_ref[i]), 1.0, 0.0), jnp.add) for i in range(n_blocks)]
            while len(parts) > 1:
                parts = [a + b for a, b in zip(parts[0::2], parts[1::2])] + parts[len(parts) & ~1:]
            return jnp.sum(parts[0], axis=0, keepdims=True)

        def search(b, t_u):
            cand_u = t_u | lax.shift_left(jnp.int32(1), 31 - b)
            cand_s = cand_u ^ jnp.int32(INT_MIN)
            return jnp.where(count(lambda x: x >= cand_s) >= kf32, cand_u, t_u)

        t_u = lax.fori_loop(0, 32, search, jnp.zeros((1, tq), jnp.int32))
        thr = t_u ^ jnp.int32(INT_MIN)
        thr_ref[0:1, :] = thr
        stat_ref[0:1, :] = kf32 - count(lambda x: x > thr)
        stat_ref[1:2, :] = count(lambda x: x == thr)

    for n_blocks in range(1, seq // kb + 1):
        pl.when(nkb == n_blocks)(functools.partial(search_blocks, n_blocks))

    thr = thr_ref[0:1, :]
    need = stat_ref[0:1, :]
    open_row = thr == jnp.int32(INT_MIN)
    conflict = jnp.logical_and(stat_ref[1:2, :] != need, jnp.logical_not(open_row))
    floor = jnp.maximum(thr, jnp.int32(INT_MIN + 1))

    def select_block(i, carry):
        sel_ref[i] = jnp.where(key_ref[i] >= floor, 0.0, NEG)
        return carry

    lax.fori_loop(0, nkb, select_block, 0)

    @pl.when(jnp.max(jnp.where(conflict, 1.0, 0.0)) > 0.0)
    def _():
        below = jnp.where(_iota((LANES, LANES), 1) < _iota((LANES, LANES), 0), 1.0, 0.0).astype(BF16)

        def tie_block(ii, offset):
            i = nkb - 1 - ii
            for r0 in range(0, kb, LANES):
                kblk = key_ref[i, r0:r0 + LANES, :]
                e = jnp.where(kblk == thr, 1.0, 0.0)
                rank = offset + jnp.dot(below, e.astype(BF16), preferred_element_type=F32)
                take = jnp.where(kblk > thr, 1.0, jnp.where(jnp.logical_or(rank >= need, open_row), 0.0, e))
                sel_ref[i, r0:r0 + LANES, :] = jnp.where(take > 0.5, 0.0, NEG)
                offset = offset + jnp.sum(e, axis=0, keepdims=True)
            return offset

        lax.fori_loop(0, nkb, tie_block, jnp.zeros((1, tq), F32))

    heads = range(N_HEADS)
    hs = [slice(h * HEAD_DIM, (h + 1) * HEAD_DIM) for h in heads]

    ones_rows = jnp.ones((SUBLANES, LANES), BF16)

    def attend_block(i, first, near=None):
        st = start_of(i)
        vblk = st // LANES
        sel = sel_ref[i]
        s = [_dot_nt(kb_ref[pl.ds(st, kb), hs[h]], qa[:, hs[h]]) + sel for h in heads]
        if near is not None:
            s = [s[h] + corr_ref[near, h] for h in heads]
        m_blk = [jnp.max(_block_rows(x, jnp.maximum), axis=0, keepdims=True) for x in s]
        m_old = [m_ref[h:h + 1, :] for h in heads]
        m_new = m_blk if first else [jnp.maximum(a, b) for a, b in zip(m_old, m_blk)]
        pb = [jnp.exp((x - m).astype(BF16)) for x, m in zip(s, m_new)]
        l_blk = [sum(jnp.dot(ones_rows, x[j * LANES:(j + 1) * LANES, :], preferred_element_type=F32)
                     for j in range(kb // LANES))[0:1, :] for x in pb]
        pv = [sum(jnp.dot(vt_ref[vblk + j, hs[h], :], pb[h][j * LANES:(j + 1) * LANES, :],
                          preferred_element_type=F32) for j in range(kb // LANES)) for h in heads]
        for h in heads:
            if first:
                l_ref[h:h + 1, :] = l_blk[h]
                ot_ref[hs[h], :] = pv[h]
            else:
                alpha = jnp.exp(m_old[h] - m_new[h])
                l_ref[h:h + 1, :] = l_ref[h:h + 1, :] * alpha + l_blk[h]
                ot_ref[hs[h], :] = ot_ref[hs[h], :] * alpha + pv[h]
            m_ref[h:h + 1, :] = m_new[h]

    attend_block(0, True, 0)
    pl.when(nkb > 1)(lambda: attend_block(1, False, 1))

    def attend_rest(i, carry):
        attend_block(i, False)
        return carry

    lax.fori_loop(2, nkb, attend_rest, 0)
    for h in heads:
        ot_ref[hs[h], :] = ot_ref[hs[h], :] * (1.0 / l_ref[h:h + 1, :])
    o_ref[...] = ot_ref[...].T


def _dsa_blocks(qa, qi, qmisc, k, v, ki, corr, *, nb, tq, topk, wi_col):
    n, w = qa.shape
    seq = k.shape[1]
    nq = seq // tq
    kb = 2 * LANES
    row = lambda b, g: (b * nq + g, 0)
    per_b = lambda b, g: (b, 0, 0)
    return pl.pallas_call(
        functools.partial(_dsa_blocks_kernel, topk=topk, tq=tq, seq=seq, wi_col=wi_col),
        grid=(nb, nq),
        in_specs=[pl.BlockSpec((tq, w), row),
                  pl.BlockSpec((tq, qi.shape[1]), row),
                  pl.BlockSpec((tq, qmisc.shape[1]), row),
                  pl.BlockSpec((1, seq, w), per_b),
                  pl.BlockSpec((1, seq, w), per_b),
                  pl.BlockSpec((1, seq, ki.shape[2]), per_b),
                  pl.BlockSpec(corr.shape, lambda b, g: (0, 0, 0, 0))],
        out_specs=pl.BlockSpec((tq, w), row),
        out_shape=jax.ShapeDtypeStruct(qa.shape, F32),
        scratch_shapes=[pltpu.VMEM((seq, w), BF16),
                        pltpu.VMEM((seq // LANES, w, LANES), BF16),
                        pltpu.VMEM((seq, ki.shape[2]), BF16),
                        pltpu.VMEM((seq // kb, kb, tq), jnp.int32),
                        pltpu.VMEM((seq // kb, kb, tq), F32),
                        pltpu.VMEM((SUBLANES, tq), jnp.int32),
                        pltpu.VMEM((SUBLANES, tq), F32),
                        pltpu.VMEM((N_HEADS, tq), F32),
                        pltpu.VMEM((N_HEADS, tq), F32),
                        pltpu.VMEM((w, tq), F32)],
        compiler_params=_params("parallel", "arbitrary"),
        name="dsa_attention_prompt",
    )(qa, qi, qmisc, k, v, ki, corr)


def _unit_lower_inverse(mats, n):
    row, col = _iota((n, n), 0), _iota((n, n), 1)
    same = lambda width: row // width == col // width
    dot = functools.partial(jnp.dot, preferred_element_type=F32)
    eye = jnp.where(row == col, 1.0, 0.0)
    diag = [jnp.where(same(INV_BLOCK), a, 0.0) for a in mats]
    ts = [eye - d for d in diag]
    dks = [_split(d) for d in diag]
    k = 1
    while 2 * k < INV_BLOCK:
        dks = [_split(_dot3(dk, dk)) for dk in dks]
        ts = [t + _dot3(_split(t), dk) for t, dk in zip(ts, dks)]
        k *= 2
    width = INV_BLOCK
    while width < n:
        off = jnp.logical_and(same(2 * width), jnp.logical_not(same(width)))
        ls = [jnp.where(off, a, 0.0).astype(BF16) for a in mats]
        tb = [t.astype(BF16) for t in ts]
        tl = [dot(t, l).astype(BF16) for t, l in zip(tb, ls)]
        ts = [t - dot(x, y) for t, x, y in zip(ts, tl, tb)]
        width *= 2
    return ts


def _head_rms(x, w):
    return x * lax.rsqrt(jnp.mean(x * x, axis=-1, keepdims=True) + 1e-6) * w


def _gdn_multi_kernel(qkv_ref, misc_ref, z_ref, hist_ref, s0_ref, cw_ref, alog_ref, dtb_ref, nw_ref,
                      y_ref, sfin_ref, newhist_ref, ext_ref, s_ref, *, c, a_col, b_col):
    j = pl.program_id(1)
    hw = CONV_B - 1
    base = SUBLANES
    wq = N_HEADS * DK_B
    seqs = range(qkv_ref.shape[0])

    @pl.when(j == 0)
    def _():
        ext_ref[:, base - hw:base, :] = hist_ref[...]
        s_ref[...] = s0_ref[...]

    @pl.when(j > 0)
    def _():
        ext_ref[:, base - hw:base, :] = ext_ref[:, base + c - hw:base + c, :]

    ext_ref[:, base:base + c, :] = qkv_ref[...]
    newhist_ref[...] = ext_ref[:, base + c - hw:base + c, :]
    ri = _iota((c, c), 0)
    ci = _iota((c, c), 1)
    lower = ri >= ci
    strict = ri > ci
    tri = jnp.where(lower, 1.0, 0.0)
    eye_h = jnp.where(_iota((N_HEADS, N_HEADS), 0) == _iota((N_HEADS, N_HEADS), 1), 1.0, 0.0)

    def conv_act(b):
        conv = None
        for k in range(CONV_B):
            term = ext_ref[b, base - hw + k:base - hw + k + c, :] * cw_ref[k:k + 1, :]
            conv = term if conv is None else conv + term
        return conv * _sigmoid(conv)

    act = [conv_act(b) for b in seqs]
    beta = [_sigmoid(misc_ref[b, :, b_col:b_col + N_HEADS]) for b in seqs]
    g = [-jnp.exp(alog_ref[...]) * _softplus(misc_ref[b, :, a_col:a_col + N_HEADS] + dtb_ref[...]) for b in seqs]
    gc = [_dot_hi(tri, x) for x in g]
    gc_t = [_dot_nt(eye_h, x, HI) for x in gc]
    eg = [jnp.exp(x) for x in gc]
    e_last = [jnp.exp(x[c - 1:c, :]) for x in gc]
    e_rest = [jnp.exp(x[c - 1:c, :] - x) for x in gc]

    items = [(b, h) for b in seqs for h in range(N_HEADS)]
    col = lambda x, h: x[:, h:h + 1]
    qs = [act[b][:, h * DK_B:(h + 1) * DK_B] for b, h in items]
    ks = [act[b][:, wq + h * DK_B:wq + (h + 1) * DK_B] for b, h in items]
    vs = [act[b][:, 2 * wq + h * DV_B:2 * wq + (h + 1) * DV_B] for b, h in items]
    qs = [q * lax.rsqrt(jnp.sum(q * q, axis=-1, keepdims=True) + 1e-6) * (DK_B ** -0.5) for q in qs]
    ks = [k * lax.rsqrt(jnp.sum(k * k, axis=-1, keepdims=True) + 1e-6) for k in ks]
    kbs = [k.astype(BF16) for k in ks]
    decay = [jnp.where(lower, jnp.exp(jnp.where(lower, col(gc[b], h) - gc_t[b][h:h + 1, :], 0.0)), 0.0)
             for b, h in items]
    kk = [_dot_nt(k, k) for k in kbs]
    attn = [_dot_nt(q.astype(BF16), k) * d for q, k, d in zip(qs, kbs, decay)]
    a_mat = [jnp.where(strict, col(beta[b], h) * x * d, 0.0) for (b, h), x, d in zip(items, kk, decay)]
    t_mat = [t.astype(BF16) for t in _unit_lower_inverse(a_mat, c)]
    value = [_dot(t, v * col(beta[b], h)) for (b, h), t, v in zip(items, t_mat, vs)]
    k_cum = [_dot(t, k * (col(beta[b], h) * col(eg[b], h))) for (b, h), t, k in zip(items, t_mat, ks)]
    s_old = [s_ref[b, h] for b, h in items]
    sbs = [s.astype(BF16) for s in s_old]
    v_new = [(v - _dot(kc, s)).astype(BF16) for v, kc, s in zip(value, k_cum, sbs)]
    o_inter = [_dot(q * col(eg[b], h), s) for (b, h), q, s in zip(items, qs, sbs)]
    o = [oi + _dot(a, v) for oi, a, v in zip(o_inter, attn, v_new)]
    for i, (b, h) in enumerate(items):
        s_ref[b, h] = s_old[i] * col(e_last[b], h) + _dot_tn((ks[i] * col(e_rest[b], h)).astype(BF16), v_new[i])
    for i, (b, h) in enumerate(items):
        zh = z_ref[b, :, h * DV_B:(h + 1) * DV_B]
        y_ref[b, :, h * DV_B:(h + 1) * DV_B] = _head_rms(o[i], nw_ref[...]) * (zh * _sigmoid(zh))
    sfin_ref[...] = s_ref[...]


def _gdn(qkv, misc, z, hist, s0, conv_w, a_log, dt_bias, norm_w, *, nb, c, a_col, b_col):
    n, wqkv = qkv.shape
    seq = n // nb
    nbb = GDN_SEQS_PER_STEP if nb % GDN_SEQS_PER_STEP == 0 else 1
    tok = lambda b, j: (b, j, 0)
    per_b3 = lambda b, j: (b, 0, 0)
    per_b4 = lambda b, j: (b, 0, 0, 0)
    const2 = lambda b, j: (0, 0)
    hw = CONV_B - 1
    tokens = [a.reshape(nb, seq, a.shape[1]) for a in (qkv, misc, z)]
    y, s_fin, new_hist = pl.pallas_call(
        functools.partial(_gdn_multi_kernel, c=c, a_col=a_col, b_col=b_col),
        grid=(nb // nbb, seq // c),
        in_specs=[pl.BlockSpec((nbb, c, a.shape[2]), tok) for a in tokens]
                 + [pl.BlockSpec((nbb, hw, wqkv), per_b3),
                    pl.BlockSpec((nbb,) + s0.shape[1:], per_b4),
                    pl.BlockSpec((CONV_B, wqkv), const2),
                    pl.BlockSpec((1, N_HEADS), const2),
                    pl.BlockSpec((1, N_HEADS), const2),
                    pl.BlockSpec((1, DV_B), const2)],
        out_specs=[pl.BlockSpec((nbb, c, z.shape[1]), tok),
                   pl.BlockSpec((nbb,) + s0.shape[1:], per_b4),
                   pl.BlockSpec((nbb, hw, wqkv), per_b3)],
        out_shape=[jax.ShapeDtypeStruct((nb, seq, z.shape[1]), F32),
                   jax.ShapeDtypeStruct(s0.shape, F32),
                   jax.ShapeDtypeStruct(hist.shape, F32)],
        scratch_shapes=[pltpu.VMEM((nbb, SUBLANES + c, wqkv), F32),
                        pltpu.VMEM((nbb,) + s0.shape[1:], F32)],
        compiler_params=_params("parallel", "arbitrary"),
        name="gated_deltanet",
    )(*tokens, hist, s0, conv_w, a_log.reshape(1, -1), dt_bias.reshape(1, -1), norm_w.reshape(1, -1))
    return y.reshape(n, z.shape[1]), s_fin, new_hist


def _block_mask(rows, row_group, cols, col_group):
    return (np.arange(rows)[:, None] // row_group == np.arange(cols)[None, :] // col_group).astype(np.float32)


def _cummax_rows(x):
    rows = x.shape[0]
    row = _iota(x.shape, 0)
    sh = 1
    while sh < rows:
        x = jnp.maximum(x, jnp.where(row >= sh, pltpu.roll(x, sh, axis=0), -jnp.inf))
        sh *= 2
    return x


def _dot2(a, b):
    hi, lo = _split(a)
    return jnp.dot(hi, b, preferred_element_type=F32) + jnp.dot(lo, b, preferred_element_type=F32)


def _mlstm_dense_kernel(qk_ref, v_ref, og_ref, misc_ref, c0_ref, n0_ref, m0_ref, ib_ref, fb_ref, nw_ref,
                        el_ref, ev_ref, ek_ref, kmask_ref, vmask_ref, cmask_ref, cmaskb_ref, rms_ref, causal_ref,
                        dsel_ref,
                        y_ref, cfin_ref, nfin_ref, mfin_ref, c_ref, n_ref, m_ref, *, l, i_col, f_col):
    j = pl.program_id(1)
    wq = N_HEADS * DK_C
    nbb = qk_ref.shape[0]
    seqs = range(nbb)

    @pl.when(j == 0)
    def _():
        c_ref[...] = jnp.zeros(c_ref.shape, F32)
        for b in seqs:
            for h in range(N_HEADS):
                c_ref[b, h * DK_C:(h + 1) * DK_C, h * DV_C:(h + 1) * DV_C] = c0_ref[b, h]
        n_ref[...] = n0_ref[...]
        m_ref[...] = m0_ref[...]

    each = lambda f, *xs: [f(*args) for args in zip(*xs)]
    dot = functools.partial(jnp.dot, preferred_element_type=F32)
    spread = lambda x, e_ref: sum(dot(p, e_ref[...]) for p in _split3(x))
    tri = jnp.where(_iota((l, l), 0) >= _iota((l, l), 1), 1.0, 0.0).astype(BF16)
    ig = [misc_ref[b, :, i_col:i_col + N_HEADS] + ib_ref[...] for b in seqs]
    lf = [-_softplus(-(misc_ref[b, :, f_col:f_col + N_HEADS] + fb_ref[...])) for b in seqs]
    fc = each(lambda x: sum(dot(tri, p) for p in _split3(x)), lf)
    m_prev = [m_ref[b] for b in seqs]
    log_inter = each(jnp.add, fc, m_prev)
    a = each(jnp.subtract, ig, fc)
    m_t = each(lambda li, f, x: jnp.maximum(li, f + _cummax_rows(x)), log_inter, fc, a)
    w_inter = each(lambda li, m: jnp.exp(li - m), log_inter, m_t)
    w_last = each(lambda f, i, m: jnp.exp(f[l - 1:l, :] - f + i - m[l - 1:l, :]), fc, ig, m_t)

    key_terms = each(lambda x: jnp.sum(spread(x, el_ref) * dsel_ref[...], axis=0, keepdims=True), a)
    log_w = each(lambda f, kt: jnp.where(causal_ref[...] > 0.5, spread(f, el_ref) + kt, -jnp.inf), fc, key_terms)
    q = [qk_ref[b, :, :wq] for b in seqs]
    q_b = [x.astype(BF16) for x in q]
    k_s = [qk_ref[b, :, wq:] * (DK_C ** -0.5) for b in seqs]
    k_b = [x.astype(BF16) for x in k_s]
    v_b = [v_ref[b].astype(BF16) for b in seqs]
    kt_bd = [jnp.concatenate([x] * N_HEADS, axis=0) * kmask_ref[...] for x in k_b]
    v_bd = [jnp.concatenate([x] * N_HEADS, axis=0) * vmask_ref[...] for x in v_b]
    m_l = each(lambda m: spread(m, el_ref), m_t)
    qkw = each(lambda x, kt, lw, m: (_dot_nt(x, kt) * jnp.exp(lw - m)).astype(BF16), q_b, kt_bd, log_w, m_l)
    c_old = [c_ref[b] for b in seqs]
    n_old = [n_ref[b] for b in seqs]
    wi_s = each(lambda w: spread(w, ev_ref), w_inter)
    q_c = each(lambda x, c: dot(x, c.astype(BF16)), q_b, c_old)
    qn = each(lambda x, n: _dot2(x * n, cmaskb_ref[...]), q, n_old)
    num = each(lambda w, vb, wi, qc: dot(w, vb) + wi * qc, qkw, v_bd, wi_s, q_c)
    den = each(lambda w, wi, x: dot(w, vmask_ref[...]) + wi * x, qkw, wi_s, qn)
    m_v = m_l if l == DV_C else each(lambda m: spread(m, ev_ref), m_t)
    hid = each(lambda nu, de, m: nu / jnp.maximum(jnp.abs(de), jnp.exp(-m)), num, den, m_v)
    ms = each(lambda x: _dot2(x * x, rms_ref[...]), hid)
    for b in seqs:
        y_ref[b] = hid[b] * lax.rsqrt(ms[b] + 1e-6) * nw_ref[...] * _sigmoid(og_ref[b])

    kw = each(lambda k, w: k * spread(w, ek_ref), k_s, w_last)
    dec_rows = [jnp.broadcast_to(w[l - 1:l, :], (SUBLANES, N_HEADS)) for w in w_inter]
    upd = each(lambda x, vb: _dot_tn(x.astype(BF16), vb), kw, v_b)
    for b in seqs:
        c_ref[b] = c_old[b] * spread(dec_rows[b], ev_ref)[0:1, :] + upd[b] * cmask_ref[...]
        n_ref[b] = n_old[b] * spread(dec_rows[b], ek_ref)[0:1, :] + jnp.sum(kw[b], axis=0, keepdims=True)
        m_ref[b] = m_t[b][l - 1:l, :]

    @pl.when(j == pl.num_programs(1) - 1)
    def _():
        for b in seqs:
            for h in range(N_HEADS):
                cfin_ref[b, h] = c_ref[b, h * DK_C:(h + 1) * DK_C, h * DV_C:(h + 1) * DV_C]
        nfin_ref[...] = n_ref[...]
        mfin_ref[...] = m_ref[...]


def _mlstm_dense(qk, v, og, misc, c0, n0, m0, i_bias, f_bias, norm_w, *, nb, l, i_col, f_col):
    n = qk.shape[0]
    seq = n // nb
    wq, wv = N_HEADS * DK_C, N_HEADS * DV_C
    hl = N_HEADS * l
    nbb = SEQS_PER_STEP if nb % SEQS_PER_STEP == 0 else 1
    tok = lambda b, j: (b, j, 0)
    per_b3 = lambda b, j: (b, 0, 0)
    per_b4 = lambda b, j: (b, 0, 0, 0)
    const2 = lambda b, j: (0, 0)
    consts = [jnp.asarray(_block_mask(N_HEADS, 1, hl, l), BF16),
              jnp.asarray(_block_mask(N_HEADS, 1, wv, DV_C), BF16),
              jnp.asarray(_block_mask(N_HEADS, 1, wq, DK_C), BF16),
              jnp.asarray(_block_mask(hl, l, wq, DK_C), BF16),
              jnp.asarray(_block_mask(hl, l, wv, DV_C), BF16),
              jnp.asarray(_block_mask(wq, DK_C, wv, DV_C)),
              jnp.asarray(_block_mask(wq, DK_C, wv, DV_C), BF16),
              jnp.asarray(_block_mask(wv, DV_C, wv, DV_C) / DV_C, BF16),
              jnp.asarray(np.tile(np.tril(np.ones((l, l), np.float32)), (1, N_HEADS))),
              jnp.asarray(np.tile(np.eye(l, dtype=np.float32), (1, N_HEADS)))]
    tokens = [a.reshape(nb, seq, a.shape[1]) for a in (qk, v, og, misc)]
    outs = pl.pallas_call(
        functools.partial(_mlstm_dense_kernel, l=l, i_col=i_col, f_col=f_col),
        grid=(nb // nbb, seq // l),
        in_specs=[pl.BlockSpec((nbb, l, a.shape[2]), tok) for a in tokens]
                 + [pl.BlockSpec((nbb,) + c0.shape[1:], per_b4),
                    pl.BlockSpec((nbb, 1, wq), per_b3),
                    pl.BlockSpec((nbb, 1, N_HEADS), per_b3),
                    pl.BlockSpec((1, N_HEADS), const2),
                    pl.BlockSpec((1, N_HEADS), const2),
                    pl.BlockSpec((1, wv), const2)]
                 + [pl.BlockSpec(c.shape, const2) for c in consts],
        out_specs=[pl.BlockSpec((nbb, l, wv), tok),
                   pl.BlockSpec((nbb,) + c0.shape[1:], per_b4),
                   pl.BlockSpec((nbb, 1, wq), per_b3),
                   pl.BlockSpec((nbb, 1, N_HEADS), per_b3)],
        out_shape=[jax.ShapeDtypeStruct((nb, seq, wv), F32),
                   jax.ShapeDtypeStruct(c0.shape, F32),
                   jax.ShapeDtypeStruct((nb, 1, wq), F32),
                   jax.ShapeDtypeStruct((nb, 1, N_HEADS), F32)],
        scratch_shapes=[pltpu.VMEM((nbb, wq, wv), F32),
                        pltpu.VMEM((nbb, 1, wq), F32),
                        pltpu.VMEM((nbb, 1, N_HEADS), F32)],
        compiler_params=_params("parallel", "arbitrary"),
        name="mlstm",
    )(*tokens, c0, n0.reshape(nb, 1, wq), m0.reshape(nb, 1, N_HEADS), i_bias.reshape(1, -1), f_bias.reshape(1, -1),
      jnp.tile(norm_w, N_HEADS).reshape(1, wv), *consts)
    y, c_fin, n_fin, m_fin = outs
    return y.reshape(n, wv), c_fin, n_fin.reshape(nb, N_HEADS, DK_C), m_fin.reshape(nb, N_HEADS)


def _band_prompt_kernel(q_ref, k_ref, v_ref, bias_ref, o_ref, kb_ref, vt_ref, ot_ref, *, tq, seq):
    g = pl.program_id(1)
    w = q_ref.shape[1]
    pad = BAND_CHUNKS * CHUNK
    lw = pad + tq
    npad = pad // LANES
    per_tile = tq // LANES
    blk = _row_tile(seq, 4 * LANES)

    @pl.when(g == 0)
    def _():
        kb_ref[0:pad, :] = jnp.zeros((pad, w), BF16)
        vt_ref[0:npad] = jnp.zeros((npad, w, LANES), BF16)
        for r0 in range(0, seq, blk):
            kb_ref[pad + r0:pad + r0 + blk, :] = k_ref[0, r0:r0 + blk, :].astype(BF16)
        for j in range(seq // LANES):
            vt_ref[npad + j] = v_ref[0, j * LANES:(j + 1) * LANES, :].T.astype(BF16)

    start = pl.multiple_of(g * tq, tq)
    q = (q_ref[...] * (HEAD_DIM ** -0.5)).astype(BF16)
    before_seq = jnp.where(_iota((lw, tq), 0) >= pad - g * tq, 0.0, NEG)
    group = 4
    for h0 in range(0, N_HEADS, group):
        heads = range(h0, h0 + group)
        hs = {h: slice(h * HEAD_DIM, (h + 1) * HEAD_DIM) for h in heads}
        s = [_dot_nt(kb_ref[pl.ds(start, lw), hs[h]], q[:, hs[h]]) + (bias_ref[h] + before_seq) for h in heads]
        p = [jnp.exp(x - _reduce_rows(x, jnp.maximum, jnp.max)) for x in s]
        inv = [1.0 / _reduce_rows(x, jnp.add, jnp.sum) for x in p]
        pb = [x.astype(BF16) for x in p]
        acc = [sum(jnp.dot(vt_ref[g * per_tile + j, hs[h], :], x[j * LANES:(j + 1) * LANES, :],
                           preferred_element_type=F32) for j in range(lw // LANES)) for h, x in zip(heads, pb)]
        for h, a, r in zip(heads, acc, inv):
            ot_ref[hs[h], :] = a * r
    o_ref[...] = ot_ref[...].T


def _band_prompt(q, k, v, bias, *, nb, tq):
    n, w = q.shape
    seq = k.shape[1]
    nq = seq // tq
    pad = BAND_CHUNKS * CHUNK
    row = lambda b, g: (b * nq + g, 0)
    per_b = lambda b, g: (b, 0, 0)
    return pl.pallas_call(
        functools.partial(_band_prompt_kernel, tq=tq, seq=seq),
        grid=(nb, nq),
        in_specs=[pl.BlockSpec((tq, w), row),
                  pl.BlockSpec((1, seq, w), per_b),
                  pl.BlockSpec((1, seq, w), per_b),
                  pl.BlockSpec(bias.shape, lambda b, g: (0, 0, 0))],
        out_specs=pl.BlockSpec((tq, w), row),
        out_shape=jax.ShapeDtypeStruct(q.shape, F32),
        scratch_shapes=[pltpu.VMEM((pad + seq, w), BF16),
                        pltpu.VMEM(((pad + seq) // LANES, w, LANES), BF16),
                        pltpu.VMEM((w, tq), F32)],
        compiler_params=_params("parallel", "arbitrary"),
        name="band_attention_prompt",
    )(q, k, v, bias)


def _band_sample_kernel(q_ref, kc_ref, vc_ref, kn_ref, vn_ref, biasc_ref, biasn_ref, o_ref, *, tq, nbs):
    streams = range(nbs)
    rows = [slice(b * tq, (b + 1) * tq) for b in streams]
    dot = functools.partial(jnp.dot, preferred_element_type=F32)
    for h in range(N_HEADS):
        hs = slice(h * HEAD_DIM, (h + 1) * HEAD_DIM)
        qh = [q_ref[rows[b], hs].astype(BF16) for b in streams]
        s_c = [dot(qh[b], kc_ref[0, b, h].astype(BF16)) * (HEAD_DIM ** -0.5) + biasc_ref[h] for b in streams]
        s_n = [dot(qh[b], kn_ref[b, h].astype(BF16)) * (HEAD_DIM ** -0.5) + biasn_ref[h] for b in streams]
        m = [jnp.maximum(jnp.max(x, axis=-1, keepdims=True), jnp.max(y, axis=-1, keepdims=True))
             for x, y in zip(s_c, s_n)]
        p_c = [jnp.exp(x - mm) for x, mm in zip(s_c, m)]
        p_n = [jnp.exp(x - mm) for x, mm in zip(s_n, m)]
        den = [jnp.sum(x, axis=-1, keepdims=True) + jnp.sum(y, axis=-1, keepdims=True) for x, y in zip(p_c, p_n)]
        o = [_dot_nt(p_c[b].astype(BF16), vc_ref[0, b, h].astype(BF16))
             + _dot_nt(p_n[b].astype(BF16), vn_ref[b, h].astype(BF16)) for b in streams]
        for b in streams:
            o_ref[rows[b], hs] = o[b] / den[b]


def _band_sample(q, kc, vc, layer, kn, vn, bias_c, bias_n, *, nb, tq):
    n, w = q.shape
    nbs = SAMPLE_STREAMS_PER_STEP if nb % SAMPLE_STREAMS_PER_STEP == 0 else 1
    row = lambda b: (b, 0)
    const3 = lambda b: (0, 0, 0)
    return pl.pallas_call(
        functools.partial(_band_sample_kernel, tq=tq, nbs=nbs),
        grid=(nb // nbs,),
        in_specs=[pl.BlockSpec((nbs * tq, w), row)]
                 + [pl.BlockSpec((1, nbs) + a.shape[2:], lambda b: (layer, b, 0, 0, 0)) for a in (kc, vc)]
                 + [pl.BlockSpec((nbs,) + a.shape[1:], lambda b: (b, 0, 0, 0)) for a in (kn, vn)]
                 + [pl.BlockSpec(bias_c.shape, const3), pl.BlockSpec(bias_n.shape, const3)],
        out_specs=pl.BlockSpec((nbs * tq, w), row),
        out_shape=jax.ShapeDtypeStruct(q.shape, F32),
        compiler_params=_params("parallel"),
        name="band_attention_sample",
    )(q, kc, vc, kn, vn, bias_c, bias_n)


def _t5_bucket(rel):
    nb = T5_BUCKETS // 2
    max_exact = nb // 2
    n = jnp.abs(rel)
    n_f = jnp.maximum(n, 1).astype(jnp.float32)
    large = max_exact + (jnp.log(n_f / max_exact) / math.log(T5_MAX_DIST / max_exact) * (nb - max_exact)).astype(jnp.int32)
    large = jnp.minimum(large, nb - 1)
    return jnp.where(rel > 0, nb, 0) + jnp.where(n < max_exact, n, large)


def _toeplitz_bias(fn, n_rows, n_cols):
    n = n_rows + n_cols
    m = np.arange(n)
    f = jnp.transpose(fn(np.where(m < n_cols, m, m - n))).astype(F32)
    flat = jnp.tile(f, (1, n_rows))[:, :n_rows * (n - 1)]
    return flat.reshape(f.shape[0], n_rows, n - 1)[:, :, :n_cols]


def _pack_cols(w, sizes, groups):
    offs = np.concatenate([[0], np.cumsum(sizes)])
    cols, widths = [], []
    for grp in groups:
        width = 0
        for idx in grp:
            cols.append(w[:, offs[idx]:offs[idx + 1]])
            width += sizes[idx]
        pad = (-width) % LANES
        if pad:
            cols.append(jnp.zeros((w.shape[0], pad), w.dtype))
        widths.append(width + pad)
    return jnp.concatenate(cols, axis=1).astype(BF16), tuple(widths)


def _row_tile(n, target):
    t = min(n, target)
    while n % t:
        t //= 2
    return t


def kernel(x_prompt, x_sample, cache_a_k, cache_a_v, cache_a_kidx, state_b_s, state_b_conv, state_c_c, state_c_n, state_c_m, cache_d_k, cache_d_v, state_ffn_conv, w_in_even, w_out_even, t5_bias, b_conv_w, b_a_log, b_dt_bias, b_norm_w, w_in_odd, w_out_odd, c_i_bias, c_f_bias, c_norm_w, d_rel_bias, ln_mix_g, ln_mix_b, ln_ffn_g, ln_ffn_b, ffn_w_up, ffn_conv_w, ffn_w_down):
    bp, sp, d = x_prompt.shape
    bs, ts, _ = x_sample.shape
    depth = ffn_w_up.shape[0]
    past = cache_a_k.shape[2]
    d_win = cache_d_k.shape[2]
    dff = ffn_w_down.shape[1]
    alpha = (2 * depth) ** 0.25
    w_a = N_HEADS * HEAD_DIM
    w_b = N_HEADS * DV_B
    w_c = N_HEADS * DV_C
    qkv_b_w = 2 * N_HEADS * DK_B + w_b
    even_sizes = (w_a, w_a, w_a, N_IDX_HEADS * D_IDX, D_IDX, N_IDX_HEADS, qkv_b_w, N_HEADS, N_HEADS, w_b)
    odd_sizes = (N_HEADS * DK_C, N_HEADS * DK_C, w_c, N_HEADS, N_HEADS, w_c, w_a, w_a, w_a)
    even_groups = ((0,), (1,), (2,), (3,), (6,), (9,), (4, 5, 7, 8))
    wi_col, a_col, b_col = D_IDX, D_IDX + N_IDX_HEADS, D_IDX + N_IDX_HEADS + N_HEADS
    odd_groups = ((0, 1), (2,), (5,), (6,), (7,), (8,), (3, 4))
    i_col, f_col = 0, N_HEADS

    assert sp % CHUNK == 0 and ts <= CHUNK and past % CHUNK == 0 and past >= T5_FAR
    assert (past + ts - 1) // CHUNK == past // CHUNK
    topk_p = min(TOPK_MAX, sp // 4)
    topk_s = min(TOPK_MAX, (past + ts) // 4)
    n_p, n_s = bp * sp, bs * ts
    tm_p = _row_tile(n_p, 512)
    tm_o = _row_tile(n_p, 1024)
    tm_s = _row_tile(n_s, 512)
    tff_p = _row_tile(sp, 512)
    ns_s = _row_tile(bs, max(1, 256 // ts))

    t5 = lambda rel: t5_bias[_t5_bucket(jnp.asarray(rel, jnp.int32))]
    fbias = t5(np.array([-T5_FAR - 1]))
    tq_dsa = 2 * LANES
    assert sp % tq_dsa == 0 and T5_FAR <= tq_dsa
    corr_p = jnp.stack([_toeplitz_bias(lambda dd: t5(-dd - tq_dsa * i) - fbias, tq_dsa, tq_dsa) for i in range(2)])
    tq_a = 2 * CHUNK
    ln_s = T5_FAR + ts
    nbias_s = _toeplitz_bias(lambda dd: t5(dd - T5_FAR), ts, ln_s)[None]

    lw = BAND_CHUNKS * CHUNK + tq_a
    r_chunk = np.arange(lw)[:, None] // CHUNK
    q_chunk = BAND_CHUNKS + np.arange(tq_a)[None, :] // CHUNK
    band_ok = (r_chunk >= q_chunk - BAND_CHUNKS) & (r_chunk <= q_chunk)
    pos_q = past + np.arange(ts)
    pos_kc = past - d_win + np.arange(d_win)
    def band_valid(pos_k):
        kch, qch = pos_k // CHUNK, pos_q // CHUNK
        return (pos_k[None] >= 0) & (kch[None] >= qch[:, None] - BAND_CHUNKS) & (kch[None] <= qch[:, None])
    def band_bias(table, shift, n_rows, n_cols, valid=None):
        bias = _toeplitz_bias(lambda dd: table[np.clip(dd + shift, -REL_CLIP, REL_CLIP) + REL_CLIP], n_rows, n_cols)
        return bias if valid is None else jnp.where(jnp.asarray(valid)[None], bias, NEG)

    cak_t = jnp.transpose(cache_a_k, (0, 1, 3, 4, 2))
    cav_t = jnp.transpose(cache_a_v, (0, 1, 3, 4, 2))
    caki_t = jnp.transpose(cache_a_kidx, (0, 1, 3, 2))
    cdk_t = jnp.transpose(cache_d_k, (0, 1, 3, 4, 2))
    cdv_t = jnp.transpose(cache_d_v, (0, 1, 3, 4, 2))

    xp = x_prompt.reshape(n_p, d)
    xs = x_sample.reshape(n_s, d)
    outs = {k: [] for k in ("ak_p", "ak_s", "av_p", "av_s", "aki_p", "aki_s", "bs_p", "bs_s", "bc_p", "bc_s",
                            "cc_p", "cc_s", "cn_p", "cn_s", "cm_p", "cm_s", "dk_p", "dk_s", "dv_p", "dv_s",
                            "fc_p", "fc_s")}
    for layer in range(depth):
        if layer % 2 == 0:
            e = layer // 2
            w_in, widths = _pack_cols(w_in_even[e], even_sizes, even_groups)
            w_out = w_out_even[e].astype(BF16)
            qa, ka, va, qi, qkv_b, z_b, misc = _proj(xp, w_in, widths, tm_p)
            o_a = _dsa_blocks(qa, qi, misc, ka.reshape(bp, sp, w_a), va.reshape(bp, sp, w_a),
                              misc.reshape(bp, sp, LANES), corr_p, nb=bp, tq=tq_dsa, topk=topk_p, wi_col=wi_col)
            y_b, s_b, h_b = _gdn(qkv_b, misc, z_b, jnp.zeros((bp, CONV_B - 1, qkv_b_w), F32),
                                 jnp.zeros((bp, N_HEADS, DK_B, DV_B), F32), b_conv_w[e], b_a_log[e], b_dt_bias[e],
                                 b_norm_w[e], nb=bp, c=CHUNK, a_col=a_col, b_col=b_col)
            xp = _mm_res_ln([o_a, y_b], [w_out[:w_a], w_out[w_a:]], xp, ln_mix_g[layer], ln_mix_b[layer], alpha, tm_o)
            outs["ak_p"].append(ka.reshape(bp, sp, N_HEADS, HEAD_DIM))
            outs["av_p"].append(va.reshape(bp, sp, N_HEADS, HEAD_DIM))
            outs["aki_p"].append(misc[:, :D_IDX].reshape(bp, sp, D_IDX))
            outs["bs_p"].append(s_b)
            outs["bc_p"].append(h_b)
            qa, ka, va, qi, qkv_b, z_b, misc = _proj(xs, w_in, widths, tm_s)
            ki = misc[:, :D_IDX]
            heads_t = lambda u: jnp.transpose(u.reshape(bs, ts, N_HEADS, HEAD_DIM), (0, 2, 3, 1))
            near = (jnp.concatenate([cak_t[e, ..., past - T5_FAR:], heads_t(ka)], axis=-1),
                    jnp.concatenate([cav_t[e, ..., past - T5_FAR:], heads_t(va)], axis=-1),
                    jnp.concatenate([caki_t[e, ..., past - T5_FAR:],
                                     jnp.transpose(ki.reshape(bs, ts, D_IDX), (0, 2, 1))], axis=-1))
            o_a = _dsa_sample(qa, qi, misc, cak_t, cav_t, caki_t, e, near, nbias_s, fbias, nb=bs, tq=ts,
                              topk=topk_s, wi_col=wi_col)
            y_b, s_b, h_b = _gdn(qkv_b, misc, z_b, state_b_conv[e], state_b_s[e], b_conv_w[e], b_a_log[e],
                                 b_dt_bias[e], b_norm_w[e], nb=bs, c=ts, a_col=a_col, b_col=b_col)
            xs = _mm_res_ln([o_a, y_b], [w_out[:w_a], w_out[w_a:]], xs, ln_mix_g[layer], ln_mix_b[layer], alpha, tm_s)
            outs["ak_s"].append(ka.reshape(bs, ts, N_HEADS, HEAD_DIM))
            outs["av_s"].append(va.reshape(bs, ts, N_HEADS, HEAD_DIM))
            outs["aki_s"].append(ki.reshape(bs, ts, D_IDX))
            outs["bs_s"].append(s_b)
            outs["bc_s"].append(h_b)
        else:
            o = layer // 2
            w_in, widths = _pack_cols(w_in_odd[o], odd_sizes, odd_groups)
            w_out = w_out_odd[o].astype(BF16)
            qk_c, v_c, o_c, q_d, k_d, v_d, misc = _proj(xp, w_in, widths, tm_p)
            y_c, c_c, c_n, c_m = _mlstm_dense(qk_c, v_c, o_c, misc,jnp.zeros((bp, N_HEADS, DK_C, DV_C), F32),
                                        jnp.zeros((bp, N_HEADS, DK_C), F32), jnp.zeros((bp, N_HEADS), F32),
                                        c_i_bias[o], c_f_bias[o], c_norm_w[o], nb=bp, l=CHUNK, i_col=i_col, f_col=f_col)
            k3 = k_d.reshape(bp, sp, w_a)
            v3 = v_d.reshape(bp, sp, w_a)
            bias_bp = _toeplitz_bias(
                lambda dd: d_rel_bias[o][np.clip(-dd - BAND_CHUNKS * CHUNK, -REL_CLIP, REL_CLIP) + REL_CLIP], lw, tq_a)
            bias_bp = jnp.where(jnp.asarray(band_ok)[None], bias_bp, NEG)
            o_d = _band_prompt(q_d, k3, v3, bias_bp, nb=bp, tq=tq_a)
            xp = _mm_res_ln([y_c, o_d], [w_out[:w_c], w_out[w_c:]], xp, ln_mix_g[layer], ln_mix_b[layer], alpha, tm_o)
            d_win_p = min(BAND_CHUNKS * CHUNK, sp)
            outs["cc_p"].append(c_c)
            outs["cn_p"].append(c_n)
            outs["cm_p"].append(c_m)
            outs["dk_p"].append(k3[:, sp - d_win_p:].reshape(bp, d_win_p, N_HEADS, HEAD_DIM))
            outs["dv_p"].append(v3[:, sp - d_win_p:].reshape(bp, d_win_p, N_HEADS, HEAD_DIM))
            qk_c, v_c, o_c, q_d, k_d, v_d, misc = _proj(xs, w_in, widths, tm_s)
            y_c, c_c, c_n, c_m = _mlstm_dense(qk_c, v_c, o_c, misc,state_c_c[o], state_c_n[o], state_c_m[o],
                                        c_i_bias[o], c_f_bias[o], c_norm_w[o], nb=bs, l=ts, i_col=i_col, f_col=f_col)
            heads_t = lambda u: jnp.transpose(u.reshape(bs, ts, N_HEADS, HEAD_DIM), (0, 2, 3, 1))
            o_d = _band_sample(q_d, cdk_t, cdv_t, o, heads_t(k_d), heads_t(v_d),
                               band_bias(d_rel_bias[o], -d_win, ts, d_win, band_valid(pos_kc)),
                               band_bias(d_rel_bias[o], 0, ts, ts, band_valid(pos_q)), nb=bs, tq=ts)
            xs = _mm_res_ln([y_c, o_d], [w_out[:w_c], w_out[w_c:]], xs, ln_mix_g[layer], ln_mix_b[layer], alpha, tm_s)
            outs["cc_s"].append(c_c)
            outs["cn_s"].append(c_n)
            outs["cm_s"].append(c_m)
            outs["dk_s"].append(k_d.reshape(bs, ts, N_HEADS, HEAD_DIM))
            outs["dv_s"].append(v_d.reshape(bs, ts, N_HEADS, HEAD_DIM))
        w_up = ffn_w_up[layer].astype(BF16)
        w_down = ffn_w_down[layer].astype(BF16)
        act, hist_p = _ffn_up(xp, w_up, ffn_conv_w[layer], jnp.zeros((bp, CONV_FF - 1, 2 * dff), F32),
                              1, tff_p, sp // tff_p)
        xp = _mm_res_ln([act], [w_down], xp, ln_ffn_g[layer], ln_ffn_b[layer], alpha, tm_o)
        act, hist_s = _ffn_up(xs, w_up, ffn_conv_w[layer], state_ffn_conv[layer], ns_s, ts, 1)
        xs = _mm_res_ln([act], [w_down], xs, ln_ffn_g[layer], ln_ffn_b[layer], alpha, tm_s)
        outs["fc_p"].append(hist_p)
        outs["fc_s"].append(hist_s)

    st = lambda k: jnp.stack(outs[k])
    return (xp.reshape(bp, sp, d), xs.reshape(bs, ts, d),
            st("ak_p"), st("ak_s"), st("av_p"), st("av_s"), st("aki_p"), st("aki_s"),
            st("bs_p"), st("bs_s"), st("bc_p"), st("bc_s"),
            st("cc_p"), st("cc_s"), st("cn_p"), st("cn_s"), st("cm_p"), st("cm_s"),
            st("dk_p"), st("dk_s"), st("dv_p"), st("dv_s"),
            st("fc_p"), st("fc_s"))
```
